```python
import math
import jax
import jax.numpy as jnp
from jax import lax
import numpy as np

D_MODEL = 1024
BATCH = 4
SEQ = 8192
DEPTH = 1

GRID_W = 64
CTX_LEN = 256
S5_WIDTH = 512
S5_GROUP_CH = 16
S5_GROUPS = S5_WIDTH // S5_GROUP_CH
S5_STATE = 64
DT_MIN = 1e-3
DT_MAX = 1e-1
CONV_WIDTH = 512
CONV_K = 3
N_EXPERT_GROUPS = 4
EXPERTS_PER_GROUP = 8
N_EXPERTS = N_EXPERT_GROUPS * EXPERTS_PER_GROUP
EXPERT_FF = 512
TOP_K_IN_GROUP = 2
EXPERT_BLOCK = 128
IN_PROJ_WIDTH = S5_WIDTH + 3 * CONV_WIDTH + 2 * D_MODEL
ALPHA = (2.0 * DEPTH) ** 0.25
BETA = (8.0 * DEPTH) ** -0.25
LN_EPS = 1e-6
POS_BASE = 10000.0

kernel_name = 'hybrid_s5_shortconv_hmoe_prefix_block'


def _layer_norm(x, gain=None, bias=None):
    xf = x.astype(jnp.float32)
    mu = xf.mean(-1, keepdims=True)
    var = jnp.square(xf - mu).mean(-1, keepdims=True)
    y = (xf - mu) * lax.rsqrt(var + LN_EPS)
    if gain is not None:
        y = y * gain.astype(jnp.float32) + bias.astype(jnp.float32)
    return y.astype(x.dtype)


def _modulate(xn, shift, scale):
    return xn * (1.0 + scale) + shift


def _sincos_2d(rows, cols, dim):
    q = dim // 4
    omega = 1.0 / (POS_BASE ** (jnp.arange(q, dtype=jnp.float32) / q))
    r = jnp.arange(rows, dtype=jnp.float32)[:, None] * omega
    cl = jnp.arange(cols, dtype=jnp.float32)[:, None] * omega
    r_emb = jnp.concatenate([jnp.sin(r), jnp.cos(r)], -1)
    c_emb = jnp.concatenate([jnp.sin(cl), jnp.cos(cl)], -1)
    emb = jnp.concatenate([jnp.broadcast_to(r_emb[:, None, :], (rows, cols, 2 * q)),
                           jnp.broadcast_to(c_emb[None, :, :], (rows, cols, 2 * q))], -1)
    return emb.reshape(rows * cols, dim)


def _ssm_combine(e1, e2):
    a1r, a1i, b1r, b1i = e1
    a2r, a2i, b2r, b2i = e2
    return (a1r * a2r - a1i * a2i, a1r * a2i + a1i * a2r,
            a2r * b1r - a2i * b1i + b2r, a2r * b1i + a2i * b1r + b2i)


def _s5_groups(u):
    b, l, _ = u.shape
    return u.reshape(b, l, S5_GROUPS, S5_GROUP_CH).astype(jnp.float32)


def _s5_dir(lp, d):
    return (lp['s5_log_dt_' + d], lp['s5_a_re_' + d], lp['s5_a_im_' + d],
            lp['s5_b_re_' + d], lp['s5_b_im_' + d])


def _s5_states(ug, log_dt, a_re, a_im, b_re, b_im, s0, reverse):
    f32 = jnp.float32
    dt = jnp.exp(log_dt.astype(f32))[:, None]
    a_re = a_re.astype(f32)
    a_im = a_im.astype(f32)
    mag = jnp.exp(dt * a_re)
    ab_re = mag * jnp.cos(dt * a_im)
    ab_im = mag * jnp.sin(dt * a_im)
    den = a_re * a_re + a_im * a_im
    x_re = ab_re - 1.0
    f_re = (x_re * a_re + ab_im * a_im) / den
    f_im = (ab_im * a_re - x_re * a_im) / den
    b_re = b_re.astype(f32)
    b_im = b_im.astype(f32)
    bb_re = f_re[..., None] * b_re - f_im[..., None] * b_im
    bb_im = f_re[..., None] * b_im + f_im[..., None] * b_re
    bu_re = jnp.einsum('blgc,gnc->blgn', ug, bb_re)
    bu_im = jnp.einsum('blgc,gnc->blgn', ug, bb_im)
    if s0 is not None:
        s0_re, s0_im = s0
        edge = -1 if reverse else 0
        bu_re = bu_re.at[:, edge].add(ab_re * s0_re - ab_im * s0_im)
        bu_im = bu_im.at[:, edge].add(ab_re * s0_im + ab_im * s0_re)
    l = ug.shape[1]
    a_seq_re = jnp.broadcast_to(ab_re, (1, l) + ab_re.shape)
    a_seq_im = jnp.broadcast_to(ab_im, (1, l) + ab_im.shape)
    _, _, s_re, s_im = lax.associative_scan(_ssm_combine, (a_seq_re, a_seq_im, bu_re, bu_im),
                                            reverse=reverse, axis=1)
    return s_re, s_im


def _edge_state(s_re, s_im, reverse):
    e = 0 if reverse else -1
    return (s_re[:, e], s_im[:, e])


def _s5_readout(s_re, s_im, c_re, c_im):
    return (jnp.einsum('blgn,gcn->blgc', s_re, c_re.astype(jnp.float32))
            - jnp.einsum('blgn,gcn->blgc', s_im, c_im.astype(jnp.float32)))


def _short_conv(v, w, row_w):
    b, l, ch = v.shape
    seg = l if row_w is None else row_w
    vr = v.reshape(b * (l // seg), seg, ch)
    vp = jnp.pad(vr, ((0, 0), (1, 1), (0, 0)))
    out = w[0] * vp[:, :-2] + w[1] * vp[:, 1:-1] + w[2] * vp[:, 2:]
    return out.reshape(b, l, ch)


def _token_mixer(h, lp, row_w, s0_f, s0_b):
    b, l, _ = h.shape
    proj = h @ lp['w_in']
    o1 = S5_WIDTH
    o2 = o1 + CONV_WIDTH
    o3 = o2 + CONV_WIDTH
    o4 = o3 + CONV_WIDTH
    o5 = o4 + D_MODEL
    u_a, z_b, gate_b, gate_c = proj[..., :o1], proj[..., o1:o2], proj[..., o2:o3], proj[..., o3:o4]
    merge_a, merge_b = proj[..., o4:o5], proj[..., o5:]
    ug = _s5_groups(u_a)
    sf_re, sf_im = _s5_states(ug, *_s5_dir(lp, 'f'), s0_f, False)
    y_a = _s5_readout(sf_re, sf_im, lp['s5_c_re_f'], lp['s5_c_im_f'])
    fin_f = _edge_state(sf_re, sf_im, False)
    sb_re, sb_im = _s5_states(ug, *_s5_dir(lp, 'b'), s0_b, True)
    y_a = y_a + _s5_readout(sb_re, sb_im, lp['s5_c_re_b'], lp['s5_c_im_b'])
    fin_b = _edge_state(sb_re, sb_im, True)
    y_a = (y_a + lp['s5_d'].astype(jnp.float32) * ug).reshape(b, l, S5_WIDTH).astype(h.dtype)
    ya = jax.nn.gelu(y_a)
    out_a = (ya @ lp['s5_w_glu_val']) * jax.nn.sigmoid(ya @ lp['s5_w_glu_gate'])
    v = _short_conv(gate_c * z_b, lp['conv_w'], row_w)
    out_b = (gate_b * v) @ lp['conv_w_out']
    merged = jax.nn.sigmoid(merge_a) * out_a + jax.nn.sigmoid(merge_b) * out_b
    return merged @ lp['w_o'], fin_f, fin_b


def _routed_experts(hf, e_idx, w_gate, w_up, w_down):
    t, k = e_idx.shape
    d = hf.shape[-1]
    n_exp = w_gate.shape[0]
    n_assign = t * k
    flat_e = e_idx.reshape(n_assign)
    order = jnp.argsort(flat_e)
    sorted_e = flat_e[order]
    counts = jnp.zeros((n_exp,), jnp.int32).at[flat_e].add(1)
    padded = (counts + EXPERT_BLOCK - 1) // EXPERT_BLOCK * EXPERT_BLOCK
    pad_end = jnp.cumsum(padded)
    pad_start = pad_end - padded
    start = jnp.cumsum(counts) - counts
    dest = pad_start[sorted_e] + jnp.arange(n_assign, dtype=jnp.int32) - start[sorted_e]
    n_blocks = (n_assign + EXPERT_BLOCK - 1) // EXPERT_BLOCK + n_exp
    n_rows = n_blocks * EXPERT_BLOCK
    src_tok = jnp.full((n_rows,), t, jnp.int32).at[dest].set(order // k)
    h_pad = jnp.concatenate([hf, jnp.zeros((1, d), hf.dtype)], 0)
    xb = h_pad[src_tok].reshape(n_blocks, EXPERT_BLOCK, d)
    block_e = jnp.minimum(jnp.searchsorted(pad_end, jnp.arange(n_blocks, dtype=jnp.int32) * EXPERT_BLOCK,
                                           side='right'), n_exp - 1)

    def _expert_block(args):
        xblk, e = args
        hid = jax.nn.silu(xblk @ w_gate[e]) * (xblk @ w_up[e])
        return hid @ w_down[e]

    yb = lax.map(_expert_block, (xb, block_e)).reshape(n_rows, d)
    dest_of_assign = jnp.zeros((n_assign,), jnp.int32).at[order].set(dest)
    return yb[dest_of_assign].reshape(t, k, d)


def _hier_moe(h, lp):
    b, l, d = h.shape
    f32 = jnp.float32
    hf = h.reshape(b * l, d)
    g_logits = (hf @ lp['router_w_group']).astype(f32) + lp['router_b_group'].astype(f32)
    g_prob = jax.nn.softmax(g_logits, axis=-1)
    g_idx = jnp.argmax(g_logits, axis=-1).astype(jnp.int32)
    p_group = jnp.take_along_axis(g_prob, g_idx[:, None], axis=-1)
    e_logits = ((hf @ lp['router_w_expert']).astype(f32) + lp['router_b_expert'].astype(f32)
                ).reshape(-1, N_EXPERT_GROUPS, EXPERTS_PER_GROUP)
    e_sel = jnp.take_along_axis(e_logits, g_idx[:, None, None], axis=1)[:, 0]
    e_prob = jax.nn.softmax(e_sel, axis=-1)
    top_p, top_i = lax.top_k(e_prob, TOP_K_IN_GROUP)
    weights = p_group * top_p / top_p.sum(-1, keepdims=True)
    e_idx = g_idx[:, None] * EXPERTS_PER_GROUP + top_i.astype(jnp.int32)
    y = _routed_experts(hf, e_idx, lp['exp_w_gate'], lp['exp_w_up'], lp['exp_w_down'])
    out = jnp.einsum('tkd,tk->td', y.astype(f32), weights)
    return out.astype(h.dtype).reshape(b, l, d)


def setup_inputs(seed: int = 0) -> dict:
    key = jax.random.key(seed)
    ks = iter(jax.random.split(key, 48))
    f32 = jnp.float32

    def nrm(shape, scale):
        return jax.random.normal(next(ks), shape, f32) * scale

    def s5_dir():
        log_dt = jax.random.uniform(next(ks), (DEPTH, S5_GROUPS), f32, math.log(DT_MIN), math.log(DT_MAX))
        a_re = -0.5 + nrm((DEPTH, S5_GROUPS, S5_STATE), 0.01)
        a_im = math.pi * jnp.arange(S5_STATE, dtype=f32) + nrm((DEPTH, S5_GROUPS, S5_STATE), 0.01)
        b_re = nrm((DEPTH, S5_GROUPS, S5_STATE, S5_GROUP_CH), (2 * S5_GROUP_CH) ** -0.5)
        b_im = nrm((DEPTH, S5_GROUPS, S5_STATE, S5_GROUP_CH), (2 * S5_GROUP_CH) ** -0.5)
        c_re = nrm((DEPTH, S5_GROUPS, S5_GROUP_CH, S5_STATE), 0.5)
        c_im = nrm((DEPTH, S5_GROUPS, S5_GROUP_CH, S5_STATE), 0.5)
        return log_dt, a_re, a_im, b_re, b_im, c_re, c_im

    x = nrm((BATCH, SEQ, D_MODEL), 1.0)
    c = nrm((BATCH, D_MODEL), 1.0)
    ctx = nrm((BATCH, CTX_LEN, D_MODEL), 1.0)
    c_ctx = nrm((D_MODEL,), 1.0)
    w_ada = nrm((DEPTH, D_MODEL, 6 * D_MODEL), 0.5 * D_MODEL ** -0.5)
    b_ada = nrm((DEPTH, 6 * D_MODEL), 0.02)
    w_in = nrm((DEPTH, D_MODEL, IN_PROJ_WIDTH), D_MODEL ** -0.5)
    f_par = s5_dir()
    b_par = s5_dir()
    s5_d = nrm((DEPTH, S5_GROUPS, S5_GROUP_CH), 1.0)
    s5_w_glu_val = nrm((DEPTH, S5_WIDTH, D_MODEL), BETA * S5_WIDTH ** -0.5)
    s5_w_glu_gate = nrm((DEPTH, S5_WIDTH, D_MODEL), S5_WIDTH ** -0.5)
    conv_w = nrm((DEPTH, CONV_K, CONV_WIDTH), CONV_K ** -0.5)
    conv_w_out = nrm((DEPTH, CONV_WIDTH, D_MODEL), BETA * CONV_WIDTH ** -0.5)
    w_o = nrm((DEPTH, D_MODEL, D_MODEL), BETA * D_MODEL ** -0.5)
    ln1_g = 1.0 + nrm((DEPTH, D_MODEL), 0.02)
    ln1_b = nrm((DEPTH, D_MODEL), 0.02)
    router_w_group = nrm((DEPTH, D_MODEL, N_EXPERT_GROUPS), D_MODEL ** -0.5)
    router_b_group = nrm((DEPTH, N_EXPERT_GROUPS), 0.01)
    router_w_expert = nrm((DEPTH, D_MODEL, N_EXPERTS), D_MODEL ** -0.5)
    router_b_expert = nrm((DEPTH, N_EXPERTS), 0.01)
    exp_w_gate = nrm((DEPTH, N_EXPERTS, D_MODEL, EXPERT_FF), D_MODEL ** -0.5)
    exp_w_up = nrm((DEPTH, N_EXPERTS, D_MODEL, EXPERT_FF), D_MODEL ** -0.5)
    exp_w_down = nrm((DEPTH, N_EXPERTS, EXPERT_FF, D_MODEL), BETA * EXPERT_FF ** -0.5)
    ln2_g = 1.0 + nrm((DEPTH, D_MODEL), 0.02)
    ln2_b = nrm((DEPTH, D_MODEL), 0.02)
    return {
        'x': x, 'c': c, 'ctx': ctx, 'c_ctx': c_ctx,
        'w_ada': w_ada, 'b_ada': b_ada, 'w_in': w_in,
        's5_log_dt_f': f_par[0], 's5_a_re_f': f_par[1], 's5_a_im_f': f_par[2],
        's5_b_re_f': f_par[3], 's5_b_im_f': f_par[4], 's5_c_re_f': f_par[5], 's5_c_im_f': f_par[6],
        's5_log_dt_b': b_par[0], 's5_a_re_b': b_par[1], 's5_a_im_b': b_par[2],
        's5_b_re_b': b_par[3], 's5_b_im_b': b_par[4], 's5_c_re_b': b_par[5], 's5_c_im_b': b_par[6],
        's5_d': s5_d, 's5_w_glu_val': s5_w_glu_val, 's5_w_glu_gate': s5_w_glu_gate,
        'conv_w': conv_w, 'conv_w_out': conv_w_out, 'w_o': w_o,
        'ln1_g': ln1_g, 'ln1_b': ln1_b,
        'router_w_group': router_w_group, 'router_b_group': router_b_group,
        'router_w_expert': router_w_expert, 'router_b_expert': router_b_expert,
        'exp_w_gate': exp_w_gate, 'exp_w_up': exp_w_up, 'exp_w_down': exp_w_down,
        'ln2_g': ln2_g, 'ln2_b': ln2_b,
    }


def reference(x, c, ctx, c_ctx, w_ada, b_ada, w_in,
              s5_log_dt_f, s5_a_re_f, s5_a_im_f, s5_b_re_f, s5_b_im_f, s5_c_re_f, s5_c_im_f,
              s5_log_dt_b, s5_a_re_b, s5_a_im_b, s5_b_re_b, s5_b_im_b, s5_c_re_b, s5_c_im_b,
              s5_d, s5_w_glu_val, s5_w_glu_gate, conv_w, conv_w_out, w_o, ln1_g, ln1_b,
              router_w_group, router_b_group, router_w_expert, router_b_expert,
              exp_w_gate, exp_w_up, exp_w_down, ln2_g, ln2_b):
    stacked = dict(
        w_ada=w_ada, b_ada=b_ada, w_in=w_in,
        s5_log_dt_f=s5_log_dt_f, s5_a_re_f=s5_a_re_f, s5_a_im_f=s5_a_im_f, s5_b_re_f=s5_b_re_f,
        s5_b_im_f=s5_b_im_f, s5_c_re_f=s5_c_re_f, s5_c_im_f=s5_c_im_f,
        s5_log_dt_b=s5_log_dt_b, s5_a_re_b=s5_a_re_b, s5_a_im_b=s5_a_im_b, s5_b_re_b=s5_b_re_b,
        s5_b_im_b=s5_b_im_b, s5_c_re_b=s5_c_re_b, s5_c_im_b=s5_c_im_b,
        s5_d=s5_d, s5_w_glu_val=s5_w_glu_val, s5_w_glu_gate=s5_w_glu_gate,
        conv_w=conv_w, conv_w_out=conv_w_out, w_o=w_o, ln1_g=ln1_g, ln1_b=ln1_b,
        router_w_group=router_w_group, router_b_group=router_b_group,
        router_w_expert=router_w_expert, router_b_expert=router_b_expert,
        exp_w_gate=exp_w_gate, exp_w_up=exp_w_up, exp_w_down=exp_w_down,
        ln2_g=ln2_g, ln2_b=ln2_b)
    rows = x.shape[1] // GRID_W
    x = x + _sincos_2d(rows, GRID_W, x.shape[-1]).astype(x.dtype)
    for layer in range(DEPTH):
        lp = {name: arr[layer] for name, arr in stacked.items()}
        mod_lat = jax.nn.silu(c) @ lp['w_ada'] + lp['b_ada']
        mod_ctx = jax.nn.silu(c_ctx) @ lp['w_ada'] + lp['b_ada']
        sh1, sc1, g1, sh2, sc2, g2 = jnp.split(mod_lat[:, None, :], 6, axis=-1)
        csh1, csc1, cg1, csh2, csc2, cg2 = jnp.split(mod_ctx, 6, axis=-1)
        h_ctx = _modulate(_layer_norm(ctx), csh1, csc1)
        if layer == DEPTH - 1:
            ug_ctx = _s5_groups(h_ctx @ lp['w_in'][:, :S5_WIDTH])
            s0_f = _edge_state(*_s5_states(ug_ctx, *_s5_dir(lp, 'f'), None, False), False)
            s0_b = _edge_state(*_s5_states(ug_ctx, *_s5_dir(lp, 'b'), None, True), True)
        else:
            mix_ctx, s0_f, s0_b = _token_mixer(h_ctx, lp, None, None, None)
            ctx = _layer_norm(ALPHA * ctx + cg1 * mix_ctx, lp['ln1_g'], lp['ln1_b'])
            h2_ctx = _modulate(_layer_norm(ctx), csh2, csc2)
            ctx = _layer_norm(ALPHA * ctx + cg2 * _hier_moe(h2_ctx, lp), lp['ln2_g'], lp['ln2_b'])
        h = _modulate(_layer_norm(x), sh1, sc1)
        mix, _, _ = _token_mixer(h, lp, GRID_W, s0_f, s0_b)
        x = _layer_norm(ALPHA * x + g1 * mix, lp['ln1_g'], lp['ln1_b'])
        h2 = _modulate(_layer_norm(x), sh2, sc2)
        x = _layer_norm(ALPHA * x + g2 * _hier_moe(h2, lp), lp['ln2_g'], lp['ln2_b'])
    return x
```

```python
import functools
import math

import jax
import jax.numpy as jnp
from jax import lax
from jax.experimental import pallas as pl
from jax.experimental.pallas import tpu as pltpu

F32 = jnp.float32
BF16 = jnp.bfloat16
HI = lax.Precision.HIGHEST

D_MODEL = 1024
GRID_W = 64
S5_WIDTH = 512
S5_GROUP_CH = 16
S5_GROUPS = S5_WIDTH // S5_GROUP_CH
S5_STATE = 64
CONV_WIDTH = 512
N_EXPERT_GROUPS = 4
EXPERTS_PER_GROUP = 8
N_EXPERTS = N_EXPERT_GROUPS * EXPERTS_PER_GROUP
EXPERT_FF = 512
TOP_K = 2
DEPTH = 1
ALPHA = (2.0 * DEPTH) ** 0.25
LN_EPS = 1e-6
POS_BASE = 10000.0

CHUNK = 16
PAIR_W = 2 * CHUNK * S5_GROUP_CH
N_PAIRS = S5_GROUPS // 2
ROUTE_LANES = 128
ROW_BLOCK = 256
VMEM_LIMIT = 56 * 1024 * 1024


def _ln(x):
    mu = jnp.mean(x, axis=-1, keepdims=True)
    xc = x - mu
    var = jnp.mean(xc * xc, axis=-1, keepdims=True)
    return xc * lax.rsqrt(var + LN_EPS)


def _sigmoid(x):
    return 1.0 / (1.0 + jnp.exp(-x))


def _dot(a, b):
    return jnp.dot(a, b, preferred_element_type=F32)


def _mods_kernel(c_ref, w_ref, b_ref, o_ref):
    c = c_ref[...]
    a = c * _sigmoid(c)
    o_ref[...] = jnp.dot(a, w_ref[...], precision=HI, preferred_element_type=F32) + b_ref[...]


def _mods(cc, w_ada, b_ada):
    n = w_ada.shape[1]
    nb = 1536
    return pl.pallas_call(
        _mods_kernel,
        grid=(n // nb,),
        in_specs=[pl.BlockSpec((8, D_MODEL), lambda i: (0, 0)),
                  pl.BlockSpec((D_MODEL, nb), lambda i: (0, i)),
                  pl.BlockSpec((1, nb), lambda i: (0, i))],
        out_specs=pl.BlockSpec((8, nb), lambda i: (0, i)),
        out_shape=jax.ShapeDtypeStruct((8, n), F32),
        compiler_params=pltpu.CompilerParams(vmem_limit_bytes=VMEM_LIMIT),
        name="mods",
    )(cc, w_ada, b_ada.reshape(1, n))


def _in_proj_kernel(x_ref, pos_ref, mod_ref, w_ref, *rest, full):
    if full:
        cw_ref, cwo_ref, u_ref, siga_ref, sb_ref = rest
    else:
        (u_ref,) = rest
    xp = x_ref[...] + pos_ref[...]
    h = (_ln(xp) * mod_ref[1:2, :] + mod_ref[0:1, :]).astype(BF16)
    o1, o2, o3, o4, o5 = 512, 1024, 1536, 2048, 3072
    u_ref[...] = _dot(h, w_ref[:, 0:o1]).astype(u_ref.dtype)
    if not full:
        return
    z_b = _dot(h, w_ref[:, o1:o2])
    gate_c = _dot(h, w_ref[:, o3:o4])
    p = gate_c * z_b
    tm = p.shape[0]
    col = lax.broadcasted_iota(jnp.int32, (tm, 1), 0) % GRID_W
    prev = jnp.where(col == 0, 0.0, pltpu.roll(p, 1, 0))
    nxt = jnp.where(col == GRID_W - 1, 0.0, pltpu.roll(p, tm - 1, 0))
    v = cw_ref[0:1, :] * prev + cw_ref[1:2, :] * p + cw_ref[2:3, :] * nxt
    gate_b = _dot(h, w_ref[:, o2:o3])
    out_b = _dot((gate_b * v).astype(BF16), cwo_ref[...])
    merge_b = _dot(h, w_ref[:, o5:])
    sb_ref[...] = (_sigmoid(merge_b) * out_b).astype(sb_ref.dtype)
    merge_a = _dot(h, w_ref[:, o4:o5])
    siga_ref[...] = _sigmoid(merge_a).astype(siga_ref.dtype)


def _in_proj(x, pos, mod, w_in_bf, conv_w, conv_w_out_bf, tm, full):
    b, l, d = x.shape
    grid = (l // tm, b)
    tok = lambda w: pl.BlockSpec((None, tm, w), lambda i, j: (j, i, 0))
    in_specs = [tok(d),
                pl.BlockSpec((tm, d), lambda i, j: (i, 0)),
                pl.BlockSpec((None, 2, d), lambda i, j: (j, 0, 0))]
    args = [x, pos, mod]
    if full:
        in_specs += [pl.BlockSpec(w_in_bf.shape, lambda i, j: (0, 0)),
                     pl.BlockSpec(conv_w.shape, lambda i, j: (0, 0)),
                     pl.BlockSpec(conv_w_out_bf.shape, lambda i, j: (0, 0))]
        args += [w_in_bf, conv_w, conv_w_out_bf]
        out_specs = [tok(S5_WIDTH), tok(d), tok(d)]
        out_shape = [jax.ShapeDtypeStruct((b, l, S5_WIDTH), BF16),
                     jax.ShapeDtypeStruct((b, l, d), BF16),
                     jax.ShapeDtypeStruct((b, l, d), BF16)]
    else:
        in_specs += [pl.BlockSpec((d, S5_WIDTH), lambda i, j: (0, 0))]
        args += [w_in_bf]
        out_specs = [tok(S5_WIDTH)]
        out_shape = [jax.ShapeDtypeStruct((b, l, S5_WIDTH), BF16)]
    return pl.pallas_call(
        functools.partial(_in_proj_kernel, full=full),
        grid=grid, in_specs=in_specs, out_specs=out_specs, out_shape=out_shape,
        compiler_params=pltpu.CompilerParams(
            dimension_semantics=("arbitrary", "arbitrary"), vmem_limit_bytes=VMEM_LIMIT),
        name="in_proj" if full else "in_proj_ctx",
    )(*args)


def _s5_dir_tables(log_dt, a_re, a_im, b_re, b_im, c_re, c_im):
    f32 = F32
    dt = jnp.exp(log_dt.astype(f32))[:, None]
    a_re = a_re.astype(f32)
    a_im = a_im.astype(f32)
    mag = jnp.exp(dt * a_re)
    ab_re = mag * jnp.cos(dt * a_im)
    ab_im = mag * jnp.sin(dt * a_im)
    den = a_re * a_re + a_im * a_im
    x_re = ab_re - 1.0
    f_re = (x_re * a_re + ab_im * a_im) / den
    f_im = (ab_im * a_re - x_re * a_im) / den
    b_re = b_re.astype(f32)
    b_im = b_im.astype(f32)
    bb_re = f_re[..., None] * b_re - f_im[..., None] * b_im
    bb_im = f_re[..., None] * b_im + f_im[..., None] * b_re
    k = jnp.arange(CHUNK + 1, dtype=f32)[:, None, None]
    pmag = jnp.exp(k * (dt * a_re)[None])
    p_re = pmag * jnp.cos(k * (dt * a_im)[None])
    p_im = pmag * jnp.sin(k * (dt * a_im)[None])
    pb_re = p_re[..., None] * bb_re[None] - p_im[..., None] * bb_im[None]
    pb_im = p_re[..., None] * bb_im[None] + p_im[..., None] * bb_re[None]
    c_re = c_re.astype(f32)
    c_im = c_im.astype(f32)
    kern = (jnp.einsum('gdn,kgnc->kgcd', c_re, pb_re, precision=HI)
            - jnp.einsum('gdn,kgnc->kgcd', c_im, pb_im, precision=HI))
    cp_re = c_re[None] * p_re[:, :, None, :] - c_im[None] * p_im[:, :, None, :]
    cp_im = -(c_re[None] * p_im[:, :, None, :] + c_im[None] * p_re[:, :, None, :])
    return dict(p_re=p_re, p_im=p_im, pb_re=pb_re, pb_im=pb_im, kern=kern, cp_re=cp_re, cp_im=cp_im)


def _s5_operators(f, bk, s5_d):
    q = CHUNK
    g, n, c = S5_GROUPS, S5_STATE, S5_GROUP_CH
    qi = jnp.arange(q)
    lag = qi[None, :] - qi[:, None]
    kf = jnp.where((lag >= 0)[:, :, None, None, None], f['kern'][jnp.clip(lag, 0, q)], 0.0)
    kb = jnp.where((lag <= 0)[:, :, None, None, None], bk['kern'][jnp.clip(-lag, 0, q)], 0.0)
    skip = (jnp.eye(q, dtype=F32)[:, :, None, None, None]
            * (s5_d.astype(F32)[:, :, None] * jnp.eye(c, dtype=F32)[None])[None, None])
    m_intra = (kf + kb + skip).transpose(2, 0, 3, 1, 4).reshape(g, q * c, q * c)
    wf_re = f['pb_re'][q - 1 - qi]
    wf_im = f['pb_im'][q - 1 - qi]
    wb_re = bk['pb_re'][qi]
    wb_im = bk['pb_im'][qi]
    w_st = jnp.stack([wf_re, wf_im, wb_re, wb_im], 0)
    w_st = w_st.transpose(2, 1, 4, 0, 3).reshape(g, q * c, 4, n)
    of_re = f['cp_re'][qi + 1]
    of_im = f['cp_im'][qi + 1]
    ob_re = bk['cp_re'][q - qi]
    ob_im = bk['cp_im'][q - qi]
    w_out = jnp.stack([of_re, of_im, ob_re, ob_im], 0)
    w_out = w_out.transpose(2, 0, 4, 1, 3).reshape(g, 4, n, q * c)
    np_ = N_PAIRS
    eye2 = jnp.eye(2, dtype=F32)
    mi = m_intra.reshape(np_, 2, q * c, q * c)
    mi_pair = (mi[:, :, :, None, :] * eye2[None, :, None, :, None]).reshape(np_, PAIR_W, PAIR_W)
    ws = w_st.reshape(np_, 2, q * c, 4, n)
    ws_pair = (ws[:, :, :, :, None, :] * eye2[None, :, None, None, :, None]).reshape(np_, PAIR_W, 4 * 2 * n)
    wo = w_out.reshape(np_, 2, 4, n, q * c).transpose(0, 2, 1, 3, 4)
    wo_pair = (wo[:, :, :, :, None, :] * eye2[None, None, :, None, :, None]).reshape(np_, 4 * 2 * n, PAIR_W)
    ap = jnp.stack([f['p_re'][q], f['p_im'][q], bk['p_re'][q], bk['p_im'][q]], 0)
    ap = ap.reshape(4, np_, 2 * n).transpose(1, 0, 2)
    ap = jnp.concatenate([ap, jnp.zeros((np_, 4, 2 * n), F32)], 1)
    return mi_pair.astype(BF16), ws_pair.astype(BF16), wo_pair.astype(BF16), ap


def _s5_scan_kernel(uc_ref, mi_ref, ws_ref, wo_ref, ap_ref, s0_ref, y_ref, fin_ref, s_scr):
    rows = uc_ref.shape[0]
    n_tiles = rows // 8
    u = uc_ref[...]
    s_scr[...] = _dot(u, ws_ref[...])
    lo_half = lax.broadcasted_iota(jnp.int32, (8, 128), 0) < 4
    ar_f = jnp.broadcast_to(ap_ref[0:1, :], (8, 128))
    ai_f = jnp.broadcast_to(ap_ref[1:2, :], (8, 128))
    ar_b = jnp.broadcast_to(ap_ref[2:3, :], (8, 128))
    ai_b = jnp.broadcast_to(ap_ref[3:4, :], (8, 128))

    def half_step(r0, col, ar, ai, c_re, c_im, first_half_is_entry):
        rs = pl.ds(r0, 8)
        x_re = s_scr[rs, col:col + 128]
        x_im = s_scr[rs, col + 128:col + 256]
        xr_re = pltpu.roll(x_re, 4, 0)
        xr_im = pltpu.roll(x_im, 4, 0)
        v_re = ar * c_re - ai * c_im + xr_re
        v_im = ar * c_im + ai * c_re + xr_im
        entry = lo_half if first_half_is_entry else jnp.logical_not(lo_half)
        s_scr[rs, col:col + 128] = jnp.where(entry, c_re, v_re)
        s_scr[rs, col + 128:col + 256] = jnp.where(entry, c_im, v_im)
        w_re = ar * v_re - ai * v_im + x_re
        w_im = ar * v_im + ai * v_re + x_im
        n_re = jnp.where(entry, pltpu.roll(w_re, 4, 0), w_re)
        n_im = jnp.where(entry, pltpu.roll(w_im, 4, 0), w_im)
        return n_re, n_im

    def body(m, carry):
        cf_re, cf_im, cb_re, cb_im = carry
        rf = pl.multiple_of(m * 8, 8)
        rb = pl.multiple_of((n_tiles - 1 - m) * 8, 8)
        cf_re, cf_im = half_step(rf, 0, ar_f, ai_f, cf_re, cf_im, True)
        cb_re, cb_im = half_step(rb, 256, ar_b, ai_b, cb_re, cb_im, False)
        return cf_re, cf_im, cb_re, cb_im

    init = (s0_ref[0], s0_ref[1], s0_ref[2], s0_ref[3])
    fin = lax.fori_loop(0, n_tiles, body, init)
    for t in range(4):
        fin_ref[t] = fin[t]
    y = _dot(u, mi_ref[...]) + _dot(s_scr[...].astype(BF16), wo_ref[...])
    y_ref[...] = y.astype(y_ref.dtype)


def _s5_scan(uc, mi, ws, wo, ap, s0):
    rows = uc.shape[0]
    pair = lambda *shape: pl.BlockSpec((None,) + shape, lambda p: (p,) + (0,) * len(shape))
    return pl.pallas_call(
        _s5_scan_kernel,
        grid=(N_PAIRS,),
        in_specs=[pl.BlockSpec((rows, PAIR_W), lambda p: (0, p)),
                  pair(PAIR_W, PAIR_W), pair(PAIR_W, PAIR_W), pair(PAIR_W, PAIR_W),
                  pair(8, 128), pair(4, 8, 128)],
        out_specs=[pl.BlockSpec((rows, PAIR_W), lambda p: (0, p)), pair(4, 8, 128)],
        out_shape=[jax.ShapeDtypeStruct((rows, N_PAIRS * PAIR_W), BF16),
                   jax.ShapeDtypeStruct((N_PAIRS, 4, 8, 128), F32)],
        scratch_shapes=[pltpu.VMEM((rows, PAIR_W), F32)],
        compiler_params=pltpu.CompilerParams(
            dimension_semantics=("arbitrary",), vmem_limit_bytes=VMEM_LIMIT),
        name="s5_scan",
    )(uc, mi, ws, wo, ap, s0)


def _to_chunk_layout(u):
    b, l, _ = u.shape
    j = l // CHUNK
    u = u.reshape(b, j, CHUNK, S5_GROUPS, S5_GROUP_CH).transpose(1, 0, 3, 2, 4)
    return u.reshape(j * b, S5_GROUPS * CHUNK * S5_GROUP_CH)


def _from_chunk_layout(y, b):
    rows = y.shape[0]
    j = rows // b
    y = y.reshape(j, b, S5_GROUPS, CHUNK, S5_GROUP_CH).transpose(1, 0, 3, 2, 4)
    return y.reshape(b, j * CHUNK, S5_WIDTH)


def _mix_out_kernel(y_ref, siga_ref, sb_ref, x_ref, pos_ref, mod_ref, wv_ref, wg_ref, wo_ref,
                    ln1_ref, wrh_ref, wrl_ref, br_ref, x1_ref, h2_ref, route_ref):
    y = y_ref[...].astype(F32)
    ya = (0.5 * y * (1.0 + jnp.tanh(math.sqrt(2.0 / math.pi) * (y + 0.044715 * (y * y * y))))).astype(BF16)
    out_a = _dot(ya, wv_ref[...]) * _sigmoid(_dot(ya, wg_ref[...]))
    merged = siga_ref[...].astype(F32) * out_a + sb_ref[...].astype(F32)
    mix = _dot(merged.astype(BF16), wo_ref[...])
    xp = x_ref[...] + pos_ref[...]
    x1 = _ln(ALPHA * xp + mod_ref[0:1, :] * mix) * ln1_ref[0:1, :] + ln1_ref[1:2, :]
    x1_ref[...] = x1
    h2 = _ln(x1) * mod_ref[2:3, :] + mod_ref[1:2, :]
    h2_ref[...] = h2
    h_hi = h2.astype(BF16)
    h_lo = (h2 - h_hi.astype(F32)).astype(BF16)
    lg = (_dot(h_hi, wrh_ref[...]) + _dot(h_hi, wrl_ref[...]) + _dot(h_lo, wrh_ref[...])) + br_ref[...]
    tm = lg.shape[0]
    lane = lax.broadcasted_iota(jnp.int32, (tm, ROUTE_LANES), 1).astype(F32)
    neg = jnp.float32(-jnp.inf)
    big = jnp.float32(ROUTE_LANES)
    gl = jnp.where(lane < N_EXPERT_GROUPS, lg, neg)
    gmax = jnp.max(gl, axis=-1, keepdims=True)
    g_idx = jnp.min(jnp.where(gl == gmax, lane, big), axis=-1, keepdims=True)
    p_group = 1.0 / jnp.sum(jnp.exp(gl - gmax), axis=-1, keepdims=True)
    e_lo = N_EXPERT_GROUPS + g_idx * EXPERTS_PER_GROUP
    el = jnp.where((lane >= e_lo) & (lane < e_lo + EXPERTS_PER_GROUP), lg, neg)
    m1 = jnp.max(el, axis=-1, keepdims=True)
    i1 = jnp.min(jnp.where(el == m1, lane, big), axis=-1, keepdims=True)
    el2 = jnp.where(lane == i1, neg, el)
    m2 = jnp.max(el2, axis=-1, keepdims=True)
    i2 = jnp.min(jnp.where(el2 == m2, lane, big), axis=-1, keepdims=True)
    r = jnp.exp(m2 - m1)
    w1 = p_group / (1.0 + r)
    w2 = p_group * r / (1.0 + r)
    e1 = i1 - N_EXPERT_GROUPS
    e2 = i2 - N_EXPERT_GROUPS
    route_ref[...] = jnp.where(lane == 0, e1, jnp.where(lane == 1, e2,
                               jnp.where(lane == 2, w1, jnp.where(lane == 3, w2, 0.0))))


def _mix_out(y, siga, sb, x, pos, mod, wv, wg, wo, ln1, wrh, wrl, br, tm):
    b, l, d = x.shape
    tok = lambda w: pl.BlockSpec((None, tm, w), lambda i, j: (j, i, 0))
    whole = lambda a: pl.BlockSpec(a.shape, lambda i, j: (0,) * a.ndim)
    return pl.pallas_call(
        _mix_out_kernel,
        grid=(l // tm, b),
        in_specs=[tok(S5_WIDTH), tok(d), tok(d), tok(d),
                  pl.BlockSpec((tm, d), lambda i, j: (i, 0)),
                  pl.BlockSpec((None, 4, d), lambda i, j: (j, 0, 0)),
                  whole(wv), whole(wg), whole(wo), whole(ln1), whole(wrh), whole(wrl), whole(br)],
        out_specs=[tok(d), tok(d), tok(ROUTE_LANES)],
        out_shape=[jax.ShapeDtypeStruct((b, l, d), F32),
                   jax.ShapeDtypeStruct((b, l, d), F32),
                   jax.ShapeDtypeStruct((b, l, ROUTE_LANES), F32)],
        compiler_params=pltpu.CompilerParams(
            dimension_semantics=("arbitrary", "arbitrary"), vmem_limit_bytes=VMEM_LIMIT),
        name="mix_out",
    )(y, siga, sb, x, pos, mod, wv, wg, wo, ln1, wrh, wrl, br)


def _row_gather_copy(src_hbm, row, dst, dst_row, sem):
    return pltpu.make_async_copy(src_hbm.at[pl.ds(row, 1), :], dst.at[pl.ds(dst_row, 1), :], sem)


def _experts_kernel(be_ref, tok_ref, h_hbm, wg_ref, wu_ref, wd_ref, ys_ref, xs_buf, sem):
    i = pl.program_id(0)
    n = pl.num_programs(0)
    slot = i % 2

    def issue(block, s):
        def body(r, _):
            _row_gather_copy(h_hbm, tok_ref[block * ROW_BLOCK + r], xs_buf.at[s], r, sem.at[s]).start()
            return 0
        lax.fori_loop(0, ROW_BLOCK, body, 0)

    def wait(s):
        pltpu.make_async_copy(h_hbm.at[pl.ds(0, ROW_BLOCK), :], xs_buf.at[s], sem.at[s]).wait()

    @pl.when(i == 0)
    def _():
        issue(0, 0)

    wait(slot)

    @pl.when(i + 1 < n)
    def _():
        issue(i + 1, 1 - slot)

    xb = xs_buf[slot].astype(BF16)
    gate = _dot(xb, wg_ref[...])
    up = _dot(xb, wu_ref[...])
    hid = (gate * _sigmoid(gate) * up).astype(BF16)
    ys_ref[...] = _dot(hid, wd_ref[...])


def _experts(block_e, src_tok, h2, wg, wu, wd, n_blocks):
    d = h2.shape[1]
    grid_spec = pltpu.PrefetchScalarGridSpec(
        num_scalar_prefetch=2,
        grid=(n_blocks,),
        in_specs=[pl.BlockSpec(memory_space=pl.ANY),
                  pl.BlockSpec((None, d, EXPERT_FF), lambda i, be, st: (be[i], 0, 0)),
                  pl.BlockSpec((None, d, EXPERT_FF), lambda i, be, st: (be[i], 0, 0)),
                  pl.BlockSpec((None, EXPERT_FF, d), lambda i, be, st: (be[i], 0, 0))],
        out_specs=pl.BlockSpec((ROW_BLOCK, d), lambda i, be, st: (i, 0)),
        scratch_shapes=[pltpu.VMEM((2, ROW_BLOCK, d), F32), pltpu.SemaphoreType.DMA((2,))],
    )
    return pl.pallas_call(
        _experts_kernel,
        grid_spec=grid_spec,
        out_shape=jax.ShapeDtypeStruct((n_blocks * ROW_BLOCK, d), F32),
        compiler_params=pltpu.CompilerParams(
            dimension_semantics=("arbitrary",), vmem_limit_bytes=VMEM_LIMIT),
        name="experts",
    )(block_e, src_tok, h2, wg, wu, wd)


def _combine_kernel(dest_ref, ys_hbm, x1_ref, route_ref, mod_ref, ln2_ref, o_ref, buf, sem, *, tm):
    i = pl.program_id(0)
    n = pl.num_programs(0)
    slot = i % 2

    def issue(tile, s):
        def body(t, _):
            base = (tile * tm + t) * TOP_K
            _row_gather_copy(ys_hbm, dest_ref[base], buf.at[s], t, sem.at[s]).start()
            _row_gather_copy(ys_hbm, dest_ref[base + 1], buf.at[s], tm + t, sem.at[s]).start()
            return 0
        lax.fori_loop(0, tm, body, 0)

    def wait(s):
        pltpu.make_async_copy(ys_hbm.at[pl.ds(0, TOP_K * tm), :], buf.at[s], sem.at[s]).wait()

    @pl.when(i == 0)
    def _():
        issue(0, 0)

    wait(slot)

    @pl.when(i + 1 < n)
    def _():
        issue(i + 1, 1 - slot)

    w1 = route_ref[:, 2:3]
    w2 = route_ref[:, 3:4]
    moe = buf[slot, 0:tm, :] * w1 + buf[slot, tm:2 * tm, :] * w2
    z = ALPHA * x1_ref[...] + mod_ref[0:1, :] * moe
    o_ref[...] = _ln(z) * ln2_ref[0:1, :] + ln2_ref[1:2, :]


def _combine(dest, ys, x1, route, mod, ln2, tm, tiles_per_batch):
    t, d = x1.shape
    grid_spec = pltpu.PrefetchScalarGridSpec(
        num_scalar_prefetch=1,
        grid=(t // tm,),
        in_specs=[pl.BlockSpec(memory_space=pl.ANY),
                  pl.BlockSpec((tm, d), lambda i, de: (i, 0)),
                  pl.BlockSpec((tm, ROUTE_LANES), lambda i, de: (i, 0)),
                  pl.BlockSpec((None, 8, d), lambda i, de: (i // tiles_per_batch, 0, 0)),
                  pl.BlockSpec((2, d), lambda i, de: (0, 0))],
        out_specs=pl.BlockSpec((tm, d), lambda i, de: (i, 0)),
        scratch_shapes=[pltpu.VMEM((2, TOP_K * tm, d), F32), pltpu.SemaphoreType.DMA((2,))],
    )
    return pl.pallas_call(
        functools.partial(_combine_kernel, tm=tm),
        grid_spec=grid_spec,
        out_shape=jax.ShapeDtypeStruct((t, d), F32),
        compiler_params=pltpu.CompilerParams(
            dimension_semantics=("arbitrary",), vmem_limit_bytes=VMEM_LIMIT),
        name="combine",
    )(dest, ys, x1, route, mod, ln2)


def _sincos_2d(rows, cols, dim):
    q = dim // 4
    omega = 1.0 / (POS_BASE ** (jnp.arange(q, dtype=F32) / q))
    r = jnp.arange(rows, dtype=F32)[:, None] * omega
    cl = jnp.arange(cols, dtype=F32)[:, None] * omega
    r_emb = jnp.concatenate([jnp.sin(r), jnp.cos(r)], -1)
    c_emb = jnp.concatenate([jnp.sin(cl), jnp.cos(cl)], -1)
    emb = jnp.concatenate([jnp.broadcast_to(r_emb[:, None, :], (rows, cols, 2 * q)),
                           jnp.broadcast_to(c_emb[None, :, :], (rows, cols, 2 * q))], -1)
    return emb.reshape(rows * cols, dim)


def _routing_tables(e_idx, n_blocks):
    t, k = e_idx.shape
    n_assign = t * k
    flat_e = e_idx.reshape(n_assign)
    order = jnp.argsort(flat_e)
    sorted_e = flat_e[order]
    counts = jnp.zeros((N_EXPERTS,), jnp.int32).at[flat_e].add(1)
    padded = (counts + ROW_BLOCK - 1) // ROW_BLOCK * ROW_BLOCK
    pad_end = jnp.cumsum(padded)
    pad_start = pad_end - padded
    start = jnp.cumsum(counts) - counts
    dest = pad_start[sorted_e] + jnp.arange(n_assign, dtype=jnp.int32) - start[sorted_e]
    n_rows = n_blocks * ROW_BLOCK
    src_tok = jnp.zeros((n_rows,), jnp.int32).at[dest].set((order // k).astype(jnp.int32))
    block_e = jnp.minimum(
        jnp.searchsorted(pad_end, jnp.arange(n_blocks, dtype=jnp.int32) * ROW_BLOCK, side='right'),
        N_EXPERTS - 1).astype(jnp.int32)
    dest_of_assign = jnp.zeros((n_assign,), jnp.int32).at[order].set(dest.astype(jnp.int32))
    return block_e, src_tok, dest_of_assign


def kernel(x, c, ctx, c_ctx, w_ada, b_ada, w_in, s5_log_dt_f, s5_a_re_f, s5_a_im_f, s5_b_re_f, s5_b_im_f, s5_c_re_f, s5_c_im_f, s5_log_dt_b, s5_a_re_b, s5_a_im_b, s5_b_re_b, s5_b_im_b, s5_c_re_b, s5_c_im_b, s5_d, s5_w_glu_val, s5_w_glu_gate, conv_w, conv_w_out, w_o, ln1_g, ln1_b, router_w_group, router_b_group, router_w_expert, router_b_expert, exp_w_gate, exp_w_up, exp_w_down, ln2_g, ln2_b):
    b, l, d = x.shape
    lc = ctx.shape[1]
    assert d == D_MODEL and b == 4 and w_ada.shape[0] == DEPTH
    assert l % (2 * CHUNK) == 0 and lc % (2 * CHUNK) == 0 and l % GRID_W == 0
    t = b * l
    tm = min(512, l)
    tmc = min(512, lc)

    cc = jnp.concatenate([c, c_ctx[None, :], jnp.zeros((8 - b - 1, d), F32)], 0)
    mods = _mods(cc, w_ada[0], b_ada[0])
    sh1, sc1, g1, sh2, sc2, g2 = jnp.split(mods, 6, axis=-1)
    mod_a = jnp.stack([sh1[:b], 1.0 + sc1[:b]], 1)
    mod_ctx = jnp.broadcast_to(jnp.stack([sh1[b], 1.0 + sc1[b]], 0)[None], (b, 2, d))
    mod_c = jnp.stack([g1[:b], sh2[:b], 1.0 + sc2[:b], jnp.zeros((b, d), F32)], 1)
    mod_f = jnp.concatenate([g2[:b, None, :], jnp.zeros((b, 7, d), F32)], 1)

    w_in_bf = w_in[0].astype(BF16)
    f_tab = _s5_dir_tables(s5_log_dt_f[0], s5_a_re_f[0], s5_a_im_f[0], s5_b_re_f[0], s5_b_im_f[0],
                           s5_c_re_f[0], s5_c_im_f[0])
    b_tab = _s5_dir_tables(s5_log_dt_b[0], s5_a_re_b[0], s5_a_im_b[0], s5_b_re_b[0], s5_b_im_b[0],
                           s5_c_re_b[0], s5_c_im_b[0])
    mi, ws, wo_s5, ap = _s5_operators(f_tab, b_tab, s5_d[0])

    (u_ctx,) = _in_proj(ctx, jnp.zeros((lc, d), F32), mod_ctx, w_in_bf, None, None, tmc, False)
    zero_state = jnp.zeros((N_PAIRS, 4, 8, 128), F32)
    _, s0 = _s5_scan(_to_chunk_layout(u_ctx), mi, ws, wo_s5, ap, zero_state)

    pos = _sincos_2d(l // GRID_W, GRID_W, d)
    u, siga, sb = _in_proj(x, pos, mod_a, w_in_bf, conv_w[0], conv_w_out[0].astype(BF16), tm, True)
    yc, _ = _s5_scan(_to_chunk_layout(u), mi, ws, wo_s5, ap, s0)
    y = _from_chunk_layout(yc, b)

    wr = jnp.concatenate([router_w_group[0], router_w_expert[0],
                          jnp.zeros((d, ROUTE_LANES - N_EXPERT_GROUPS - N_EXPERTS), F32)], 1)
    wr_hi = wr.astype(BF16)
    wr_lo = (wr - wr_hi.astype(F32)).astype(BF16)
    br = jnp.concatenate([router_b_group[0], router_b_expert[0],
                          jnp.zeros((ROUTE_LANES - N_EXPERT_GROUPS - N_EXPERTS,), F32)])[None, :]
    ln1 = jnp.stack([ln1_g[0], ln1_b[0]], 0)
    x1, h2, route = _mix_out(y, siga, sb, x, pos, mod_c,
                             s5_w_glu_val[0].astype(BF16), s5_w_glu_gate[0].astype(BF16),
                             w_o[0].astype(BF16), ln1, wr_hi, wr_lo, br, tm)

    x1 = x1.reshape(t, d)
    h2 = h2.reshape(t, d)
    route = route.reshape(t, ROUTE_LANES)
    e_idx = route[:, 0:TOP_K].astype(jnp.int32)
    n_blocks = (t * TOP_K) // ROW_BLOCK + N_EXPERTS
    block_e, src_tok, dest = _routing_tables(e_idx, n_blocks)
    ys = _experts(block_e, src_tok, h2, exp_w_gate[0].astype(BF16), exp_w_up[0].astype(BF16),
                  exp_w_down[0].astype(BF16), n_blocks)
    tmf = min(256, l)
    ln2 = jnp.stack([ln2_g[0], ln2_b[0]], 0)
    out = _combine(dest, ys, x1, route, mod_f, ln2, tmf, l // tmf)
    return out.reshape(b, l, d)
```

```python
import functools
import math

import jax
import jax.numpy as jnp
from jax import lax
from jax.experimental import pallas as pl
from jax.experimental.pallas import tpu as pltpu

F32 = jnp.float32
BF16 = jnp.bfloat16
HI = lax.Precision.HIGHEST

D_MODEL = 1024
GRID_W = 64
S5_WIDTH = 512
S5_GROUP_CH = 16
S5_GROUPS = S5_WIDTH // S5_GROUP_CH
S5_STATE = 64
CONV_WIDTH = 512
N_EXPERT_GROUPS = 4
EXPERTS_PER_GROUP = 8
N_EXPERTS = N_EXPERT_GROUPS * EXPERTS_PER_GROUP
EXPERT_FF = 512
TOP_K = 2
DEPTH = 1
ALPHA = (2.0 * DEPTH) ** 0.25
LN_EPS = 1e-6
POS_BASE = 10000.0

LANES = 128
SUBLANES = 8
CHUNK = 16
GROUP_W = CHUNK * S5_GROUP_CH
PAIR_W = 2 * GROUP_W
N_PAIRS = S5_GROUPS // 2
TOK_PER_VREG = LANES // S5_GROUP_CH
TAB_ROWS = 24
ROUTE_LANES = 128
ROW_BLOCK = 256
VMEM_LIMIT = 56 * 1024 * 1024


def _ln(x):
    mu = jnp.mean(x, axis=-1, keepdims=True)
    xc = x - mu
    var = jnp.mean(xc * xc, axis=-1, keepdims=True)
    return xc * lax.rsqrt(var + LN_EPS)


def _sigmoid(x):
    return 1.0 / (1.0 + jnp.exp(-x))


def _dot(a, b):
    return jnp.dot(a, b, preferred_element_type=F32)


def _mods_kernel(c_ref, w_ref, b_ref, o_ref):
    c = c_ref[...]
    a = c * _sigmoid(c)
    o_ref[...] = jnp.dot(a, w_ref[...], precision=HI, preferred_element_type=F32) + b_ref[...]


def _mods(cc, w_ada, b_ada):
    n = w_ada.shape[1]
    nb = 1536
    return pl.pallas_call(
        _mods_kernel,
        grid=(n // nb,),
        in_specs=[pl.BlockSpec((8, D_MODEL), lambda i: (0, 0)),
                  pl.BlockSpec((D_MODEL, nb), lambda i: (0, i)),
                  pl.BlockSpec((1, nb), lambda i: (0, i))],
        out_specs=pl.BlockSpec((8, nb), lambda i: (0, i)),
        out_shape=jax.ShapeDtypeStruct((8, n), F32),
        compiler_params=pltpu.CompilerParams(vmem_limit_bytes=VMEM_LIMIT),
        name="mods",
    )(cc, w_ada, b_ada.reshape(1, n))


def _slot_masks(rows):
    slot = lax.broadcasted_iota(jnp.int32, (rows, LANES), 1) // S5_GROUP_CH
    return [slot == s for s in range(TOK_PER_VREG)]


def _to_chunk_tile(u_scr, uc_ref):
    nch = uc_ref.shape[0]
    masks = _slot_masks(nch)
    for qh in range(CHUNK // TOK_PER_VREG):
        for v in range(S5_WIDTH // LANES):
            src = [u_scr[v, pl.ds(qh * TOK_PER_VREG + s, nch, stride=CHUNK), :] for s in range(TOK_PER_VREG)]
            for i in range(TOK_PER_VREG):
                acc = None
                for s in range(TOK_PER_VREG):
                    shift = ((s - i) * S5_GROUP_CH) % LANES
                    piece = pltpu.roll(src[s], shift, 1) if shift else src[s]
                    acc = piece if acc is None else jnp.where(masks[s], piece, acc)
                lo = (v * TOK_PER_VREG + i) * GROUP_W + qh * LANES
                uc_ref[:, lo:lo + LANES] = acc.astype(uc_ref.dtype)


def _from_chunk_tile(yc_ref, y_scr):
    nch = yc_ref.shape[0]
    masks = _slot_masks(nch)
    for qh in range(CHUNK // TOK_PER_VREG):
        for v in range(S5_WIDTH // LANES):
            src = []
            for i in range(TOK_PER_VREG):
                lo = (v * TOK_PER_VREG + i) * GROUP_W + qh * LANES
                src.append(yc_ref[:, lo:lo + LANES].astype(F32))
            for s in range(TOK_PER_VREG):
                acc = None
                for i in range(TOK_PER_VREG):
                    shift = ((i - s) * S5_GROUP_CH) % LANES
                    piece = pltpu.roll(src[i], shift, 1) if shift else src[i]
                    acc = piece if acc is None else jnp.where(masks[i], piece, acc)
                y_scr[v, pl.ds(qh * TOK_PER_VREG + s, nch, stride=CHUNK), :] = acc


def _in_proj_kernel(x_ref, pos_ref, mod_ref, w_ref, *rest, full):
    if full:
        cw_ref, cwo_ref, uc_ref, siga_ref, sb_ref, u_scr = rest
    else:
        uc_ref, u_scr = rest
    xp = x_ref[...] + pos_ref[...]
    h = (_ln(xp) * mod_ref[1:2, :] + mod_ref[0:1, :]).astype(BF16)
    o1, o2, o3, o4, o5 = 512, 1024, 1536, 2048, 3072
    u = _dot(h, w_ref[:, 0:o1])
    for v in range(S5_WIDTH // LANES):
        u_scr[v] = u[:, v * LANES:(v + 1) * LANES]
    _to_chunk_tile(u_scr, uc_ref)
    if not full:
        return
    z_b = _dot(h, w_ref[:, o1:o2])
    gate_c = _dot(h, w_ref[:, o3:o4])
    p = gate_c * z_b
    tm = p.shape[0]
    col = lax.broadcasted_iota(jnp.int32, (tm, 1), 0) % GRID_W
    prev = jnp.where(col == 0, 0.0, pltpu.roll(p, 1, 0))
    nxt = jnp.where(col == GRID_W - 1, 0.0, pltpu.roll(p, tm - 1, 0))
    v = cw_ref[0:1, :] * prev + cw_ref[1:2, :] * p + cw_ref[2:3, :] * nxt
    gate_b = _dot(h, w_ref[:, o2:o3])
    out_b = _dot((gate_b * v).astype(BF16), cwo_ref[...])
    merge_b = _dot(h, w_ref[:, o5:])
    sb_ref[...] = (_sigmoid(merge_b) * out_b).astype(sb_ref.dtype)
    merge_a = _dot(h, w_ref[:, o4:o5])
    siga_ref[...] = _sigmoid(merge_a).astype(siga_ref.dtype)


def _in_proj(x, pos, mod, w_in_bf, conv_w, conv_w_out_bf, tm, full):
    b, l, d = x.shape
    n_tiles = l // tm
    grid = (n_tiles, b)
    tok = lambda w: pl.BlockSpec((None, tm, w), lambda i, j: (j, i, 0))
    chunk_spec = pl.BlockSpec((tm // CHUNK, CHUNK * S5_WIDTH), lambda i, j: (j * n_tiles + i, 0))
    chunk_shape = jax.ShapeDtypeStruct((b * l // CHUNK, CHUNK * S5_WIDTH), BF16)
    in_specs = [tok(d),
                pl.BlockSpec((tm, d), lambda i, j: (i, 0)),
                pl.BlockSpec((None, 2, d), lambda i, j: (j, 0, 0))]
    args = [x, pos, mod]
    if full:
        in_specs += [pl.BlockSpec(w_in_bf.shape, lambda i, j: (0, 0)),
                     pl.BlockSpec(conv_w.shape, lambda i, j: (0, 0)),
                     pl.BlockSpec(conv_w_out_bf.shape, lambda i, j: (0, 0))]
        args += [w_in_bf, conv_w, conv_w_out_bf]
        out_specs = [chunk_spec, tok(d), tok(d)]
        out_shape = [chunk_shape,
                     jax.ShapeDtypeStruct((b, l, d), BF16),
                     jax.ShapeDtypeStruct((b, l, d), BF16)]
    else:
        in_specs += [pl.BlockSpec((d, S5_WIDTH), lambda i, j: (0, 0))]
        args += [w_in_bf]
        out_specs = [chunk_spec]
        out_shape = [chunk_shape]
    return pl.pallas_call(
        functools.partial(_in_proj_kernel, full=full),
        grid=grid, in_specs=in_specs, out_specs=out_specs, out_shape=out_shape,
        scratch_shapes=[pltpu.VMEM((S5_WIDTH // LANES, tm, LANES), F32)],
        compiler_params=pltpu.CompilerParams(
            dimension_semantics=("arbitrary", "arbitrary"), vmem_limit_bytes=VMEM_LIMIT),
        name="in_proj" if full else "in_proj_ctx",
    )(*args)


def _s5_dir_tables(log_dt, a_re, a_im, b_re, b_im, c_re, c_im):
    f32 = F32
    dt = jnp.exp(log_dt.astype(f32))[:, None]
    a_re = a_re.astype(f32)
    a_im = a_im.astype(f32)
    mag = jnp.exp(dt * a_re)
    ab_re = mag * jnp.cos(dt * a_im)
    ab_im = mag * jnp.sin(dt * a_im)
    den = a_re * a_re + a_im * a_im
    x_re = ab_re - 1.0
    f_re = (x_re * a_re + ab_im * a_im) / den
    f_im = (ab_im * a_re - x_re * a_im) / den
    b_re = b_re.astype(f32)
    b_im = b_im.astype(f32)
    bb_re = f_re[..., None] * b_re - f_im[..., None] * b_im
    bb_im = f_re[..., None] * b_im + f_im[..., None] * b_re
    k = jnp.arange(CHUNK + 1, dtype=f32)[:, None, None]
    pmag = jnp.exp(k * (dt * a_re)[None])
    p_re = pmag * jnp.cos(k * (dt * a_im)[None])
    p_im = pmag * jnp.sin(k * (dt * a_im)[None])
    pb_re = p_re[..., None] * bb_re[None] - p_im[..., None] * bb_im[None]
    pb_im = p_re[..., None] * bb_im[None] + p_im[..., None] * bb_re[None]
    c_re = c_re.astype(f32)
    c_im = c_im.astype(f32)
    kern = (jnp.einsum('gdn,kgnc->kgcd', c_re, pb_re, precision=HI)
            - jnp.einsum('gdn,kgnc->kgcd', c_im, pb_im, precision=HI))
    cp_re = c_re[None] * p_re[:, :, None, :] - c_im[None] * p_im[:, :, None, :]
    cp_im = -(c_re[None] * p_im[:, :, None, :] + c_im[None] * p_re[:, :, None, :])
    return dict(p_re=p_re, p_im=p_im, pb_re=pb_re, pb_im=pb_im, kern=kern, cp_re=cp_re, cp_im=cp_im)


def _s5_operators(f, bk, s5_d):
    q = CHUNK
    g, n, c = S5_GROUPS, S5_STATE, S5_GROUP_CH
    qi = jnp.arange(q)
    lag = qi[None, :] - qi[:, None]
    kf = jnp.where((lag >= 0)[:, :, None, None, None], f['kern'][jnp.clip(lag, 0, q)], 0.0)
    kb = jnp.where((lag <= 0)[:, :, None, None, None], bk['kern'][jnp.clip(-lag, 0, q)], 0.0)
    skip = (jnp.eye(q, dtype=F32)[:, :, None, None, None]
            * (s5_d.astype(F32)[:, :, None] * jnp.eye(c, dtype=F32)[None])[None, None])
    m_intra = (kf + kb + skip).transpose(2, 0, 3, 1, 4).reshape(g, q * c, q * c)
    wf_re = f['pb_re'][q - 1 - qi]
    wf_im = f['pb_im'][q - 1 - qi]
    wb_re = bk['pb_re'][qi]
    wb_im = bk['pb_im'][qi]
    w_st = jnp.stack([wf_re, wf_im, wb_re, wb_im], 0)
    w_st = w_st.transpose(2, 1, 4, 0, 3).reshape(g, q * c, 4, n)
    of_re = f['cp_re'][qi + 1]
    of_im = f['cp_im'][qi + 1]
    ob_re = bk['cp_re'][q - qi]
    ob_im = bk['cp_im'][q - qi]
    w_out = jnp.stack([of_re, of_im, ob_re, ob_im], 0)
    w_out = w_out.transpose(2, 0, 4, 1, 3).reshape(g, 4, n, q * c)
    np_ = N_PAIRS
    eye2 = jnp.eye(2, dtype=F32)
    mi = m_intra.reshape(np_, 2, q * c, q * c)
    mi_pair = (mi[:, :, :, None, :] * eye2[None, :, None, :, None]).reshape(np_, PAIR_W, PAIR_W)
    ws = w_st.reshape(np_, 2, q * c, 4, n)
    ws_pair = (ws[:, :, :, :, None, :] * eye2[None, :, None, None, :, None]).reshape(np_, PAIR_W, 4 * 2 * n)
    wo = w_out.reshape(np_, 2, 4, n, q * c).transpose(0, 2, 1, 3, 4)
    wo_pair = (wo[:, :, :, :, None, :] * eye2[None, None, :, None, :, None]).reshape(np_, 4 * 2 * n, PAIR_W)
    tab = jnp.concatenate([_chunk_power_table(f, False), _chunk_power_table(bk, True)], 0)
    tab = tab.reshape(2 * TAB_ROWS, np_, 2 * n).transpose(1, 0, 2)
    return mi_pair.astype(BF16), ws_pair.astype(BF16), wo_pair.astype(BF16), tab


def _chunk_power_table(t, backward):
    def cmul(x, y):
        return x[0] * y[0] - x[1] * y[1], x[0] * y[1] + x[1] * y[0]
    p1 = (t['p_re'][CHUNK], t['p_im'][CHUNK])
    p2 = cmul(p1, p1)
    p4 = cmul(p2, p2)
    p8 = cmul(p4, p4)
    pr = [(jnp.ones_like(p1[0]), jnp.zeros_like(p1[0]))]
    for _ in range(SUBLANES - 1):
        pr.append(cmul(pr[-1], p1))
    if backward:
        pr = pr[::-1]
    rows = [p[0] for p in pr] + [p[1] for p in pr]
    for p in (p1, p2, p4, p8):
        rows += [p[0], p[1]]
    return jnp.stack(rows, 0)


def _s5_scan_kernel(uc_ref, mi_ref, ws_ref, wo_ref, tab_ref, s0_ref, y_ref, fin_ref, s_scr, *, batch):
    rows = uc_ref.shape[0]
    chunks = rows // batch
    n_tiles = chunks // SUBLANES
    u = uc_ref[...]
    s_scr[...] = _dot(u, ws_ref[...])
    row = lax.broadcasted_iota(jnp.int32, (SUBLANES, LANES), 0)

    def tile_scan(r0, backward, c_re, c_im):
        base = TAB_ROWS if backward else 0
        col = 2 * LANES if backward else 0
        rs = pl.ds(r0, SUBLANES)

        def shift(z, k):
            if backward:
                return jnp.where(row < SUBLANES - k, pltpu.roll(z, SUBLANES - k, 0), 0.0)
            return jnp.where(row >= k, pltpu.roll(z, k, 0), 0.0)

        z_re = s_scr[rs, col:col + LANES]
        z_im = s_scr[rs, col + LANES:col + 2 * LANES]
        for k, t in ((1, 16), (2, 18), (4, 20)):
            a_re = tab_ref[base + t:base + t + 1, :]
            a_im = tab_ref[base + t + 1:base + t + 2, :]
            sh_re = shift(z_re, k)
            sh_im = shift(z_im, k)
            z_re, z_im = z_re + (a_re * sh_re - a_im * sh_im), z_im + (a_re * sh_im + a_im * sh_re)
        pr_re = tab_ref[base:base + SUBLANES, :]
        pr_im = tab_ref[base + SUBLANES:base + 2 * SUBLANES, :]
        s_scr[rs, col:col + LANES] = pr_re * c_re - pr_im * c_im + shift(z_re, 1)
        s_scr[rs, col + LANES:col + 2 * LANES] = pr_re * c_im + pr_im * c_re + shift(z_im, 1)
        last = 0 if backward else SUBLANES - 1
        l_re = jnp.broadcast_to(z_re[last:last + 1, :], (SUBLANES, LANES))
        l_im = jnp.broadcast_to(z_im[last:last + 1, :], (SUBLANES, LANES))
        p8_re = tab_ref[base + 22:base + 23, :]
        p8_im = tab_ref[base + 23:base + 24, :]
        return p8_re * c_re - p8_im * c_im + l_re, p8_re * c_im + p8_im * c_re + l_im

    def body(m, carry):
        out = []
        for b in range(batch):
            cf_re, cf_im, cb_re, cb_im = carry[4 * b:4 * b + 4]
            rf = pl.multiple_of(b * chunks + m * SUBLANES, SUBLANES)
            rb = pl.multiple_of(b * chunks + (n_tiles - 1 - m) * SUBLANES, SUBLANES)
            out += list(tile_scan(rf, False, cf_re, cf_im))
            out += list(tile_scan(rb, True, cb_re, cb_im))
        return tuple(out)

    init = tuple(jnp.broadcast_to(s0_ref[t, b:b + 1, :], (SUBLANES, LANES))
                 for b in range(batch) for t in range(4))
    fin = lax.fori_loop(0, n_tiles, body, init)
    fin_ref[...] = jnp.zeros(fin_ref.shape, F32)
    for b in range(batch):
        for t in range(4):
            fin_ref[t, b:b + 1, :] = fin[4 * b + t][0:1, :]
    y = _dot(u, mi_ref[...]) + _dot(s_scr[...].astype(BF16), wo_ref[...])
    y_ref[...] = y.astype(y_ref.dtype)


def _s5_scan(uc, mi, ws, wo, tab, s0, batch):
    rows = uc.shape[0]
    pair = lambda *shape: pl.BlockSpec((None,) + shape, lambda p: (p,) + (0,) * len(shape))
    return pl.pallas_call(
        functools.partial(_s5_scan_kernel, batch=batch),
        grid=(N_PAIRS,),
        in_specs=[pl.BlockSpec((rows, PAIR_W), lambda p: (0, p)),
                  pair(PAIR_W, PAIR_W), pair(PAIR_W, PAIR_W), pair(PAIR_W, PAIR_W),
                  pair(2 * TAB_ROWS, LANES), pair(4, SUBLANES, LANES)],
        out_specs=[pl.BlockSpec((rows, PAIR_W), lambda p: (0, p)), pair(4, SUBLANES, LANES)],
        out_shape=[jax.ShapeDtypeStruct((rows, N_PAIRS * PAIR_W), BF16),
                   jax.ShapeDtypeStruct((N_PAIRS, 4, SUBLANES, LANES), F32)],
        scratch_shapes=[pltpu.VMEM((rows, PAIR_W), F32)],
        compiler_params=pltpu.CompilerParams(
            dimension_semantics=("arbitrary",), vmem_limit_bytes=VMEM_LIMIT),
        name="s5_scan",
    )(uc, mi, ws, wo, tab, s0)


def _mix_out_kernel(y_ref, siga_ref, sb_ref, x_ref, pos_ref, mod_ref, wv_ref, wg_ref, wo_ref,
                    ln1_ref, wrh_ref, wrl_ref, br_ref, x1_ref, h2_ref, route_ref, y_scr):
    _from_chunk_tile(y_ref, y_scr)
    y = jnp.concatenate([y_scr[v] for v in range(S5_WIDTH // LANES)], axis=-1)
    ya = (0.5 * y * (1.0 + jnp.tanh(math.sqrt(2.0 / math.pi) * (y + 0.044715 * (y * y * y))))).astype(BF16)
    out_a = _dot(ya, wv_ref[...]) * _sigmoid(_dot(ya, wg_ref[...]))
    merged = siga_ref[...].astype(F32) * out_a + sb_ref[...].astype(F32)
    mix = _dot(merged.astype(BF16), wo_ref[...])
    xp = x_ref[...] + pos_ref[...]
    x1 = _ln(ALPHA * xp + mod_ref[0:1, :] * mix) * ln1_ref[0:1, :] + ln1_ref[1:2, :]
    x1_ref[...] = x1
    h2 = _ln(x1) * mod_ref[2:3, :] + mod_ref[1:2, :]
    h2_ref[...] = h2
    h_hi = h2.astype(BF16)
    h_lo = (h2 - h_hi.astype(F32)).astype(BF16)
    lg = (_dot(h_hi, wrh_ref[...]) + _dot(h_hi, wrl_ref[...]) + _dot(h_lo, wrh_ref[...])) + br_ref[...]
    tm = lg.shape[0]
    lane = lax.broadcasted_iota(jnp.int32, (tm, ROUTE_LANES), 1).astype(F32)
    neg = jnp.float32(-jnp.inf)
    big = jnp.float32(ROUTE_LANES)
    gl = jnp.where(lane < N_EXPERT_GROUPS, lg, neg)
    gmax = jnp.max(gl, axis=-1, keepdims=True)
    g_idx = jnp.min(jnp.where(gl == gmax, lane, big), axis=-1, keepdims=True)
    p_group = 1.0 / jnp.sum(jnp.exp(gl - gmax), axis=-1, keepdims=True)
    e_lo = N_EXPERT_GROUPS + g_idx * EXPERTS_PER_GROUP
    el = jnp.where((lane >= e_lo) & (lane < e_lo + EXPERTS_PER_GROUP), lg, neg)
    m1 = jnp.max(el, axis=-1, keepdims=True)
    i1 = jnp.min(jnp.where(el == m1, lane, big), axis=-1, keepdims=True)
    el2 = jnp.where(lane == i1, neg, el)
    m2 = jnp.max(el2, axis=-1, keepdims=True)
    i2 = jnp.min(jnp.where(el2 == m2, lane, big), axis=-1, keepdims=True)
    r = jnp.exp(m2 - m1)
    w1 = p_group / (1.0 + r)
    w2 = p_group * r / (1.0 + r)
    e1 = i1 - N_EXPERT_GROUPS
    e2 = i2 - N_EXPERT_GROUPS
    route_ref[...] = jnp.where(lane == 0, e1, jnp.where(lane == 1, e2,
                               jnp.where(lane == 2, w1, jnp.where(lane == 3, w2, 0.0))))


def _mix_out(y, siga, sb, x, pos, mod, wv, wg, wo, ln1, wrh, wrl, br, tm):
    b, l, d = x.shape
    n_tiles = l // tm
    tok = lambda w: pl.BlockSpec((None, tm, w), lambda i, j: (j, i, 0))
    whole = lambda a: pl.BlockSpec(a.shape, lambda i, j: (0,) * a.ndim)
    return pl.pallas_call(
        _mix_out_kernel,
        grid=(n_tiles, b),
        in_specs=[pl.BlockSpec((tm // CHUNK, CHUNK * S5_WIDTH), lambda i, j: (j * n_tiles + i, 0)),
                  tok(d), tok(d), tok(d),
                  pl.BlockSpec((tm, d), lambda i, j: (i, 0)),
                  pl.BlockSpec((None, 4, d), lambda i, j: (j, 0, 0)),
                  whole(wv), whole(wg), whole(wo), whole(ln1), whole(wrh), whole(wrl), whole(br)],
        out_specs=[tok(d), tok(d), tok(ROUTE_LANES)],
        out_shape=[jax.ShapeDtypeStruct((b, l, d), F32),
                   jax.ShapeDtypeStruct((b, l, d), F32),
                   jax.ShapeDtypeStruct((b, l, ROUTE_LANES), F32)],
        scratch_shapes=[pltpu.VMEM((S5_WIDTH // LANES, tm, LANES), F32)],
        compiler_params=pltpu.CompilerParams(
            dimension_semantics=("arbitrary", "arbitrary"), vmem_limit_bytes=VMEM_LIMIT),
        name="mix_out",
    )(y, siga, sb, x, pos, mod, wv, wg, wo, ln1, wrh, wrl, br)


def _row_gather_copy(src_hbm, row, dst, dst_row, sem):
    return pltpu.make_async_copy(src_hbm.at[pl.ds(row, 1), :], dst.at[pl.ds(dst_row, 1), :], sem)


def _experts_kernel(be_ref, tok_ref, h_hbm, wg_ref, wu_ref, wd_ref, ys_ref, xs_buf, sem):
    i = pl.program_id(0)
    n = pl.num_programs(0)
    slot = i % 2

    def issue(block, s):
        def body(r, _):
            _row_gather_copy(h_hbm, tok_ref[block * ROW_BLOCK + r], xs_buf.at[s], r, sem.at[s]).start()
            return 0
        lax.fori_loop(0, ROW_BLOCK, body, 0)

    def wait(s):
        pltpu.make_async_copy(h_hbm.at[pl.ds(0, ROW_BLOCK), :], xs_buf.at[s], sem.at[s]).wait()

    @pl.when(i == 0)
    def _():
        issue(0, 0)

    wait(slot)

    @pl.when(i + 1 < n)
    def _():
        issue(i + 1, 1 - slot)

    xb = xs_buf[slot].astype(BF16)
    gate = _dot(xb, wg_ref[...])
    up = _dot(xb, wu_ref[...])
    hid = (gate * _sigmoid(gate) * up).astype(BF16)
    ys_ref[...] = _dot(hid, wd_ref[...])


def _experts(block_e, src_tok, h2, wg, wu, wd, n_blocks):
    d = h2.shape[1]
    grid_spec = pltpu.PrefetchScalarGridSpec(
        num_scalar_prefetch=2,
        grid=(n_blocks,),
        in_specs=[pl.BlockSpec(memory_space=pl.ANY),
                  pl.BlockSpec((None, d, EXPERT_FF), lambda i, be, st: (be[i], 0, 0)),
                  pl.BlockSpec((None, d, EXPERT_FF), lambda i, be, st: (be[i], 0, 0)),
                  pl.BlockSpec((None, EXPERT_FF, d), lambda i, be, st: (be[i], 0, 0))],
        out_specs=pl.BlockSpec((ROW_BLOCK, d), lambda i, be, st: (i, 0)),
        scratch_shapes=[pltpu.VMEM((2, ROW_BLOCK, d), F32), pltpu.SemaphoreType.DMA((2,))],
    )
    return pl.pallas_call(
        _experts_kernel,
        grid_spec=grid_spec,
        out_shape=jax.ShapeDtypeStruct((n_blocks * ROW_BLOCK, d), F32),
        compiler_params=pltpu.CompilerParams(
            dimension_semantics=("arbitrary",), vmem_limit_bytes=VMEM_LIMIT),
        name="experts",
    )(block_e, src_tok, h2, wg, wu, wd)


def _combine_kernel(dest_ref, ys_hbm, x1_ref, route_ref, mod_ref, ln2_ref, o_ref, buf, sem, *, tm):
    i = pl.program_id(0)
    n = pl.num_programs(0)
    slot = i % 2

    def issue(tile, s):
        def body(t, _):
            base = (tile * tm + t) * TOP_K
            _row_gather_copy(ys_hbm, dest_ref[base], buf.at[s], t, sem.at[s]).start()
            _row_gather_copy(ys_hbm, dest_ref[base + 1], buf.at[s], tm + t, sem.at[s]).start()
            return 0
        lax.fori_loop(0, tm, body, 0)

    def wait(s):
        pltpu.make_async_copy(ys_hbm.at[pl.ds(0, TOP_K * tm), :], buf.at[s], sem.at[s]).wait()

    @pl.when(i == 0)
    def _():
        issue(0, 0)

    wait(slot)

    @pl.when(i + 1 < n)
    def _():
        issue(i + 1, 1 - slot)

    w1 = route_ref[:, 2:3]
    w2 = route_ref[:, 3:4]
    moe = buf[slot, 0:tm, :] * w1 + buf[slot, tm:2 * tm, :] * w2
    z = ALPHA * x1_ref[...] + mod_ref[0:1, :] * moe
    o_ref[...] = _ln(z) * ln2_ref[0:1, :] + ln2_ref[1:2, :]


def _combine(dest, ys, x1, route, mod, ln2, tm, tiles_per_batch):
    t, d = x1.shape
    grid_spec = pltpu.PrefetchScalarGridSpec(
        num_scalar_prefetch=1,
        grid=(t // tm,),
        in_specs=[pl.BlockSpec(memory_space=pl.ANY),
                  pl.BlockSpec((tm, d), lambda i, de: (i, 0)),
                  pl.BlockSpec((tm, ROUTE_LANES), lambda i, de: (i, 0)),
                  pl.BlockSpec((None, 8, d), lambda i, de: (i // tiles_per_batch, 0, 0)),
                  pl.BlockSpec((2, d), lambda i, de: (0, 0))],
        out_specs=pl.BlockSpec((tm, d), lambda i, de: (i, 0)),
        scratch_shapes=[pltpu.VMEM((2, TOP_K * tm, d), F32), pltpu.SemaphoreType.DMA((2,))],
    )
    return pl.pallas_call(
        functools.partial(_combine_kernel, tm=tm),
        grid_spec=grid_spec,
        out_shape=jax.ShapeDtypeStruct((t, d), F32),
        compiler_params=pltpu.CompilerParams(
            dimension_semantics=("arbitrary",), vmem_limit_bytes=VMEM_LIMIT),
        name="combine",
    )(dest, ys, x1, route, mod, ln2)


def _sincos_2d(rows, cols, dim):
    q = dim // 4
    omega = 1.0 / (POS_BASE ** (jnp.arange(q, dtype=F32) / q))
    r = jnp.arange(rows, dtype=F32)[:, None] * omega
    cl = jnp.arange(cols, dtype=F32)[:, None] * omega
    r_emb = jnp.concatenate([jnp.sin(r), jnp.cos(r)], -1)
    c_emb = jnp.concatenate([jnp.sin(cl), jnp.cos(cl)], -1)
    emb = jnp.concatenate([jnp.broadcast_to(r_emb[:, None, :], (rows, cols, 2 * q)),
                           jnp.broadcast_to(c_emb[None, :, :], (rows, cols, 2 * q))], -1)
    return emb.reshape(rows * cols, dim)


def _routing_tables(e_idx, n_blocks):
    t, k = e_idx.shape
    n_assign = t * k
    flat_e = e_idx.reshape(n_assign)
    order = jnp.argsort(flat_e)
    sorted_e = flat_e[order]
    counts = jnp.zeros((N_EXPERTS,), jnp.int32).at[flat_e].add(1)
    padded = (counts + ROW_BLOCK - 1) // ROW_BLOCK * ROW_BLOCK
    pad_end = jnp.cumsum(padded)
    pad_start = pad_end - padded
    start = jnp.cumsum(counts) - counts
    dest = pad_start[sorted_e] + jnp.arange(n_assign, dtype=jnp.int32) - start[sorted_e]
    n_rows = n_blocks * ROW_BLOCK
    src_tok = jnp.zeros((n_rows,), jnp.int32).at[dest].set((order // k).astype(jnp.int32))
    block_e = jnp.minimum(
        jnp.searchsorted(pad_end, jnp.arange(n_blocks, dtype=jnp.int32) * ROW_BLOCK, side='right'),
        N_EXPERTS - 1).astype(jnp.int32)
    dest_of_assign = jnp.zeros((n_assign,), jnp.int32).at[order].set(dest.astype(jnp.int32))
    return block_e, src_tok, dest_of_assign


def kernel(x, c, ctx, c_ctx, w_ada, b_ada, w_in, s5_log_dt_f, s5_a_re_f, s5_a_im_f, s5_b_re_f, s5_b_im_f, s5_c_re_f, s5_c_im_f, s5_log_dt_b, s5_a_re_b, s5_a_im_b, s5_b_re_b, s5_b_im_b, s5_c_re_b, s5_c_im_b, s5_d, s5_w_glu_val, s5_w_glu_gate, conv_w, conv_w_out, w_o, ln1_g, ln1_b, router_w_group, router_b_group, router_w_expert, router_b_expert, exp_w_gate, exp_w_up, exp_w_down, ln2_g, ln2_b):
    b, l, d = x.shape
    lc = ctx.shape[1]
    assert d == D_MODEL and b < SUBLANES and w_ada.shape[0] == DEPTH
    assert l % (SUBLANES * CHUNK) == 0 and lc % (SUBLANES * CHUNK) == 0 and l % GRID_W == 0
    t = b * l
    tm = min(512, l)
    tmc = min(512, lc)

    cc = jnp.concatenate([c, c_ctx[None, :], jnp.zeros((8 - b - 1, d), F32)], 0)
    mods = _mods(cc, w_ada[0], b_ada[0])
    sh1, sc1, g1, sh2, sc2, g2 = jnp.split(mods, 6, axis=-1)
    mod_a = jnp.stack([sh1[:b], 1.0 + sc1[:b]], 1)
    mod_ctx = jnp.broadcast_to(jnp.stack([sh1[b], 1.0 + sc1[b]], 0)[None], (b, 2, d))
    mod_c = jnp.stack([g1[:b], sh2[:b], 1.0 + sc2[:b], jnp.zeros((b, d), F32)], 1)
    mod_f = jnp.concatenate([g2[:b, None, :], jnp.zeros((b, 7, d), F32)], 1)

    w_in_bf = w_in[0].astype(BF16)
    f_tab = _s5_dir_tables(s5_log_dt_f[0], s5_a_re_f[0], s5_a_im_f[0], s5_b_re_f[0], s5_b_im_f[0],
                           s5_c_re_f[0], s5_c_im_f[0])
    b_tab = _s5_dir_tables(s5_log_dt_b[0], s5_a_re_b[0], s5_a_im_b[0], s5_b_re_b[0], s5_b_im_b[0],
                           s5_c_re_b[0], s5_c_im_b[0])
    mi, ws, wo_s5, tab = _s5_operators(f_tab, b_tab, s5_d[0])

    (uc_ctx,) = _in_proj(ctx, jnp.zeros((lc, d), F32), mod_ctx, w_in_bf, None, None, tmc, False)
    zero_state = jnp.zeros((N_PAIRS, 4, SUBLANES, LANES), F32)
    _, s0 = _s5_scan(uc_ctx, mi, ws, wo_s5, tab, zero_state, b)

    pos = _sincos_2d(l // GRID_W, GRID_W, d)
    uc, siga, sb = _in_proj(x, pos, mod_a, w_in_bf, conv_w[0], conv_w_out[0].astype(BF16), tm, True)
    y, _ = _s5_scan(uc, mi, ws, wo_s5, tab, s0, b)

    wr = jnp.concatenate([router_w_group[0], router_w_expert[0],
                          jnp.zeros((d, ROUTE_LANES - N_EXPERT_GROUPS - N_EXPERTS), F32)], 1)
    wr_hi = wr.astype(BF16)
    wr_lo = (wr - wr_hi.astype(F32)).astype(BF16)
    br = jnp.concatenate([router_b_group[0], router_b_expert[0],
                          jnp.zeros((ROUTE_LANES - N_EXPERT_GROUPS - N_EXPERTS,), F32)])[None, :]
    ln1 = jnp.stack([ln1_g[0], ln1_b[0]], 0)
    x1, h2, route = _mix_out(y, siga, sb, x, pos, mod_c,
                             s5_w_glu_val[0].astype(BF16), s5_w_glu_gate[0].astype(BF16),
                             w_o[0].astype(BF16), ln1, wr_hi, wr_lo, br, tm)

    x1 = x1.reshape(t, d)
    h2 = h2.reshape(t, d)
    route = route.reshape(t, ROUTE_LANES)
    e_idx = route[:, 0:TOP_K].astype(jnp.int32)
    n_blocks = (t * TOP_K) // ROW_BLOCK + N_EXPERTS
    block_e, src_tok, dest = _routing_tables(e_idx, n_blocks)
    ys = _experts(block_e, src_tok, h2, exp_w_gate[0].astype(BF16), exp_w_up[0].astype(BF16),
                  exp_w_down[0].astype(BF16), n_blocks)
    tmf = min(256, l)
    ln2 = jnp.stack([ln2_g[0], ln2_b[0]], 0)
    out = _combine(dest, ys, x1, route, mod_f, ln2, tmf, l // tmf)
    return out.reshape(b, l, d)
```

```python
import functools
import math

import jax
import jax.numpy as jnp
from jax import lax
from jax.experimental import pallas as pl
from jax.experimental.pallas import tpu as pltpu

F32 = jnp.float32
BF16 = jnp.bfloat16
HI = lax.Precision.HIGHEST

D_MODEL = 1024
GRID_W = 64
S5_WIDTH = 512
S5_GROUP_CH = 16
S5_GROUPS = S5_WIDTH // S5_GROUP_CH
S5_STATE = 64
CONV_WIDTH = 512
N_EXPERT_GROUPS = 4
EXPERTS_PER_GROUP = 8
N_EXPERTS = N_EXPERT_GROUPS * EXPERTS_PER_GROUP
EXPERT_FF = 512
TOP_K = 2
DEPTH = 1
ALPHA = (2.0 * DEPTH) ** 0.25
LN_EPS = 1e-6
POS_BASE = 10000.0

LANES = 128
SUBLANES = 8
CHUNK = 16
GROUP_W = CHUNK * S5_GROUP_CH
PAIR_W = 2 * GROUP_W
N_PAIRS = S5_GROUPS // 2
TOK_PER_VREG = LANES // S5_GROUP_CH
TAB_ROWS = 24
ROUTE_LANES = 128
ROW_BLOCK = 256
PIECE = 16
VMEM_LIMIT = 56 * 1024 * 1024


def _ln(x):
    mu = jnp.mean(x, axis=-1, keepdims=True)
    xc = x - mu
    var = jnp.mean(xc * xc, axis=-1, keepdims=True)
    return xc * lax.rsqrt(var + LN_EPS)


def _sigmoid(x):
    return 1.0 / (1.0 + jnp.exp(-x))


def _dot(a, b):
    return jnp.dot(a, b, preferred_element_type=F32)


def _mods_kernel(c_ref, w_ref, b_ref, o_ref):
    c = c_ref[...]
    a = c * _sigmoid(c)
    o_ref[...] = jnp.dot(a, w_ref[...], precision=HI, preferred_element_type=F32) + b_ref[...]


def _mods(cc, w_ada, b_ada):
    n = w_ada.shape[1]
    nb = 1536
    return pl.pallas_call(
        _mods_kernel,
        grid=(n // nb,),
        in_specs=[pl.BlockSpec((8, D_MODEL), lambda i: (0, 0)),
                  pl.BlockSpec((D_MODEL, nb), lambda i: (0, i)),
                  pl.BlockSpec((1, nb), lambda i: (0, i))],
        out_specs=pl.BlockSpec((8, nb), lambda i: (0, i)),
        out_shape=jax.ShapeDtypeStruct((8, n), F32),
        compiler_params=pltpu.CompilerParams(vmem_limit_bytes=VMEM_LIMIT),
        name="mods",
    )(cc, w_ada, b_ada.reshape(1, n))


def _slot_masks(rows):
    slot = lax.broadcasted_iota(jnp.int32, (rows, LANES), 1) // S5_GROUP_CH
    return [slot == s for s in range(TOK_PER_VREG)]


def _to_chunk_tile(u_scr, uc_ref):
    nch = uc_ref.shape[0]
    masks = _slot_masks(nch)
    for qh in range(CHUNK // TOK_PER_VREG):
        for v in range(S5_WIDTH // LANES):
            src = [u_scr[v, pl.ds(qh * TOK_PER_VREG + s, nch, stride=CHUNK), :] for s in range(TOK_PER_VREG)]
            for i in range(TOK_PER_VREG):
                acc = None
                for s in range(TOK_PER_VREG):
                    shift = ((s - i) * S5_GROUP_CH) % LANES
                    piece = pltpu.roll(src[s], shift, 1) if shift else src[s]
                    acc = piece if acc is None else jnp.where(masks[s], piece, acc)
                lo = (v * TOK_PER_VREG + i) * GROUP_W + qh * LANES
                uc_ref[:, lo:lo + LANES] = acc.astype(uc_ref.dtype)


def _from_chunk_tile(yc_ref, y_scr):
    nch = yc_ref.shape[0]
    masks = _slot_masks(nch)
    for qh in range(CHUNK // TOK_PER_VREG):
        for v in range(S5_WIDTH // LANES):
            src = []
            for i in range(TOK_PER_VREG):
                lo = (v * TOK_PER_VREG + i) * GROUP_W + qh * LANES
                src.append(yc_ref[:, lo:lo + LANES].astype(F32))
            for s in range(TOK_PER_VREG):
                acc = None
                for i in range(TOK_PER_VREG):
                    shift = ((i - s) * S5_GROUP_CH) % LANES
                    piece = pltpu.roll(src[i], shift, 1) if shift else src[i]
                    acc = piece if acc is None else jnp.where(masks[i], piece, acc)
                y_scr[v, pl.ds(qh * TOK_PER_VREG + s, nch, stride=CHUNK), :] = acc


def _in_proj_kernel(x_ref, pos_ref, mod_ref, w_ref, *rest, full):
    if full:
        cw_ref, cwo_ref, uc_ref, siga_ref, sb_ref, u_scr = rest
    else:
        uc_ref, u_scr = rest
    xp = x_ref[...] + pos_ref[...]
    h = (_ln(xp) * mod_ref[1:2, :] + mod_ref[0:1, :]).astype(BF16)
    o1, o2, o3, o4, o5 = 512, 1024, 1536, 2048, 3072
    u = _dot(h, w_ref[:, 0:o1])
    for v in range(S5_WIDTH // LANES):
        u_scr[v] = u[:, v * LANES:(v + 1) * LANES]
    _to_chunk_tile(u_scr, uc_ref)
    if not full:
        return
    z_b = _dot(h, w_ref[:, o1:o2])
    gate_c = _dot(h, w_ref[:, o3:o4])
    p = gate_c * z_b
    tm = p.shape[0]
    col = lax.broadcasted_iota(jnp.int32, (tm, 1), 0) % GRID_W
    prev = jnp.where(col == 0, 0.0, pltpu.roll(p, 1, 0))
    nxt = jnp.where(col == GRID_W - 1, 0.0, pltpu.roll(p, tm - 1, 0))
    v = cw_ref[0:1, :] * prev + cw_ref[1:2, :] * p + cw_ref[2:3, :] * nxt
    gate_b = _dot(h, w_ref[:, o2:o3])
    out_b = _dot((gate_b * v).astype(BF16), cwo_ref[...])
    merge_b = _dot(h, w_ref[:, o5:])
    sb_ref[...] = (_sigmoid(merge_b) * out_b).astype(sb_ref.dtype)
    merge_a = _dot(h, w_ref[:, o4:o5])
    siga_ref[...] = _sigmoid(merge_a).astype(siga_ref.dtype)


def _in_proj(x, pos, mod, w_in_bf, conv_w, conv_w_out_bf, tm, full):
    b, l, d = x.shape
    n_tiles = l // tm
    grid = (n_tiles, b)
    tok = lambda w: pl.BlockSpec((None, tm, w), lambda i, j: (j, i, 0))
    chunk_spec = pl.BlockSpec((tm // CHUNK, CHUNK * S5_WIDTH), lambda i, j: (j * n_tiles + i, 0))
    chunk_shape = jax.ShapeDtypeStruct((b * l // CHUNK, CHUNK * S5_WIDTH), BF16)
    in_specs = [tok(d),
                pl.BlockSpec((tm, d), lambda i, j: (i, 0)),
                pl.BlockSpec((None, 2, d), lambda i, j: (j, 0, 0))]
    args = [x, pos, mod]
    if full:
        in_specs += [pl.BlockSpec(w_in_bf.shape, lambda i, j: (0, 0)),
                     pl.BlockSpec(conv_w.shape, lambda i, j: (0, 0)),
                     pl.BlockSpec(conv_w_out_bf.shape, lambda i, j: (0, 0))]
        args += [w_in_bf, conv_w, conv_w_out_bf]
        out_specs = [chunk_spec, tok(d), tok(d)]
        out_shape = [chunk_shape,
                     jax.ShapeDtypeStruct((b, l, d), BF16),
                     jax.ShapeDtypeStruct((b, l, d), BF16)]
    else:
        in_specs += [pl.BlockSpec((d, S5_WIDTH), lambda i, j: (0, 0))]
        args += [w_in_bf]
        out_specs = [chunk_spec]
        out_shape = [chunk_shape]
    return pl.pallas_call(
        functools.partial(_in_proj_kernel, full=full),
        grid=grid, in_specs=in_specs, out_specs=out_specs, out_shape=out_shape,
        scratch_shapes=[pltpu.VMEM((S5_WIDTH // LANES, tm, LANES), F32)],
        compiler_params=pltpu.CompilerParams(
            dimension_semantics=("arbitrary", "arbitrary"), vmem_limit_bytes=VMEM_LIMIT),
        name="in_proj" if full else "in_proj_ctx",
    )(*args)


def _s5_dir_tables(log_dt, a_re, a_im, b_re, b_im, c_re, c_im):
    f32 = F32
    dt = jnp.exp(log_dt.astype(f32))[:, None]
    a_re = a_re.astype(f32)
    a_im = a_im.astype(f32)
    mag = jnp.exp(dt * a_re)
    ab_re = mag * jnp.cos(dt * a_im)
    ab_im = mag * jnp.sin(dt * a_im)
    den = a_re * a_re + a_im * a_im
    x_re = ab_re - 1.0
    f_re = (x_re * a_re + ab_im * a_im) / den
    f_im = (ab_im * a_re - x_re * a_im) / den
    b_re = b_re.astype(f32)
    b_im = b_im.astype(f32)
    bb_re = f_re[..., None] * b_re - f_im[..., None] * b_im
    bb_im = f_re[..., None] * b_im + f_im[..., None] * b_re
    k = jnp.arange(CHUNK + 1, dtype=f32)[:, None, None]
    pmag = jnp.exp(k * (dt * a_re)[None])
    p_re = pmag * jnp.cos(k * (dt * a_im)[None])
    p_im = pmag * jnp.sin(k * (dt * a_im)[None])
    pb_re = p_re[..., None] * bb_re[None] - p_im[..., None] * bb_im[None]
    pb_im = p_re[..., None] * bb_im[None] + p_im[..., None] * bb_re[None]
    c_re = c_re.astype(f32)
    c_im = c_im.astype(f32)
    kern = (jnp.einsum('gdn,kgnc->kgcd', c_re, pb_re, precision=HI)
            - jnp.einsum('gdn,kgnc->kgcd', c_im, pb_im, precision=HI))
    cp_re = c_re[None] * p_re[:, :, None, :] - c_im[None] * p_im[:, :, None, :]
    cp_im = -(c_re[None] * p_im[:, :, None, :] + c_im[None] * p_re[:, :, None, :])
    return dict(p_re=p_re, p_im=p_im, pb_re=pb_re, pb_im=pb_im, kern=kern, cp_re=cp_re, cp_im=cp_im)


def _s5_operators(f, bk, s5_d):
    q = CHUNK
    g, n, c = S5_GROUPS, S5_STATE, S5_GROUP_CH
    qi = jnp.arange(q)
    lag = qi[None, :] - qi[:, None]
    kf = jnp.where((lag >= 0)[:, :, None, None, None], f['kern'][jnp.clip(lag, 0, q)], 0.0)
    kb = jnp.where((lag <= 0)[:, :, None, None, None], bk['kern'][jnp.clip(-lag, 0, q)], 0.0)
    skip = (jnp.eye(q, dtype=F32)[:, :, None, None, None]
            * (s5_d.astype(F32)[:, :, None] * jnp.eye(c, dtype=F32)[None])[None, None])
    m_intra = (kf + kb + skip).transpose(2, 0, 3, 1, 4).reshape(g, q * c, q * c)
    wf_re = f['pb_re'][q - 1 - qi]
    wf_im = f['pb_im'][q - 1 - qi]
    wb_re = bk['pb_re'][qi]
    wb_im = bk['pb_im'][qi]
    w_st = jnp.stack([wf_re, wf_im, wb_re, wb_im], 0)
    w_st = w_st.transpose(2, 1, 4, 0, 3).reshape(g, q * c, 4, n)
    of_re = f['cp_re'][qi + 1]
    of_im = f['cp_im'][qi + 1]
    ob_re = bk['cp_re'][q - qi]
    ob_im = bk['cp_im'][q - qi]
    w_out = jnp.stack([of_re, of_im, ob_re, ob_im], 0)
    w_out = w_out.transpose(2, 0, 4, 1, 3).reshape(g, 4, n, q * c)
    np_ = N_PAIRS
    eye2 = jnp.eye(2, dtype=F32)
    mi = m_intra.reshape(np_, 2, q * c, q * c)
    mi_pair = (mi[:, :, :, None, :] * eye2[None, :, None, :, None]).reshape(np_, PAIR_W, PAIR_W)
    ws = w_st.reshape(np_, 2, q * c, 4, n)
    ws_pair = (ws[:, :, :, :, None, :] * eye2[None, :, None, None, :, None]).reshape(np_, PAIR_W, 4 * 2 * n)
    wo = w_out.reshape(np_, 2, 4, n, q * c).transpose(0, 2, 1, 3, 4)
    wo_pair = (wo[:, :, :, :, None, :] * eye2[None, None, :, None, :, None]).reshape(np_, 4 * 2 * n, PAIR_W)
    tab = jnp.concatenate([_chunk_power_table(f, False), _chunk_power_table(bk, True)], 0)
    tab = tab.reshape(2 * TAB_ROWS, np_, 2 * n).transpose(1, 0, 2)
    return mi_pair.astype(BF16), ws_pair.astype(BF16), wo_pair.astype(BF16), tab


def _chunk_power_table(t, backward):
    def cmul(x, y):
        return x[0] * y[0] - x[1] * y[1], x[0] * y[1] + x[1] * y[0]
    p1 = (t['p_re'][CHUNK], t['p_im'][CHUNK])
    p2 = cmul(p1, p1)
    p4 = cmul(p2, p2)
    p8 = cmul(p4, p4)
    pr = [(jnp.ones_like(p1[0]), jnp.zeros_like(p1[0]))]
    for _ in range(SUBLANES - 1):
        pr.append(cmul(pr[-1], p1))
    if backward:
        pr = pr[::-1]
    rows = [p[0] for p in pr] + [p[1] for p in pr]
    for p in (p1, p2, p4, p8):
        rows += [p[0], p[1]]
    return jnp.stack(rows, 0)


def _s5_scan_kernel(uc_ref, mi_ref, ws_ref, wo_ref, tab_ref, s0_ref, y_ref, fin_ref, s_scr, *, batch):
    rows = uc_ref.shape[0]
    chunks = rows // batch
    n_tiles = chunks // SUBLANES
    u = uc_ref[...]
    s_scr[...] = _dot(u, ws_ref[...])
    row = lax.broadcasted_iota(jnp.int32, (SUBLANES, LANES), 0)

    def tile_scan(r0, backward, c_re, c_im):
        base = TAB_ROWS if backward else 0
        col = 2 * LANES if backward else 0
        rs = pl.ds(r0, SUBLANES)

        def shift(z, k):
            if backward:
                return jnp.where(row < SUBLANES - k, pltpu.roll(z, SUBLANES - k, 0), 0.0)
            return jnp.where(row >= k, pltpu.roll(z, k, 0), 0.0)

        z_re = s_scr[rs, col:col + LANES]
        z_im = s_scr[rs, col + LANES:col + 2 * LANES]
        for k, t in ((1, 16), (2, 18), (4, 20)):
            a_re = tab_ref[base + t:base + t + 1, :]
            a_im = tab_ref[base + t + 1:base + t + 2, :]
            sh_re = shift(z_re, k)
            sh_im = shift(z_im, k)
            z_re, z_im = z_re + (a_re * sh_re - a_im * sh_im), z_im + (a_re * sh_im + a_im * sh_re)
        pr_re = tab_ref[base:base + SUBLANES, :]
        pr_im = tab_ref[base + SUBLANES:base + 2 * SUBLANES, :]
        s_scr[rs, col:col + LANES] = pr_re * c_re - pr_im * c_im + shift(z_re, 1)
        s_scr[rs, col + LANES:col + 2 * LANES] = pr_re * c_im + pr_im * c_re + shift(z_im, 1)
        last = 0 if backward else SUBLANES - 1
        l_re = jnp.broadcast_to(z_re[last:last + 1, :], (SUBLANES, LANES))
        l_im = jnp.broadcast_to(z_im[last:last + 1, :], (SUBLANES, LANES))
        p8_re = tab_ref[base + 22:base + 23, :]
        p8_im = tab_ref[base + 23:base + 24, :]
        return p8_re * c_re - p8_im * c_im + l_re, p8_re * c_im + p8_im * c_re + l_im

    def body(m, carry):
        out = []
        for b in range(batch):
            cf_re, cf_im, cb_re, cb_im = carry[4 * b:4 * b + 4]
            rf = pl.multiple_of(b * chunks + m * SUBLANES, SUBLANES)
            rb = pl.multiple_of(b * chunks + (n_tiles - 1 - m) * SUBLANES, SUBLANES)
            out += list(tile_scan(rf, False, cf_re, cf_im))
            out += list(tile_scan(rb, True, cb_re, cb_im))
        return tuple(out)

    init = tuple(jnp.broadcast_to(s0_ref[t, b:b + 1, :], (SUBLANES, LANES))
                 for b in range(batch) for t in range(4))
    fin = lax.fori_loop(0, n_tiles, body, init)
    fin_ref[...] = jnp.zeros(fin_ref.shape, F32)
    for b in range(batch):
        for t in range(4):
            fin_ref[t, b:b + 1, :] = fin[4 * b + t][0:1, :]
    y = _dot(u, mi_ref[...]) + _dot(s_scr[...].astype(BF16), wo_ref[...])
    y_ref[...] = y.astype(y_ref.dtype)


def _s5_scan(uc, mi, ws, wo, tab, s0, batch):
    rows = uc.shape[0]
    pair = lambda *shape: pl.BlockSpec((None,) + shape, lambda p: (p,) + (0,) * len(shape))
    return pl.pallas_call(
        functools.partial(_s5_scan_kernel, batch=batch),
        grid=(N_PAIRS,),
        in_specs=[pl.BlockSpec((rows, PAIR_W), lambda p: (0, p)),
                  pair(PAIR_W, PAIR_W), pair(PAIR_W, PAIR_W), pair(PAIR_W, PAIR_W),
                  pair(2 * TAB_ROWS, LANES), pair(4, SUBLANES, LANES)],
        out_specs=[pl.BlockSpec((rows, PAIR_W), lambda p: (0, p)), pair(4, SUBLANES, LANES)],
        out_shape=[jax.ShapeDtypeStruct((rows, N_PAIRS * PAIR_W), BF16),
                   jax.ShapeDtypeStruct((N_PAIRS, 4, SUBLANES, LANES), F32)],
        scratch_shapes=[pltpu.VMEM((rows, PAIR_W), F32)],
        compiler_params=pltpu.CompilerParams(
            dimension_semantics=("arbitrary",), vmem_limit_bytes=VMEM_LIMIT),
        name="s5_scan",
    )(uc, mi, ws, wo, tab, s0)


def _mix_out_kernel(y_ref, siga_ref, sb_ref, x_ref, pos_ref, mod_ref, wv_ref, wg_ref, wo_ref,
                    ln1_ref, wrh_ref, wrl_ref, br_ref, ut_ref, lt_ref,
                    x1_ref, xs_ref, route_ref, len_ref, y_scr):
    _from_chunk_tile(y_ref, y_scr)
    y = jnp.concatenate([y_scr[v] for v in range(S5_WIDTH // LANES)], axis=-1)
    ya = (0.5 * y * (1.0 + jnp.tanh(math.sqrt(2.0 / math.pi) * (y + 0.044715 * (y * y * y))))).astype(BF16)
    out_a = _dot(ya, wv_ref[...]) * _sigmoid(_dot(ya, wg_ref[...]))
    merged = siga_ref[...].astype(F32) * out_a + sb_ref[...].astype(F32)
    mix = _dot(merged.astype(BF16), wo_ref[...])
    xp = x_ref[...] + pos_ref[...]
    x1 = _ln(ALPHA * xp + mod_ref[0:1, :] * mix) * ln1_ref[0:1, :] + ln1_ref[1:2, :]
    x1_ref[...] = x1
    h2 = _ln(x1) * mod_ref[2:3, :] + mod_ref[1:2, :]
    _route_and_sort(h2, wrh_ref, wrl_ref, br_ref, ut_ref, lt_ref, xs_ref, route_ref, len_ref)


def _route_and_sort(h2, wrh_ref, wrl_ref, br_ref, ut_ref, lt_ref, xs_ref, route_ref, len_ref):
    tm = h2.shape[0]
    f32 = F32
    h_hi = h2.astype(BF16)
    h_lo = (h2 - h_hi.astype(f32)).astype(BF16)
    nt = (((1,), (1,)), ((), ()))
    lg = (lax.dot_general(wrh_ref[...], h_hi, nt, preferred_element_type=f32)
          + lax.dot_general(wrl_ref[...], h_hi, nt, preferred_element_type=f32)
          + lax.dot_general(wrh_ref[...], h_lo, nt, preferred_element_type=f32)) + br_ref[...]
    rowi = lax.broadcasted_iota(jnp.int32, (ROUTE_LANES, tm), 0).astype(f32)
    neg = jnp.float32(-jnp.inf)
    big = jnp.float32(ROUTE_LANES)
    gl = jnp.where(rowi < N_EXPERT_GROUPS, lg, neg)
    gmax = jnp.max(gl, axis=0, keepdims=True)
    g_idx = jnp.min(jnp.where(gl == gmax, rowi, big), axis=0, keepdims=True)
    p_group = 1.0 / jnp.sum(jnp.exp(gl - gmax), axis=0, keepdims=True)
    e_lo = N_EXPERT_GROUPS + g_idx * EXPERTS_PER_GROUP
    el = jnp.where((rowi >= e_lo) & (rowi < e_lo + EXPERTS_PER_GROUP), lg, neg)
    m1 = jnp.max(el, axis=0, keepdims=True)
    i1 = jnp.min(jnp.where(el == m1, rowi, big), axis=0, keepdims=True)
    el2 = jnp.where(rowi == i1, neg, el)
    m2 = jnp.max(el2, axis=0, keepdims=True)
    i2 = jnp.min(jnp.where(el2 == m2, rowi, big), axis=0, keepdims=True)
    r = jnp.exp(m2 - m1)
    w1 = p_group / (1.0 + r)
    w2 = p_group * r / (1.0 + r)
    e1 = i1 - N_EXPERT_GROUPS
    e2 = i2 - N_EXPERT_GROUPS
    a12 = jnp.where(rowi == e1, 1.0, 0.0) + jnp.where(rowi == e2 + N_EXPERTS, 1.0, 0.0)
    rank = _dot(a12.astype(BF16), ut_ref[...])
    cnt = jnp.broadcast_to(jnp.sum(a12, axis=1, keepdims=True), (ROUTE_LANES, LANES))
    row = lax.broadcasted_iota(jnp.int32, (ROUTE_LANES, LANES), 0)
    tot = cnt + pltpu.roll(cnt, ROUTE_LANES - N_EXPERTS, 0)
    run = jnp.where(row < N_EXPERTS, jnp.floor((tot + (PIECE - 1)) * (1.0 / PIECE)), 0.0)
    off = PIECE * _dot(lt_ref[...], run.astype(BF16))
    base = jnp.where(row < N_EXPERTS, off, pltpu.roll(off + cnt, N_EXPERTS, 0))
    posmat = a12 * (rank + base[:, 0:1])
    pos1 = jnp.sum(posmat[0:N_EXPERTS, :], axis=0, keepdims=True)
    pos2 = jnp.sum(posmat[N_EXPERTS:2 * N_EXPERTS, :], axis=0, keepdims=True)
    ri = lax.broadcasted_iota(jnp.int32, (xs_ref.shape[0], tm), 0).astype(f32)
    perm = jnp.where((ri == pos1) | (ri == pos2), 1.0, 0.0).astype(BF16)
    xs_ref[...] = _dot(perm, h_hi).astype(xs_ref.dtype)
    rec = jnp.where(rowi == 0, e1, jnp.where(rowi == 1, e2, jnp.where(rowi == 2, w1, jnp.where(
        rowi == 3, w2, jnp.where(rowi == 4, pos1, jnp.where(rowi == 5, pos2, 0.0))))))
    route_ref[...] = rec.T
    len_ref[...] = run


def _local_rows(tm):
    return TOP_K * tm + N_EXPERTS * PIECE


def _mix_out(y, siga, sb, x, pos, mod, wv, wg, wo, ln1, wrh, wrl, br, tm):
    b, l, d = x.shape
    n_tiles = l // tm
    xs_rows = _local_rows(tm)
    tok = lambda w: pl.BlockSpec((None, tm, w), lambda i, j: (j, i, 0))
    whole = lambda a: pl.BlockSpec(a.shape, lambda i, j: (0,) * a.ndim)
    ut = (jnp.arange(tm)[:, None] < jnp.arange(tm)[None, :]).astype(BF16)
    lt = (jnp.arange(ROUTE_LANES)[None, :] < jnp.arange(ROUTE_LANES)[:, None]).astype(BF16)
    return pl.pallas_call(
        _mix_out_kernel,
        grid=(n_tiles, b),
        in_specs=[pl.BlockSpec((tm // CHUNK, CHUNK * S5_WIDTH), lambda i, j: (j * n_tiles + i, 0)),
                  tok(d), tok(d), tok(d),
                  pl.BlockSpec((tm, d), lambda i, j: (i, 0)),
                  pl.BlockSpec((None, 4, d), lambda i, j: (j, 0, 0)),
                  whole(wv), whole(wg), whole(wo), whole(ln1), whole(wrh), whole(wrl), whole(br),
                  whole(ut), whole(lt)],
        out_specs=[tok(d),
                   pl.BlockSpec((xs_rows, d), lambda i, j: (j * n_tiles + i, 0)),
                   tok(ROUTE_LANES),
                   pl.BlockSpec((None, ROUTE_LANES, LANES), lambda i, j: (j * n_tiles + i, 0, 0))],
        out_shape=[jax.ShapeDtypeStruct((b, l, d), F32),
                   jax.ShapeDtypeStruct((b * n_tiles * xs_rows, d), BF16),
                   jax.ShapeDtypeStruct((b, l, ROUTE_LANES), F32),
                   jax.ShapeDtypeStruct((b * n_tiles, ROUTE_LANES, LANES), F32)],
        scratch_shapes=[pltpu.VMEM((S5_WIDTH // LANES, tm, LANES), F32)],
        compiler_params=pltpu.CompilerParams(
            dimension_semantics=("arbitrary", "arbitrary"), vmem_limit_bytes=VMEM_LIMIT),
        name="mix_out",
    )(y, siga, sb, x, pos, mod, wv, wg, wo, ln1, wrh, wrl, br, ut, lt)


def _piece_copy(src_hbm, src_row, dst, piece, sem):
    return pltpu.make_async_copy(src_hbm.at[pl.ds(pl.multiple_of(src_row, PIECE), PIECE), :],
                                 dst.at[pl.ds(pl.multiple_of(piece * PIECE, PIECE), PIECE), :], sem)


def _issue_pieces(src_hbm, table_ref, first, n_pieces, dst, sem):
    def body(p, _):
        _piece_copy(src_hbm, table_ref[first + p], dst, p, sem).start()
        return 0
    lax.fori_loop(0, n_pieces, body, 0)


def _wait_pieces(src_hbm, dst, sem):
    pltpu.make_async_copy(src_hbm.at[pl.ds(0, dst.shape[0]), :], dst, sem).wait()


def _experts_kernel(be_ref, piece_ref, nused_ref, xs_hbm, wg_ref, wu_ref, wd_ref, ys_ref, xs_buf, sem):
    i = pl.program_id(0)
    n_used = nused_ref[0]
    slot = i % 2
    per_block = ROW_BLOCK // PIECE

    @pl.when(i == 0)
    def _():
        _issue_pieces(xs_hbm, piece_ref, 0, per_block, xs_buf.at[0], sem.at[0])

    @pl.when(i < n_used)
    def _():
        _wait_pieces(xs_hbm, xs_buf.at[slot], sem.at[slot])

        @pl.when(i + 1 < n_used)
        def _():
            _issue_pieces(xs_hbm, piece_ref, (i + 1) * per_block, per_block, xs_buf.at[1 - slot], sem.at[1 - slot])

        xb = xs_buf[slot]
        gate = _dot(xb, wg_ref[...])
        up = _dot(xb, wu_ref[...])
        hid = (gate * _sigmoid(gate) * up).astype(BF16)
        ys_ref[...] = _dot(hid, wd_ref[...]).astype(ys_ref.dtype)

    @pl.when(i >= n_used)
    def _():
        ys_ref[...] = jnp.zeros(ys_ref.shape, ys_ref.dtype)


def _experts(block_e, piece_src, n_used, xs, wg, wu, wd, n_blocks):
    d = xs.shape[1]
    by_expert = lambda i, be, ps, nu: (be[i], 0, 0)
    grid_spec = pltpu.PrefetchScalarGridSpec(
        num_scalar_prefetch=3,
        grid=(n_blocks,),
        in_specs=[pl.BlockSpec(memory_space=pl.ANY),
                  pl.BlockSpec((None, d, EXPERT_FF), by_expert),
                  pl.BlockSpec((None, d, EXPERT_FF), by_expert),
                  pl.BlockSpec((None, EXPERT_FF, d), by_expert)],
        out_specs=pl.BlockSpec((ROW_BLOCK, d), lambda i, be, ps, nu: (i, 0)),
        scratch_shapes=[pltpu.VMEM((2, ROW_BLOCK, d), BF16), pltpu.SemaphoreType.DMA((2,))],
    )
    return pl.pallas_call(
        _experts_kernel,
        grid_spec=grid_spec,
        out_shape=jax.ShapeDtypeStruct((n_blocks * ROW_BLOCK, d), BF16),
        compiler_params=pltpu.CompilerParams(
            dimension_semantics=("arbitrary",), vmem_limit_bytes=VMEM_LIMIT),
        name="experts",
    )(block_e, piece_src, n_used, xs, wg, wu, wd)


def _combine_kernel(piece_ref, ys_hbm, x1_ref, route_ref, mod_ref, ln2_ref, o_ref, buf, sem):
    i = pl.program_id(0)
    n = pl.num_programs(0)
    slot = i % 2
    rows = buf.shape[1]
    per_tile = rows // PIECE

    @pl.when(i == 0)
    def _():
        _issue_pieces(ys_hbm, piece_ref, 0, per_tile, buf.at[0], sem.at[0])

    _wait_pieces(ys_hbm, buf.at[slot], sem.at[slot])

    @pl.when(i + 1 < n)
    def _():
        _issue_pieces(ys_hbm, piece_ref, (i + 1) * per_tile, per_tile, buf.at[1 - slot], sem.at[1 - slot])

    tm = x1_ref.shape[0]
    lane = lax.broadcasted_iota(jnp.int32, (tm, rows), 1).astype(F32)
    sel = (jnp.where(lane == route_ref[:, 4:5], route_ref[:, 2:3], 0.0)
           + jnp.where(lane == route_ref[:, 5:6], route_ref[:, 3:4], 0.0)).astype(BF16)
    moe = _dot(sel, buf[slot])
    z = ALPHA * x1_ref[...] + mod_ref[0:1, :] * moe
    o_ref[...] = _ln(z) * ln2_ref[0:1, :] + ln2_ref[1:2, :]


def _combine(piece_glob, ys, x1, route, mod, ln2, tm, tiles_per_batch):
    t, d = x1.shape
    grid_spec = pltpu.PrefetchScalarGridSpec(
        num_scalar_prefetch=1,
        grid=(t // tm,),
        in_specs=[pl.BlockSpec(memory_space=pl.ANY),
                  pl.BlockSpec((tm, d), lambda i, pg: (i, 0)),
                  pl.BlockSpec((tm, ROUTE_LANES), lambda i, pg: (i, 0)),
                  pl.BlockSpec((None, 8, d), lambda i, pg: (i // tiles_per_batch, 0, 0)),
                  pl.BlockSpec((2, d), lambda i, pg: (0, 0))],
        out_specs=pl.BlockSpec((tm, d), lambda i, pg: (i, 0)),
        scratch_shapes=[pltpu.VMEM((2, _local_rows(tm), d), BF16), pltpu.SemaphoreType.DMA((2,))],
    )
    return pl.pallas_call(
        _combine_kernel,
        grid_spec=grid_spec,
        out_shape=jax.ShapeDtypeStruct((t, d), F32),
        compiler_params=pltpu.CompilerParams(
            dimension_semantics=("arbitrary",), vmem_limit_bytes=VMEM_LIMIT),
        name="combine",
    )(piece_glob, ys, x1, route, mod, ln2)


def _sincos_2d(rows, cols, dim):
    q = dim // 4
    omega = 1.0 / (POS_BASE ** (jnp.arange(q, dtype=F32) / q))
    r = jnp.arange(rows, dtype=F32)[:, None] * omega
    cl = jnp.arange(cols, dtype=F32)[:, None] * omega
    r_emb = jnp.concatenate([jnp.sin(r), jnp.cos(r)], -1)
    c_emb = jnp.concatenate([jnp.sin(cl), jnp.cos(cl)], -1)
    emb = jnp.concatenate([jnp.broadcast_to(r_emb[:, None, :], (rows, cols, 2 * q)),
                           jnp.broadcast_to(c_emb[None, :, :], (rows, cols, 2 * q))], -1)
    return emb.reshape(rows * cols, dim)


def _routing_tables(run_pieces, xs_rows, n_blocks):
    i32 = jnp.int32
    n_tiles = run_pieces.shape[0]
    ppb = ROW_BLOCK // PIECE
    loc_start = jnp.cumsum(run_pieces, axis=1) - run_pieces
    seg_tot = jnp.sum(run_pieces, axis=0)
    seg_pad = (seg_tot + ppb - 1) // ppb * ppb
    seg_end = jnp.cumsum(seg_pad)
    seg_start = seg_end - seg_pad
    run_t = run_pieces.T
    glob_start = seg_start[:, None] + jnp.cumsum(run_t, axis=1) - run_t
    n_used = (seg_end[-1] // ppb).astype(i32)
    blk = jnp.minimum(jnp.arange(n_blocks, dtype=i32), n_used - 1)
    block_e = jnp.minimum(jnp.sum((seg_end[None, :] <= (blk * ppb)[:, None]).astype(i32), axis=1),
                          N_EXPERTS - 1).astype(i32)
    starts = glob_start.reshape(-1)
    lens = run_t.reshape(-1)
    p = jnp.arange(n_blocks * ppb, dtype=i32)
    run_id = jnp.sum((starts[None, :] <= p[:, None]).astype(i32), axis=1) - 1
    within = p - starts[run_id]
    tile_of = run_id % n_tiles
    src_piece = tile_of * (xs_rows // PIECE) + loc_start.T.reshape(-1)[run_id] + within
    piece_src = jnp.where(within < lens[run_id], src_piece * PIECE, 0).astype(i32)
    s = jnp.arange(xs_rows // PIECE, dtype=i32)
    e_of = jnp.sum((loc_start[:, None, :] <= s[None, :, None]).astype(i32), axis=2) - 1
    loc_within = s[None, :] - jnp.take_along_axis(loc_start, e_of, axis=1)
    g_piece = jnp.take_along_axis(glob_start.T, e_of, axis=1) + loc_within
    valid = loc_within < jnp.take_along_axis(run_pieces, e_of, axis=1)
    piece_glob = jnp.where(valid, g_piece * PIECE, 0).astype(i32).reshape(-1)
    return block_e, piece_src, piece_glob, n_used.reshape(1)


def kernel(x, c, ctx, c_ctx, w_ada, b_ada, w_in, s5_log_dt_f, s5_a_re_f, s5_a_im_f, s5_b_re_f, s5_b_im_f, s5_c_re_f, s5_c_im_f, s5_log_dt_b, s5_a_re_b, s5_a_im_b, s5_b_re_b, s5_b_im_b, s5_c_re_b, s5_c_im_b, s5_d, s5_w_glu_val, s5_w_glu_gate, conv_w, conv_w_out, w_o, ln1_g, ln1_b, router_w_group, router_b_group, router_w_expert, router_b_expert, exp_w_gate, exp_w_up, exp_w_down, ln2_g, ln2_b):
    b, l, d = x.shape
    lc = ctx.shape[1]
    assert d == D_MODEL and b < SUBLANES and w_ada.shape[0] == DEPTH
    assert l % (SUBLANES * CHUNK) == 0 and lc % (SUBLANES * CHUNK) == 0 and l % GRID_W == 0
    t = b * l
    tm = min(512, l)
    tmc = min(512, lc)

    cc = jnp.concatenate([c, c_ctx[None, :], jnp.zeros((8 - b - 1, d), F32)], 0)
    mods = _mods(cc, w_ada[0], b_ada[0])
    sh1, sc1, g1, sh2, sc2, g2 = jnp.split(mods, 6, axis=-1)
    mod_a = jnp.stack([sh1[:b], 1.0 + sc1[:b]], 1)
    mod_ctx = jnp.broadcast_to(jnp.stack([sh1[b], 1.0 + sc1[b]], 0)[None], (b, 2, d))
    mod_c = jnp.stack([g1[:b], sh2[:b], 1.0 + sc2[:b], jnp.zeros((b, d), F32)], 1)
    mod_f = jnp.concatenate([g2[:b, None, :], jnp.zeros((b, 7, d), F32)], 1)

    w_in_bf = w_in[0].astype(BF16)
    f_tab = _s5_dir_tables(s5_log_dt_f[0], s5_a_re_f[0], s5_a_im_f[0], s5_b_re_f[0], s5_b_im_f[0],
                           s5_c_re_f[0], s5_c_im_f[0])
    b_tab = _s5_dir_tables(s5_log_dt_b[0], s5_a_re_b[0], s5_a_im_b[0], s5_b_re_b[0], s5_b_im_b[0],
                           s5_c_re_b[0], s5_c_im_b[0])
    mi, ws, wo_s5, tab = _s5_operators(f_tab, b_tab, s5_d[0])

    (uc_ctx,) = _in_proj(ctx, jnp.zeros((lc, d), F32), mod_ctx, w_in_bf, None, None, tmc, False)
    zero_state = jnp.zeros((N_PAIRS, 4, SUBLANES, LANES), F32)
    _, s0 = _s5_scan(uc_ctx, mi, ws, wo_s5, tab, zero_state, b)

    pos = _sincos_2d(l // GRID_W, GRID_W, d)
    uc, siga, sb = _in_proj(x, pos, mod_a, w_in_bf, conv_w[0], conv_w_out[0].astype(BF16), tm, True)
    y, _ = _s5_scan(uc, mi, ws, wo_s5, tab, s0, b)

    wr = jnp.concatenate([router_w_group[0], router_w_expert[0],
                          jnp.zeros((d, ROUTE_LANES - N_EXPERT_GROUPS - N_EXPERTS), F32)], 1)
    wr = wr.T
    wr_hi = wr.astype(BF16)
    wr_lo = (wr - wr_hi.astype(F32)).astype(BF16)
    br = jnp.concatenate([router_b_group[0], router_b_expert[0],
                          jnp.zeros((ROUTE_LANES - N_EXPERT_GROUPS - N_EXPERTS,), F32)])[:, None]
    ln1 = jnp.stack([ln1_g[0], ln1_b[0]], 0)
    x1, xs, route, run_len = _mix_out(y, siga, sb, x, pos, mod_c,
                                      s5_w_glu_val[0].astype(BF16), s5_w_glu_gate[0].astype(BF16),
                                      w_o[0].astype(BF16), ln1, wr_hi, wr_lo, br, tm)

    x1 = x1.reshape(t, d)
    route = route.reshape(t, ROUTE_LANES)
    n_tiles = t // tm
    xs_rows = _local_rows(tm)
    run_pieces = run_len[:, :N_EXPERTS, 0].astype(jnp.int32)
    max_rows = t * TOP_K + n_tiles * N_EXPERTS * (PIECE - 1) + N_EXPERTS * (ROW_BLOCK - 1)
    n_blocks = -(-max_rows // ROW_BLOCK)
    block_e, piece_src, piece_glob, n_used = _routing_tables(run_pieces, xs_rows, n_blocks)
    ys = _experts(block_e, piece_src, n_used, xs, exp_w_gate[0].astype(BF16), exp_w_up[0].astype(BF16),
                  exp_w_down[0].astype(BF16), n_blocks)
    ln2 = jnp.stack([ln2_g[0], ln2_b[0]], 0)
    out = _combine(piece_glob, ys, x1, route, mod_f, ln2, tm, l // tm)
    return out.reshape(b, l, d)
```

```python
import functools
import math

import jax
import jax.numpy as jnp
import numpy as np
from jax import lax
from jax.experimental import pallas as pl
from jax.experimental.pallas import tpu as pltpu

F32 = jnp.float32
BF16 = jnp.bfloat16
HI = lax.Precision.HIGHEST

D_MODEL = 1024
GRID_W = 64
S5_WIDTH = 512
S5_GROUP_CH = 16
S5_GROUPS = S5_WIDTH // S5_GROUP_CH
S5_STATE = 64
CONV_WIDTH = 512
N_EXPERT_GROUPS = 4
EXPERTS_PER_GROUP = 8
N_EXPERTS = N_EXPERT_GROUPS * EXPERTS_PER_GROUP
EXPERT_FF = 512
TOP_K = 2
DEPTH = 1
ALPHA = (2.0 * DEPTH) ** 0.25
LN_EPS = 1e-6
POS_BASE = 10000.0

LANES = 128
SUBLANES = 8
CHUNK = 16
GROUP_W = CHUNK * S5_GROUP_CH
PAIR_W = 2 * GROUP_W
N_PAIRS = S5_GROUPS // 2
TOK_PER_VREG = LANES // S5_GROUP_CH
TAB_ROWS = 24
ROUTE_LANES = 128
ROW_BLOCK = 256
PIECE = 16
VMEM_LIMIT = 56 * 1024 * 1024


def _ln(x):
    mu = jnp.mean(x, axis=-1, keepdims=True)
    xc = x - mu
    var = jnp.mean(xc * xc, axis=-1, keepdims=True)
    return xc * lax.rsqrt(var + LN_EPS)


def _sigmoid(x):
    return 1.0 / (1.0 + jnp.exp(-x))


def _dot(a, b):
    return jnp.dot(a, b, preferred_element_type=F32)


def _mods_kernel(c_ref, w_ref, b_ref, o_ref):
    c = c_ref[...]
    a = c * _sigmoid(c)
    o_ref[...] = jnp.dot(a, w_ref[...], precision=HI, preferred_element_type=F32) + b_ref[...]


def _mods(cc, w_ada, b_ada):
    n = w_ada.shape[1]
    nb = 1536
    return pl.pallas_call(
        _mods_kernel,
        grid=(n // nb,),
        in_specs=[pl.BlockSpec((8, D_MODEL), lambda i: (0, 0)),
                  pl.BlockSpec((D_MODEL, nb), lambda i: (0, i)),
                  pl.BlockSpec((1, nb), lambda i: (0, i))],
        out_specs=pl.BlockSpec((8, nb), lambda i: (0, i)),
        out_shape=jax.ShapeDtypeStruct((8, n), F32),
        compiler_params=pltpu.CompilerParams(vmem_limit_bytes=VMEM_LIMIT),
        name="mods",
    )(cc, w_ada, b_ada.reshape(1, n))


def _slot_masks(rows):
    slot = lax.broadcasted_iota(jnp.int32, (rows, LANES), 1) // S5_GROUP_CH
    return [slot == s for s in range(TOK_PER_VREG)]


def _to_chunk_tile(u_scr, uc_ref):
    nch = uc_ref.shape[0]
    masks = _slot_masks(nch)
    for qh in range(CHUNK // TOK_PER_VREG):
        for v in range(S5_WIDTH // LANES):
            src = [u_scr[v, pl.ds(qh * TOK_PER_VREG + s, nch, stride=CHUNK), :] for s in range(TOK_PER_VREG)]
            for i in range(TOK_PER_VREG):
                acc = None
                for s in range(TOK_PER_VREG):
                    shift = ((s - i) * S5_GROUP_CH) % LANES
                    piece = pltpu.roll(src[s], shift, 1) if shift else src[s]
                    acc = piece if acc is None else jnp.where(masks[s], piece, acc)
                lo = (v * TOK_PER_VREG + i) * GROUP_W + qh * LANES
                uc_ref[:, lo:lo + LANES] = acc.astype(uc_ref.dtype)


def _from_chunk_tile(yc_ref, y_scr):
    nch = yc_ref.shape[0]
    masks = _slot_masks(nch)
    for qh in range(CHUNK // TOK_PER_VREG):
        for v in range(S5_WIDTH // LANES):
            src = []
            for i in range(TOK_PER_VREG):
                lo = (v * TOK_PER_VREG + i) * GROUP_W + qh * LANES
                src.append(yc_ref[:, lo:lo + LANES].astype(F32))
            for s in range(TOK_PER_VREG):
                acc = None
                for i in range(TOK_PER_VREG):
                    shift = ((i - s) * S5_GROUP_CH) % LANES
                    piece = pltpu.roll(src[i], shift, 1) if shift else src[i]
                    acc = piece if acc is None else jnp.where(masks[i], piece, acc)
                y_scr[v, pl.ds(qh * TOK_PER_VREG + s, nch, stride=CHUNK), :] = acc


def _in_proj_kernel(x_ref, pos_ref, mod_ref, w_ref, *rest, full):
    if full:
        cw_ref, cwo_ref, uc_ref, siga_ref, sb_ref, u_scr = rest
    else:
        uc_ref, u_scr = rest
    xp = x_ref[...] + pos_ref[...]
    h = (_ln(xp) * mod_ref[1:2, :] + mod_ref[0:1, :]).astype(BF16)
    o1, o2, o3, o4, o5 = 512, 1024, 1536, 2048, 3072
    u = _dot(h, w_ref[:, 0:o1])
    for v in range(S5_WIDTH // LANES):
        u_scr[v] = u[:, v * LANES:(v + 1) * LANES]
    _to_chunk_tile(u_scr, uc_ref)
    if not full:
        return
    z_b = _dot(h, w_ref[:, o1:o2])
    gate_c = _dot(h, w_ref[:, o3:o4])
    p = gate_c * z_b
    tm = p.shape[0]
    col = lax.broadcasted_iota(jnp.int32, (tm, 1), 0) % GRID_W
    prev = jnp.where(col == 0, 0.0, pltpu.roll(p, 1, 0))
    nxt = jnp.where(col == GRID_W - 1, 0.0, pltpu.roll(p, tm - 1, 0))
    v = cw_ref[0:1, :] * prev + cw_ref[1:2, :] * p + cw_ref[2:3, :] * nxt
    gate_b = _dot(h, w_ref[:, o2:o3])
    out_b = _dot((gate_b * v).astype(BF16), cwo_ref[...])
    merge_b = _dot(h, w_ref[:, o5:])
    sb_ref[...] = (_sigmoid(merge_b) * out_b).astype(sb_ref.dtype)
    merge_a = _dot(h, w_ref[:, o4:o5])
    siga_ref[...] = _sigmoid(merge_a).astype(siga_ref.dtype)


def _in_proj(x, pos, mod, w_in_bf, conv_w, conv_w_out_bf, tm, full):
    b, l, d = x.shape
    n_tiles = l // tm
    grid = (n_tiles, b)
    tok = lambda w: pl.BlockSpec((None, tm, w), lambda i, j: (j, i, 0))
    chunk_spec = pl.BlockSpec((tm // CHUNK, CHUNK * S5_WIDTH), lambda i, j: (j * n_tiles + i, 0))
    chunk_shape = jax.ShapeDtypeStruct((b * l // CHUNK, CHUNK * S5_WIDTH), BF16)
    in_specs = [tok(d),
                pl.BlockSpec((tm, d), lambda i, j: (i, 0)),
                pl.BlockSpec((None, 2, d), lambda i, j: (j, 0, 0))]
    args = [x, pos, mod]
    if full:
        in_specs += [pl.BlockSpec(w_in_bf.shape, lambda i, j: (0, 0)),
                     pl.BlockSpec(conv_w.shape, lambda i, j: (0, 0)),
                     pl.BlockSpec(conv_w_out_bf.shape, lambda i, j: (0, 0))]
        args += [w_in_bf, conv_w, conv_w_out_bf]
        out_specs = [chunk_spec, tok(d), tok(d)]
        out_shape = [chunk_shape,
                     jax.ShapeDtypeStruct((b, l, d), BF16),
                     jax.ShapeDtypeStruct((b, l, d), BF16)]
    else:
        in_specs += [pl.BlockSpec((d, S5_WIDTH), lambda i, j: (0, 0))]
        args += [w_in_bf]
        out_specs = [chunk_spec]
        out_shape = [chunk_shape]
    return pl.pallas_call(
        functools.partial(_in_proj_kernel, full=full),
        grid=grid, in_specs=in_specs, out_specs=out_specs, out_shape=out_shape,
        scratch_shapes=[pltpu.VMEM((S5_WIDTH // LANES, tm, LANES), F32)],
        compiler_params=pltpu.CompilerParams(
            dimension_semantics=("arbitrary", "arbitrary"), vmem_limit_bytes=VMEM_LIMIT),
        name="in_proj" if full else "in_proj_ctx",
    )(*args)


def _s5_dir_tables(log_dt, a_re, a_im, b_re, b_im, c_re, c_im):
    f32 = F32
    dt = jnp.exp(log_dt.astype(f32))[:, None]
    a_re = a_re.astype(f32)
    a_im = a_im.astype(f32)
    mag = jnp.exp(dt * a_re)
    ab_re = mag * jnp.cos(dt * a_im)
    ab_im = mag * jnp.sin(dt * a_im)
    den = a_re * a_re + a_im * a_im
    x_re = ab_re - 1.0
    f_re = (x_re * a_re + ab_im * a_im) / den
    f_im = (ab_im * a_re - x_re * a_im) / den
    b_re = b_re.astype(f32)
    b_im = b_im.astype(f32)
    bb_re = f_re[..., None] * b_re - f_im[..., None] * b_im
    bb_im = f_re[..., None] * b_im + f_im[..., None] * b_re
    k = jnp.arange(CHUNK + 1, dtype=f32)[:, None, None]
    pmag = jnp.exp(k * (dt * a_re)[None])
    p_re = pmag * jnp.cos(k * (dt * a_im)[None])
    p_im = pmag * jnp.sin(k * (dt * a_im)[None])
    pb_re = p_re[..., None] * bb_re[None] - p_im[..., None] * bb_im[None]
    pb_im = p_re[..., None] * bb_im[None] + p_im[..., None] * bb_re[None]
    c_re = c_re.astype(f32)
    c_im = c_im.astype(f32)
    kern = (jnp.einsum('gdn,kgnc->kgcd', c_re, pb_re, precision=HI)
            - jnp.einsum('gdn,kgnc->kgcd', c_im, pb_im, precision=HI))
    cp_re = c_re[None] * p_re[:, :, None, :] - c_im[None] * p_im[:, :, None, :]
    cp_im = -(c_re[None] * p_im[:, :, None, :] + c_im[None] * p_re[:, :, None, :])
    return dict(p_re=p_re, p_im=p_im, pb_re=pb_re, pb_im=pb_im, kern=kern, cp_re=cp_re, cp_im=cp_im)


def _s5_operators(f, bk, s5_d):
    q = CHUNK
    g, n, c = S5_GROUPS, S5_STATE, S5_GROUP_CH
    k0 = f['kern'][0] + bk['kern'][0] + s5_d.astype(F32)[:, :, None] * jnp.eye(c, dtype=F32)[None]
    kc = jnp.concatenate([bk['kern'][1:q][::-1], k0[None], f['kern'][1:q]], 0)
    qi = np.arange(q)
    toeplitz = np.eye(2 * q - 1, dtype=np.float32)[qi[None, :] - qi[:, None] + q - 1]
    m_intra = jnp.einsum('iok,kgcd->gicod', toeplitz, kc, precision=HI).reshape(g, q * c, q * c)
    wf_re = f['pb_re'][:q][::-1]
    wf_im = f['pb_im'][:q][::-1]
    wb_re = bk['pb_re'][:q]
    wb_im = bk['pb_im'][:q]
    w_st = jnp.stack([wf_re, wf_im, wb_re, wb_im], 0)
    w_st = w_st.transpose(2, 1, 4, 0, 3).reshape(g, q * c, 4, n)
    of_re = f['cp_re'][1:]
    of_im = f['cp_im'][1:]
    ob_re = bk['cp_re'][1:][::-1]
    ob_im = bk['cp_im'][1:][::-1]
    w_out = jnp.stack([of_re, of_im, ob_re, ob_im], 0)
    w_out = w_out.transpose(2, 0, 4, 1, 3).reshape(g, 4, n, q * c)
    np_ = N_PAIRS
    eye2 = jnp.eye(2, dtype=F32)
    mi = m_intra.reshape(np_, 2, q * c, q * c)
    mi_pair = (mi[:, :, :, None, :] * eye2[None, :, None, :, None]).reshape(np_, PAIR_W, PAIR_W)
    ws = w_st.reshape(np_, 2, q * c, 4, n)
    ws_pair = (ws[:, :, :, :, None, :] * eye2[None, :, None, None, :, None]).reshape(np_, PAIR_W, 4 * 2 * n)
    wo = w_out.reshape(np_, 2, 4, n, q * c).transpose(0, 2, 1, 3, 4)
    wo_pair = (wo[:, :, :, :, None, :] * eye2[None, None, :, None, :, None]).reshape(np_, 4 * 2 * n, PAIR_W)
    tab = jnp.concatenate([_chunk_power_table(f, False), _chunk_power_table(bk, True)], 0)
    tab = tab.reshape(2 * TAB_ROWS, np_, 2 * n).transpose(1, 0, 2)
    return mi_pair.astype(BF16), ws_pair.astype(BF16), wo_pair.astype(BF16), tab


def _chunk_power_table(t, backward):
    def cmul(x, y):
        return x[0] * y[0] - x[1] * y[1], x[0] * y[1] + x[1] * y[0]
    p1 = (t['p_re'][CHUNK], t['p_im'][CHUNK])
    p2 = cmul(p1, p1)
    p4 = cmul(p2, p2)
    p8 = cmul(p4, p4)
    pr = [(jnp.ones_like(p1[0]), jnp.zeros_like(p1[0]))]
    for _ in range(SUBLANES - 1):
        pr.append(cmul(pr[-1], p1))
    if backward:
        pr = pr[::-1]
    rows = [p[0] for p in pr] + [p[1] for p in pr]
    for p in (p1, p2, p4, p8):
        rows += [p[0], p[1]]
    return jnp.stack(rows, 0)


def _s5_scan_kernel(uc_ref, mi_ref, ws_ref, wo_ref, tab_ref, s0_ref, y_ref, fin_ref, s_scr, *, batch):
    rows = uc_ref.shape[0]
    chunks = rows // batch
    n_tiles = chunks // SUBLANES
    u = uc_ref[...]
    s_scr[...] = _dot(u, ws_ref[...])
    row = lax.broadcasted_iota(jnp.int32, (SUBLANES, LANES), 0)

    def tile_scan(r0, backward, c_re, c_im):
        base = TAB_ROWS if backward else 0
        col = 2 * LANES if backward else 0
        rs = pl.ds(r0, SUBLANES)

        def shift(z, k):
            if backward:
                return jnp.where(row < SUBLANES - k, pltpu.roll(z, SUBLANES - k, 0), 0.0)
            return jnp.where(row >= k, pltpu.roll(z, k, 0), 0.0)

        z_re = s_scr[rs, col:col + LANES]
        z_im = s_scr[rs, col + LANES:col + 2 * LANES]
        for k, t in ((1, 16), (2, 18), (4, 20)):
            a_re = tab_ref[base + t:base + t + 1, :]
            a_im = tab_ref[base + t + 1:base + t + 2, :]
            sh_re = shift(z_re, k)
            sh_im = shift(z_im, k)
            z_re, z_im = z_re + (a_re * sh_re - a_im * sh_im), z_im + (a_re * sh_im + a_im * sh_re)
        pr_re = tab_ref[base:base + SUBLANES, :]
        pr_im = tab_ref[base + SUBLANES:base + 2 * SUBLANES, :]
        s_scr[rs, col:col + LANES] = pr_re * c_re - pr_im * c_im + shift(z_re, 1)
        s_scr[rs, col + LANES:col + 2 * LANES] = pr_re * c_im + pr_im * c_re + shift(z_im, 1)
        last = 0 if backward else SUBLANES - 1
        l_re = jnp.broadcast_to(z_re[last:last + 1, :], (SUBLANES, LANES))
        l_im = jnp.broadcast_to(z_im[last:last + 1, :], (SUBLANES, LANES))
        p8_re = tab_ref[base + 22:base + 23, :]
        p8_im = tab_ref[base + 23:base + 24, :]
        return p8_re * c_re - p8_im * c_im + l_re, p8_re * c_im + p8_im * c_re + l_im

    def body(m, carry):
        out = []
        for b in range(batch):
            cf_re, cf_im, cb_re, cb_im = carry[4 * b:4 * b + 4]
            rf = pl.multiple_of(b * chunks + m * SUBLANES, SUBLANES)
            rb = pl.multiple_of(b * chunks + (n_tiles - 1 - m) * SUBLANES, SUBLANES)
            out += list(tile_scan(rf, False, cf_re, cf_im))
            out += list(tile_scan(rb, True, cb_re, cb_im))
        return tuple(out)

    init = tuple(jnp.broadcast_to(s0_ref[t, b:b + 1, :], (SUBLANES, LANES))
                 for b in range(batch) for t in range(4))
    fin = lax.fori_loop(0, n_tiles, body, init)
    fin_ref[...] = jnp.zeros(fin_ref.shape, F32)
    for b in range(batch):
        for t in range(4):
            fin_ref[t, b:b + 1, :] = fin[4 * b + t][0:1, :]
    y = _dot(u, mi_ref[...]) + _dot(s_scr[...].astype(BF16), wo_ref[...])
    y_ref[...] = y.astype(y_ref.dtype)


def _s5_scan(uc, mi, ws, wo, tab, s0, batch):
    rows = uc.shape[0]
    pair = lambda *shape: pl.BlockSpec((None,) + shape, lambda p: (p,) + (0,) * len(shape))
    return pl.pallas_call(
        functools.partial(_s5_scan_kernel, batch=batch),
        grid=(N_PAIRS,),
        in_specs=[pl.BlockSpec((rows, PAIR_W), lambda p: (0, p)),
                  pair(PAIR_W, PAIR_W), pair(PAIR_W, PAIR_W), pair(PAIR_W, PAIR_W),
                  pair(2 * TAB_ROWS, LANES), pair(4, SUBLANES, LANES)],
        out_specs=[pl.BlockSpec((rows, PAIR_W), lambda p: (0, p)), pair(4, SUBLANES, LANES)],
        out_shape=[jax.ShapeDtypeStruct((rows, N_PAIRS * PAIR_W), BF16),
                   jax.ShapeDtypeStruct((N_PAIRS, 4, SUBLANES, LANES), F32)],
        scratch_shapes=[pltpu.VMEM((rows, PAIR_W), F32)],
        compiler_params=pltpu.CompilerParams(
            dimension_semantics=("arbitrary",), vmem_limit_bytes=VMEM_LIMIT),
        name="s5_scan",
    )(uc, mi, ws, wo, tab, s0)


def _mix_out_kernel(y_ref, siga_ref, sb_ref, x_ref, pos_ref, mod_ref, wv_ref, wg_ref, wo_ref,
                    ln1_ref, wrh_ref, wrl_ref, br_ref, ut_ref, lt_ref,
                    x1_ref, xs_ref, route_ref, len_ref, y_scr):
    _from_chunk_tile(y_ref, y_scr)
    y = jnp.concatenate([y_scr[v] for v in range(S5_WIDTH // LANES)], axis=-1)
    ya = (0.5 * y * (1.0 + jnp.tanh(math.sqrt(2.0 / math.pi) * (y + 0.044715 * (y * y * y))))).astype(BF16)
    out_a = _dot(ya, wv_ref[...]) * _sigmoid(_dot(ya, wg_ref[...]))
    merged = siga_ref[...].astype(F32) * out_a + sb_ref[...].astype(F32)
    mix = _dot(merged.astype(BF16), wo_ref[...])
    xp = x_ref[...] + pos_ref[...]
    x1 = _ln(ALPHA * xp + mod_ref[0:1, :] * mix) * ln1_ref[0:1, :] + ln1_ref[1:2, :]
    x1_ref[...] = x1
    h2 = _ln(x1) * mod_ref[2:3, :] + mod_ref[1:2, :]
    _route_and_sort(h2, wrh_ref, wrl_ref, br_ref, ut_ref, lt_ref, xs_ref, route_ref, len_ref)


def _route_and_sort(h2, wrh_ref, wrl_ref, br_ref, ut_ref, lt_ref, xs_ref, route_ref, len_ref):
    tm = h2.shape[0]
    f32 = F32
    h_hi = h2.astype(BF16)
    h_lo = (h2 - h_hi.astype(f32)).astype(BF16)
    nt = (((1,), (1,)), ((), ()))
    lg = (lax.dot_general(wrh_ref[...], h_hi, nt, preferred_element_type=f32)
          + lax.dot_general(wrl_ref[...], h_hi, nt, preferred_element_type=f32)
          + lax.dot_general(wrh_ref[...], h_lo, nt, preferred_element_type=f32)) + br_ref[...]
    rowi = lax.broadcasted_iota(jnp.int32, (ROUTE_LANES, tm), 0).astype(f32)
    neg = jnp.float32(-jnp.inf)
    big = jnp.float32(ROUTE_LANES)
    gl = jnp.where(rowi < N_EXPERT_GROUPS, lg, neg)
    gmax = jnp.max(gl, axis=0, keepdims=True)
    g_idx = jnp.min(jnp.where(gl == gmax, rowi, big), axis=0, keepdims=True)
    p_group = 1.0 / jnp.sum(jnp.exp(gl - gmax), axis=0, keepdims=True)
    e_lo = N_EXPERT_GROUPS + g_idx * EXPERTS_PER_GROUP
    el = jnp.where((rowi >= e_lo) & (rowi < e_lo + EXPERTS_PER_GROUP), lg, neg)
    m1 = jnp.max(el, axis=0, keepdims=True)
    i1 = jnp.min(jnp.where(el == m1, rowi, big), axis=0, keepdims=True)
    el2 = jnp.where(rowi == i1, neg, el)
    m2 = jnp.max(el2, axis=0, keepdims=True)
    i2 = jnp.min(jnp.where(el2 == m2, rowi, big), axis=0, keepdims=True)
    r = jnp.exp(m2 - m1)
    w1 = p_group / (1.0 + r)
    w2 = p_group * r / (1.0 + r)
    e1 = i1 - N_EXPERT_GROUPS
    e2 = i2 - N_EXPERT_GROUPS
    a12 = jnp.where(rowi == e1, 1.0, 0.0) + jnp.where(rowi == e2 + N_EXPERTS, 1.0, 0.0)
    rank = _dot(a12.astype(BF16), ut_ref[...])
    cnt = jnp.broadcast_to(jnp.sum(a12, axis=1, keepdims=True), (ROUTE_LANES, LANES))
    row = lax.broadcasted_iota(jnp.int32, (ROUTE_LANES, LANES), 0)
    tot = cnt + pltpu.roll(cnt, ROUTE_LANES - N_EXPERTS, 0)
    run = jnp.where(row < N_EXPERTS, jnp.floor((tot + (PIECE - 1)) * (1.0 / PIECE)), 0.0)
    off = PIECE * _dot(lt_ref[...], run.astype(BF16))
    base = jnp.where(row < N_EXPERTS, off, pltpu.roll(off + cnt, N_EXPERTS, 0))
    posmat = a12 * (rank + base[:, 0:1])
    pos1 = jnp.sum(posmat[0:N_EXPERTS, :], axis=0, keepdims=True)
    pos2 = jnp.sum(posmat[N_EXPERTS:2 * N_EXPERTS, :], axis=0, keepdims=True)
    ri = lax.broadcasted_iota(jnp.int32, (xs_ref.shape[0], tm), 0).astype(f32)
    perm = jnp.where((ri == pos1) | (ri == pos2), 1.0, 0.0).astype(BF16)
    xs_ref[...] = _dot(perm, h_hi).astype(xs_ref.dtype)
    rec = jnp.where(rowi == 0, e1, jnp.where(rowi == 1, e2, jnp.where(rowi == 2, w1, jnp.where(
        rowi == 3, w2, jnp.where(rowi == 4, pos1, jnp.where(rowi == 5, pos2, 0.0))))))
    route_ref[...] = rec.T
    len_ref[...] = run


def _local_rows(tm):
    return TOP_K * tm + N_EXPERTS * PIECE


def _mix_out(y, siga, sb, x, pos, mod, wv, wg, wo, ln1, wrh, wrl, br, tm):
    b, l, d = x.shape
    n_tiles = l // tm
    xs_rows = _local_rows(tm)
    tok = lambda w: pl.BlockSpec((None, tm, w), lambda i, j: (j, i, 0))
    whole = lambda a: pl.BlockSpec(a.shape, lambda i, j: (0,) * a.ndim)
    ut = (jnp.arange(tm)[:, None] < jnp.arange(tm)[None, :]).astype(BF16)
    lt = (jnp.arange(ROUTE_LANES)[None, :] < jnp.arange(ROUTE_LANES)[:, None]).astype(BF16)
    return pl.pallas_call(
        _mix_out_kernel,
        grid=(n_tiles, b),
        in_specs=[pl.BlockSpec((tm // CHUNK, CHUNK * S5_WIDTH), lambda i, j: (j * n_tiles + i, 0)),
                  tok(d), tok(d), tok(d),
                  pl.BlockSpec((tm, d), lambda i, j: (i, 0)),
                  pl.BlockSpec((None, 4, d), lambda i, j: (j, 0, 0)),
                  whole(wv), whole(wg), whole(wo), whole(ln1), whole(wrh), whole(wrl), whole(br),
                  whole(ut), whole(lt)],
        out_specs=[tok(d),
                   pl.BlockSpec((xs_rows, d), lambda i, j: (j * n_tiles + i, 0)),
                   tok(ROUTE_LANES),
                   pl.BlockSpec((None, ROUTE_LANES, LANES), lambda i, j: (j * n_tiles + i, 0, 0))],
        out_shape=[jax.ShapeDtypeStruct((b, l, d), F32),
                   jax.ShapeDtypeStruct((b * n_tiles * xs_rows, d), BF16),
                   jax.ShapeDtypeStruct((b, l, ROUTE_LANES), F32),
                   jax.ShapeDtypeStruct((b * n_tiles, ROUTE_LANES, LANES), F32)],
        scratch_shapes=[pltpu.VMEM((S5_WIDTH // LANES, tm, LANES), F32)],
        compiler_params=pltpu.CompilerParams(
            dimension_semantics=("arbitrary", "arbitrary"), vmem_limit_bytes=VMEM_LIMIT),
        name="mix_out",
    )(y, siga, sb, x, pos, mod, wv, wg, wo, ln1, wrh, wrl, br, ut, lt)


def _piece_copy(src_hbm, src_row, dst, piece, sem):
    return pltpu.make_async_copy(src_hbm.at[pl.ds(pl.multiple_of(src_row, PIECE), PIECE), :],
                                 dst.at[pl.ds(pl.multiple_of(piece * PIECE, PIECE), PIECE), :], sem)


def _issue_pieces(src_hbm, table_ref, first, n_pieces, dst, sem):
    def body(p, _):
        _piece_copy(src_hbm, table_ref[first + p], dst, p, sem).start()
        return 0
    lax.fori_loop(0, n_pieces, body, 0)


def _wait_pieces(src_hbm, dst, sem):
    pltpu.make_async_copy(src_hbm.at[pl.ds(0, dst.shape[0]), :], dst, sem).wait()


def _experts_kernel(be_ref, piece_ref, nused_ref, xs_hbm, wg_ref, wu_ref, wd_ref, ys_ref,
                    xs_buf, wg_bf, wu_bf, wd_bf, sem):
    i = pl.program_id(0)
    n_used = nused_ref[0]
    slot = i % 2
    per_block = ROW_BLOCK // PIECE

    @pl.when(i == 0)
    def _():
        _issue_pieces(xs_hbm, piece_ref, 0, per_block, xs_buf.at[0], sem.at[0])

    @pl.when(i < n_used)
    def _():
        @pl.when((i == 0) | (be_ref[i] != be_ref[jnp.maximum(i - 1, 0)]))
        def _():
            wg_bf[...] = wg_ref[...].astype(BF16)
            wu_bf[...] = wu_ref[...].astype(BF16)
            wd_bf[...] = wd_ref[...].astype(BF16)

        _wait_pieces(xs_hbm, xs_buf.at[slot], sem.at[slot])

        @pl.when(i + 1 < n_used)
        def _():
            _issue_pieces(xs_hbm, piece_ref, (i + 1) * per_block, per_block, xs_buf.at[1 - slot], sem.at[1 - slot])

        xb = xs_buf[slot]
        gate = _dot(xb, wg_bf[...])
        up = _dot(xb, wu_bf[...])
        hid = (gate * _sigmoid(gate) * up).astype(BF16)
        ys_ref[...] = _dot(hid, wd_bf[...]).astype(ys_ref.dtype)

    @pl.when(i >= n_used)
    def _():
        ys_ref[...] = jnp.zeros(ys_ref.shape, ys_ref.dtype)


def _experts(block_e, piece_src, n_used, xs, wg, wu, wd, n_blocks):
    d = xs.shape[1]
    by_expert = lambda i, be, ps, nu: (0, be[i], 0, 0)
    grid_spec = pltpu.PrefetchScalarGridSpec(
        num_scalar_prefetch=3,
        grid=(n_blocks,),
        in_specs=[pl.BlockSpec(memory_space=pl.ANY),
                  pl.BlockSpec((None, None, d, EXPERT_FF), by_expert),
                  pl.BlockSpec((None, None, d, EXPERT_FF), by_expert),
                  pl.BlockSpec((None, None, EXPERT_FF, d), by_expert)],
        out_specs=pl.BlockSpec((ROW_BLOCK, d), lambda i, be, ps, nu: (i, 0)),
        scratch_shapes=[pltpu.VMEM((2, ROW_BLOCK, d), BF16),
                        pltpu.VMEM((d, EXPERT_FF), BF16), pltpu.VMEM((d, EXPERT_FF), BF16),
                        pltpu.VMEM((EXPERT_FF, d), BF16), pltpu.SemaphoreType.DMA((2,))],
    )
    return pl.pallas_call(
        _experts_kernel,
        grid_spec=grid_spec,
        out_shape=jax.ShapeDtypeStruct((n_blocks * ROW_BLOCK, d), BF16),
        compiler_params=pltpu.CompilerParams(
            dimension_semantics=("arbitrary",), vmem_limit_bytes=VMEM_LIMIT),
        name="experts",
    )(block_e, piece_src, n_used, xs, wg, wu, wd)


def _combine_kernel(piece_ref, ys_hbm, x1_ref, route_ref, mod_ref, ln2_ref, o_ref, buf, sem):
    i = pl.program_id(0)
    n = pl.num_programs(0)
    slot = i % 2
    rows = buf.shape[1]
    per_tile = rows // PIECE

    @pl.when(i == 0)
    def _():
        _issue_pieces(ys_hbm, piece_ref, 0, per_tile, buf.at[0], sem.at[0])

    _wait_pieces(ys_hbm, buf.at[slot], sem.at[slot])

    @pl.when(i + 1 < n)
    def _():
        _issue_pieces(ys_hbm, piece_ref, (i + 1) * per_tile, per_tile, buf.at[1 - slot], sem.at[1 - slot])

    tm = x1_ref.shape[0]
    lane = lax.broadcasted_iota(jnp.int32, (tm, rows), 1).astype(F32)
    sel = (jnp.where(lane == route_ref[:, 4:5], route_ref[:, 2:3], 0.0)
           + jnp.where(lane == route_ref[:, 5:6], route_ref[:, 3:4], 0.0)).astype(BF16)
    moe = _dot(sel, buf[slot])
    z = ALPHA * x1_ref[...] + mod_ref[0:1, :] * moe
    o_ref[...] = _ln(z) * ln2_ref[0:1, :] + ln2_ref[1:2, :]


def _combine(piece_glob, ys, x1, route, mod, ln2, tm, tiles_per_batch):
    t, d = x1.shape
    grid_spec = pltpu.PrefetchScalarGridSpec(
        num_scalar_prefetch=1,
        grid=(t // tm,),
        in_specs=[pl.BlockSpec(memory_space=pl.ANY),
                  pl.BlockSpec((tm, d), lambda i, pg: (i, 0)),
                  pl.BlockSpec((tm, ROUTE_LANES), lambda i, pg: (i, 0)),
                  pl.BlockSpec((None, 8, d), lambda i, pg: (i // tiles_per_batch, 0, 0)),
                  pl.BlockSpec((2, d), lambda i, pg: (0, 0))],
        out_specs=pl.BlockSpec((tm, d), lambda i, pg: (i, 0)),
        scratch_shapes=[pltpu.VMEM((2, _local_rows(tm), d), BF16), pltpu.SemaphoreType.DMA((2,))],
    )
    return pl.pallas_call(
        _combine_kernel,
        grid_spec=grid_spec,
        out_shape=jax.ShapeDtypeStruct((t, d), F32),
        compiler_params=pltpu.CompilerParams(
            dimension_semantics=("arbitrary",), vmem_limit_bytes=VMEM_LIMIT),
        name="combine",
    )(piece_glob, ys, x1, route, mod, ln2)


def _sincos_2d(rows, cols, dim):
    q = dim // 4
    omega = 1.0 / (POS_BASE ** (jnp.arange(q, dtype=F32) / q))
    r = jnp.arange(rows, dtype=F32)[:, None] * omega
    cl = jnp.arange(cols, dtype=F32)[:, None] * omega
    r_emb = jnp.concatenate([jnp.sin(r), jnp.cos(r)], -1)
    c_emb = jnp.concatenate([jnp.sin(cl), jnp.cos(cl)], -1)
    emb = jnp.concatenate([jnp.broadcast_to(r_emb[:, None, :], (rows, cols, 2 * q)),
                           jnp.broadcast_to(c_emb[None, :, :], (rows, cols, 2 * q))], -1)
    return emb.reshape(rows * cols, dim)


def _routing_tables(run_pieces, xs_rows, n_blocks):
    i32 = jnp.int32
    n_tiles = run_pieces.shape[0]
    ppb = ROW_BLOCK // PIECE
    loc_start = jnp.cumsum(run_pieces, axis=1) - run_pieces
    seg_tot = jnp.sum(run_pieces, axis=0)
    seg_pad = (seg_tot + ppb - 1) // ppb * ppb
    seg_end = jnp.cumsum(seg_pad)
    seg_start = seg_end - seg_pad
    run_t = run_pieces.T
    glob_start = seg_start[:, None] + jnp.cumsum(run_t, axis=1) - run_t
    n_used = (seg_end[-1] // ppb).astype(i32)
    blk = jnp.minimum(jnp.arange(n_blocks, dtype=i32), n_used - 1)
    block_e = jnp.minimum(jnp.sum((seg_end[None, :] <= (blk * ppb)[:, None]).astype(i32), axis=1),
                          N_EXPERTS - 1).astype(i32)
    lpt = xs_rows // PIECE
    starts = glob_start.reshape(-1)
    lens = run_t.reshape(-1)
    src0 = (jnp.arange(n_tiles, dtype=i32)[None, :] * lpt + loc_start.T).reshape(-1)
    p = jnp.arange(n_blocks * ppb, dtype=i32)
    within = p[:, None] - starts[None, :]
    hit = (within >= 0) & (within < lens[None, :])
    piece_src = jnp.sum(jnp.where(hit, (src0[None, :] + within) * PIECE, 0), axis=1).astype(i32)
    s = jnp.arange(lpt, dtype=i32)
    loc_within = s[None, :, None] - loc_start[:, None, :]
    hit = (loc_within >= 0) & (loc_within < run_pieces[:, None, :])
    piece_glob = jnp.sum(jnp.where(hit, (glob_start.T[:, None, :] + loc_within) * PIECE, 0), axis=2)
    return block_e, piece_src, piece_glob.astype(i32).reshape(-1), n_used.reshape(1)


def kernel(x, c, ctx, c_ctx, w_ada, b_ada, w_in, s5_log_dt_f, s5_a_re_f, s5_a_im_f, s5_b_re_f, s5_b_im_f, s5_c_re_f, s5_c_im_f, s5_log_dt_b, s5_a_re_b, s5_a_im_b, s5_b_re_b, s5_b_im_b, s5_c_re_b, s5_c_im_b, s5_d, s5_w_glu_val, s5_w_glu_gate, conv_w, conv_w_out, w_o, ln1_g, ln1_b, router_w_group, router_b_group, router_w_expert, router_b_expert, exp_w_gate, exp_w_up, exp_w_down, ln2_g, ln2_b):
    b, l, d = x.shape
    lc = ctx.shape[1]
    assert d == D_MODEL and b < SUBLANES and w_ada.shape[0] == DEPTH
    assert l % (SUBLANES * CHUNK) == 0 and lc % (SUBLANES * CHUNK) == 0 and l % GRID_W == 0
    t = b * l
    tm = min(512, l)
    tmc = min(512, lc)

    cc = jnp.concatenate([c, c_ctx[None, :], jnp.zeros((8 - b - 1, d), F32)], 0)
    mods = _mods(cc, w_ada[0], b_ada[0])
    sh1, sc1, g1, sh2, sc2, g2 = jnp.split(mods, 6, axis=-1)
    mod_a = jnp.stack([sh1[:b], 1.0 + sc1[:b]], 1)
    mod_ctx = jnp.broadcast_to(jnp.stack([sh1[b], 1.0 + sc1[b]], 0)[None], (b, 2, d))
    mod_c = jnp.stack([g1[:b], sh2[:b], 1.0 + sc2[:b], jnp.zeros((b, d), F32)], 1)
    mod_f = jnp.concatenate([g2[:b, None, :], jnp.zeros((b, 7, d), F32)], 1)

    w_in_bf = w_in[0].astype(BF16)
    f_tab = _s5_dir_tables(s5_log_dt_f[0], s5_a_re_f[0], s5_a_im_f[0], s5_b_re_f[0], s5_b_im_f[0],
                           s5_c_re_f[0], s5_c_im_f[0])
    b_tab = _s5_dir_tables(s5_log_dt_b[0], s5_a_re_b[0], s5_a_im_b[0], s5_b_re_b[0], s5_b_im_b[0],
                           s5_c_re_b[0], s5_c_im_b[0])
    mi, ws, wo_s5, tab = _s5_operators(f_tab, b_tab, s5_d[0])

    (uc_ctx,) = _in_proj(ctx, jnp.zeros((lc, d), F32), mod_ctx, w_in_bf, None, None, tmc, False)
    zero_state = jnp.zeros((N_PAIRS, 4, SUBLANES, LANES), F32)
    _, s0 = _s5_scan(uc_ctx, mi, ws, wo_s5, tab, zero_state, b)

    pos = _sincos_2d(l // GRID_W, GRID_W, d)
    uc, siga, sb = _in_proj(x, pos, mod_a, w_in_bf, conv_w[0], conv_w_out[0].astype(BF16), tm, True)
    y, _ = _s5_scan(uc, mi, ws, wo_s5, tab, s0, b)

    wr = jnp.concatenate([router_w_group[0], router_w_expert[0],
                          jnp.zeros((d, ROUTE_LANES - N_EXPERT_GROUPS - N_EXPERTS), F32)], 1)
    wr = wr.T
    wr_hi = wr.astype(BF16)
    wr_lo = (wr - wr_hi.astype(F32)).astype(BF16)
    br = jnp.concatenate([router_b_group[0], router_b_expert[0],
                          jnp.zeros((ROUTE_LANES - N_EXPERT_GROUPS - N_EXPERTS,), F32)])[:, None]
    ln1 = jnp.stack([ln1_g[0], ln1_b[0]], 0)
    x1, xs, route, run_len = _mix_out(y, siga, sb, x, pos, mod_c,
                                      s5_w_glu_val[0].astype(BF16), s5_w_glu_gate[0].astype(BF16),
                                      w_o[0].astype(BF16), ln1, wr_hi, wr_lo, br, tm)

    x1 = x1.reshape(t, d)
    route = route.reshape(t, ROUTE_LANES)
    n_tiles = t // tm
    xs_rows = _local_rows(tm)
    run_pieces = run_len[:, :N_EXPERTS, 0].astype(jnp.int32)
    max_rows = t * TOP_K + n_tiles * N_EXPERTS * (PIECE - 1) + N_EXPERTS * (ROW_BLOCK - 1)
    n_blocks = -(-max_rows // ROW_BLOCK)
    block_e, piece_src, piece_glob, n_used = _routing_tables(run_pieces, xs_rows, n_blocks)
    ys = _experts(block_e, piece_src, n_used, xs, exp_w_gate, exp_w_up, exp_w_down, n_blocks)
    ln2 = jnp.stack([ln2_g[0], ln2_b[0]], 0)
    out = _combine(piece_glob, ys, x1, route, mod_f, ln2, tm, l // tm)
    return out.reshape(b, l, d)
```

```python
import functools
import math

import jax
import jax.numpy as jnp
import numpy as np
from jax import lax
from jax.experimental import pallas as pl
from jax.experimental.pallas import tpu as pltpu

F32 = jnp.float32
BF16 = jnp.bfloat16
HI = lax.Precision.HIGHEST

D_MODEL = 1024
GRID_W = 64
S5_WIDTH = 512
S5_GROUP_CH = 16
S5_GROUPS = S5_WIDTH // S5_GROUP_CH
S5_STATE = 64
CONV_WIDTH = 512
N_EXPERT_GROUPS = 4
EXPERTS_PER_GROUP = 8
N_EXPERTS = N_EXPERT_GROUPS * EXPERTS_PER_GROUP
EXPERT_FF = 512
TOP_K = 2
DEPTH = 1
ALPHA = (2.0 * DEPTH) ** 0.25
LN_EPS = 1e-6
POS_BASE = 10000.0

LANES = 128
SUBLANES = 8
CHUNK = 16
GROUP_W = CHUNK * S5_GROUP_CH
PAIR_W = 2 * GROUP_W
N_PAIRS = S5_GROUPS // 2
TOK_PER_VREG = LANES // S5_GROUP_CH
TAB_ROWS = 24
ROUTE_LANES = 128
ROW_BLOCK = 512
PIECE = 16
VMEM_LIMIT = 56 * 1024 * 1024


def _ln(x):
    mu = jnp.mean(x, axis=-1, keepdims=True)
    xc = x - mu
    var = jnp.mean(xc * xc, axis=-1, keepdims=True)
    return xc * lax.rsqrt(var + LN_EPS)


def _sigmoid(x):
    return 1.0 / (1.0 + jnp.exp(-x))


def _dot(a, b):
    return jnp.dot(a, b, preferred_element_type=F32)


def _mods_kernel(c_ref, w_ref, b_ref, o_ref):
    c = c_ref[...]
    a = c * _sigmoid(c)
    o_ref[...] = jnp.dot(a, w_ref[...], precision=HI, preferred_element_type=F32) + b_ref[...]


def _mods(cc, w_ada, b_ada):
    n = w_ada.shape[1]
    nb = 1536
    return pl.pallas_call(
        _mods_kernel,
        grid=(n // nb,),
        in_specs=[pl.BlockSpec((8, D_MODEL), lambda i: (0, 0)),
                  pl.BlockSpec((D_MODEL, nb), lambda i: (0, i)),
                  pl.BlockSpec((1, nb), lambda i: (0, i))],
        out_specs=pl.BlockSpec((8, nb), lambda i: (0, i)),
        out_shape=jax.ShapeDtypeStruct((8, n), F32),
        compiler_params=pltpu.CompilerParams(vmem_limit_bytes=VMEM_LIMIT),
        name="mods",
    )(cc, w_ada, b_ada.reshape(1, n))


def _slot_masks(rows):
    slot = lax.broadcasted_iota(jnp.int32, (rows, LANES), 1) // S5_GROUP_CH
    return [slot == s for s in range(TOK_PER_VREG)]


def _to_chunk_tile(u_scr, uc_ref):
    nch = uc_ref.shape[0]
    masks = _slot_masks(nch)
    for qh in range(CHUNK // TOK_PER_VREG):
        for v in range(S5_WIDTH // LANES):
            src = [u_scr[v, pl.ds(qh * TOK_PER_VREG + s, nch, stride=CHUNK), :] for s in range(TOK_PER_VREG)]
            for i in range(TOK_PER_VREG):
                acc = None
                for s in range(TOK_PER_VREG):
                    shift = ((s - i) * S5_GROUP_CH) % LANES
                    piece = pltpu.roll(src[s], shift, 1) if shift else src[s]
                    acc = piece if acc is None else jnp.where(masks[s], piece, acc)
                lo = (v * TOK_PER_VREG + i) * GROUP_W + qh * LANES
                uc_ref[:, lo:lo + LANES] = acc.astype(uc_ref.dtype)


def _from_chunk_tile(yc_ref, y_scr):
    nch = yc_ref.shape[0]
    masks = _slot_masks(nch)
    for qh in range(CHUNK // TOK_PER_VREG):
        for v in range(S5_WIDTH // LANES):
            src = []
            for i in range(TOK_PER_VREG):
                lo = (v * TOK_PER_VREG + i) * GROUP_W + qh * LANES
                src.append(yc_ref[:, lo:lo + LANES].astype(F32))
            for s in range(TOK_PER_VREG):
                acc = None
                for i in range(TOK_PER_VREG):
                    shift = ((i - s) * S5_GROUP_CH) % LANES
                    piece = pltpu.roll(src[i], shift, 1) if shift else src[i]
                    acc = piece if acc is None else jnp.where(masks[i], piece, acc)
                y_scr[v, pl.ds(qh * TOK_PER_VREG + s, nch, stride=CHUNK), :] = acc


def _in_proj_kernel(x_ref, pos_ref, mod_ref, w_ref, *rest, full):
    if full:
        cw_ref, cwo_ref, uc_ref, siga_ref, sb_ref, u_scr = rest
    else:
        uc_ref, u_scr = rest
    xp = x_ref[...] + pos_ref[...]
    h = (_ln(xp) * mod_ref[1:2, :] + mod_ref[0:1, :]).astype(BF16)
    o1, o2, o3, o4, o5 = 512, 1024, 1536, 2048, 3072
    u = _dot(h, w_ref[:, 0:o1])
    for v in range(S5_WIDTH // LANES):
        u_scr[v] = u[:, v * LANES:(v + 1) * LANES]
    _to_chunk_tile(u_scr, uc_ref)
    if not full:
        return
    z_b = _dot(h, w_ref[:, o1:o2])
    gate_c = _dot(h, w_ref[:, o3:o4])
    p = gate_c * z_b
    tm = p.shape[0]
    col = lax.broadcasted_iota(jnp.int32, (tm, 1), 0) % GRID_W
    prev = jnp.where(col == 0, 0.0, pltpu.roll(p, 1, 0))
    nxt = jnp.where(col == GRID_W - 1, 0.0, pltpu.roll(p, tm - 1, 0))
    v = cw_ref[0:1, :] * prev + cw_ref[1:2, :] * p + cw_ref[2:3, :] * nxt
    gate_b = _dot(h, w_ref[:, o2:o3])
    out_b = _dot((gate_b * v).astype(BF16), cwo_ref[...])
    merge_b = _dot(h, w_ref[:, o5:])
    sb_ref[...] = (_sigmoid(merge_b) * out_b).astype(sb_ref.dtype)
    merge_a = _dot(h, w_ref[:, o4:o5])
    siga_ref[...] = _sigmoid(merge_a).astype(siga_ref.dtype)


def _in_proj(x, pos, mod, w_in_bf, conv_w, conv_w_out_bf, tm, full):
    b, l, d = x.shape
    n_tiles = l // tm
    grid = (n_tiles, b)
    tok = lambda w: pl.BlockSpec((None, tm, w), lambda i, j: (j, i, 0))
    chunk_spec = pl.BlockSpec((tm // CHUNK, CHUNK * S5_WIDTH), lambda i, j: (j * n_tiles + i, 0))
    chunk_shape = jax.ShapeDtypeStruct((b * l // CHUNK, CHUNK * S5_WIDTH), BF16)
    in_specs = [tok(d),
                pl.BlockSpec((tm, d), lambda i, j: (i, 0)),
                pl.BlockSpec((None, 2, d), lambda i, j: (j, 0, 0))]
    args = [x, pos, mod]
    if full:
        in_specs += [pl.BlockSpec(w_in_bf.shape, lambda i, j: (0, 0)),
                     pl.BlockSpec(conv_w.shape, lambda i, j: (0, 0)),
                     pl.BlockSpec(conv_w_out_bf.shape, lambda i, j: (0, 0))]
        args += [w_in_bf, conv_w, conv_w_out_bf]
        out_specs = [chunk_spec, tok(d), tok(d)]
        out_shape = [chunk_shape,
                     jax.ShapeDtypeStruct((b, l, d), BF16),
                     jax.ShapeDtypeStruct((b, l, d), BF16)]
    else:
        in_specs += [pl.BlockSpec((d, S5_WIDTH), lambda i, j: (0, 0))]
        args += [w_in_bf]
        out_specs = [chunk_spec]
        out_shape = [chunk_shape]
    return pl.pallas_call(
        functools.partial(_in_proj_kernel, full=full),
        grid=grid, in_specs=in_specs, out_specs=out_specs, out_shape=out_shape,
        scratch_shapes=[pltpu.VMEM((S5_WIDTH // LANES, tm, LANES), F32)],
        compiler_params=pltpu.CompilerParams(
            dimension_semantics=("arbitrary", "arbitrary"), vmem_limit_bytes=VMEM_LIMIT),
        name="in_proj" if full else "in_proj_ctx",
    )(*args)


def _s5_dir_tables(log_dt, a_re, a_im, b_re, b_im, c_re, c_im):
    f32 = F32
    dt = jnp.exp(log_dt.astype(f32))[:, None]
    a_re = a_re.astype(f32)
    a_im = a_im.astype(f32)
    mag = jnp.exp(dt * a_re)
    ab_re = mag * jnp.cos(dt * a_im)
    ab_im = mag * jnp.sin(dt * a_im)
    den = a_re * a_re + a_im * a_im
    x_re = ab_re - 1.0
    f_re = (x_re * a_re + ab_im * a_im) / den
    f_im = (ab_im * a_re - x_re * a_im) / den
    b_re = b_re.astype(f32)
    b_im = b_im.astype(f32)
    bb_re = f_re[..., None] * b_re - f_im[..., None] * b_im
    bb_im = f_re[..., None] * b_im + f_im[..., None] * b_re
    k = jnp.arange(CHUNK + 1, dtype=f32)[:, None, None]
    pmag = jnp.exp(k * (dt * a_re)[None])
    p_re = pmag * jnp.cos(k * (dt * a_im)[None])
    p_im = pmag * jnp.sin(k * (dt * a_im)[None])
    pb_re = p_re[..., None] * bb_re[None] - p_im[..., None] * bb_im[None]
    pb_im = p_re[..., None] * bb_im[None] + p_im[..., None] * bb_re[None]
    c_re = c_re.astype(f32)
    c_im = c_im.astype(f32)
    kern = (jnp.einsum('gdn,kgnc->kgcd', c_re, pb_re, precision=HI)
            - jnp.einsum('gdn,kgnc->kgcd', c_im, pb_im, precision=HI))
    cp_re = c_re[None] * p_re[:, :, None, :] - c_im[None] * p_im[:, :, None, :]
    cp_im = -(c_re[None] * p_im[:, :, None, :] + c_im[None] * p_re[:, :, None, :])
    return dict(p_re=p_re, p_im=p_im, pb_re=pb_re, pb_im=pb_im, kern=kern, cp_re=cp_re, cp_im=cp_im)


def _s5_operators(f, bk, s5_d):
    q = CHUNK
    g, n, c = S5_GROUPS, S5_STATE, S5_GROUP_CH
    k0 = f['kern'][0] + bk['kern'][0] + s5_d.astype(F32)[:, :, None] * jnp.eye(c, dtype=F32)[None]
    kc = jnp.concatenate([bk['kern'][1:q][::-1], k0[None], f['kern'][1:q]], 0)
    qi = np.arange(q)
    toeplitz = np.eye(2 * q - 1, dtype=np.float32)[qi[None, :] - qi[:, None] + q - 1]
    m_intra = jnp.einsum('iok,kgcd->gicod', toeplitz, kc, precision=HI).reshape(g, q * c, q * c)
    wf_re = f['pb_re'][:q][::-1]
    wf_im = f['pb_im'][:q][::-1]
    wb_re = bk['pb_re'][:q]
    wb_im = bk['pb_im'][:q]
    w_st = jnp.stack([wf_re, wf_im, wb_re, wb_im], 0)
    w_st = w_st.transpose(2, 1, 4, 0, 3).reshape(g, q * c, 4, n)
    of_re = f['cp_re'][1:]
    of_im = f['cp_im'][1:]
    ob_re = bk['cp_re'][1:][::-1]
    ob_im = bk['cp_im'][1:][::-1]
    w_out = jnp.stack([of_re, of_im, ob_re, ob_im], 0)
    w_out = w_out.transpose(2, 0, 4, 1, 3).reshape(g, 4, n, q * c)
    np_ = N_PAIRS
    w_st = w_st.astype(BF16).reshape(np_, 2, q * c, 4, n)
    ws_pair = jnp.concatenate([jnp.pad(w_st[:, 0], ((0, 0), (0, 0), (0, 0), (0, n))),
                               jnp.pad(w_st[:, 1], ((0, 0), (0, 0), (0, 0), (n, 0)))], 1)
    ws_pair = ws_pair.reshape(np_, PAIR_W, 4 * 2 * n)
    w_out = w_out.astype(BF16).reshape(np_, 2, 4, n, q * c)
    wo_pair = jnp.stack([jnp.pad(w_out[:, 0], ((0, 0), (0, 0), (0, 0), (0, q * c))),
                         jnp.pad(w_out[:, 1], ((0, 0), (0, 0), (0, 0), (q * c, 0)))], 2)
    wo_pair = wo_pair.reshape(np_, 4 * 2 * n, PAIR_W)
    tab = jnp.concatenate([_chunk_power_table(f, False), _chunk_power_table(bk, True)], 0)
    tab = tab.reshape(2 * TAB_ROWS, np_, 2 * n).transpose(1, 0, 2)
    return m_intra.astype(BF16), ws_pair, wo_pair, tab


def _chunk_power_table(t, backward):
    def cmul(x, y):
        return x[0] * y[0] - x[1] * y[1], x[0] * y[1] + x[1] * y[0]
    p1 = (t['p_re'][CHUNK], t['p_im'][CHUNK])
    p2 = cmul(p1, p1)
    p4 = cmul(p2, p2)
    p8 = cmul(p4, p4)
    pr = [(jnp.ones_like(p1[0]), jnp.zeros_like(p1[0]))]
    for _ in range(SUBLANES - 1):
        pr.append(cmul(pr[-1], p1))
    if backward:
        pr = pr[::-1]
    rows = [p[0] for p in pr] + [p[1] for p in pr]
    for p in (p1, p2, p4, p8):
        rows += [p[0], p[1]]
    return jnp.stack(rows, 0)


def _s5_scan_kernel(uc_ref, mi_ref, ws_ref, wo_ref, tab_ref, s0_ref, y_ref, fin_ref, s_scr, *, batch):
    rows = uc_ref.shape[0]
    chunks = rows // batch
    n_tiles = chunks // SUBLANES
    u = uc_ref[...]
    s_scr[...] = _dot(u, ws_ref[...])
    row = lax.broadcasted_iota(jnp.int32, (SUBLANES, LANES), 0)

    def tile_scan(r0, backward, c_re, c_im):
        base = TAB_ROWS if backward else 0
        col = 2 * LANES if backward else 0
        rs = pl.ds(r0, SUBLANES)

        def shift(z, k):
            if backward:
                return jnp.where(row < SUBLANES - k, pltpu.roll(z, SUBLANES - k, 0), 0.0)
            return jnp.where(row >= k, pltpu.roll(z, k, 0), 0.0)

        z_re = s_scr[rs, col:col + LANES]
        z_im = s_scr[rs, col + LANES:col + 2 * LANES]
        for k, t in ((1, 16), (2, 18), (4, 20)):
            a_re = tab_ref[base + t:base + t + 1, :]
            a_im = tab_ref[base + t + 1:base + t + 2, :]
            sh_re = shift(z_re, k)
            sh_im = shift(z_im, k)
            z_re, z_im = z_re + (a_re * sh_re - a_im * sh_im), z_im + (a_re * sh_im + a_im * sh_re)
        pr_re = tab_ref[base:base + SUBLANES, :]
        pr_im = tab_ref[base + SUBLANES:base + 2 * SUBLANES, :]
        s_scr[rs, col:col + LANES] = pr_re * c_re - pr_im * c_im + shift(z_re, 1)
        s_scr[rs, col + LANES:col + 2 * LANES] = pr_re * c_im + pr_im * c_re + shift(z_im, 1)
        last = 0 if backward else SUBLANES - 1
        l_re = jnp.broadcast_to(z_re[last:last + 1, :], (SUBLANES, LANES))
        l_im = jnp.broadcast_to(z_im[last:last + 1, :], (SUBLANES, LANES))
        p8_re = tab_ref[base + 22:base + 23, :]
        p8_im = tab_ref[base + 23:base + 24, :]
        return p8_re * c_re - p8_im * c_im + l_re, p8_re * c_im + p8_im * c_re + l_im

    def body(m, carry):
        out = []
        for b in range(batch):
            cf_re, cf_im, cb_re, cb_im = carry[4 * b:4 * b + 4]
            rf = pl.multiple_of(b * chunks + m * SUBLANES, SUBLANES)
            rb = pl.multiple_of(b * chunks + (n_tiles - 1 - m) * SUBLANES, SUBLANES)
            out += list(tile_scan(rf, False, cf_re, cf_im))
            out += list(tile_scan(rb, True, cb_re, cb_im))
        return tuple(out)

    init = tuple(jnp.broadcast_to(s0_ref[t, b:b + 1, :], (SUBLANES, LANES))
                 for b in range(batch) for t in range(4))
    fin = lax.fori_loop(0, n_tiles, body, init, unroll=min(4, n_tiles))
    fin_ref[...] = jnp.zeros(fin_ref.shape, F32)
    for b in range(batch):
        for t in range(4):
            fin_ref[t, b:b + 1, :] = fin[4 * b + t][0:1, :]
    y_intra = jnp.concatenate([_dot(u[:, gl * GROUP_W:(gl + 1) * GROUP_W], mi_ref[gl]) for gl in range(2)], axis=-1)
    y = y_intra + _dot(s_scr[...].astype(BF16), wo_ref[...])
    y_ref[...] = y.astype(y_ref.dtype)


def _s5_scan(uc, mi, ws, wo, tab, s0, batch):
    rows = uc.shape[0]
    pair = lambda *shape: pl.BlockSpec((None,) + shape, lambda p: (p,) + (0,) * len(shape))
    return pl.pallas_call(
        functools.partial(_s5_scan_kernel, batch=batch),
        grid=(N_PAIRS,),
        in_specs=[pl.BlockSpec((rows, PAIR_W), lambda p: (0, p)),
                  pl.BlockSpec((2, GROUP_W, GROUP_W), lambda p: (p, 0, 0)),
                  pair(PAIR_W, PAIR_W), pair(PAIR_W, PAIR_W),
                  pair(2 * TAB_ROWS, LANES), pair(4, SUBLANES, LANES)],
        out_specs=[pl.BlockSpec((rows, PAIR_W), lambda p: (0, p)), pair(4, SUBLANES, LANES)],
        out_shape=[jax.ShapeDtypeStruct((rows, N_PAIRS * PAIR_W), BF16),
                   jax.ShapeDtypeStruct((N_PAIRS, 4, SUBLANES, LANES), F32)],
        scratch_shapes=[pltpu.VMEM((rows, PAIR_W), F32)],
        compiler_params=pltpu.CompilerParams(
            dimension_semantics=("arbitrary",), vmem_limit_bytes=VMEM_LIMIT),
        name="s5_scan",
    )(uc, mi, ws, wo, tab, s0)


def _mix_out_kernel(y_ref, siga_ref, sb_ref, x_ref, pos_ref, mod_ref, wv_ref, wg_ref, wo_ref,
                    ln1_ref, wr_ref, br_ref, ut_ref, lt_ref,
                    x1_ref, xs_ref, route_ref, len_ref, y_scr):
    _from_chunk_tile(y_ref, y_scr)
    y = jnp.concatenate([y_scr[v] for v in range(S5_WIDTH // LANES)], axis=-1)
    ya = (0.5 * y * (1.0 + jnp.tanh(math.sqrt(2.0 / math.pi) * (y + 0.044715 * (y * y * y))))).astype(BF16)
    out_a = _dot(ya, wv_ref[...]) * _sigmoid(_dot(ya, wg_ref[...]))
    merged = siga_ref[...].astype(F32) * out_a + sb_ref[...].astype(F32)
    mix = _dot(merged.astype(BF16), wo_ref[...])
    xp = x_ref[...] + pos_ref[...]
    x1 = _ln(ALPHA * xp + mod_ref[0:1, :] * mix) * ln1_ref[0:1, :] + ln1_ref[1:2, :]
    x1_ref[...] = x1
    h2 = _ln(x1) * mod_ref[2:3, :] + mod_ref[1:2, :]
    _route_and_sort(h2, wr_ref, br_ref, ut_ref, lt_ref, xs_ref, route_ref, len_ref)


def _route_and_sort(h2, wr_ref, br_ref, ut_ref, lt_ref, xs_ref, route_ref, len_ref):
    tm = h2.shape[0]
    f32 = F32
    h_hi = h2.astype(BF16)
    nt = (((1,), (1,)), ((), ()))
    lg = lax.dot_general(wr_ref[...], h_hi, nt, preferred_element_type=f32) + br_ref[...]
    rowi = lax.broadcasted_iota(jnp.int32, (ROUTE_LANES, tm), 0).astype(f32)
    neg = jnp.float32(-jnp.inf)
    big = jnp.float32(ROUTE_LANES)
    gl = jnp.where(rowi < N_EXPERT_GROUPS, lg, neg)
    gmax = jnp.max(gl, axis=0, keepdims=True)
    g_idx = jnp.min(jnp.where(gl == gmax, rowi, big), axis=0, keepdims=True)
    p_group = 1.0 / jnp.sum(jnp.exp(gl - gmax), axis=0, keepdims=True)
    e_lo = N_EXPERT_GROUPS + g_idx * EXPERTS_PER_GROUP
    el = jnp.where((rowi >= e_lo) & (rowi < e_lo + EXPERTS_PER_GROUP), lg, neg)
    m1 = jnp.max(el, axis=0, keepdims=True)
    i1 = jnp.min(jnp.where(el == m1, rowi, big), axis=0, keepdims=True)
    el2 = jnp.where(rowi == i1, neg, el)
    m2 = jnp.max(el2, axis=0, keepdims=True)
    i2 = jnp.min(jnp.where(el2 == m2, rowi, big), axis=0, keepdims=True)
    r = jnp.exp(m2 - m1)
    w1 = p_group / (1.0 + r)
    w2 = p_group * r / (1.0 + r)
    e1 = i1 - N_EXPERT_GROUPS
    e2 = i2 - N_EXPERT_GROUPS
    a12 = jnp.where(rowi == e1, 1.0, 0.0) + jnp.where(rowi == e2 + N_EXPERTS, 1.0, 0.0)
    rank = _dot(a12.astype(BF16), ut_ref[...])
    cnt = jnp.broadcast_to(jnp.sum(a12, axis=1, keepdims=True), (ROUTE_LANES, LANES))
    row = lax.broadcasted_iota(jnp.int32, (ROUTE_LANES, LANES), 0)
    tot = cnt + pltpu.roll(cnt, ROUTE_LANES - N_EXPERTS, 0)
    run = jnp.where(row < N_EXPERTS, jnp.floor((tot + (PIECE - 1)) * (1.0 / PIECE)), 0.0)
    off = PIECE * _dot(lt_ref[...], run.astype(BF16))
    base = jnp.where(row < N_EXPERTS, off, pltpu.roll(off + cnt, N_EXPERTS, 0))
    posmat = a12 * (rank + base[:, 0:1])
    pos1 = jnp.sum(posmat[0:N_EXPERTS, :], axis=0, keepdims=True)
    pos2 = jnp.sum(posmat[N_EXPERTS:2 * N_EXPERTS, :], axis=0, keepdims=True)
    ri = lax.broadcasted_iota(jnp.int32, (xs_ref.shape[0], tm), 0).astype(f32)
    perm = jnp.where((ri == pos1) | (ri == pos2), 1.0, 0.0).astype(BF16)
    xs_ref[...] = _dot(perm, h_hi).astype(xs_ref.dtype)
    rec = jnp.where(rowi == 0, e1, jnp.where(rowi == 1, e2, jnp.where(rowi == 2, w1, jnp.where(
        rowi == 3, w2, jnp.where(rowi == 4, pos1, jnp.where(rowi == 5, pos2, 0.0))))))
    route_ref[...] = rec.T
    len_ref[...] = run


def _local_rows(tm):
    return TOP_K * tm + N_EXPERTS * PIECE


def _mix_out(y, siga, sb, x, pos, mod, wv, wg, wo, ln1, wr, br, tm):
    b, l, d = x.shape
    n_tiles = l // tm
    xs_rows = _local_rows(tm)
    tok = lambda w: pl.BlockSpec((None, tm, w), lambda i, j: (j, i, 0))
    whole = lambda a: pl.BlockSpec(a.shape, lambda i, j: (0,) * a.ndim)
    ut = (jnp.arange(tm)[:, None] < jnp.arange(tm)[None, :]).astype(BF16)
    lt = (jnp.arange(ROUTE_LANES)[None, :] < jnp.arange(ROUTE_LANES)[:, None]).astype(BF16)
    return pl.pallas_call(
        _mix_out_kernel,
        grid=(n_tiles, b),
        in_specs=[pl.BlockSpec((tm // CHUNK, CHUNK * S5_WIDTH), lambda i, j: (j * n_tiles + i, 0)),
                  tok(d), tok(d), tok(d),
                  pl.BlockSpec((tm, d), lambda i, j: (i, 0)),
                  pl.BlockSpec((None, 4, d), lambda i, j: (j, 0, 0)),
                  whole(wv), whole(wg), whole(wo), whole(ln1), whole(wr), whole(br),
                  whole(ut), whole(lt)],
        out_specs=[tok(d),
                   pl.BlockSpec((xs_rows, d), lambda i, j: (j * n_tiles + i, 0)),
                   tok(ROUTE_LANES),
                   pl.BlockSpec((None, ROUTE_LANES, LANES), lambda i, j: (j * n_tiles + i, 0, 0))],
        out_shape=[jax.ShapeDtypeStruct((b, l, d), F32),
                   jax.ShapeDtypeStruct((b * n_tiles * xs_rows, d), BF16),
                   jax.ShapeDtypeStruct((b, l, ROUTE_LANES), F32),
                   jax.ShapeDtypeStruct((b * n_tiles, ROUTE_LANES, LANES), F32)],
        scratch_shapes=[pltpu.VMEM((S5_WIDTH // LANES, tm, LANES), F32)],
        compiler_params=pltpu.CompilerParams(
            dimension_semantics=("arbitrary", "arbitrary"), vmem_limit_bytes=VMEM_LIMIT),
        name="mix_out",
    )(y, siga, sb, x, pos, mod, wv, wg, wo, ln1, wr, br, ut, lt)


def _piece_copy(src_hbm, src_row, dst, piece, sem):
    return pltpu.make_async_copy(src_hbm.at[pl.ds(pl.multiple_of(src_row, PIECE), PIECE), :],
                                 dst.at[pl.ds(pl.multiple_of(piece * PIECE, PIECE), PIECE), :], sem)


def _issue_pieces(src_hbm, table_ref, first, n_pieces, dst, sem):
    def body(p, _):
        _piece_copy(src_hbm, table_ref[first + p], dst, p, sem).start()
        return 0
    lax.fori_loop(0, n_pieces, body, 0)


def _wait_pieces(src_hbm, dst, sem):
    pltpu.make_async_copy(src_hbm.at[pl.ds(0, dst.shape[0]), :], dst, sem).wait()


def _experts_kernel(be_ref, piece_ref, nused_ref, xs_hbm, wg_ref, wu_ref, wd_ref, ys_ref,
                    xs_buf, wg_bf, wu_bf, wd_bf, sem):
    i = pl.program_id(0)
    n_used = nused_ref[0]
    slot = i % 2
    per_block = ROW_BLOCK // PIECE

    @pl.when(i == 0)
    def _():
        _issue_pieces(xs_hbm, piece_ref, 0, per_block, xs_buf.at[0], sem.at[0])

    @pl.when(i < n_used)
    def _():
        @pl.when((i == 0) | (be_ref[i] != be_ref[jnp.maximum(i - 1, 0)]))
        def _():
            wg_bf[...] = wg_ref[...].astype(BF16)
            wu_bf[...] = wu_ref[...].astype(BF16)
            wd_bf[...] = wd_ref[...].astype(BF16)

        _wait_pieces(xs_hbm, xs_buf.at[slot], sem.at[slot])

        @pl.when(i + 1 < n_used)
        def _():
            _issue_pieces(xs_hbm, piece_ref, (i + 1) * per_block, per_block, xs_buf.at[1 - slot], sem.at[1 - slot])

        xb = xs_buf[slot]
        gate = _dot(xb, wg_bf[...])
        up = _dot(xb, wu_bf[...])
        hid = (gate * _sigmoid(gate) * up).astype(BF16)
        ys_ref[...] = _dot(hid, wd_bf[...]).astype(ys_ref.dtype)

    @pl.when(i >= n_used)
    def _():
        ys_ref[...] = jnp.zeros(ys_ref.shape, ys_ref.dtype)


def _experts(block_e, piece_src, n_used, xs, wg, wu, wd, n_blocks):
    d = xs.shape[1]
    by_expert = lambda i, be, ps, nu: (0, be[i], 0, 0)
    grid_spec = pltpu.PrefetchScalarGridSpec(
        num_scalar_prefetch=3,
        grid=(n_blocks,),
        in_specs=[pl.BlockSpec(memory_space=pl.ANY),
                  pl.BlockSpec((None, None, d, EXPERT_FF), by_expert),
                  pl.BlockSpec((None, None, d, EXPERT_FF), by_expert),
                  pl.BlockSpec((None, None, EXPERT_FF, d), by_expert)],
        out_specs=pl.BlockSpec((ROW_BLOCK, d), lambda i, be, ps, nu: (i, 0)),
        scratch_shapes=[pltpu.VMEM((2, ROW_BLOCK, d), BF16),
                        pltpu.VMEM((d, EXPERT_FF), BF16), pltpu.VMEM((d, EXPERT_FF), BF16),
                        pltpu.VMEM((EXPERT_FF, d), BF16), pltpu.SemaphoreType.DMA((2,))],
    )
    return pl.pallas_call(
        _experts_kernel,
        grid_spec=grid_spec,
        out_shape=jax.ShapeDtypeStruct((n_blocks * ROW_BLOCK, d), BF16),
        compiler_params=pltpu.CompilerParams(
            dimension_semantics=("arbitrary",), vmem_limit_bytes=VMEM_LIMIT),
        name="experts",
    )(block_e, piece_src, n_used, xs, wg, wu, wd)


def _combine_kernel(piece_ref, ys_hbm, x1_ref, route_ref, mod_ref, ln2_ref, o_ref, buf, sem):
    i = pl.program_id(0)
    n = pl.num_programs(0)
    slot = i % 2
    rows = buf.shape[1]
    per_tile = rows // PIECE

    @pl.when(i == 0)
    def _():
        _issue_pieces(ys_hbm, piece_ref, 0, per_tile, buf.at[0], sem.at[0])

    _wait_pieces(ys_hbm, buf.at[slot], sem.at[slot])

    @pl.when(i + 1 < n)
    def _():
        _issue_pieces(ys_hbm, piece_ref, (i + 1) * per_tile, per_tile, buf.at[1 - slot], sem.at[1 - slot])

    tm = x1_ref.shape[0]
    lane = lax.broadcasted_iota(jnp.int32, (tm, rows), 1).astype(F32)
    sel = (jnp.where(lane == route_ref[:, 4:5], route_ref[:, 2:3], 0.0)
           + jnp.where(lane == route_ref[:, 5:6], route_ref[:, 3:4], 0.0)).astype(BF16)
    moe = _dot(sel, buf[slot])
    z = ALPHA * x1_ref[...] + mod_ref[0:1, :] * moe
    o_ref[...] = _ln(z) * ln2_ref[0:1, :] + ln2_ref[1:2, :]


def _combine(piece_glob, ys, x1, route, mod, ln2, tm, tiles_per_batch):
    t, d = x1.shape
    grid_spec = pltpu.PrefetchScalarGridSpec(
        num_scalar_prefetch=1,
        grid=(t // tm,),
        in_specs=[pl.BlockSpec(memory_space=pl.ANY),
                  pl.BlockSpec((tm, d), lambda i, pg: (i, 0)),
                  pl.BlockSpec((tm, ROUTE_LANES), lambda i, pg: (i, 0)),
                  pl.BlockSpec((None, 8, d), lambda i, pg: (i // tiles_per_batch, 0, 0)),
                  pl.BlockSpec((2, d), lambda i, pg: (0, 0))],
        out_specs=pl.BlockSpec((tm, d), lambda i, pg: (i, 0)),
        scratch_shapes=[pltpu.VMEM((2, _local_rows(tm), d), BF16), pltpu.SemaphoreType.DMA((2,))],
    )
    return pl.pallas_call(
        _combine_kernel,
        grid_spec=grid_spec,
        out_shape=jax.ShapeDtypeStruct((t, d), F32),
        compiler_params=pltpu.CompilerParams(
            dimension_semantics=("arbitrary",), vmem_limit_bytes=VMEM_LIMIT),
        name="combine",
    )(piece_glob, ys, x1, route, mod, ln2)


def _sincos_2d(rows, cols, dim):
    q = dim // 4
    omega = 1.0 / (POS_BASE ** (jnp.arange(q, dtype=F32) / q))
    r = jnp.arange(rows, dtype=F32)[:, None] * omega
    cl = jnp.arange(cols, dtype=F32)[:, None] * omega
    r_emb = jnp.concatenate([jnp.sin(r), jnp.cos(r)], -1)
    c_emb = jnp.concatenate([jnp.sin(cl), jnp.cos(cl)], -1)
    emb = jnp.concatenate([jnp.broadcast_to(r_emb[:, None, :], (rows, cols, 2 * q)),
                           jnp.broadcast_to(c_emb[None, :, :], (rows, cols, 2 * q))], -1)
    return emb.reshape(rows * cols, dim)


def _routing_tables(run_pieces, xs_rows, n_blocks):
    i32 = jnp.int32
    n_tiles = run_pieces.shape[0]
    ppb = ROW_BLOCK // PIECE
    loc_start = jnp.cumsum(run_pieces, axis=1) - run_pieces
    seg_tot = jnp.sum(run_pieces, axis=0)
    seg_pad = (seg_tot + ppb - 1) // ppb * ppb
    seg_end = jnp.cumsum(seg_pad)
    seg_start = seg_end - seg_pad
    run_t = run_pieces.T
    glob_start = seg_start[:, None] + jnp.cumsum(run_t, axis=1) - run_t
    n_used = (seg_end[-1] // ppb).astype(i32)
    blk = jnp.minimum(jnp.arange(n_blocks, dtype=i32), n_used - 1)
    block_e = jnp.minimum(jnp.sum((seg_end[None, :] <= (blk * ppb)[:, None]).astype(i32), axis=1),
                          N_EXPERTS - 1).astype(i32)
    lpt = xs_rows // PIECE
    starts = glob_start.reshape(-1)
    lens = run_t.reshape(-1)
    src0 = (jnp.arange(n_tiles, dtype=i32)[None, :] * lpt + loc_start.T).reshape(-1)
    p = jnp.arange(n_blocks * ppb, dtype=i32)
    within = p[:, None] - starts[None, :]
    hit = (within >= 0) & (within < lens[None, :])
    piece_src = jnp.sum(jnp.where(hit, (src0[None, :] + within) * PIECE, 0), axis=1).astype(i32)
    s = jnp.arange(lpt, dtype=i32)
    loc_within = s[None, :, None] - loc_start[:, None, :]
    hit = (loc_within >= 0) & (loc_within < run_pieces[:, None, :])
    piece_glob = jnp.sum(jnp.where(hit, (glob_start.T[:, None, :] + loc_within) * PIECE, 0), axis=2)
    return block_e, piece_src, piece_glob.astype(i32).reshape(-1), n_used.reshape(1)


def kernel(x, c, ctx, c_ctx, w_ada, b_ada, w_in, s5_log_dt_f, s5_a_re_f, s5_a_im_f, s5_b_re_f, s5_b_im_f, s5_c_re_f, s5_c_im_f, s5_log_dt_b, s5_a_re_b, s5_a_im_b, s5_b_re_b, s5_b_im_b, s5_c_re_b, s5_c_im_b, s5_d, s5_w_glu_val, s5_w_glu_gate, conv_w, conv_w_out, w_o, ln1_g, ln1_b, router_w_group, router_b_group, router_w_expert, router_b_expert, exp_w_gate, exp_w_up, exp_w_down, ln2_g, ln2_b):
    b, l, d = x.shape
    lc = ctx.shape[1]
    assert d == D_MODEL and b < SUBLANES and w_ada.shape[0] == DEPTH
    assert l % (SUBLANES * CHUNK) == 0 and lc % (SUBLANES * CHUNK) == 0 and l % GRID_W == 0
    t = b * l
    tm = min(512, l)
    tmc = min(512, lc)

    cc = jnp.concatenate([c, c_ctx[None, :], jnp.zeros((8 - b - 1, d), F32)], 0)
    mods = _mods(cc, w_ada[0], b_ada[0])
    sh1, sc1, g1, sh2, sc2, g2 = jnp.split(mods, 6, axis=-1)
    mod_a = jnp.stack([sh1[:b], 1.0 + sc1[:b]], 1)
    mod_ctx = jnp.broadcast_to(jnp.stack([sh1[b], 1.0 + sc1[b]], 0)[None], (b, 2, d))
    mod_c = jnp.stack([g1[:b], sh2[:b], 1.0 + sc2[:b], jnp.zeros((b, d), F32)], 1)
    mod_f = jnp.concatenate([g2[:b, None, :], jnp.zeros((b, 7, d), F32)], 1)

    w_in_bf = w_in[0].astype(BF16)
    f_tab = _s5_dir_tables(s5_log_dt_f[0], s5_a_re_f[0], s5_a_im_f[0], s5_b_re_f[0], s5_b_im_f[0],
                           s5_c_re_f[0], s5_c_im_f[0])
    b_tab = _s5_dir_tables(s5_log_dt_b[0], s5_a_re_b[0], s5_a_im_b[0], s5_b_re_b[0], s5_b_im_b[0],
                           s5_c_re_b[0], s5_c_im_b[0])
    mi, ws, wo_s5, tab = _s5_operators(f_tab, b_tab, s5_d[0])

    (uc_ctx,) = _in_proj(ctx, jnp.zeros((lc, d), F32), mod_ctx, w_in_bf, None, None, tmc, False)
    zero_state = jnp.zeros((N_PAIRS, 4, SUBLANES, LANES), F32)
    _, s0 = _s5_scan(uc_ctx, mi, ws, wo_s5, tab, zero_state, b)

    pos = _sincos_2d(l // GRID_W, GRID_W, d)
    uc, siga, sb = _in_proj(x, pos, mod_a, w_in_bf, conv_w[0], conv_w_out[0].astype(BF16), tm, True)
    y, _ = _s5_scan(uc, mi, ws, wo_s5, tab, s0, b)

    wr = jnp.concatenate([router_w_group[0], router_w_expert[0],
                          jnp.zeros((d, ROUTE_LANES - N_EXPERT_GROUPS - N_EXPERTS), F32)], 1).T.astype(BF16)
    br =jnp.concatenate([router_b_group[0], router_b_expert[0],
                          jnp.zeros((ROUTE_LANES - N_EXPERT_GROUPS - N_EXPERTS,), F32)])[:, None]
    ln1 = jnp.stack([ln1_g[0], ln1_b[0]], 0)
    x1, xs, route, run_len = _mix_out(y, siga, sb, x, pos, mod_c,
                                      s5_w_glu_val[0].astype(BF16), s5_w_glu_gate[0].astype(BF16),
                                      w_o[0].astype(BF16), ln1, wr, br, tm)

    x1 = x1.reshape(t, d)
    route = route.reshape(t, ROUTE_LANES)
    n_tiles = t // tm
    xs_rows = _local_rows(tm)
    run_pieces = run_len[:, :N_EXPERTS, 0].astype(jnp.int32)
    max_rows = t * TOP_K + n_tiles * N_EXPERTS * (PIECE - 1) + N_EXPERTS * (ROW_BLOCK - 1)
    n_blocks = -(-max_rows // ROW_BLOCK)
    block_e, piece_src, piece_glob, n_used = _routing_tables(run_pieces, xs_rows, n_blocks)
    ys = _experts(block_e, piece_src, n_used, xs, exp_w_gate, exp_w_up, exp_w_down, n_blocks)
    ln2 = jnp.stack([ln2_g[0], ln2_b[0]], 0)
    out = _combine(piece_glob, ys, x1, route, mod_f, ln2, tm, l // tm)
    return out.reshape(b, l, d)
```

```python
import functools
import math

import jax
import jax.numpy as jnp
import numpy as np
from jax import lax
from jax.experimental import pallas as pl
from jax.experimental.pallas import tpu as pltpu

F32 = jnp.float32
BF16 = jnp.bfloat16
HI = lax.Precision.HIGHEST

D_MODEL = 1024
GRID_W = 64
S5_WIDTH = 512
S5_GROUP_CH = 16
S5_GROUPS = S5_WIDTH // S5_GROUP_CH
S5_STATE = 64
CONV_WIDTH = 512
N_EXPERT_GROUPS = 4
EXPERTS_PER_GROUP = 8
N_EXPERTS = N_EXPERT_GROUPS * EXPERTS_PER_GROUP
EXPERT_FF = 512
TOP_K = 2
DEPTH = 1
ALPHA = (2.0 * DEPTH) ** 0.25
LN_EPS = 1e-6
POS_BASE = 10000.0

LANES = 128
SUBLANES = 8
CHUNK = 16
GROUP_W = CHUNK * S5_GROUP_CH
PAIR_W = 2 * GROUP_W
N_PAIRS = S5_GROUPS // 2
TOK_PER_VREG = LANES // S5_GROUP_CH
TAB_ROWS = 24
ROUTE_LANES = 128
ROW_BLOCK = 512
PIECE = 16
VMEM_LIMIT = 56 * 1024 * 1024


def _ln(x):
    mu = jnp.mean(x, axis=-1, keepdims=True)
    xc = x - mu
    var = jnp.mean(xc * xc, axis=-1, keepdims=True)
    return xc * lax.rsqrt(var + LN_EPS)


def _sigmoid(x):
    return 0.5 * (jnp.tanh(0.5 * x) + 1.0)


def _dot(a, b):
    return jnp.dot(a, b, preferred_element_type=F32)


def _mods_kernel(c_ref, w_ref, b_ref, o_ref):
    c = c_ref[...]
    a = c * _sigmoid(c)
    o_ref[...] = jnp.dot(a, w_ref[...], precision=HI, preferred_element_type=F32) + b_ref[...]


def _mods(cc, w_ada, b_ada):
    n = w_ada.shape[1]
    nb = 1536
    return pl.pallas_call(
        _mods_kernel,
        grid=(n // nb,),
        in_specs=[pl.BlockSpec((8, D_MODEL), lambda i: (0, 0)),
                  pl.BlockSpec((D_MODEL, nb), lambda i: (0, i)),
                  pl.BlockSpec((1, nb), lambda i: (0, i))],
        out_specs=pl.BlockSpec((8, nb), lambda i: (0, i)),
        out_shape=jax.ShapeDtypeStruct((8, n), F32),
        compiler_params=pltpu.CompilerParams(vmem_limit_bytes=VMEM_LIMIT),
        name="mods",
    )(cc, w_ada, b_ada.reshape(1, n))


def _slot_masks(rows):
    slot = lax.broadcasted_iota(jnp.int32, (rows, LANES), 1) // S5_GROUP_CH
    return [slot == s for s in range(TOK_PER_VREG)]


def _to_chunk_tile(u_scr, uc_ref):
    nch = uc_ref.shape[0]
    masks = _slot_masks(nch)
    for qh in range(CHUNK // TOK_PER_VREG):
        for v in range(S5_WIDTH // LANES):
            src = [u_scr[v, pl.ds(qh * TOK_PER_VREG + s, nch, stride=CHUNK), :] for s in range(TOK_PER_VREG)]
            for i in range(TOK_PER_VREG):
                acc = None
                for s in range(TOK_PER_VREG):
                    shift = ((s - i) * S5_GROUP_CH) % LANES
                    piece = pltpu.roll(src[s], shift, 1) if shift else src[s]
                    acc = piece if acc is None else jnp.where(masks[s], piece, acc)
                lo = (v * TOK_PER_VREG + i) * GROUP_W + qh * LANES
                uc_ref[:, lo:lo + LANES] = acc.astype(uc_ref.dtype)


def _from_chunk_tile(yc_ref, y_scr):
    nch = yc_ref.shape[0]
    masks = _slot_masks(nch)
    for qh in range(CHUNK // TOK_PER_VREG):
        for v in range(S5_WIDTH // LANES):
            src = []
            for i in range(TOK_PER_VREG):
                lo = (v * TOK_PER_VREG + i) * GROUP_W + qh * LANES
                src.append(yc_ref[:, lo:lo + LANES].astype(F32))
            for s in range(TOK_PER_VREG):
                acc = None
                for i in range(TOK_PER_VREG):
                    shift = ((i - s) * S5_GROUP_CH) % LANES
                    piece = pltpu.roll(src[i], shift, 1) if shift else src[i]
                    acc = piece if acc is None else jnp.where(masks[i], piece, acc)
                y_scr[v, pl.ds(qh * TOK_PER_VREG + s, nch, stride=CHUNK), :] = acc


def _with_positions(x_ref, remb_ref, cemb_ref):
    c = cemb_ref[...]
    slabs = []
    for j in range(x_ref.shape[0] // GRID_W):
        r = jnp.broadcast_to(remb_ref[j:j + 1, :], c.shape)
        slabs.append(x_ref[j * GRID_W:(j + 1) * GRID_W, :] + jnp.concatenate([r, c], axis=-1))
    return jnp.concatenate(slabs, axis=0)


def _in_proj_kernel(x_ref, remb_ref, cemb_ref, mod_ref, w_ref, *rest, full):
    if full:
        cw_ref, cwo_ref, uc_ref, siga_ref, sb_ref, u_scr = rest
    else:
        uc_ref, u_scr = rest
    xp = _with_positions(x_ref, remb_ref, cemb_ref)
    h = (_ln(xp) * mod_ref[1:2, :] + mod_ref[0:1, :]).astype(BF16)
    o1, o2, o3, o4, o5 = 512, 1024, 1536, 2048, 3072
    u = _dot(h, w_ref[:, 0:o1])
    for v in range(S5_WIDTH // LANES):
        u_scr[v] = u[:, v * LANES:(v + 1) * LANES]
    _to_chunk_tile(u_scr, uc_ref)
    if not full:
        return
    z_b = _dot(h, w_ref[:, o1:o2])
    gate_c = _dot(h, w_ref[:, o3:o4])
    p = gate_c * z_b
    tm = p.shape[0]
    col = lax.broadcasted_iota(jnp.int32, (tm, 1), 0) % GRID_W
    prev = jnp.where(col == 0, 0.0, pltpu.roll(p, 1, 0))
    nxt = jnp.where(col == GRID_W - 1, 0.0, pltpu.roll(p, tm - 1, 0))
    v = cw_ref[0:1, :] * prev + cw_ref[1:2, :] * p + cw_ref[2:3, :] * nxt
    gate_b = _dot(h, w_ref[:, o2:o3])
    out_b = _dot((gate_b * v).astype(BF16), cwo_ref[...])
    merge_b = _dot(h, w_ref[:, o5:])
    sb_ref[...] = (_sigmoid(merge_b) * out_b).astype(sb_ref.dtype)
    merge_a = _dot(h, w_ref[:, o4:o5])
    siga_ref[...] = _sigmoid(merge_a).astype(siga_ref.dtype)


def _pos_specs(tm, d):
    return [pl.BlockSpec((tm // GRID_W, d // 2), lambda i, j: (i, 0)),
            pl.BlockSpec((GRID_W, d // 2), lambda i, j: (0, 0))]


def _in_proj(x, r_emb, c_emb, mod, w_in_bf, conv_w, conv_w_out_bf, tm, full):
    b, l, d = x.shape
    n_tiles = l // tm
    grid = (n_tiles, b)
    tok = lambda w: pl.BlockSpec((None, tm, w), lambda i, j: (j, i, 0))
    chunk_spec = pl.BlockSpec((tm // CHUNK, CHUNK * S5_WIDTH), lambda i, j: (j * n_tiles + i, 0))
    chunk_shape = jax.ShapeDtypeStruct((b * l // CHUNK, CHUNK * S5_WIDTH), BF16)
    in_specs = [tok(d)] + _pos_specs(tm, d) + [pl.BlockSpec((None, 2, d), lambda i, j: (j, 0, 0))]
    args = [x, r_emb, c_emb, mod]
    if full:
        in_specs += [pl.BlockSpec(w_in_bf.shape, lambda i, j: (0, 0)),
                     pl.BlockSpec(conv_w.shape, lambda i, j: (0, 0)),
                     pl.BlockSpec(conv_w_out_bf.shape, lambda i, j: (0, 0))]
        args += [w_in_bf, conv_w, conv_w_out_bf]
        out_specs = [chunk_spec, tok(d), tok(d)]
        out_shape = [chunk_shape,
                     jax.ShapeDtypeStruct((b, l, d), BF16),
                     jax.ShapeDtypeStruct((b, l, d), BF16)]
    else:
        in_specs += [pl.BlockSpec((d, S5_WIDTH), lambda i, j: (0, 0))]
        args += [w_in_bf]
        out_specs = [chunk_spec]
        out_shape = [chunk_shape]
    return pl.pallas_call(
        functools.partial(_in_proj_kernel, full=full),
        grid=grid, in_specs=in_specs, out_specs=out_specs, out_shape=out_shape,
        scratch_shapes=[pltpu.VMEM((S5_WIDTH // LANES, tm, LANES), F32)],
        compiler_params=pltpu.CompilerParams(
            dimension_semantics=("arbitrary", "arbitrary"), vmem_limit_bytes=VMEM_LIMIT),
        name="in_proj" if full else "in_proj_ctx",
    )(*args)


def _s5_dir_tables(log_dt, a_re, a_im, b_re, b_im, c_re, c_im):
    f32 = F32
    dt = jnp.exp(log_dt.astype(f32))[:, None]
    a_re = a_re.astype(f32)
    a_im = a_im.astype(f32)
    mag = jnp.exp(dt * a_re)
    ab_re = mag * jnp.cos(dt * a_im)
    ab_im = mag * jnp.sin(dt * a_im)
    den = a_re * a_re + a_im * a_im
    x_re = ab_re - 1.0
    f_re = (x_re * a_re + ab_im * a_im) / den
    f_im = (ab_im * a_re - x_re * a_im) / den
    b_re = b_re.astype(f32)
    b_im = b_im.astype(f32)
    bb_re = f_re[..., None] * b_re - f_im[..., None] * b_im
    bb_im = f_re[..., None] * b_im + f_im[..., None] * b_re
    k = jnp.arange(CHUNK + 1, dtype=f32)[:, None, None]
    pmag = jnp.exp(k * (dt * a_re)[None])
    p_re = pmag * jnp.cos(k * (dt * a_im)[None])
    p_im = pmag * jnp.sin(k * (dt * a_im)[None])
    pb_re = p_re[..., None] * bb_re[None] - p_im[..., None] * bb_im[None]
    pb_im = p_re[..., None] * bb_im[None] + p_im[..., None] * bb_re[None]
    c_re = c_re.astype(f32)
    c_im = c_im.astype(f32)
    cp_re = c_re[None] * p_re[:, :, None, :] - c_im[None] * p_im[:, :, None, :]
    cp_im = -(c_re[None] * p_im[:, :, None, :] + c_im[None] * p_re[:, :, None, :])
    return dict(p_re=p_re, p_im=p_im, pb_re=pb_re, pb_im=pb_im, cp_re=cp_re, cp_im=cp_im,
                bb_re=bb_re, bb_im=bb_im)


def _lag_kernels(f, bk):
    g, n, c = S5_GROUPS, S5_STATE, S5_GROUP_CH
    k = CHUNK + 1
    lhs = jnp.stack([jnp.concatenate([t['cp_re'], t['cp_im']], -1) for t in (f, bk)], 0)
    lhs = lhs.transpose(0, 2, 1, 3, 4).reshape(2 * g, k * c, 2 * n)
    rhs = jnp.stack([jnp.concatenate([t['bb_re'], t['bb_im']], 1) for t in (f, bk)], 0)
    out = jnp.einsum('bmn,bnc->bmc', lhs, rhs.reshape(2 * g, 2 * n, c), precision=HI)
    out = out.reshape(2, g, k, c, c).transpose(0, 2, 1, 4, 3)
    return out[0], out[1]


def _s5_operators(f, bk, s5_d):
    q = CHUNK
    g, n, c = S5_GROUPS, S5_STATE, S5_GROUP_CH
    kern_f, kern_b = _lag_kernels(f, bk)
    k0 = kern_f[0] + kern_b[0] + s5_d.astype(F32)[:, :, None] * jnp.eye(c, dtype=F32)[None]
    kc = jnp.concatenate([kern_b[1:q][::-1], k0[None], kern_f[1:q]], 0)
    qi = np.arange(q)
    toeplitz = np.eye(2 * q - 1, dtype=np.float32)[qi[None, :] - qi[:, None] + q - 1]
    m_intra = jnp.einsum('iok,kgcd->gicod', toeplitz, kc, precision=HI).reshape(g, q * c, q * c)
    wf_re = f['pb_re'][:q][::-1]
    wf_im = f['pb_im'][:q][::-1]
    wb_re = bk['pb_re'][:q]
    wb_im = bk['pb_im'][:q]
    w_st = jnp.stack([wf_re, wf_im, wb_re, wb_im], 0)
    w_st = w_st.transpose(2, 1, 4, 0, 3).reshape(g, q * c, 4, n)
    of_re = f['cp_re'][1:]
    of_im = f['cp_im'][1:]
    ob_re = bk['cp_re'][1:][::-1]
    ob_im = bk['cp_im'][1:][::-1]
    w_out = jnp.stack([of_re, of_im, ob_re, ob_im], 0)
    w_out = w_out.transpose(2, 0, 4, 1, 3).reshape(g, 4, n, q * c)
    np_ = N_PAIRS
    w_st = w_st.astype(BF16).reshape(np_, 2, q * c, 4, n)
    ws_pair = jnp.concatenate([jnp.pad(w_st[:, 0], ((0, 0), (0, 0), (0, 0), (0, n))),
                               jnp.pad(w_st[:, 1], ((0, 0), (0, 0), (0, 0), (n, 0)))], 1)
    ws_pair = ws_pair.reshape(np_, PAIR_W, 4 * 2 * n)
    w_out = w_out.astype(BF16).reshape(np_, 2, 4, n, q * c)
    wo_pair = jnp.stack([jnp.pad(w_out[:, 0], ((0, 0), (0, 0), (0, 0), (0, q * c))),
                         jnp.pad(w_out[:, 1], ((0, 0), (0, 0), (0, 0), (q * c, 0)))], 2)
    wo_pair = wo_pair.reshape(np_, 4 * 2 * n, PAIR_W)
    tab = jnp.concatenate([_chunk_power_table(f, False), _chunk_power_table(bk, True)], 0)
    tab = tab.reshape(2 * TAB_ROWS, np_, 2 * n).transpose(1, 0, 2)
    return m_intra.astype(BF16), ws_pair, wo_pair, tab


def _chunk_power_table(t, backward):
    def cmul(x, y):
        return x[0] * y[0] - x[1] * y[1], x[0] * y[1] + x[1] * y[0]
    p1 = (t['p_re'][CHUNK], t['p_im'][CHUNK])
    p2 = cmul(p1, p1)
    p4 = cmul(p2, p2)
    p8 = cmul(p4, p4)
    pr = [(jnp.ones_like(p1[0]), jnp.zeros_like(p1[0]))]
    for _ in range(SUBLANES - 1):
        pr.append(cmul(pr[-1], p1))
    if backward:
        pr = pr[::-1]
    rows = [p[0] for p in pr] + [p[1] for p in pr]
    for p in (p1, p2, p4, p8):
        rows += [p[0], p[1]]
    return jnp.stack(rows, 0)


def _s5_scan_kernel(uc_ref, mi_ref, ws_ref, wo_ref, tab_ref, s0_ref, y_ref, fin_ref, s_scr, in_scr, *, batch):
    rows = uc_ref.shape[0]
    chunks = rows // batch
    n_tiles = chunks // SUBLANES
    u = uc_ref[...]
    s_scr[...] = _dot(u, ws_ref[...])
    row = lax.broadcasted_iota(jnp.int32, (SUBLANES, LANES), 0)

    def tile_scan(r0, backward, c_re, c_im):
        base = TAB_ROWS if backward else 0
        col = 2 * LANES if backward else 0
        rs = pl.ds(r0, SUBLANES)

        def shift(z, k):
            if backward:
                return jnp.where(row < SUBLANES - k, pltpu.roll(z, SUBLANES - k, 0), 0.0)
            return jnp.where(row >= k, pltpu.roll(z, k, 0), 0.0)

        z_re = s_scr[rs, col:col + LANES]
        z_im = s_scr[rs, col + LANES:col + 2 * LANES]
        for k, t in ((1, 16), (2, 18), (4, 20)):
            a_re = tab_ref[base + t:base + t + 1, :]
            a_im = tab_ref[base + t + 1:base + t + 2, :]
            sh_re = shift(z_re, k)
            sh_im = shift(z_im, k)
            z_re, z_im = z_re + (a_re * sh_re - a_im * sh_im), z_im + (a_re * sh_im + a_im * sh_re)
        pr_re = tab_ref[base:base + SUBLANES, :]
        pr_im = tab_ref[base + SUBLANES:base + 2 * SUBLANES, :]
        in_scr[rs, col:col + LANES] = pr_re * c_re - pr_im * c_im + shift(z_re, 1)
        in_scr[rs, col + LANES:col + 2 * LANES] = pr_re * c_im + pr_im * c_re + shift(z_im, 1)
        last = 0 if backward else SUBLANES - 1
        l_re = jnp.broadcast_to(z_re[last:last + 1, :], (SUBLANES, LANES))
        l_im = jnp.broadcast_to(z_im[last:last + 1, :], (SUBLANES, LANES))
        p8_re = tab_ref[base + 22:base + 23, :]
        p8_im = tab_ref[base + 23:base + 24, :]
        return p8_re * c_re - p8_im * c_im + l_re, p8_re * c_im + p8_im * c_re + l_im

    def body(m, carry):
        out = []
        for b in range(batch):
            cf_re, cf_im, cb_re, cb_im = carry[4 * b:4 * b + 4]
            rf = pl.multiple_of(b * chunks + m * SUBLANES, SUBLANES)
            rb = pl.multiple_of(b * chunks + (n_tiles - 1 - m) * SUBLANES, SUBLANES)
            out += list(tile_scan(rf, False, cf_re, cf_im))
            out += list(tile_scan(rb, True, cb_re, cb_im))
        return tuple(out)

    init = tuple(jnp.broadcast_to(s0_ref[t, b:b + 1, :], (SUBLANES, LANES))
                 for b in range(batch) for t in range(4))
    fin = lax.fori_loop(0, n_tiles, body, init, unroll=min(4, n_tiles))
    fin_ref[...] = jnp.zeros(fin_ref.shape, F32)
    for b in range(batch):
        for t in range(4):
            fin_ref[t, b:b + 1, :] = fin[4 * b + t][0:1, :]
    y_intra = jnp.concatenate([_dot(u[:, gl * GROUP_W:(gl + 1) * GROUP_W], mi_ref[gl]) for gl in range(2)], axis=-1)
    y = y_intra + _dot(in_scr[...].astype(BF16), wo_ref[...])
    y_ref[...] = y.astype(y_ref.dtype)


def _s5_scan(uc, mi, ws, wo, tab, s0, batch):
    rows = uc.shape[0]
    pair = lambda *shape: pl.BlockSpec((None,) + shape, lambda p: (p,) + (0,) * len(shape))
    return pl.pallas_call(
        functools.partial(_s5_scan_kernel, batch=batch),
        grid=(N_PAIRS,),
        in_specs=[pl.BlockSpec((rows, PAIR_W), lambda p: (0, p)),
                  pl.BlockSpec((2, GROUP_W, GROUP_W), lambda p: (p, 0, 0)),
                  pair(PAIR_W, PAIR_W), pair(PAIR_W, PAIR_W),
                  pair(2 * TAB_ROWS, LANES), pair(4, SUBLANES, LANES)],
        out_specs=[pl.BlockSpec((rows, PAIR_W), lambda p: (0, p)), pair(4, SUBLANES, LANES)],
        out_shape=[jax.ShapeDtypeStruct((rows, N_PAIRS * PAIR_W), BF16),
                   jax.ShapeDtypeStruct((N_PAIRS, 4, SUBLANES, LANES), F32)],
        scratch_shapes=[pltpu.VMEM((rows, PAIR_W), F32), pltpu.VMEM((rows, PAIR_W), F32)],
        compiler_params=pltpu.CompilerParams(
            dimension_semantics=("arbitrary",), vmem_limit_bytes=VMEM_LIMIT),
        name="s5_scan",
    )(uc, mi, ws, wo, tab, s0)


def _mix_out_kernel(y_ref, siga_ref, sb_ref, x_ref, remb_ref, cemb_ref, mod_ref, wv_ref, wg_ref, wo_ref,
                    ln1_ref, wr_ref, br_ref, ut_ref, lt_ref,
                    x1_ref, xs_ref, route_ref, len_ref, y_scr):
    _from_chunk_tile(y_ref, y_scr)
    y = jnp.concatenate([y_scr[v] for v in range(S5_WIDTH // LANES)], axis=-1)
    ya = (0.5 * y * (1.0 + jnp.tanh(math.sqrt(2.0 / math.pi) * (y + 0.044715 * (y * y * y))))).astype(BF16)
    out_a = _dot(ya, wv_ref[...]) * _sigmoid(_dot(ya, wg_ref[...]))
    merged = siga_ref[...].astype(F32) * out_a + sb_ref[...].astype(F32)
    mix = _dot(merged.astype(BF16), wo_ref[...])
    xp = _with_positions(x_ref, remb_ref, cemb_ref)
    x1 = _ln(ALPHA * xp + mod_ref[0:1, :] * mix) * ln1_ref[0:1, :] + ln1_ref[1:2, :]
    x1_ref[...] = x1
    h2 = _ln(x1) * mod_ref[2:3, :] + mod_ref[1:2, :]
    _route_and_sort(h2, wr_ref, br_ref, ut_ref, lt_ref, xs_ref, route_ref, len_ref)


def _route_and_sort(h2, wr_ref, br_ref, ut_ref, lt_ref, xs_ref, route_ref, len_ref):
    tm = h2.shape[0]
    f32 = F32
    h_hi = h2.astype(BF16)
    nt = (((1,), (1,)), ((), ()))
    lg = lax.dot_general(wr_ref[...], h_hi, nt, preferred_element_type=f32) + br_ref[...]
    rowi = lax.broadcasted_iota(jnp.int32, (ROUTE_LANES, tm), 0).astype(f32)
    neg = jnp.float32(-jnp.inf)
    big = jnp.float32(ROUTE_LANES)
    gl = jnp.where(rowi < N_EXPERT_GROUPS, lg, neg)
    gmax = jnp.max(gl, axis=0, keepdims=True)
    g_idx = jnp.min(jnp.where(gl == gmax, rowi, big), axis=0, keepdims=True)
    p_group = 1.0 / jnp.sum(jnp.exp(gl - gmax), axis=0, keepdims=True)
    e_lo = N_EXPERT_GROUPS + g_idx * EXPERTS_PER_GROUP
    el = jnp.where((rowi >= e_lo) & (rowi < e_lo + EXPERTS_PER_GROUP), lg, neg)
    m1 = jnp.max(el, axis=0, keepdims=True)
    i1 = jnp.min(jnp.where(el == m1, rowi, big), axis=0, keepdims=True)
    el2 = jnp.where(rowi == i1, neg, el)
    m2 = jnp.max(el2, axis=0, keepdims=True)
    i2 = jnp.min(jnp.where(el2 == m2, rowi, big), axis=0, keepdims=True)
    r = jnp.exp(m2 - m1)
    w1 = p_group / (1.0 + r)
    w2 = p_group * r / (1.0 + r)
    e1 = i1 - N_EXPERT_GROUPS
    e2 = i2 - N_EXPERT_GROUPS
    a12 = jnp.where(rowi == e1, 1.0, 0.0) + jnp.where(rowi == e2 + N_EXPERTS, 1.0, 0.0)
    rank = _dot(a12.astype(BF16), ut_ref[...])
    cnt = jnp.broadcast_to(jnp.sum(a12, axis=1, keepdims=True), (ROUTE_LANES, LANES))
    row = lax.broadcasted_iota(jnp.int32, (ROUTE_LANES, LANES), 0)
    tot = cnt + pltpu.roll(cnt, ROUTE_LANES - N_EXPERTS, 0)
    run = jnp.where(row < N_EXPERTS, jnp.floor((tot + (PIECE - 1)) * (1.0 / PIECE)), 0.0)
    off = PIECE * _dot(lt_ref[...], run.astype(BF16))
    base = jnp.where(row < N_EXPERTS, off, pltpu.roll(off + cnt, N_EXPERTS, 0))
    posmat = a12 * (rank + base[:, 0:1])
    pos1 = jnp.sum(posmat[0:N_EXPERTS, :], axis=0, keepdims=True)
    pos2 = jnp.sum(posmat[N_EXPERTS:2 * N_EXPERTS, :], axis=0, keepdims=True)
    ri = lax.broadcasted_iota(jnp.int32, (xs_ref.shape[0], tm), 0).astype(f32)
    perm = jnp.where((ri == pos1) | (ri == pos2), 1.0, 0.0).astype(BF16)
    xs_ref[...] = _dot(perm, h_hi).astype(xs_ref.dtype)
    rec = jnp.where(rowi == 0, e1, jnp.where(rowi == 1, e2, jnp.where(rowi == 2, w1, jnp.where(
        rowi == 3, w2, jnp.where(rowi == 4, pos1, jnp.where(rowi == 5, pos2, 0.0))))))
    route_ref[...] = rec.T
    len_ref[...] = run


def _local_rows(tm):
    return TOP_K * tm + N_EXPERTS * PIECE


def _mix_out(y, siga, sb, x, r_emb, c_emb, mod, wv, wg, wo, ln1, wr, br, tm):
    b, l, d = x.shape
    n_tiles = l // tm
    xs_rows = _local_rows(tm)
    tok = lambda w: pl.BlockSpec((None, tm, w), lambda i, j: (j, i, 0))
    whole = lambda a: pl.BlockSpec(a.shape, lambda i, j: (0,) * a.ndim)
    ut = (jnp.arange(tm)[:, None] < jnp.arange(tm)[None, :]).astype(BF16)
    lt = (jnp.arange(ROUTE_LANES)[None, :] < jnp.arange(ROUTE_LANES)[:, None]).astype(BF16)
    return pl.pallas_call(
        _mix_out_kernel,
        grid=(n_tiles, b),
        in_specs=[pl.BlockSpec((tm // CHUNK, CHUNK * S5_WIDTH), lambda i, j: (j * n_tiles + i, 0)),
                  tok(d), tok(d), tok(d)] + _pos_specs(tm, d) + [
                  pl.BlockSpec((None, 4, d), lambda i, j: (j, 0, 0)),
                  whole(wv), whole(wg), whole(wo), whole(ln1), whole(wr), whole(br),
                  whole(ut), whole(lt)],
        out_specs=[tok(d),
                   pl.BlockSpec((xs_rows, d), lambda i, j: (j * n_tiles + i, 0)),
                   tok(ROUTE_LANES),
                   pl.BlockSpec((None, ROUTE_LANES, LANES), lambda i, j: (j * n_tiles + i, 0, 0))],
        out_shape=[jax.ShapeDtypeStruct((b, l, d), F32),
                   jax.ShapeDtypeStruct((b * n_tiles * xs_rows, d), BF16),
                   jax.ShapeDtypeStruct((b, l, ROUTE_LANES), F32),
                   jax.ShapeDtypeStruct((b * n_tiles, ROUTE_LANES, LANES), F32)],
        scratch_shapes=[pltpu.VMEM((S5_WIDTH // LANES, tm, LANES), F32)],
        compiler_params=pltpu.CompilerParams(
            dimension_semantics=("arbitrary", "arbitrary"), vmem_limit_bytes=VMEM_LIMIT),
        name="mix_out",
    )(y, siga, sb, x, r_emb, c_emb, mod, wv, wg, wo, ln1, wr, br, ut, lt)


def _piece_copy(src_hbm, src_row, dst, piece, sem):
    return pltpu.make_async_copy(src_hbm.at[pl.ds(pl.multiple_of(src_row, PIECE), PIECE), :],
                                 dst.at[pl.ds(pl.multiple_of(piece * PIECE, PIECE), PIECE), :], sem)


def _issue_pieces(src_hbm, table_ref, first, n_pieces, dst, sem):
    def body(p, _):
        _piece_copy(src_hbm, table_ref[first + p], dst, p, sem).start()
        return 0
    lax.fori_loop(0, n_pieces, body, 0)


def _wait_pieces(src_hbm, dst, sem):
    pltpu.make_async_copy(src_hbm.at[pl.ds(0, dst.shape[0]), :], dst, sem).wait()


def _experts_kernel(be_ref, piece_ref, nused_ref, xs_hbm, wg_ref, wu_ref, wd_ref, ys_ref,
                    xs_buf, wg_bf, wu_bf, wd_bf, sem):
    i = pl.program_id(0)
    n_used = nused_ref[0]
    slot = i % 2
    per_block = ROW_BLOCK // PIECE

    @pl.when(i == 0)
    def _():
        _issue_pieces(xs_hbm, piece_ref, 0, per_block, xs_buf.at[0], sem.at[0])

    @pl.when(i < n_used)
    def _():
        @pl.when((i == 0) | (be_ref[i] != be_ref[jnp.maximum(i - 1, 0)]))
        def _():
            wg_bf[...] = wg_ref[...].astype(BF16)
            wu_bf[...] = wu_ref[...].astype(BF16)
            wd_bf[...] = wd_ref[...].astype(BF16)

        _wait_pieces(xs_hbm, xs_buf.at[slot], sem.at[slot])

        @pl.when(i + 1 < n_used)
        def _():
            _issue_pieces(xs_hbm, piece_ref, (i + 1) * per_block, per_block, xs_buf.at[1 - slot], sem.at[1 - slot])

        xb = xs_buf[slot]
        gate = _dot(xb, wg_bf[...])
        up = _dot(xb, wu_bf[...])
        hid = (gate * _sigmoid(gate) * up).astype(BF16)
        ys_ref[...] = _dot(hid, wd_bf[...]).astype(ys_ref.dtype)

    @pl.when(i >= n_used)
    def _():
        ys_ref[...] = jnp.zeros(ys_ref.shape, ys_ref.dtype)


def _experts(block_e, piece_src, n_used, xs, wg, wu, wd, n_blocks):
    d = xs.shape[1]
    by_expert = lambda i, be, ps, nu: (0, be[i], 0, 0)
    grid_spec = pltpu.PrefetchScalarGridSpec(
        num_scalar_prefetch=3,
        grid=(n_blocks,),
        in_specs=[pl.BlockSpec(memory_space=pl.ANY),
                  pl.BlockSpec((None, None, d, EXPERT_FF), by_expert),
                  pl.BlockSpec((None, None, d, EXPERT_FF), by_expert),
                  pl.BlockSpec((None, None, EXPERT_FF, d), by_expert)],
        out_specs=pl.BlockSpec((ROW_BLOCK, d), lambda i, be, ps, nu: (i, 0)),
        scratch_shapes=[pltpu.VMEM((2, ROW_BLOCK, d), BF16),
                        pltpu.VMEM((d, EXPERT_FF), BF16), pltpu.VMEM((d, EXPERT_FF), BF16),
                        pltpu.VMEM((EXPERT_FF, d), BF16), pltpu.SemaphoreType.DMA((2,))],
    )
    return pl.pallas_call(
        _experts_kernel,
        grid_spec=grid_spec,
        out_shape=jax.ShapeDtypeStruct((n_blocks * ROW_BLOCK, d), BF16),
        compiler_params=pltpu.CompilerParams(
            dimension_semantics=("arbitrary",), vmem_limit_bytes=VMEM_LIMIT),
        name="experts",
    )(block_e, piece_src, n_used, xs, wg, wu, wd)


def _combine_kernel(piece_ref, ys_hbm, x1_ref, route_ref, mod_ref, ln2_ref, o_ref, buf, sem):
    i = pl.program_id(0)
    n = pl.num_programs(0)
    slot = i % 2
    rows = buf.shape[1]
    per_tile = rows // PIECE

    @pl.when(i == 0)
    def _():
        _issue_pieces(ys_hbm, piece_ref, 0, per_tile, buf.at[0], sem.at[0])

    _wait_pieces(ys_hbm, buf.at[slot], sem.at[slot])

    @pl.when(i + 1 < n)
    def _():
        _issue_pieces(ys_hbm, piece_ref, (i + 1) * per_tile, per_tile, buf.at[1 - slot], sem.at[1 - slot])

    tm = x1_ref.shape[0]
    lane = lax.broadcasted_iota(jnp.int32, (tm, rows), 1).astype(F32)
    sel = (jnp.where(lane == route_ref[:, 4:5], route_ref[:, 2:3], 0.0)
           + jnp.where(lane == route_ref[:, 5:6], route_ref[:, 3:4], 0.0)).astype(BF16)
    moe = _dot(sel, buf[slot])
    z = ALPHA * x1_ref[...] + mod_ref[0:1, :] * moe
    o_ref[...] = _ln(z) * ln2_ref[0:1, :] + ln2_ref[1:2, :]


def _combine(piece_glob, ys, x1, route, mod, ln2, tm, tiles_per_batch):
    t, d = x1.shape
    grid_spec = pltpu.PrefetchScalarGridSpec(
        num_scalar_prefetch=1,
        grid=(t // tm,),
        in_specs=[pl.BlockSpec(memory_space=pl.ANY),
                  pl.BlockSpec((tm, d), lambda i, pg: (i, 0)),
                  pl.BlockSpec((tm, ROUTE_LANES), lambda i, pg: (i, 0)),
                  pl.BlockSpec((None, 8, d), lambda i, pg: (i // tiles_per_batch, 0, 0)),
                  pl.BlockSpec((2, d), lambda i, pg: (0, 0))],
        out_specs=pl.BlockSpec((tm, d), lambda i, pg: (i, 0)),
        scratch_shapes=[pltpu.VMEM((2, _local_rows(tm), d), BF16), pltpu.SemaphoreType.DMA((2,))],
    )
    return pl.pallas_call(
        _combine_kernel,
        grid_spec=grid_spec,
        out_shape=jax.ShapeDtypeStruct((t, d), F32),
        compiler_params=pltpu.CompilerParams(
            dimension_semantics=("arbitrary",), vmem_limit_bytes=VMEM_LIMIT),
        name="combine",
    )(piece_glob, ys, x1, route, mod, ln2)


def _sincos_2d(rows, cols, dim):
    q = dim // 4
    omega = 1.0 / (POS_BASE ** (jnp.arange(q, dtype=F32) / q))
    r = jnp.arange(rows, dtype=F32)[:, None] * omega
    cl = jnp.arange(cols, dtype=F32)[:, None] * omega
    r_emb = jnp.concatenate([jnp.sin(r), jnp.cos(r)], -1)
    c_emb = jnp.concatenate([jnp.sin(cl), jnp.cos(cl)], -1)
    return r_emb, c_emb


def _routing_tables(run_pieces, xs_rows, n_blocks):
    i32 = jnp.int32
    n_tiles = run_pieces.shape[0]
    ppb = ROW_BLOCK // PIECE
    loc_start = jnp.cumsum(run_pieces, axis=1) - run_pieces
    seg_tot = jnp.sum(run_pieces, axis=0)
    seg_pad = (seg_tot + ppb - 1) // ppb * ppb
    seg_end = jnp.cumsum(seg_pad)
    seg_start = seg_end - seg_pad
    run_t = run_pieces.T
    glob_start = seg_start[:, None] + jnp.cumsum(run_t, axis=1) - run_t
    n_used = (seg_end[-1] // ppb).astype(i32)
    blk = jnp.minimum(jnp.arange(n_blocks, dtype=i32), n_used - 1)
    block_e = jnp.minimum(jnp.sum((seg_end[None, :] <= (blk * ppb)[:, None]).astype(i32), axis=1),
                          N_EXPERTS - 1).astype(i32)
    lpt = xs_rows // PIECE
    starts = glob_start.reshape(-1)
    lens = run_t.reshape(-1)
    src0 = (jnp.arange(n_tiles, dtype=i32)[None, :] * lpt + loc_start.T).reshape(-1)
    p = jnp.arange(n_blocks * ppb, dtype=i32)
    within = p[:, None] - starts[None, :]
    hit = (within >= 0) & (within < lens[None, :])
    piece_src = jnp.sum(jnp.where(hit, (src0[None, :] + within) * PIECE, 0), axis=1).astype(i32)
    s = jnp.arange(lpt, dtype=i32)
    loc_within = s[None, :, None] - loc_start[:, None, :]
    hit = (loc_within >= 0) & (loc_within < run_pieces[:, None, :])
    piece_glob = jnp.sum(jnp.where(hit, (glob_start.T[:, None, :] + loc_within) * PIECE, 0), axis=2)
    return block_e, piece_src, piece_glob.astype(i32).reshape(-1), n_used.reshape(1)


def kernel(x, c, ctx, c_ctx, w_ada, b_ada, w_in, s5_log_dt_f, s5_a_re_f, s5_a_im_f, s5_b_re_f, s5_b_im_f, s5_c_re_f, s5_c_im_f, s5_log_dt_b, s5_a_re_b, s5_a_im_b, s5_b_re_b, s5_b_im_b, s5_c_re_b, s5_c_im_b, s5_d, s5_w_glu_val, s5_w_glu_gate, conv_w, conv_w_out, w_o, ln1_g, ln1_b, router_w_group, router_b_group, router_w_expert, router_b_expert, exp_w_gate, exp_w_up, exp_w_down, ln2_g, ln2_b):
    b, l, d = x.shape
    lc = ctx.shape[1]
    assert d == D_MODEL and b < SUBLANES and w_ada.shape[0] == DEPTH
    assert l % (SUBLANES * CHUNK) == 0 and lc % (SUBLANES * CHUNK) == 0 and l % GRID_W == 0
    t = b * l
    tm = min(512, l)
    tmc = min(512, lc)

    cc = jnp.concatenate([c, c_ctx[None, :], jnp.zeros((8 - b - 1, d), F32)], 0)
    mods = _mods(cc, w_ada[0], b_ada[0])
    sh1, sc1, g1, sh2, sc2, g2 = jnp.split(mods, 6, axis=-1)
    mod_a = jnp.stack([sh1[:b], 1.0 + sc1[:b]], 1)
    mod_ctx = jnp.broadcast_to(jnp.stack([sh1[b], 1.0 + sc1[b]], 0)[None], (b, 2, d))
    mod_c = jnp.stack([g1[:b], sh2[:b], 1.0 + sc2[:b], jnp.zeros((b, d), F32)], 1)
    mod_f = jnp.concatenate([g2[:b, None, :], jnp.zeros((b, 7, d), F32)], 1)

    w_in_bf = w_in[0].astype(BF16)
    f_tab = _s5_dir_tables(s5_log_dt_f[0], s5_a_re_f[0], s5_a_im_f[0], s5_b_re_f[0], s5_b_im_f[0],
                           s5_c_re_f[0], s5_c_im_f[0])
    b_tab = _s5_dir_tables(s5_log_dt_b[0], s5_a_re_b[0], s5_a_im_b[0], s5_b_re_b[0], s5_b_im_b[0],
                           s5_c_re_b[0], s5_c_im_b[0])
    mi, ws, wo_s5, tab = _s5_operators(f_tab, b_tab, s5_d[0])

    (uc_ctx,) = _in_proj(ctx, jnp.zeros((lc // GRID_W, d // 2), F32), jnp.zeros((GRID_W, d // 2), F32),
                         mod_ctx, w_in_bf, None, None, tmc, False)
    zero_state = jnp.zeros((N_PAIRS, 4, SUBLANES, LANES), F32)
    _, s0 = _s5_scan(uc_ctx, mi, ws, wo_s5, tab, zero_state, b)

    r_emb, c_emb = _sincos_2d(l // GRID_W, GRID_W, d)
    uc, siga, sb = _in_proj(x, r_emb, c_emb, mod_a, w_in_bf, conv_w[0], conv_w_out[0].astype(BF16), tm, True)
    y, _ = _s5_scan(uc, mi, ws, wo_s5, tab, s0, b)

    wr = jnp.concatenate([router_w_group[0], router_w_expert[0],
                          jnp.zeros((d, ROUTE_LANES - N_EXPERT_GROUPS - N_EXPERTS), F32)], 1).T.astype(BF16)
    br = jnp.concatenate([router_b_group[0], router_b_expert[0],
                          jnp.zeros((ROUTE_LANES - N_EXPERT_GROUPS - N_EXPERTS,), F32)])[:, None]
    ln1 = jnp.stack([ln1_g[0], ln1_b[0]], 0)
    x1, xs, route, run_len = _mix_out(y, siga, sb, x, r_emb, c_emb, mod_c,
                                      s5_w_glu_val[0].astype(BF16), s5_w_glu_gate[0].astype(BF16),
                                      w_o[0].astype(BF16), ln1, wr, br, tm)

    x1 = x1.reshape(t, d)
    route = route.reshape(t, ROUTE_LANES)
    n_tiles = t // tm
    xs_rows = _local_rows(tm)
    run_pieces = run_len[:, :N_EXPERTS, 0].astype(jnp.int32)
    max_rows = t * TOP_K + n_tiles * N_EXPERTS * (PIECE - 1) + N_EXPERTS * (ROW_BLOCK - 1)
    n_blocks = -(-max_rows // ROW_BLOCK)
    block_e, piece_src, piece_glob, n_used = _routing_tables(run_pieces, xs_rows, n_blocks)
    ys = _experts(block_e, piece_src, n_used, xs, exp_w_gate, exp_w_up, exp_w_down, n_blocks)
    ln2 = jnp.stack([ln2_g[0], ln2_b[0]], 0)
    out = _combine(piece_glob, ys, x1, route, mod_f, ln2, tm, l // tm)
    return out.reshape(b, l, d)
```

```python
import functools
import math

import jax
import jax.numpy as jnp
import numpy as np
from jax import lax
from jax.experimental import pallas as pl
from jax.experimental.pallas import tpu as pltpu

F32 = jnp.float32
BF16 = jnp.bfloat16
HI = lax.Precision.HIGHEST

D_MODEL = 1024
GRID_W = 64
S5_WIDTH = 512
S5_GROUP_CH = 16
S5_GROUPS = S5_WIDTH // S5_GROUP_CH
S5_STATE = 64
CONV_WIDTH = 512
N_EXPERT_GROUPS = 4
EXPERTS_PER_GROUP = 8
N_EXPERTS = N_EXPERT_GROUPS * EXPERTS_PER_GROUP
EXPERT_FF = 512
TOP_K = 2
DEPTH = 1
ALPHA = (2.0 * DEPTH) ** 0.25
LN_EPS = 1e-6
POS_BASE = 10000.0

LANES = 128
SUBLANES = 8
CHUNK = 16
GROUP_W = CHUNK * S5_GROUP_CH
PAIR_W = 2 * GROUP_W
N_PAIRS = S5_GROUPS // 2
TOK_PER_VREG = LANES // S5_GROUP_CH
TAB_ROWS = 24
ROUTE_LANES = 128
ROW_BLOCK = 512
PIECE = 16
VMEM_LIMIT = 56 * 1024 * 1024


def _ln(x):
    mu = jnp.mean(x, axis=-1, keepdims=True)
    xc = x - mu
    var = jnp.mean(xc * xc, axis=-1, keepdims=True)
    return xc * lax.rsqrt(var + LN_EPS)


def _sigmoid(x):
    return 0.5 * (jnp.tanh(0.5 * x) + 1.0)


def _dot(a, b):
    return jnp.dot(a, b, preferred_element_type=F32)


def _mods_kernel(c_ref, w_ref, b_ref, o_ref):
    c = c_ref[...]
    a = c * _sigmoid(c)
    o_ref[...] = jnp.dot(a, w_ref[...], precision=HI, preferred_element_type=F32) + b_ref[...]


def _mods(cc, w_ada, b_ada):
    n = w_ada.shape[1]
    nb = 1536
    return pl.pallas_call(
        _mods_kernel,
        grid=(n // nb,),
        in_specs=[pl.BlockSpec((8, D_MODEL), lambda i: (0, 0)),
                  pl.BlockSpec((D_MODEL, nb), lambda i: (0, i)),
                  pl.BlockSpec((1, nb), lambda i: (0, i))],
        out_specs=pl.BlockSpec((8, nb), lambda i: (0, i)),
        out_shape=jax.ShapeDtypeStruct((8, n), F32),
        compiler_params=pltpu.CompilerParams(vmem_limit_bytes=VMEM_LIMIT),
        name="mods",
    )(cc, w_ada, b_ada.reshape(1, n))


def _slot_masks(rows):
    slot = lax.broadcasted_iota(jnp.int32, (rows, LANES), 1) // S5_GROUP_CH
    return [slot == s for s in range(TOK_PER_VREG)]


def _to_chunk_tile(u_scr, uc_ref):
    nch = uc_ref.shape[0]
    masks = _slot_masks(nch)
    for qh in range(CHUNK // TOK_PER_VREG):
        for v in range(S5_WIDTH // LANES):
            src = [u_scr[v, pl.ds(qh * TOK_PER_VREG + s, nch, stride=CHUNK), :] for s in range(TOK_PER_VREG)]
            for i in range(TOK_PER_VREG):
                acc = None
                for s in range(TOK_PER_VREG):
                    shift = ((s - i) * S5_GROUP_CH) % LANES
                    piece = pltpu.roll(src[s], shift, 1) if shift else src[s]
                    acc = piece if acc is None else jnp.where(masks[s], piece, acc)
                lo = (v * TOK_PER_VREG + i) * GROUP_W + qh * LANES
                uc_ref[:, lo:lo + LANES] = acc.astype(uc_ref.dtype)


def _from_chunk_tile(yc_ref, y_scr):
    nch = yc_ref.shape[0]
    masks = _slot_masks(nch)
    for qh in range(CHUNK // TOK_PER_VREG):
        for v in range(S5_WIDTH // LANES):
            src = []
            for i in range(TOK_PER_VREG):
                lo = (v * TOK_PER_VREG + i) * GROUP_W + qh * LANES
                src.append(yc_ref[:, lo:lo + LANES].astype(F32))
            for s in range(TOK_PER_VREG):
                acc = None
                for i in range(TOK_PER_VREG):
                    shift = ((i - s) * S5_GROUP_CH) % LANES
                    piece = pltpu.roll(src[i], shift, 1) if shift else src[i]
                    acc = piece if acc is None else jnp.where(masks[i], piece, acc)
                y_scr[v, pl.ds(qh * TOK_PER_VREG + s, nch, stride=CHUNK), :] = acc


def _with_positions(x_ref, remb_ref, cemb_ref):
    c = cemb_ref[...]
    slabs = []
    for j in range(x_ref.shape[0] // GRID_W):
        r = jnp.broadcast_to(remb_ref[j:j + 1, :], c.shape)
        slabs.append(x_ref[j * GRID_W:(j + 1) * GRID_W, :] + jnp.concatenate([r, c], axis=-1))
    return jnp.concatenate(slabs, axis=0)


def _in_proj_kernel(x_ref, remb_ref, cemb_ref, mod_ref, w_ref, *rest, full):
    if full:
        cw_ref, cwo_ref, uc_ref, siga_ref, sb_ref, u_scr = rest
    else:
        uc_ref, u_scr = rest
    xp = _with_positions(x_ref, remb_ref, cemb_ref)
    h = (_ln(xp) * mod_ref[1:2, :] + mod_ref[0:1, :]).astype(BF16)
    o1, o2, o3, o4, o5 = 512, 1024, 1536, 2048, 3072
    u = _dot(h, w_ref[:, 0:o1])
    for v in range(S5_WIDTH // LANES):
        u_scr[v] = u[:, v * LANES:(v + 1) * LANES]
    _to_chunk_tile(u_scr, uc_ref)
    if not full:
        return
    z_b = _dot(h, w_ref[:, o1:o2])
    gate_c = _dot(h, w_ref[:, o3:o4])
    p = gate_c * z_b
    tm = p.shape[0]
    col = lax.broadcasted_iota(jnp.int32, (tm, 1), 0) % GRID_W
    prev = jnp.where(col == 0, 0.0, pltpu.roll(p, 1, 0))
    nxt = jnp.where(col == GRID_W - 1, 0.0, pltpu.roll(p, tm - 1, 0))
    v = cw_ref[0:1, :] * prev + cw_ref[1:2, :] * p + cw_ref[2:3, :] * nxt
    gate_b = _dot(h, w_ref[:, o2:o3])
    out_b = _dot((gate_b * v).astype(BF16), cwo_ref[...])
    merge_b = _dot(h, w_ref[:, o5:])
    sb_ref[...] = (_sigmoid(merge_b) * out_b).astype(sb_ref.dtype)
    merge_a = _dot(h, w_ref[:, o4:o5])
    siga_ref[...] = _sigmoid(merge_a).astype(siga_ref.dtype)


def _pos_specs(tm, d):
    return [pl.BlockSpec((tm // GRID_W, d // 2), lambda i, j: (i, 0)),
            pl.BlockSpec((GRID_W, d // 2), lambda i, j: (0, 0))]


def _in_proj(x, r_emb, c_emb, mod, w_in_bf, conv_w, conv_w_out_bf, tm, full):
    b, l, d = x.shape
    n_tiles = l // tm
    grid = (n_tiles, b)
    tok = lambda w: pl.BlockSpec((None, tm, w), lambda i, j: (j, i, 0))
    chunk_spec = pl.BlockSpec((tm // CHUNK, CHUNK * S5_WIDTH), lambda i, j: (j * n_tiles + i, 0))
    chunk_shape = jax.ShapeDtypeStruct((b * l // CHUNK, CHUNK * S5_WIDTH), BF16)
    in_specs = [tok(d)] + _pos_specs(tm, d) + [pl.BlockSpec((None, 2, d), lambda i, j: (j, 0, 0))]
    args = [x, r_emb, c_emb, mod]
    if full:
        in_specs += [pl.BlockSpec(w_in_bf.shape, lambda i, j: (0, 0)),
                     pl.BlockSpec(conv_w.shape, lambda i, j: (0, 0)),
                     pl.BlockSpec(conv_w_out_bf.shape, lambda i, j: (0, 0))]
        args += [w_in_bf, conv_w, conv_w_out_bf]
        out_specs = [chunk_spec, tok(d), tok(d)]
        out_shape = [chunk_shape,
                     jax.ShapeDtypeStruct((b, l, d), BF16),
                     jax.ShapeDtypeStruct((b, l, d), BF16)]
    else:
        in_specs += [pl.BlockSpec((d, S5_WIDTH), lambda i, j: (0, 0))]
        args += [w_in_bf]
        out_specs = [chunk_spec]
        out_shape = [chunk_shape]
    return pl.pallas_call(
        functools.partial(_in_proj_kernel, full=full),
        grid=grid, in_specs=in_specs, out_specs=out_specs, out_shape=out_shape,
        scratch_shapes=[pltpu.VMEM((S5_WIDTH // LANES, tm, LANES), F32)],
        compiler_params=pltpu.CompilerParams(
            dimension_semantics=("arbitrary", "arbitrary"), vmem_limit_bytes=VMEM_LIMIT),
        name="in_proj" if full else "in_proj_ctx",
    )(*args)


def _s5_dir_tables(log_dt, a_re, a_im, b_re, b_im, c_re, c_im):
    f32 = F32
    dt = jnp.exp(log_dt.astype(f32))[:, None]
    a_re = a_re.astype(f32)
    a_im = a_im.astype(f32)
    mag = jnp.exp(dt * a_re)
    ab_re = mag * jnp.cos(dt * a_im)
    ab_im = mag * jnp.sin(dt * a_im)
    den = a_re * a_re + a_im * a_im
    x_re = ab_re - 1.0
    f_re = (x_re * a_re + ab_im * a_im) / den
    f_im = (ab_im * a_re - x_re * a_im) / den
    b_re = b_re.astype(f32)
    b_im = b_im.astype(f32)
    bb_re = f_re[..., None] * b_re - f_im[..., None] * b_im
    bb_im = f_re[..., None] * b_im + f_im[..., None] * b_re
    k = jnp.arange(CHUNK + 1, dtype=f32)[:, None, None]
    pmag = jnp.exp(k * (dt * a_re)[None])
    p_re = pmag * jnp.cos(k * (dt * a_im)[None])
    p_im = pmag * jnp.sin(k * (dt * a_im)[None])
    pb_re = p_re[..., None] * bb_re[None] - p_im[..., None] * bb_im[None]
    pb_im = p_re[..., None] * bb_im[None] + p_im[..., None] * bb_re[None]
    c_re = c_re.astype(f32)
    c_im = c_im.astype(f32)
    cp_re = c_re[None] * p_re[:, :, None, :] - c_im[None] * p_im[:, :, None, :]
    cp_im = -(c_re[None] * p_im[:, :, None, :] + c_im[None] * p_re[:, :, None, :])
    return dict(p_re=p_re, p_im=p_im, pb_re=pb_re, pb_im=pb_im, cp_re=cp_re, cp_im=cp_im,
                bb_re=bb_re, bb_im=bb_im)


def _lag_kernels(f, bk):
    g, n, c = S5_GROUPS, S5_STATE, S5_GROUP_CH
    k = CHUNK + 1
    lhs = jnp.stack([jnp.concatenate([t['cp_re'], t['cp_im']], -1) for t in (f, bk)], 0)
    lhs = lhs.transpose(0, 2, 1, 3, 4).reshape(2 * g, k * c, 2 * n)
    rhs = jnp.stack([jnp.concatenate([t['bb_re'], t['bb_im']], 1) for t in (f, bk)], 0)
    out = jnp.einsum('bmn,bnc->bmc', lhs, rhs.reshape(2 * g, 2 * n, c), precision=HI)
    out = out.reshape(2, g, k, c, c).transpose(0, 2, 1, 4, 3)
    return out[0], out[1]


def _s5_operators(f, bk, s5_d):
    q = CHUNK
    g, n, c = S5_GROUPS, S5_STATE, S5_GROUP_CH
    kern_f, kern_b = _lag_kernels(f, bk)
    k0 = kern_f[0] + kern_b[0] + s5_d.astype(F32)[:, :, None] * jnp.eye(c, dtype=F32)[None]
    kc = jnp.concatenate([kern_b[1:q][::-1], k0[None], kern_f[1:q]], 0)
    qi = np.arange(q)
    toeplitz = np.eye(2 * q - 1, dtype=np.float32)[qi[None, :] - qi[:, None] + q - 1]
    m_intra = jnp.einsum('iok,kgcd->gicod', toeplitz, kc, precision=HI).reshape(g, q * c, q * c)
    wf_re = f['pb_re'][:q][::-1]
    wf_im = f['pb_im'][:q][::-1]
    wb_re = bk['pb_re'][:q]
    wb_im = bk['pb_im'][:q]
    w_st = jnp.stack([wf_re, wf_im, wb_re, wb_im], 0)
    w_st = w_st.transpose(2, 1, 4, 0, 3).reshape(g, q * c, 4, n)
    of_re = f['cp_re'][1:]
    of_im = f['cp_im'][1:]
    ob_re = bk['cp_re'][1:][::-1]
    ob_im = bk['cp_im'][1:][::-1]
    w_out = jnp.stack([of_re, of_im, ob_re, ob_im], 0)
    w_out = w_out.transpose(2, 0, 4, 1, 3).reshape(g, 4, n, q * c)
    np_ = N_PAIRS
    w_st = w_st.astype(BF16).reshape(np_, 2, q * c, 4, n)
    ws_pair = jnp.concatenate([jnp.pad(w_st[:, 0], ((0, 0), (0, 0), (0, 0), (0, n))),
                               jnp.pad(w_st[:, 1], ((0, 0), (0, 0), (0, 0), (n, 0)))], 1)
    ws_pair = ws_pair.reshape(np_, PAIR_W, 4 * 2 * n)
    w_out = w_out.astype(BF16).reshape(np_, 2, 4, n, q * c)
    wo_pair = jnp.stack([jnp.pad(w_out[:, 0], ((0, 0), (0, 0), (0, 0), (0, q * c))),
                         jnp.pad(w_out[:, 1], ((0, 0), (0, 0), (0, 0), (q * c, 0)))], 2)
    wo_pair = wo_pair.reshape(np_, 4 * 2 * n, PAIR_W)
    tab = jnp.concatenate([_chunk_power_table(f, False), _chunk_power_table(bk, True)], 0)
    tab = tab.reshape(2 * TAB_ROWS, np_, 2 * n).transpose(1, 0, 2)
    return m_intra.astype(BF16), ws_pair, wo_pair, tab


def _chunk_power_table(t, backward):
    def cmul(x, y):
        return x[0] * y[0] - x[1] * y[1], x[0] * y[1] + x[1] * y[0]
    p1 = (t['p_re'][CHUNK], t['p_im'][CHUNK])
    p2 = cmul(p1, p1)
    p4 = cmul(p2, p2)
    p8 = cmul(p4, p4)
    pr = [(jnp.ones_like(p1[0]), jnp.zeros_like(p1[0]))]
    for _ in range(SUBLANES - 1):
        pr.append(cmul(pr[-1], p1))
    if backward:
        pr = pr[::-1]
    rows = [p[0] for p in pr] + [p[1] for p in pr]
    for p in (p1, p2, p4, p8):
        rows += [p[0], p[1]]
    return jnp.stack(rows, 0)


def _s5_scan_kernel(uc_ref, mi_ref, ws_ref, wo_ref, tab_ref, s0_ref, y_ref, fin_ref, s_scr, in_scr, *, batch):
    rows = uc_ref.shape[0]
    chunks = rows // batch
    n_tiles = chunks // SUBLANES
    u = uc_ref[...]
    s_scr[...] = _dot(u, ws_ref[...])
    row = lax.broadcasted_iota(jnp.int32, (SUBLANES, LANES), 0)

    def tile_scan(r0, backward, c_re, c_im):
        base = TAB_ROWS if backward else 0
        col = 2 * LANES if backward else 0
        rs = pl.ds(r0, SUBLANES)

        def shift(z, k):
            if backward:
                return jnp.where(row < SUBLANES - k, pltpu.roll(z, SUBLANES - k, 0), 0.0)
            return jnp.where(row >= k, pltpu.roll(z, k, 0), 0.0)

        z_re = s_scr[rs, col:col + LANES]
        z_im = s_scr[rs, col + LANES:col + 2 * LANES]
        for k, t in ((1, 16), (2, 18), (4, 20)):
            a_re = tab_ref[base + t:base + t + 1, :]
            a_im = tab_ref[base + t + 1:base + t + 2, :]
            sh_re = shift(z_re, k)
            sh_im = shift(z_im, k)
            z_re, z_im = z_re + (a_re * sh_re - a_im * sh_im), z_im + (a_re * sh_im + a_im * sh_re)
        pr_re = tab_ref[base:base + SUBLANES, :]
        pr_im = tab_ref[base + SUBLANES:base + 2 * SUBLANES, :]
        in_scr[rs, col:col + LANES] = pr_re * c_re - pr_im * c_im + shift(z_re, 1)
        in_scr[rs, col + LANES:col + 2 * LANES] = pr_re * c_im + pr_im * c_re + shift(z_im, 1)
        last = 0 if backward else SUBLANES - 1
        l_re = jnp.broadcast_to(z_re[last:last + 1, :], (SUBLANES, LANES))
        l_im = jnp.broadcast_to(z_im[last:last + 1, :], (SUBLANES, LANES))
        p8_re = tab_ref[base + 22:base + 23, :]
        p8_im = tab_ref[base + 23:base + 24, :]
        return p8_re * c_re - p8_im * c_im + l_re, p8_re * c_im + p8_im * c_re + l_im

    def body(m, carry):
        out = []
        for b in range(batch):
            cf_re, cf_im, cb_re, cb_im = carry[4 * b:4 * b + 4]
            rf = pl.multiple_of(b * chunks + m * SUBLANES, SUBLANES)
            rb = pl.multiple_of(b * chunks + (n_tiles - 1 - m) * SUBLANES, SUBLANES)
            out += list(tile_scan(rf, False, cf_re, cf_im))
            out += list(tile_scan(rb, True, cb_re, cb_im))
        return tuple(out)

    init = tuple(jnp.broadcast_to(s0_ref[t, b:b + 1, :], (SUBLANES, LANES))
                 for b in range(batch) for t in range(4))
    fin = lax.fori_loop(0, n_tiles, body, init, unroll=min(4, n_tiles))
    fin_ref[...] = jnp.zeros(fin_ref.shape, F32)
    for b in range(batch):
        for t in range(4):
            fin_ref[t, b:b + 1, :] = fin[4 * b + t][0:1, :]
    y_intra = jnp.concatenate([_dot(u[:, gl * GROUP_W:(gl + 1) * GROUP_W], mi_ref[gl]) for gl in range(2)], axis=-1)
    y = y_intra + _dot(in_scr[...].astype(BF16), wo_ref[...])
    y_ref[...] = y.astype(y_ref.dtype)


def _s5_scan(uc, mi, ws, wo, tab, s0, batch):
    rows = uc.shape[0]
    pair = lambda *shape: pl.BlockSpec((None,) + shape, lambda p: (p,) + (0,) * len(shape))
    return pl.pallas_call(
        functools.partial(_s5_scan_kernel, batch=batch),
        grid=(N_PAIRS,),
        in_specs=[pl.BlockSpec((rows, PAIR_W), lambda p: (0, p)),
                  pl.BlockSpec((2, GROUP_W, GROUP_W), lambda p: (p, 0, 0)),
                  pair(PAIR_W, PAIR_W), pair(PAIR_W, PAIR_W),
                  pair(2 * TAB_ROWS, LANES), pair(4, SUBLANES, LANES)],
        out_specs=[pl.BlockSpec((rows, PAIR_W), lambda p: (0, p)), pair(4, SUBLANES, LANES)],
        out_shape=[jax.ShapeDtypeStruct((rows, N_PAIRS * PAIR_W), BF16),
                   jax.ShapeDtypeStruct((N_PAIRS, 4, SUBLANES, LANES), F32)],
        scratch_shapes=[pltpu.VMEM((rows, PAIR_W), F32), pltpu.VMEM((rows, PAIR_W), F32)],
        compiler_params=pltpu.CompilerParams(
            dimension_semantics=("arbitrary",), vmem_limit_bytes=VMEM_LIMIT),
        name="s5_scan",
    )(uc, mi, ws, wo, tab, s0)


def _mix_out_kernel(y_ref, siga_ref, sb_ref, x_ref, remb_ref, cemb_ref, mod_ref, wv_ref, wg_ref, wo_ref,
                    ln1_ref, wr_ref, br_ref, ut_ref, lt_ref,
                    x1_ref, xs_ref, route_ref, len_ref, y_scr):
    _from_chunk_tile(y_ref, y_scr)
    y = jnp.concatenate([y_scr[v] for v in range(S5_WIDTH // LANES)], axis=-1)
    ya = (0.5 * y * (1.0 + jnp.tanh(math.sqrt(2.0 / math.pi) * (y + 0.044715 * (y * y * y))))).astype(BF16)
    out_a = _dot(ya, wv_ref[...]) * _sigmoid(_dot(ya, wg_ref[...]))
    merged = siga_ref[...].astype(F32) * out_a + sb_ref[...].astype(F32)
    mix = _dot(merged.astype(BF16), wo_ref[...])
    xp = _with_positions(x_ref, remb_ref, cemb_ref)
    x1 = _ln(ALPHA * xp + mod_ref[0:1, :] * mix) * ln1_ref[0:1, :] + ln1_ref[1:2, :]
    x1_ref[...] = x1
    h2 = _ln(x1) * mod_ref[2:3, :] + mod_ref[1:2, :]
    _route_and_sort(h2, wr_ref, br_ref, ut_ref, lt_ref, xs_ref, route_ref, len_ref)


def _route_and_sort(h2, wr_ref, br_ref, ut_ref, lt_ref, xs_ref, route_ref, len_ref):
    tm = h2.shape[0]
    f32 = F32
    h_hi = h2.astype(BF16)
    nt = (((1,), (1,)), ((), ()))
    lg = lax.dot_general(wr_ref[...], h_hi, nt, preferred_element_type=f32) + br_ref[...]
    rowi = lax.broadcasted_iota(jnp.int32, (ROUTE_LANES, tm), 0).astype(f32)
    neg = jnp.float32(-jnp.inf)
    big = jnp.float32(ROUTE_LANES)
    gl = jnp.where(rowi < N_EXPERT_GROUPS, lg, neg)
    gmax = jnp.max(gl, axis=0, keepdims=True)
    g_idx = jnp.min(jnp.where(gl == gmax, rowi, big), axis=0, keepdims=True)
    p_group = 1.0 / jnp.sum(jnp.exp(gl - gmax), axis=0, keepdims=True)
    e_lo = N_EXPERT_GROUPS + g_idx * EXPERTS_PER_GROUP
    el = jnp.where((rowi >= e_lo) & (rowi < e_lo + EXPERTS_PER_GROUP), lg, neg)
    m1 = jnp.max(el, axis=0, keepdims=True)
    i1 = jnp.min(jnp.where(el == m1, rowi, big), axis=0, keepdims=True)
    el2 = jnp.where(rowi == i1, neg, el)
    m2 = jnp.max(el2, axis=0, keepdims=True)
    i2 = jnp.min(jnp.where(el2 == m2, rowi, big), axis=0, keepdims=True)
    r = jnp.exp(m2 - m1)
    w1 = p_group / (1.0 + r)
    w2 = p_group * r / (1.0 + r)
    e1 = i1 - N_EXPERT_GROUPS
    e2 = i2 - N_EXPERT_GROUPS
    a12 = jnp.where(rowi == e1, 1.0, 0.0) + jnp.where(rowi == e2 + N_EXPERTS, 1.0, 0.0)
    rank = _dot(a12.astype(BF16), ut_ref[...])
    cnt = jnp.broadcast_to(jnp.sum(a12, axis=1, keepdims=True), (ROUTE_LANES, LANES))
    row = lax.broadcasted_iota(jnp.int32, (ROUTE_LANES, LANES), 0)
    tot = cnt + pltpu.roll(cnt, ROUTE_LANES - N_EXPERTS, 0)
    run = jnp.where(row < N_EXPERTS, jnp.floor((tot + (PIECE - 1)) * (1.0 / PIECE)), 0.0)
    off = PIECE * _dot(lt_ref[...], run.astype(BF16))
    base = jnp.where(row < N_EXPERTS, off, pltpu.roll(off + cnt, N_EXPERTS, 0))
    posmat = a12 * (rank + base[:, 0:1])
    pos1 = jnp.sum(posmat[0:N_EXPERTS, :], axis=0, keepdims=True)
    pos2 = jnp.sum(posmat[N_EXPERTS:2 * N_EXPERTS, :], axis=0, keepdims=True)
    ri = lax.broadcasted_iota(jnp.int32, (xs_ref.shape[0], tm), 0).astype(f32)
    perm = jnp.where((ri == pos1) | (ri == pos2), 1.0, 0.0).astype(BF16)
    xs_ref[...] = _dot(perm, h_hi).astype(xs_ref.dtype)
    rec = jnp.where(rowi == 0, e1, jnp.where(rowi == 1, e2, jnp.where(rowi == 2, w1, jnp.where(
        rowi == 3, w2, jnp.where(rowi == 4, pos1, jnp.where(rowi == 5, pos2, 0.0))))))
    route_ref[...] = rec.T
    len_ref[...] = run


def _local_rows(tm):
    return TOP_K * tm + N_EXPERTS * PIECE


def _mix_out(y, siga, sb, x, r_emb, c_emb, mod, wv, wg, wo, ln1, wr, br, tm):
    b, l, d = x.shape
    n_tiles = l // tm
    xs_rows = _local_rows(tm)
    tok = lambda w: pl.BlockSpec((None, tm, w), lambda i, j: (j, i, 0))
    whole = lambda a: pl.BlockSpec(a.shape, lambda i, j: (0,) * a.ndim)
    ut = (jnp.arange(tm)[:, None] < jnp.arange(tm)[None, :]).astype(BF16)
    lt = (jnp.arange(ROUTE_LANES)[None, :] < jnp.arange(ROUTE_LANES)[:, None]).astype(BF16)
    return pl.pallas_call(
        _mix_out_kernel,
        grid=(n_tiles, b),
        in_specs=[pl.BlockSpec((tm // CHUNK, CHUNK * S5_WIDTH), lambda i, j: (j * n_tiles + i, 0)),
                  tok(d), tok(d), tok(d)] + _pos_specs(tm, d) + [
                  pl.BlockSpec((None, 4, d), lambda i, j: (j, 0, 0)),
                  whole(wv), whole(wg), whole(wo), whole(ln1), whole(wr), whole(br),
                  whole(ut), whole(lt)],
        out_specs=[tok(d),
                   pl.BlockSpec((xs_rows, d), lambda i, j: (j * n_tiles + i, 0)),
                   tok(ROUTE_LANES),
                   pl.BlockSpec((None, ROUTE_LANES, LANES), lambda i, j: (j * n_tiles + i, 0, 0))],
        out_shape=[jax.ShapeDtypeStruct((b, l, d), F32),
                   jax.ShapeDtypeStruct((b * n_tiles * xs_rows, d), BF16),
                   jax.ShapeDtypeStruct((b, l, ROUTE_LANES), F32),
                   jax.ShapeDtypeStruct((b * n_tiles, ROUTE_LANES, LANES), F32)],
        scratch_shapes=[pltpu.VMEM((S5_WIDTH // LANES, tm, LANES), F32)],
        compiler_params=pltpu.CompilerParams(
            dimension_semantics=("arbitrary", "arbitrary"), vmem_limit_bytes=VMEM_LIMIT),
        name="mix_out",
    )(y, siga, sb, x, r_emb, c_emb, mod, wv, wg, wo, ln1, wr, br, ut, lt)


def _piece_copy(src_hbm, src_row, dst, piece, sem):
    return pltpu.make_async_copy(src_hbm.at[pl.ds(pl.multiple_of(src_row, PIECE), PIECE), :],
                                 dst.at[pl.ds(pl.multiple_of(piece * PIECE, PIECE), PIECE), :], sem)


def _issue_pieces(src_hbm, table_ref, first, n_pieces, dst, sem):
    for p in range(n_pieces):
        _piece_copy(src_hbm, table_ref[first + p], dst, p, sem).start()


def _wait_pieces(src_hbm, dst, sem):
    pltpu.make_async_copy(src_hbm.at[pl.ds(0, dst.shape[0]), :], dst, sem).wait()


def _experts_kernel(be_ref, piece_ref, nused_ref, xs_hbm, wg_ref, wu_ref, wd_ref, ys_ref,
                    xs_buf, wg_bf, wu_bf, wd_bf, sem):
    i = pl.program_id(0)
    n_used = nused_ref[0]
    slot = i % 2
    per_block = ROW_BLOCK // PIECE

    @pl.when(i == 0)
    def _():
        _issue_pieces(xs_hbm, piece_ref, 0, per_block, xs_buf.at[0], sem.at[0])

    @pl.when(i < n_used)
    def _():
        @pl.when((i == 0) | (be_ref[i] != be_ref[jnp.maximum(i - 1, 0)]))
        def _():
            wg_bf[...] = wg_ref[...].astype(BF16)
            wu_bf[...] = wu_ref[...].astype(BF16)
            wd_bf[...] = wd_ref[...].astype(BF16)

        _wait_pieces(xs_hbm, xs_buf.at[slot], sem.at[slot])
        nxt = jnp.minimum(i + 1, n_used - 1)
        _issue_pieces(xs_hbm, piece_ref, nxt * per_block, per_block, xs_buf.at[1 - slot], sem.at[1 - slot])

        xb = xs_buf[slot]
        gate = _dot(xb, wg_bf[...])
        up = _dot(xb, wu_bf[...])
        hid = (gate * _sigmoid(gate) * up).astype(BF16)
        ys_ref[...] = _dot(hid, wd_bf[...]).astype(ys_ref.dtype)

        @pl.when(i == n_used - 1)
        def _():
            _wait_pieces(xs_hbm, xs_buf.at[1 - slot], sem.at[1 - slot])

    @pl.when(i >= n_used)
    def _():
        ys_ref[...] = jnp.zeros(ys_ref.shape, ys_ref.dtype)


def _experts(block_e, piece_src, n_used, xs, wg, wu, wd, n_blocks):
    d = xs.shape[1]
    by_expert = lambda i, be, ps, nu: (0, be[i], 0, 0)
    grid_spec = pltpu.PrefetchScalarGridSpec(
        num_scalar_prefetch=3,
        grid=(n_blocks,),
        in_specs=[pl.BlockSpec(memory_space=pl.ANY),
                  pl.BlockSpec((None, None, d, EXPERT_FF), by_expert),
                  pl.BlockSpec((None, None, d, EXPERT_FF), by_expert),
                  pl.BlockSpec((None, None, EXPERT_FF, d), by_expert)],
        out_specs=pl.BlockSpec((ROW_BLOCK, d), lambda i, be, ps, nu: (i, 0)),
        scratch_shapes=[pltpu.VMEM((2, ROW_BLOCK, d), BF16),
                        pltpu.VMEM((d, EXPERT_FF), BF16), pltpu.VMEM((d, EXPERT_FF), BF16),
                        pltpu.VMEM((EXPERT_FF, d), BF16), pltpu.SemaphoreType.DMA((2,))],
    )
    return pl.pallas_call(
        _experts_kernel,
        grid_spec=grid_spec,
        out_shape=jax.ShapeDtypeStruct((n_blocks * ROW_BLOCK, d), BF16),
        compiler_params=pltpu.CompilerParams(
            dimension_semantics=("arbitrary",), vmem_limit_bytes=VMEM_LIMIT),
        name="experts",
    )(block_e, piece_src, n_used, xs, wg, wu, wd)


def _combine_kernel(piece_ref, ys_hbm, x1_ref, route_ref, mod_ref, ln2_ref, o_ref, buf, sem):
    i = pl.program_id(0)
    n = pl.num_programs(0)
    slot = i % 2
    rows = buf.shape[1]
    per_tile = rows // PIECE

    @pl.when(i == 0)
    def _():
        _issue_pieces(ys_hbm, piece_ref, 0, per_tile, buf.at[0], sem.at[0])

    _wait_pieces(ys_hbm, buf.at[slot], sem.at[slot])
    nxt = jnp.minimum(i + 1, n - 1)
    _issue_pieces(ys_hbm, piece_ref, nxt * per_tile, per_tile, buf.at[1 - slot], sem.at[1 - slot])

    tm = x1_ref.shape[0]
    lane = lax.broadcasted_iota(jnp.int32, (tm, rows), 1).astype(F32)
    sel = (jnp.where(lane == route_ref[:, 4:5], route_ref[:, 2:3], 0.0)
           + jnp.where(lane == route_ref[:, 5:6], route_ref[:, 3:4], 0.0)).astype(BF16)
    moe = _dot(sel, buf[slot])
    z = ALPHA * x1_ref[...] + mod_ref[0:1, :] * moe
    o_ref[...] = _ln(z) * ln2_ref[0:1, :] + ln2_ref[1:2, :]

    @pl.when(i == n - 1)
    def _():
        _wait_pieces(ys_hbm, buf.at[1 - slot], sem.at[1 - slot])


def _combine(piece_glob, ys, x1, route, mod, ln2, tm, tiles_per_batch):
    t, d = x1.shape
    grid_spec = pltpu.PrefetchScalarGridSpec(
        num_scalar_prefetch=1,
        grid=(t // tm,),
        in_specs=[pl.BlockSpec(memory_space=pl.ANY),
                  pl.BlockSpec((tm, d), lambda i, pg: (i, 0)),
                  pl.BlockSpec((tm, ROUTE_LANES), lambda i, pg: (i, 0)),
                  pl.BlockSpec((None, 8, d), lambda i, pg: (i // tiles_per_batch, 0, 0)),
                  pl.BlockSpec((2, d), lambda i, pg: (0, 0))],
        out_specs=pl.BlockSpec((tm, d), lambda i, pg: (i, 0)),
        scratch_shapes=[pltpu.VMEM((2, _local_rows(tm), d), BF16), pltpu.SemaphoreType.DMA((2,))],
    )
    return pl.pallas_call(
        _combine_kernel,
        grid_spec=grid_spec,
        out_shape=jax.ShapeDtypeStruct((t, d), F32),
        compiler_params=pltpu.CompilerParams(
            dimension_semantics=("arbitrary",), vmem_limit_bytes=VMEM_LIMIT),
        name="combine",
    )(piece_glob, ys, x1, route, mod, ln2)


def _sincos_2d(rows, cols, dim):
    q = dim // 4
    omega = 1.0 / (POS_BASE ** (jnp.arange(q, dtype=F32) / q))
    r = jnp.arange(rows, dtype=F32)[:, None] * omega
    cl = jnp.arange(cols, dtype=F32)[:, None] * omega
    r_emb = jnp.concatenate([jnp.sin(r), jnp.cos(r)], -1)
    c_emb = jnp.concatenate([jnp.sin(cl), jnp.cos(cl)], -1)
    return r_emb, c_emb


def _routing_tables(run_pieces, xs_rows, n_blocks):
    i32 = jnp.int32
    n_tiles = run_pieces.shape[0]
    ppb = ROW_BLOCK // PIECE
    loc_start = jnp.cumsum(run_pieces, axis=1) - run_pieces
    seg_tot = jnp.sum(run_pieces, axis=0)
    seg_pad = (seg_tot + ppb - 1) // ppb * ppb
    seg_end = jnp.cumsum(seg_pad)
    seg_start = seg_end - seg_pad
    run_t = run_pieces.T
    glob_start = seg_start[:, None] + jnp.cumsum(run_t, axis=1) - run_t
    n_used = (seg_end[-1] // ppb).astype(i32)
    blk = jnp.minimum(jnp.arange(n_blocks, dtype=i32), n_used - 1)
    block_e = jnp.minimum(jnp.sum((seg_end[None, :] <= (blk * ppb)[:, None]).astype(i32), axis=1),
                          N_EXPERTS - 1).astype(i32)
    lpt = xs_rows // PIECE
    starts = glob_start.reshape(-1)
    lens = run_t.reshape(-1)
    src0 = (jnp.arange(n_tiles, dtype=i32)[None, :] * lpt + loc_start.T).reshape(-1)
    p = jnp.arange(n_blocks * ppb, dtype=i32)
    within = p[:, None] - starts[None, :]
    hit = (within >= 0) & (within < lens[None, :])
    piece_src = jnp.sum(jnp.where(hit, (src0[None, :] + within) * PIECE, 0), axis=1).astype(i32)
    s = jnp.arange(lpt, dtype=i32)
    loc_within = s[None, :, None] - loc_start[:, None, :]
    hit = (loc_within >= 0) & (loc_within < run_pieces[:, None, :])
    piece_glob = jnp.sum(jnp.where(hit, (glob_start.T[:, None, :] + loc_within) * PIECE, 0), axis=2)
    return block_e, piece_src, piece_glob.astype(i32).reshape(-1), n_used.reshape(1)


def kernel(x, c, ctx, c_ctx, w_ada, b_ada, w_in, s5_log_dt_f, s5_a_re_f, s5_a_im_f, s5_b_re_f, s5_b_im_f, s5_c_re_f, s5_c_im_f, s5_log_dt_b, s5_a_re_b, s5_a_im_b, s5_b_re_b, s5_b_im_b, s5_c_re_b, s5_c_im_b, s5_d, s5_w_glu_val, s5_w_glu_gate, conv_w, conv_w_out, w_o, ln1_g, ln1_b, router_w_group, router_b_group, router_w_expert, router_b_expert, exp_w_gate, exp_w_up, exp_w_down, ln2_g, ln2_b):
    b, l, d = x.shape
    lc = ctx.shape[1]
    assert d == D_MODEL and b < SUBLANES and w_ada.shape[0] == DEPTH
    assert l % (SUBLANES * CHUNK) == 0 and lc % (SUBLANES * CHUNK) == 0 and l % GRID_W == 0
    t = b * l
    tm = min(512, l)
    tmc = min(512, lc)

    cc = jnp.concatenate([c, c_ctx[None, :], jnp.zeros((8 - b - 1, d), F32)], 0)
    mods = _mods(cc, w_ada[0], b_ada[0])
    sh1, sc1, g1, sh2, sc2, g2 = jnp.split(mods, 6, axis=-1)
    mod_a = jnp.stack([sh1[:b], 1.0 + sc1[:b]], 1)
    mod_ctx = jnp.broadcast_to(jnp.stack([sh1[b], 1.0 + sc1[b]], 0)[None], (b, 2, d))
    mod_c = jnp.stack([g1[:b], sh2[:b], 1.0 + sc2[:b], jnp.zeros((b, d), F32)], 1)
    mod_f = jnp.concatenate([g2[:b, None, :], jnp.zeros((b, 7, d), F32)], 1)

    w_in_bf = w_in[0].astype(BF16)
    f_tab = _s5_dir_tables(s5_log_dt_f[0], s5_a_re_f[0], s5_a_im_f[0], s5_b_re_f[0], s5_b_im_f[0],
                           s5_c_re_f[0], s5_c_im_f[0])
    b_tab = _s5_dir_tables(s5_log_dt_b[0], s5_a_re_b[0], s5_a_im_b[0], s5_b_re_b[0], s5_b_im_b[0],
                           s5_c_re_b[0], s5_c_im_b[0])
    mi, ws, wo_s5, tab = _s5_operators(f_tab, b_tab, s5_d[0])

    (uc_ctx,) = _in_proj(ctx, jnp.zeros((lc // GRID_W, d // 2), F32), jnp.zeros((GRID_W, d // 2), F32),
                         mod_ctx, w_in_bf, None, None, tmc, False)
    zero_state = jnp.zeros((N_PAIRS, 4, SUBLANES, LANES), F32)
    _, s0 = _s5_scan(uc_ctx, mi, ws, wo_s5, tab, zero_state, b)

    r_emb, c_emb = _sincos_2d(l // GRID_W, GRID_W, d)
    uc, siga, sb = _in_proj(x, r_emb, c_emb, mod_a, w_in_bf, conv_w[0], conv_w_out[0].astype(BF16), tm, True)
    y, _ = _s5_scan(uc, mi, ws, wo_s5, tab, s0, b)

    wr = jnp.concatenate([router_w_group[0], router_w_expert[0],
                          jnp.zeros((d, ROUTE_LANES - N_EXPERT_GROUPS - N_EXPERTS), F32)], 1).T.astype(BF16)
    br = jnp.concatenate([router_b_group[0], router_b_expert[0],
                          jnp.zeros((ROUTE_LANES - N_EXPERT_GROUPS - N_EXPERTS,), F32)])[:, None]
    ln1 = jnp.stack([ln1_g[0], ln1_b[0]], 0)
    x1, xs, route, run_len = _mix_out(y, siga, sb, x, r_emb, c_emb, mod_c,
                                      s5_w_glu_val[0].astype(BF16), s5_w_glu_gate[0].astype(BF16),
                                      w_o[0].astype(BF16), ln1, wr, br, tm)

    x1 = x1.reshape(t, d)
    route = route.reshape(t, ROUTE_LANES)
    n_tiles = t // tm
    xs_rows = _local_rows(tm)
    run_pieces = run_len[:, :N_EXPERTS, 0].astype(jnp.int32)
    max_rows = t * TOP_K + n_tiles * N_EXPERTS * (PIECE - 1) + N_EXPERTS * (ROW_BLOCK - 1)
    n_blocks = -(-max_rows // ROW_BLOCK)
    block_e, piece_src, piece_glob, n_used = _routing_tables(run_pieces, xs_rows, n_blocks)
    ys = _experts(block_e, piece_src, n_used, xs, exp_w_gate, exp_w_up, exp_w_down, n_blocks)
    ln2 = jnp.stack([ln2_g[0], ln2_b[0]], 0)
    out = _combine(piece_glob, ys, x1, route, mod_f, ln2, tm, l // tm)
    return out.reshape(b, l, d)
```

```python
import functools
import math

import jax
import jax.numpy as jnp
from jax import lax
from jax.experimental import pallas as pl
from jax.experimental.pallas import tpu as pltpu

F32 = jnp.float32
BF16 = jnp.bfloat16
HI = lax.Precision.HIGHEST

D_MODEL = 1024
GRID_W = 64
S5_WIDTH = 512
S5_GROUP_CH = 16
S5_GROUPS = S5_WIDTH // S5_GROUP_CH
S5_STATE = 64
CONV_WIDTH = 512
N_EXPERT_GROUPS = 4
EXPERTS_PER_GROUP = 8
N_EXPERTS = N_EXPERT_GROUPS * EXPERTS_PER_GROUP
EXPERT_FF = 512
TOP_K = 2
DEPTH = 1
ALPHA = (2.0 * DEPTH) ** 0.25
LN_EPS = 1e-6
POS_BASE = 10000.0

LANES = 128
SUBLANES = 8
CHUNK = 16
GROUP_W = CHUNK * S5_GROUP_CH
PAIR_W = 2 * GROUP_W
N_PAIRS = S5_GROUPS // 2
TOK_PER_VREG = LANES // S5_GROUP_CH
TAB_ROWS = 24
ROUTE_LANES = 128
ROW_BLOCK = 512
PIECE = 16
VMEM_LIMIT = 56 * 1024 * 1024


def _ln(x):
    mu = jnp.mean(x, axis=-1, keepdims=True)
    xc = x - mu
    var = jnp.mean(xc * xc, axis=-1, keepdims=True)
    return xc * lax.rsqrt(var + LN_EPS)


def _sigmoid(x):
    return 0.5 * (jnp.tanh(0.5 * x) + 1.0)


def _dot(a, b):
    return jnp.dot(a, b, preferred_element_type=F32)


def _mods_kernel(c_ref, w_ref, b_ref, o_ref):
    c = c_ref[...]
    a = c * _sigmoid(c)
    o_ref[...] = jnp.dot(a, w_ref[...], precision=HI, preferred_element_type=F32) + b_ref[...]


def _mods(cc, w_ada, b_ada):
    n = w_ada.shape[1]
    nb = 1536
    return pl.pallas_call(
        _mods_kernel,
        grid=(n // nb,),
        in_specs=[pl.BlockSpec((8, D_MODEL), lambda i: (0, 0)),
                  pl.BlockSpec((D_MODEL, nb), lambda i: (0, i)),
                  pl.BlockSpec((1, nb), lambda i: (0, i))],
        out_specs=pl.BlockSpec((8, nb), lambda i: (0, i)),
        out_shape=jax.ShapeDtypeStruct((8, n), F32),
        compiler_params=pltpu.CompilerParams(vmem_limit_bytes=VMEM_LIMIT),
        name="mods",
    )(cc, w_ada, b_ada.reshape(1, n))


def _slot_masks(rows):
    slot = lax.broadcasted_iota(jnp.int32, (rows, LANES), 1) // S5_GROUP_CH
    return [slot == s for s in range(TOK_PER_VREG)]


def _to_chunk_tile(u_scr, uc_ref):
    nch = uc_ref.shape[0]
    masks = _slot_masks(nch)
    for qh in range(CHUNK // TOK_PER_VREG):
        for v in range(S5_WIDTH // LANES):
            src = [u_scr[v, pl.ds(qh * TOK_PER_VREG + s, nch, stride=CHUNK), :] for s in range(TOK_PER_VREG)]
            for i in range(TOK_PER_VREG):
                acc = None
                for s in range(TOK_PER_VREG):
                    shift = ((s - i) * S5_GROUP_CH) % LANES
                    piece = pltpu.roll(src[s], shift, 1) if shift else src[s]
                    acc = piece if acc is None else jnp.where(masks[s], piece, acc)
                lo = (v * TOK_PER_VREG + i) * GROUP_W + qh * LANES
                uc_ref[:, lo:lo + LANES] = acc.astype(uc_ref.dtype)


def _from_chunk_tile(yc_ref, y_scr):
    nch = yc_ref.shape[0]
    masks = _slot_masks(nch)
    for qh in range(CHUNK // TOK_PER_VREG):
        for v in range(S5_WIDTH // LANES):
            src = []
            for i in range(TOK_PER_VREG):
                lo = (v * TOK_PER_VREG + i) * GROUP_W + qh * LANES
                src.append(yc_ref[:, lo:lo + LANES].astype(F32))
            for s in range(TOK_PER_VREG):
                acc = None
                for i in range(TOK_PER_VREG):
                    shift = ((i - s) * S5_GROUP_CH) % LANES
                    piece = pltpu.roll(src[i], shift, 1) if shift else src[i]
                    acc = piece if acc is None else jnp.where(masks[i], piece, acc)
                y_scr[v, pl.ds(qh * TOK_PER_VREG + s, nch, stride=CHUNK), :] = acc


def _with_positions(x_ref, remb_ref, cemb_ref):
    c = cemb_ref[...]
    slabs = []
    for j in range(x_ref.shape[0] // GRID_W):
        r = jnp.broadcast_to(remb_ref[j:j + 1, :], c.shape)
        slabs.append(x_ref[j * GRID_W:(j + 1) * GRID_W, :] + jnp.concatenate([r, c], axis=-1))
    return jnp.concatenate(slabs, axis=0)


def _in_proj_kernel(x_ref, remb_ref, cemb_ref, mod_ref, w_ref, *rest, full):
    if full:
        cw_ref, cwo_ref, uc_ref, siga_ref, sb_ref, u_scr = rest
    else:
        uc_ref, u_scr = rest
    xp = _with_positions(x_ref, remb_ref, cemb_ref)
    h = (_ln(xp) * mod_ref[1:2, :] + mod_ref[0:1, :]).astype(BF16)
    o1, o2, o3, o4, o5 = 512, 1024, 1536, 2048, 3072
    u = _dot(h, w_ref[:, 0:o1])
    for v in range(S5_WIDTH // LANES):
        u_scr[v] = u[:, v * LANES:(v + 1) * LANES]
    _to_chunk_tile(u_scr, uc_ref)
    if not full:
        return
    z_b = _dot(h, w_ref[:, o1:o2])
    gate_c = _dot(h, w_ref[:, o3:o4])
    p = gate_c * z_b
    tm = p.shape[0]
    col = lax.broadcasted_iota(jnp.int32, (tm, 1), 0) % GRID_W
    prev = jnp.where(col == 0, 0.0, pltpu.roll(p, 1, 0))
    nxt = jnp.where(col == GRID_W - 1, 0.0, pltpu.roll(p, tm - 1, 0))
    v = cw_ref[0:1, :] * prev + cw_ref[1:2, :] * p + cw_ref[2:3, :] * nxt
    gate_b = _dot(h, w_ref[:, o2:o3])
    out_b = _dot((gate_b * v).astype(BF16), cwo_ref[...])
    merge_b = _dot(h, w_ref[:, o5:])
    sb_ref[...] = (_sigmoid(merge_b) * out_b).astype(sb_ref.dtype)
    merge_a = _dot(h, w_ref[:, o4:o5])
    siga_ref[...] = _sigmoid(merge_a).astype(siga_ref.dtype)


def _pos_specs(tm, d):
    return [pl.BlockSpec((tm // GRID_W, d // 2), lambda i, j: (i, 0)),
            pl.BlockSpec((GRID_W, d // 2), lambda i, j: (0, 0))]


def _in_proj(x, r_emb, c_emb, mod, w_in_bf, conv_w, conv_w_out_bf, tm, full):
    b, l, d = x.shape
    n_tiles = l // tm
    grid = (n_tiles, b)
    tok = lambda w: pl.BlockSpec((None, tm, w), lambda i, j: (j, i, 0))
    chunk_spec = pl.BlockSpec((tm // CHUNK, CHUNK * S5_WIDTH), lambda i, j: (j * n_tiles + i, 0))
    chunk_shape = jax.ShapeDtypeStruct((b * l // CHUNK, CHUNK * S5_WIDTH), BF16)
    in_specs = [tok(d)] + _pos_specs(tm, d) + [pl.BlockSpec((None, 2, d), lambda i, j: (j, 0, 0))]
    args = [x, r_emb, c_emb, mod]
    if full:
        in_specs += [pl.BlockSpec(w_in_bf.shape, lambda i, j: (0, 0)),
                     pl.BlockSpec(conv_w.shape, lambda i, j: (0, 0)),
                     pl.BlockSpec(conv_w_out_bf.shape, lambda i, j: (0, 0))]
        args += [w_in_bf, conv_w, conv_w_out_bf]
        out_specs = [chunk_spec, tok(d), tok(d)]
        out_shape = [chunk_shape,
                     jax.ShapeDtypeStruct((b, l, d), BF16),
                     jax.ShapeDtypeStruct((b, l, d), BF16)]
    else:
        in_specs += [pl.BlockSpec((d, S5_WIDTH), lambda i, j: (0, 0))]
        args += [w_in_bf]
        out_specs = [chunk_spec]
        out_shape = [chunk_shape]
    return pl.pallas_call(
        functools.partial(_in_proj_kernel, full=full),
        grid=grid, in_specs=in_specs, out_specs=out_specs, out_shape=out_shape,
        scratch_shapes=[pltpu.VMEM((S5_WIDTH // LANES, tm, LANES), F32)],
        compiler_params=pltpu.CompilerParams(
            dimension_semantics=("arbitrary", "arbitrary"), vmem_limit_bytes=VMEM_LIMIT),
        name="in_proj" if full else "in_proj_ctx",
    )(*args)


def _s5_dir_tables(log_dt, a_re, a_im, b_re, b_im, c_re, c_im):
    f32 = F32
    dt = jnp.exp(log_dt.astype(f32))[:, None]
    a_re = a_re.astype(f32)
    a_im = a_im.astype(f32)
    mag = jnp.exp(dt * a_re)
    ab_re = mag * jnp.cos(dt * a_im)
    ab_im = mag * jnp.sin(dt * a_im)
    den = a_re * a_re + a_im * a_im
    x_re = ab_re - 1.0
    f_re = (x_re * a_re + ab_im * a_im) / den
    f_im = (ab_im * a_re - x_re * a_im) / den
    b_re = b_re.astype(f32)
    b_im = b_im.astype(f32)
    bb_re = f_re[..., None] * b_re - f_im[..., None] * b_im
    bb_im = f_re[..., None] * b_im + f_im[..., None] * b_re
    k = jnp.arange(CHUNK + 1, dtype=f32)[:, None, None]
    pmag = jnp.exp(k * (dt * a_re)[None])
    p_re = pmag * jnp.cos(k * (dt * a_im)[None])
    p_im = pmag * jnp.sin(k * (dt * a_im)[None])
    pb_re = p_re[..., None] * bb_re[None] - p_im[..., None] * bb_im[None]
    pb_im = p_re[..., None] * bb_im[None] + p_im[..., None] * bb_re[None]
    c_re = c_re.astype(f32)
    c_im = c_im.astype(f32)
    cp_re = c_re[None] * p_re[:, :, None, :] - c_im[None] * p_im[:, :, None, :]
    cp_im = -(c_re[None] * p_im[:, :, None, :] + c_im[None] * p_re[:, :, None, :])
    return dict(p_re=p_re, p_im=p_im, pb_re=pb_re, pb_im=pb_im, cp_re=cp_re, cp_im=cp_im,
                bb_re=bb_re, bb_im=bb_im)


def _lag_kernels(f, bk):
    g, n, c = S5_GROUPS, S5_STATE, S5_GROUP_CH
    k = CHUNK + 1
    lhs = jnp.stack([jnp.concatenate([t['cp_re'], t['cp_im']], -1) for t in (f, bk)], 0)
    lhs = lhs.transpose(0, 2, 1, 3, 4).reshape(2 * g, k * c, 2 * n)
    rhs = jnp.stack([jnp.concatenate([t['bb_re'], t['bb_im']], 1) for t in (f, bk)], 0)
    out = jnp.einsum('bmn,bnc->bmc', lhs, rhs.reshape(2 * g, 2 * n, c), precision=HI)
    out = out.reshape(2, g, k, c, c).transpose(0, 2, 1, 4, 3)
    return out[0], out[1]


def _s5_operators(f, bk, s5_d):
    q = CHUNK
    g, n, c = S5_GROUPS, S5_STATE, S5_GROUP_CH
    kern_f, kern_b = _lag_kernels(f, bk)
    k0 = kern_f[0] + kern_b[0] + s5_d.astype(F32)[:, :, None] * jnp.eye(c, dtype=F32)[None]
    kc = jnp.concatenate([kern_b[1:q][::-1], k0[None], kern_f[1:q]], 0)
    kct = kc.transpose(1, 2, 0, 3)
    m_intra = jnp.stack([kct[:, :, q - 1 - i:2 * q - 1 - i, :] for i in range(q)], 1)
    m_intra = m_intra.reshape(g, q * c, q * c)
    wf_re = f['pb_re'][:q][::-1]
    wf_im = f['pb_im'][:q][::-1]
    wb_re = bk['pb_re'][:q]
    wb_im = bk['pb_im'][:q]
    w_st = jnp.stack([wf_re, wf_im, wb_re, wb_im], 0)
    w_st = w_st.transpose(2, 1, 4, 0, 3).reshape(g, q * c, 4, n)
    of_re = f['cp_re'][1:]
    of_im = f['cp_im'][1:]
    ob_re = bk['cp_re'][1:][::-1]
    ob_im = bk['cp_im'][1:][::-1]
    w_out = jnp.stack([of_re, of_im, ob_re, ob_im], 0)
    w_out = w_out.transpose(2, 0, 4, 1, 3).reshape(g, 4, n, q * c)
    np_ = N_PAIRS
    w_st = w_st.astype(BF16).reshape(np_, 2, q * c, 4, n)
    ws_pair = jnp.concatenate([jnp.pad(w_st[:, 0], ((0, 0), (0, 0), (0, 0), (0, n))),
                               jnp.pad(w_st[:, 1], ((0, 0), (0, 0), (0, 0), (n, 0)))], 1)
    ws_pair = ws_pair.reshape(np_, PAIR_W, 4 * 2 * n)
    w_out = w_out.astype(BF16).reshape(np_, 2, 4, n, q * c)
    wo_pair = jnp.stack([jnp.pad(w_out[:, 0], ((0, 0), (0, 0), (0, 0), (0, q * c))),
                         jnp.pad(w_out[:, 1], ((0, 0), (0, 0), (0, 0), (q * c, 0)))], 2)
    wo_pair = wo_pair.reshape(np_, 4 * 2 * n, PAIR_W)
    tab = jnp.concatenate([_chunk_power_table(f, False), _chunk_power_table(bk, True)], 0)
    tab = tab.reshape(2 * TAB_ROWS, np_, 2 * n).transpose(1, 0, 2)
    return m_intra.astype(BF16), ws_pair, wo_pair, tab


def _chunk_power_table(t, backward):
    def cmul(x, y):
        return x[0] * y[0] - x[1] * y[1], x[0] * y[1] + x[1] * y[0]
    p1 = (t['p_re'][CHUNK], t['p_im'][CHUNK])
    p2 = cmul(p1, p1)
    p4 = cmul(p2, p2)
    p8 = cmul(p4, p4)
    pr = [(jnp.ones_like(p1[0]), jnp.zeros_like(p1[0]))]
    for _ in range(SUBLANES - 1):
        pr.append(cmul(pr[-1], p1))
    if backward:
        pr = pr[::-1]
    rows = [p[0] for p in pr] + [p[1] for p in pr]
    for p in (p1, p2, p4, p8):
        rows += [p[0], p[1]]
    return jnp.stack(rows, 0)


def _s5_scan_kernel(uc_ref, mi_ref, ws_ref, wo_ref, tab_ref, s0_ref, y_ref, fin_ref, s_scr, in_scr, *, batch):
    rows = uc_ref.shape[0]
    chunks = rows // batch
    n_tiles = chunks // SUBLANES
    u = uc_ref[...]
    s_scr[...] = _dot(u, ws_ref[...])
    row = lax.broadcasted_iota(jnp.int32, (SUBLANES, LANES), 0)

    def tile_scan(r0, backward, c_re, c_im):
        base = TAB_ROWS if backward else 0
        col = 2 * LANES if backward else 0
        rs = pl.ds(r0, SUBLANES)

        def shift(z, k):
            if backward:
                return jnp.where(row < SUBLANES - k, pltpu.roll(z, SUBLANES - k, 0), 0.0)
            return jnp.where(row >= k, pltpu.roll(z, k, 0), 0.0)

        z_re = s_scr[rs, col:col + LANES]
        z_im = s_scr[rs, col + LANES:col + 2 * LANES]
        for k, t in ((1, 16), (2, 18), (4, 20)):
            a_re = tab_ref[base + t:base + t + 1, :]
            a_im = tab_ref[base + t + 1:base + t + 2, :]
            sh_re = shift(z_re, k)
            sh_im = shift(z_im, k)
            z_re, z_im = z_re + (a_re * sh_re - a_im * sh_im), z_im + (a_re * sh_im + a_im * sh_re)
        pr_re = tab_ref[base:base + SUBLANES, :]
        pr_im = tab_ref[base + SUBLANES:base + 2 * SUBLANES, :]
        in_scr[rs, col:col + LANES] = pr_re * c_re - pr_im * c_im + shift(z_re, 1)
        in_scr[rs, col + LANES:col + 2 * LANES] = pr_re * c_im + pr_im * c_re + shift(z_im, 1)
        last = 0 if backward else SUBLANES - 1
        l_re = jnp.broadcast_to(z_re[last:last + 1, :], (SUBLANES, LANES))
        l_im = jnp.broadcast_to(z_im[last:last + 1, :], (SUBLANES, LANES))
        p8_re = tab_ref[base + 22:base + 23, :]
        p8_im = tab_ref[base + 23:base + 24, :]
        return p8_re * c_re - p8_im * c_im + l_re, p8_re * c_im + p8_im * c_re + l_im

    def body(m, carry):
        out = []
        for b in range(batch):
            cf_re, cf_im, cb_re, cb_im = carry[4 * b:4 * b + 4]
            rf = pl.multiple_of(b * chunks + m * SUBLANES, SUBLANES)
            rb = pl.multiple_of(b * chunks + (n_tiles - 1 - m) * SUBLANES, SUBLANES)
            out += list(tile_scan(rf, False, cf_re, cf_im))
            out += list(tile_scan(rb, True, cb_re, cb_im))
        return tuple(out)

    init = tuple(jnp.broadcast_to(s0_ref[t, b:b + 1, :], (SUBLANES, LANES))
                 for b in range(batch) for t in range(4))
    fin = lax.fori_loop(0, n_tiles, body, init, unroll=min(4, n_tiles))
    fin_ref[...] = jnp.zeros(fin_ref.shape, F32)
    for b in range(batch):
        for t in range(4):
            fin_ref[t, b:b + 1, :] = fin[4 * b + t][0:1, :]
    y_intra = jnp.concatenate([_dot(u[:, gl * GROUP_W:(gl + 1) * GROUP_W], mi_ref[gl]) for gl in range(2)], axis=-1)
    y = y_intra + _dot(in_scr[...].astype(BF16), wo_ref[...])
    y_ref[...] = y.astype(y_ref.dtype)


def _s5_scan(uc, mi, ws, wo, tab, s0, batch):
    rows = uc.shape[0]
    pair = lambda *shape: pl.BlockSpec((None,) + shape, lambda p: (p,) + (0,) * len(shape))
    return pl.pallas_call(
        functools.partial(_s5_scan_kernel, batch=batch),
        grid=(N_PAIRS,),
        in_specs=[pl.BlockSpec((rows, PAIR_W), lambda p: (0, p)),
                  pl.BlockSpec((2, GROUP_W, GROUP_W), lambda p: (p, 0, 0)),
                  pair(PAIR_W, PAIR_W), pair(PAIR_W, PAIR_W),
                  pair(2 * TAB_ROWS, LANES), pair(4, SUBLANES, LANES)],
        out_specs=[pl.BlockSpec((rows, PAIR_W), lambda p: (0, p)), pair(4, SUBLANES, LANES)],
        out_shape=[jax.ShapeDtypeStruct((rows, N_PAIRS * PAIR_W), BF16),
                   jax.ShapeDtypeStruct((N_PAIRS, 4, SUBLANES, LANES), F32)],
        scratch_shapes=[pltpu.VMEM((rows, PAIR_W), F32), pltpu.VMEM((rows, PAIR_W), F32)],
        compiler_params=pltpu.CompilerParams(
            dimension_semantics=("arbitrary",), vmem_limit_bytes=VMEM_LIMIT),
        name="s5_scan",
    )(uc, mi, ws, wo, tab, s0)


def _mix_out_kernel(y_ref, siga_ref, sb_ref, x_ref, remb_ref, cemb_ref, mod_ref, wv_ref, wg_ref, wo_ref,
                    ln1_ref, wr_ref, br_ref, ut_ref, lt_ref,
                    x1_ref, xs_ref, route_ref, len_ref, y_scr):
    _from_chunk_tile(y_ref, y_scr)
    y = jnp.concatenate([y_scr[v] for v in range(S5_WIDTH // LANES)], axis=-1)
    ya = (0.5 * y * (1.0 + jnp.tanh(math.sqrt(2.0 / math.pi) * (y + 0.044715 * (y * y * y))))).astype(BF16)
    out_a = _dot(ya, wv_ref[...]) * _sigmoid(_dot(ya, wg_ref[...]))
    merged = siga_ref[...].astype(F32) * out_a + sb_ref[...].astype(F32)
    mix = _dot(merged.astype(BF16), wo_ref[...])
    xp = _with_positions(x_ref, remb_ref, cemb_ref)
    x1 = _ln(ALPHA * xp + mod_ref[0:1, :] * mix) * ln1_ref[0:1, :] + ln1_ref[1:2, :]
    x1_ref[...] = x1
    h2 = _ln(x1) * mod_ref[2:3, :] + mod_ref[1:2, :]
    _route_and_sort(h2, wr_ref, br_ref, ut_ref, lt_ref, xs_ref, route_ref, len_ref)


def _route_and_sort(h2, wr_ref, br_ref, ut_ref, lt_ref, xs_ref, route_ref, len_ref):
    tm = h2.shape[0]
    f32 = F32
    h_hi = h2.astype(BF16)
    nt = (((1,), (1,)), ((), ()))
    lg = lax.dot_general(wr_ref[...], h_hi, nt, preferred_element_type=f32) + br_ref[...]
    rowi = lax.broadcasted_iota(jnp.int32, (ROUTE_LANES, tm), 0).astype(f32)
    neg = jnp.float32(-jnp.inf)
    big = jnp.float32(ROUTE_LANES)
    gl = jnp.where(rowi < N_EXPERT_GROUPS, lg, neg)
    gmax = jnp.max(gl, axis=0, keepdims=True)
    g_idx = jnp.min(jnp.where(gl == gmax, rowi, big), axis=0, keepdims=True)
    p_group = 1.0 / jnp.sum(jnp.exp(gl - gmax), axis=0, keepdims=True)
    e_lo = N_EXPERT_GROUPS + g_idx * EXPERTS_PER_GROUP
    el = jnp.where((rowi >= e_lo) & (rowi < e_lo + EXPERTS_PER_GROUP), lg, neg)
    m1 = jnp.max(el, axis=0, keepdims=True)
    i1 = jnp.min(jnp.where(el == m1, rowi, big), axis=0, keepdims=True)
    el2 = jnp.where(rowi == i1, neg, el)
    m2 = jnp.max(el2, axis=0, keepdims=True)
    i2 = jnp.min(jnp.where(el2 == m2, rowi, big), axis=0, keepdims=True)
    r = jnp.exp(m2 - m1)
    w1 = p_group / (1.0 + r)
    w2 = p_group * r / (1.0 + r)
    e1 = i1 - N_EXPERT_GROUPS
    e2 = i2 - N_EXPERT_GROUPS
    a12 = jnp.where(rowi == e1, 1.0, 0.0) + jnp.where(rowi == e2 + N_EXPERTS, 1.0, 0.0)
    rank = _dot(a12.astype(BF16), ut_ref[...])
    cnt = jnp.broadcast_to(jnp.sum(a12, axis=1, keepdims=True), (ROUTE_LANES, LANES))
    row = lax.broadcasted_iota(jnp.int32, (ROUTE_LANES, LANES), 0)
    tot = cnt + pltpu.roll(cnt, ROUTE_LANES - N_EXPERTS, 0)
    run = jnp.where(row < N_EXPERTS, jnp.floor((tot + (PIECE - 1)) * (1.0 / PIECE)), 0.0)
    off = PIECE * _dot(lt_ref[...], run.astype(BF16))
    base = jnp.where(row < N_EXPERTS, off, pltpu.roll(off + cnt, N_EXPERTS, 0))
    posmat = a12 * (rank + base[:, 0:1])
    pos1 = jnp.sum(posmat[0:N_EXPERTS, :], axis=0, keepdims=True)
    pos2 = jnp.sum(posmat[N_EXPERTS:2 * N_EXPERTS, :], axis=0, keepdims=True)
    ri = lax.broadcasted_iota(jnp.int32, (xs_ref.shape[0], tm), 0).astype(f32)
    perm = jnp.where((ri == pos1) | (ri == pos2), 1.0, 0.0).astype(BF16)
    xs_ref[...] = _dot(perm, h_hi).astype(xs_ref.dtype)
    rec = jnp.where(rowi == 0, e1, jnp.where(rowi == 1, e2, jnp.where(rowi == 2, w1, jnp.where(
        rowi == 3, w2, jnp.where(rowi == 4, pos1, jnp.where(rowi == 5, pos2, 0.0))))))
    route_ref[...] = rec.T
    len_ref[...] = run


def _local_rows(tm):
    return TOP_K * tm + N_EXPERTS * PIECE


def _mix_out(y, siga, sb, x, r_emb, c_emb, mod, wv, wg, wo, ln1, wr, br, tm):
    b, l, d = x.shape
    n_tiles = l // tm
    xs_rows = _local_rows(tm)
    tok = lambda w: pl.BlockSpec((None, tm, w), lambda i, j: (j, i, 0))
    whole = lambda a: pl.BlockSpec(a.shape, lambda i, j: (0,) * a.ndim)
    ut = (jnp.arange(tm)[:, None] < jnp.arange(tm)[None, :]).astype(BF16)
    lt = (jnp.arange(ROUTE_LANES)[None, :] < jnp.arange(ROUTE_LANES)[:, None]).astype(BF16)
    return pl.pallas_call(
        _mix_out_kernel,
        grid=(n_tiles, b),
        in_specs=[pl.BlockSpec((tm // CHUNK, CHUNK * S5_WIDTH), lambda i, j: (j * n_tiles + i, 0)),
                  tok(d), tok(d), tok(d)] + _pos_specs(tm, d) + [
                  pl.BlockSpec((None, 4, d), lambda i, j: (j, 0, 0)),
                  whole(wv), whole(wg), whole(wo), whole(ln1), whole(wr), whole(br),
                  whole(ut), whole(lt)],
        out_specs=[tok(d),
                   pl.BlockSpec((xs_rows, d), lambda i, j: (j * n_tiles + i, 0)),
                   tok(ROUTE_LANES),
                   pl.BlockSpec((None, ROUTE_LANES, LANES), lambda i, j: (j * n_tiles + i, 0, 0))],
        out_shape=[jax.ShapeDtypeStruct((b, l, d), F32),
                   jax.ShapeDtypeStruct((b * n_tiles * xs_rows, d), BF16),
                   jax.ShapeDtypeStruct((b, l, ROUTE_LANES), F32),
                   jax.ShapeDtypeStruct((b * n_tiles, ROUTE_LANES, LANES), F32)],
        scratch_shapes=[pltpu.VMEM((S5_WIDTH // LANES, tm, LANES), F32)],
        compiler_params=pltpu.CompilerParams(
            dimension_semantics=("arbitrary", "arbitrary"), vmem_limit_bytes=VMEM_LIMIT),
        name="mix_out",
    )(y, siga, sb, x, r_emb, c_emb, mod, wv, wg, wo, ln1, wr, br, ut, lt)


def _piece_copy(src_hbm, src_row, dst, piece, sem):
    return pltpu.make_async_copy(src_hbm.at[pl.ds(pl.multiple_of(src_row, PIECE), PIECE), :],
                                 dst.at[pl.ds(pl.multiple_of(piece * PIECE, PIECE), PIECE), :], sem)


def _issue_pieces(src_hbm, table_ref, first, n_pieces, dst, sem):
    for p in range(n_pieces):
        _piece_copy(src_hbm, table_ref[first + p], dst, p, sem).start()


def _wait_pieces(src_hbm, dst, sem):
    pltpu.make_async_copy(src_hbm.at[pl.ds(0, dst.shape[0]), :], dst, sem).wait()


def _experts_kernel(be_ref, piece_ref, nused_ref, xs_hbm, wg_ref, wu_ref, wd_ref, ys_ref,
                    xs_buf0, xs_buf1, wg_bf, wu_bf, wd_bf, sem):
    i = pl.program_id(0)
    n_used = nused_ref[0]
    per_block = ROW_BLOCK // PIECE

    @pl.when(i == 0)
    def _():
        _issue_pieces(xs_hbm, piece_ref, 0, per_block, xs_buf0, sem.at[0])

    def block(cur, cur_sem, oth, oth_sem):
        @pl.when((i == 0) | (be_ref[i] != be_ref[jnp.maximum(i - 1, 0)]))
        def _():
            wg_bf[...] = wg_ref[...].astype(BF16)
            wu_bf[...] = wu_ref[...].astype(BF16)
            wd_bf[...] = wd_ref[...].astype(BF16)

        _wait_pieces(xs_hbm, cur, cur_sem)
        nxt = jnp.minimum(i + 1, n_used - 1)
        _issue_pieces(xs_hbm, piece_ref, nxt * per_block, per_block, oth, oth_sem)

        xb = cur[...]
        gate = _dot(xb, wg_bf[...])
        up = _dot(xb, wu_bf[...])
        hid = (gate * _sigmoid(gate) * up).astype(BF16)
        ys_ref[...] = _dot(hid, wd_bf[...]).astype(ys_ref.dtype)

        @pl.when(i == n_used - 1)
        def _():
            _wait_pieces(xs_hbm, oth, oth_sem)

    for s, (cur, oth) in enumerate(((xs_buf0, xs_buf1), (xs_buf1, xs_buf0))):
        @pl.when((i < n_used) & (i % 2 == s))
        def _(s=s, cur=cur, oth=oth):
            block(cur, sem.at[s], oth, sem.at[1 - s])

    @pl.when(i >= n_used)
    def _():
        ys_ref[...] = jnp.zeros(ys_ref.shape, ys_ref.dtype)


def _experts(block_e, piece_src, n_used, xs, wg, wu, wd, n_blocks):
    d = xs.shape[1]
    by_expert = lambda i, be, ps, nu: (0, be[i], 0, 0)
    grid_spec = pltpu.PrefetchScalarGridSpec(
        num_scalar_prefetch=3,
        grid=(n_blocks,),
        in_specs=[pl.BlockSpec(memory_space=pl.ANY),
                  pl.BlockSpec((None, None, d, EXPERT_FF), by_expert),
                  pl.BlockSpec((None, None, d, EXPERT_FF), by_expert),
                  pl.BlockSpec((None, None, EXPERT_FF, d), by_expert)],
        out_specs=pl.BlockSpec((ROW_BLOCK, d), lambda i, be, ps, nu: (i, 0)),
        scratch_shapes=[pltpu.VMEM((ROW_BLOCK, d), BF16), pltpu.VMEM((ROW_BLOCK, d), BF16),
                        pltpu.VMEM((d, EXPERT_FF), BF16), pltpu.VMEM((d, EXPERT_FF), BF16),
                        pltpu.VMEM((EXPERT_FF, d), BF16), pltpu.SemaphoreType.DMA((2,))],
    )
    return pl.pallas_call(
        _experts_kernel,
        grid_spec=grid_spec,
        out_shape=jax.ShapeDtypeStruct((n_blocks * ROW_BLOCK, d), BF16),
        compiler_params=pltpu.CompilerParams(
            dimension_semantics=("arbitrary",), vmem_limit_bytes=VMEM_LIMIT),
        name="experts",
    )(block_e, piece_src, n_used, xs, wg, wu, wd)


def _combine_kernel(piece_ref, ys_hbm, x1_ref, route_ref, mod_ref, ln2_ref, o_ref, buf0, buf1, sem):
    i = pl.program_id(0)
    n = pl.num_programs(0)
    rows = buf0.shape[0]
    per_tile = rows // PIECE

    @pl.when(i == 0)
    def _():
        _issue_pieces(ys_hbm, piece_ref, 0, per_tile, buf0, sem.at[0])

    def tile(cur, cur_sem, oth, oth_sem):
        _wait_pieces(ys_hbm, cur, cur_sem)
        nxt = jnp.minimum(i + 1, n - 1)
        _issue_pieces(ys_hbm, piece_ref, nxt * per_tile, per_tile, oth, oth_sem)

        tm = x1_ref.shape[0]
        lane = lax.broadcasted_iota(jnp.int32, (tm, rows), 1).astype(F32)
        sel = (jnp.where(lane == route_ref[:, 4:5], route_ref[:, 2:3], 0.0)
               + jnp.where(lane == route_ref[:, 5:6], route_ref[:, 3:4], 0.0)).astype(BF16)
        moe = _dot(sel, cur[...])
        z = ALPHA * x1_ref[...] + mod_ref[0:1, :] * moe
        o_ref[...] = _ln(z) * ln2_ref[0:1, :] + ln2_ref[1:2, :]

        @pl.when(i == n - 1)
        def _():
            _wait_pieces(ys_hbm, oth, oth_sem)

    for s, (cur, oth) in enumerate(((buf0, buf1), (buf1, buf0))):
        @pl.when(i % 2 == s)
        def _(s=s, cur=cur, oth=oth):
            tile(cur, sem.at[s], oth, sem.at[1 - s])


def _combine(piece_glob, ys, x1, route, mod, ln2, tm, tiles_per_batch):
    t, d = x1.shape
    grid_spec = pltpu.PrefetchScalarGridSpec(
        num_scalar_prefetch=1,
        grid=(t // tm,),
        in_specs=[pl.BlockSpec(memory_space=pl.ANY),
                  pl.BlockSpec((tm, d), lambda i, pg: (i, 0)),
                  pl.BlockSpec((tm, ROUTE_LANES), lambda i, pg: (i, 0)),
                  pl.BlockSpec((None, 8, d), lambda i, pg: (i // tiles_per_batch, 0, 0)),
                  pl.BlockSpec((2, d), lambda i, pg: (0, 0))],
        out_specs=pl.BlockSpec((tm, d), lambda i, pg: (i, 0)),
        scratch_shapes=[pltpu.VMEM((_local_rows(tm), d), BF16), pltpu.VMEM((_local_rows(tm), d), BF16),
                        pltpu.SemaphoreType.DMA((2,))],
    )
    return pl.pallas_call(
        _combine_kernel,
        grid_spec=grid_spec,
        out_shape=jax.ShapeDtypeStruct((t, d), F32),
        compiler_params=pltpu.CompilerParams(
            dimension_semantics=("arbitrary",), vmem_limit_bytes=VMEM_LIMIT),
        name="combine",
    )(piece_glob, ys, x1, route, mod, ln2)


def _sincos_2d(rows, cols, dim):
    q = dim // 4
    omega = 1.0 / (POS_BASE ** (jnp.arange(q, dtype=F32) / q))
    r = jnp.arange(rows, dtype=F32)[:, None] * omega
    cl = jnp.arange(cols, dtype=F32)[:, None] * omega
    r_emb = jnp.concatenate([jnp.sin(r), jnp.cos(r)], -1)
    c_emb = jnp.concatenate([jnp.sin(cl), jnp.cos(cl)], -1)
    return r_emb, c_emb


def _routing_tables(run_pieces, xs_rows, n_blocks):
    i32 = jnp.int32
    n_tiles = run_pieces.shape[0]
    ppb = ROW_BLOCK // PIECE
    loc_start = jnp.cumsum(run_pieces, axis=1) - run_pieces
    seg_tot = jnp.sum(run_pieces, axis=0)
    seg_pad = (seg_tot + ppb - 1) // ppb * ppb
    seg_end = jnp.cumsum(seg_pad)
    seg_start = seg_end - seg_pad
    run_t = run_pieces.T
    glob_start = seg_start[:, None] + jnp.cumsum(run_t, axis=1) - run_t
    n_used = (seg_end[-1] // ppb).astype(i32)
    blk = jnp.minimum(jnp.arange(n_blocks, dtype=i32), n_used - 1)
    block_e = jnp.minimum(jnp.sum((seg_end[None, :] <= (blk * ppb)[:, None]).astype(i32), axis=1),
                          N_EXPERTS - 1).astype(i32)
    lpt = xs_rows // PIECE
    starts = glob_start.reshape(-1)
    lens = run_t.reshape(-1)
    src0 = (jnp.arange(n_tiles, dtype=i32)[None, :] * lpt + loc_start.T).reshape(-1)
    p = jnp.arange(n_blocks * ppb, dtype=i32)
    within = p[:, None] - starts[None, :]
    hit = (within >= 0) & (within < lens[None, :])
    piece_src = jnp.sum(jnp.where(hit, (src0[None, :] + within) * PIECE, 0), axis=1).astype(i32)
    s = jnp.arange(lpt, dtype=i32)
    loc_within = s[None, :, None] - loc_start[:, None, :]
    hit = (loc_within >= 0) & (loc_within < run_pieces[:, None, :])
    piece_glob = jnp.sum(jnp.where(hit, (glob_start.T[:, None, :] + loc_within) * PIECE, 0), axis=2)
    return block_e, piece_src, piece_glob.astype(i32).reshape(-1), n_used.reshape(1)


def kernel(x, c, ctx, c_ctx, w_ada, b_ada, w_in, s5_log_dt_f, s5_a_re_f, s5_a_im_f, s5_b_re_f, s5_b_im_f, s5_c_re_f, s5_c_im_f, s5_log_dt_b, s5_a_re_b, s5_a_im_b, s5_b_re_b, s5_b_im_b, s5_c_re_b, s5_c_im_b, s5_d, s5_w_glu_val, s5_w_glu_gate, conv_w, conv_w_out, w_o, ln1_g, ln1_b, router_w_group, router_b_group, router_w_expert, router_b_expert, exp_w_gate, exp_w_up, exp_w_down, ln2_g, ln2_b):
    b, l, d = x.shape
    lc = ctx.shape[1]
    assert d == D_MODEL and b < SUBLANES and w_ada.shape[0] == DEPTH
    assert l % (SUBLANES * CHUNK) == 0 and lc % (SUBLANES * CHUNK) == 0 and l % GRID_W == 0
    t = b * l
    tm = min(512, l)
    tmc = min(512, lc)

    cc = jnp.concatenate([c, c_ctx[None, :], jnp.zeros((8 - b - 1, d), F32)], 0)
    mods = _mods(cc, w_ada[0], b_ada[0])
    sh1, sc1, g1, sh2, sc2, g2 = jnp.split(mods, 6, axis=-1)
    mod_a = jnp.stack([sh1[:b], 1.0 + sc1[:b]], 1)
    mod_ctx = jnp.broadcast_to(jnp.stack([sh1[b], 1.0 + sc1[b]], 0)[None], (b, 2, d))
    mod_c = jnp.stack([g1[:b], sh2[:b], 1.0 + sc2[:b], jnp.zeros((b, d), F32)], 1)
    mod_f = jnp.concatenate([g2[:b, None, :], jnp.zeros((b, 7, d), F32)], 1)

    w_in_bf = w_in[0].astype(BF16)
    f_tab = _s5_dir_tables(s5_log_dt_f[0], s5_a_re_f[0], s5_a_im_f[0], s5_b_re_f[0], s5_b_im_f[0],
                           s5_c_re_f[0], s5_c_im_f[0])
    b_tab = _s5_dir_tables(s5_log_dt_b[0], s5_a_re_b[0], s5_a_im_b[0], s5_b_re_b[0], s5_b_im_b[0],
                           s5_c_re_b[0], s5_c_im_b[0])
    mi, ws, wo_s5, tab = _s5_operators(f_tab, b_tab, s5_d[0])

    (uc_ctx,) = _in_proj(ctx, jnp.zeros((lc // GRID_W, d // 2), F32), jnp.zeros((GRID_W, d // 2), F32),
                         mod_ctx, w_in_bf, None, None, tmc, False)
    zero_state = jnp.zeros((N_PAIRS, 4, SUBLANES, LANES), F32)
    _, s0 = _s5_scan(uc_ctx, mi, ws, wo_s5, tab, zero_state, b)

    r_emb, c_emb = _sincos_2d(l // GRID_W, GRID_W, d)
    uc, siga, sb = _in_proj(x, r_emb, c_emb, mod_a, w_in_bf, conv_w[0], conv_w_out[0].astype(BF16), tm, True)
    y, _ = _s5_scan(uc, mi, ws, wo_s5, tab, s0, b)

    wr = jnp.concatenate([router_w_group[0], router_w_expert[0],
                          jnp.zeros((d, ROUTE_LANES - N_EXPERT_GROUPS - N_EXPERTS), F32)], 1).T.astype(BF16)
    br = jnp.concatenate([router_b_group[0], router_b_expert[0],
                          jnp.zeros((ROUTE_LANES - N_EXPERT_GROUPS - N_EXPERTS,), F32)])[:, None]
    ln1 = jnp.stack([ln1_g[0], ln1_b[0]], 0)
    x1, xs, route, run_len = _mix_out(y, siga, sb, x, r_emb, c_emb, mod_c,
                                      s5_w_glu_val[0].astype(BF16), s5_w_glu_gate[0].astype(BF16),
                                      w_o[0].astype(BF16), ln1, wr, br, tm)

    x1 = x1.reshape(t, d)
    route = route.reshape(t, ROUTE_LANES)
    n_tiles = t // tm
    xs_rows = _local_rows(tm)
    run_pieces = run_len[:, :N_EXPERTS, 0].astype(jnp.int32)
    max_rows = t * TOP_K + n_tiles * N_EXPERTS * (PIECE - 1) + N_EXPERTS * (ROW_BLOCK - 1)
    n_blocks = -(-max_rows // ROW_BLOCK)
    block_e, piece_src, piece_glob, n_used = _routing_tables(run_pieces, xs_rows, n_blocks)
    ys = _experts(block_e, piece_src, n_used, xs, exp_w_gate, exp_w_up, exp_w_down, n_blocks)
    ln2 = jnp.stack([ln2_g[0], ln2_b[0]], 0)
    out = _combine(piece_glob, ys, x1, route, mod_f, ln2, tm, l // tm)
    return out.reshape(b, l, d)
```

```python
import functools
import math

import jax
import jax.numpy as jnp
from jax import lax
from jax.experimental import pallas as pl
from jax.experimental.pallas import tpu as pltpu

F32 = jnp.float32
BF16 = jnp.bfloat16
HI = lax.Precision.HIGHEST

D_MODEL = 1024
GRID_W = 64
S5_WIDTH = 512
S5_GROUP_CH = 16
S5_GROUPS = S5_WIDTH // S5_GROUP_CH
S5_STATE = 64
CONV_WIDTH = 512
N_EXPERT_GROUPS = 4
EXPERTS_PER_GROUP = 8
N_EXPERTS = N_EXPERT_GROUPS * EXPERTS_PER_GROUP
EXPERT_FF = 512
TOP_K = 2
DEPTH = 1
ALPHA = (2.0 * DEPTH) ** 0.25
LN_EPS = 1e-6
POS_BASE = 10000.0

LANES = 128
SUBLANES = 8
CHUNK = 16
GROUP_W = CHUNK * S5_GROUP_CH
PAIR_W = 2 * GROUP_W
N_PAIRS = S5_GROUPS // 2
TOK_PER_VREG = LANES // S5_GROUP_CH
TAB_ROWS = 24
TOKEN_SUBTILES = 2
ROUTE_LANES = 128
ROW_BLOCK = 512
PIECE = 16
VMEM_LIMIT = 56 * 1024 * 1024


def _ln(x):
    mu = jnp.mean(x, axis=-1, keepdims=True)
    xc = x - mu
    var = jnp.mean(xc * xc, axis=-1, keepdims=True)
    return xc * lax.rsqrt(var + LN_EPS)


def _sigmoid(x):
    return 0.5 * (jnp.tanh(0.5 * x) + 1.0)


def _dot(a, b):
    return jnp.dot(a, b, preferred_element_type=F32)


def _mods_kernel(c_ref, w_ref, b_ref, o_ref):
    c = c_ref[...]
    a = c * _sigmoid(c)
    o_ref[...] = jnp.dot(a, w_ref[...], precision=HI, preferred_element_type=F32) + b_ref[...]


def _mods(cc, w_ada, b_ada):
    n = w_ada.shape[1]
    nb = 1536
    return pl.pallas_call(
        _mods_kernel,
        grid=(n // nb,),
        in_specs=[pl.BlockSpec((8, D_MODEL), lambda i: (0, 0)),
                  pl.BlockSpec((D_MODEL, nb), lambda i: (0, i)),
                  pl.BlockSpec((1, nb), lambda i: (0, i))],
        out_specs=pl.BlockSpec((8, nb), lambda i: (0, i)),
        out_shape=jax.ShapeDtypeStruct((8, n), F32),
        compiler_params=pltpu.CompilerParams(vmem_limit_bytes=VMEM_LIMIT),
        name="mods",
    )(cc, w_ada, b_ada.reshape(1, n))


def _slot_masks(rows):
    slot = lax.broadcasted_iota(jnp.int32, (rows, LANES), 1) // S5_GROUP_CH
    return [slot == s for s in range(TOK_PER_VREG)]


def _to_chunk_tile(u_scr, uc_ref):
    nch = uc_ref.shape[0]
    masks = _slot_masks(nch)
    for qh in range(CHUNK // TOK_PER_VREG):
        for v in range(S5_WIDTH // LANES):
            src = [u_scr[v, pl.ds(qh * TOK_PER_VREG + s, nch, stride=CHUNK), :] for s in range(TOK_PER_VREG)]
            for i in range(TOK_PER_VREG):
                acc = None
                for s in range(TOK_PER_VREG):
                    shift = ((s - i) * S5_GROUP_CH) % LANES
                    piece = pltpu.roll(src[s], shift, 1) if shift else src[s]
                    acc = piece if acc is None else jnp.where(masks[s], piece, acc)
                lo = (v * TOK_PER_VREG + i) * GROUP_W + qh * LANES
                uc_ref[:, lo:lo + LANES] = acc.astype(uc_ref.dtype)


def _from_chunk_tile(yc_ref, y_scr, c0=0, nch=None):
    nch = yc_ref.shape[0] if nch is None else nch
    masks = _slot_masks(nch)
    for qh in range(CHUNK // TOK_PER_VREG):
        for v in range(S5_WIDTH // LANES):
            src = []
            for i in range(TOK_PER_VREG):
                lo = (v * TOK_PER_VREG + i) * GROUP_W + qh * LANES
                src.append(yc_ref[c0:c0 + nch, lo:lo + LANES].astype(F32))
            for s in range(TOK_PER_VREG):
                acc = None
                for i in range(TOK_PER_VREG):
                    shift = ((i - s) * S5_GROUP_CH) % LANES
                    piece = pltpu.roll(src[i], shift, 1) if shift else src[i]
                    acc = piece if acc is None else jnp.where(masks[i], piece, acc)
                y_scr[v, pl.ds(c0 * CHUNK + qh * TOK_PER_VREG + s, nch, stride=CHUNK), :] = acc


def _with_positions(x_ref, remb_ref, cemb_ref, r0=0, rows=None):
    rows = x_ref.shape[0] if rows is None else rows
    c = cemb_ref[...]
    slabs = []
    for j in range(r0 // GRID_W, (r0 + rows) // GRID_W):
        r = jnp.broadcast_to(remb_ref[j:j + 1, :], c.shape)
        slabs.append(x_ref[j * GRID_W:(j + 1) * GRID_W, :] + jnp.concatenate([r, c], axis=-1))
    return jnp.concatenate(slabs, axis=0)


def _in_proj_kernel(x_ref, remb_ref, cemb_ref, mod_ref, w_ref, *rest, full):
    if full:
        cw_ref, cwo_ref, uc_ref, siga_ref, sb_ref, u_scr = rest
    else:
        uc_ref, u_scr = rest
    xp = _with_positions(x_ref, remb_ref, cemb_ref)
    h = (_ln(xp) * mod_ref[1:2, :] + mod_ref[0:1, :]).astype(BF16)
    o1, o2, o3, o4, o5 = 512, 1024, 1536, 2048, 3072
    u = _dot(h, w_ref[:, 0:o1])
    for v in range(S5_WIDTH // LANES):
        u_scr[v] = u[:, v * LANES:(v + 1) * LANES]
    _to_chunk_tile(u_scr, uc_ref)
    if not full:
        return
    z_b = _dot(h, w_ref[:, o1:o2])
    gate_c = _dot(h, w_ref[:, o3:o4])
    p = gate_c * z_b
    tm = p.shape[0]
    col = lax.broadcasted_iota(jnp.int32, (tm, 1), 0) % GRID_W
    prev = jnp.where(col == 0, 0.0, pltpu.roll(p, 1, 0))
    nxt = jnp.where(col == GRID_W - 1, 0.0, pltpu.roll(p, tm - 1, 0))
    v = cw_ref[0:1, :] * prev + cw_ref[1:2, :] * p + cw_ref[2:3, :] * nxt
    gate_b = _dot(h, w_ref[:, o2:o3])
    out_b = _dot((gate_b * v).astype(BF16), cwo_ref[...])
    merge_b = _dot(h, w_ref[:, o5:])
    sb_ref[...] = (_sigmoid(merge_b) * out_b).astype(sb_ref.dtype)
    merge_a = _dot(h, w_ref[:, o4:o5])
    siga_ref[...] = _sigmoid(merge_a).astype(siga_ref.dtype)


def _pos_specs(tm, d):
    return [pl.BlockSpec((tm // GRID_W, d // 2), lambda i, j: (i, 0)),
            pl.BlockSpec((GRID_W, d // 2), lambda i, j: (0, 0))]


def _in_proj(x, r_emb, c_emb, mod, w_in_bf, conv_w, conv_w_out_bf, tm, full):
    b, l, d = x.shape
    n_tiles = l // tm
    grid = (n_tiles, b)
    tok = lambda w: pl.BlockSpec((None, tm, w), lambda i, j: (j, i, 0))
    chunk_spec = pl.BlockSpec((tm // CHUNK, CHUNK * S5_WIDTH), lambda i, j: (j * n_tiles + i, 0))
    chunk_shape = jax.ShapeDtypeStruct((b * l // CHUNK, CHUNK * S5_WIDTH), BF16)
    in_specs = [tok(d)] + _pos_specs(tm, d) + [pl.BlockSpec((None, 2, d), lambda i, j: (j, 0, 0))]
    args = [x, r_emb, c_emb, mod]
    if full:
        in_specs += [pl.BlockSpec(w_in_bf.shape, lambda i, j: (0, 0)),
                     pl.BlockSpec(conv_w.shape, lambda i, j: (0, 0)),
                     pl.BlockSpec(conv_w_out_bf.shape, lambda i, j: (0, 0))]
        args += [w_in_bf, conv_w, conv_w_out_bf]
        out_specs = [chunk_spec, tok(d), tok(d)]
        out_shape = [chunk_shape,
                     jax.ShapeDtypeStruct((b, l, d), BF16),
                     jax.ShapeDtypeStruct((b, l, d), BF16)]
    else:
        in_specs += [pl.BlockSpec((d, S5_WIDTH), lambda i, j: (0, 0))]
        args += [w_in_bf]
        out_specs = [chunk_spec]
        out_shape = [chunk_shape]
    return pl.pallas_call(
        functools.partial(_in_proj_kernel, full=full),
        grid=grid, in_specs=in_specs, out_specs=out_specs, out_shape=out_shape,
        scratch_shapes=[pltpu.VMEM((S5_WIDTH // LANES, tm, LANES), F32)],
        compiler_params=pltpu.CompilerParams(
            dimension_semantics=("arbitrary", "arbitrary"), vmem_limit_bytes=VMEM_LIMIT),
        name="in_proj" if full else "in_proj_ctx",
    )(*args)


def _s5_dir_tables(log_dt, a_re, a_im, b_re, b_im, c_re, c_im):
    f32 = F32
    dt = jnp.exp(log_dt.astype(f32))[:, None]
    a_re = a_re.astype(f32)
    a_im = a_im.astype(f32)
    mag = jnp.exp(dt * a_re)
    ab_re = mag * jnp.cos(dt * a_im)
    ab_im = mag * jnp.sin(dt * a_im)
    den = a_re * a_re + a_im * a_im
    x_re = ab_re - 1.0
    f_re = (x_re * a_re + ab_im * a_im) / den
    f_im = (ab_im * a_re - x_re * a_im) / den
    b_re = b_re.astype(f32)
    b_im = b_im.astype(f32)
    bb_re = f_re[..., None] * b_re - f_im[..., None] * b_im
    bb_im = f_re[..., None] * b_im + f_im[..., None] * b_re
    k = jnp.arange(CHUNK + 1, dtype=f32)[:, None, None]
    pmag = jnp.exp(k * (dt * a_re)[None])
    p_re = pmag * jnp.cos(k * (dt * a_im)[None])
    p_im = pmag * jnp.sin(k * (dt * a_im)[None])
    pb_re = p_re[..., None] * bb_re[None] - p_im[..., None] * bb_im[None]
    pb_im = p_re[..., None] * bb_im[None] + p_im[..., None] * bb_re[None]
    c_re = c_re.astype(f32)
    c_im = c_im.astype(f32)
    cp_re = c_re[None] * p_re[:, :, None, :] - c_im[None] * p_im[:, :, None, :]
    cp_im = -(c_re[None] * p_im[:, :, None, :] + c_im[None] * p_re[:, :, None, :])
    return dict(p_re=p_re, p_im=p_im, pb_re=pb_re, pb_im=pb_im, cp_re=cp_re, cp_im=cp_im,
                bb_re=bb_re, bb_im=bb_im)


def _lag_kernels(f, bk):
    g, n, c = S5_GROUPS, S5_STATE, S5_GROUP_CH
    k = CHUNK + 1
    lhs = jnp.stack([jnp.concatenate([t['cp_re'], t['cp_im']], -1) for t in (f, bk)], 0)
    lhs = lhs.transpose(0, 2, 1, 3, 4).reshape(2 * g, k * c, 2 * n)
    rhs = jnp.stack([jnp.concatenate([t['bb_re'], t['bb_im']], 1) for t in (f, bk)], 0)
    out = jnp.einsum('bmn,bnc->bmc', lhs, rhs.reshape(2 * g, 2 * n, c), precision=HI)
    out = out.reshape(2, g, k, c, c).transpose(0, 2, 1, 4, 3)
    return out[0], out[1]


def _s5_operators(f, bk, s5_d):
    q = CHUNK
    g, n, c = S5_GROUPS, S5_STATE, S5_GROUP_CH
    kern_f, kern_b = _lag_kernels(f, bk)
    k0 = kern_f[0] + kern_b[0] + s5_d.astype(F32)[:, :, None] * jnp.eye(c, dtype=F32)[None]
    kc = jnp.concatenate([kern_b[1:q][::-1], k0[None], kern_f[1:q]], 0)
    kct = kc.transpose(1, 2, 0, 3)
    m_intra = jnp.stack([kct[:, :, q - 1 - i:2 * q - 1 - i, :] for i in range(q)], 1)
    m_intra = m_intra.reshape(g, q * c, q * c)
    wf_re = f['pb_re'][:q][::-1]
    wf_im = f['pb_im'][:q][::-1]
    wb_re = bk['pb_re'][:q]
    wb_im = bk['pb_im'][:q]
    w_st = jnp.stack([wf_re, wf_im, wb_re, wb_im], 0)
    w_st = w_st.transpose(2, 1, 4, 0, 3).reshape(g, q * c, 4, n)
    of_re = f['cp_re'][1:]
    of_im = f['cp_im'][1:]
    ob_re = bk['cp_re'][1:][::-1]
    ob_im = bk['cp_im'][1:][::-1]
    w_out = jnp.stack([of_re, of_im, ob_re, ob_im], 0)
    w_out = w_out.transpose(2, 0, 4, 1, 3).reshape(g, 4, n, q * c)
    np_ = N_PAIRS
    w_st = w_st.astype(BF16).reshape(np_, 2, q * c, 4, n)
    ws_pair = jnp.concatenate([jnp.pad(w_st[:, 0], ((0, 0), (0, 0), (0, 0), (0, n))),
                               jnp.pad(w_st[:, 1], ((0, 0), (0, 0), (0, 0), (n, 0)))], 1)
    ws_pair = ws_pair.reshape(np_, PAIR_W, 4 * 2 * n)
    w_out = w_out.astype(BF16).reshape(np_, 2, 4, n, q * c)
    wo_pair = jnp.stack([jnp.pad(w_out[:, 0], ((0, 0), (0, 0), (0, 0), (0, q * c))),
                         jnp.pad(w_out[:, 1], ((0, 0), (0, 0), (0, 0), (q * c, 0)))], 2)
    wo_pair = wo_pair.reshape(np_, 4 * 2 * n, PAIR_W)
    tab = jnp.concatenate([_chunk_power_table(f, False), _chunk_power_table(bk, True)], 0)
    tab = tab.reshape(2 * TAB_ROWS, np_, 2 * n).transpose(1, 0, 2)
    return m_intra.astype(BF16), ws_pair, wo_pair, tab


def _chunk_power_table(t, backward):
    def cmul(x, y):
        return x[0] * y[0] - x[1] * y[1], x[0] * y[1] + x[1] * y[0]
    p1 = (t['p_re'][CHUNK], t['p_im'][CHUNK])
    p2 = cmul(p1, p1)
    p4 = cmul(p2, p2)
    p8 = cmul(p4, p4)
    pr = [(jnp.ones_like(p1[0]), jnp.zeros_like(p1[0]))]
    for _ in range(SUBLANES - 1):
        pr.append(cmul(pr[-1], p1))
    if backward:
        pr = pr[::-1]
    rows = [p[0] for p in pr] + [p[1] for p in pr]
    for p in (p1, p2, p4, p8):
        rows += [p[0], p[1]]
    return jnp.stack(rows, 0)


def _s5_scan_kernel(uc_ref, mi_ref, ws_ref, wo_ref, tab_ref, s0_ref, y_ref, fin_ref, s_scr, in_scr, *, batch):
    rows = uc_ref.shape[0]
    chunks = rows // batch
    n_tiles = chunks // SUBLANES
    u = uc_ref[...]
    s_scr[...] = _dot(u, ws_ref[...])
    row = lax.broadcasted_iota(jnp.int32, (SUBLANES, LANES), 0)

    def tile_scan(r0, backward, c_re, c_im):
        base = TAB_ROWS if backward else 0
        col = 2 * LANES if backward else 0
        rs = pl.ds(r0, SUBLANES)

        def shift(z, k):
            if backward:
                return jnp.where(row < SUBLANES - k, pltpu.roll(z, SUBLANES - k, 0), 0.0)
            return jnp.where(row >= k, pltpu.roll(z, k, 0), 0.0)

        z_re = s_scr[rs, col:col + LANES]
        z_im = s_scr[rs, col + LANES:col + 2 * LANES]
        for k, t in ((1, 16), (2, 18), (4, 20)):
            a_re = tab_ref[base + t:base + t + 1, :]
            a_im = tab_ref[base + t + 1:base + t + 2, :]
            sh_re = shift(z_re, k)
            sh_im = shift(z_im, k)
            z_re, z_im = z_re + (a_re * sh_re - a_im * sh_im), z_im + (a_re * sh_im + a_im * sh_re)
        pr_re = tab_ref[base:base + SUBLANES, :]
        pr_im = tab_ref[base + SUBLANES:base + 2 * SUBLANES, :]
        in_scr[rs, col:col + LANES] = pr_re * c_re - pr_im * c_im + shift(z_re, 1)
        in_scr[rs, col + LANES:col + 2 * LANES] = pr_re * c_im + pr_im * c_re + shift(z_im, 1)
        last = 0 if backward else SUBLANES - 1
        l_re = jnp.broadcast_to(z_re[last:last + 1, :], (SUBLANES, LANES))
        l_im = jnp.broadcast_to(z_im[last:last + 1, :], (SUBLANES, LANES))
        p8_re = tab_ref[base + 22:base + 23, :]
        p8_im = tab_ref[base + 23:base + 24, :]
        return p8_re * c_re - p8_im * c_im + l_re, p8_re * c_im + p8_im * c_re + l_im

    def body(m, carry):
        out = []
        for b in range(batch):
            cf_re, cf_im, cb_re, cb_im = carry[4 * b:4 * b + 4]
            rf = pl.multiple_of(b * chunks + m * SUBLANES, SUBLANES)
            rb = pl.multiple_of(b * chunks + (n_tiles - 1 - m) * SUBLANES, SUBLANES)
            out += list(tile_scan(rf, False, cf_re, cf_im))
            out += list(tile_scan(rb, True, cb_re, cb_im))
        return tuple(out)

    init = tuple(jnp.broadcast_to(s0_ref[t, b:b + 1, :], (SUBLANES, LANES))
                 for b in range(batch) for t in range(4))
    fin = lax.fori_loop(0, n_tiles, body, init, unroll=min(4, n_tiles))
    fin_ref[...] = jnp.zeros(fin_ref.shape, F32)
    for b in range(batch):
        for t in range(4):
            fin_ref[t, b:b + 1, :] = fin[4 * b + t][0:1, :]
    y_intra = jnp.concatenate([_dot(u[:, gl * GROUP_W:(gl + 1) * GROUP_W], mi_ref[gl]) for gl in range(2)], axis=-1)
    y = y_intra + _dot(in_scr[...].astype(BF16), wo_ref[...])
    y_ref[...] = y.astype(y_ref.dtype)


def _s5_scan(uc, mi, ws, wo, tab, s0, batch):
    rows = uc.shape[0]
    pair = lambda *shape: pl.BlockSpec((None,) + shape, lambda p: (p,) + (0,) * len(shape))
    return pl.pallas_call(
        functools.partial(_s5_scan_kernel, batch=batch),
        grid=(N_PAIRS,),
        in_specs=[pl.BlockSpec((rows, PAIR_W), lambda p: (0, p)),
                  pl.BlockSpec((2, GROUP_W, GROUP_W), lambda p: (p, 0, 0)),
                  pair(PAIR_W, PAIR_W), pair(PAIR_W, PAIR_W),
                  pair(2 * TAB_ROWS, LANES), pair(4, SUBLANES, LANES)],
        out_specs=[pl.BlockSpec((rows, PAIR_W), lambda p: (0, p)), pair(4, SUBLANES, LANES)],
        out_shape=[jax.ShapeDtypeStruct((rows, N_PAIRS * PAIR_W), BF16),
                   jax.ShapeDtypeStruct((N_PAIRS, 4, SUBLANES, LANES), F32)],
        scratch_shapes=[pltpu.VMEM((rows, PAIR_W), F32), pltpu.VMEM((rows, PAIR_W), F32)],
        compiler_params=pltpu.CompilerParams(
            dimension_semantics=("arbitrary",), vmem_limit_bytes=VMEM_LIMIT),
        name="s5_scan",
    )(uc, mi, ws, wo, tab, s0)


def _mix_out_kernel(y_ref, siga_ref, sb_ref, x_ref, remb_ref, cemb_ref, mod_ref, wv_ref, wg_ref, wo_ref,
                    ln1_ref, wr_ref, br_ref, ut_ref, lt_ref,
                    x1_ref, xs_ref, route_ref, len_ref, y_scr):
    tm = x_ref.shape[0]
    sub = tm // TOKEN_SUBTILES
    h_parts = []
    for s in range(TOKEN_SUBTILES):
        r0 = s * sub
        _from_chunk_tile(y_ref, y_scr, r0 // CHUNK, sub // CHUNK)
        y = jnp.concatenate([y_scr[v, r0:r0 + sub, :] for v in range(S5_WIDTH // LANES)], axis=-1)
        ya = (0.5 * y * (1.0 + jnp.tanh(math.sqrt(2.0 / math.pi) * (y + 0.044715 * (y * y * y))))).astype(BF16)
        out_a = _dot(ya, wv_ref[...]) * _sigmoid(_dot(ya, wg_ref[...]))
        merged = siga_ref[r0:r0 + sub, :].astype(F32) * out_a + sb_ref[r0:r0 + sub, :].astype(F32)
        mix = _dot(merged.astype(BF16), wo_ref[...])
        xp = _with_positions(x_ref, remb_ref, cemb_ref, r0, sub)
        x1 = _ln(ALPHA * xp + mod_ref[0:1, :] * mix) * ln1_ref[0:1, :] + ln1_ref[1:2, :]
        x1_ref[r0:r0 + sub, :] = x1
        h_parts.append((_ln(x1) * mod_ref[2:3, :] + mod_ref[1:2, :]).astype(BF16))
    h_hi = jnp.concatenate(h_parts, axis=0)
    _route_and_sort(h_hi, wr_ref, br_ref, ut_ref, lt_ref, xs_ref, route_ref, len_ref)


def _route_and_sort(h_hi, wr_ref, br_ref, ut_ref, lt_ref, xs_ref, route_ref, len_ref):
    tm = h_hi.shape[0]
    f32 = F32
    nt = (((1,), (1,)), ((), ()))
    lg = lax.dot_general(wr_ref[...], h_hi, nt, preferred_element_type=f32) + br_ref[...]
    rowi = lax.broadcasted_iota(jnp.int32, (ROUTE_LANES, tm), 0).astype(f32)
    neg = jnp.float32(-jnp.inf)
    big = jnp.float32(ROUTE_LANES)
    gl = jnp.where(rowi < N_EXPERT_GROUPS, lg, neg)
    gmax = jnp.max(gl, axis=0, keepdims=True)
    g_idx = jnp.min(jnp.where(gl == gmax, rowi, big), axis=0, keepdims=True)
    p_group = 1.0 / jnp.sum(jnp.exp(gl - gmax), axis=0, keepdims=True)
    e_lo = N_EXPERT_GROUPS + g_idx * EXPERTS_PER_GROUP
    el = jnp.where((rowi >= e_lo) & (rowi < e_lo + EXPERTS_PER_GROUP), lg, neg)
    m1 = jnp.max(el, axis=0, keepdims=True)
    i1 = jnp.min(jnp.where(el == m1, rowi, big), axis=0, keepdims=True)
    el2 = jnp.where(rowi == i1, neg, el)
    m2 = jnp.max(el2, axis=0, keepdims=True)
    i2 = jnp.min(jnp.where(el2 == m2, rowi, big), axis=0, keepdims=True)
    r = jnp.exp(m2 - m1)
    w1 = p_group / (1.0 + r)
    w2 = p_group * r / (1.0 + r)
    e1 = i1 - N_EXPERT_GROUPS
    e2 = i2 - N_EXPERT_GROUPS
    a12 = jnp.where(rowi == e1, 1.0, 0.0) + jnp.where(rowi == e2 + N_EXPERTS, 1.0, 0.0)
    rank = _dot(a12.astype(BF16), ut_ref[...])
    cnt = jnp.broadcast_to(jnp.sum(a12, axis=1, keepdims=True), (ROUTE_LANES, LANES))
    row = lax.broadcasted_iota(jnp.int32, (ROUTE_LANES, LANES), 0)
    tot = cnt + pltpu.roll(cnt, ROUTE_LANES - N_EXPERTS, 0)
    run = jnp.where(row < N_EXPERTS, jnp.floor((tot + (PIECE - 1)) * (1.0 / PIECE)), 0.0)
    off = PIECE * _dot(lt_ref[...], run.astype(BF16))
    base = jnp.where(row < N_EXPERTS, off, pltpu.roll(off + cnt, N_EXPERTS, 0))
    posmat = a12 * (rank + base[:, 0:1])
    pos1 = jnp.sum(posmat[0:N_EXPERTS, :], axis=0, keepdims=True)
    pos2 = jnp.sum(posmat[N_EXPERTS:2 * N_EXPERTS, :], axis=0, keepdims=True)
    ri = lax.broadcasted_iota(jnp.int32, (xs_ref.shape[0], tm), 0).astype(f32)
    perm = jnp.where((ri == pos1) | (ri == pos2), 1.0, 0.0).astype(BF16)
    xs_ref[...] = _dot(perm, h_hi).astype(xs_ref.dtype)
    rec = jnp.where(rowi == 0, e1, jnp.where(rowi == 1, e2, jnp.where(rowi == 2, w1, jnp.where(
        rowi == 3, w2, jnp.where(rowi == 4, pos1, jnp.where(rowi == 5, pos2, 0.0))))))
    route_ref[...] = rec.T
    len_ref[...] = run


def _local_rows(tm):
    return TOP_K * tm + N_EXPERTS * PIECE


def _mix_out(y, siga, sb, x, r_emb, c_emb, mod, wv, wg, wo, ln1, wr, br, tm):
    b, l, d = x.shape
    n_tiles = l // tm
    xs_rows = _local_rows(tm)
    tok = lambda w: pl.BlockSpec((None, tm, w), lambda i, j: (j, i, 0))
    whole = lambda a: pl.BlockSpec(a.shape, lambda i, j: (0,) * a.ndim)
    ut = (jnp.arange(tm)[:, None] < jnp.arange(tm)[None, :]).astype(BF16)
    lt = (jnp.arange(ROUTE_LANES)[None, :] < jnp.arange(ROUTE_LANES)[:, None]).astype(BF16)
    return pl.pallas_call(
        _mix_out_kernel,
        grid=(n_tiles, b),
        in_specs=[pl.BlockSpec((tm // CHUNK, CHUNK * S5_WIDTH), lambda i, j: (j * n_tiles + i, 0)),
                  tok(d), tok(d), tok(d)] + _pos_specs(tm, d) + [
                  pl.BlockSpec((None, 4, d), lambda i, j: (j, 0, 0)),
                  whole(wv), whole(wg), whole(wo), whole(ln1), whole(wr), whole(br),
                  whole(ut), whole(lt)],
        out_specs=[tok(d),
                   pl.BlockSpec((xs_rows, d), lambda i, j: (j * n_tiles + i, 0)),
                   tok(ROUTE_LANES),
                   pl.BlockSpec((None, ROUTE_LANES, LANES), lambda i, j: (j * n_tiles + i, 0, 0))],
        out_shape=[jax.ShapeDtypeStruct((b, l, d), F32),
                   jax.ShapeDtypeStruct((b * n_tiles * xs_rows, d), BF16),
                   jax.ShapeDtypeStruct((b, l, ROUTE_LANES), F32),
                   jax.ShapeDtypeStruct((b * n_tiles, ROUTE_LANES, LANES), F32)],
        scratch_shapes=[pltpu.VMEM((S5_WIDTH // LANES, tm, LANES), F32)],
        compiler_params=pltpu.CompilerParams(
            dimension_semantics=("arbitrary", "arbitrary"), vmem_limit_bytes=VMEM_LIMIT),
        name="mix_out",
    )(y, siga, sb, x, r_emb, c_emb, mod, wv, wg, wo, ln1, wr, br, ut, lt)


def _piece_copy(src_hbm, src_row, dst, piece, sem):
    return pltpu.make_async_copy(src_hbm.at[pl.ds(pl.multiple_of(src_row, PIECE), PIECE), :],
                                 dst.at[pl.ds(pl.multiple_of(piece * PIECE, PIECE), PIECE), :], sem)


def _issue_pieces(src_hbm, table_ref, first, n_pieces, dst, sem):
    for p in range(n_pieces):
        _piece_copy(src_hbm, table_ref[first + p], dst, p, sem).start()


def _wait_pieces(src_hbm, dst, sem):
    pltpu.make_async_copy(src_hbm.at[pl.ds(0, dst.shape[0]), :], dst, sem).wait()


def _experts_kernel(be_ref, piece_ref, nused_ref, xs_hbm, wg_ref, wu_ref, wd_ref, ys_ref,
                    xs_buf0, xs_buf1, wg_bf, wu_bf, wd_bf, sem):
    i = pl.program_id(0)
    n_used = nused_ref[0]
    per_block = ROW_BLOCK // PIECE

    @pl.when(i == 0)
    def _():
        _issue_pieces(xs_hbm, piece_ref, 0, per_block, xs_buf0, sem.at[0])

    def block(cur, cur_sem, oth, oth_sem):
        @pl.when((i == 0) | (be_ref[i] != be_ref[jnp.maximum(i - 1, 0)]))
        def _():
            wg_bf[...] = wg_ref[...].astype(BF16)
            wu_bf[...] = wu_ref[...].astype(BF16)
            wd_bf[...] = wd_ref[...].astype(BF16)

        _wait_pieces(xs_hbm, cur, cur_sem)
        nxt = jnp.minimum(i + 1, n_used - 1)
        _issue_pieces(xs_hbm, piece_ref, nxt * per_block, per_block, oth, oth_sem)

        xb = cur[...]
        gate = _dot(xb, wg_bf[...])
        up = _dot(xb, wu_bf[...])
        hid = (gate * _sigmoid(gate) * up).astype(BF16)
        ys_ref[...] = _dot(hid, wd_bf[...]).astype(ys_ref.dtype)

        @pl.when(i == n_used - 1)
        def _():
            _wait_pieces(xs_hbm, oth, oth_sem)

    for s, (cur, oth) in enumerate(((xs_buf0, xs_buf1), (xs_buf1, xs_buf0))):
        @pl.when((i < n_used) & (i % 2 == s))
        def _(s=s, cur=cur, oth=oth):
            block(cur, sem.at[s], oth, sem.at[1 - s])

    @pl.when(i >= n_used)
    def _():
        ys_ref[...] = jnp.zeros(ys_ref.shape, ys_ref.dtype)


def _experts(block_e, piece_src, n_used, xs, wg, wu, wd, n_blocks):
    d = xs.shape[1]
    by_expert = lambda i, be, ps, nu: (0, be[i], 0, 0)
    grid_spec = pltpu.PrefetchScalarGridSpec(
        num_scalar_prefetch=3,
        grid=(n_blocks,),
        in_specs=[pl.BlockSpec(memory_space=pl.ANY),
                  pl.BlockSpec((None, None, d, EXPERT_FF), by_expert),
                  pl.BlockSpec((None, None, d, EXPERT_FF), by_expert),
                  pl.BlockSpec((None, None, EXPERT_FF, d), by_expert)],
        out_specs=pl.BlockSpec((ROW_BLOCK, d), lambda i, be, ps, nu: (i, 0)),
        scratch_shapes=[pltpu.VMEM((ROW_BLOCK, d), BF16), pltpu.VMEM((ROW_BLOCK, d), BF16),
                        pltpu.VMEM((d, EXPERT_FF), BF16), pltpu.VMEM((d, EXPERT_FF), BF16),
                        pltpu.VMEM((EXPERT_FF, d), BF16), pltpu.SemaphoreType.DMA((2,))],
    )
    return pl.pallas_call(
        _experts_kernel,
        grid_spec=grid_spec,
        out_shape=jax.ShapeDtypeStruct((n_blocks * ROW_BLOCK, d), BF16),
        compiler_params=pltpu.CompilerParams(
            dimension_semantics=("arbitrary",), vmem_limit_bytes=VMEM_LIMIT),
        name="experts",
    )(block_e, piece_src, n_used, xs, wg, wu, wd)


def _combine_kernel(piece_ref, ys_hbm, x1_ref, route_ref, mod_ref, ln2_ref, o_ref, buf, sem):
    i = pl.program_id(0)
    n = pl.num_programs(0)
    slot = i % 2
    rows = buf.shape[1]
    per_tile = rows // PIECE

    def issue(tile, s):
        def body(p, _):
            _piece_copy(ys_hbm, piece_ref[tile * per_tile + p], buf.at[s], p, sem.at[s]).start()
            return 0
        lax.fori_loop(0, per_tile, body, 0)

    @pl.when(i == 0)
    def _():
        issue(0, 0)

    _wait_pieces(ys_hbm, buf.at[slot], sem.at[slot])

    @pl.when(i + 1 < n)
    def _():
        issue(i + 1, 1 - slot)

    tm = x1_ref.shape[0]
    lane = lax.broadcasted_iota(jnp.int32, (tm, rows), 1).astype(F32)
    sel = (jnp.where(lane == route_ref[:, 4:5], route_ref[:, 2:3], 0.0)
           + jnp.where(lane == route_ref[:, 5:6], route_ref[:, 3:4], 0.0)).astype(BF16)
    moe = _dot(sel, buf[slot])
    z = ALPHA * x1_ref[...] + mod_ref[0:1, :] * moe
    o_ref[...] = _ln(z) * ln2_ref[0:1, :] + ln2_ref[1:2, :]


def _combine(piece_glob, ys, x1, route, mod, ln2, tm, tiles_per_batch):
    t, d = x1.shape
    grid_spec = pltpu.PrefetchScalarGridSpec(
        num_scalar_prefetch=1,
        grid=(t // tm,),
        in_specs=[pl.BlockSpec(memory_space=pl.ANY),
                  pl.BlockSpec((tm, d), lambda i, pg: (i, 0)),
                  pl.BlockSpec((tm, ROUTE_LANES), lambda i, pg: (i, 0)),
                  pl.BlockSpec((None, 8, d), lambda i, pg: (i // tiles_per_batch, 0, 0)),
                  pl.BlockSpec((2, d), lambda i, pg: (0, 0))],
        out_specs=pl.BlockSpec((tm, d), lambda i, pg: (i, 0)),
        scratch_shapes=[pltpu.VMEM((2, _local_rows(tm), d), BF16), pltpu.SemaphoreType.DMA((2,))],
    )
    return pl.pallas_call(
        _combine_kernel,
        grid_spec=grid_spec,
        out_shape=jax.ShapeDtypeStruct((t, d), F32),
        compiler_params=pltpu.CompilerParams(
            dimension_semantics=("arbitrary",), vmem_limit_bytes=VMEM_LIMIT),
        name="combine",
    )(piece_glob, ys, x1, route, mod, ln2)


def _sincos_2d(rows, cols, dim):
    q = dim // 4
    omega = 1.0 / (POS_BASE ** (jnp.arange(q, dtype=F32) / q))
    r = jnp.arange(rows, dtype=F32)[:, None] * omega
    cl = jnp.arange(cols, dtype=F32)[:, None] * omega
    r_emb = jnp.concatenate([jnp.sin(r), jnp.cos(r)], -1)
    c_emb = jnp.concatenate([jnp.sin(cl), jnp.cos(cl)], -1)
    return r_emb, c_emb


def _routing_tables(run_pieces, xs_rows, n_blocks):
    i32 = jnp.int32
    n_tiles = run_pieces.shape[0]
    ppb = ROW_BLOCK // PIECE
    loc_start = jnp.cumsum(run_pieces, axis=1) - run_pieces
    seg_tot = jnp.sum(run_pieces, axis=0)
    seg_pad = (seg_tot + ppb - 1) // ppb * ppb
    seg_end = jnp.cumsum(seg_pad)
    seg_start = seg_end - seg_pad
    run_t = run_pieces.T
    glob_start = seg_start[:, None] + jnp.cumsum(run_t, axis=1) - run_t
    n_used = (seg_end[-1] // ppb).astype(i32)
    blk = jnp.minimum(jnp.arange(n_blocks, dtype=i32), n_used - 1)
    block_e = jnp.minimum(jnp.sum((seg_end[None, :] <= (blk * ppb)[:, None]).astype(i32), axis=1),
                          N_EXPERTS - 1).astype(i32)
    lpt = xs_rows // PIECE
    starts = glob_start.reshape(-1)
    lens = run_t.reshape(-1)
    src0 = (jnp.arange(n_tiles, dtype=i32)[None, :] * lpt + loc_start.T).reshape(-1)
    p = jnp.arange(n_blocks * ppb, dtype=i32)
    within = p[:, None] - starts[None, :]
    hit = (within >= 0) & (within < lens[None, :])
    piece_src = jnp.sum(jnp.where(hit, (src0[None, :] + within) * PIECE, 0), axis=1).astype(i32)
    s = jnp.arange(lpt, dtype=i32)
    loc_within = s[None, :, None] - loc_start[:, None, :]
    hit = (loc_within >= 0) & (loc_within < run_pieces[:, None, :])
    piece_glob = jnp.sum(jnp.where(hit, (glob_start.T[:, None, :] + loc_within) * PIECE, 0), axis=2)
    return block_e, piece_src, piece_glob.astype(i32).reshape(-1), n_used.reshape(1)


def kernel(x, c, ctx, c_ctx, w_ada, b_ada, w_in, s5_log_dt_f, s5_a_re_f, s5_a_im_f, s5_b_re_f, s5_b_im_f, s5_c_re_f, s5_c_im_f, s5_log_dt_b, s5_a_re_b, s5_a_im_b, s5_b_re_b, s5_b_im_b, s5_c_re_b, s5_c_im_b, s5_d, s5_w_glu_val, s5_w_glu_gate, conv_w, conv_w_out, w_o, ln1_g, ln1_b, router_w_group, router_b_group, router_w_expert, router_b_expert, exp_w_gate, exp_w_up, exp_w_down, ln2_g, ln2_b):
    b, l, d = x.shape
    lc = ctx.shape[1]
    assert d == D_MODEL and b < SUBLANES and w_ada.shape[0] == DEPTH
    assert l % (SUBLANES * CHUNK) == 0 and lc % (SUBLANES * CHUNK) == 0 and l % GRID_W == 0
    t = b * l
    tm = min(512, l)
    tmc = min(512, lc)

    cc = jnp.concatenate([c, c_ctx[None, :], jnp.zeros((8 - b - 1, d), F32)], 0)
    mods = _mods(cc, w_ada[0], b_ada[0])
    sh1, sc1, g1, sh2, sc2, g2 = jnp.split(mods, 6, axis=-1)
    mod_a = jnp.stack([sh1[:b], 1.0 + sc1[:b]], 1)
    mod_ctx = jnp.broadcast_to(jnp.stack([sh1[b], 1.0 + sc1[b]], 0)[None], (b, 2, d))
    mod_c = jnp.stack([g1[:b], sh2[:b], 1.0 + sc2[:b], jnp.zeros((b, d), F32)], 1)
    mod_f = jnp.concatenate([g2[:b, None, :], jnp.zeros((b, 7, d), F32)], 1)

    w_in_bf = w_in[0].astype(BF16)
    f_tab = _s5_dir_tables(s5_log_dt_f[0], s5_a_re_f[0], s5_a_im_f[0], s5_b_re_f[0], s5_b_im_f[0],
                           s5_c_re_f[0], s5_c_im_f[0])
    b_tab = _s5_dir_tables(s5_log_dt_b[0], s5_a_re_b[0], s5_a_im_b[0], s5_b_re_b[0], s5_b_im_b[0],
                           s5_c_re_b[0], s5_c_im_b[0])
    mi, ws, wo_s5, tab = _s5_operators(f_tab, b_tab, s5_d[0])

    (uc_ctx,) = _in_proj(ctx, jnp.zeros((lc // GRID_W, d // 2), F32), jnp.zeros((GRID_W, d // 2), F32),
                         mod_ctx, w_in_bf, None, None, tmc, False)
    zero_state = jnp.zeros((N_PAIRS, 4, SUBLANES, LANES), F32)
    _, s0 = _s5_scan(uc_ctx, mi, ws, wo_s5, tab, zero_state, b)

    r_emb, c_emb = _sincos_2d(l // GRID_W, GRID_W, d)
    uc, siga, sb = _in_proj(x, r_emb, c_emb, mod_a, w_in_bf, conv_w[0], conv_w_out[0].astype(BF16), tm, True)
    y, _ = _s5_scan(uc, mi, ws, wo_s5, tab, s0, b)

    wr = jnp.concatenate([router_w_group[0], router_w_expert[0],
                          jnp.zeros((d, ROUTE_LANES - N_EXPERT_GROUPS - N_EXPERTS), F32)], 1).T.astype(BF16)
    br = jnp.concatenate([router_b_group[0], router_b_expert[0],
                          jnp.zeros((ROUTE_LANES - N_EXPERT_GROUPS - N_EXPERTS,), F32)])[:, None]
    ln1 = jnp.stack([ln1_g[0], ln1_b[0]], 0)
    x1, xs, route, run_len = _mix_out(y, siga, sb, x, r_emb, c_emb, mod_c,
                                      s5_w_glu_val[0].astype(BF16), s5_w_glu_gate[0].astype(BF16),
                                      w_o[0].astype(BF16), ln1, wr, br, tm)

    x1 = x1.reshape(t, d)
    route = route.reshape(t, ROUTE_LANES)
    n_tiles = t // tm
    xs_rows = _local_rows(tm)
    run_pieces = run_len[:, :N_EXPERTS, 0].astype(jnp.int32)
    max_rows = t * TOP_K + n_tiles * N_EXPERTS * (PIECE - 1) + N_EXPERTS * (ROW_BLOCK - 1)
    n_blocks = -(-max_rows // ROW_BLOCK)
    block_e, piece_src, piece_glob, n_used = _routing_tables(run_pieces, xs_rows, n_blocks)
    ys = _experts(block_e, piece_src, n_used, xs, exp_w_gate, exp_w_up, exp_w_down, n_blocks)
    ln2 = jnp.stack([ln2_g[0], ln2_b[0]], 0)
    out = _combine(piece_glob, ys, x1, route, mod_f, ln2, tm, l // tm)
    return out.reshape(b, l, d)
```

```python
import functools
import math

import jax
import jax.numpy as jnp
from jax import lax
from jax.experimental import pallas as pl
from jax.experimental.pallas import tpu as pltpu

F32 = jnp.float32
BF16 = jnp.bfloat16
HI = lax.Precision.HIGHEST

D_MODEL = 1024
GRID_W = 64
S5_WIDTH = 512
S5_GROUP_CH = 16
S5_GROUPS = S5_WIDTH // S5_GROUP_CH
S5_STATE = 64
CONV_WIDTH = 512
N_EXPERT_GROUPS = 4
EXPERTS_PER_GROUP = 8
N_EXPERTS = N_EXPERT_GROUPS * EXPERTS_PER_GROUP
EXPERT_FF = 512
TOP_K = 2
DEPTH = 1
ALPHA = (2.0 * DEPTH) ** 0.25
LN_EPS = 1e-6
POS_BASE = 10000.0

LANES = 128
SUBLANES = 8
CHUNK = 16
GROUP_W = CHUNK * S5_GROUP_CH
PAIR_W = 2 * GROUP_W
N_PAIRS = S5_GROUPS // 2
TOK_PER_VREG = LANES // S5_GROUP_CH
TAB_ROWS = 24
ROUTE_LANES = 128
ROW_BLOCK = 512
PIECE = 16
VMEM_LIMIT = 56 * 1024 * 1024


def _ln(x):
    mu = jnp.mean(x, axis=-1, keepdims=True)
    xc = x - mu
    var = jnp.mean(xc * xc, axis=-1, keepdims=True)
    return xc * lax.rsqrt(var + LN_EPS)


def _sigmoid(x):
    return 0.5 * (jnp.tanh(0.5 * x) + 1.0)


def _dot(a, b):
    return jnp.dot(a, b, preferred_element_type=F32)


def _mods_kernel(c_ref, w_ref, b_ref, o_ref):
    c = c_ref[...]
    a = c * _sigmoid(c)
    o_ref[...] = jnp.dot(a, w_ref[...], precision=HI, preferred_element_type=F32) + b_ref[...]


def _mods(cc, w_ada, b_ada):
    n = w_ada.shape[1]
    nb = 1536
    return pl.pallas_call(
        _mods_kernel,
        grid=(n // nb,),
        in_specs=[pl.BlockSpec((8, D_MODEL), lambda i: (0, 0)),
                  pl.BlockSpec((D_MODEL, nb), lambda i: (0, i)),
                  pl.BlockSpec((1, nb), lambda i: (0, i))],
        out_specs=pl.BlockSpec((8, nb), lambda i: (0, i)),
        out_shape=jax.ShapeDtypeStruct((8, n), F32),
        compiler_params=pltpu.CompilerParams(vmem_limit_bytes=VMEM_LIMIT),
        name="mods",
    )(cc, w_ada, b_ada.reshape(1, n))


def _slot_masks(rows):
    slot = lax.broadcasted_iota(jnp.int32, (rows, LANES), 1) // S5_GROUP_CH
    return [slot == s for s in range(TOK_PER_VREG)]


def _to_chunk_tile(u_scr, uc_ref):
    nch = uc_ref.shape[0]
    masks = _slot_masks(nch)
    for qh in range(CHUNK // TOK_PER_VREG):
        for v in range(S5_WIDTH // LANES):
            src = [u_scr[v, pl.ds(qh * TOK_PER_VREG + s, nch, stride=CHUNK), :] for s in range(TOK_PER_VREG)]
            for i in range(TOK_PER_VREG):
                acc = None
                for s in range(TOK_PER_VREG):
                    shift = ((s - i) * S5_GROUP_CH) % LANES
                    piece = pltpu.roll(src[s], shift, 1) if shift else src[s]
                    acc = piece if acc is None else jnp.where(masks[s], piece, acc)
                lo = (v * TOK_PER_VREG + i) * GROUP_W + qh * LANES
                uc_ref[:, lo:lo + LANES] = acc.astype(uc_ref.dtype)


def _from_chunk_tile(yc_ref, y_scr, c0=0, nch=None):
    nch = yc_ref.shape[0] if nch is None else nch
    masks = _slot_masks(nch)
    for qh in range(CHUNK // TOK_PER_VREG):
        for v in range(S5_WIDTH // LANES):
            src = []
            for i in range(TOK_PER_VREG):
                lo = (v * TOK_PER_VREG + i) * GROUP_W + qh * LANES
                src.append(yc_ref[c0:c0 + nch, lo:lo + LANES].astype(F32))
            for s in range(TOK_PER_VREG):
                acc = None
                for i in range(TOK_PER_VREG):
                    shift = ((i - s) * S5_GROUP_CH) % LANES
                    piece = pltpu.roll(src[i], shift, 1) if shift else src[i]
                    acc = piece if acc is None else jnp.where(masks[i], piece, acc)
                y_scr[v, pl.ds(c0 * CHUNK + qh * TOK_PER_VREG + s, nch, stride=CHUNK), :] = acc


def _with_positions(x_ref, remb_ref, cemb_ref, r0=0, rows=None):
    rows = x_ref.shape[0] if rows is None else rows
    c = cemb_ref[...]
    slabs = []
    for j in range(r0 // GRID_W, (r0 + rows) // GRID_W):
        r = jnp.broadcast_to(remb_ref[j:j + 1, :], c.shape)
        slabs.append(x_ref[j * GRID_W:(j + 1) * GRID_W, :] + jnp.concatenate([r, c], axis=-1))
    return jnp.concatenate(slabs, axis=0)


def _in_proj_kernel(x_ref, remb_ref, cemb_ref, mod_ref, w_ref, *rest, full):
    if full:
        cw_ref, cwo_ref, uc_ref, siga_ref, sb_ref, u_scr = rest
    else:
        uc_ref, u_scr = rest
    xp = _with_positions(x_ref, remb_ref, cemb_ref)
    h = (_ln(xp) * mod_ref[1:2, :] + mod_ref[0:1, :]).astype(BF16)
    o1, o2, o3, o4, o5 = 512, 1024, 1536, 2048, 3072
    u = _dot(h, w_ref[:, 0:o1])
    for v in range(S5_WIDTH // LANES):
        u_scr[v] = u[:, v * LANES:(v + 1) * LANES]
    _to_chunk_tile(u_scr, uc_ref)
    if not full:
        return
    z_b = _dot(h, w_ref[:, o1:o2])
    gate_c = _dot(h, w_ref[:, o3:o4])
    p = gate_c * z_b
    tm = p.shape[0]
    col = lax.broadcasted_iota(jnp.int32, (tm, 1), 0) % GRID_W
    prev = jnp.where(col == 0, 0.0, pltpu.roll(p, 1, 0))
    nxt = jnp.where(col == GRID_W - 1, 0.0, pltpu.roll(p, tm - 1, 0))
    v = cw_ref[0:1, :] * prev + cw_ref[1:2, :] * p + cw_ref[2:3, :] * nxt
    gate_b = _dot(h, w_ref[:, o2:o3])
    out_b = _dot((gate_b * v).astype(BF16), cwo_ref[...])
    merge_b = _dot(h, w_ref[:, o5:])
    sb_ref[...] = (_sigmoid(merge_b) * out_b).astype(sb_ref.dtype)
    merge_a = _dot(h, w_ref[:, o4:o5])
    siga_ref[...] = _sigmoid(merge_a).astype(siga_ref.dtype)


def _pos_specs(tm, d):
    return [pl.BlockSpec((tm // GRID_W, d // 2), lambda i, j: (i, 0)),
            pl.BlockSpec((GRID_W, d // 2), lambda i, j: (0, 0))]


def _in_proj(x, r_emb, c_emb, mod, w_in_bf, conv_w, conv_w_out_bf, tm, full):
    b, l, d = x.shape
    n_tiles = l // tm
    grid = (n_tiles, b)
    tok = lambda w: pl.BlockSpec((None, tm, w), lambda i, j: (j, i, 0))
    chunk_spec = pl.BlockSpec((tm // CHUNK, CHUNK * S5_WIDTH), lambda i, j: (j * n_tiles + i, 0))
    chunk_shape = jax.ShapeDtypeStruct((b * l // CHUNK, CHUNK * S5_WIDTH), BF16)
    in_specs = [tok(d)] + _pos_specs(tm, d) + [pl.BlockSpec((None, 2, d), lambda i, j: (j, 0, 0))]
    args = [x, r_emb, c_emb, mod]
    if full:
        in_specs += [pl.BlockSpec(w_in_bf.shape, lambda i, j: (0, 0)),
                     pl.BlockSpec(conv_w.shape, lambda i, j: (0, 0)),
                     pl.BlockSpec(conv_w_out_bf.shape, lambda i, j: (0, 0))]
        args += [w_in_bf, conv_w, conv_w_out_bf]
        out_specs = [chunk_spec, tok(d), tok(d)]
        out_shape = [chunk_shape,
                     jax.ShapeDtypeStruct((b, l, d), BF16),
                     jax.ShapeDtypeStruct((b, l, d), BF16)]
    else:
        in_specs += [pl.BlockSpec((d, S5_WIDTH), lambda i, j: (0, 0))]
        args += [w_in_bf]
        out_specs = [chunk_spec]
        out_shape = [chunk_shape]
    return pl.pallas_call(
        functools.partial(_in_proj_kernel, full=full),
        grid=grid, in_specs=in_specs, out_specs=out_specs, out_shape=out_shape,
        scratch_shapes=[pltpu.VMEM((S5_WIDTH // LANES, tm, LANES), F32)],
        compiler_params=pltpu.CompilerParams(
            dimension_semantics=("arbitrary", "arbitrary"), vmem_limit_bytes=VMEM_LIMIT),
        name="in_proj" if full else "in_proj_ctx",
    )(*args)


def _s5_tables(log_dt, a_re, a_im, b_re, b_im, c_re, c_im):
    f32 = F32
    dt = jnp.exp(log_dt.astype(f32))[..., None]
    a_re = a_re.astype(f32)
    a_im = a_im.astype(f32)
    mag = jnp.exp(dt * a_re)
    ab_re = mag * jnp.cos(dt * a_im)
    ab_im = mag * jnp.sin(dt * a_im)
    den = a_re * a_re + a_im * a_im
    x_re = ab_re - 1.0
    f_re = (x_re * a_re + ab_im * a_im) / den
    f_im = (ab_im * a_re - x_re * a_im) / den
    b_re = b_re.astype(f32)
    b_im = b_im.astype(f32)
    bb_re = f_re[..., None] * b_re - f_im[..., None] * b_im
    bb_im = f_re[..., None] * b_im + f_im[..., None] * b_re
    k = jnp.arange(CHUNK + 1, dtype=f32)[None, :, None, None]
    pmag = jnp.exp(k * (dt * a_re)[:, None])
    p_re = pmag * jnp.cos(k * (dt * a_im)[:, None])
    p_im = pmag * jnp.sin(k * (dt * a_im)[:, None])
    pb_re = p_re[..., None] * bb_re[:, None] - p_im[..., None] * bb_im[:, None]
    pb_im = p_re[..., None] * bb_im[:, None] + p_im[..., None] * bb_re[:, None]
    c_re = c_re.astype(f32)[:, None]
    c_im = c_im.astype(f32)[:, None]
    cp_re = c_re * p_re[:, :, :, None, :] - c_im * p_im[:, :, :, None, :]
    cp_im = -(c_re * p_im[:, :, :, None, :] + c_im * p_re[:, :, :, None, :])
    return dict(p_re=p_re, p_im=p_im, pb_re=pb_re, pb_im=pb_im, cp_re=cp_re, cp_im=cp_im,
                bb_re=bb_re, bb_im=bb_im)


def _lag_kernels(t):
    g, n, c = S5_GROUPS, S5_STATE, S5_GROUP_CH
    k = CHUNK + 1
    lhs = jnp.concatenate([t['cp_re'], t['cp_im']], -1)
    lhs = lhs.transpose(0, 2, 1, 3, 4).reshape(2 * g, k * c, 2 * n)
    rhs = jnp.concatenate([t['bb_re'], t['bb_im']], -2).reshape(2 * g, 2 * n, c)
    out = jnp.einsum('bmn,bnc->bmc', lhs, rhs, precision=HI)
    out = out.reshape(2, g, k, c, c).transpose(0, 2, 1, 4, 3)
    return out[0], out[1]


def _s5_operators(t, s5_d):
    q = CHUNK
    g, n, c = S5_GROUPS, S5_STATE, S5_GROUP_CH
    kern_f, kern_b = _lag_kernels(t)
    k0 = kern_f[0] + kern_b[0] + s5_d.astype(F32)[:, :, None] * jnp.eye(c, dtype=F32)[None]
    kc = jnp.concatenate([kern_b[1:q][::-1], k0[None], kern_f[1:q]], 0)
    kct = kc.transpose(1, 2, 0, 3)
    m_intra = jnp.stack([kct[:, :, q - 1 - i:2 * q - 1 - i, :] for i in range(q)], 1)
    m_intra = m_intra.reshape(g, q * c, q * c)
    w_st = jnp.stack([t['pb_re'][0, :q][::-1], t['pb_im'][0, :q][::-1],
                      t['pb_re'][1, :q], t['pb_im'][1, :q]], 0)
    w_st = w_st.transpose(2, 1, 4, 0, 3).reshape(g, q * c, 4, n)
    w_out = jnp.stack([t['cp_re'][0, 1:], t['cp_im'][0, 1:],
                       t['cp_re'][1, 1:][::-1], t['cp_im'][1, 1:][::-1]], 0)
    w_out = w_out.transpose(2, 0, 4, 1, 3).reshape(g, 4, n, q * c)
    np_ = N_PAIRS
    w_st = w_st.astype(BF16).reshape(np_, 2, q * c, 4, n)
    ws_pair = jnp.concatenate([jnp.pad(w_st[:, 0], ((0, 0), (0, 0), (0, 0), (0, n))),
                               jnp.pad(w_st[:, 1], ((0, 0), (0, 0), (0, 0), (n, 0)))], 1)
    ws_pair = ws_pair.reshape(np_, PAIR_W, 4 * 2 * n)
    w_out = w_out.astype(BF16).reshape(np_, 2, 4, n, q * c)
    wo_pair = jnp.stack([jnp.pad(w_out[:, 0], ((0, 0), (0, 0), (0, 0), (0, q * c))),
                         jnp.pad(w_out[:, 1], ((0, 0), (0, 0), (0, 0), (q * c, 0)))], 2)
    wo_pair = wo_pair.reshape(np_, 4 * 2 * n, PAIR_W)
    tab = _chunk_power_table(t).reshape(2 * TAB_ROWS, np_, 2 * n).transpose(1, 0, 2)
    return m_intra.astype(BF16), ws_pair, wo_pair, tab


def _chunk_power_table(t):
    def cmul(x, y):
        return x[0] * y[0] - x[1] * y[1], x[0] * y[1] + x[1] * y[0]
    p1 = (t['p_re'][:, CHUNK], t['p_im'][:, CHUNK])
    p2 = cmul(p1, p1)
    p4 = cmul(p2, p2)
    p8 = cmul(p4, p4)
    pr = [(jnp.ones_like(p1[0]), jnp.zeros_like(p1[0]))]
    for _ in range(SUBLANES - 1):
        pr.append(cmul(pr[-1], p1))
    pr_re = jnp.stack([p[0] for p in pr], 0)
    pr_im = jnp.stack([p[1] for p in pr], 0)
    pw = jnp.stack([p1[0], p1[1], p2[0], p2[1], p4[0], p4[1], p8[0], p8[1]], 0)
    return jnp.concatenate([pr_re[:, 0], pr_im[:, 0], pw[:, 0],
                            pr_re[::-1, 1], pr_im[::-1, 1], pw[:, 1]], 0)


def _s5_scan_kernel(uc_ref, mi_ref, ws_ref, wo_ref, tab_ref, s0_ref, y_ref, fin_ref, s_scr, in_scr, *, batch):
    rows = uc_ref.shape[0]
    chunks = rows // batch
    n_tiles = chunks // SUBLANES
    u = uc_ref[...]
    s_scr[...] = _dot(u, ws_ref[...])
    row = lax.broadcasted_iota(jnp.int32, (SUBLANES, LANES), 0)

    def tile_scan(r0, backward, c_re, c_im):
        base = TAB_ROWS if backward else 0
        col = 2 * LANES if backward else 0
        rs = pl.ds(r0, SUBLANES)

        def shift(z, k):
            if backward:
                return jnp.where(row < SUBLANES - k, pltpu.roll(z, SUBLANES - k, 0), 0.0)
            return jnp.where(row >= k, pltpu.roll(z, k, 0), 0.0)

        z_re = s_scr[rs, col:col + LANES]
        z_im = s_scr[rs, col + LANES:col + 2 * LANES]
        for k, t in ((1, 16), (2, 18), (4, 20)):
            a_re = tab_ref[base + t:base + t + 1, :]
            a_im = tab_ref[base + t + 1:base + t + 2, :]
            sh_re = shift(z_re, k)
            sh_im = shift(z_im, k)
            z_re, z_im = z_re + (a_re * sh_re - a_im * sh_im), z_im + (a_re * sh_im + a_im * sh_re)
        pr_re = tab_ref[base:base + SUBLANES, :]
        pr_im = tab_ref[base + SUBLANES:base + 2 * SUBLANES, :]
        in_scr[rs, col:col + LANES] = pr_re * c_re - pr_im * c_im + shift(z_re, 1)
        in_scr[rs, col + LANES:col + 2 * LANES] = pr_re * c_im + pr_im * c_re + shift(z_im, 1)
        last = 0 if backward else SUBLANES - 1
        l_re = jnp.broadcast_to(z_re[last:last + 1, :], (SUBLANES, LANES))
        l_im = jnp.broadcast_to(z_im[last:last + 1, :], (SUBLANES, LANES))
        p8_re = tab_ref[base + 22:base + 23, :]
        p8_im = tab_ref[base + 23:base + 24, :]
        return p8_re * c_re - p8_im * c_im + l_re, p8_re * c_im + p8_im * c_re + l_im

    def body(m, carry):
        out = []
        for b in range(batch):
            cf_re, cf_im, cb_re, cb_im = carry[4 * b:4 * b + 4]
            rf = pl.multiple_of(b * chunks + m * SUBLANES, SUBLANES)
            rb = pl.multiple_of(b * chunks + (n_tiles - 1 - m) * SUBLANES, SUBLANES)
            out += list(tile_scan(rf, False, cf_re, cf_im))
            out += list(tile_scan(rb, True, cb_re, cb_im))
        return tuple(out)

    init = tuple(jnp.broadcast_to(s0_ref[t, b:b + 1, :], (SUBLANES, LANES))
                 for b in range(batch) for t in range(4))
    fin = lax.fori_loop(0, n_tiles, body, init, unroll=min(4, n_tiles))
    fin_ref[...] = jnp.zeros(fin_ref.shape, F32)
    for b in range(batch):
        for t in range(4):
            fin_ref[t, b:b + 1, :] = fin[4 * b + t][0:1, :]
    y_intra = jnp.concatenate([_dot(u[:, gl * GROUP_W:(gl + 1) * GROUP_W], mi_ref[gl]) for gl in range(2)], axis=-1)
    y = y_intra + _dot(in_scr[...].astype(BF16), wo_ref[...])
    y_ref[...] = y.astype(y_ref.dtype)


def _s5_scan(uc, mi, ws, wo, tab, s0, batch):
    rows = uc.shape[0]
    pair = lambda *shape: pl.BlockSpec((None,) + shape, lambda p: (p,) + (0,) * len(shape))
    return pl.pallas_call(
        functools.partial(_s5_scan_kernel, batch=batch),
        grid=(N_PAIRS,),
        in_specs=[pl.BlockSpec((rows, PAIR_W), lambda p: (0, p)),
                  pl.BlockSpec((2, GROUP_W, GROUP_W), lambda p: (p, 0, 0)),
                  pair(PAIR_W, PAIR_W), pair(PAIR_W, PAIR_W),
                  pair(2 * TAB_ROWS, LANES), pair(4, SUBLANES, LANES)],
        out_specs=[pl.BlockSpec((rows, PAIR_W), lambda p: (0, p)), pair(4, SUBLANES, LANES)],
        out_shape=[jax.ShapeDtypeStruct((rows, N_PAIRS * PAIR_W), BF16),
                   jax.ShapeDtypeStruct((N_PAIRS, 4, SUBLANES, LANES), F32)],
        scratch_shapes=[pltpu.VMEM((rows, PAIR_W), F32), pltpu.VMEM((rows, PAIR_W), F32)],
        compiler_params=pltpu.CompilerParams(
            dimension_semantics=("arbitrary",), vmem_limit_bytes=VMEM_LIMIT),
        name="s5_scan",
    )(uc, mi, ws, wo, tab, s0)


def _mix_out_kernel(y_ref, siga_ref, sb_ref, x_ref, remb_ref, cemb_ref, mod_ref, wv_ref, wg_ref, wo_ref,
                    ln1_ref, wr_ref, br_ref, ut_ref, lt_ref,
                    x1_ref, xs_ref, route_ref, len_ref, y_scr):
    _from_chunk_tile(y_ref, y_scr)
    y = jnp.concatenate([y_scr[v] for v in range(S5_WIDTH // LANES)], axis=-1)
    ya = (0.5 * y * (1.0 + jnp.tanh(math.sqrt(2.0 / math.pi) * (y + 0.044715 * (y * y * y))))).astype(BF16)
    out_a = _dot(ya, wv_ref[...]) * _sigmoid(_dot(ya, wg_ref[...]))
    merged = siga_ref[...].astype(F32) * out_a + sb_ref[...].astype(F32)
    mix = _dot(merged.astype(BF16), wo_ref[...])
    xp = _with_positions(x_ref, remb_ref, cemb_ref)
    x1 = _ln(ALPHA * xp + mod_ref[0:1, :] * mix) * ln1_ref[0:1, :] + ln1_ref[1:2, :]
    x1_ref[...] = x1
    h_hi = (_ln(x1) * mod_ref[2:3, :] + mod_ref[1:2, :]).astype(BF16)
    _route_and_sort(h_hi, wr_ref, br_ref, ut_ref, lt_ref, xs_ref, route_ref, len_ref)


def _route_and_sort(h_hi, wr_ref, br_ref, ut_ref, lt_ref, xs_ref, route_ref, len_ref):
    tm = h_hi.shape[0]
    f32 = F32
    nt = (((1,), (1,)), ((), ()))
    lg = lax.dot_general(wr_ref[...], h_hi, nt, preferred_element_type=f32) + br_ref[...]
    rowi = lax.broadcasted_iota(jnp.int32, (ROUTE_LANES, tm), 0).astype(f32)
    neg = jnp.float32(-jnp.inf)
    big = jnp.float32(ROUTE_LANES)
    gl = jnp.where(rowi < N_EXPERT_GROUPS, lg, neg)
    gmax = jnp.max(gl, axis=0, keepdims=True)
    g_idx = jnp.min(jnp.where(gl == gmax, rowi, big), axis=0, keepdims=True)
    p_group = 1.0 / jnp.sum(jnp.exp(gl - gmax), axis=0, keepdims=True)
    e_lo = N_EXPERT_GROUPS + g_idx * EXPERTS_PER_GROUP
    el = jnp.where((rowi >= e_lo) & (rowi < e_lo + EXPERTS_PER_GROUP), lg, neg)
    m1 = jnp.max(el, axis=0, keepdims=True)
    i1 = jnp.min(jnp.where(el == m1, rowi, big), axis=0, keepdims=True)
    el2 = jnp.where(rowi == i1, neg, el)
    m2 = jnp.max(el2, axis=0, keepdims=True)
    i2 = jnp.min(jnp.where(el2 == m2, rowi, big), axis=0, keepdims=True)
    r = jnp.exp(m2 - m1)
    w1 = p_group / (1.0 + r)
    w2 = p_group * r / (1.0 + r)
    e1 = i1 - N_EXPERT_GROUPS
    e2 = i2 - N_EXPERT_GROUPS
    a12 = jnp.where(rowi == e1, 1.0, 0.0) + jnp.where(rowi == e2 + N_EXPERTS, 1.0, 0.0)
    rank = _dot(a12.astype(BF16), ut_ref[...])
    cnt = jnp.broadcast_to(jnp.sum(a12, axis=1, keepdims=True), (ROUTE_LANES, LANES))
    row = lax.broadcasted_iota(jnp.int32, (ROUTE_LANES, LANES), 0)
    tot = cnt + pltpu.roll(cnt, ROUTE_LANES - N_EXPERTS, 0)
    run = jnp.where(row < N_EXPERTS, jnp.floor((tot + (PIECE - 1)) * (1.0 / PIECE)), 0.0)
    off = PIECE * _dot(lt_ref[...], run.astype(BF16))
    base = jnp.where(row < N_EXPERTS, off, pltpu.roll(off + cnt, N_EXPERTS, 0))
    posmat = a12 * (rank + base[:, 0:1])
    pos1 = jnp.sum(posmat[0:N_EXPERTS, :], axis=0, keepdims=True)
    pos2 = jnp.sum(posmat[N_EXPERTS:2 * N_EXPERTS, :], axis=0, keepdims=True)
    ri = lax.broadcasted_iota(jnp.int32, (xs_ref.shape[0], tm), 0).astype(f32)
    perm = jnp.where((ri == pos1) | (ri == pos2), 1.0, 0.0).astype(BF16)
    xs_ref[...] = _dot(perm, h_hi).astype(xs_ref.dtype)
    rec = jnp.where(rowi == 0, e1, jnp.where(rowi == 1, e2, jnp.where(rowi == 2, w1, jnp.where(
        rowi == 3, w2, jnp.where(rowi == 4, pos1, jnp.where(rowi == 5, pos2, 0.0))))))
    route_ref[...] = rec.T
    len_ref[...] = run


def _local_rows(tm):
    return TOP_K * tm + N_EXPERTS * PIECE


def _mix_out(y, siga, sb, x, r_emb, c_emb, mod, wv, wg, wo, ln1, wr, br, tm):
    b, l, d = x.shape
    n_tiles = l // tm
    xs_rows = _local_rows(tm)
    tok = lambda w: pl.BlockSpec((None, tm, w), lambda i, j: (j, i, 0))
    whole = lambda a: pl.BlockSpec(a.shape, lambda i, j: (0,) * a.ndim)
    ut = (jnp.arange(tm)[:, None] < jnp.arange(tm)[None, :]).astype(BF16)
    lt = (jnp.arange(ROUTE_LANES)[None, :] < jnp.arange(ROUTE_LANES)[:, None]).astype(BF16)
    return pl.pallas_call(
        _mix_out_kernel,
        grid=(n_tiles, b),
        in_specs=[pl.BlockSpec((tm // CHUNK, CHUNK * S5_WIDTH), lambda i, j: (j * n_tiles + i, 0)),
                  tok(d), tok(d), tok(d)] + _pos_specs(tm, d) + [
                  pl.BlockSpec((None, 4, d), lambda i, j: (j, 0, 0)),
                  whole(wv), whole(wg), whole(wo), whole(ln1), whole(wr), whole(br),
                  whole(ut), whole(lt)],
        out_specs=[tok(d),
                   pl.BlockSpec((xs_rows, d), lambda i, j: (j * n_tiles + i, 0)),
                   tok(ROUTE_LANES),
                   pl.BlockSpec((None, ROUTE_LANES, LANES), lambda i, j: (j * n_tiles + i, 0, 0))],
        out_shape=[jax.ShapeDtypeStruct((b, l, d), F32),
                   jax.ShapeDtypeStruct((b * n_tiles * xs_rows, d), BF16),
                   jax.ShapeDtypeStruct((b, l, ROUTE_LANES), F32),
                   jax.ShapeDtypeStruct((b * n_tiles, ROUTE_LANES, LANES), F32)],
        scratch_shapes=[pltpu.VMEM((S5_WIDTH // LANES, tm, LANES), F32)],
        compiler_params=pltpu.CompilerParams(
            dimension_semantics=("arbitrary", "arbitrary"), vmem_limit_bytes=VMEM_LIMIT),
        name="mix_out",
    )(y, siga, sb, x, r_emb, c_emb, mod, wv, wg, wo, ln1, wr, br, ut, lt)


def _piece_copy(src_hbm, src_row, dst, piece, sem):
    return pltpu.make_async_copy(src_hbm.at[pl.ds(pl.multiple_of(src_row, PIECE), PIECE), :],
                                 dst.at[pl.ds(pl.multiple_of(piece * PIECE, PIECE), PIECE), :], sem)


def _issue_pieces(src_hbm, table_ref, first, n_pieces, dst, sem):
    for p in range(n_pieces):
        _piece_copy(src_hbm, table_ref[first + p], dst, p, sem).start()


def _wait_pieces(src_hbm, dst, sem):
    pltpu.make_async_copy(src_hbm.at[pl.ds(0, dst.shape[0]), :], dst, sem).wait()


def _experts_kernel(be_ref, piece_ref, nused_ref, xs_hbm, wg_ref, wu_ref, wd_ref, ys_ref,
                    xs_buf0, xs_buf1, wg_bf, wu_bf, wd_bf, sem):
    i = pl.program_id(0)
    n_used = nused_ref[0]
    per_block = ROW_BLOCK // PIECE

    @pl.when(i == 0)
    def _():
        _issue_pieces(xs_hbm, piece_ref, 0, per_block, xs_buf0, sem.at[0])

    def block(cur, cur_sem, oth, oth_sem):
        @pl.when((i == 0) | (be_ref[i] != be_ref[jnp.maximum(i - 1, 0)]))
        def _():
            wg_bf[...] = wg_ref[...].astype(BF16)
            wu_bf[...] = wu_ref[...].astype(BF16)
            wd_bf[...] = wd_ref[...].astype(BF16)

        _wait_pieces(xs_hbm, cur, cur_sem)
        nxt = jnp.minimum(i + 1, n_used - 1)
        _issue_pieces(xs_hbm, piece_ref, nxt * per_block, per_block, oth, oth_sem)

        xb = cur[...]
        gate = _dot(xb, wg_bf[...])
        up = _dot(xb, wu_bf[...])
        hid = (gate * _sigmoid(gate) * up).astype(BF16)
        ys_ref[...] = _dot(hid, wd_bf[...]).astype(ys_ref.dtype)

        @pl.when(i == n_used - 1)
        def _():
            _wait_pieces(xs_hbm, oth, oth_sem)

    for s, (cur, oth) in enumerate(((xs_buf0, xs_buf1), (xs_buf1, xs_buf0))):
        @pl.when((i < n_used) & (i % 2 == s))
        def _(s=s, cur=cur, oth=oth):
            block(cur, sem.at[s], oth, sem.at[1 - s])

    @pl.when(i >= n_used)
    def _():
        ys_ref[...] = jnp.zeros(ys_ref.shape, ys_ref.dtype)


def _experts(block_e, piece_src, n_used, xs, wg, wu, wd, n_blocks):
    d = xs.shape[1]
    by_expert = lambda i, be, ps, nu: (0, be[i], 0, 0)
    grid_spec = pltpu.PrefetchScalarGridSpec(
        num_scalar_prefetch=3,
        grid=(n_blocks,),
        in_specs=[pl.BlockSpec(memory_space=pl.ANY),
                  pl.BlockSpec((None, None, d, EXPERT_FF), by_expert),
                  pl.BlockSpec((None, None, d, EXPERT_FF), by_expert),
                  pl.BlockSpec((None, None, EXPERT_FF, d), by_expert)],
        out_specs=pl.BlockSpec((ROW_BLOCK, d), lambda i, be, ps, nu: (i, 0)),
        scratch_shapes=[pltpu.VMEM((ROW_BLOCK, d), BF16), pltpu.VMEM((ROW_BLOCK, d), BF16),
                        pltpu.VMEM((d, EXPERT_FF), BF16), pltpu.VMEM((d, EXPERT_FF), BF16),
                        pltpu.VMEM((EXPERT_FF, d), BF16), pltpu.SemaphoreType.DMA((2,))],
    )
    return pl.pallas_call(
        _experts_kernel,
        grid_spec=grid_spec,
        out_shape=jax.ShapeDtypeStruct((n_blocks * ROW_BLOCK, d), BF16),
        compiler_params=pltpu.CompilerParams(
            dimension_semantics=("arbitrary",), vmem_limit_bytes=VMEM_LIMIT),
        name="experts",
    )(block_e, piece_src, n_used, xs, wg, wu, wd)


def _combine_kernel(piece_ref, ys_hbm, x1_ref, route_ref, mod_ref, ln2_ref, o_ref, buf, sem):
    i = pl.program_id(0)
    n = pl.num_programs(0)
    slot = i % 2
    rows = buf.shape[1]
    per_tile = rows // PIECE

    def issue(tile, s):
        def body(p, _):
            _piece_copy(ys_hbm, piece_ref[tile * per_tile + p], buf.at[s], p, sem.at[s]).start()
            return 0
        lax.fori_loop(0, per_tile, body, 0)

    @pl.when(i == 0)
    def _():
        issue(0, 0)

    _wait_pieces(ys_hbm, buf.at[slot], sem.at[slot])

    @pl.when(i + 1 < n)
    def _():
        issue(i + 1, 1 - slot)

    tm = x1_ref.shape[0]
    lane = lax.broadcasted_iota(jnp.int32, (tm, rows), 1).astype(F32)
    sel = (jnp.where(lane == route_ref[:, 4:5], route_ref[:, 2:3], 0.0)
           + jnp.where(lane == route_ref[:, 5:6], route_ref[:, 3:4], 0.0)).astype(BF16)
    moe = _dot(sel, buf[slot])
    z = ALPHA * x1_ref[...] + mod_ref[0:1, :] * moe
    o_ref[...] = _ln(z) * ln2_ref[0:1, :] + ln2_ref[1:2, :]


def _combine(piece_glob, ys, x1, route, mod, ln2, tm, tiles_per_batch):
    t, d = x1.shape
    grid_spec = pltpu.PrefetchScalarGridSpec(
        num_scalar_prefetch=1,
        grid=(t // tm,),
        in_specs=[pl.BlockSpec(memory_space=pl.ANY),
                  pl.BlockSpec((tm, d), lambda i, pg: (i, 0)),
                  pl.BlockSpec((tm, ROUTE_LANES), lambda i, pg: (i, 0)),
                  pl.BlockSpec((None, 8, d), lambda i, pg: (i // tiles_per_batch, 0, 0)),
                  pl.BlockSpec((2, d), lambda i, pg: (0, 0))],
        out_specs=pl.BlockSpec((tm, d), lambda i, pg: (i, 0)),
        scratch_shapes=[pltpu.VMEM((2, _local_rows(tm), d), BF16), pltpu.SemaphoreType.DMA((2,))],
    )
    return pl.pallas_call(
        _combine_kernel,
        grid_spec=grid_spec,
        out_shape=jax.ShapeDtypeStruct((t, d), F32),
        compiler_params=pltpu.CompilerParams(
            dimension_semantics=("arbitrary",), vmem_limit_bytes=VMEM_LIMIT),
        name="combine",
    )(piece_glob, ys, x1, route, mod, ln2)


def _sincos_2d(rows, cols, dim):
    q = dim // 4
    omega = 1.0 / (POS_BASE ** (jnp.arange(q, dtype=F32) / q))
    r = jnp.arange(rows, dtype=F32)[:, None] * omega
    cl = jnp.arange(cols, dtype=F32)[:, None] * omega
    r_emb = jnp.concatenate([jnp.sin(r), jnp.cos(r)], -1)
    c_emb = jnp.concatenate([jnp.sin(cl), jnp.cos(cl)], -1)
    return r_emb, c_emb


def _routing_tables(run_pieces, xs_rows, n_blocks):
    i32 = jnp.int32
    n_tiles = run_pieces.shape[0]
    ppb = ROW_BLOCK // PIECE
    loc_start = jnp.cumsum(run_pieces, axis=1) - run_pieces
    seg_tot = jnp.sum(run_pieces, axis=0)
    seg_pad = (seg_tot + ppb - 1) // ppb * ppb
    seg_end = jnp.cumsum(seg_pad)
    seg_start = seg_end - seg_pad
    run_t = run_pieces.T
    glob_start = seg_start[:, None] + jnp.cumsum(run_t, axis=1) - run_t
    n_used = (seg_end[-1] // ppb).astype(i32)
    blk = jnp.minimum(jnp.arange(n_blocks, dtype=i32), n_used - 1)
    block_e = jnp.minimum(jnp.sum((seg_end[None, :] <= (blk * ppb)[:, None]).astype(i32), axis=1),
                          N_EXPERTS - 1).astype(i32)
    lpt = xs_rows // PIECE
    starts = glob_start.reshape(-1)
    lens = run_t.reshape(-1)
    src0 = (jnp.arange(n_tiles, dtype=i32)[None, :] * lpt + loc_start.T).reshape(-1)
    p = jnp.arange(n_blocks * ppb, dtype=i32)
    within = p[:, None] - starts[None, :]
    hit = (within >= 0) & (within < lens[None, :])
    piece_src = jnp.sum(jnp.where(hit, (src0[None, :] + within) * PIECE, 0), axis=1).astype(i32)
    s = jnp.arange(lpt, dtype=i32)
    loc_within = s[None, :, None] - loc_start[:, None, :]
    hit = (loc_within >= 0) & (loc_within < run_pieces[:, None, :])
    piece_glob = jnp.sum(jnp.where(hit, (glob_start.T[:, None, :] + loc_within) * PIECE, 0), axis=2)
    return block_e, piece_src, piece_glob.astype(i32).reshape(-1), n_used.reshape(1)


def kernel(x, c, ctx, c_ctx, w_ada, b_ada, w_in, s5_log_dt_f, s5_a_re_f, s5_a_im_f, s5_b_re_f, s5_b_im_f, s5_c_re_f, s5_c_im_f, s5_log_dt_b, s5_a_re_b, s5_a_im_b, s5_b_re_b, s5_b_im_b, s5_c_re_b, s5_c_im_b, s5_d, s5_w_glu_val, s5_w_glu_gate, conv_w, conv_w_out, w_o, ln1_g, ln1_b, router_w_group, router_b_group, router_w_expert, router_b_expert, exp_w_gate, exp_w_up, exp_w_down, ln2_g, ln2_b):
    b, l, d = x.shape
    lc = ctx.shape[1]
    assert d == D_MODEL and b < SUBLANES and w_ada.shape[0] == DEPTH
    assert l % (SUBLANES * CHUNK) == 0 and lc % (SUBLANES * CHUNK) == 0 and l % GRID_W == 0
    t = b * l
    tm = min(512, l)
    tmc = min(512, lc)

    cc = jnp.concatenate([c, c_ctx[None, :], jnp.zeros((8 - b - 1, d), F32)], 0)
    mods = _mods(cc, w_ada[0], b_ada[0])
    sh1, sc1, g1, sh2, sc2, g2 = jnp.split(mods, 6, axis=-1)
    mod_a = jnp.stack([sh1[:b], 1.0 + sc1[:b]], 1)
    mod_ctx = jnp.broadcast_to(jnp.stack([sh1[b], 1.0 + sc1[b]], 0)[None], (b, 2, d))
    mod_c = jnp.stack([g1[:b], sh2[:b], 1.0 + sc2[:b], jnp.zeros((b, d), F32)], 1)
    mod_f = jnp.concatenate([g2[:b, None, :], jnp.zeros((b, 7, d), F32)], 1)

    w_in_bf = w_in[0].astype(BF16)
    both = lambda fwd, bwd: jnp.concatenate([fwd, bwd], 0)
    s5_tab = _s5_tables(both(s5_log_dt_f, s5_log_dt_b), both(s5_a_re_f, s5_a_re_b), both(s5_a_im_f, s5_a_im_b),
                        both(s5_b_re_f, s5_b_re_b), both(s5_b_im_f, s5_b_im_b),
                        both(s5_c_re_f, s5_c_re_b), both(s5_c_im_f, s5_c_im_b))
    mi, ws, wo_s5, tab = _s5_operators(s5_tab, s5_d[0])

    (uc_ctx,) = _in_proj(ctx, jnp.zeros((lc // GRID_W, d // 2), F32), jnp.zeros((GRID_W, d // 2), F32),
                         mod_ctx, w_in_bf, None, None, tmc, False)
    zero_state = jnp.zeros((N_PAIRS, 4, SUBLANES, LANES), F32)
    _, s0 = _s5_scan(uc_ctx, mi, ws, wo_s5, tab, zero_state, b)

    r_emb, c_emb = _sincos_2d(l // GRID_W, GRID_W, d)
    uc, siga, sb = _in_proj(x, r_emb, c_emb, mod_a, w_in_bf, conv_w[0], conv_w_out[0].astype(BF16), tm, True)
    y, _ = _s5_scan(uc, mi, ws, wo_s5, tab, s0, b)

    wr = jnp.concatenate([router_w_group[0], router_w_expert[0],
                          jnp.zeros((d, ROUTE_LANES - N_EXPERT_GROUPS - N_EXPERTS), F32)], 1).T.astype(BF16)
    br = jnp.concatenate([router_b_group[0], router_b_expert[0],
                          jnp.zeros((ROUTE_LANES - N_EXPERT_GROUPS - N_EXPERTS,), F32)])[:, None]
    ln1 = jnp.stack([ln1_g[0], ln1_b[0]], 0)
    x1, xs, route, run_len = _mix_out(y, siga, sb, x, r_emb, c_emb, mod_c,
                                      s5_w_glu_val[0].astype(BF16), s5_w_glu_gate[0].astype(BF16),
                                      w_o[0].astype(BF16), ln1, wr, br, tm)

    x1 = x1.reshape(t, d)
    route = route.reshape(t, ROUTE_LANES)
    n_tiles = t // tm
    xs_rows = _local_rows(tm)
    run_pieces = run_len[:, :N_EXPERTS, 0].astype(jnp.int32)
    max_rows = t * TOP_K + n_tiles * N_EXPERTS * (PIECE - 1) + N_EXPERTS * (ROW_BLOCK - 1)
    n_blocks = -(-max_rows // ROW_BLOCK)
    block_e, piece_src, piece_glob, n_used = _routing_tables(run_pieces, xs_rows, n_blocks)
    ys = _experts(block_e, piece_src, n_used, xs, exp_w_gate, exp_w_up, exp_w_down, n_blocks)
    ln2 = jnp.stack([ln2_g[0], ln2_b[0]], 0)
    out = _combine(piece_glob, ys, x1, route, mod_f, ln2, tm, l // tm)
    return out.reshape(b, l, d)
```

```python
import functools
import math

import jax
import jax.numpy as jnp
from jax import lax
from jax.experimental import pallas as pl
from jax.experimental.pallas import tpu as pltpu

F32 = jnp.float32
BF16 = jnp.bfloat16
HI = lax.Precision.HIGHEST

D_MODEL = 1024
GRID_W = 64
S5_WIDTH = 512
S5_GROUP_CH = 16
S5_GROUPS = S5_WIDTH // S5_GROUP_CH
S5_STATE = 64
CONV_WIDTH = 512
N_EXPERT_GROUPS = 4
EXPERTS_PER_GROUP = 8
N_EXPERTS = N_EXPERT_GROUPS * EXPERTS_PER_GROUP
EXPERT_FF = 512
TOP_K = 2
DEPTH = 1
ALPHA = (2.0 * DEPTH) ** 0.25
LN_EPS = 1e-6
POS_BASE = 10000.0

LANES = 128
SUBLANES = 8
CHUNK = 16
GROUP_W = CHUNK * S5_GROUP_CH
PAIR_W = 2 * GROUP_W
N_PAIRS = S5_GROUPS // 2
TOK_PER_VREG = LANES // S5_GROUP_CH
TAB_ROWS = 24
ROUTE_LANES = 128
ROW_BLOCK = 1024
PIECE = 16
VMEM_LIMIT = 56 * 1024 * 1024


def _ln(x):
    mu = jnp.mean(x, axis=-1, keepdims=True)
    xc = x - mu
    var = jnp.mean(xc * xc, axis=-1, keepdims=True)
    return xc * lax.rsqrt(var + LN_EPS)


def _sigmoid(x):
    return 0.5 * (jnp.tanh(0.5 * x) + 1.0)


def _dot(a, b):
    return jnp.dot(a, b, preferred_element_type=F32)


def _mods_kernel(c_ref, w_ref, b_ref, o_ref):
    c = c_ref[...]
    a = c * _sigmoid(c)
    o_ref[...] = jnp.dot(a, w_ref[...], precision=HI, preferred_element_type=F32) + b_ref[...]


def _mods(cc, w_ada, b_ada):
    n = w_ada.shape[1]
    nb = 1536
    return pl.pallas_call(
        _mods_kernel,
        grid=(n // nb,),
        in_specs=[pl.BlockSpec((8, D_MODEL), lambda i: (0, 0)),
                  pl.BlockSpec((D_MODEL, nb), lambda i: (0, i)),
                  pl.BlockSpec((1, nb), lambda i: (0, i))],
        out_specs=pl.BlockSpec((8, nb), lambda i: (0, i)),
        out_shape=jax.ShapeDtypeStruct((8, n), F32),
        compiler_params=pltpu.CompilerParams(vmem_limit_bytes=VMEM_LIMIT),
        name="mods",
    )(cc, w_ada, b_ada.reshape(1, n))


def _slot_masks(rows):
    slot = lax.broadcasted_iota(jnp.int32, (rows, LANES), 1) // S5_GROUP_CH
    return [slot == s for s in range(TOK_PER_VREG)]


def _to_chunk_tile(u_scr, uc_ref):
    nch = uc_ref.shape[0]
    masks = _slot_masks(nch)
    for qh in range(CHUNK // TOK_PER_VREG):
        for v in range(S5_WIDTH // LANES):
            src = [u_scr[v, pl.ds(qh * TOK_PER_VREG + s, nch, stride=CHUNK), :] for s in range(TOK_PER_VREG)]
            for i in range(TOK_PER_VREG):
                acc = None
                for s in range(TOK_PER_VREG):
                    shift = ((s - i) * S5_GROUP_CH) % LANES
                    piece = pltpu.roll(src[s], shift, 1) if shift else src[s]
                    acc = piece if acc is None else jnp.where(masks[s], piece, acc)
                lo = (v * TOK_PER_VREG + i) * GROUP_W + qh * LANES
                uc_ref[:, lo:lo + LANES] = acc.astype(uc_ref.dtype)


def _from_chunk_tile(yc_ref, y_scr, c0=0, nch=None):
    nch = yc_ref.shape[0] if nch is None else nch
    masks = _slot_masks(nch)
    for qh in range(CHUNK // TOK_PER_VREG):
        for v in range(S5_WIDTH // LANES):
            src = []
            for i in range(TOK_PER_VREG):
                lo = (v * TOK_PER_VREG + i) * GROUP_W + qh * LANES
                src.append(yc_ref[c0:c0 + nch, lo:lo + LANES].astype(F32))
            for s in range(TOK_PER_VREG):
                acc = None
                for i in range(TOK_PER_VREG):
                    shift = ((i - s) * S5_GROUP_CH) % LANES
                    piece = pltpu.roll(src[i], shift, 1) if shift else src[i]
                    acc = piece if acc is None else jnp.where(masks[i], piece, acc)
                y_scr[v, pl.ds(c0 * CHUNK + qh * TOK_PER_VREG + s, nch, stride=CHUNK), :] = acc


def _with_positions(x_ref, remb_ref, cemb_ref, r0=0, rows=None):
    rows = x_ref.shape[0] if rows is None else rows
    c = cemb_ref[...]
    slabs = []
    for j in range(r0 // GRID_W, (r0 + rows) // GRID_W):
        r = jnp.broadcast_to(remb_ref[j:j + 1, :], c.shape)
        slabs.append(x_ref[j * GRID_W:(j + 1) * GRID_W, :] + jnp.concatenate([r, c], axis=-1))
    return jnp.concatenate(slabs, axis=0)


def _in_proj_kernel(x_ref, remb_ref, cemb_ref, mod_ref, w_ref, *rest, full):
    if full:
        cw_ref, cwo_ref, uc_ref, siga_ref, sb_ref, u_scr = rest
    else:
        uc_ref, u_scr = rest
    xp = _with_positions(x_ref, remb_ref, cemb_ref)
    h = (_ln(xp) * mod_ref[1:2, :] + mod_ref[0:1, :]).astype(BF16)
    o1, o2, o3, o4, o5 = 512, 1024, 1536, 2048, 3072
    u = _dot(h, w_ref[:, 0:o1])
    for v in range(S5_WIDTH // LANES):
        u_scr[v] = u[:, v * LANES:(v + 1) * LANES]
    _to_chunk_tile(u_scr, uc_ref)
    if not full:
        return
    z_b = _dot(h, w_ref[:, o1:o2])
    gate_c = _dot(h, w_ref[:, o3:o4])
    p = gate_c * z_b
    tm = p.shape[0]
    col = lax.broadcasted_iota(jnp.int32, (tm, 1), 0) % GRID_W
    prev = jnp.where(col == 0, 0.0, pltpu.roll(p, 1, 0))
    nxt = jnp.where(col == GRID_W - 1, 0.0, pltpu.roll(p, tm - 1, 0))
    v = cw_ref[0:1, :] * prev + cw_ref[1:2, :] * p + cw_ref[2:3, :] * nxt
    gate_b = _dot(h, w_ref[:, o2:o3])
    out_b = _dot((gate_b * v).astype(BF16), cwo_ref[...])
    merge_b = _dot(h, w_ref[:, o5:])
    sb_ref[...] = (_sigmoid(merge_b) * out_b).astype(sb_ref.dtype)
    merge_a = _dot(h, w_ref[:, o4:o5])
    siga_ref[...] = _sigmoid(merge_a).astype(siga_ref.dtype)


def _pos_specs(tm, d):
    return [pl.BlockSpec((tm // GRID_W, d // 2), lambda i, j: (i, 0)),
            pl.BlockSpec((GRID_W, d // 2), lambda i, j: (0, 0))]


def _in_proj(x, r_emb, c_emb, mod, w_in_bf, conv_w, conv_w_out_bf, tm, full):
    b, l, d = x.shape
    n_tiles = l // tm
    grid = (n_tiles, b)
    tok = lambda w: pl.BlockSpec((None, tm, w), lambda i, j: (j, i, 0))
    chunk_spec = pl.BlockSpec((tm // CHUNK, CHUNK * S5_WIDTH), lambda i, j: (j * n_tiles + i, 0))
    chunk_shape = jax.ShapeDtypeStruct((b * l // CHUNK, CHUNK * S5_WIDTH), BF16)
    in_specs = [tok(d)] + _pos_specs(tm, d) + [pl.BlockSpec((None, 2, d), lambda i, j: (j, 0, 0))]
    args = [x, r_emb, c_emb, mod]
    if full:
        in_specs += [pl.BlockSpec(w_in_bf.shape, lambda i, j: (0, 0)),
                     pl.BlockSpec(conv_w.shape, lambda i, j: (0, 0)),
                     pl.BlockSpec(conv_w_out_bf.shape, lambda i, j: (0, 0))]
        args += [w_in_bf, conv_w, conv_w_out_bf]
        out_specs = [chunk_spec, tok(d), tok(d)]
        out_shape = [chunk_shape,
                     jax.ShapeDtypeStruct((b, l, d), BF16),
                     jax.ShapeDtypeStruct((b, l, d), BF16)]
    else:
        in_specs += [pl.BlockSpec((d, S5_WIDTH), lambda i, j: (0, 0))]
        args += [w_in_bf]
        out_specs = [chunk_spec]
        out_shape = [chunk_shape]
    return pl.pallas_call(
        functools.partial(_in_proj_kernel, full=full),
        grid=grid, in_specs=in_specs, out_specs=out_specs, out_shape=out_shape,
        scratch_shapes=[pltpu.VMEM((S5_WIDTH // LANES, tm, LANES), F32)],
        compiler_params=pltpu.CompilerParams(
            dimension_semantics=("arbitrary", "arbitrary"), vmem_limit_bytes=VMEM_LIMIT),
        name="in_proj" if full else "in_proj_ctx",
    )(*args)


def _s5_tables(log_dt, a_re, a_im, b_re, b_im, c_re, c_im):
    f32 = F32
    dt = jnp.exp(log_dt.astype(f32))[..., None]
    a_re = a_re.astype(f32)
    a_im = a_im.astype(f32)
    mag = jnp.exp(dt * a_re)
    ab_re = mag * jnp.cos(dt * a_im)
    ab_im = mag * jnp.sin(dt * a_im)
    den = a_re * a_re + a_im * a_im
    x_re = ab_re - 1.0
    f_re = (x_re * a_re + ab_im * a_im) / den
    f_im = (ab_im * a_re - x_re * a_im) / den
    b_re = b_re.astype(f32)
    b_im = b_im.astype(f32)
    bb_re = f_re[..., None] * b_re - f_im[..., None] * b_im
    bb_im = f_re[..., None] * b_im + f_im[..., None] * b_re
    k = jnp.arange(CHUNK + 1, dtype=f32)[None, :, None, None]
    pmag = jnp.exp(k * (dt * a_re)[:, None])
    p_re = pmag * jnp.cos(k * (dt * a_im)[:, None])
    p_im = pmag * jnp.sin(k * (dt * a_im)[:, None])
    pb_re = p_re[..., None] * bb_re[:, None] - p_im[..., None] * bb_im[:, None]
    pb_im = p_re[..., None] * bb_im[:, None] + p_im[..., None] * bb_re[:, None]
    c_re = c_re.astype(f32)[:, None]
    c_im = c_im.astype(f32)[:, None]
    cp_re = c_re * p_re[:, :, :, None, :] - c_im * p_im[:, :, :, None, :]
    cp_im = -(c_re * p_im[:, :, :, None, :] + c_im * p_re[:, :, :, None, :])
    return dict(p_re=p_re, p_im=p_im, pb_re=pb_re, pb_im=pb_im, cp_re=cp_re, cp_im=cp_im,
                bb_re=bb_re, bb_im=bb_im)


def _lag_kernels(t):
    g, n, c = S5_GROUPS, S5_STATE, S5_GROUP_CH
    k = CHUNK + 1
    lhs = jnp.concatenate([t['cp_re'], t['cp_im']], -1)
    lhs = lhs.transpose(0, 2, 1, 3, 4).reshape(2 * g, k * c, 2 * n)
    rhs = jnp.concatenate([t['bb_re'], t['bb_im']], -2).reshape(2 * g, 2 * n, c)
    out = jnp.einsum('bmn,bnc->bmc', lhs, rhs, precision=HI)
    out = out.reshape(2, g, k, c, c).transpose(0, 2, 1, 4, 3)
    return out[0], out[1]


def _s5_operators(t, s5_d):
    q = CHUNK
    g, n, c = S5_GROUPS, S5_STATE, S5_GROUP_CH
    kern_f, kern_b = _lag_kernels(t)
    k0 = kern_f[0] + kern_b[0] + s5_d.astype(F32)[:, :, None] * jnp.eye(c, dtype=F32)[None]
    kc = jnp.concatenate([kern_b[1:q][::-1], k0[None], kern_f[1:q]], 0)
    kct = kc.transpose(1, 2, 0, 3)
    m_intra = jnp.stack([kct[:, :, q - 1 - i:2 * q - 1 - i, :] for i in range(q)], 1)
    m_intra = m_intra.reshape(g, q * c, q * c)
    w_st = jnp.stack([t['pb_re'][0, :q][::-1], t['pb_im'][0, :q][::-1],
                      t['pb_re'][1, :q], t['pb_im'][1, :q]], 0)
    w_st = w_st.transpose(2, 1, 4, 0, 3).reshape(g, q * c, 4, n)
    w_out = jnp.stack([t['cp_re'][0, 1:], t['cp_im'][0, 1:],
                       t['cp_re'][1, 1:][::-1], t['cp_im'][1, 1:][::-1]], 0)
    w_out = w_out.transpose(2, 0, 4, 1, 3).reshape(g, 4, n, q * c)
    np_ = N_PAIRS
    w_st = w_st.astype(BF16).reshape(np_, 2, q * c, 4, n)
    ws_pair = jnp.concatenate([jnp.pad(w_st[:, 0], ((0, 0), (0, 0), (0, 0), (0, n))),
                               jnp.pad(w_st[:, 1], ((0, 0), (0, 0), (0, 0), (n, 0)))], 1)
    ws_pair = ws_pair.reshape(np_, PAIR_W, 4 * 2 * n)
    w_out = w_out.astype(BF16).reshape(np_, 2, 4, n, q * c)
    wo_pair = jnp.stack([jnp.pad(w_out[:, 0], ((0, 0), (0, 0), (0, 0), (0, q * c))),
                         jnp.pad(w_out[:, 1], ((0, 0), (0, 0), (0, 0), (q * c, 0)))], 2)
    wo_pair = wo_pair.reshape(np_, 4 * 2 * n, PAIR_W)
    tab = _chunk_power_table(t).reshape(2 * TAB_ROWS, np_, 2 * n).transpose(1, 0, 2)
    return m_intra.astype(BF16), ws_pair, wo_pair, tab


def _chunk_power_table(t):
    def cmul(x, y):
        return x[0] * y[0] - x[1] * y[1], x[0] * y[1] + x[1] * y[0]
    p1 = (t['p_re'][:, CHUNK], t['p_im'][:, CHUNK])
    p2 = cmul(p1, p1)
    p4 = cmul(p2, p2)
    p8 = cmul(p4, p4)
    pr = [(jnp.ones_like(p1[0]), jnp.zeros_like(p1[0]))]
    for _ in range(SUBLANES - 1):
        pr.append(cmul(pr[-1], p1))
    pr_re = jnp.stack([p[0] for p in pr], 0)
    pr_im = jnp.stack([p[1] for p in pr], 0)
    pw = jnp.stack([p1[0], p1[1], p2[0], p2[1], p4[0], p4[1], p8[0], p8[1]], 0)
    return jnp.concatenate([pr_re[:, 0], pr_im[:, 0], pw[:, 0],
                            pr_re[::-1, 1], pr_im[::-1, 1], pw[:, 1]], 0)


def _s5_scan_kernel(uc_ref, mi_ref, ws_ref, wo_ref, tab_ref, s0_ref, y_ref, fin_ref, s_scr, in_scr, *, batch):
    rows = uc_ref.shape[0]
    chunks = rows // batch
    n_tiles = chunks // SUBLANES
    u = uc_ref[...]
    s_scr[...] = _dot(u, ws_ref[...])
    row = lax.broadcasted_iota(jnp.int32, (SUBLANES, LANES), 0)

    def tile_scan(r0, backward, c_re, c_im):
        base = TAB_ROWS if backward else 0
        col = 2 * LANES if backward else 0
        rs = pl.ds(r0, SUBLANES)

        def shift(z, k):
            if backward:
                return jnp.where(row < SUBLANES - k, pltpu.roll(z, SUBLANES - k, 0), 0.0)
            return jnp.where(row >= k, pltpu.roll(z, k, 0), 0.0)

        z_re = s_scr[rs, col:col + LANES]
        z_im = s_scr[rs, col + LANES:col + 2 * LANES]
        for k, t in ((1, 16), (2, 18), (4, 20)):
            a_re = tab_ref[base + t:base + t + 1, :]
            a_im = tab_ref[base + t + 1:base + t + 2, :]
            sh_re = shift(z_re, k)
            sh_im = shift(z_im, k)
            z_re, z_im = z_re + (a_re * sh_re - a_im * sh_im), z_im + (a_re * sh_im + a_im * sh_re)
        pr_re = tab_ref[base:base + SUBLANES, :]
        pr_im = tab_ref[base + SUBLANES:base + 2 * SUBLANES, :]
        in_scr[rs, col:col + LANES] = pr_re * c_re - pr_im * c_im + shift(z_re, 1)
        in_scr[rs, col + LANES:col + 2 * LANES] = pr_re * c_im + pr_im * c_re + shift(z_im, 1)
        last = 0 if backward else SUBLANES - 1
        l_re = jnp.broadcast_to(z_re[last:last + 1, :], (SUBLANES, LANES))
        l_im = jnp.broadcast_to(z_im[last:last + 1, :], (SUBLANES, LANES))
        p8_re = tab_ref[base + 22:base + 23, :]
        p8_im = tab_ref[base + 23:base + 24, :]
        return p8_re * c_re - p8_im * c_im + l_re, p8_re * c_im + p8_im * c_re + l_im

    def body(m, carry):
        out = []
        for b in range(batch):
            cf_re, cf_im, cb_re, cb_im = carry[4 * b:4 * b + 4]
            rf = pl.multiple_of(b * chunks + m * SUBLANES, SUBLANES)
            rb = pl.multiple_of(b * chunks + (n_tiles - 1 - m) * SUBLANES, SUBLANES)
            out += list(tile_scan(rf, False, cf_re, cf_im))
            out += list(tile_scan(rb, True, cb_re, cb_im))
        return tuple(out)

    init = tuple(jnp.broadcast_to(s0_ref[t, b:b + 1, :], (SUBLANES, LANES))
                 for b in range(batch) for t in range(4))
    fin = lax.fori_loop(0, n_tiles, body, init, unroll=min(4, n_tiles))
    fin_ref[...] = jnp.zeros(fin_ref.shape, F32)
    for b in range(batch):
        for t in range(4):
            fin_ref[t, b:b + 1, :] = fin[4 * b + t][0:1, :]
    y_intra = jnp.concatenate([_dot(u[:, gl * GROUP_W:(gl + 1) * GROUP_W], mi_ref[gl]) for gl in range(2)], axis=-1)
    y = y_intra + _dot(in_scr[...].astype(BF16), wo_ref[...])
    y_ref[...] = y.astype(y_ref.dtype)


def _s5_scan(uc, mi, ws, wo, tab, s0, batch):
    rows = uc.shape[0]
    pair = lambda *shape: pl.BlockSpec((None,) + shape, lambda p: (p,) + (0,) * len(shape))
    return pl.pallas_call(
        functools.partial(_s5_scan_kernel, batch=batch),
        grid=(N_PAIRS,),
        in_specs=[pl.BlockSpec((rows, PAIR_W), lambda p: (0, p)),
                  pl.BlockSpec((2, GROUP_W, GROUP_W), lambda p: (p, 0, 0)),
                  pair(PAIR_W, PAIR_W), pair(PAIR_W, PAIR_W),
                  pair(2 * TAB_ROWS, LANES), pair(4, SUBLANES, LANES)],
        out_specs=[pl.BlockSpec((rows, PAIR_W), lambda p: (0, p)), pair(4, SUBLANES, LANES)],
        out_shape=[jax.ShapeDtypeStruct((rows, N_PAIRS * PAIR_W), BF16),
                   jax.ShapeDtypeStruct((N_PAIRS, 4, SUBLANES, LANES), F32)],
        scratch_shapes=[pltpu.VMEM((rows, PAIR_W), F32), pltpu.VMEM((rows, PAIR_W), F32)],
        compiler_params=pltpu.CompilerParams(
            dimension_semantics=("arbitrary",), vmem_limit_bytes=VMEM_LIMIT),
        name="s5_scan",
    )(uc, mi, ws, wo, tab, s0)


def _mix_out_kernel(y_ref, siga_ref, sb_ref, x_ref, remb_ref, cemb_ref, mod_ref, wv_ref, wg_ref, wo_ref,
                    ln1_ref, wr_ref, br_ref, ut_ref, lt_ref,
                    x1_ref, xs_ref, route_ref, len_ref, y_scr):
    _from_chunk_tile(y_ref, y_scr)
    y = jnp.concatenate([y_scr[v] for v in range(S5_WIDTH // LANES)], axis=-1)
    ya = (0.5 * y * (1.0 + jnp.tanh(math.sqrt(2.0 / math.pi) * (y + 0.044715 * (y * y * y))))).astype(BF16)
    out_a = _dot(ya, wv_ref[...]) * _sigmoid(_dot(ya, wg_ref[...]))
    merged = siga_ref[...] * out_a.astype(BF16) + sb_ref[...]
    mix = _dot(merged, wo_ref[...])
    xp = _with_positions(x_ref, remb_ref, cemb_ref)
    x1 = _ln(ALPHA * xp + mod_ref[0:1, :] * mix) * ln1_ref[0:1, :] + ln1_ref[1:2, :]
    x1_ref[...] = x1
    h_hi = (_ln(x1) * mod_ref[2:3, :] + mod_ref[1:2, :]).astype(BF16)
    _route_and_sort(h_hi, wr_ref, br_ref, ut_ref, lt_ref, xs_ref, route_ref, len_ref)


def _route_and_sort(h_hi, wr_ref, br_ref, ut_ref, lt_ref, xs_ref, route_ref, len_ref):
    tm = h_hi.shape[0]
    f32 = F32
    nt = (((1,), (1,)), ((), ()))
    lg = lax.dot_general(wr_ref[...], h_hi, nt, preferred_element_type=f32) + br_ref[...]
    rowi = lax.broadcasted_iota(jnp.int32, (ROUTE_LANES, tm), 0).astype(f32)
    neg = jnp.float32(-jnp.inf)
    big = jnp.float32(ROUTE_LANES)
    gl = jnp.where(rowi < N_EXPERT_GROUPS, lg, neg)
    gmax = jnp.max(gl, axis=0, keepdims=True)
    g_idx = jnp.min(jnp.where(gl == gmax, rowi, big), axis=0, keepdims=True)
    p_group = 1.0 / jnp.sum(jnp.exp(gl - gmax), axis=0, keepdims=True)
    e_lo = N_EXPERT_GROUPS + g_idx * EXPERTS_PER_GROUP
    el = jnp.where((rowi >= e_lo) & (rowi < e_lo + EXPERTS_PER_GROUP), lg, neg)
    m1 = jnp.max(el, axis=0, keepdims=True)
    i1 = jnp.min(jnp.where(el == m1, rowi, big), axis=0, keepdims=True)
    el2 = jnp.where(rowi == i1, neg, el)
    m2 = jnp.max(el2, axis=0, keepdims=True)
    i2 = jnp.min(jnp.where(el2 == m2, rowi, big), axis=0, keepdims=True)
    r = jnp.exp(m2 - m1)
    w1 = p_group / (1.0 + r)
    w2 = p_group * r / (1.0 + r)
    e1 = i1 - N_EXPERT_GROUPS
    e2 = i2 - N_EXPERT_GROUPS
    a12 = jnp.where(rowi == e1, 1.0, 0.0) + jnp.where(rowi == e2 + N_EXPERTS, 1.0, 0.0)
    rank = _dot(a12.astype(BF16), ut_ref[...])
    cnt = jnp.broadcast_to(jnp.sum(a12, axis=1, keepdims=True), (ROUTE_LANES, LANES))
    row = lax.broadcasted_iota(jnp.int32, (ROUTE_LANES, LANES), 0)
    tot = cnt + pltpu.roll(cnt, ROUTE_LANES - N_EXPERTS, 0)
    run = jnp.where(row < N_EXPERTS, jnp.floor((tot + (PIECE - 1)) * (1.0 / PIECE)), 0.0)
    off = PIECE * _dot(lt_ref[...], run.astype(BF16))
    base = jnp.where(row < N_EXPERTS, off, pltpu.roll(off + cnt, N_EXPERTS, 0))
    posmat = a12 * (rank + base[:, 0:1])
    pos1 = jnp.sum(posmat[0:N_EXPERTS, :], axis=0, keepdims=True)
    pos2 = jnp.sum(posmat[N_EXPERTS:2 * N_EXPERTS, :], axis=0, keepdims=True)
    ri = lax.broadcasted_iota(jnp.int32, (xs_ref.shape[0], tm), 0).astype(f32)
    perm = jnp.where((ri == pos1) | (ri == pos2), 1.0, 0.0).astype(BF16)
    xs_ref[...] = _dot(perm, h_hi).astype(xs_ref.dtype)
    rec = jnp.where(rowi == 0, e1, jnp.where(rowi == 1, e2, jnp.where(rowi == 2, w1, jnp.where(
        rowi == 3, w2, jnp.where(rowi == 4, pos1, jnp.where(rowi == 5, pos2, 0.0))))))
    route_ref[...] = rec.T
    len_ref[...] = run


def _local_rows(tm):
    return TOP_K * tm + N_EXPERTS * PIECE


def _mix_out(y, siga, sb, x, r_emb, c_emb, mod, wv, wg, wo, ln1, wr, br, tm):
    b, l, d = x.shape
    n_tiles = l // tm
    xs_rows = _local_rows(tm)
    tok = lambda w: pl.BlockSpec((None, tm, w), lambda i, j: (j, i, 0))
    whole = lambda a: pl.BlockSpec(a.shape, lambda i, j: (0,) * a.ndim)
    ut = (jnp.arange(tm)[:, None] < jnp.arange(tm)[None, :]).astype(BF16)
    lt = (jnp.arange(ROUTE_LANES)[None, :] < jnp.arange(ROUTE_LANES)[:, None]).astype(BF16)
    return pl.pallas_call(
        _mix_out_kernel,
        grid=(n_tiles, b),
        in_specs=[pl.BlockSpec((tm // CHUNK, CHUNK * S5_WIDTH), lambda i, j: (j * n_tiles + i, 0)),
                  tok(d), tok(d), tok(d)] + _pos_specs(tm, d) + [
                  pl.BlockSpec((None, 4, d), lambda i, j: (j, 0, 0)),
                  whole(wv), whole(wg), whole(wo), whole(ln1), whole(wr), whole(br),
                  whole(ut), whole(lt)],
        out_specs=[tok(d),
                   pl.BlockSpec((xs_rows, d), lambda i, j: (j * n_tiles + i, 0)),
                   tok(ROUTE_LANES),
                   pl.BlockSpec((None, ROUTE_LANES, LANES), lambda i, j: (j * n_tiles + i, 0, 0))],
        out_shape=[jax.ShapeDtypeStruct((b, l, d), F32),
                   jax.ShapeDtypeStruct((b * n_tiles * xs_rows, d), BF16),
                   jax.ShapeDtypeStruct((b, l, ROUTE_LANES), F32),
                   jax.ShapeDtypeStruct((b * n_tiles, ROUTE_LANES, LANES), F32)],
        scratch_shapes=[pltpu.VMEM((S5_WIDTH // LANES, tm, LANES), F32)],
        compiler_params=pltpu.CompilerParams(
            dimension_semantics=("arbitrary", "arbitrary"), vmem_limit_bytes=VMEM_LIMIT),
        name="mix_out",
    )(y, siga, sb, x, r_emb, c_emb, mod, wv, wg, wo, ln1, wr, br, ut, lt)


def _piece_copy(src_hbm, src_row, dst, piece, sem):
    return pltpu.make_async_copy(src_hbm.at[pl.ds(pl.multiple_of(src_row, PIECE), PIECE), :],
                                 dst.at[pl.ds(pl.multiple_of(piece * PIECE, PIECE), PIECE), :], sem)


def _issue_pieces(src_hbm, table_ref, first, n_pieces, dst, sem):
    for p in range(n_pieces):
        _piece_copy(src_hbm, table_ref[first + p], dst, p, sem).start()


def _wait_pieces(src_hbm, dst, sem):
    pltpu.make_async_copy(src_hbm.at[pl.ds(0, dst.shape[0]), :], dst, sem).wait()


def _experts_kernel(be_ref, piece_ref, nused_ref, xs_hbm, wg_ref, wu_ref, wd_ref, ys_ref,
                    xs_buf0, xs_buf1, wg_bf, wu_bf, wd_bf, sem):
    i = pl.program_id(0)
    n_used = nused_ref[0]
    per_block = ROW_BLOCK // PIECE

    @pl.when(i == 0)
    def _():
        _issue_pieces(xs_hbm, piece_ref, 0, per_block, xs_buf0, sem.at[0])

    def block(cur, cur_sem, oth, oth_sem):
        @pl.when((i == 0) | (be_ref[i] != be_ref[jnp.maximum(i - 1, 0)]))
        def _():
            wg_bf[...] = wg_ref[...].astype(BF16)
            wu_bf[...] = wu_ref[...].astype(BF16)
            wd_bf[...] = wd_ref[...].astype(BF16)

        _wait_pieces(xs_hbm, cur, cur_sem)
        nxt = jnp.minimum(i + 1, n_used - 1)
        _issue_pieces(xs_hbm, piece_ref, nxt * per_block, per_block, oth, oth_sem)

        xb = cur[...]
        gate = _dot(xb, wg_bf[...])
        up = _dot(xb, wu_bf[...])
        hid = (gate * _sigmoid(gate) * up).astype(BF16)
        ys_ref[...] = _dot(hid, wd_bf[...]).astype(ys_ref.dtype)

        @pl.when(i == n_used - 1)
        def _():
            _wait_pieces(xs_hbm, oth, oth_sem)

    for s, (cur, oth) in enumerate(((xs_buf0, xs_buf1), (xs_buf1, xs_buf0))):
        @pl.when((i < n_used) & (i % 2 == s))
        def _(s=s, cur=cur, oth=oth):
            block(cur, sem.at[s], oth, sem.at[1 - s])

    @pl.when(i >= n_used)
    def _():
        ys_ref[...] = jnp.zeros(ys_ref.shape, ys_ref.dtype)


def _experts(block_e, piece_src, n_used, xs, wg, wu, wd, n_blocks):
    d = xs.shape[1]
    by_expert = lambda i, be, ps, nu: (0, be[i], 0, 0)
    grid_spec = pltpu.PrefetchScalarGridSpec(
        num_scalar_prefetch=3,
        grid=(n_blocks,),
        in_specs=[pl.BlockSpec(memory_space=pl.ANY),
                  pl.BlockSpec((None, None, d, EXPERT_FF), by_expert),
                  pl.BlockSpec((None, None, d, EXPERT_FF), by_expert),
                  pl.BlockSpec((None, None, EXPERT_FF, d), by_expert)],
        out_specs=pl.BlockSpec((ROW_BLOCK, d), lambda i, be, ps, nu: (i, 0)),
        scratch_shapes=[pltpu.VMEM((ROW_BLOCK, d), BF16), pltpu.VMEM((ROW_BLOCK, d), BF16),
                        pltpu.VMEM((d, EXPERT_FF), BF16), pltpu.VMEM((d, EXPERT_FF), BF16),
                        pltpu.VMEM((EXPERT_FF, d), BF16), pltpu.SemaphoreType.DMA((2,))],
    )
    return pl.pallas_call(
        _experts_kernel,
        grid_spec=grid_spec,
        out_shape=jax.ShapeDtypeStruct((n_blocks * ROW_BLOCK, d), BF16),
        compiler_params=pltpu.CompilerParams(
            dimension_semantics=("arbitrary",), vmem_limit_bytes=VMEM_LIMIT),
        name="experts",
    )(block_e, piece_src, n_used, xs, wg, wu, wd)


def _combine_kernel(piece_ref, ys_hbm, x1_ref, route_ref, mod_ref, ln2_ref, o_ref, buf, sem):
    i = pl.program_id(0)
    n = pl.num_programs(0)
    slot = i % 2
    rows = buf.shape[1]
    per_tile = rows // PIECE

    def issue(tile, s):
        def body(p, _):
            _piece_copy(ys_hbm, piece_ref[tile * per_tile + p], buf.at[s], p, sem.at[s]).start()
            return 0
        lax.fori_loop(0, per_tile, body, 0)

    @pl.when(i == 0)
    def _():
        issue(0, 0)

    _wait_pieces(ys_hbm, buf.at[slot], sem.at[slot])

    @pl.when(i + 1 < n)
    def _():
        issue(i + 1, 1 - slot)

    tm = x1_ref.shape[0]
    lane = lax.broadcasted_iota(jnp.int32, (tm, rows), 1).astype(F32)
    sel = (jnp.where(lane == route_ref[:, 4:5], route_ref[:, 2:3], 0.0)
           + jnp.where(lane == route_ref[:, 5:6], route_ref[:, 3:4], 0.0)).astype(BF16)
    moe = _dot(sel, buf[slot])
    z = ALPHA * x1_ref[...] + mod_ref[0:1, :] * moe
    o_ref[...] = _ln(z) * ln2_ref[0:1, :] + ln2_ref[1:2, :]


def _combine(piece_glob, ys, x1, route, mod, ln2, tm, tiles_per_batch):
    t, d = x1.shape
    grid_spec = pltpu.PrefetchScalarGridSpec(
        num_scalar_prefetch=1,
        grid=(t // tm,),
        in_specs=[pl.BlockSpec(memory_space=pl.ANY),
                  pl.BlockSpec((tm, d), lambda i, pg: (i, 0)),
                  pl.BlockSpec((tm, ROUTE_LANES), lambda i, pg: (i, 0)),
                  pl.BlockSpec((None, 8, d), lambda i, pg: (i // tiles_per_batch, 0, 0)),
                  pl.BlockSpec((2, d), lambda i, pg: (0, 0))],
        out_specs=pl.BlockSpec((tm, d), lambda i, pg: (i, 0)),
        scratch_shapes=[pltpu.VMEM((2, _local_rows(tm), d), BF16), pltpu.SemaphoreType.DMA((2,))],
    )
    return pl.pallas_call(
        _combine_kernel,
        grid_spec=grid_spec,
        out_shape=jax.ShapeDtypeStruct((t, d), F32),
        compiler_params=pltpu.CompilerParams(
            dimension_semantics=("arbitrary",), vmem_limit_bytes=VMEM_LIMIT),
        name="combine",
    )(piece_glob, ys, x1, route, mod, ln2)


def _sincos_2d(rows, cols, dim):
    q = dim // 4
    omega = 1.0 / (POS_BASE ** (jnp.arange(q, dtype=F32) / q))
    r = jnp.arange(rows, dtype=F32)[:, None] * omega
    cl = jnp.arange(cols, dtype=F32)[:, None] * omega
    r_emb = jnp.concatenate([jnp.sin(r), jnp.cos(r)], -1)
    c_emb = jnp.concatenate([jnp.sin(cl), jnp.cos(cl)], -1)
    return r_emb, c_emb


def _routing_tables(run_pieces, xs_rows, n_blocks):
    i32 = jnp.int32
    n_tiles = run_pieces.shape[0]
    ppb = ROW_BLOCK // PIECE
    loc_start = jnp.cumsum(run_pieces, axis=1) - run_pieces
    seg_tot = jnp.sum(run_pieces, axis=0)
    seg_pad = (seg_tot + ppb - 1) // ppb * ppb
    seg_end = jnp.cumsum(seg_pad)
    seg_start = seg_end - seg_pad
    run_t = run_pieces.T
    glob_start = seg_start[:, None] + jnp.cumsum(run_t, axis=1) - run_t
    n_used = (seg_end[-1] // ppb).astype(i32)
    blk = jnp.minimum(jnp.arange(n_blocks, dtype=i32), n_used - 1)
    block_e = jnp.minimum(jnp.sum((seg_end[None, :] <= (blk * ppb)[:, None]).astype(i32), axis=1),
                          N_EXPERTS - 1).astype(i32)
    lpt = xs_rows // PIECE
    starts = glob_start.reshape(-1)
    lens = run_t.reshape(-1)
    src0 = (jnp.arange(n_tiles, dtype=i32)[None, :] * lpt + loc_start.T).reshape(-1)
    p = jnp.arange(n_blocks * ppb, dtype=i32)
    within = p[:, None] - starts[None, :]
    hit = (within >= 0) & (within < lens[None, :])
    piece_src = jnp.sum(jnp.where(hit, (src0[None, :] + within) * PIECE, 0), axis=1).astype(i32)
    s = jnp.arange(lpt, dtype=i32)
    loc_within = s[None, :, None] - loc_start[:, None, :]
    hit = (loc_within >= 0) & (loc_within < run_pieces[:, None, :])
    piece_glob = jnp.sum(jnp.where(hit, (glob_start.T[:, None, :] + loc_within) * PIECE, 0), axis=2)
    return block_e, piece_src, piece_glob.astype(i32).reshape(-1), n_used.reshape(1)


def kernel(x, c, ctx, c_ctx, w_ada, b_ada, w_in, s5_log_dt_f, s5_a_re_f, s5_a_im_f, s5_b_re_f, s5_b_im_f, s5_c_re_f, s5_c_im_f, s5_log_dt_b, s5_a_re_b, s5_a_im_b, s5_b_re_b, s5_b_im_b, s5_c_re_b, s5_c_im_b, s5_d, s5_w_glu_val, s5_w_glu_gate, conv_w, conv_w_out, w_o, ln1_g, ln1_b, router_w_group, router_b_group, router_w_expert, router_b_expert, exp_w_gate, exp_w_up, exp_w_down, ln2_g, ln2_b):
    b, l, d = x.shape
    lc = ctx.shape[1]
    assert d == D_MODEL and b < SUBLANES and w_ada.shape[0] == DEPTH
    assert l % (SUBLANES * CHUNK) == 0 and lc % (SUBLANES * CHUNK) == 0 and l % GRID_W == 0
    t = b * l
    tm = min(512, l)
    tmc = min(512, lc)

    cc = jnp.concatenate([c, c_ctx[None, :], jnp.zeros((8 - b - 1, d), F32)], 0)
    mods = _mods(cc, w_ada[0], b_ada[0])
    sh1, sc1, g1, sh2, sc2, g2 = jnp.split(mods, 6, axis=-1)
    mod_a = jnp.stack([sh1[:b], 1.0 + sc1[:b]], 1)
    mod_ctx = jnp.broadcast_to(jnp.stack([sh1[b], 1.0 + sc1[b]], 0)[None], (b, 2, d))
    mod_c = jnp.stack([g1[:b], sh2[:b], 1.0 + sc2[:b], jnp.zeros((b, d), F32)], 1)
    mod_f = jnp.concatenate([g2[:b, None, :], jnp.zeros((b, 7, d), F32)], 1)

    w_in_bf = w_in[0].astype(BF16)
    both = lambda fwd, bwd: jnp.concatenate([fwd, bwd], 0)
    s5_tab = _s5_tables(both(s5_log_dt_f, s5_log_dt_b), both(s5_a_re_f, s5_a_re_b), both(s5_a_im_f, s5_a_im_b),
                        both(s5_b_re_f, s5_b_re_b), both(s5_b_im_f, s5_b_im_b),
                        both(s5_c_re_f, s5_c_re_b), both(s5_c_im_f, s5_c_im_b))
    mi, ws, wo_s5, tab = _s5_operators(s5_tab, s5_d[0])

    (uc_ctx,) = _in_proj(ctx, jnp.zeros((lc // GRID_W, d // 2), F32), jnp.zeros((GRID_W, d // 2), F32),
                         mod_ctx, w_in_bf, None, None, tmc, False)
    zero_state = jnp.zeros((N_PAIRS, 4, SUBLANES, LANES), F32)
    _, s0 = _s5_scan(uc_ctx, mi, ws, wo_s5, tab, zero_state, b)

    r_emb, c_emb = _sincos_2d(l // GRID_W, GRID_W, d)
    uc, siga, sb = _in_proj(x, r_emb, c_emb, mod_a, w_in_bf, conv_w[0], conv_w_out[0].astype(BF16), tm, True)
    y, _ = _s5_scan(uc, mi, ws, wo_s5, tab, s0, b)

    wr = jnp.concatenate([router_w_group[0], router_w_expert[0],
                          jnp.zeros((d, ROUTE_LANES - N_EXPERT_GROUPS - N_EXPERTS), F32)], 1).T.astype(BF16)
    br = jnp.concatenate([router_b_group[0], router_b_expert[0],
                          jnp.zeros((ROUTE_LANES - N_EXPERT_GROUPS - N_EXPERTS,), F32)])[:, None]
    ln1 = jnp.stack([ln1_g[0], ln1_b[0]], 0)
    x1, xs, route, run_len = _mix_out(y, siga, sb, x, r_emb, c_emb, mod_c,
                                      s5_w_glu_val[0].astype(BF16), s5_w_glu_gate[0].astype(BF16),
                                      w_o[0].astype(BF16), ln1, wr, br, tm)

    x1 = x1.reshape(t, d)
    route = route.reshape(t, ROUTE_LANES)
    n_tiles = t // tm
    xs_rows = _local_rows(tm)
    run_pieces = run_len[:, :N_EXPERTS, 0].astype(jnp.int32)
    max_rows = t * TOP_K + n_tiles * N_EXPERTS * (PIECE - 1) + N_EXPERTS * (ROW_BLOCK - 1)
    n_blocks = -(-max_rows // ROW_BLOCK)
    block_e, piece_src, piece_glob, n_used = _routing_tables(run_pieces, xs_rows, n_blocks)
    ys = _experts(block_e, piece_src, n_used, xs, exp_w_gate, exp_w_up, exp_w_down, n_blocks)
    ln2 = jnp.stack([ln2_g[0], ln2_b[0]], 0)
    out = _combine(piece_glob, ys, x1, route, mod_f, ln2, tm, l // tm)
    return out.reshape(b, l, d)
```

```python
import functools
import math

import jax
import jax.numpy as jnp
from jax import lax
from jax.experimental import pallas as pl
from jax.experimental.pallas import tpu as pltpu

F32 = jnp.float32
BF16 = jnp.bfloat16
HI = lax.Precision.HIGHEST

D_MODEL = 1024
GRID_W = 64
S5_WIDTH = 512
S5_GROUP_CH = 16
S5_GROUPS = S5_WIDTH // S5_GROUP_CH
S5_STATE = 64
CONV_WIDTH = 512
N_EXPERT_GROUPS = 4
EXPERTS_PER_GROUP = 8
N_EXPERTS = N_EXPERT_GROUPS * EXPERTS_PER_GROUP
EXPERT_FF = 512
TOP_K = 2
DEPTH = 1
ALPHA = (2.0 * DEPTH) ** 0.25
LN_EPS = 1e-6
POS_BASE = 10000.0

LANES = 128
SUBLANES = 8
CHUNK = 16
GROUP_W = CHUNK * S5_GROUP_CH
PAIR_W = 2 * GROUP_W
N_PAIRS = S5_GROUPS // 2
TOK_PER_VREG = LANES // S5_GROUP_CH
TAB_ROWS = 24
ROUTE_LANES = 128
ROW_BLOCK = 512
PIECE = 16
VMEM_LIMIT = 56 * 1024 * 1024


def _ln(x):
    mu = jnp.mean(x, axis=-1, keepdims=True)
    xc = x - mu
    var = jnp.mean(xc * xc, axis=-1, keepdims=True)
    return xc * lax.rsqrt(var + LN_EPS)


def _sigmoid(x):
    return 0.5 * (jnp.tanh(0.5 * x) + 1.0)


def _dot(a, b):
    return jnp.dot(a, b, preferred_element_type=F32)


def _mods_kernel(c_ref, w_ref, b_ref, o_ref):
    c = c_ref[...]
    a = c * _sigmoid(c)
    o_ref[...] = jnp.dot(a, w_ref[...], precision=HI, preferred_element_type=F32) + b_ref[...]


def _mods(cc, w_ada, b_ada):
    n = w_ada.shape[1]
    nb = 1536
    return pl.pallas_call(
        _mods_kernel,
        grid=(n // nb,),
        in_specs=[pl.BlockSpec((8, D_MODEL), lambda i: (0, 0)),
                  pl.BlockSpec((D_MODEL, nb), lambda i: (0, i)),
                  pl.BlockSpec((1, nb), lambda i: (0, i))],
        out_specs=pl.BlockSpec((8, nb), lambda i: (0, i)),
        out_shape=jax.ShapeDtypeStruct((8, n), F32),
        compiler_params=pltpu.CompilerParams(vmem_limit_bytes=VMEM_LIMIT),
        name="mods",
    )(cc, w_ada, b_ada.reshape(1, n))


def _slot_masks(rows):
    slot = lax.broadcasted_iota(jnp.int32, (rows, LANES), 1) // S5_GROUP_CH
    return [slot == s for s in range(TOK_PER_VREG)]


def _to_chunk_tile(u_scr, uc_ref):
    nch = uc_ref.shape[0]
    masks = _slot_masks(nch)
    for qh in range(CHUNK // TOK_PER_VREG):
        for v in range(S5_WIDTH // LANES):
            src = [u_scr[v, pl.ds(qh * TOK_PER_VREG + s, nch, stride=CHUNK), :] for s in range(TOK_PER_VREG)]
            for i in range(TOK_PER_VREG):
                acc = None
                for s in range(TOK_PER_VREG):
                    shift = ((s - i) * S5_GROUP_CH) % LANES
                    piece = pltpu.roll(src[s], shift, 1) if shift else src[s]
                    acc = piece if acc is None else jnp.where(masks[s], piece, acc)
                lo = (v * TOK_PER_VREG + i) * GROUP_W + qh * LANES
                uc_ref[:, lo:lo + LANES] = acc.astype(uc_ref.dtype)


def _from_chunk_tile(yc_ref, y_scr, c0=0, nch=None):
    nch = yc_ref.shape[0] if nch is None else nch
    masks = _slot_masks(nch)
    for qh in range(CHUNK // TOK_PER_VREG):
        for v in range(S5_WIDTH // LANES):
            src = []
            for i in range(TOK_PER_VREG):
                lo = (v * TOK_PER_VREG + i) * GROUP_W + qh * LANES
                src.append(yc_ref[c0:c0 + nch, lo:lo + LANES].astype(F32))
            for s in range(TOK_PER_VREG):
                acc = None
                for i in range(TOK_PER_VREG):
                    shift = ((i - s) * S5_GROUP_CH) % LANES
                    piece = pltpu.roll(src[i], shift, 1) if shift else src[i]
                    acc = piece if acc is None else jnp.where(masks[i], piece, acc)
                y_scr[v, pl.ds(c0 * CHUNK + qh * TOK_PER_VREG + s, nch, stride=CHUNK), :] = acc


def _with_positions(x_ref, remb_ref, cemb_ref, r0=0, rows=None):
    rows = x_ref.shape[0] if rows is None else rows
    c = cemb_ref[...]
    slabs = []
    for j in range(r0 // GRID_W, (r0 + rows) // GRID_W):
        r = jnp.broadcast_to(remb_ref[j:j + 1, :], c.shape)
        slabs.append(x_ref[j * GRID_W:(j + 1) * GRID_W, :] + jnp.concatenate([r, c], axis=-1))
    return jnp.concatenate(slabs, axis=0)


def _in_proj_kernel(x_ref, remb_ref, cemb_ref, mod_ref, w_ref, *rest, full):
    if full:
        cw_ref, cwo_ref, uc_ref, siga_ref, sb_ref, u_scr = rest
    else:
        uc_ref, u_scr = rest
    xp = _with_positions(x_ref, remb_ref, cemb_ref)
    h = (_ln(xp) * mod_ref[1:2, :] + mod_ref[0:1, :]).astype(BF16)
    o1, o2, o3, o4, o5 = 512, 1024, 1536, 2048, 3072
    u = _dot(h, w_ref[:, 0:o1])
    for v in range(S5_WIDTH // LANES):
        u_scr[v] = u[:, v * LANES:(v + 1) * LANES]
    _to_chunk_tile(u_scr, uc_ref)
    if not full:
        return
    z_b = _dot(h, w_ref[:, o1:o2])
    gate_c = _dot(h, w_ref[:, o3:o4])
    p = gate_c * z_b
    tm = p.shape[0]
    col = lax.broadcasted_iota(jnp.int32, (tm, 1), 0) % GRID_W
    prev = jnp.where(col == 0, 0.0, pltpu.roll(p, 1, 0))
    nxt = jnp.where(col == GRID_W - 1, 0.0, pltpu.roll(p, tm - 1, 0))
    v = cw_ref[0:1, :] * prev + cw_ref[1:2, :] * p + cw_ref[2:3, :] * nxt
    gate_b = _dot(h, w_ref[:, o2:o3])
    out_b = _dot((gate_b * v).astype(BF16), cwo_ref[...])
    merge_b = _dot(h, w_ref[:, o5:])
    sb_ref[...] = (_sigmoid(merge_b) * out_b).astype(sb_ref.dtype)
    merge_a = _dot(h, w_ref[:, o4:o5])
    siga_ref[...] = _sigmoid(merge_a).astype(siga_ref.dtype)


def _pos_specs(tm, d):
    return [pl.BlockSpec((tm // GRID_W, d // 2), lambda i, j: (i, 0)),
            pl.BlockSpec((GRID_W, d // 2), lambda i, j: (0, 0))]


def _in_proj(x, r_emb, c_emb, mod, w_in_bf, conv_w, conv_w_out_bf, tm, full):
    b, l, d = x.shape
    n_tiles = l // tm
    grid = (n_tiles, b)
    tok = lambda w: pl.BlockSpec((None, tm, w), lambda i, j: (j, i, 0))
    chunk_spec = pl.BlockSpec((tm // CHUNK, CHUNK * S5_WIDTH), lambda i, j: (j * n_tiles + i, 0))
    chunk_shape = jax.ShapeDtypeStruct((b * l // CHUNK, CHUNK * S5_WIDTH), BF16)
    in_specs = [tok(d)] + _pos_specs(tm, d) + [pl.BlockSpec((None, 2, d), lambda i, j: (j, 0, 0))]
    args = [x, r_emb, c_emb, mod]
    if full:
        in_specs += [pl.BlockSpec(w_in_bf.shape, lambda i, j: (0, 0)),
                     pl.BlockSpec(conv_w.shape, lambda i, j: (0, 0)),
                     pl.BlockSpec(conv_w_out_bf.shape, lambda i, j: (0, 0))]
        args += [w_in_bf, conv_w, conv_w_out_bf]
        out_specs = [chunk_spec, tok(d), tok(d)]
        out_shape = [chunk_shape,
                     jax.ShapeDtypeStruct((b, l, d), BF16),
                     jax.ShapeDtypeStruct((b, l, d), BF16)]
    else:
        in_specs += [pl.BlockSpec((d, S5_WIDTH), lambda i, j: (0, 0))]
        args += [w_in_bf]
        out_specs = [chunk_spec]
        out_shape = [chunk_shape]
    return pl.pallas_call(
        functools.partial(_in_proj_kernel, full=full),
        grid=grid, in_specs=in_specs, out_specs=out_specs, out_shape=out_shape,
        scratch_shapes=[pltpu.VMEM((S5_WIDTH // LANES, tm, LANES), F32)],
        compiler_params=pltpu.CompilerParams(
            dimension_semantics=("arbitrary", "arbitrary"), vmem_limit_bytes=VMEM_LIMIT),
        name="in_proj" if full else "in_proj_ctx",
    )(*args)


def _s5_tables(log_dt, a_re, a_im, b_re, b_im, c_re, c_im):
    f32 = F32
    dt = jnp.exp(log_dt.astype(f32))[..., None]
    a_re = a_re.astype(f32)
    a_im = a_im.astype(f32)
    mag = jnp.exp(dt * a_re)
    ab_re = mag * jnp.cos(dt * a_im)
    ab_im = mag * jnp.sin(dt * a_im)
    den = a_re * a_re + a_im * a_im
    x_re = ab_re - 1.0
    f_re = (x_re * a_re + ab_im * a_im) / den
    f_im = (ab_im * a_re - x_re * a_im) / den
    b_re = b_re.astype(f32)
    b_im = b_im.astype(f32)
    bb_re = f_re[..., None] * b_re - f_im[..., None] * b_im
    bb_im = f_re[..., None] * b_im + f_im[..., None] * b_re
    k = jnp.arange(CHUNK + 1, dtype=f32)[None, :, None, None]
    pmag = jnp.exp(k * (dt * a_re)[:, None])
    p_re = pmag * jnp.cos(k * (dt * a_im)[:, None])
    p_im = pmag * jnp.sin(k * (dt * a_im)[:, None])
    pb_re = p_re[..., None] * bb_re[:, None] - p_im[..., None] * bb_im[:, None]
    pb_im = p_re[..., None] * bb_im[:, None] + p_im[..., None] * bb_re[:, None]
    c_re = c_re.astype(f32)[:, None]
    c_im = c_im.astype(f32)[:, None]
    cp_re = c_re * p_re[:, :, :, None, :] - c_im * p_im[:, :, :, None, :]
    cp_im = -(c_re * p_im[:, :, :, None, :] + c_im * p_re[:, :, :, None, :])
    return dict(p_re=p_re, p_im=p_im, pb_re=pb_re, pb_im=pb_im, cp_re=cp_re, cp_im=cp_im,
                bb_re=bb_re, bb_im=bb_im)


def _lag_kernels(t):
    g, n, c = S5_GROUPS, S5_STATE, S5_GROUP_CH
    k = CHUNK + 1
    lhs = jnp.concatenate([t['cp_re'], t['cp_im']], -1)
    lhs = lhs.transpose(0, 2, 1, 3, 4).reshape(2 * g, k * c, 2 * n)
    rhs = jnp.concatenate([t['bb_re'], t['bb_im']], -2).reshape(2 * g, 2 * n, c)
    out = jnp.einsum('bmn,bnc->bmc', lhs, rhs, precision=HI)
    out = out.reshape(2, g, k, c, c).transpose(0, 2, 1, 4, 3)
    return out[0], out[1]


def _s5_operators(t, s5_d):
    q = CHUNK
    g, n, c = S5_GROUPS, S5_STATE, S5_GROUP_CH
    kern_f, kern_b = _lag_kernels(t)
    k0 = kern_f[0] + kern_b[0] + s5_d.astype(F32)[:, :, None] * jnp.eye(c, dtype=F32)[None]
    kc = jnp.concatenate([kern_b[1:q][::-1], k0[None], kern_f[1:q]], 0)
    kct = kc.transpose(1, 2, 0, 3).reshape(g, c, (2 * q - 1) * c)
    m_intra = jnp.stack([kct[:, :, (q - 1 - i) * c:(2 * q - 1 - i) * c] for i in range(q)], 1)
    m_intra = m_intra.reshape(g, q * c, q * c)
    w_st = jnp.stack([t['pb_re'][0, :q][::-1], t['pb_im'][0, :q][::-1],
                      t['pb_re'][1, :q], t['pb_im'][1, :q]], 0)
    w_st = w_st.transpose(2, 1, 4, 0, 3).reshape(g, q * c, 4, n)
    w_out = jnp.stack([t['cp_re'][0, 1:], t['cp_im'][0, 1:],
                       t['cp_re'][1, 1:][::-1], t['cp_im'][1, 1:][::-1]], 0)
    w_out = w_out.transpose(2, 0, 4, 1, 3).reshape(g, 4, n, q * c)
    np_ = N_PAIRS
    w_st = w_st.astype(BF16).reshape(np_, 2, q * c, 4, n)
    ws_pair = jnp.concatenate([jnp.pad(w_st[:, 0], ((0, 0), (0, 0), (0, 0), (0, n))),
                               jnp.pad(w_st[:, 1], ((0, 0), (0, 0), (0, 0), (n, 0)))], 1)
    ws_pair = ws_pair.reshape(np_, PAIR_W, 4 * 2 * n)
    w_out = w_out.astype(BF16).reshape(np_, 2, 4, n, q * c)
    wo_pair = jnp.stack([jnp.pad(w_out[:, 0], ((0, 0), (0, 0), (0, 0), (0, q * c))),
                         jnp.pad(w_out[:, 1], ((0, 0), (0, 0), (0, 0), (q * c, 0)))], 2)
    wo_pair = wo_pair.reshape(np_, 4 * 2 * n, PAIR_W)
    tab = _chunk_power_table(t).reshape(2 * TAB_ROWS, np_, 2 * n).transpose(1, 0, 2)
    return m_intra.astype(BF16), ws_pair, wo_pair, tab


def _chunk_power_table(t):
    def cmul(x, y):
        return x[0] * y[0] - x[1] * y[1], x[0] * y[1] + x[1] * y[0]
    p1 = (t['p_re'][:, CHUNK], t['p_im'][:, CHUNK])
    p2 = cmul(p1, p1)
    p4 = cmul(p2, p2)
    p8 = cmul(p4, p4)
    pr = [(jnp.ones_like(p1[0]), jnp.zeros_like(p1[0]))]
    for _ in range(SUBLANES - 1):
        pr.append(cmul(pr[-1], p1))
    pr_re = jnp.stack([p[0] for p in pr], 0)
    pr_im = jnp.stack([p[1] for p in pr], 0)
    pw = jnp.stack([p1[0], p1[1], p2[0], p2[1], p4[0], p4[1], p8[0], p8[1]], 0)
    return jnp.concatenate([pr_re[:, 0], pr_im[:, 0], pw[:, 0],
                            pr_re[::-1, 1], pr_im[::-1, 1], pw[:, 1]], 0)


def _s5_scan_kernel(uc_ref, mi_ref, ws_ref, wo_ref, tab_ref, s0_ref, y_ref, fin_ref, s_scr, in_scr, *, batch):
    rows = uc_ref.shape[0]
    chunks = rows // batch
    n_tiles = chunks // SUBLANES
    u = uc_ref[...]
    s_scr[...] = _dot(u, ws_ref[...])
    row = lax.broadcasted_iota(jnp.int32, (SUBLANES, LANES), 0)

    def tile_scan(r0, backward, c_re, c_im):
        base = TAB_ROWS if backward else 0
        col = 2 * LANES if backward else 0
        rs = pl.ds(r0, SUBLANES)

        def shift(z, k):
            if backward:
                return jnp.where(row < SUBLANES - k, pltpu.roll(z, SUBLANES - k, 0), 0.0)
            return jnp.where(row >= k, pltpu.roll(z, k, 0), 0.0)

        z_re = s_scr[rs, col:col + LANES]
        z_im = s_scr[rs, col + LANES:col + 2 * LANES]
        for k, t in ((1, 16), (2, 18), (4, 20)):
            a_re = tab_ref[base + t:base + t + 1, :]
            a_im = tab_ref[base + t + 1:base + t + 2, :]
            sh_re = shift(z_re, k)
            sh_im = shift(z_im, k)
            z_re, z_im = z_re + (a_re * sh_re - a_im * sh_im), z_im + (a_re * sh_im + a_im * sh_re)
        pr_re = tab_ref[base:base + SUBLANES, :]
        pr_im = tab_ref[base + SUBLANES:base + 2 * SUBLANES, :]
        in_scr[rs, col:col + LANES] = pr_re * c_re - pr_im * c_im + shift(z_re, 1)
        in_scr[rs, col + LANES:col + 2 * LANES] = pr_re * c_im + pr_im * c_re + shift(z_im, 1)
        last = 0 if backward else SUBLANES - 1
        l_re = jnp.broadcast_to(z_re[last:last + 1, :], (SUBLANES, LANES))
        l_im = jnp.broadcast_to(z_im[last:last + 1, :], (SUBLANES, LANES))
        p8_re = tab_ref[base + 22:base + 23, :]
        p8_im = tab_ref[base + 23:base + 24, :]
        return p8_re * c_re - p8_im * c_im + l_re, p8_re * c_im + p8_im * c_re + l_im

    def body(m, carry):
        out = []
        for b in range(batch):
            cf_re, cf_im, cb_re, cb_im = carry[4 * b:4 * b + 4]
            rf = pl.multiple_of(b * chunks + m * SUBLANES, SUBLANES)
            rb = pl.multiple_of(b * chunks + (n_tiles - 1 - m) * SUBLANES, SUBLANES)
            out += list(tile_scan(rf, False, cf_re, cf_im))
            out += list(tile_scan(rb, True, cb_re, cb_im))
        return tuple(out)

    init = tuple(jnp.broadcast_to(s0_ref[t, b:b + 1, :], (SUBLANES, LANES))
                 for b in range(batch) for t in range(4))
    fin = lax.fori_loop(0, n_tiles, body, init, unroll=min(4, n_tiles))
    fin_ref[...] = jnp.zeros(fin_ref.shape, F32)
    for b in range(batch):
        for t in range(4):
            fin_ref[t, b:b + 1, :] = fin[4 * b + t][0:1, :]
    y_intra = jnp.concatenate([_dot(u[:, gl * GROUP_W:(gl + 1) * GROUP_W], mi_ref[gl]) for gl in range(2)], axis=-1)
    y = y_intra + _dot(in_scr[...].astype(BF16), wo_ref[...])
    y_ref[...] = y.astype(y_ref.dtype)


def _s5_scan(uc, mi, ws, wo, tab, s0, batch):
    rows = uc.shape[0]
    pair = lambda *shape: pl.BlockSpec((None,) + shape, lambda p: (p,) + (0,) * len(shape))
    return pl.pallas_call(
        functools.partial(_s5_scan_kernel, batch=batch),
        grid=(N_PAIRS,),
        in_specs=[pl.BlockSpec((rows, PAIR_W), lambda p: (0, p)),
                  pl.BlockSpec((2, GROUP_W, GROUP_W), lambda p: (p, 0, 0)),
                  pair(PAIR_W, PAIR_W), pair(PAIR_W, PAIR_W),
                  pair(2 * TAB_ROWS, LANES), pair(4, SUBLANES, LANES)],
        out_specs=[pl.BlockSpec((rows, PAIR_W), lambda p: (0, p)), pair(4, SUBLANES, LANES)],
        out_shape=[jax.ShapeDtypeStruct((rows, N_PAIRS * PAIR_W), BF16),
                   jax.ShapeDtypeStruct((N_PAIRS, 4, SUBLANES, LANES), F32)],
        scratch_shapes=[pltpu.VMEM((rows, PAIR_W), F32), pltpu.VMEM((rows, PAIR_W), F32)],
        compiler_params=pltpu.CompilerParams(
            dimension_semantics=("arbitrary",), vmem_limit_bytes=VMEM_LIMIT),
        name="s5_scan",
    )(uc, mi, ws, wo, tab, s0)


def _mix_out_kernel(y_ref, siga_ref, sb_ref, x_ref, remb_ref, cemb_ref, mod_ref, wv_ref, wg_ref, wo_ref,
                    ln1_ref, wr_ref, br_ref, ut_ref, lt_ref,
                    x1_ref, xs_ref, route_ref, len_ref, y_scr):
    _from_chunk_tile(y_ref, y_scr)
    y = jnp.concatenate([y_scr[v] for v in range(S5_WIDTH // LANES)], axis=-1)
    ya = (0.5 * y * (1.0 + jnp.tanh(math.sqrt(2.0 / math.pi) * (y + 0.044715 * (y * y * y))))).astype(BF16)
    out_a = _dot(ya, wv_ref[...]) * _sigmoid(_dot(ya, wg_ref[...]))
    merged = siga_ref[...].astype(F32) * out_a + sb_ref[...].astype(F32)
    mix = _dot(merged.astype(BF16), wo_ref[...])
    xp = _with_positions(x_ref, remb_ref, cemb_ref)
    x1 = _ln(ALPHA * xp + mod_ref[0:1, :] * mix) * ln1_ref[0:1, :] + ln1_ref[1:2, :]
    x1_ref[...] = x1
    h_hi = (_ln(x1) * mod_ref[2:3, :] + mod_ref[1:2, :]).astype(BF16)
    _route_and_sort(h_hi, wr_ref, br_ref, ut_ref, lt_ref, xs_ref, route_ref, len_ref)


def _route_and_sort(h_hi, wr_ref, br_ref, ut_ref, lt_ref, xs_ref, route_ref, len_ref):
    tm = h_hi.shape[0]
    f32 = F32
    nt = (((1,), (1,)), ((), ()))
    lg = lax.dot_general(wr_ref[...], h_hi, nt, preferred_element_type=f32) + br_ref[...]
    rowi = lax.broadcasted_iota(jnp.int32, (ROUTE_LANES, tm), 0).astype(f32)
    neg = jnp.float32(-jnp.inf)
    big = jnp.float32(ROUTE_LANES)
    gl = jnp.where(rowi < N_EXPERT_GROUPS, lg, neg)
    gmax = jnp.max(gl, axis=0, keepdims=True)
    g_idx = jnp.min(jnp.where(gl == gmax, rowi, big), axis=0, keepdims=True)
    p_group = 1.0 / jnp.sum(jnp.exp(gl - gmax), axis=0, keepdims=True)
    e_lo = N_EXPERT_GROUPS + g_idx * EXPERTS_PER_GROUP
    el = jnp.where((rowi >= e_lo) & (rowi < e_lo + EXPERTS_PER_GROUP), lg, neg)
    m1 = jnp.max(el, axis=0, keepdims=True)
    i1 = jnp.min(jnp.where(el == m1, rowi, big), axis=0, keepdims=True)
    el2 = jnp.where(rowi == i1, neg, el)
    m2 = jnp.max(el2, axis=0, keepdims=True)
    i2 = jnp.min(jnp.where(el2 == m2, rowi, big), axis=0, keepdims=True)
    r = jnp.exp(m2 - m1)
    w1 = p_group / (1.0 + r)
    w2 = p_group * r / (1.0 + r)
    e1 = i1 - N_EXPERT_GROUPS
    e2 = i2 - N_EXPERT_GROUPS
    a12 = jnp.where(rowi == e1, 1.0, 0.0) + jnp.where(rowi == e2 + N_EXPERTS, 1.0, 0.0)
    rank = _dot(a12.astype(BF16), ut_ref[...])
    cnt = jnp.broadcast_to(jnp.sum(a12, axis=1, keepdims=True), (ROUTE_LANES, LANES))
    row = lax.broadcasted_iota(jnp.int32, (ROUTE_LANES, LANES), 0)
    tot = cnt + pltpu.roll(cnt, ROUTE_LANES - N_EXPERTS, 0)
    run = jnp.where(row < N_EXPERTS, jnp.floor((tot + (PIECE - 1)) * (1.0 / PIECE)), 0.0)
    off = PIECE * _dot(lt_ref[...], run.astype(BF16))
    base = jnp.where(row < N_EXPERTS, off, pltpu.roll(off + cnt, N_EXPERTS, 0))
    posmat = a12 * (rank + base[:, 0:1])
    pos1 = jnp.sum(posmat[0:N_EXPERTS, :], axis=0, keepdims=True)
    pos2 = jnp.sum(posmat[N_EXPERTS:2 * N_EXPERTS, :], axis=0, keepdims=True)
    ri = lax.broadcasted_iota(jnp.int32, (xs_ref.shape[0], tm), 0).astype(f32)
    perm = jnp.where((ri == pos1) | (ri == pos2), 1.0, 0.0).astype(BF16)
    xs_ref[...] = _dot(perm, h_hi).astype(xs_ref.dtype)
    rec = jnp.where(rowi == 0, e1, jnp.where(rowi == 1, e2, jnp.where(rowi == 2, w1, jnp.where(
        rowi == 3, w2, jnp.where(rowi == 4, pos1, jnp.where(rowi == 5, pos2, 0.0))))))
    route_ref[...] = rec.T
    len_ref[...] = run


def _local_rows(tm):
    return TOP_K * tm + N_EXPERTS * PIECE


def _mix_out(y, siga, sb, x, r_emb, c_emb, mod, wv, wg, wo, ln1, wr, br, tm):
    b, l, d = x.shape
    n_tiles = l // tm
    xs_rows = _local_rows(tm)
    tok = lambda w: pl.BlockSpec((None, tm, w), lambda i, j: (j, i, 0))
    whole = lambda a: pl.BlockSpec(a.shape, lambda i, j: (0,) * a.ndim)
    ut = (jnp.arange(tm)[:, None] < jnp.arange(tm)[None, :]).astype(BF16)
    lt = (jnp.arange(ROUTE_LANES)[None, :] < jnp.arange(ROUTE_LANES)[:, None]).astype(BF16)
    return pl.pallas_call(
        _mix_out_kernel,
        grid=(n_tiles, b),
        in_specs=[pl.BlockSpec((tm // CHUNK, CHUNK * S5_WIDTH), lambda i, j: (j * n_tiles + i, 0)),
                  tok(d), tok(d), tok(d)] + _pos_specs(tm, d) + [
                  pl.BlockSpec((None, 4, d), lambda i, j: (j, 0, 0)),
                  whole(wv), whole(wg), whole(wo), whole(ln1), whole(wr), whole(br),
                  whole(ut), whole(lt)],
        out_specs=[tok(d),
                   pl.BlockSpec((xs_rows, d), lambda i, j: (j * n_tiles + i, 0)),
                   tok(ROUTE_LANES),
                   pl.BlockSpec((None, ROUTE_LANES, LANES), lambda i, j: (j * n_tiles + i, 0, 0))],
        out_shape=[jax.ShapeDtypeStruct((b, l, d), F32),
                   jax.ShapeDtypeStruct((b * n_tiles * xs_rows, d), BF16),
                   jax.ShapeDtypeStruct((b, l, ROUTE_LANES), F32),
                   jax.ShapeDtypeStruct((b * n_tiles, ROUTE_LANES, LANES), F32)],
        scratch_shapes=[pltpu.VMEM((S5_WIDTH // LANES, tm, LANES), F32)],
        compiler_params=pltpu.CompilerParams(
            dimension_semantics=("arbitrary", "arbitrary"), vmem_limit_bytes=VMEM_LIMIT),
        name="mix_out",
    )(y, siga, sb, x, r_emb, c_emb, mod, wv, wg, wo, ln1, wr, br, ut, lt)


def _piece_copy(src_hbm, src_row, dst, piece, sem):
    return pltpu.make_async_copy(src_hbm.at[pl.ds(pl.multiple_of(src_row, PIECE), PIECE), :],
                                 dst.at[pl.ds(pl.multiple_of(piece * PIECE, PIECE), PIECE), :], sem)


def _issue_pieces(src_hbm, table_ref, first, n_pieces, dst, sem):
    for p in range(n_pieces):
        _piece_copy(src_hbm, table_ref[first + p], dst, p, sem).start()


def _wait_pieces(src_hbm, dst, sem):
    pltpu.make_async_copy(src_hbm.at[pl.ds(0, dst.shape[0]), :], dst, sem).wait()


def _experts_kernel(be_ref, piece_ref, nused_ref, xs_hbm, wg_ref, wu_ref, wd_ref, ys_ref,
                    xs_buf0, xs_buf1, wg_bf, wu_bf, wd_bf, sem):
    i = pl.program_id(0)
    n_used = nused_ref[0]
    per_block = ROW_BLOCK // PIECE

    @pl.when(i == 0)
    def _():
        _issue_pieces(xs_hbm, piece_ref, 0, per_block, xs_buf0, sem.at[0])

    def block(cur, cur_sem, oth, oth_sem):
        @pl.when((i == 0) | (be_ref[i] != be_ref[jnp.maximum(i - 1, 0)]))
        def _():
            wg_bf[...] = wg_ref[...].astype(BF16)
            wu_bf[...] = wu_ref[...].astype(BF16)
            wd_bf[...] = wd_ref[...].astype(BF16)

        _wait_pieces(xs_hbm, cur, cur_sem)
        nxt = jnp.minimum(i + 1, n_used - 1)
        _issue_pieces(xs_hbm, piece_ref, nxt * per_block, per_block, oth, oth_sem)

        xb = cur[...]
        gate = _dot(xb, wg_bf[...])
        up = _dot(xb, wu_bf[...])
        hid = (gate * _sigmoid(gate) * up).astype(BF16)
        ys_ref[...] = _dot(hid, wd_bf[...]).astype(ys_ref.dtype)

        @pl.when(i == n_used - 1)
        def _():
            _wait_pieces(xs_hbm, oth, oth_sem)

    for s, (cur, oth) in enumerate(((xs_buf0, xs_buf1), (xs_buf1, xs_buf0))):
        @pl.when((i < n_used) & (i % 2 == s))
        def _(s=s, cur=cur, oth=oth):
            block(cur, sem.at[s], oth, sem.at[1 - s])

    @pl.when(i >= n_used)
    def _():
        ys_ref[...] = jnp.zeros(ys_ref.shape, ys_ref.dtype)


def _experts(block_e, piece_src, n_used, xs, wg, wu, wd, n_blocks):
    d = xs.shape[1]
    by_expert = lambda i, be, ps, nu: (0, be[i], 0, 0)
    grid_spec = pltpu.PrefetchScalarGridSpec(
        num_scalar_prefetch=3,
        grid=(n_blocks,),
        in_specs=[pl.BlockSpec(memory_space=pl.ANY),
                  pl.BlockSpec((None, None, d, EXPERT_FF), by_expert),
                  pl.BlockSpec((None, None, d, EXPERT_FF), by_expert),
                  pl.BlockSpec((None, None, EXPERT_FF, d), by_expert)],
        out_specs=pl.BlockSpec((ROW_BLOCK, d), lambda i, be, ps, nu: (i, 0)),
        scratch_shapes=[pltpu.VMEM((ROW_BLOCK, d), BF16), pltpu.VMEM((ROW_BLOCK, d), BF16),
                        pltpu.VMEM((d, EXPERT_FF), BF16), pltpu.VMEM((d, EXPERT_FF), BF16),
                        pltpu.VMEM((EXPERT_FF, d), BF16), pltpu.SemaphoreType.DMA((2,))],
    )
    return pl.pallas_call(
        _experts_kernel,
        grid_spec=grid_spec,
        out_shape=jax.ShapeDtypeStruct((n_blocks * ROW_BLOCK, d), BF16),
        compiler_params=pltpu.CompilerParams(
            dimension_semantics=("arbitrary",), vmem_limit_bytes=VMEM_LIMIT),
        name="experts",
    )(block_e, piece_src, n_used, xs, wg, wu, wd)


def _combine_kernel(piece_ref, ys_hbm, x1_ref, route_ref, mod_ref, ln2_ref, o_ref, buf, sem):
    i = pl.program_id(0)
    n = pl.num_programs(0)
    slot = i % 2
    rows = buf.shape[1]
    per_tile = rows // PIECE

    def issue(tile, s):
        def body(p, _):
            _piece_copy(ys_hbm, piece_ref[tile * per_tile + p], buf.at[s], p, sem.at[s]).start()
            return 0
        lax.fori_loop(0, per_tile, body, 0)

    @pl.when(i == 0)
    def _():
        issue(0, 0)

    _wait_pieces(ys_hbm, buf.at[slot], sem.at[slot])

    @pl.when(i + 1 < n)
    def _():
        issue(i + 1, 1 - slot)

    tm = x1_ref.shape[0]
    lane = lax.broadcasted_iota(jnp.int32, (tm, rows), 1).astype(F32)
    sel = (jnp.where(lane == route_ref[:, 4:5], route_ref[:, 2:3], 0.0)
           + jnp.where(lane == route_ref[:, 5:6], route_ref[:, 3:4], 0.0)).astype(BF16)
    moe = _dot(sel, buf[slot])
    z = ALPHA * x1_ref[...] + mod_ref[0:1, :] * moe
    o_ref[...] = _ln(z) * ln2_ref[0:1, :] + ln2_ref[1:2, :]


def _combine(piece_glob, ys, x1, route, mod, ln2, tm, tiles_per_batch):
    t, d = x1.shape
    grid_spec = pltpu.PrefetchScalarGridSpec(
        num_scalar_prefetch=1,
        grid=(t // tm,),
        in_specs=[pl.BlockSpec(memory_space=pl.ANY),
                  pl.BlockSpec((tm, d), lambda i, pg: (i, 0)),
                  pl.BlockSpec((tm, ROUTE_LANES), lambda i, pg: (i, 0)),
                  pl.BlockSpec((None, 8, d), lambda i, pg: (i // tiles_per_batch, 0, 0)),
                  pl.BlockSpec((2, d), lambda i, pg: (0, 0))],
        out_specs=pl.BlockSpec((tm, d), lambda i, pg: (i, 0)),
        scratch_shapes=[pltpu.VMEM((2, _local_rows(tm), d), BF16), pltpu.SemaphoreType.DMA((2,))],
    )
    return pl.pallas_call(
        _combine_kernel,
        grid_spec=grid_spec,
        out_shape=jax.ShapeDtypeStruct((t, d), F32),
        compiler_params=pltpu.CompilerParams(
            dimension_semantics=("arbitrary",), vmem_limit_bytes=VMEM_LIMIT),
        name="combine",
    )(piece_glob, ys, x1, route, mod, ln2)


def _sincos_2d(rows, cols, dim):
    q = dim // 4
    omega = 1.0 / (POS_BASE ** (jnp.arange(q, dtype=F32) / q))
    r = jnp.arange(rows, dtype=F32)[:, None] * omega
    cl = jnp.arange(cols, dtype=F32)[:, None] * omega
    r_emb = jnp.concatenate([jnp.sin(r), jnp.cos(r)], -1)
    c_emb = jnp.concatenate([jnp.sin(cl), jnp.cos(cl)], -1)
    return r_emb, c_emb


def _routing_tables(run_pieces, xs_rows, n_blocks):
    i32 = jnp.int32
    n_tiles = run_pieces.shape[0]
    ppb = ROW_BLOCK // PIECE
    loc_start = jnp.cumsum(run_pieces, axis=1) - run_pieces
    seg_tot = jnp.sum(run_pieces, axis=0)
    seg_pad = (seg_tot + ppb - 1) // ppb * ppb
    seg_end = jnp.cumsum(seg_pad)
    seg_start = seg_end - seg_pad
    run_t = run_pieces.T
    glob_start = seg_start[:, None] + jnp.cumsum(run_t, axis=1) - run_t
    n_used = (seg_end[-1] // ppb).astype(i32)
    blk = jnp.minimum(jnp.arange(n_blocks, dtype=i32), n_used - 1)
    block_e = jnp.minimum(jnp.sum((seg_end[None, :] <= (blk * ppb)[:, None]).astype(i32), axis=1),
                          N_EXPERTS - 1).astype(i32)
    lpt = xs_rows // PIECE
    starts = glob_start.reshape(-1)
    lens = run_t.reshape(-1)
    src0 = (jnp.arange(n_tiles, dtype=i32)[None, :] * lpt + loc_start.T).reshape(-1)
    p = jnp.arange(n_blocks * ppb, dtype=i32)
    within = p[:, None] - starts[None, :]
    hit = (within >= 0) & (within < lens[None, :])
    piece_src = jnp.sum(jnp.where(hit, (src0[None, :] + within) * PIECE, 0), axis=1).astype(i32)
    s = jnp.arange(lpt, dtype=i32)
    loc_within = s[None, :, None] - loc_start[:, None, :]
    hit = (loc_within >= 0) & (loc_within < run_pieces[:, None, :])
    piece_glob = jnp.sum(jnp.where(hit, (glob_start.T[:, None, :] + loc_within) * PIECE, 0), axis=2)
    return block_e, piece_src, piece_glob.astype(i32).reshape(-1), n_used.reshape(1)


def kernel(x, c, ctx, c_ctx, w_ada, b_ada, w_in, s5_log_dt_f, s5_a_re_f, s5_a_im_f, s5_b_re_f, s5_b_im_f, s5_c_re_f, s5_c_im_f, s5_log_dt_b, s5_a_re_b, s5_a_im_b, s5_b_re_b, s5_b_im_b, s5_c_re_b, s5_c_im_b, s5_d, s5_w_glu_val, s5_w_glu_gate, conv_w, conv_w_out, w_o, ln1_g, ln1_b, router_w_group, router_b_group, router_w_expert, router_b_expert, exp_w_gate, exp_w_up, exp_w_down, ln2_g, ln2_b):
    b, l, d = x.shape
    lc = ctx.shape[1]
    assert d == D_MODEL and b < SUBLANES and w_ada.shape[0] == DEPTH
    assert l % (SUBLANES * CHUNK) == 0 and lc % (SUBLANES * CHUNK) == 0 and l % GRID_W == 0
    t = b * l
    tm = min(512, l)
    tmc = min(512, lc)

    cc = jnp.concatenate([c, c_ctx[None, :], jnp.zeros((8 - b - 1, d), F32)], 0)
    mods = _mods(cc, w_ada[0], b_ada[0])
    sh1, sc1, g1, sh2, sc2, g2 = jnp.split(mods, 6, axis=-1)
    mod_a = jnp.stack([sh1[:b], 1.0 + sc1[:b]], 1)
    mod_ctx = jnp.broadcast_to(jnp.stack([sh1[b], 1.0 + sc1[b]], 0)[None], (b, 2, d))
    mod_c = jnp.stack([g1[:b], sh2[:b], 1.0 + sc2[:b], jnp.zeros((b, d), F32)], 1)
    mod_f = jnp.concatenate([g2[:b, None, :], jnp.zeros((b, 7, d), F32)], 1)

    w_in_bf = w_in[0].astype(BF16)
    both = lambda fwd, bwd: jnp.concatenate([fwd, bwd], 0)
    s5_tab = _s5_tables(both(s5_log_dt_f, s5_log_dt_b), both(s5_a_re_f, s5_a_re_b), both(s5_a_im_f, s5_a_im_b),
                        both(s5_b_re_f, s5_b_re_b), both(s5_b_im_f, s5_b_im_b),
                        both(s5_c_re_f, s5_c_re_b), both(s5_c_im_f, s5_c_im_b))
    mi, ws, wo_s5, tab = _s5_operators(s5_tab, s5_d[0])

    (uc_ctx,) = _in_proj(ctx, jnp.zeros((lc // GRID_W, d // 2), F32), jnp.zeros((GRID_W, d // 2), F32),
                         mod_ctx, w_in_bf, None, None, tmc, False)
    zero_state = jnp.zeros((N_PAIRS, 4, SUBLANES, LANES), F32)
    _, s0 = _s5_scan(uc_ctx, mi, ws, wo_s5, tab, zero_state, b)

    r_emb, c_emb = _sincos_2d(l // GRID_W, GRID_W, d)
    uc, siga, sb = _in_proj(x, r_emb, c_emb, mod_a, w_in_bf, conv_w[0], conv_w_out[0].astype(BF16), tm, True)
    y, _ = _s5_scan(uc, mi, ws, wo_s5, tab, s0, b)

    wr = jnp.concatenate([router_w_group[0], router_w_expert[0],
                          jnp.zeros((d, ROUTE_LANES - N_EXPERT_GROUPS - N_EXPERTS), F32)], 1).T.astype(BF16)
    br = jnp.concatenate([router_b_group[0], router_b_expert[0],
                          jnp.zeros((ROUTE_LANES - N_EXPERT_GROUPS - N_EXPERTS,), F32)])[:, None]
    ln1 = jnp.stack([ln1_g[0], ln1_b[0]], 0)
    x1, xs, route, run_len = _mix_out(y, siga, sb, x, r_emb, c_emb, mod_c,
                                      s5_w_glu_val[0].astype(BF16), s5_w_glu_gate[0].astype(BF16),
                                      w_o[0].astype(BF16), ln1, wr, br, tm)

    x1 = x1.reshape(t, d)
    route = route.reshape(t, ROUTE_LANES)
    n_tiles = t // tm
    xs_rows = _local_rows(tm)
    run_pieces = run_len[:, :N_EXPERTS, 0].astype(jnp.int32)
    max_rows = t * TOP_K + n_tiles * N_EXPERTS * (PIECE - 1) + N_EXPERTS * (ROW_BLOCK - 1)
    n_blocks = -(-max_rows // ROW_BLOCK)
    block_e, piece_src, piece_glob, n_used = _routing_tables(run_pieces, xs_rows, n_blocks)
    ys = _experts(block_e, piece_src, n_used, xs, exp_w_gate, exp_w_up, exp_w_down, n_blocks)
    ln2 = jnp.stack([ln2_g[0], ln2_b[0]], 0)
    out = _combine(piece_glob, ys, x1, route, mod_f, ln2, tm, l // tm)
    return out.reshape(b, l, d)
```

```python
import functools
import math

import jax
import jax.numpy as jnp
from jax import lax
from jax.experimental import pallas as pl
from jax.experimental.pallas import tpu as pltpu

F32 = jnp.float32
BF16 = jnp.bfloat16
HI = lax.Precision.HIGHEST

D_MODEL = 1024
GRID_W = 64
S5_WIDTH = 512
S5_GROUP_CH = 16
S5_GROUPS = S5_WIDTH // S5_GROUP_CH
S5_STATE = 64
CONV_WIDTH = 512
N_EXPERT_GROUPS = 4
EXPERTS_PER_GROUP = 8
N_EXPERTS = N_EXPERT_GROUPS * EXPERTS_PER_GROUP
EXPERT_FF = 512
TOP_K = 2
DEPTH = 1
ALPHA = (2.0 * DEPTH) ** 0.25
LN_EPS = 1e-6
POS_BASE = 10000.0

LANES = 128
SUBLANES = 8
CHUNK = 16
GROUP_W = CHUNK * S5_GROUP_CH
PAIR_W = 2 * GROUP_W
N_PAIRS = S5_GROUPS // 2
TOK_PER_VREG = LANES // S5_GROUP_CH
TAB_ROWS = 24
TAB_POW = 2 * SUBLANES
MODS_COLS = 1536
ROUTE_LANES = 128
REC_W1, REC_W2, REC_POS1, REC_POS2 = range(4)
TOKEN_TILE = 512
ROW_BLOCK = 512
PIECE = 16
VMEM_LIMIT = 56 * 1024 * 1024


def _ln(x):
    mu = jnp.mean(x, axis=-1, keepdims=True)
    xc = x - mu
    var = jnp.mean(xc * xc, axis=-1, keepdims=True)
    return xc * lax.rsqrt(var + LN_EPS)


def _sigmoid(x):
    return 0.5 * (jnp.tanh(0.5 * x) + 1.0)


def _dot(a, b):
    return jnp.dot(a, b, preferred_element_type=F32)


def _mods_kernel(c_ref, w_ref, b_ref, o_ref):
    c = c_ref[...]
    a = c * _sigmoid(c)
    o_ref[...] = jnp.dot(a, w_ref[...], precision=HI, preferred_element_type=F32) + b_ref[...]


def _mods(cc, w_ada, b_ada):
    n = w_ada.shape[1]
    nb = MODS_COLS
    return pl.pallas_call(
        _mods_kernel,
        grid=(n // nb,),
        in_specs=[pl.BlockSpec((SUBLANES, D_MODEL), lambda i: (0, 0)),
                  pl.BlockSpec((D_MODEL, nb), lambda i: (0, i)),
                  pl.BlockSpec((1, nb), lambda i: (0, i))],
        out_specs=pl.BlockSpec((SUBLANES, nb), lambda i: (0, i)),
        out_shape=jax.ShapeDtypeStruct((SUBLANES, n), F32),
        compiler_params=pltpu.CompilerParams(vmem_limit_bytes=VMEM_LIMIT),
        name="mods",
    )(cc, w_ada, b_ada.reshape(1, n))


def _slot_masks(rows):
    slot = lax.broadcasted_iota(jnp.int32, (rows, LANES), 1) // S5_GROUP_CH
    return [slot == s for s in range(TOK_PER_VREG)]


def _to_chunk_tile(u_scr, uc_ref):
    nch = uc_ref.shape[0]
    masks = _slot_masks(nch)
    for qh in range(CHUNK // TOK_PER_VREG):
        for v in range(S5_WIDTH // LANES):
            src = [u_scr[v, pl.ds(qh * TOK_PER_VREG + s, nch, stride=CHUNK), :] for s in range(TOK_PER_VREG)]
            for i in range(TOK_PER_VREG):
                acc = None
                for s in range(TOK_PER_VREG):
                    shift = ((s - i) * S5_GROUP_CH) % LANES
                    piece = pltpu.roll(src[s], shift, 1) if shift else src[s]
                    acc = piece if acc is None else jnp.where(masks[s], piece, acc)
                lo = (v * TOK_PER_VREG + i) * GROUP_W + qh * LANES
                uc_ref[:, lo:lo + LANES] = acc.astype(uc_ref.dtype)


def _from_chunk_tile(yc_ref, y_scr):
    nch = yc_ref.shape[0]
    masks = _slot_masks(nch)
    for qh in range(CHUNK // TOK_PER_VREG):
        for v in range(S5_WIDTH // LANES):
            src = []
            for i in range(TOK_PER_VREG):
                lo = (v * TOK_PER_VREG + i) * GROUP_W + qh * LANES
                src.append(yc_ref[:, lo:lo + LANES].astype(F32))
            for s in range(TOK_PER_VREG):
                acc = None
                for i in range(TOK_PER_VREG):
                    shift = ((i - s) * S5_GROUP_CH) % LANES
                    piece = pltpu.roll(src[i], shift, 1) if shift else src[i]
                    acc = piece if acc is None else jnp.where(masks[i], piece, acc)
                y_scr[v, pl.ds(qh * TOK_PER_VREG + s, nch, stride=CHUNK), :] = acc


def _with_positions(x_ref, remb_ref, cemb_ref):
    c = cemb_ref[...]
    slabs = []
    for j in range(x_ref.shape[0] // GRID_W):
        r = jnp.broadcast_to(remb_ref[j:j + 1, :], c.shape)
        slabs.append(x_ref[j * GRID_W:(j + 1) * GRID_W, :] + jnp.concatenate([r, c], axis=-1))
    return jnp.concatenate(slabs, axis=0)


def _in_proj_kernel(x_ref, remb_ref, cemb_ref, mod_ref, w_ref, *rest, full):
    if full:
        cw_ref, cwo_ref, uc_ref, siga_ref, sb_ref, u_scr = rest
    else:
        uc_ref, u_scr = rest
    xp = _with_positions(x_ref, remb_ref, cemb_ref)
    h = (_ln(xp) * mod_ref[1:2, :] + mod_ref[0:1, :]).astype(BF16)
    o1 = S5_WIDTH
    o2, o3, o4 = o1 + CONV_WIDTH, o1 + 2 * CONV_WIDTH, o1 + 3 * CONV_WIDTH
    o5 = o4 + D_MODEL
    u = _dot(h, w_ref[:, 0:o1])
    for v in range(S5_WIDTH // LANES):
        u_scr[v] = u[:, v * LANES:(v + 1) * LANES]
    _to_chunk_tile(u_scr, uc_ref)
    if not full:
        return
    z_b = _dot(h, w_ref[:, o1:o2])
    gate_c = _dot(h, w_ref[:, o3:o4])
    p = gate_c * z_b
    tm = p.shape[0]
    col = lax.broadcasted_iota(jnp.int32, (tm, 1), 0) % GRID_W
    prev = jnp.where(col == 0, 0.0, pltpu.roll(p, 1, 0))
    nxt = jnp.where(col == GRID_W - 1, 0.0, pltpu.roll(p, tm - 1, 0))
    v = cw_ref[0:1, :] * prev + cw_ref[1:2, :] * p + cw_ref[2:3, :] * nxt
    gate_b = _dot(h, w_ref[:, o2:o3])
    out_b = _dot((gate_b * v).astype(BF16), cwo_ref[...])
    merge_b = _dot(h, w_ref[:, o5:])
    sb_ref[...] = (_sigmoid(merge_b) * out_b).astype(sb_ref.dtype)
    merge_a = _dot(h, w_ref[:, o4:o5])
    siga_ref[...] = _sigmoid(merge_a).astype(siga_ref.dtype)


def _pos_specs(tm, d):
    return [pl.BlockSpec((tm // GRID_W, d // 2), lambda i, j: (i, 0)),
            pl.BlockSpec((GRID_W, d // 2), lambda i, j: (0, 0))]


def _in_proj(x, r_emb, c_emb, mod, w_in_bf, conv_w, conv_w_out_bf, tm, full):
    b, l, d = x.shape
    n_tiles = l // tm
    grid = (n_tiles, b)
    tok = lambda w: pl.BlockSpec((None, tm, w), lambda i, j: (j, i, 0))
    chunk_spec = pl.BlockSpec((tm // CHUNK, CHUNK * S5_WIDTH), lambda i, j: (j * n_tiles + i, 0))
    chunk_shape = jax.ShapeDtypeStruct((b * l // CHUNK, CHUNK * S5_WIDTH), BF16)
    in_specs = [tok(d)] + _pos_specs(tm, d) + [pl.BlockSpec((None, 2, d), lambda i, j: (j, 0, 0))]
    args = [x, r_emb, c_emb, mod]
    if full:
        in_specs += [pl.BlockSpec(w_in_bf.shape, lambda i, j: (0, 0)),
                     pl.BlockSpec(conv_w.shape, lambda i, j: (0, 0)),
                     pl.BlockSpec(conv_w_out_bf.shape, lambda i, j: (0, 0))]
        args += [w_in_bf, conv_w, conv_w_out_bf]
        out_specs = [chunk_spec, tok(d), tok(d)]
        out_shape = [chunk_shape,
                     jax.ShapeDtypeStruct((b, l, d), BF16),
                     jax.ShapeDtypeStruct((b, l, d), BF16)]
    else:
        in_specs += [pl.BlockSpec((d, S5_WIDTH), lambda i, j: (0, 0))]
        args += [w_in_bf]
        out_specs = [chunk_spec]
        out_shape = [chunk_shape]
    return pl.pallas_call(
        functools.partial(_in_proj_kernel, full=full),
        grid=grid, in_specs=in_specs, out_specs=out_specs, out_shape=out_shape,
        scratch_shapes=[pltpu.VMEM((S5_WIDTH // LANES, tm, LANES), F32)],
        compiler_params=pltpu.CompilerParams(
            dimension_semantics=("arbitrary", "arbitrary"), vmem_limit_bytes=VMEM_LIMIT),
        name="in_proj" if full else "in_proj_ctx",
    )(*args)


def _s5_tables(log_dt, a_re, a_im, b_re, b_im, c_re, c_im):
    f32 = F32
    dt = jnp.exp(log_dt.astype(f32))[..., None]
    a_re = a_re.astype(f32)
    a_im = a_im.astype(f32)
    mag = jnp.exp(dt * a_re)
    ab_re = mag * jnp.cos(dt * a_im)
    ab_im = mag * jnp.sin(dt * a_im)
    den = a_re * a_re + a_im * a_im
    x_re = ab_re - 1.0
    f_re = (x_re * a_re + ab_im * a_im) / den
    f_im = (ab_im * a_re - x_re * a_im) / den
    b_re = b_re.astype(f32)
    b_im = b_im.astype(f32)
    bb_re = f_re[..., None] * b_re - f_im[..., None] * b_im
    bb_im = f_re[..., None] * b_im + f_im[..., None] * b_re
    k = jnp.arange(CHUNK + 1, dtype=f32)[None, :, None, None]
    pmag = jnp.exp(k * (dt * a_re)[:, None])
    p_re = pmag * jnp.cos(k * (dt * a_im)[:, None])
    p_im = pmag * jnp.sin(k * (dt * a_im)[:, None])
    pb_re = p_re[..., None] * bb_re[:, None] - p_im[..., None] * bb_im[:, None]
    pb_im = p_re[..., None] * bb_im[:, None] + p_im[..., None] * bb_re[:, None]
    c_re = c_re.astype(f32)[:, None]
    c_im = c_im.astype(f32)[:, None]
    cp_re = c_re * p_re[:, :, :, None, :] - c_im * p_im[:, :, :, None, :]
    cp_im = -(c_re * p_im[:, :, :, None, :] + c_im * p_re[:, :, :, None, :])
    return dict(p_re=p_re, p_im=p_im, pb_re=pb_re, pb_im=pb_im, cp_re=cp_re, cp_im=cp_im,
                bb_re=bb_re, bb_im=bb_im)


def _lag_kernels(t):
    g, n, c = S5_GROUPS, S5_STATE, S5_GROUP_CH
    k = CHUNK + 1
    lhs = jnp.concatenate([t['cp_re'], t['cp_im']], -1)
    lhs = lhs.transpose(0, 2, 1, 3, 4).reshape(2 * g, k * c, 2 * n)
    rhs = jnp.concatenate([t['bb_re'], t['bb_im']], -2).reshape(2 * g, 2 * n, c)
    out = jnp.einsum('bmn,bnc->bmc', lhs, rhs, precision=HI)
    out = out.reshape(2, g, k, c, c).transpose(0, 2, 1, 4, 3)
    return out[0], out[1]


def _s5_operators(t, s5_d):
    q = CHUNK
    g, n, c = S5_GROUPS, S5_STATE, S5_GROUP_CH
    kern_f, kern_b = _lag_kernels(t)
    k0 = kern_f[0] + kern_b[0] + s5_d.astype(F32)[:, :, None] * jnp.eye(c, dtype=F32)[None]
    kc = jnp.concatenate([kern_b[1:q][::-1], k0[None], kern_f[1:q]], 0)
    kct = kc.transpose(1, 2, 0, 3)
    m_intra = jnp.stack([kct[:, :, q - 1 - i:2 * q - 1 - i, :] for i in range(q)], 1)
    m_intra = m_intra.reshape(g, q * c, q * c)
    w_st = jnp.stack([t['pb_re'][0, :q][::-1], t['pb_im'][0, :q][::-1],
                      t['pb_re'][1, :q], t['pb_im'][1, :q]], 0)
    w_st = w_st.transpose(2, 1, 4, 0, 3).reshape(g, q * c, 4, n)
    w_out = jnp.stack([t['cp_re'][0, 1:], t['cp_im'][0, 1:],
                       t['cp_re'][1, 1:][::-1], t['cp_im'][1, 1:][::-1]], 0)
    w_out = w_out.transpose(2, 0, 4, 1, 3).reshape(g, 4, n, q * c)
    np_ = N_PAIRS
    w_st = w_st.astype(BF16).reshape(np_, 2, q * c, 4, n)
    ws_pair = jnp.concatenate([jnp.pad(w_st[:, 0], ((0, 0), (0, 0), (0, 0), (0, n))),
                               jnp.pad(w_st[:, 1], ((0, 0), (0, 0), (0, 0), (n, 0)))], 1)
    ws_pair = ws_pair.reshape(np_, PAIR_W, 4 * 2 * n)
    w_out = w_out.astype(BF16).reshape(np_, 2, 4, n, q * c)
    wo_pair = jnp.stack([jnp.pad(w_out[:, 0], ((0, 0), (0, 0), (0, 0), (0, q * c))),
                         jnp.pad(w_out[:, 1], ((0, 0), (0, 0), (0, 0), (q * c, 0)))], 2)
    wo_pair = wo_pair.reshape(np_, 4 * 2 * n, PAIR_W)
    tab = _chunk_power_table(t).reshape(2 * TAB_ROWS, np_, 2 * n).transpose(1, 0, 2)
    return m_intra.astype(BF16), ws_pair, wo_pair, tab


def _chunk_power_table(t):
    def cmul(x, y):
        return x[0] * y[0] - x[1] * y[1], x[0] * y[1] + x[1] * y[0]
    p1 = (t['p_re'][:, CHUNK], t['p_im'][:, CHUNK])
    p2 = cmul(p1, p1)
    p4 = cmul(p2, p2)
    p8 = cmul(p4, p4)
    pr = [(jnp.ones_like(p1[0]), jnp.zeros_like(p1[0]))]
    for _ in range(SUBLANES - 1):
        pr.append(cmul(pr[-1], p1))
    pr_re = jnp.stack([p[0] for p in pr], 0)
    pr_im = jnp.stack([p[1] for p in pr], 0)
    pw = jnp.stack([p1[0], p1[1], p2[0], p2[1], p4[0], p4[1], p8[0], p8[1]], 0)
    return jnp.concatenate([pr_re[:, 0], pr_im[:, 0], pw[:, 0],
                            pr_re[::-1, 1], pr_im[::-1, 1], pw[:, 1]], 0)


def _s5_scan_kernel(uc_ref, mi_ref, ws_ref, wo_ref, tab_ref, s0_ref, y_ref, fin_ref, s_scr, in_scr, *, batch):
    rows = uc_ref.shape[0]
    chunks = rows // batch
    n_tiles = chunks // SUBLANES
    u = uc_ref[...]
    s_scr[...] = _dot(u, ws_ref[...])
    row = lax.broadcasted_iota(jnp.int32, (SUBLANES, LANES), 0)

    def tile_scan(r0, backward, c_re, c_im):
        base = TAB_ROWS if backward else 0
        col = 2 * LANES if backward else 0
        rs = pl.ds(r0, SUBLANES)

        def shift(z, k):
            if backward:
                return jnp.where(row < SUBLANES - k, pltpu.roll(z, SUBLANES - k, 0), 0.0)
            return jnp.where(row >= k, pltpu.roll(z, k, 0), 0.0)

        z_re = s_scr[rs, col:col + LANES]
        z_im = s_scr[rs, col + LANES:col + 2 * LANES]
        for k, t in ((1, TAB_POW), (2, TAB_POW + 2), (4, TAB_POW + 4)):
            a_re = tab_ref[base + t:base + t + 1, :]
            a_im = tab_ref[base + t + 1:base + t + 2, :]
            sh_re = shift(z_re, k)
            sh_im = shift(z_im, k)
            z_re, z_im = z_re + (a_re * sh_re - a_im * sh_im), z_im + (a_re * sh_im + a_im * sh_re)
        pr_re = tab_ref[base:base + SUBLANES, :]
        pr_im = tab_ref[base + SUBLANES:base + 2 * SUBLANES, :]
        in_scr[rs, col:col + LANES] = pr_re * c_re - pr_im * c_im + shift(z_re, 1)
        in_scr[rs, col + LANES:col + 2 * LANES] = pr_re * c_im + pr_im * c_re + shift(z_im, 1)
        last = 0 if backward else SUBLANES - 1
        l_re = jnp.broadcast_to(z_re[last:last + 1, :], (SUBLANES, LANES))
        l_im = jnp.broadcast_to(z_im[last:last + 1, :], (SUBLANES, LANES))
        p8_re = tab_ref[base + TAB_POW + 6:base + TAB_POW + 7, :]
        p8_im = tab_ref[base + TAB_POW + 7:base + TAB_POW + 8, :]
        return p8_re * c_re - p8_im * c_im + l_re, p8_re * c_im + p8_im * c_re + l_im

    def body(m, carry):
        out = []
        for b in range(batch):
            cf_re, cf_im, cb_re, cb_im = carry[4 * b:4 * b + 4]
            rf = pl.multiple_of(b * chunks + m * SUBLANES, SUBLANES)
            rb = pl.multiple_of(b * chunks + (n_tiles - 1 - m) * SUBLANES, SUBLANES)
            out += list(tile_scan(rf, False, cf_re, cf_im))
            out += list(tile_scan(rb, True, cb_re, cb_im))
        return tuple(out)

    init = tuple(jnp.broadcast_to(s0_ref[t, b:b + 1, :], (SUBLANES, LANES))
                 for b in range(batch) for t in range(4))
    fin = lax.fori_loop(0, n_tiles, body, init, unroll=min(4, n_tiles))
    fin_ref[...] = jnp.zeros(fin_ref.shape, F32)
    for b in range(batch):
        for t in range(4):
            fin_ref[t, b:b + 1, :] = fin[4 * b + t][0:1, :]
    y_intra = jnp.concatenate([_dot(u[:, gl * GROUP_W:(gl + 1) * GROUP_W], mi_ref[gl]) for gl in range(2)], axis=-1)
    y = y_intra + _dot(in_scr[...].astype(BF16), wo_ref[...])
    y_ref[...] = y.astype(y_ref.dtype)


def _s5_scan(uc, mi, ws, wo, tab, s0, batch):
    rows = uc.shape[0]
    pair = lambda *shape: pl.BlockSpec((None,) + shape, lambda p: (p,) + (0,) * len(shape))
    return pl.pallas_call(
        functools.partial(_s5_scan_kernel, batch=batch),
        grid=(N_PAIRS,),
        in_specs=[pl.BlockSpec((rows, PAIR_W), lambda p: (0, p)),
                  pl.BlockSpec((2, GROUP_W, GROUP_W), lambda p: (p, 0, 0)),
                  pair(PAIR_W, PAIR_W), pair(PAIR_W, PAIR_W),
                  pair(2 * TAB_ROWS, LANES), pair(4, SUBLANES, LANES)],
        out_specs=[pl.BlockSpec((rows, PAIR_W), lambda p: (0, p)), pair(4, SUBLANES, LANES)],
        out_shape=[jax.ShapeDtypeStruct((rows, N_PAIRS * PAIR_W), BF16),
                   jax.ShapeDtypeStruct((N_PAIRS, 4, SUBLANES, LANES), F32)],
        scratch_shapes=[pltpu.VMEM((rows, PAIR_W), F32), pltpu.VMEM((rows, PAIR_W), F32)],
        compiler_params=pltpu.CompilerParams(
            dimension_semantics=("arbitrary",), vmem_limit_bytes=VMEM_LIMIT),
        name="s5_scan",
    )(uc, mi, ws, wo, tab, s0)


def _mix_out_kernel(y_ref, siga_ref, sb_ref, x_ref, remb_ref, cemb_ref, mod_ref, wv_ref, wg_ref, wo_ref,
                    ln1_ref, wr_ref, br_ref, ut_ref, lt_ref,
                    x1_ref, xs_ref, route_ref, len_ref, y_scr):
    _from_chunk_tile(y_ref, y_scr)
    y = jnp.concatenate([y_scr[v] for v in range(S5_WIDTH // LANES)], axis=-1)
    ya = (0.5 * y * (1.0 + jnp.tanh(math.sqrt(2.0 / math.pi) * (y + 0.044715 * (y * y * y))))).astype(BF16)
    out_a = _dot(ya, wv_ref[...]) * _sigmoid(_dot(ya, wg_ref[...]))
    merged = siga_ref[...].astype(F32) * out_a + sb_ref[...].astype(F32)
    mix = _dot(merged.astype(BF16), wo_ref[...])
    xp = _with_positions(x_ref, remb_ref, cemb_ref)
    x1 = _ln(ALPHA * xp + mod_ref[0:1, :] * mix) * ln1_ref[0:1, :] + ln1_ref[1:2, :]
    x1_ref[...] = x1
    h_hi = (_ln(x1) * mod_ref[2:3, :] + mod_ref[1:2, :]).astype(BF16)
    _route_and_sort(h_hi, wr_ref, br_ref, ut_ref, lt_ref, xs_ref, route_ref, len_ref)


def _route_and_sort(h_hi, wr_ref, br_ref, ut_ref, lt_ref, xs_ref, route_ref, len_ref):
    tm = h_hi.shape[0]
    f32 = F32
    nt = (((1,), (1,)), ((), ()))
    lg = lax.dot_general(wr_ref[...], h_hi, nt, preferred_element_type=f32) + br_ref[...]
    rowi = lax.broadcasted_iota(jnp.int32, (ROUTE_LANES, tm), 0).astype(f32)
    neg = jnp.float32(-jnp.inf)
    big = jnp.float32(ROUTE_LANES)
    gl = jnp.where(rowi < N_EXPERT_GROUPS, lg, neg)
    gmax = jnp.max(gl, axis=0, keepdims=True)
    g_idx = jnp.min(jnp.where(gl == gmax, rowi, big), axis=0, keepdims=True)
    p_group = 1.0 / jnp.sum(jnp.exp(gl - gmax), axis=0, keepdims=True)
    e_lo = N_EXPERT_GROUPS + g_idx * EXPERTS_PER_GROUP
    el = jnp.where((rowi >= e_lo) & (rowi < e_lo + EXPERTS_PER_GROUP), lg, neg)
    m1 = jnp.max(el, axis=0, keepdims=True)
    i1 = jnp.min(jnp.where(el == m1, rowi, big), axis=0, keepdims=True)
    el2 = jnp.where(rowi == i1, neg, el)
    m2 = jnp.max(el2, axis=0, keepdims=True)
    i2 = jnp.min(jnp.where(el2 == m2, rowi, big), axis=0, keepdims=True)
    r = jnp.exp(m2 - m1)
    w1 = p_group / (1.0 + r)
    w2 = p_group * r / (1.0 + r)
    e1 = i1 - N_EXPERT_GROUPS
    e2 = i2 - N_EXPERT_GROUPS
    a12 = jnp.where(rowi == e1, 1.0, 0.0) + jnp.where(rowi == e2 + N_EXPERTS, 1.0, 0.0)
    rank = _dot(a12.astype(BF16), ut_ref[...])
    cnt = jnp.broadcast_to(jnp.sum(a12, axis=1, keepdims=True), (ROUTE_LANES, LANES))
    row = lax.broadcasted_iota(jnp.int32, (ROUTE_LANES, LANES), 0)
    tot = cnt + pltpu.roll(cnt, ROUTE_LANES - N_EXPERTS, 0)
    run = jnp.where(row < N_EXPERTS, jnp.floor((tot + (PIECE - 1)) * (1.0 / PIECE)), 0.0)
    off = PIECE * _dot(lt_ref[...], run.astype(BF16))
    base = jnp.where(row < N_EXPERTS, off, pltpu.roll(off + cnt, N_EXPERTS, 0))
    posmat = a12 * (rank + base[:, 0:1])
    pos1 = jnp.sum(posmat[0:N_EXPERTS, :], axis=0, keepdims=True)
    pos2 = jnp.sum(posmat[N_EXPERTS:2 * N_EXPERTS, :], axis=0, keepdims=True)
    ri = lax.broadcasted_iota(jnp.int32, (xs_ref.shape[0], tm), 0).astype(f32)
    perm = jnp.where((ri == pos1) | (ri == pos2), 1.0, 0.0).astype(BF16)
    xs_ref[...] = _dot(perm, h_hi).astype(xs_ref.dtype)
    rec = jnp.zeros_like(lg)
    for col, val in ((REC_W1, w1), (REC_W2, w2), (REC_POS1, pos1), (REC_POS2, pos2)):
        rec = jnp.where(rowi == col, val, rec)
    route_ref[...] = rec.T
    len_ref[...] = run


def _local_rows(tm):
    return TOP_K * tm + N_EXPERTS * PIECE


def _mix_out(y, siga, sb, x, r_emb, c_emb, mod, wv, wg, wo, ln1, wr, br, tm):
    b, l, d = x.shape
    n_tiles = l // tm
    xs_rows = _local_rows(tm)
    tok = lambda w: pl.BlockSpec((None, tm, w), lambda i, j: (j, i, 0))
    whole = lambda a: pl.BlockSpec(a.shape, lambda i, j: (0,) * a.ndim)
    ut = (jnp.arange(tm)[:, None] < jnp.arange(tm)[None, :]).astype(BF16)
    lt = (jnp.arange(ROUTE_LANES)[None, :] < jnp.arange(ROUTE_LANES)[:, None]).astype(BF16)
    return pl.pallas_call(
        _mix_out_kernel,
        grid=(n_tiles, b),
        in_specs=[pl.BlockSpec((tm // CHUNK, CHUNK * S5_WIDTH), lambda i, j: (j * n_tiles + i, 0)),
                  tok(d), tok(d), tok(d)] + _pos_specs(tm, d) + [
                  pl.BlockSpec((None, 4, d), lambda i, j: (j, 0, 0)),
                  whole(wv), whole(wg), whole(wo), whole(ln1), whole(wr), whole(br),
                  whole(ut), whole(lt)],
        out_specs=[tok(d),
                   pl.BlockSpec((xs_rows, d), lambda i, j: (j * n_tiles + i, 0)),
                   tok(ROUTE_LANES),
                   pl.BlockSpec((None, ROUTE_LANES, LANES), lambda i, j: (j * n_tiles + i, 0, 0))],
        out_shape=[jax.ShapeDtypeStruct((b, l, d), F32),
                   jax.ShapeDtypeStruct((b * n_tiles * xs_rows, d), BF16),
                   jax.ShapeDtypeStruct((b, l, ROUTE_LANES), F32),
                   jax.ShapeDtypeStruct((b * n_tiles, ROUTE_LANES, LANES), F32)],
        scratch_shapes=[pltpu.VMEM((S5_WIDTH // LANES, tm, LANES), F32)],
        compiler_params=pltpu.CompilerParams(
            dimension_semantics=("arbitrary", "arbitrary"), vmem_limit_bytes=VMEM_LIMIT),
        name="mix_out",
    )(y, siga, sb, x, r_emb, c_emb, mod, wv, wg, wo, ln1, wr, br, ut, lt)


def _piece_copy(src_hbm, src_row, dst, piece, sem):
    return pltpu.make_async_copy(src_hbm.at[pl.ds(pl.multiple_of(src_row, PIECE), PIECE), :],
                                 dst.at[pl.ds(pl.multiple_of(piece * PIECE, PIECE), PIECE), :], sem)


def _issue_pieces(src_hbm, table_ref, first, n_pieces, dst, sem):
    for p in range(n_pieces):
        _piece_copy(src_hbm, table_ref[first + p], dst, p, sem).start()


def _wait_pieces(src_hbm, dst, sem):
    pltpu.make_async_copy(src_hbm.at[pl.ds(0, dst.shape[0]), :], dst, sem).wait()


def _experts_kernel(be_ref, piece_ref, nused_ref, xs_hbm, wg_ref, wu_ref, wd_ref, ys_ref,
                    xs_buf0, xs_buf1, wg_bf, wu_bf, wd_bf, sem):
    i = pl.program_id(0)
    n_used = nused_ref[0]
    per_block = ROW_BLOCK // PIECE

    @pl.when(i == 0)
    def _():
        _issue_pieces(xs_hbm, piece_ref, 0, per_block, xs_buf0, sem.at[0])

    def block(cur, cur_sem, oth, oth_sem):
        @pl.when((i == 0) | (be_ref[i] != be_ref[jnp.maximum(i - 1, 0)]))
        def _():
            wg_bf[...] = wg_ref[...].astype(BF16)
            wu_bf[...] = wu_ref[...].astype(BF16)
            wd_bf[...] = wd_ref[...].astype(BF16)

        _wait_pieces(xs_hbm, cur, cur_sem)
        nxt = jnp.minimum(i + 1, n_used - 1)
        _issue_pieces(xs_hbm, piece_ref, nxt * per_block, per_block, oth, oth_sem)

        xb = cur[...]
        gate = _dot(xb, wg_bf[...])
        up = _dot(xb, wu_bf[...])
        hid = (gate * _sigmoid(gate) * up).astype(BF16)
        ys_ref[...] = _dot(hid, wd_bf[...]).astype(ys_ref.dtype)

        @pl.when(i == n_used - 1)
        def _():
            _wait_pieces(xs_hbm, oth, oth_sem)

    for s, (cur, oth) in enumerate(((xs_buf0, xs_buf1), (xs_buf1, xs_buf0))):
        @pl.when((i < n_used) & (i % 2 == s))
        def _(s=s, cur=cur, oth=oth):
            block(cur, sem.at[s], oth, sem.at[1 - s])

    @pl.when(i >= n_used)
    def _():
        ys_ref[...] = jnp.zeros(ys_ref.shape, ys_ref.dtype)


def _experts(block_e, piece_src, n_used, xs, wg, wu, wd, n_blocks):
    d = xs.shape[1]
    by_expert = lambda i, be, ps, nu: (0, be[i], 0, 0)
    grid_spec = pltpu.PrefetchScalarGridSpec(
        num_scalar_prefetch=3,
        grid=(n_blocks,),
        in_specs=[pl.BlockSpec(memory_space=pl.ANY),
                  pl.BlockSpec((None, None, d, EXPERT_FF), by_expert),
                  pl.BlockSpec((None, None, d, EXPERT_FF), by_expert),
                  pl.BlockSpec((None, None, EXPERT_FF, d), by_expert)],
        out_specs=pl.BlockSpec((ROW_BLOCK, d), lambda i, be, ps, nu: (i, 0)),
        scratch_shapes=[pltpu.VMEM((ROW_BLOCK, d), BF16), pltpu.VMEM((ROW_BLOCK, d), BF16),
                        pltpu.VMEM((d, EXPERT_FF), BF16), pltpu.VMEM((d, EXPERT_FF), BF16),
                        pltpu.VMEM((EXPERT_FF, d), BF16), pltpu.SemaphoreType.DMA((2,))],
    )
    return pl.pallas_call(
        _experts_kernel,
        grid_spec=grid_spec,
        out_shape=jax.ShapeDtypeStruct((n_blocks * ROW_BLOCK, d), BF16),
        compiler_params=pltpu.CompilerParams(
            dimension_semantics=("arbitrary",), vmem_limit_bytes=VMEM_LIMIT),
        name="experts",
    )(block_e, piece_src, n_used, xs, wg, wu, wd)


def _combine_kernel(piece_ref, ys_hbm, x1_ref, route_ref, mod_ref, ln2_ref, o_ref, buf, sem):
    i = pl.program_id(0)
    n = pl.num_programs(0)
    slot = i % 2
    rows = buf.shape[1]
    per_tile = rows // PIECE

    def issue(tile, s):
        def body(p, _):
            _piece_copy(ys_hbm, piece_ref[tile * per_tile + p], buf.at[s], p, sem.at[s]).start()
            return 0
        lax.fori_loop(0, per_tile, body, 0)

    @pl.when(i == 0)
    def _():
        issue(0, 0)

    _wait_pieces(ys_hbm, buf.at[slot], sem.at[slot])

    @pl.when(i + 1 < n)
    def _():
        issue(i + 1, 1 - slot)

    tm = x1_ref.shape[0]
    lane = lax.broadcasted_iota(jnp.int32, (tm, rows), 1).astype(F32)
    rec = lambda col: route_ref[:, col:col + 1]
    sel = (jnp.where(lane == rec(REC_POS1), rec(REC_W1), 0.0)
           + jnp.where(lane == rec(REC_POS2), rec(REC_W2), 0.0)).astype(BF16)
    moe = _dot(sel, buf[slot])
    z = ALPHA * x1_ref[...] + mod_ref[0:1, :] * moe
    o_ref[...] = _ln(z) * ln2_ref[0:1, :] + ln2_ref[1:2, :]


def _combine(piece_glob, ys, x1, route, mod, ln2, tm, tiles_per_batch):
    t, d = x1.shape
    grid_spec = pltpu.PrefetchScalarGridSpec(
        num_scalar_prefetch=1,
        grid=(t // tm,),
        in_specs=[pl.BlockSpec(memory_space=pl.ANY),
                  pl.BlockSpec((tm, d), lambda i, pg: (i, 0)),
                  pl.BlockSpec((tm, ROUTE_LANES), lambda i, pg: (i, 0)),
                  pl.BlockSpec((None, SUBLANES, d), lambda i, pg: (i // tiles_per_batch, 0, 0)),
                  pl.BlockSpec((2, d), lambda i, pg: (0, 0))],
        out_specs=pl.BlockSpec((tm, d), lambda i, pg: (i, 0)),
        scratch_shapes=[pltpu.VMEM((2, _local_rows(tm), d), BF16), pltpu.SemaphoreType.DMA((2,))],
    )
    return pl.pallas_call(
        _combine_kernel,
        grid_spec=grid_spec,
        out_shape=jax.ShapeDtypeStruct((t, d), F32),
        compiler_params=pltpu.CompilerParams(
            dimension_semantics=("arbitrary",), vmem_limit_bytes=VMEM_LIMIT),
        name="combine",
    )(piece_glob, ys, x1, route, mod, ln2)


def _sincos_2d(rows, cols, dim):
    q = dim // 4
    omega = 1.0 / (POS_BASE ** (jnp.arange(q, dtype=F32) / q))
    r = jnp.arange(rows, dtype=F32)[:, None] * omega
    cl = jnp.arange(cols, dtype=F32)[:, None] * omega
    r_emb = jnp.concatenate([jnp.sin(r), jnp.cos(r)], -1)
    c_emb = jnp.concatenate([jnp.sin(cl), jnp.cos(cl)], -1)
    return r_emb, c_emb


def _routing_tables(run_pieces, xs_rows, n_blocks):
    i32 = jnp.int32
    n_tiles = run_pieces.shape[0]
    ppb = ROW_BLOCK // PIECE
    loc_start = jnp.cumsum(run_pieces, axis=1) - run_pieces
    seg_tot = jnp.sum(run_pieces, axis=0)
    seg_pad = (seg_tot + ppb - 1) // ppb * ppb
    seg_end = jnp.cumsum(seg_pad)
    seg_start = seg_end - seg_pad
    run_t = run_pieces.T
    glob_start = seg_start[:, None] + jnp.cumsum(run_t, axis=1) - run_t
    n_used = (seg_end[-1] // ppb).astype(i32)
    blk = jnp.minimum(jnp.arange(n_blocks, dtype=i32), n_used - 1)
    block_e = jnp.minimum(jnp.sum((seg_end[None, :] <= (blk * ppb)[:, None]).astype(i32), axis=1),
                          N_EXPERTS - 1).astype(i32)
    lpt = xs_rows // PIECE
    starts = glob_start.reshape(-1)
    lens = run_t.reshape(-1)
    src0 = (jnp.arange(n_tiles, dtype=i32)[None, :] * lpt + loc_start.T).reshape(-1)
    p = jnp.arange(n_blocks * ppb, dtype=i32)
    within = p[:, None] - starts[None, :]
    hit = (within >= 0) & (within < lens[None, :])
    piece_src = jnp.sum(jnp.where(hit, (src0[None, :] + within) * PIECE, 0), axis=1).astype(i32)
    s = jnp.arange(lpt, dtype=i32)
    loc_within = s[None, :, None] - loc_start[:, None, :]
    hit = (loc_within >= 0) & (loc_within < run_pieces[:, None, :])
    piece_glob = jnp.sum(jnp.where(hit, (glob_start.T[:, None, :] + loc_within) * PIECE, 0), axis=2)
    return block_e, piece_src, piece_glob.astype(i32).reshape(-1), n_used.reshape(1)


def kernel(x, c, ctx, c_ctx, w_ada, b_ada, w_in, s5_log_dt_f, s5_a_re_f, s5_a_im_f, s5_b_re_f, s5_b_im_f, s5_c_re_f, s5_c_im_f, s5_log_dt_b, s5_a_re_b, s5_a_im_b, s5_b_re_b, s5_b_im_b, s5_c_re_b, s5_c_im_b, s5_d, s5_w_glu_val, s5_w_glu_gate, conv_w, conv_w_out, w_o, ln1_g, ln1_b, router_w_group, router_b_group, router_w_expert, router_b_expert, exp_w_gate, exp_w_up, exp_w_down, ln2_g, ln2_b):
    b, l, d = x.shape
    lc = ctx.shape[1]
    assert d == D_MODEL and b < SUBLANES and w_ada.shape[0] == DEPTH
    assert l % (SUBLANES * CHUNK) == 0 and lc % (SUBLANES * CHUNK) == 0 and l % GRID_W == 0
    t = b * l
    tm = min(TOKEN_TILE, l)
    tmc = min(TOKEN_TILE, lc)

    cc = jnp.concatenate([c, c_ctx[None, :], jnp.zeros((SUBLANES - b - 1, d), F32)], 0)
    mods = _mods(cc, w_ada[0], b_ada[0])
    sh1, sc1, g1, sh2, sc2, g2 = jnp.split(mods, 6, axis=-1)
    mod_a = jnp.stack([sh1[:b], 1.0 + sc1[:b]], 1)
    mod_ctx = jnp.broadcast_to(jnp.stack([sh1[b], 1.0 + sc1[b]], 0)[None], (b, 2, d))
    mod_c = jnp.stack([g1[:b], sh2[:b], 1.0 + sc2[:b], jnp.zeros((b, d), F32)], 1)
    mod_f = jnp.concatenate([g2[:b, None, :], jnp.zeros((b, SUBLANES - 1, d), F32)], 1)

    w_in_bf = w_in[0].astype(BF16)
    both = lambda fwd, bwd: jnp.concatenate([fwd, bwd], 0)
    s5_tab = _s5_tables(both(s5_log_dt_f, s5_log_dt_b), both(s5_a_re_f, s5_a_re_b), both(s5_a_im_f, s5_a_im_b),
                        both(s5_b_re_f, s5_b_re_b), both(s5_b_im_f, s5_b_im_b),
                        both(s5_c_re_f, s5_c_re_b), both(s5_c_im_f, s5_c_im_b))
    mi, ws, wo_s5, tab = _s5_operators(s5_tab, s5_d[0])

    (uc_ctx,) = _in_proj(ctx, jnp.zeros((lc // GRID_W, d // 2), F32), jnp.zeros((GRID_W, d // 2), F32),
                         mod_ctx, w_in_bf, None, None, tmc, False)
    zero_state = jnp.zeros((N_PAIRS, 4, SUBLANES, LANES), F32)
    _, s0 = _s5_scan(uc_ctx, mi, ws, wo_s5, tab, zero_state, b)

    r_emb, c_emb = _sincos_2d(l // GRID_W, GRID_W, d)
    uc, siga, sb = _in_proj(x, r_emb, c_emb, mod_a, w_in_bf, conv_w[0], conv_w_out[0].astype(BF16), tm, True)
    y, _ = _s5_scan(uc, mi, ws, wo_s5, tab, s0, b)

    wr = jnp.concatenate([router_w_group[0], router_w_expert[0],
                          jnp.zeros((d, ROUTE_LANES - N_EXPERT_GROUPS - N_EXPERTS), F32)], 1).T.astype(BF16)
    br = jnp.concatenate([router_b_group[0], router_b_expert[0],
                          jnp.zeros((ROUTE_LANES - N_EXPERT_GROUPS - N_EXPERTS,), F32)])[:, None]
    ln1 = jnp.stack([ln1_g[0], ln1_b[0]], 0)
    x1, xs, route, run_len = _mix_out(y, siga, sb, x, r_emb, c_emb, mod_c,
                                      s5_w_glu_val[0].astype(BF16), s5_w_glu_gate[0].astype(BF16),
                                      w_o[0].astype(BF16), ln1, wr, br, tm)

    x1 = x1.reshape(t, d)
    route = route.reshape(t, ROUTE_LANES)
    n_tiles = t // tm
    xs_rows = _local_rows(tm)
    run_pieces = run_len[:, :N_EXPERTS, 0].astype(jnp.int32)
    max_rows = t * TOP_K + n_tiles * N_EXPERTS * (PIECE - 1) + N_EXPERTS * (ROW_BLOCK - 1)
    n_blocks = -(-max_rows // ROW_BLOCK)
    block_e, piece_src, piece_glob, n_used = _routing_tables(run_pieces, xs_rows, n_blocks)
    ys = _experts(block_e, piece_src, n_used, xs, exp_w_gate, exp_w_up, exp_w_down, n_blocks)
    ln2 = jnp.stack([ln2_g[0], ln2_b[0]], 0)
    out = _combine(piece_glob, ys, x1, route, mod_f, ln2, tm, l // tm)
    return out.reshape(b, l, d)
```

```python
import functools
import math

import jax
import jax.numpy as jnp
from jax import lax
from jax.experimental import pallas as pl
from jax.experimental.pallas import tpu as pltpu

F32 = jnp.float32
BF16 = jnp.bfloat16
HI = lax.Precision.HIGHEST

D_MODEL = 1024
GRID_W = 64
S5_WIDTH = 512
S5_GROUP_CH = 16
S5_GROUPS = S5_WIDTH // S5_GROUP_CH
S5_STATE = 64
CONV_WIDTH = 512
N_EXPERT_GROUPS = 4
EXPERTS_PER_GROUP = 8
N_EXPERTS = N_EXPERT_GROUPS * EXPERTS_PER_GROUP
EXPERT_FF = 512
TOP_K = 2
DEPTH = 1
ALPHA = (2.0 * DEPTH) ** 0.25
LN_EPS = 1e-6
POS_BASE = 10000.0

LANES = 128
SUBLANES = 8
CHUNK = 16
GROUP_W = CHUNK * S5_GROUP_CH
PAIR_W = 2 * GROUP_W
N_PAIRS = S5_GROUPS // 2
TOK_PER_VREG = LANES // S5_GROUP_CH
TAB_ROWS = 24
TAB_POW = 2 * SUBLANES
MODS_COLS = 1536
ROUTE_LANES = 128
LOGIT_ROWS = 48
ASSIGN_ROWS = TOP_K * N_EXPERTS
REC_W1, REC_W2, REC_POS1, REC_POS2 = range(4)
TOKEN_TILE = 512
ROW_BLOCK = 512
PIECE = 16
VMEM_LIMIT = 56 * 1024 * 1024


def _ln(x):
    mu = jnp.mean(x, axis=-1, keepdims=True)
    xc = x - mu
    var = jnp.mean(xc * xc, axis=-1, keepdims=True)
    return xc * lax.rsqrt(var + LN_EPS)


def _sigmoid(x):
    return 0.5 * (jnp.tanh(0.5 * x) + 1.0)


def _dot(a, b):
    return jnp.dot(a, b, preferred_element_type=F32)


def _mods_kernel(c_ref, w_ref, b_ref, o_ref):
    c = c_ref[...]
    a = c * _sigmoid(c)
    o_ref[...] = jnp.dot(a, w_ref[...], precision=HI, preferred_element_type=F32) + b_ref[...]


def _mods(cc, w_ada, b_ada):
    n = w_ada.shape[1]
    nb = MODS_COLS
    return pl.pallas_call(
        _mods_kernel,
        grid=(n // nb,),
        in_specs=[pl.BlockSpec((SUBLANES, D_MODEL), lambda i: (0, 0)),
                  pl.BlockSpec((D_MODEL, nb), lambda i: (0, i)),
                  pl.BlockSpec((1, nb), lambda i: (0, i))],
        out_specs=pl.BlockSpec((SUBLANES, nb), lambda i: (0, i)),
        out_shape=jax.ShapeDtypeStruct((SUBLANES, n), F32),
        compiler_params=pltpu.CompilerParams(vmem_limit_bytes=VMEM_LIMIT),
        name="mods",
    )(cc, w_ada, b_ada.reshape(1, n))


def _slot_masks(rows):
    slot = lax.broadcasted_iota(jnp.int32, (rows, LANES), 1) // S5_GROUP_CH
    return [slot == s for s in range(TOK_PER_VREG)]


def _to_chunk_tile(u_scr, uc_ref):
    nch = uc_ref.shape[0]
    masks = _slot_masks(nch)
    for qh in range(CHUNK // TOK_PER_VREG):
        for v in range(S5_WIDTH // LANES):
            src = [u_scr[v, pl.ds(qh * TOK_PER_VREG + s, nch, stride=CHUNK), :] for s in range(TOK_PER_VREG)]
            for i in range(TOK_PER_VREG):
                acc = None
                for s in range(TOK_PER_VREG):
                    shift = ((s - i) * S5_GROUP_CH) % LANES
                    piece = pltpu.roll(src[s], shift, 1) if shift else src[s]
                    acc = piece if acc is None else jnp.where(masks[s], piece, acc)
                lo = (v * TOK_PER_VREG + i) * GROUP_W + qh * LANES
                uc_ref[:, lo:lo + LANES] = acc.astype(uc_ref.dtype)


def _from_chunk_tile(yc_ref, y_scr):
    nch = yc_ref.shape[0]
    masks = _slot_masks(nch)
    for qh in range(CHUNK // TOK_PER_VREG):
        for v in range(S5_WIDTH // LANES):
            src = []
            for i in range(TOK_PER_VREG):
                lo = (v * TOK_PER_VREG + i) * GROUP_W + qh * LANES
                src.append(yc_ref[:, lo:lo + LANES].astype(F32))
            for s in range(TOK_PER_VREG):
                acc = None
                for i in range(TOK_PER_VREG):
                    shift = ((i - s) * S5_GROUP_CH) % LANES
                    piece = pltpu.roll(src[i], shift, 1) if shift else src[i]
                    acc = piece if acc is None else jnp.where(masks[i], piece, acc)
                y_scr[v, pl.ds(qh * TOK_PER_VREG + s, nch, stride=CHUNK), :] = acc


def _with_positions(x_ref, remb_ref, cemb_ref):
    c = cemb_ref[...]
    slabs = []
    for j in range(x_ref.shape[0] // GRID_W):
        r = jnp.broadcast_to(remb_ref[j:j + 1, :], c.shape)
        slabs.append(x_ref[j * GRID_W:(j + 1) * GRID_W, :] + jnp.concatenate([r, c], axis=-1))
    return jnp.concatenate(slabs, axis=0)


def _in_proj_kernel(x_ref, remb_ref, cemb_ref, mod_ref, w_ref, *rest, full):
    if full:
        cw_ref, cwo_ref, uc_ref, siga_ref, sb_ref, u_scr = rest
    else:
        uc_ref, u_scr = rest
    xp = _with_positions(x_ref, remb_ref, cemb_ref)
    h = (_ln(xp) * mod_ref[1:2, :] + mod_ref[0:1, :]).astype(BF16)
    o1 = S5_WIDTH
    o2, o3, o4 = o1 + CONV_WIDTH, o1 + 2 * CONV_WIDTH, o1 + 3 * CONV_WIDTH
    o5 = o4 + D_MODEL
    u = _dot(h, w_ref[:, 0:o1])
    for v in range(S5_WIDTH // LANES):
        u_scr[v] = u[:, v * LANES:(v + 1) * LANES]
    _to_chunk_tile(u_scr, uc_ref)
    if not full:
        return
    z_b = _dot(h, w_ref[:, o1:o2])
    gate_c = _dot(h, w_ref[:, o3:o4])
    p = gate_c * z_b
    tm = p.shape[0]
    col = lax.broadcasted_iota(jnp.int32, (tm, 1), 0) % GRID_W
    prev = jnp.where(col == 0, 0.0, pltpu.roll(p, 1, 0))
    nxt = jnp.where(col == GRID_W - 1, 0.0, pltpu.roll(p, tm - 1, 0))
    v = cw_ref[0:1, :] * prev + cw_ref[1:2, :] * p + cw_ref[2:3, :] * nxt
    gate_b = _dot(h, w_ref[:, o2:o3])
    out_b = _dot((gate_b * v).astype(BF16), cwo_ref[...])
    merge_b = _dot(h, w_ref[:, o5:])
    sb_ref[...] = (_sigmoid(merge_b) * out_b).astype(sb_ref.dtype)
    merge_a = _dot(h, w_ref[:, o4:o5])
    siga_ref[...] = _sigmoid(merge_a).astype(siga_ref.dtype)


def _pos_specs(tm, d):
    return [pl.BlockSpec((tm // GRID_W, d // 2), lambda i, j: (i, 0)),
            pl.BlockSpec((GRID_W, d // 2), lambda i, j: (0, 0))]


def _in_proj(x, r_emb, c_emb, mod, w_in_bf, conv_w, conv_w_out_bf, tm, full):
    b, l, d = x.shape
    n_tiles = l // tm
    grid = (n_tiles, b)
    tok = lambda w: pl.BlockSpec((None, tm, w), lambda i, j: (j, i, 0))
    chunk_spec = pl.BlockSpec((tm // CHUNK, CHUNK * S5_WIDTH), lambda i, j: (j * n_tiles + i, 0))
    chunk_shape = jax.ShapeDtypeStruct((b * l // CHUNK, CHUNK * S5_WIDTH), BF16)
    in_specs = [tok(d)] + _pos_specs(tm, d) + [pl.BlockSpec((None, 2, d), lambda i, j: (j, 0, 0))]
    args = [x, r_emb, c_emb, mod]
    if full:
        in_specs += [pl.BlockSpec(w_in_bf.shape, lambda i, j: (0, 0)),
                     pl.BlockSpec(conv_w.shape, lambda i, j: (0, 0)),
                     pl.BlockSpec(conv_w_out_bf.shape, lambda i, j: (0, 0))]
        args += [w_in_bf, conv_w, conv_w_out_bf]
        out_specs = [chunk_spec, tok(d), tok(d)]
        out_shape = [chunk_shape,
                     jax.ShapeDtypeStruct((b, l, d), BF16),
                     jax.ShapeDtypeStruct((b, l, d), BF16)]
    else:
        in_specs += [pl.BlockSpec((d, S5_WIDTH), lambda i, j: (0, 0))]
        args += [w_in_bf]
        out_specs = [chunk_spec]
        out_shape = [chunk_shape]
    return pl.pallas_call(
        functools.partial(_in_proj_kernel, full=full),
        grid=grid, in_specs=in_specs, out_specs=out_specs, out_shape=out_shape,
        scratch_shapes=[pltpu.VMEM((S5_WIDTH // LANES, tm, LANES), F32)],
        compiler_params=pltpu.CompilerParams(
            dimension_semantics=("arbitrary", "arbitrary"), vmem_limit_bytes=VMEM_LIMIT),
        name="in_proj" if full else "in_proj_ctx",
    )(*args)


def _s5_tables(log_dt, a_re, a_im, b_re, b_im, c_re, c_im):
    f32 = F32
    dt = jnp.exp(log_dt.astype(f32))[..., None]
    a_re = a_re.astype(f32)
    a_im = a_im.astype(f32)
    mag = jnp.exp(dt * a_re)
    ab_re = mag * jnp.cos(dt * a_im)
    ab_im = mag * jnp.sin(dt * a_im)
    den = a_re * a_re + a_im * a_im
    x_re = ab_re - 1.0
    f_re = (x_re * a_re + ab_im * a_im) / den
    f_im = (ab_im * a_re - x_re * a_im) / den
    b_re = b_re.astype(f32)
    b_im = b_im.astype(f32)
    bb_re = f_re[..., None] * b_re - f_im[..., None] * b_im
    bb_im = f_re[..., None] * b_im + f_im[..., None] * b_re
    k = jnp.arange(CHUNK + 1, dtype=f32)[None, :, None, None]
    pmag = jnp.exp(k * (dt * a_re)[:, None])
    p_re = pmag * jnp.cos(k * (dt * a_im)[:, None])
    p_im = pmag * jnp.sin(k * (dt * a_im)[:, None])
    pb_re = p_re[..., None] * bb_re[:, None] - p_im[..., None] * bb_im[:, None]
    pb_im = p_re[..., None] * bb_im[:, None] + p_im[..., None] * bb_re[:, None]
    c_re = c_re.astype(f32)[:, None]
    c_im = c_im.astype(f32)[:, None]
    cp_re = c_re * p_re[:, :, :, None, :] - c_im * p_im[:, :, :, None, :]
    cp_im = -(c_re * p_im[:, :, :, None, :] + c_im * p_re[:, :, :, None, :])
    return dict(p_re=p_re, p_im=p_im, pb_re=pb_re, pb_im=pb_im, cp_re=cp_re, cp_im=cp_im,
                bb_re=bb_re, bb_im=bb_im)


def _lag_kernels(t):
    g, n, c = S5_GROUPS, S5_STATE, S5_GROUP_CH
    k = CHUNK + 1
    lhs = jnp.concatenate([t['cp_re'], t['cp_im']], -1)
    lhs = lhs.transpose(0, 2, 1, 3, 4).reshape(2 * g, k * c, 2 * n)
    rhs = jnp.concatenate([t['bb_re'], t['bb_im']], -2).reshape(2 * g, 2 * n, c)
    out = jnp.einsum('bmn,bnc->bmc', lhs, rhs, precision=HI)
    out = out.reshape(2, g, k, c, c).transpose(0, 2, 1, 4, 3)
    return out[0], out[1]


def _s5_operators(t, s5_d):
    q = CHUNK
    g, n, c = S5_GROUPS, S5_STATE, S5_GROUP_CH
    kern_f, kern_b = _lag_kernels(t)
    k0 = kern_f[0] + kern_b[0] + s5_d.astype(F32)[:, :, None] * jnp.eye(c, dtype=F32)[None]
    kc = jnp.concatenate([kern_b[1:q][::-1], k0[None], kern_f[1:q]], 0)
    kct = kc.transpose(1, 2, 0, 3)
    m_intra = jnp.stack([kct[:, :, q - 1 - i:2 * q - 1 - i, :] for i in range(q)], 1)
    m_intra = m_intra.reshape(g, q * c, q * c)
    w_st = jnp.stack([t['pb_re'][0, :q][::-1], t['pb_im'][0, :q][::-1],
                      t['pb_re'][1, :q], t['pb_im'][1, :q]], 0)
    w_st = w_st.transpose(2, 1, 4, 0, 3).reshape(g, q * c, 4, n)
    w_out = jnp.stack([t['cp_re'][0, 1:], t['cp_im'][0, 1:],
                       t['cp_re'][1, 1:][::-1], t['cp_im'][1, 1:][::-1]], 0)
    w_out = w_out.transpose(2, 0, 4, 1, 3).reshape(g, 4, n, q * c)
    np_ = N_PAIRS
    w_st = w_st.astype(BF16).reshape(np_, 2, q * c, 4, n)
    ws_pair = jnp.concatenate([jnp.pad(w_st[:, 0], ((0, 0), (0, 0), (0, 0), (0, n))),
                               jnp.pad(w_st[:, 1], ((0, 0), (0, 0), (0, 0), (n, 0)))], 1)
    ws_pair = ws_pair.reshape(np_, PAIR_W, 4 * 2 * n)
    w_out = w_out.astype(BF16).reshape(np_, 2, 4, n, q * c)
    wo_pair = jnp.stack([jnp.pad(w_out[:, 0], ((0, 0), (0, 0), (0, 0), (0, q * c))),
                         jnp.pad(w_out[:, 1], ((0, 0), (0, 0), (0, 0), (q * c, 0)))], 2)
    wo_pair = wo_pair.reshape(np_, 4 * 2 * n, PAIR_W)
    tab = _chunk_power_table(t).reshape(2 * TAB_ROWS, np_, 2 * n).transpose(1, 0, 2)
    return m_intra.astype(BF16), ws_pair, wo_pair, tab


def _chunk_power_table(t):
    def cmul(x, y):
        return x[0] * y[0] - x[1] * y[1], x[0] * y[1] + x[1] * y[0]
    p1 = (t['p_re'][:, CHUNK], t['p_im'][:, CHUNK])
    p2 = cmul(p1, p1)
    p4 = cmul(p2, p2)
    p8 = cmul(p4, p4)
    pr = [(jnp.ones_like(p1[0]), jnp.zeros_like(p1[0]))]
    for _ in range(SUBLANES - 1):
        pr.append(cmul(pr[-1], p1))
    pr_re = jnp.stack([p[0] for p in pr], 0)
    pr_im = jnp.stack([p[1] for p in pr], 0)
    pw = jnp.stack([p1[0], p1[1], p2[0], p2[1], p4[0], p4[1], p8[0], p8[1]], 0)
    return jnp.concatenate([pr_re[:, 0], pr_im[:, 0], pw[:, 0],
                            pr_re[::-1, 1], pr_im[::-1, 1], pw[:, 1]], 0)


def _s5_scan_kernel(uc_ref, *rest, batch, emit_y):
    if emit_y:
        mi_ref, ws_ref, wo_ref, tab_ref, s0_ref, y_ref, fin_ref, s_scr, in_scr = rest
    else:
        ws_ref, tab_ref, s0_ref, fin_ref, s_scr = rest
    rows = uc_ref.shape[0]
    chunks = rows // batch
    n_tiles = chunks // SUBLANES
    u = uc_ref[...]
    s_scr[...] = _dot(u, ws_ref[...])
    row = lax.broadcasted_iota(jnp.int32, (SUBLANES, LANES), 0)

    def tile_scan(r0, backward, c_re, c_im):
        base = TAB_ROWS if backward else 0
        col = 2 * LANES if backward else 0
        rs = pl.ds(r0, SUBLANES)

        def shift(z, k):
            if backward:
                return jnp.where(row < SUBLANES - k, pltpu.roll(z, SUBLANES - k, 0), 0.0)
            return jnp.where(row >= k, pltpu.roll(z, k, 0), 0.0)

        z_re = s_scr[rs, col:col + LANES]
        z_im = s_scr[rs, col + LANES:col + 2 * LANES]
        for k, t in ((1, TAB_POW), (2, TAB_POW + 2), (4, TAB_POW + 4)):
            a_re = tab_ref[base + t:base + t + 1, :]
            a_im = tab_ref[base + t + 1:base + t + 2, :]
            sh_re = shift(z_re, k)
            sh_im = shift(z_im, k)
            z_re, z_im = z_re + (a_re * sh_re - a_im * sh_im), z_im + (a_re * sh_im + a_im * sh_re)
        pr_re = tab_ref[base:base + SUBLANES, :]
        pr_im = tab_ref[base + SUBLANES:base + 2 * SUBLANES, :]
        if emit_y:
            in_scr[rs, col:col + LANES] = pr_re * c_re - pr_im * c_im + shift(z_re, 1)
            in_scr[rs, col + LANES:col + 2 * LANES] = pr_re * c_im + pr_im * c_re + shift(z_im, 1)
        last = 0 if backward else SUBLANES - 1
        l_re = jnp.broadcast_to(z_re[last:last + 1, :], (SUBLANES, LANES))
        l_im = jnp.broadcast_to(z_im[last:last + 1, :], (SUBLANES, LANES))
        p8_re = tab_ref[base + TAB_POW + 6:base + TAB_POW + 7, :]
        p8_im = tab_ref[base + TAB_POW + 7:base + TAB_POW + 8, :]
        return p8_re * c_re - p8_im * c_im + l_re, p8_re * c_im + p8_im * c_re + l_im

    def body(m, carry):
        out = []
        for b in range(batch):
            cf_re, cf_im, cb_re, cb_im = carry[4 * b:4 * b + 4]
            rf = pl.multiple_of(b * chunks + m * SUBLANES, SUBLANES)
            rb = pl.multiple_of(b * chunks + (n_tiles - 1 - m) * SUBLANES, SUBLANES)
            out += list(tile_scan(rf, False, cf_re, cf_im))
            out += list(tile_scan(rb, True, cb_re, cb_im))
        return tuple(out)

    init = tuple(jnp.broadcast_to(s0_ref[t, b:b + 1, :], (SUBLANES, LANES))
                 for b in range(batch) for t in range(4))
    fin = lax.fori_loop(0, n_tiles, body, init, unroll=min(4, n_tiles))
    fin_ref[...] = jnp.zeros(fin_ref.shape, F32)
    for b in range(batch):
        for t in range(4):
            fin_ref[t, b:b + 1, :] = fin[4 * b + t][0:1, :]
    if emit_y:
        y_intra = jnp.concatenate(
            [_dot(u[:, gl * GROUP_W:(gl + 1) * GROUP_W], mi_ref[gl]) for gl in range(2)], axis=-1)
        y = y_intra + _dot(in_scr[...].astype(BF16), wo_ref[...])
        y_ref[...] = y.astype(y_ref.dtype)


def _s5_scan(uc, mi, ws, wo, tab, s0, batch, emit_y=True):
    rows = uc.shape[0]
    pair = lambda *shape: pl.BlockSpec((None,) + shape, lambda p: (p,) + (0,) * len(shape))
    uc_spec = pl.BlockSpec((rows, PAIR_W), lambda p: (0, p))
    state_spec = pair(4, SUBLANES, LANES)
    state_shape = jax.ShapeDtypeStruct((N_PAIRS, 4, SUBLANES, LANES), F32)
    scratch = [pltpu.VMEM((rows, PAIR_W), F32)]
    if emit_y:
        in_specs = [uc_spec, pl.BlockSpec((2, GROUP_W, GROUP_W), lambda p: (p, 0, 0)),
                    pair(PAIR_W, PAIR_W), pair(PAIR_W, PAIR_W), pair(2 * TAB_ROWS, LANES), state_spec]
        args = (uc, mi, ws, wo, tab, s0)
        out_specs = [uc_spec, state_spec]
        out_shape = [jax.ShapeDtypeStruct((rows, N_PAIRS * PAIR_W), BF16), state_shape]
        scratch = scratch * 2
    else:
        in_specs = [uc_spec, pair(PAIR_W, PAIR_W), pair(2 * TAB_ROWS, LANES), state_spec]
        args = (uc, ws, tab, s0)
        out_specs = state_spec
        out_shape = state_shape
    return pl.pallas_call(
        functools.partial(_s5_scan_kernel, batch=batch, emit_y=emit_y),
        grid=(N_PAIRS,), in_specs=in_specs, out_specs=out_specs, out_shape=out_shape,
        scratch_shapes=scratch,
        compiler_params=pltpu.CompilerParams(
            dimension_semantics=("arbitrary",), vmem_limit_bytes=VMEM_LIMIT),
        name="s5_scan" if emit_y else "s5_scan_ctx",
    )(*args)


def _mix_out_kernel(y_ref, siga_ref, sb_ref, x_ref, remb_ref, cemb_ref, mod_ref, wv_ref, wg_ref, wo_ref,
                    ln1_ref, wr_ref, br_ref, ut_ref, lt_ref,
                    x1_ref, xs_ref, route_ref, len_ref, y_scr):
    _from_chunk_tile(y_ref, y_scr)
    y = jnp.concatenate([y_scr[v] for v in range(S5_WIDTH // LANES)], axis=-1)
    ya = (0.5 * y * (1.0 + jnp.tanh(math.sqrt(2.0 / math.pi) * (y + 0.044715 * (y * y * y))))).astype(BF16)
    out_a = _dot(ya, wv_ref[...]) * _sigmoid(_dot(ya, wg_ref[...]))
    merged = siga_ref[...].astype(F32) * out_a + sb_ref[...].astype(F32)
    mix = _dot(merged.astype(BF16), wo_ref[...])
    xp = _with_positions(x_ref, remb_ref, cemb_ref)
    x1 = _ln(ALPHA * xp + mod_ref[0:1, :] * mix) * ln1_ref[0:1, :] + ln1_ref[1:2, :]
    x1_ref[...] = x1
    h_hi = (_ln(x1) * mod_ref[2:3, :] + mod_ref[1:2, :]).astype(BF16)
    _route_and_sort(h_hi, wr_ref, br_ref, ut_ref, lt_ref, xs_ref, route_ref, len_ref)


def _route_and_sort(h_hi, wr_ref, br_ref, ut_ref, lt_ref, xs_ref, route_ref, len_ref):
    tm = h_hi.shape[0]
    f32 = F32
    nt = (((1,), (1,)), ((), ()))
    lg = lax.dot_general(wr_ref[...], h_hi, nt, preferred_element_type=f32) + br_ref[...]
    rowi = lax.broadcasted_iota(jnp.int32, (LOGIT_ROWS, tm), 0).astype(f32)
    neg = jnp.float32(-jnp.inf)
    big = jnp.float32(LOGIT_ROWS)
    gl = jnp.where(rowi < N_EXPERT_GROUPS, lg, neg)
    gmax = jnp.max(gl, axis=0, keepdims=True)
    g_idx = jnp.min(jnp.where(gl == gmax, rowi, big), axis=0, keepdims=True)
    p_group = 1.0 / jnp.sum(jnp.exp(gl - gmax), axis=0, keepdims=True)
    e_lo = N_EXPERT_GROUPS + g_idx * EXPERTS_PER_GROUP
    el = jnp.where((rowi >= e_lo) & (rowi < e_lo + EXPERTS_PER_GROUP), lg, neg)
    m1 = jnp.max(el, axis=0, keepdims=True)
    i1 = jnp.min(jnp.where(el == m1, rowi, big), axis=0, keepdims=True)
    el2 = jnp.where(rowi == i1, neg, el)
    m2 = jnp.max(el2, axis=0, keepdims=True)
    i2 = jnp.min(jnp.where(el2 == m2, rowi, big), axis=0, keepdims=True)
    r = jnp.exp(m2 - m1)
    w1 = p_group / (1.0 + r)
    w2 = p_group * r / (1.0 + r)
    e1 = i1 - N_EXPERT_GROUPS
    e2 = i2 - N_EXPERT_GROUPS
    rowa = lax.broadcasted_iota(jnp.int32, (ASSIGN_ROWS, tm), 0).astype(f32)
    a12 = jnp.where(rowa == e1, 1.0, 0.0) + jnp.where(rowa == e2 + N_EXPERTS, 1.0, 0.0)
    rank = _dot(a12.astype(BF16), ut_ref[...])
    cnt = jnp.broadcast_to(jnp.sum(a12, axis=1, keepdims=True), (ASSIGN_ROWS, LANES))
    row = lax.broadcasted_iota(jnp.int32, (ASSIGN_ROWS, LANES), 0)
    tot = cnt + pltpu.roll(cnt, N_EXPERTS, 0)
    run = jnp.where(row < N_EXPERTS, jnp.floor((tot + (PIECE - 1)) * (1.0 / PIECE)), 0.0)
    off = PIECE * _dot(lt_ref[...], run.astype(BF16))
    base = jnp.where(row < N_EXPERTS, off, pltpu.roll(off + cnt, N_EXPERTS, 0))
    posmat = a12 * (rank + base[:, 0:1])
    pos1 = jnp.sum(posmat[0:N_EXPERTS, :], axis=0, keepdims=True)
    pos2 = jnp.sum(posmat[N_EXPERTS:2 * N_EXPERTS, :], axis=0, keepdims=True)
    ri = lax.broadcasted_iota(jnp.int32, (xs_ref.shape[0], tm), 0).astype(f32)
    perm = jnp.where((ri == pos1) | (ri == pos2), 1.0, 0.0).astype(BF16)
    xs_ref[...] = _dot(perm, h_hi).astype(xs_ref.dtype)
    rowr = lax.broadcasted_iota(jnp.int32, (SUBLANES, tm), 0)
    rec = jnp.zeros((SUBLANES, tm), f32)
    for col, val in ((REC_W1, w1), (REC_W2, w2), (REC_POS1, pos1), (REC_POS2, pos2)):
        rec = jnp.where(rowr == col, val, rec)
    rec = jnp.concatenate([rec, jnp.zeros((ROUTE_LANES - SUBLANES, tm), f32)], axis=0)
    route_ref[...] = rec.T
    len_ref[...] = run


def _local_rows(tm):
    return TOP_K * tm + N_EXPERTS * PIECE


def _mix_out(y, siga, sb, x, r_emb, c_emb, mod, wv, wg, wo, ln1, wr, br, tm):
    b, l, d = x.shape
    n_tiles = l // tm
    xs_rows = _local_rows(tm)
    tok = lambda w: pl.BlockSpec((None, tm, w), lambda i, j: (j, i, 0))
    whole = lambda a: pl.BlockSpec(a.shape, lambda i, j: (0,) * a.ndim)
    ut = (jnp.arange(tm)[:, None] < jnp.arange(tm)[None, :]).astype(BF16)
    lt = (jnp.arange(ASSIGN_ROWS)[None, :] < jnp.arange(ASSIGN_ROWS)[:, None]).astype(BF16)
    return pl.pallas_call(
        _mix_out_kernel,
        grid=(n_tiles, b),
        in_specs=[pl.BlockSpec((tm // CHUNK, CHUNK * S5_WIDTH), lambda i, j: (j * n_tiles + i, 0)),
                  tok(d), tok(d), tok(d)] + _pos_specs(tm, d) + [
                  pl.BlockSpec((None, 4, d), lambda i, j: (j, 0, 0)),
                  whole(wv), whole(wg), whole(wo), whole(ln1), whole(wr), whole(br),
                  whole(ut), whole(lt)],
        out_specs=[tok(d),
                   pl.BlockSpec((xs_rows, d), lambda i, j: (j * n_tiles + i, 0)),
                   tok(ROUTE_LANES),
                   pl.BlockSpec((None, ASSIGN_ROWS, LANES), lambda i, j: (j * n_tiles + i, 0, 0))],
        out_shape=[jax.ShapeDtypeStruct((b, l, d), F32),
                   jax.ShapeDtypeStruct((b * n_tiles * xs_rows, d), BF16),
                   jax.ShapeDtypeStruct((b, l, ROUTE_LANES), F32),
                   jax.ShapeDtypeStruct((b * n_tiles, ASSIGN_ROWS, LANES), F32)],
        scratch_shapes=[pltpu.VMEM((S5_WIDTH // LANES, tm, LANES), F32)],
        compiler_params=pltpu.CompilerParams(
            dimension_semantics=("arbitrary", "arbitrary"), vmem_limit_bytes=VMEM_LIMIT),
        name="mix_out",
    )(y, siga, sb, x, r_emb, c_emb, mod, wv, wg, wo, ln1, wr, br, ut, lt)


def _piece_copy(src_hbm, src_row, dst, piece, sem):
    return pltpu.make_async_copy(src_hbm.at[pl.ds(pl.multiple_of(src_row, PIECE), PIECE), :],
                                 dst.at[pl.ds(pl.multiple_of(piece * PIECE, PIECE), PIECE), :], sem)


def _issue_pieces(src_hbm, table_ref, first, n_pieces, dst, sem):
    for p in range(n_pieces):
        _piece_copy(src_hbm, table_ref[first + p], dst, p, sem).start()


def _wait_pieces(src_hbm, dst, sem):
    pltpu.make_async_copy(src_hbm.at[pl.ds(0, dst.shape[0]), :], dst, sem).wait()


def _experts_kernel(be_ref, piece_ref, nused_ref, xs_hbm, wg_ref, wu_ref, wd_ref, ys_ref,
                    xs_buf0, xs_buf1, wg_bf, wu_bf, wd_bf, sem):
    i = pl.program_id(0)
    n_used = nused_ref[0]
    per_block = ROW_BLOCK // PIECE

    @pl.when(i == 0)
    def _():
        _issue_pieces(xs_hbm, piece_ref, 0, per_block, xs_buf0, sem.at[0])

    def block(cur, cur_sem, oth, oth_sem):
        @pl.when((i == 0) | (be_ref[i] != be_ref[jnp.maximum(i - 1, 0)]))
        def _():
            wg_bf[...] = wg_ref[...].astype(BF16)
            wu_bf[...] = wu_ref[...].astype(BF16)
            wd_bf[...] = wd_ref[...].astype(BF16)

        _wait_pieces(xs_hbm, cur, cur_sem)
        nxt = jnp.minimum(i + 1, n_used - 1)
        _issue_pieces(xs_hbm, piece_ref, nxt * per_block, per_block, oth, oth_sem)

        xb = cur[...]
        gate = _dot(xb, wg_bf[...])
        up = _dot(xb, wu_bf[...])
        hid = (gate * _sigmoid(gate) * up).astype(BF16)
        ys_ref[...] = _dot(hid, wd_bf[...]).astype(ys_ref.dtype)

        @pl.when(i == n_used - 1)
        def _():
            _wait_pieces(xs_hbm, oth, oth_sem)

    for s, (cur, oth) in enumerate(((xs_buf0, xs_buf1), (xs_buf1, xs_buf0))):
        @pl.when((i < n_used) & (i % 2 == s))
        def _(s=s, cur=cur, oth=oth):
            block(cur, sem.at[s], oth, sem.at[1 - s])

    @pl.when(i >= n_used)
    def _():
        ys_ref[...] = jnp.zeros(ys_ref.shape, ys_ref.dtype)


def _experts(block_e, piece_src, n_used, xs, wg, wu, wd, n_blocks):
    d = xs.shape[1]
    by_expert = lambda i, be, ps, nu: (0, be[i], 0, 0)
    grid_spec = pltpu.PrefetchScalarGridSpec(
        num_scalar_prefetch=3,
        grid=(n_blocks,),
        in_specs=[pl.BlockSpec(memory_space=pl.ANY),
                  pl.BlockSpec((None, None, d, EXPERT_FF), by_expert),
                  pl.BlockSpec((None, None, d, EXPERT_FF), by_expert),
                  pl.BlockSpec((None, None, EXPERT_FF, d), by_expert)],
        out_specs=pl.BlockSpec((ROW_BLOCK, d), lambda i, be, ps, nu: (i, 0)),
        scratch_shapes=[pltpu.VMEM((ROW_BLOCK, d), BF16), pltpu.VMEM((ROW_BLOCK, d), BF16),
                        pltpu.VMEM((d, EXPERT_FF), BF16), pltpu.VMEM((d, EXPERT_FF), BF16),
                        pltpu.VMEM((EXPERT_FF, d), BF16), pltpu.SemaphoreType.DMA((2,))],
    )
    return pl.pallas_call(
        _experts_kernel,
        grid_spec=grid_spec,
        out_shape=jax.ShapeDtypeStruct((n_blocks * ROW_BLOCK, d), BF16),
        compiler_params=pltpu.CompilerParams(
            dimension_semantics=("arbitrary",), vmem_limit_bytes=VMEM_LIMIT),
        name="experts",
    )(block_e, piece_src, n_used, xs, wg, wu, wd)


def _combine_kernel(piece_ref, ys_hbm, x1_ref, route_ref, mod_ref, ln2_ref, o_ref, buf, sem):
    i = pl.program_id(0)
    n = pl.num_programs(0)
    slot = i % 2
    rows = buf.shape[1]
    per_tile = rows // PIECE

    def issue(tile, s):
        def body(p, _):
            _piece_copy(ys_hbm, piece_ref[tile * per_tile + p], buf.at[s], p, sem.at[s]).start()
            return 0
        lax.fori_loop(0, per_tile, body, 0)

    @pl.when(i == 0)
    def _():
        issue(0, 0)

    _wait_pieces(ys_hbm, buf.at[slot], sem.at[slot])

    @pl.when(i + 1 < n)
    def _():
        issue(i + 1, 1 - slot)

    tm = x1_ref.shape[0]
    lane = lax.broadcasted_iota(jnp.int32, (tm, rows), 1).astype(F32)
    rec = lambda col: route_ref[:, col:col + 1]
    sel = (jnp.where(lane == rec(REC_POS1), rec(REC_W1), 0.0)
           + jnp.where(lane == rec(REC_POS2), rec(REC_W2), 0.0)).astype(BF16)
    moe = _dot(sel, buf[slot])
    z = ALPHA * x1_ref[...] + mod_ref[0:1, :] * moe
    o_ref[...] = _ln(z) * ln2_ref[0:1, :] + ln2_ref[1:2, :]


def _combine(piece_glob, ys, x1, route, mod, ln2, tm, tiles_per_batch):
    t, d = x1.shape
    grid_spec = pltpu.PrefetchScalarGridSpec(
        num_scalar_prefetch=1,
        grid=(t // tm,),
        in_specs=[pl.BlockSpec(memory_space=pl.ANY),
                  pl.BlockSpec((tm, d), lambda i, pg: (i, 0)),
                  pl.BlockSpec((tm, ROUTE_LANES), lambda i, pg: (i, 0)),
                  pl.BlockSpec((None, SUBLANES, d), lambda i, pg: (i // tiles_per_batch, 0, 0)),
                  pl.BlockSpec((2, d), lambda i, pg: (0, 0))],
        out_specs=pl.BlockSpec((tm, d), lambda i, pg: (i, 0)),
        scratch_shapes=[pltpu.VMEM((2, _local_rows(tm), d), BF16), pltpu.SemaphoreType.DMA((2,))],
    )
    return pl.pallas_call(
        _combine_kernel,
        grid_spec=grid_spec,
        out_shape=jax.ShapeDtypeStruct((t, d), F32),
        compiler_params=pltpu.CompilerParams(
            dimension_semantics=("arbitrary",), vmem_limit_bytes=VMEM_LIMIT),
        name="combine",
    )(piece_glob, ys, x1, route, mod, ln2)


def _sincos_2d(rows, cols, dim):
    q = dim // 4
    omega = 1.0 / (POS_BASE ** (jnp.arange(q, dtype=F32) / q))
    r = jnp.arange(rows, dtype=F32)[:, None] * omega
    cl = jnp.arange(cols, dtype=F32)[:, None] * omega
    r_emb = jnp.concatenate([jnp.sin(r), jnp.cos(r)], -1)
    c_emb = jnp.concatenate([jnp.sin(cl), jnp.cos(cl)], -1)
    return r_emb, c_emb


def _routing_tables(run_pieces, xs_rows, n_blocks):
    i32 = jnp.int32
    n_tiles = run_pieces.shape[0]
    ppb = ROW_BLOCK // PIECE
    loc_start = jnp.cumsum(run_pieces, axis=1) - run_pieces
    seg_tot = jnp.sum(run_pieces, axis=0)
    seg_pad = (seg_tot + ppb - 1) // ppb * ppb
    seg_end = jnp.cumsum(seg_pad)
    seg_start = seg_end - seg_pad
    run_t = run_pieces.T
    glob_start = seg_start[:, None] + jnp.cumsum(run_t, axis=1) - run_t
    n_used = (seg_end[-1] // ppb).astype(i32)
    blk = jnp.minimum(jnp.arange(n_blocks, dtype=i32), n_used - 1)
    block_e = jnp.minimum(jnp.sum((seg_end[None, :] <= (blk * ppb)[:, None]).astype(i32), axis=1),
                          N_EXPERTS - 1).astype(i32)
    lpt = xs_rows // PIECE
    starts = glob_start.reshape(-1)
    lens = run_t.reshape(-1)
    src0 = (jnp.arange(n_tiles, dtype=i32)[None, :] * lpt + loc_start.T).reshape(-1)
    p = jnp.arange(n_blocks * ppb, dtype=i32)
    within = p[:, None] - starts[None, :]
    hit = (within >= 0) & (within < lens[None, :])
    piece_src = jnp.sum(jnp.where(hit, (src0[None, :] + within) * PIECE, 0), axis=1).astype(i32)
    s = jnp.arange(lpt, dtype=i32)
    loc_within = s[None, :, None] - loc_start[:, None, :]
    hit = (loc_within >= 0) & (loc_within < run_pieces[:, None, :])
    piece_glob = jnp.sum(jnp.where(hit, (glob_start.T[:, None, :] + loc_within) * PIECE, 0), axis=2)
    return block_e, piece_src, piece_glob.astype(i32).reshape(-1), n_used.reshape(1)


def kernel(x, c, ctx, c_ctx, w_ada, b_ada, w_in, s5_log_dt_f, s5_a_re_f, s5_a_im_f, s5_b_re_f, s5_b_im_f, s5_c_re_f, s5_c_im_f, s5_log_dt_b, s5_a_re_b, s5_a_im_b, s5_b_re_b, s5_b_im_b, s5_c_re_b, s5_c_im_b, s5_d, s5_w_glu_val, s5_w_glu_gate, conv_w, conv_w_out, w_o, ln1_g, ln1_b, router_w_group, router_b_group, router_w_expert, router_b_expert, exp_w_gate, exp_w_up, exp_w_down, ln2_g, ln2_b):
    b, l, d = x.shape
    lc = ctx.shape[1]
    assert d == D_MODEL and b < SUBLANES and w_ada.shape[0] == DEPTH
    assert l % (SUBLANES * CHUNK) == 0 and lc % (SUBLANES * CHUNK) == 0 and l % GRID_W == 0
    t = b * l
    tm = min(TOKEN_TILE, l)
    tmc = min(TOKEN_TILE, lc)

    cc = jnp.concatenate([c, c_ctx[None, :], jnp.zeros((SUBLANES - b - 1, d), F32)], 0)
    mods = _mods(cc, w_ada[0], b_ada[0])
    sh1, sc1, g1, sh2, sc2, g2 = jnp.split(mods, 6, axis=-1)
    mod_a = jnp.stack([sh1[:b], 1.0 + sc1[:b]], 1)
    mod_ctx = jnp.broadcast_to(jnp.stack([sh1[b], 1.0 + sc1[b]], 0)[None], (b, 2, d))
    mod_c = jnp.stack([g1[:b], sh2[:b], 1.0 + sc2[:b], jnp.zeros((b, d), F32)], 1)
    mod_f = jnp.concatenate([g2[:b, None, :], jnp.zeros((b, SUBLANES - 1, d), F32)], 1)

    w_in_bf = w_in[0].astype(BF16)
    both = lambda fwd, bwd: jnp.concatenate([fwd, bwd], 0)
    s5_tab = _s5_tables(both(s5_log_dt_f, s5_log_dt_b), both(s5_a_re_f, s5_a_re_b), both(s5_a_im_f, s5_a_im_b),
                        both(s5_b_re_f, s5_b_re_b), both(s5_b_im_f, s5_b_im_b),
                        both(s5_c_re_f, s5_c_re_b), both(s5_c_im_f, s5_c_im_b))
    mi, ws, wo_s5, tab = _s5_operators(s5_tab, s5_d[0])

    (uc_ctx,) = _in_proj(ctx, jnp.zeros((lc // GRID_W, d // 2), F32), jnp.zeros((GRID_W, d // 2), F32),
                         mod_ctx, w_in_bf, None, None, tmc, False)
    zero_state = jnp.zeros((N_PAIRS, 4, SUBLANES, LANES), F32)
    s0 = _s5_scan(uc_ctx, None, ws, None, tab, zero_state, b, emit_y=False)

    r_emb, c_emb = _sincos_2d(l // GRID_W, GRID_W, d)
    uc, siga, sb = _in_proj(x, r_emb, c_emb, mod_a, w_in_bf, conv_w[0], conv_w_out[0].astype(BF16), tm, True)
    y, _ = _s5_scan(uc, mi, ws, wo_s5, tab, s0, b)

    wr = jnp.concatenate([router_w_group[0], router_w_expert[0],
                          jnp.zeros((d, LOGIT_ROWS - N_EXPERT_GROUPS - N_EXPERTS), F32)], 1).T.astype(BF16)
    br = jnp.concatenate([router_b_group[0], router_b_expert[0],
                          jnp.zeros((LOGIT_ROWS - N_EXPERT_GROUPS - N_EXPERTS,), F32)])[:, None]
    ln1 = jnp.stack([ln1_g[0], ln1_b[0]], 0)
    x1, xs, route, run_len = _mix_out(y, siga, sb, x, r_emb, c_emb, mod_c,
                                      s5_w_glu_val[0].astype(BF16), s5_w_glu_gate[0].astype(BF16),
                                      w_o[0].astype(BF16), ln1, wr, br, tm)

    x1 = x1.reshape(t, d)
    route = route.reshape(t, ROUTE_LANES)
    n_tiles = t // tm
    xs_rows = _local_rows(tm)
    run_pieces = run_len[:, :N_EXPERTS, 0].astype(jnp.int32)
    max_rows = t * TOP_K + n_tiles * N_EXPERTS * (PIECE - 1) + N_EXPERTS * (ROW_BLOCK - 1)
    n_blocks = -(-max_rows // ROW_BLOCK)
    block_e, piece_src, piece_glob, n_used = _routing_tables(run_pieces, xs_rows, n_blocks)
    ys = _experts(block_e, piece_src, n_used, xs, exp_w_gate, exp_w_up, exp_w_down, n_blocks)
    ln2 = jnp.stack([ln2_g[0], ln2_b[0]], 0)
    out = _combine(piece_glob, ys, x1, route, mod_f, ln2, tm, l // tm)
    return out.reshape(b, l, d)
```

```python
import functools
import math

import jax
import jax.numpy as jnp
from jax import lax
from jax.experimental import pallas as pl
from jax.experimental.pallas import tpu as pltpu

F32 = jnp.float32
BF16 = jnp.bfloat16
HI = lax.Precision.HIGHEST

D_MODEL = 1024
GRID_W = 64
S5_WIDTH = 512
S5_GROUP_CH = 16
S5_GROUPS = S5_WIDTH // S5_GROUP_CH
S5_STATE = 64
CONV_WIDTH = 512
N_EXPERT_GROUPS = 4
EXPERTS_PER_GROUP = 8
N_EXPERTS = N_EXPERT_GROUPS * EXPERTS_PER_GROUP
EXPERT_FF = 512
TOP_K = 2
DEPTH = 1
ALPHA = (2.0 * DEPTH) ** 0.25
LN_EPS = 1e-6
POS_BASE = 10000.0

LANES = 128
SUBLANES = 8
CHUNK = 16
GROUP_W = CHUNK * S5_GROUP_CH
PAIR_W = 2 * GROUP_W
N_PAIRS = S5_GROUPS // 2
TOK_PER_VREG = LANES // S5_GROUP_CH
TAB_ROWS = 24
TAB_POW = 2 * SUBLANES
MODS_COLS = 1536
ROUTE_LANES = 128
LOGIT_ROWS = 48
ASSIGN_ROWS = TOP_K * N_EXPERTS
REC_W1, REC_W2, REC_POS1, REC_POS2 = range(4)
TOKEN_TILE = 512
ROW_BLOCK = 512
PIECE = 16
VMEM_LIMIT = 56 * 1024 * 1024


def _ln(x):
    mu = jnp.mean(x, axis=-1, keepdims=True)
    xc = x - mu
    var = jnp.mean(xc * xc, axis=-1, keepdims=True)
    return xc * lax.rsqrt(var + LN_EPS)


def _sigmoid(x):
    return 0.5 * (jnp.tanh(0.5 * x) + 1.0)


def _dot(a, b):
    return jnp.dot(a, b, preferred_element_type=F32)


def _mods_kernel(c_ref, w_ref, b_ref, o_ref):
    c = c_ref[...]
    a = c * _sigmoid(c)
    o_ref[...] = jnp.dot(a, w_ref[...], precision=HI, preferred_element_type=F32) + b_ref[...]


def _mods(cc, w_ada, b_ada):
    n = w_ada.shape[1]
    nb = MODS_COLS
    return pl.pallas_call(
        _mods_kernel,
        grid=(n // nb,),
        in_specs=[pl.BlockSpec((SUBLANES, D_MODEL), lambda i: (0, 0)),
                  pl.BlockSpec((D_MODEL, nb), lambda i: (0, i)),
                  pl.BlockSpec((1, nb), lambda i: (0, i))],
        out_specs=pl.BlockSpec((SUBLANES, nb), lambda i: (0, i)),
        out_shape=jax.ShapeDtypeStruct((SUBLANES, n), F32),
        compiler_params=pltpu.CompilerParams(vmem_limit_bytes=VMEM_LIMIT),
        name="mods",
    )(cc, w_ada, b_ada.reshape(1, n))


def _slot_masks(rows):
    slot = lax.broadcasted_iota(jnp.int32, (rows, LANES), 1) // S5_GROUP_CH
    return [slot == s for s in range(TOK_PER_VREG)]


def _to_chunk_tile(u_scr, uc_ref):
    nch = uc_ref.shape[0]
    masks = _slot_masks(nch)
    for qh in range(CHUNK // TOK_PER_VREG):
        for v in range(S5_WIDTH // LANES):
            src = [u_scr[v, pl.ds(qh * TOK_PER_VREG + s, nch, stride=CHUNK), :] for s in range(TOK_PER_VREG)]
            for i in range(TOK_PER_VREG):
                acc = None
                for s in range(TOK_PER_VREG):
                    shift = ((s - i) * S5_GROUP_CH) % LANES
                    piece = pltpu.roll(src[s], shift, 1) if shift else src[s]
                    acc = piece if acc is None else jnp.where(masks[s], piece, acc)
                lo = (v * TOK_PER_VREG + i) * GROUP_W + qh * LANES
                uc_ref[:, lo:lo + LANES] = acc.astype(uc_ref.dtype)


def _from_chunk_tile(yc_ref, y_scr):
    nch = yc_ref.shape[0]
    masks = _slot_masks(nch)
    for qh in range(CHUNK // TOK_PER_VREG):
        for v in range(S5_WIDTH // LANES):
            src = []
            for i in range(TOK_PER_VREG):
                lo = (v * TOK_PER_VREG + i) * GROUP_W + qh * LANES
                src.append(yc_ref[:, lo:lo + LANES].astype(F32))
            for s in range(TOK_PER_VREG):
                acc = None
                for i in range(TOK_PER_VREG):
                    shift = ((i - s) * S5_GROUP_CH) % LANES
                    piece = pltpu.roll(src[i], shift, 1) if shift else src[i]
                    acc = piece if acc is None else jnp.where(masks[i], piece, acc)
                y_scr[v, pl.ds(qh * TOK_PER_VREG + s, nch, stride=CHUNK), :] = acc


def _with_positions(x_ref, remb_ref, cemb_ref):
    c = cemb_ref[...]
    slabs = []
    for j in range(x_ref.shape[0] // GRID_W):
        r = jnp.broadcast_to(remb_ref[j:j + 1, :], c.shape)
        slabs.append(x_ref[j * GRID_W:(j + 1) * GRID_W, :] + jnp.concatenate([r, c], axis=-1))
    return jnp.concatenate(slabs, axis=0)


def _in_proj_kernel(x_ref, remb_ref, cemb_ref, mod_ref, w_ref, *rest, full):
    if full:
        cw_ref, cwo_ref, uc_ref, siga_ref, sb_ref, u_scr = rest
    else:
        uc_ref, u_scr = rest
    xp = _with_positions(x_ref, remb_ref, cemb_ref)
    h = (_ln(xp) * mod_ref[1:2, :] + mod_ref[0:1, :]).astype(BF16)
    o1 = S5_WIDTH
    o2, o3, o4 = o1 + CONV_WIDTH, o1 + 2 * CONV_WIDTH, o1 + 3 * CONV_WIDTH
    o5 = o4 + D_MODEL
    u = _dot(h, w_ref[:, 0:o1])
    for v in range(S5_WIDTH // LANES):
        u_scr[v] = u[:, v * LANES:(v + 1) * LANES]
    _to_chunk_tile(u_scr, uc_ref)
    if not full:
        return
    z_b = _dot(h, w_ref[:, o1:o2])
    gate_c = _dot(h, w_ref[:, o3:o4])
    p = gate_c * z_b
    tm = p.shape[0]
    col = lax.broadcasted_iota(jnp.int32, (tm, 1), 0) % GRID_W
    prev = jnp.where(col == 0, 0.0, pltpu.roll(p, 1, 0))
    nxt = jnp.where(col == GRID_W - 1, 0.0, pltpu.roll(p, tm - 1, 0))
    v = cw_ref[0:1, :] * prev + cw_ref[1:2, :] * p + cw_ref[2:3, :] * nxt
    gate_b = _dot(h, w_ref[:, o2:o3])
    out_b = _dot((gate_b * v).astype(BF16), cwo_ref[...])
    merge_b = _dot(h, w_ref[:, o5:])
    sb_ref[...] = (_sigmoid(merge_b) * out_b).astype(sb_ref.dtype)
    merge_a = _dot(h, w_ref[:, o4:o5])
    siga_ref[...] = _sigmoid(merge_a).astype(siga_ref.dtype)


def _pos_specs(tm, d):
    return [pl.BlockSpec((tm // GRID_W, d // 2), lambda i, j: (i, 0)),
            pl.BlockSpec((GRID_W, d // 2), lambda i, j: (0, 0))]


def _in_proj(x, r_emb, c_emb, mod, w_in_bf, conv_w, conv_w_out_bf, tm, full):
    b, l, d = x.shape
    n_tiles = l // tm
    grid = (n_tiles, b)
    tok = lambda w: pl.BlockSpec((None, tm, w), lambda i, j: (j, i, 0))
    chunk_spec = pl.BlockSpec((tm // CHUNK, CHUNK * S5_WIDTH), lambda i, j: (j * n_tiles + i, 0))
    chunk_shape = jax.ShapeDtypeStruct((b * l // CHUNK, CHUNK * S5_WIDTH), BF16)
    in_specs = [tok(d)] + _pos_specs(tm, d) + [pl.BlockSpec((None, 2, d), lambda i, j: (j, 0, 0))]
    args = [x, r_emb, c_emb, mod]
    if full:
        in_specs += [pl.BlockSpec(w_in_bf.shape, lambda i, j: (0, 0)),
                     pl.BlockSpec(conv_w.shape, lambda i, j: (0, 0)),
                     pl.BlockSpec(conv_w_out_bf.shape, lambda i, j: (0, 0))]
        args += [w_in_bf, conv_w, conv_w_out_bf]
        out_specs = [chunk_spec, tok(d), tok(d)]
        out_shape = [chunk_shape,
                     jax.ShapeDtypeStruct((b, l, d), BF16),
                     jax.ShapeDtypeStruct((b, l, d), BF16)]
    else:
        in_specs += [pl.BlockSpec((d, S5_WIDTH), lambda i, j: (0, 0))]
        args += [w_in_bf]
        out_specs = [chunk_spec]
        out_shape = [chunk_shape]
    return pl.pallas_call(
        functools.partial(_in_proj_kernel, full=full),
        grid=grid, in_specs=in_specs, out_specs=out_specs, out_shape=out_shape,
        scratch_shapes=[pltpu.VMEM((S5_WIDTH // LANES, tm, LANES), F32)],
        compiler_params=pltpu.CompilerParams(
            dimension_semantics=("arbitrary", "arbitrary"), vmem_limit_bytes=VMEM_LIMIT),
        name="in_proj" if full else "in_proj_ctx",
    )(*args)


def _s5_tables(log_dt, a_re, a_im, b_re, b_im, c_re, c_im):
    f32 = F32
    dt = jnp.exp(log_dt.astype(f32))[..., None]
    a_re = a_re.astype(f32)
    a_im = a_im.astype(f32)
    mag = jnp.exp(dt * a_re)
    ab_re = mag * jnp.cos(dt * a_im)
    ab_im = mag * jnp.sin(dt * a_im)
    den = a_re * a_re + a_im * a_im
    x_re = ab_re - 1.0
    f_re = (x_re * a_re + ab_im * a_im) / den
    f_im = (ab_im * a_re - x_re * a_im) / den
    b_re = b_re.astype(f32)
    b_im = b_im.astype(f32)
    bb_re = f_re[..., None] * b_re - f_im[..., None] * b_im
    bb_im = f_re[..., None] * b_im + f_im[..., None] * b_re
    k = jnp.arange(CHUNK + 1, dtype=f32)[None, :, None, None]
    pmag = jnp.exp(k * (dt * a_re)[:, None])
    p_re = pmag * jnp.cos(k * (dt * a_im)[:, None])
    p_im = pmag * jnp.sin(k * (dt * a_im)[:, None])
    pb_re = p_re[..., None] * bb_re[:, None] - p_im[..., None] * bb_im[:, None]
    pb_im = p_re[..., None] * bb_im[:, None] + p_im[..., None] * bb_re[:, None]
    c_re = c_re.astype(f32)[:, None]
    c_im = c_im.astype(f32)[:, None]
    cp_re = c_re * p_re[:, :, :, None, :] - c_im * p_im[:, :, :, None, :]
    cp_im = -(c_re * p_im[:, :, :, None, :] + c_im * p_re[:, :, :, None, :])
    return dict(p_re=p_re, p_im=p_im, pb_re=pb_re, pb_im=pb_im, cp_re=cp_re, cp_im=cp_im,
                bb_re=bb_re, bb_im=bb_im)


def _lag_kernels(t):
    g, n, c = S5_GROUPS, S5_STATE, S5_GROUP_CH
    k = CHUNK + 1
    lhs = jnp.concatenate([t['cp_re'], t['cp_im']], -1)
    lhs = lhs.transpose(0, 2, 1, 3, 4).reshape(2 * g, k * c, 2 * n)
    rhs = jnp.concatenate([t['bb_re'], t['bb_im']], -2).reshape(2 * g, 2 * n, c)
    out = jnp.einsum('bmn,bnc->bmc', lhs, rhs, precision=HI)
    out = out.reshape(2, g, k, c, c).transpose(0, 2, 1, 4, 3)
    return out[0], out[1]


def _s5_operators(t, s5_d):
    q = CHUNK
    g, n, c = S5_GROUPS, S5_STATE, S5_GROUP_CH
    kern_f, kern_b = _lag_kernels(t)
    k0 = kern_f[0] + kern_b[0] + s5_d.astype(F32)[:, :, None] * jnp.eye(c, dtype=F32)[None]
    kc = jnp.concatenate([kern_b[1:q][::-1], k0[None], kern_f[1:q]], 0)
    kct = kc.transpose(1, 2, 0, 3)
    m_intra = jnp.stack([kct[:, :, q - 1 - i:2 * q - 1 - i, :] for i in range(q)], 1)
    m_intra = m_intra.reshape(g, q * c, q * c)
    w_st = jnp.stack([t['pb_re'][0, :q][::-1], t['pb_im'][0, :q][::-1],
                      t['pb_re'][1, :q], t['pb_im'][1, :q]], 0)
    w_st = w_st.transpose(2, 1, 4, 0, 3).reshape(g, q * c, 4, n)
    w_out = jnp.stack([t['cp_re'][0, 1:], t['cp_im'][0, 1:],
                       t['cp_re'][1, 1:][::-1], t['cp_im'][1, 1:][::-1]], 0)
    w_out = w_out.transpose(2, 0, 4, 1, 3).reshape(g, 4, n, q * c)
    np_ = N_PAIRS
    w_st = w_st.astype(BF16).reshape(np_, 2, q * c, 4, n)
    ws_pair = jnp.concatenate([jnp.pad(w_st[:, 0], ((0, 0), (0, 0), (0, 0), (0, n))),
                               jnp.pad(w_st[:, 1], ((0, 0), (0, 0), (0, 0), (n, 0)))], 1)
    ws_pair = ws_pair.reshape(np_, PAIR_W, 4 * 2 * n)
    w_out = w_out.astype(BF16).reshape(np_, 2, 4, n, q * c)
    wo_pair = jnp.stack([jnp.pad(w_out[:, 0], ((0, 0), (0, 0), (0, 0), (0, q * c))),
                         jnp.pad(w_out[:, 1], ((0, 0), (0, 0), (0, 0), (q * c, 0)))], 2)
    wo_pair = wo_pair.reshape(np_, 4 * 2 * n, PAIR_W)
    tab = _chunk_power_table(t).reshape(2 * TAB_ROWS, np_, 2 * n).transpose(1, 0, 2)
    return m_intra.astype(BF16), ws_pair, wo_pair, tab


def _chunk_power_table(t):
    def cmul(x, y):
        return x[0] * y[0] - x[1] * y[1], x[0] * y[1] + x[1] * y[0]
    p1 = (t['p_re'][:, CHUNK], t['p_im'][:, CHUNK])
    p2 = cmul(p1, p1)
    p4 = cmul(p2, p2)
    p8 = cmul(p4, p4)
    pr = [(jnp.ones_like(p1[0]), jnp.zeros_like(p1[0]))]
    for _ in range(SUBLANES - 1):
        pr.append(cmul(pr[-1], p1))
    pr_re = jnp.stack([p[0] for p in pr], 0)
    pr_im = jnp.stack([p[1] for p in pr], 0)
    pw = jnp.stack([p1[0], p1[1], p2[0], p2[1], p4[0], p4[1], p8[0], p8[1]], 0)
    return jnp.concatenate([pr_re[:, 0], pr_im[:, 0], pw[:, 0],
                            pr_re[::-1, 1], pr_im[::-1, 1], pw[:, 1]], 0)


def _s5_scan_kernel(uc_ref, *rest, batch, emit_y):
    if emit_y:
        mi_ref, ws_ref, wo_ref, tab_ref, s0_ref, y_ref, fin_ref, s_scr, in_scr = rest
    else:
        ws_ref, tab_ref, s0_ref, fin_ref, s_scr = rest
    rows = uc_ref.shape[0]
    chunks = rows // batch
    n_tiles = chunks // SUBLANES
    u = uc_ref[...]
    s_scr[...] = _dot(u, ws_ref[...])
    row = lax.broadcasted_iota(jnp.int32, (SUBLANES, LANES), 0)

    def tile_scan(r0, backward, c_re, c_im):
        base = TAB_ROWS if backward else 0
        col = 2 * LANES if backward else 0
        rs = pl.ds(r0, SUBLANES)

        def shift(z, k):
            if backward:
                return jnp.where(row < SUBLANES - k, pltpu.roll(z, SUBLANES - k, 0), 0.0)
            return jnp.where(row >= k, pltpu.roll(z, k, 0), 0.0)

        z_re = s_scr[rs, col:col + LANES]
        z_im = s_scr[rs, col + LANES:col + 2 * LANES]
        for k, t in ((1, TAB_POW), (2, TAB_POW + 2), (4, TAB_POW + 4)):
            a_re = tab_ref[base + t:base + t + 1, :]
            a_im = tab_ref[base + t + 1:base + t + 2, :]
            sh_re = shift(z_re, k)
            sh_im = shift(z_im, k)
            z_re, z_im = z_re + (a_re * sh_re - a_im * sh_im), z_im + (a_re * sh_im + a_im * sh_re)
        pr_re = tab_ref[base:base + SUBLANES, :]
        pr_im = tab_ref[base + SUBLANES:base + 2 * SUBLANES, :]
        if emit_y:
            in_scr[rs, col:col + LANES] = pr_re * c_re - pr_im * c_im + shift(z_re, 1)
            in_scr[rs, col + LANES:col + 2 * LANES] = pr_re * c_im + pr_im * c_re + shift(z_im, 1)
        last = 0 if backward else SUBLANES - 1
        l_re = jnp.broadcast_to(z_re[last:last + 1, :], (SUBLANES, LANES))
        l_im = jnp.broadcast_to(z_im[last:last + 1, :], (SUBLANES, LANES))
        p8_re = tab_ref[base + TAB_POW + 6:base + TAB_POW + 7, :]
        p8_im = tab_ref[base + TAB_POW + 7:base + TAB_POW + 8, :]
        return p8_re * c_re - p8_im * c_im + l_re, p8_re * c_im + p8_im * c_re + l_im

    def body(m, carry):
        out = []
        for b in range(batch):
            cf_re, cf_im, cb_re, cb_im = carry[4 * b:4 * b + 4]
            rf = pl.multiple_of(b * chunks + m * SUBLANES, SUBLANES)
            rb = pl.multiple_of(b * chunks + (n_tiles - 1 - m) * SUBLANES, SUBLANES)
            out += list(tile_scan(rf, False, cf_re, cf_im))
            out += list(tile_scan(rb, True, cb_re, cb_im))
        return tuple(out)

    init = tuple(jnp.broadcast_to(s0_ref[t, b:b + 1, :], (SUBLANES, LANES))
                 for b in range(batch) for t in range(4))
    fin = lax.fori_loop(0, n_tiles, body, init, unroll=min(4, n_tiles))
    fin_ref[...] = jnp.zeros(fin_ref.shape, F32)
    for b in range(batch):
        for t in range(4):
            fin_ref[t, b:b + 1, :] = fin[4 * b + t][0:1, :]
    if emit_y:
        y_intra = jnp.concatenate(
            [_dot(u[:, gl * GROUP_W:(gl + 1) * GROUP_W], mi_ref[gl]) for gl in range(2)], axis=-1)
        y = y_intra + _dot(in_scr[...].astype(BF16), wo_ref[...])
        y_ref[...] = y.astype(y_ref.dtype)


def _s5_scan(uc, mi, ws, wo, tab, s0, batch, emit_y=True):
    rows = uc.shape[0]
    pair = lambda *shape: pl.BlockSpec((None,) + shape, lambda p: (p,) + (0,) * len(shape))
    uc_spec = pl.BlockSpec((rows, PAIR_W), lambda p: (0, p))
    state_spec = pair(4, SUBLANES, LANES)
    state_shape = jax.ShapeDtypeStruct((N_PAIRS, 4, SUBLANES, LANES), F32)
    scratch = [pltpu.VMEM((rows, PAIR_W), F32)]
    if emit_y:
        in_specs = [uc_spec, pl.BlockSpec((2, GROUP_W, GROUP_W), lambda p: (p, 0, 0)),
                    pair(PAIR_W, PAIR_W), pair(PAIR_W, PAIR_W), pair(2 * TAB_ROWS, LANES), state_spec]
        args = (uc, mi, ws, wo, tab, s0)
        out_specs = [uc_spec, state_spec]
        out_shape = [jax.ShapeDtypeStruct((rows, N_PAIRS * PAIR_W), BF16), state_shape]
        scratch = scratch * 2
    else:
        in_specs = [uc_spec, pair(PAIR_W, PAIR_W), pair(2 * TAB_ROWS, LANES), state_spec]
        args = (uc, ws, tab, s0)
        out_specs = state_spec
        out_shape = state_shape
    return pl.pallas_call(
        functools.partial(_s5_scan_kernel, batch=batch, emit_y=emit_y),
        grid=(N_PAIRS,), in_specs=in_specs, out_specs=out_specs, out_shape=out_shape,
        scratch_shapes=scratch,
        compiler_params=pltpu.CompilerParams(
            dimension_semantics=("arbitrary",), vmem_limit_bytes=VMEM_LIMIT),
        name="s5_scan" if emit_y else "s5_scan_ctx",
    )(*args)


def _mix_out_kernel(y_ref, siga_ref, sb_ref, x_ref, remb_ref, cemb_ref, mod_ref, wv_ref, wg_ref, wo_ref,
                    ln1_ref, wr_ref, br_ref, ut_ref, lt_ref,
                    x1_ref, xs_ref, route_ref, len_ref, y_scr):
    _from_chunk_tile(y_ref, y_scr)
    y = jnp.concatenate([y_scr[v] for v in range(S5_WIDTH // LANES)], axis=-1)
    ya = (0.5 * y * (1.0 + jnp.tanh(math.sqrt(2.0 / math.pi) * (y + 0.044715 * (y * y * y))))).astype(BF16)
    out_a = _dot(ya, wv_ref[...]) * _sigmoid(_dot(ya, wg_ref[...]))
    merged = siga_ref[...].astype(F32) * out_a + sb_ref[...].astype(F32)
    mix = _dot(merged.astype(BF16), wo_ref[...])
    xp = _with_positions(x_ref, remb_ref, cemb_ref)
    x1 = _ln(ALPHA * xp + mod_ref[0:1, :] * mix) * ln1_ref[0:1, :] + ln1_ref[1:2, :]
    x1_ref[...] = x1
    h_hi = (_ln(x1) * mod_ref[2:3, :] + mod_ref[1:2, :]).astype(BF16)
    _route_and_sort(h_hi, wr_ref, br_ref, ut_ref, lt_ref, xs_ref, route_ref, len_ref)


def _route_and_sort(h_hi, wr_ref, br_ref, ut_ref, lt_ref, xs_ref, route_ref, len_ref):
    tm = h_hi.shape[0]
    f32 = F32
    nt = (((1,), (1,)), ((), ()))
    lg = lax.dot_general(wr_ref[...], h_hi, nt, preferred_element_type=f32) + br_ref[...]
    rowi = lax.broadcasted_iota(jnp.int32, (LOGIT_ROWS, tm), 0).astype(f32)
    neg = jnp.float32(-jnp.inf)
    big = jnp.float32(LOGIT_ROWS)
    gl = jnp.where(rowi < N_EXPERT_GROUPS, lg, neg)
    gmax = jnp.max(gl, axis=0, keepdims=True)
    g_idx = jnp.min(jnp.where(gl == gmax, rowi, big), axis=0, keepdims=True)
    p_group = 1.0 / jnp.sum(jnp.exp(gl - gmax), axis=0, keepdims=True)
    e_lo = N_EXPERT_GROUPS + g_idx * EXPERTS_PER_GROUP
    el = jnp.where((rowi >= e_lo) & (rowi < e_lo + EXPERTS_PER_GROUP), lg, neg)
    m1 = jnp.max(el, axis=0, keepdims=True)
    i1 = jnp.min(jnp.where(el == m1, rowi, big), axis=0, keepdims=True)
    el2 = jnp.where(rowi == i1, neg, el)
    m2 = jnp.max(el2, axis=0, keepdims=True)
    i2 = jnp.min(jnp.where(el2 == m2, rowi, big), axis=0, keepdims=True)
    r = jnp.exp(m2 - m1)
    w1 = p_group / (1.0 + r)
    w2 = p_group * r / (1.0 + r)
    e1 = i1 - N_EXPERT_GROUPS
    e2 = i2 - N_EXPERT_GROUPS
    rowa = lax.broadcasted_iota(jnp.int32, (ASSIGN_ROWS, tm), 0).astype(f32)
    a12 = jnp.where(rowa == e1, 1.0, 0.0) + jnp.where(rowa == e2 + N_EXPERTS, 1.0, 0.0)
    rank = _dot(a12.astype(BF16), ut_ref[...])
    cnt = jnp.broadcast_to(jnp.sum(a12, axis=1, keepdims=True), (ASSIGN_ROWS, LANES))
    row = lax.broadcasted_iota(jnp.int32, (ASSIGN_ROWS, LANES), 0)
    tot = cnt + pltpu.roll(cnt, N_EXPERTS, 0)
    run = jnp.where(row < N_EXPERTS, jnp.floor((tot + (PIECE - 1)) * (1.0 / PIECE)), 0.0)
    off = PIECE * _dot(lt_ref[...], run.astype(BF16))
    base = jnp.where(row < N_EXPERTS, off, pltpu.roll(off + cnt, N_EXPERTS, 0))
    posmat = a12 * (rank + base[:, 0:1])
    pos1 = jnp.sum(posmat[0:N_EXPERTS, :], axis=0, keepdims=True)
    pos2 = jnp.sum(posmat[N_EXPERTS:2 * N_EXPERTS, :], axis=0, keepdims=True)
    ri = lax.broadcasted_iota(jnp.int32, (xs_ref.shape[0], tm), 0).astype(f32)
    perm = jnp.where((ri == pos1) | (ri == pos2), 1.0, 0.0).astype(BF16)
    xs_ref[...] = _dot(perm, h_hi).astype(xs_ref.dtype)
    rowr = lax.broadcasted_iota(jnp.int32, (SUBLANES, tm), 0)
    rec = jnp.zeros((SUBLANES, tm), f32)
    for col, val in ((REC_W1, w1), (REC_W2, w2), (REC_POS1, pos1), (REC_POS2, pos2)):
        rec = jnp.where(rowr == col, val, rec)
    rec = jnp.concatenate([rec, jnp.zeros((ROUTE_LANES - SUBLANES, tm), f32)], axis=0)
    route_ref[...] = rec.T
    len_ref[...] = run


def _local_rows(tm):
    return -(-(TOP_K * tm + N_EXPERTS * (PIECE - 1)) // PIECE) * PIECE


def _mix_out(y, siga, sb, x, r_emb, c_emb, mod, wv, wg, wo, ln1, wr, br, tm):
    b, l, d = x.shape
    n_tiles = l // tm
    xs_rows = _local_rows(tm)
    tok = lambda w: pl.BlockSpec((None, tm, w), lambda i, j: (j, i, 0))
    whole = lambda a: pl.BlockSpec(a.shape, lambda i, j: (0,) * a.ndim)
    ut = (jnp.arange(tm)[:, None] < jnp.arange(tm)[None, :]).astype(BF16)
    lt = (jnp.arange(ASSIGN_ROWS)[None, :] < jnp.arange(ASSIGN_ROWS)[:, None]).astype(BF16)
    return pl.pallas_call(
        _mix_out_kernel,
        grid=(n_tiles, b),
        in_specs=[pl.BlockSpec((tm // CHUNK, CHUNK * S5_WIDTH), lambda i, j: (j * n_tiles + i, 0)),
                  tok(d), tok(d), tok(d)] + _pos_specs(tm, d) + [
                  pl.BlockSpec((None, 4, d), lambda i, j: (j, 0, 0)),
                  whole(wv), whole(wg), whole(wo), whole(ln1), whole(wr), whole(br),
                  whole(ut), whole(lt)],
        out_specs=[tok(d),
                   pl.BlockSpec((xs_rows, d), lambda i, j: (j * n_tiles + i, 0)),
                   tok(ROUTE_LANES),
                   pl.BlockSpec((None, ASSIGN_ROWS, LANES), lambda i, j: (j * n_tiles + i, 0, 0))],
        out_shape=[jax.ShapeDtypeStruct((b, l, d), F32),
                   jax.ShapeDtypeStruct((b * n_tiles * xs_rows, d), BF16),
                   jax.ShapeDtypeStruct((b, l, ROUTE_LANES), F32),
                   jax.ShapeDtypeStruct((b * n_tiles, ASSIGN_ROWS, LANES), F32)],
        scratch_shapes=[pltpu.VMEM((S5_WIDTH // LANES, tm, LANES), F32)],
        compiler_params=pltpu.CompilerParams(
            dimension_semantics=("arbitrary", "arbitrary"), vmem_limit_bytes=VMEM_LIMIT),
        name="mix_out",
    )(y, siga, sb, x, r_emb, c_emb, mod, wv, wg, wo, ln1, wr, br, ut, lt)


def _piece_copy(src_hbm, src_row, dst, piece, sem):
    return pltpu.make_async_copy(src_hbm.at[pl.ds(pl.multiple_of(src_row, PIECE), PIECE), :],
                                 dst.at[pl.ds(pl.multiple_of(piece * PIECE, PIECE), PIECE), :], sem)


def _issue_pieces(src_hbm, table_ref, first, n_pieces, dst, sem):
    for p in range(n_pieces):
        _piece_copy(src_hbm, table_ref[first + p], dst, p, sem).start()


def _wait_pieces(src_hbm, dst, sem):
    pltpu.make_async_copy(src_hbm.at[pl.ds(0, dst.shape[0]), :], dst, sem).wait()


def _experts_kernel(be_ref, piece_ref, nused_ref, xs_hbm, wg_ref, wu_ref, wd_ref, ys_ref,
                    xs_buf0, xs_buf1, wg_bf, wu_bf, wd_bf, sem):
    i = pl.program_id(0)
    n_used = nused_ref[0]
    per_block = ROW_BLOCK // PIECE

    @pl.when(i == 0)
    def _():
        _issue_pieces(xs_hbm, piece_ref, 0, per_block, xs_buf0, sem.at[0])

    def block(cur, cur_sem, oth, oth_sem):
        @pl.when((i == 0) | (be_ref[i] != be_ref[jnp.maximum(i - 1, 0)]))
        def _():
            wg_bf[...] = wg_ref[...].astype(BF16)
            wu_bf[...] = wu_ref[...].astype(BF16)
            wd_bf[...] = wd_ref[...].astype(BF16)

        _wait_pieces(xs_hbm, cur, cur_sem)
        nxt = jnp.minimum(i + 1, n_used - 1)
        _issue_pieces(xs_hbm, piece_ref, nxt * per_block, per_block, oth, oth_sem)

        xb = cur[...]
        gate = _dot(xb, wg_bf[...])
        up = _dot(xb, wu_bf[...])
        hid = (gate * _sigmoid(gate) * up).astype(BF16)
        ys_ref[...] = _dot(hid, wd_bf[...]).astype(ys_ref.dtype)

        @pl.when(i == n_used - 1)
        def _():
            _wait_pieces(xs_hbm, oth, oth_sem)

    for s, (cur, oth) in enumerate(((xs_buf0, xs_buf1), (xs_buf1, xs_buf0))):
        @pl.when((i < n_used) & (i % 2 == s))
        def _(s=s, cur=cur, oth=oth):
            block(cur, sem.at[s], oth, sem.at[1 - s])

    @pl.when(i >= n_used)
    def _():
        ys_ref[...] = jnp.zeros(ys_ref.shape, ys_ref.dtype)


def _experts(block_e, piece_src, n_used, xs, wg, wu, wd, n_blocks):
    d = xs.shape[1]
    by_expert = lambda i, be, ps, nu: (0, be[i], 0, 0)
    grid_spec = pltpu.PrefetchScalarGridSpec(
        num_scalar_prefetch=3,
        grid=(n_blocks,),
        in_specs=[pl.BlockSpec(memory_space=pl.ANY),
                  pl.BlockSpec((None, None, d, EXPERT_FF), by_expert),
                  pl.BlockSpec((None, None, d, EXPERT_FF), by_expert),
                  pl.BlockSpec((None, None, EXPERT_FF, d), by_expert)],
        out_specs=pl.BlockSpec((ROW_BLOCK, d), lambda i, be, ps, nu: (i, 0)),
        scratch_shapes=[pltpu.VMEM((ROW_BLOCK, d), BF16), pltpu.VMEM((ROW_BLOCK, d), BF16),
                        pltpu.VMEM((d, EXPERT_FF), BF16), pltpu.VMEM((d, EXPERT_FF), BF16),
                        pltpu.VMEM((EXPERT_FF, d), BF16), pltpu.SemaphoreType.DMA((2,))],
    )
    return pl.pallas_call(
        _experts_kernel,
        grid_spec=grid_spec,
        out_shape=jax.ShapeDtypeStruct((n_blocks * ROW_BLOCK, d), BF16),
        compiler_params=pltpu.CompilerParams(
            dimension_semantics=("arbitrary",), vmem_limit_bytes=VMEM_LIMIT),
        name="experts",
    )(block_e, piece_src, n_used, xs, wg, wu, wd)


def _combine_kernel(piece_ref, ys_hbm, x1_ref, route_ref, mod_ref, ln2_ref, o_ref, buf, sem):
    i = pl.program_id(0)
    n = pl.num_programs(0)
    slot = i % 2
    rows = buf.shape[1]
    per_tile = rows // PIECE

    def issue(tile, s):
        def body(p, _):
            _piece_copy(ys_hbm, piece_ref[tile * per_tile + p], buf.at[s], p, sem.at[s]).start()
            return 0
        lax.fori_loop(0, per_tile, body, 0)

    @pl.when(i == 0)
    def _():
        issue(0, 0)

    _wait_pieces(ys_hbm, buf.at[slot], sem.at[slot])

    @pl.when(i + 1 < n)
    def _():
        issue(i + 1, 1 - slot)

    tm = x1_ref.shape[0]
    lane = lax.broadcasted_iota(jnp.int32, (tm, rows), 1).astype(F32)
    rec = lambda col: route_ref[:, col:col + 1]
    sel = jnp.where(lane == rec(REC_POS1), rec(REC_W1),
                    jnp.where(lane == rec(REC_POS2), rec(REC_W2), 0.0)).astype(BF16)
    moe = _dot(sel, buf[slot])
    z = ALPHA * x1_ref[...] + mod_ref[0:1, :] * moe
    o_ref[...] = _ln(z) * ln2_ref[0:1, :] + ln2_ref[1:2, :]


def _combine(piece_glob, ys, x1, route, mod, ln2, tm, tiles_per_batch):
    t, d = x1.shape
    grid_spec = pltpu.PrefetchScalarGridSpec(
        num_scalar_prefetch=1,
        grid=(t // tm,),
        in_specs=[pl.BlockSpec(memory_space=pl.ANY),
                  pl.BlockSpec((tm, d), lambda i, pg: (i, 0)),
                  pl.BlockSpec((tm, ROUTE_LANES), lambda i, pg: (i, 0)),
                  pl.BlockSpec((None, SUBLANES, d), lambda i, pg: (i // tiles_per_batch, 0, 0)),
                  pl.BlockSpec((2, d), lambda i, pg: (0, 0))],
        out_specs=pl.BlockSpec((tm, d), lambda i, pg: (i, 0)),
        scratch_shapes=[pltpu.VMEM((2, _local_rows(tm), d), BF16), pltpu.SemaphoreType.DMA((2,))],
    )
    return pl.pallas_call(
        _combine_kernel,
        grid_spec=grid_spec,
        out_shape=jax.ShapeDtypeStruct((t, d), F32),
        compiler_params=pltpu.CompilerParams(
            dimension_semantics=("arbitrary",), vmem_limit_bytes=VMEM_LIMIT),
        name="combine",
    )(piece_glob, ys, x1, route, mod, ln2)


def _sincos_2d(rows, cols, dim):
    q = dim // 4
    omega = 1.0 / (POS_BASE ** (jnp.arange(q, dtype=F32) / q))
    r = jnp.arange(rows, dtype=F32)[:, None] * omega
    cl = jnp.arange(cols, dtype=F32)[:, None] * omega
    r_emb = jnp.concatenate([jnp.sin(r), jnp.cos(r)], -1)
    c_emb = jnp.concatenate([jnp.sin(cl), jnp.cos(cl)], -1)
    return r_emb, c_emb


def _routing_tables(run_pieces, xs_rows, n_blocks):
    i32 = jnp.int32
    n_tiles = run_pieces.shape[0]
    ppb = ROW_BLOCK // PIECE
    loc_start = jnp.cumsum(run_pieces, axis=1) - run_pieces
    seg_tot = jnp.sum(run_pieces, axis=0)
    seg_pad = (seg_tot + ppb - 1) // ppb * ppb
    seg_end = jnp.cumsum(seg_pad)
    seg_start = seg_end - seg_pad
    run_t = run_pieces.T
    glob_start = seg_start[:, None] + jnp.cumsum(run_t, axis=1) - run_t
    n_used = (seg_end[-1] // ppb).astype(i32)
    blk = jnp.minimum(jnp.arange(n_blocks, dtype=i32), n_used - 1)
    block_e = jnp.minimum(jnp.sum((seg_end[None, :] <= (blk * ppb)[:, None]).astype(i32), axis=1),
                          N_EXPERTS - 1).astype(i32)
    lpt = xs_rows // PIECE
    starts = glob_start.reshape(-1)
    lens = run_t.reshape(-1)
    src0 = (jnp.arange(n_tiles, dtype=i32)[None, :] * lpt + loc_start.T).reshape(-1)
    p = jnp.arange(n_blocks * ppb, dtype=i32)
    within = p[:, None] - starts[None, :]
    hit = (within >= 0) & (within < lens[None, :])
    piece_src = jnp.sum(jnp.where(hit, (src0[None, :] + within) * PIECE, 0), axis=1).astype(i32)
    s = jnp.arange(lpt, dtype=i32)
    loc_within = s[None, :, None] - loc_start[:, None, :]
    hit = (loc_within >= 0) & (loc_within < run_pieces[:, None, :])
    piece_glob = jnp.sum(jnp.where(hit, (glob_start.T[:, None, :] + loc_within) * PIECE, 0), axis=2)
    return block_e, piece_src, piece_glob.astype(i32).reshape(-1), n_used.reshape(1)


def kernel(x, c, ctx, c_ctx, w_ada, b_ada, w_in, s5_log_dt_f, s5_a_re_f, s5_a_im_f, s5_b_re_f, s5_b_im_f, s5_c_re_f, s5_c_im_f, s5_log_dt_b, s5_a_re_b, s5_a_im_b, s5_b_re_b, s5_b_im_b, s5_c_re_b, s5_c_im_b, s5_d, s5_w_glu_val, s5_w_glu_gate, conv_w, conv_w_out, w_o, ln1_g, ln1_b, router_w_group, router_b_group, router_w_expert, router_b_expert, exp_w_gate, exp_w_up, exp_w_down, ln2_g, ln2_b):
    b, l, d = x.shape
    lc = ctx.shape[1]
    assert d == D_MODEL and b < SUBLANES and w_ada.shape[0] == DEPTH
    assert l % (SUBLANES * CHUNK) == 0 and lc % (SUBLANES * CHUNK) == 0 and l % GRID_W == 0
    t = b * l
    tm = min(TOKEN_TILE, l)
    tmc = min(TOKEN_TILE, lc)

    cc = jnp.concatenate([c, c_ctx[None, :], jnp.zeros((SUBLANES - b - 1, d), F32)], 0)
    mods = _mods(cc, w_ada[0], b_ada[0])
    sh1, sc1, g1, sh2, sc2, g2 = jnp.split(mods, 6, axis=-1)
    mod_a = jnp.stack([sh1[:b], 1.0 + sc1[:b]], 1)
    mod_ctx = jnp.broadcast_to(jnp.stack([sh1[b], 1.0 + sc1[b]], 0)[None], (b, 2, d))
    mod_c = jnp.stack([g1[:b], sh2[:b], 1.0 + sc2[:b], jnp.zeros((b, d), F32)], 1)
    mod_f = jnp.concatenate([g2[:b, None, :], jnp.zeros((b, SUBLANES - 1, d), F32)], 1)

    w_in_bf = w_in[0].astype(BF16)
    both = lambda fwd, bwd: jnp.concatenate([fwd, bwd], 0)
    s5_tab = _s5_tables(both(s5_log_dt_f, s5_log_dt_b), both(s5_a_re_f, s5_a_re_b), both(s5_a_im_f, s5_a_im_b),
                        both(s5_b_re_f, s5_b_re_b), both(s5_b_im_f, s5_b_im_b),
                        both(s5_c_re_f, s5_c_re_b), both(s5_c_im_f, s5_c_im_b))
    mi, ws, wo_s5, tab = _s5_operators(s5_tab, s5_d[0])

    (uc_ctx,) = _in_proj(ctx, jnp.zeros((lc // GRID_W, d // 2), F32), jnp.zeros((GRID_W, d // 2), F32),
                         mod_ctx, w_in_bf, None, None, tmc, False)
    zero_state = jnp.zeros((N_PAIRS, 4, SUBLANES, LANES), F32)
    s0 = _s5_scan(uc_ctx, None, ws, None, tab, zero_state, b, emit_y=False)

    r_emb, c_emb = _sincos_2d(l // GRID_W, GRID_W, d)
    uc, siga, sb = _in_proj(x, r_emb, c_emb, mod_a, w_in_bf, conv_w[0], conv_w_out[0].astype(BF16), tm, True)
    y, _ = _s5_scan(uc, mi, ws, wo_s5, tab, s0, b)

    wr = jnp.concatenate([router_w_group[0], router_w_expert[0],
                          jnp.zeros((d, LOGIT_ROWS - N_EXPERT_GROUPS - N_EXPERTS), F32)], 1).T.astype(BF16)
    br = jnp.concatenate([router_b_group[0], router_b_expert[0],
                          jnp.zeros((LOGIT_ROWS - N_EXPERT_GROUPS - N_EXPERTS,), F32)])[:, None]
    ln1 = jnp.stack([ln1_g[0], ln1_b[0]], 0)
    x1, xs, route, run_len = _mix_out(y, siga, sb, x, r_emb, c_emb, mod_c,
                                      s5_w_glu_val[0].astype(BF16), s5_w_glu_gate[0].astype(BF16),
                                      w_o[0].astype(BF16), ln1, wr, br, tm)

    x1 = x1.reshape(t, d)
    route = route.reshape(t, ROUTE_LANES)
    n_tiles = t // tm
    xs_rows = _local_rows(tm)
    run_pieces = run_len[:, :N_EXPERTS, 0].astype(jnp.int32)
    max_rows = t * TOP_K + n_tiles * N_EXPERTS * (PIECE - 1) + N_EXPERTS * (ROW_BLOCK - 1)
    n_blocks = -(-max_rows // ROW_BLOCK)
    block_e, piece_src, piece_glob, n_used = _routing_tables(run_pieces, xs_rows, n_blocks)
    ys = _experts(block_e, piece_src, n_used, xs, exp_w_gate, exp_w_up, exp_w_down, n_blocks)
    ln2 = jnp.stack([ln2_g[0], ln2_b[0]], 0)
    out = _combine(piece_glob, ys, x1, route, mod_f, ln2, tm, l // tm)
    return out.reshape(b, l, d)
```

```python
import functools
import math

import jax
import jax.numpy as jnp
from jax import lax
from jax.experimental import pallas as pl
from jax.experimental.pallas import tpu as pltpu

F32 = jnp.float32
BF16 = jnp.bfloat16
HI = lax.Precision.HIGHEST

D_MODEL = 1024
GRID_W = 64
S5_WIDTH = 512
S5_GROUP_CH = 16
S5_GROUPS = S5_WIDTH // S5_GROUP_CH
S5_STATE = 64
CONV_WIDTH = 512
N_EXPERT_GROUPS = 4
EXPERTS_PER_GROUP = 8
N_EXPERTS = N_EXPERT_GROUPS * EXPERTS_PER_GROUP
EXPERT_FF = 512
TOP_K = 2
DEPTH = 1
ALPHA = (2.0 * DEPTH) ** 0.25
LN_EPS = 1e-6
POS_BASE = 10000.0
GELU_K = math.sqrt(2.0 / math.pi)
GELU_A = 0.044715

LANES = 128
SUBLANES = 8
CHUNK = 16
GROUP_W = CHUNK * S5_GROUP_CH
PAIR_W = 2 * GROUP_W
N_PAIRS = S5_GROUPS // 2
TOK_PER_VREG = LANES // S5_GROUP_CH
TAB_ROWS = 24
TAB_POW = 2 * SUBLANES
MODS_COLS = 1536
ROUTE_LANES = 128
LOGIT_ROWS = 48
ASSIGN_ROWS = TOP_K * N_EXPERTS
REC_W1, REC_W2, REC_POS1, REC_POS2 = range(4)
TOKEN_TILE = 512
ROW_BLOCK = 512
PIECE = 16
VMEM_LIMIT = 56 * 1024 * 1024


def _ln(x):
    mu = jnp.mean(x, axis=-1, keepdims=True)
    xc = x - mu
    var = jnp.mean(xc * xc, axis=-1, keepdims=True)
    return xc * lax.rsqrt(var + LN_EPS)


def _sigmoid(x):
    return 0.5 * (jnp.tanh(0.5 * x) + 1.0)


def _dot(a, b):
    return jnp.dot(a, b, preferred_element_type=F32)


def _mods_kernel(c_ref, w_ref, b_ref, o_ref):
    c = c_ref[...]
    a = c * _sigmoid(c)
    o_ref[...] = jnp.dot(a, w_ref[...], precision=HI, preferred_element_type=F32) + b_ref[...]


def _mods(cc, w_ada, b_ada):
    n = w_ada.shape[1]
    nb = MODS_COLS
    return pl.pallas_call(
        _mods_kernel,
        grid=(n // nb,),
        in_specs=[pl.BlockSpec((SUBLANES, D_MODEL), lambda i: (0, 0)),
                  pl.BlockSpec((D_MODEL, nb), lambda i: (0, i)),
                  pl.BlockSpec((1, nb), lambda i: (0, i))],
        out_specs=pl.BlockSpec((SUBLANES, nb), lambda i: (0, i)),
        out_shape=jax.ShapeDtypeStruct((SUBLANES, n), F32),
        compiler_params=pltpu.CompilerParams(vmem_limit_bytes=VMEM_LIMIT),
        name="mods",
    )(cc, w_ada, b_ada.reshape(1, n))


def _slot_masks(rows):
    slot = lax.broadcasted_iota(jnp.int32, (rows, LANES), 1) // S5_GROUP_CH
    return [slot == s for s in range(TOK_PER_VREG)]


def _to_chunk_tile(u_scr, uc_ref):
    nch = uc_ref.shape[0]
    masks = _slot_masks(nch)
    for qh in range(CHUNK // TOK_PER_VREG):
        for v in range(S5_WIDTH // LANES):
            src = [u_scr[v, pl.ds(qh * TOK_PER_VREG + s, nch, stride=CHUNK), :] for s in range(TOK_PER_VREG)]
            for i in range(TOK_PER_VREG):
                acc = None
                for s in range(TOK_PER_VREG):
                    shift = ((s - i) * S5_GROUP_CH) % LANES
                    piece = pltpu.roll(src[s], shift, 1) if shift else src[s]
                    acc = piece if acc is None else jnp.where(masks[s], piece, acc)
                lo = (v * TOK_PER_VREG + i) * GROUP_W + qh * LANES
                uc_ref[:, lo:lo + LANES] = acc.astype(uc_ref.dtype)


def _from_chunk_tile(yc_ref, y_scr):
    nch = yc_ref.shape[0]
    masks = _slot_masks(nch)
    for qh in range(CHUNK // TOK_PER_VREG):
        for v in range(S5_WIDTH // LANES):
            src = []
            for i in range(TOK_PER_VREG):
                lo = (v * TOK_PER_VREG + i) * GROUP_W + qh * LANES
                src.append(yc_ref[:, lo:lo + LANES].astype(F32))
            for s in range(TOK_PER_VREG):
                acc = None
                for i in range(TOK_PER_VREG):
                    shift = ((i - s) * S5_GROUP_CH) % LANES
                    piece = pltpu.roll(src[i], shift, 1) if shift else src[i]
                    acc = piece if acc is None else jnp.where(masks[i], piece, acc)
                y_scr[v, pl.ds(qh * TOK_PER_VREG + s, nch, stride=CHUNK), :] = acc


def _with_positions(x_ref, remb_ref, cemb_ref):
    c = cemb_ref[...]
    slabs = []
    for j in range(x_ref.shape[0] // GRID_W):
        r = jnp.broadcast_to(remb_ref[j:j + 1, :], c.shape)
        slabs.append(x_ref[j * GRID_W:(j + 1) * GRID_W, :] + jnp.concatenate([r, c], axis=-1))
    return jnp.concatenate(slabs, axis=0)


def _in_proj_kernel(x_ref, remb_ref, cemb_ref, mod_ref, w_ref, *rest, full):
    if full:
        cw_ref, cwo_ref, uc_ref, siga_ref, sb_ref, u_scr = rest
    else:
        uc_ref, u_scr = rest
    xp = _with_positions(x_ref, remb_ref, cemb_ref)
    h = (_ln(xp) * mod_ref[1:2, :] + mod_ref[0:1, :]).astype(BF16)
    o1 = S5_WIDTH
    o2, o3, o4 = o1 + CONV_WIDTH, o1 + 2 * CONV_WIDTH, o1 + 3 * CONV_WIDTH
    o5 = o4 + D_MODEL
    u = _dot(h, w_ref[:, 0:o1])
    for v in range(S5_WIDTH // LANES):
        u_scr[v] = u[:, v * LANES:(v + 1) * LANES]
    _to_chunk_tile(u_scr, uc_ref)
    if not full:
        return
    z_b = _dot(h, w_ref[:, o1:o2])
    gate_c = _dot(h, w_ref[:, o3:o4])
    p = gate_c * z_b
    tm = p.shape[0]
    col = lax.broadcasted_iota(jnp.int32, (tm, 1), 0) % GRID_W
    prev = jnp.where(col == 0, 0.0, pltpu.roll(p, 1, 0))
    nxt = jnp.where(col == GRID_W - 1, 0.0, pltpu.roll(p, tm - 1, 0))
    v = cw_ref[0:1, :] * prev + cw_ref[1:2, :] * p + cw_ref[2:3, :] * nxt
    gate_b = _dot(h, w_ref[:, o2:o3])
    out_b = _dot((gate_b * v).astype(BF16), cwo_ref[...])
    merge_b = _dot(h, w_ref[:, o5:])
    sb_ref[...] = (_sigmoid(merge_b) * out_b).astype(sb_ref.dtype)
    merge_a = _dot(h, w_ref[:, o4:o5])
    siga_ref[...] = _sigmoid(merge_a).astype(siga_ref.dtype)


def _pos_specs(tm, d):
    return [pl.BlockSpec((tm // GRID_W, d // 2), lambda i, j: (i, 0)),
            pl.BlockSpec((GRID_W, d // 2), lambda i, j: (0, 0))]


def _in_proj(x, r_emb, c_emb, mod, w_in_bf, conv_w, conv_w_out_bf, tm, full):
    b, l, d = x.shape
    n_tiles = l // tm
    grid = (n_tiles, b)
    tok = lambda w: pl.BlockSpec((None, tm, w), lambda i, j: (j, i, 0))
    chunk_spec = pl.BlockSpec((tm // CHUNK, CHUNK * S5_WIDTH), lambda i, j: (j * n_tiles + i, 0))
    chunk_shape = jax.ShapeDtypeStruct((b * l // CHUNK, CHUNK * S5_WIDTH), BF16)
    in_specs = [tok(d)] + _pos_specs(tm, d) + [pl.BlockSpec((None, 2, d), lambda i, j: (j, 0, 0))]
    args = [x, r_emb, c_emb, mod]
    if full:
        in_specs += [pl.BlockSpec(w_in_bf.shape, lambda i, j: (0, 0)),
                     pl.BlockSpec(conv_w.shape, lambda i, j: (0, 0)),
                     pl.BlockSpec(conv_w_out_bf.shape, lambda i, j: (0, 0))]
        args += [w_in_bf, conv_w, conv_w_out_bf]
        out_specs = [chunk_spec, tok(d), tok(d)]
        out_shape = [chunk_shape,
                     jax.ShapeDtypeStruct((b, l, d), BF16),
                     jax.ShapeDtypeStruct((b, l, d), BF16)]
    else:
        in_specs += [pl.BlockSpec((d, S5_WIDTH), lambda i, j: (0, 0))]
        args += [w_in_bf]
        out_specs = [chunk_spec]
        out_shape = [chunk_shape]
    return pl.pallas_call(
        functools.partial(_in_proj_kernel, full=full),
        grid=grid, in_specs=in_specs, out_specs=out_specs, out_shape=out_shape,
        scratch_shapes=[pltpu.VMEM((S5_WIDTH // LANES, tm, LANES), F32)],
        compiler_params=pltpu.CompilerParams(
            dimension_semantics=("arbitrary", "arbitrary"), vmem_limit_bytes=VMEM_LIMIT),
        name="in_proj" if full else "in_proj_ctx",
    )(*args)


def _s5_tables(log_dt, a_re, a_im, b_re, b_im, c_re, c_im):
    f32 = F32
    dt = jnp.exp(log_dt.astype(f32))[..., None]
    a_re = a_re.astype(f32)
    a_im = a_im.astype(f32)
    mag = jnp.exp(dt * a_re)
    ab_re = mag * jnp.cos(dt * a_im)
    ab_im = mag * jnp.sin(dt * a_im)
    den = a_re * a_re + a_im * a_im
    x_re = ab_re - 1.0
    f_re = (x_re * a_re + ab_im * a_im) / den
    f_im = (ab_im * a_re - x_re * a_im) / den
    b_re = b_re.astype(f32)
    b_im = b_im.astype(f32)
    bb_re = f_re[..., None] * b_re - f_im[..., None] * b_im
    bb_im = f_re[..., None] * b_im + f_im[..., None] * b_re
    k = jnp.arange(CHUNK + 1, dtype=f32)[None, :, None, None]
    pmag = jnp.exp(k * (dt * a_re)[:, None])
    p_re = pmag * jnp.cos(k * (dt * a_im)[:, None])
    p_im = pmag * jnp.sin(k * (dt * a_im)[:, None])
    pb_re = p_re[..., None] * bb_re[:, None] - p_im[..., None] * bb_im[:, None]
    pb_im = p_re[..., None] * bb_im[:, None] + p_im[..., None] * bb_re[:, None]
    c_re = c_re.astype(f32)[:, None]
    c_im = c_im.astype(f32)[:, None]
    cp_re = c_re * p_re[:, :, :, None, :] - c_im * p_im[:, :, :, None, :]
    cp_im = -(c_re * p_im[:, :, :, None, :] + c_im * p_re[:, :, :, None, :])
    return dict(p_re=p_re, p_im=p_im, pb_re=pb_re, pb_im=pb_im, cp_re=cp_re, cp_im=cp_im,
                bb_re=bb_re, bb_im=bb_im)


def _lag_kernels(t):
    g, n, c = S5_GROUPS, S5_STATE, S5_GROUP_CH
    k = CHUNK + 1
    lhs = jnp.concatenate([t['cp_re'], t['cp_im']], -1)
    lhs = lhs.transpose(0, 2, 1, 3, 4).reshape(2 * g, k * c, 2 * n)
    rhs = jnp.concatenate([t['bb_re'], t['bb_im']], -2).reshape(2 * g, 2 * n, c)
    out = jnp.einsum('bmn,bnc->bmc', lhs, rhs, precision=HI)
    out = out.reshape(2, g, k, c, c).transpose(0, 2, 1, 4, 3)
    return out[0], out[1]


def _s5_operators(t, s5_d):
    q = CHUNK
    g, n, c = S5_GROUPS, S5_STATE, S5_GROUP_CH
    kern_f, kern_b = _lag_kernels(t)
    k0 = kern_f[0] + kern_b[0] + s5_d.astype(F32)[:, :, None] * jnp.eye(c, dtype=F32)[None]
    kc = jnp.concatenate([kern_b[1:q][::-1], k0[None], kern_f[1:q]], 0)
    kct = kc.transpose(1, 2, 0, 3)
    m_intra = jnp.stack([kct[:, :, q - 1 - i:2 * q - 1 - i, :] for i in range(q)], 1)
    m_intra = m_intra.reshape(g, q * c, q * c)
    w_st = jnp.stack([t['pb_re'][0, :q][::-1], t['pb_im'][0, :q][::-1],
                      t['pb_re'][1, :q], t['pb_im'][1, :q]], 0)
    w_st = w_st.transpose(2, 1, 4, 0, 3).reshape(g, q * c, 4, n)
    w_out = jnp.stack([t['cp_re'][0, 1:], t['cp_im'][0, 1:],
                       t['cp_re'][1, 1:][::-1], t['cp_im'][1, 1:][::-1]], 0)
    w_out = w_out.transpose(2, 0, 4, 1, 3).reshape(g, 4, n, q * c)
    np_ = N_PAIRS
    w_st = w_st.astype(BF16).reshape(np_, 2, q * c, 4, n)
    ws_pair = jnp.concatenate([jnp.pad(w_st[:, 0], ((0, 0), (0, 0), (0, 0), (0, n))),
                               jnp.pad(w_st[:, 1], ((0, 0), (0, 0), (0, 0), (n, 0)))], 1)
    ws_pair = ws_pair.reshape(np_, PAIR_W, 4 * 2 * n)
    w_out = w_out.astype(BF16).reshape(np_, 2, 4, n, q * c)
    wo_pair = jnp.stack([jnp.pad(w_out[:, 0], ((0, 0), (0, 0), (0, 0), (0, q * c))),
                         jnp.pad(w_out[:, 1], ((0, 0), (0, 0), (0, 0), (q * c, 0)))], 2)
    wo_pair = wo_pair.reshape(np_, 4 * 2 * n, PAIR_W)
    tab = _chunk_power_table(t).reshape(2 * TAB_ROWS, np_, 2 * n).transpose(1, 0, 2)
    return m_intra.astype(BF16), ws_pair, wo_pair, tab


def _chunk_power_table(t):
    def cmul(x, y):
        return x[0] * y[0] - x[1] * y[1], x[0] * y[1] + x[1] * y[0]
    p1 = (t['p_re'][:, CHUNK], t['p_im'][:, CHUNK])
    p2 = cmul(p1, p1)
    p4 = cmul(p2, p2)
    p8 = cmul(p4, p4)
    pr = [(jnp.ones_like(p1[0]), jnp.zeros_like(p1[0]))]
    for _ in range(SUBLANES - 1):
        pr.append(cmul(pr[-1], p1))
    pr_re = jnp.stack([p[0] for p in pr], 0)
    pr_im = jnp.stack([p[1] for p in pr], 0)
    pw = jnp.stack([p1[0], p1[1], p2[0], p2[1], p4[0], p4[1], p8[0], p8[1]], 0)
    return jnp.concatenate([pr_re[:, 0], pr_im[:, 0], pw[:, 0],
                            pr_re[::-1, 1], pr_im[::-1, 1], pw[:, 1]], 0)


def _s5_scan_kernel(uc_ref, *rest, batch, emit_y):
    if emit_y:
        mi_ref, ws_ref, wo_ref, tab_ref, s0_ref, y_ref, fin_ref, s_scr, in_scr = rest
    else:
        ws_ref, tab_ref, s0_ref, fin_ref, s_scr = rest
    rows = uc_ref.shape[0]
    chunks = rows // batch
    n_tiles = chunks // SUBLANES
    u = uc_ref[...]
    s_scr[...] = _dot(u, ws_ref[...])
    row = lax.broadcasted_iota(jnp.int32, (SUBLANES, LANES), 0)

    def tile_scan(r0, backward, c_re, c_im):
        base = TAB_ROWS if backward else 0
        col = 2 * LANES if backward else 0
        rs = pl.ds(r0, SUBLANES)

        def shift(z, k):
            if backward:
                return jnp.where(row < SUBLANES - k, pltpu.roll(z, SUBLANES - k, 0), 0.0)
            return jnp.where(row >= k, pltpu.roll(z, k, 0), 0.0)

        z_re = s_scr[rs, col:col + LANES]
        z_im = s_scr[rs, col + LANES:col + 2 * LANES]
        for k, t in ((1, TAB_POW), (2, TAB_POW + 2), (4, TAB_POW + 4)):
            a_re = tab_ref[base + t:base + t + 1, :]
            a_im = tab_ref[base + t + 1:base + t + 2, :]
            sh_re = shift(z_re, k)
            sh_im = shift(z_im, k)
            z_re, z_im = z_re + (a_re * sh_re - a_im * sh_im), z_im + (a_re * sh_im + a_im * sh_re)
        pr_re = tab_ref[base:base + SUBLANES, :]
        pr_im = tab_ref[base + SUBLANES:base + 2 * SUBLANES, :]
        if emit_y:
            in_scr[rs, col:col + LANES] = pr_re * c_re - pr_im * c_im + shift(z_re, 1)
            in_scr[rs, col + LANES:col + 2 * LANES] = pr_re * c_im + pr_im * c_re + shift(z_im, 1)
        last = 0 if backward else SUBLANES - 1
        l_re = jnp.broadcast_to(z_re[last:last + 1, :], (SUBLANES, LANES))
        l_im = jnp.broadcast_to(z_im[last:last + 1, :], (SUBLANES, LANES))
        p8_re = tab_ref[base + TAB_POW + 6:base + TAB_POW + 7, :]
        p8_im = tab_ref[base + TAB_POW + 7:base + TAB_POW + 8, :]
        return p8_re * c_re - p8_im * c_im + l_re, p8_re * c_im + p8_im * c_re + l_im

    def body(m, carry):
        out = []
        for b in range(batch):
            cf_re, cf_im, cb_re, cb_im = carry[4 * b:4 * b + 4]
            rf = pl.multiple_of(b * chunks + m * SUBLANES, SUBLANES)
            rb = pl.multiple_of(b * chunks + (n_tiles - 1 - m) * SUBLANES, SUBLANES)
            out += list(tile_scan(rf, False, cf_re, cf_im))
            out += list(tile_scan(rb, True, cb_re, cb_im))
        return tuple(out)

    init = tuple(jnp.broadcast_to(s0_ref[t, b:b + 1, :], (SUBLANES, LANES))
                 for b in range(batch) for t in range(4))
    fin = lax.fori_loop(0, n_tiles, body, init, unroll=min(4, n_tiles))
    fin_ref[...] = jnp.zeros(fin_ref.shape, F32)
    for b in range(batch):
        for t in range(4):
            fin_ref[t, b:b + 1, :] = fin[4 * b + t][0:1, :]
    if emit_y:
        y_intra = jnp.concatenate(
            [_dot(u[:, gl * GROUP_W:(gl + 1) * GROUP_W], mi_ref[gl]) for gl in range(2)], axis=-1)
        y = y_intra + _dot(in_scr[...].astype(BF16), wo_ref[...])
        y_ref[...] = y.astype(y_ref.dtype)


def _s5_scan(uc, mi, ws, wo, tab, s0, batch, emit_y=True):
    rows = uc.shape[0]
    pair = lambda *shape: pl.BlockSpec((None,) + shape, lambda p: (p,) + (0,) * len(shape))
    uc_spec = pl.BlockSpec((rows, PAIR_W), lambda p: (0, p))
    state_spec = pair(4, SUBLANES, LANES)
    state_shape = jax.ShapeDtypeStruct((N_PAIRS, 4, SUBLANES, LANES), F32)
    scratch = [pltpu.VMEM((rows, PAIR_W), F32)]
    if emit_y:
        in_specs = [uc_spec, pl.BlockSpec((2, GROUP_W, GROUP_W), lambda p: (p, 0, 0)),
                    pair(PAIR_W, PAIR_W), pair(PAIR_W, PAIR_W), pair(2 * TAB_ROWS, LANES), state_spec]
        args = (uc, mi, ws, wo, tab, s0)
        out_specs = [uc_spec, state_spec]
        out_shape = [jax.ShapeDtypeStruct((rows, N_PAIRS * PAIR_W), BF16), state_shape]
        scratch = scratch * 2
    else:
        in_specs = [uc_spec, pair(PAIR_W, PAIR_W), pair(2 * TAB_ROWS, LANES), state_spec]
        args = (uc, ws, tab, s0)
        out_specs = state_spec
        out_shape = state_shape
    return pl.pallas_call(
        functools.partial(_s5_scan_kernel, batch=batch, emit_y=emit_y),
        grid=(N_PAIRS,), in_specs=in_specs, out_specs=out_specs, out_shape=out_shape,
        scratch_shapes=scratch,
        compiler_params=pltpu.CompilerParams(
            dimension_semantics=("arbitrary",), vmem_limit_bytes=VMEM_LIMIT),
        name="s5_scan" if emit_y else "s5_scan_ctx",
    )(*args)


def _mix_out_kernel(y_ref, siga_ref, sb_ref, x_ref, remb_ref, cemb_ref, mod_ref, wv_ref, wg_ref, wo_ref,
                    ln1_ref, wr_ref, br_ref, ut_ref, lt_ref,
                    x1_ref, xs_ref, route_ref, len_ref, y_scr):
    _from_chunk_tile(y_ref, y_scr)
    y = jnp.concatenate([y_scr[v] for v in range(S5_WIDTH // LANES)], axis=-1)
    half_y = 0.5 * y
    ya = (half_y + half_y * jnp.tanh(y * (GELU_K + (GELU_K * GELU_A) * (y * y)))).astype(BF16)
    out_a = _dot(ya, wv_ref[...]) * _sigmoid(_dot(ya, wg_ref[...]))
    merged = siga_ref[...].astype(F32) * out_a + sb_ref[...].astype(F32)
    mix = _dot(merged.astype(BF16), wo_ref[...])
    xp = _with_positions(x_ref, remb_ref, cemb_ref)
    x1 = _ln(ALPHA * xp + mod_ref[0:1, :] * mix) * ln1_ref[0:1, :] + ln1_ref[1:2, :]
    x1_ref[...] = x1
    h_hi = (_ln(x1) * mod_ref[2:3, :] + mod_ref[1:2, :]).astype(BF16)
    _route_and_sort(h_hi, wr_ref, br_ref, ut_ref, lt_ref, xs_ref, route_ref, len_ref)


def _route_and_sort(h_hi, wr_ref, br_ref, ut_ref, lt_ref, xs_ref, route_ref, len_ref):
    tm = h_hi.shape[0]
    f32 = F32
    nt = (((1,), (1,)), ((), ()))
    lg = lax.dot_general(wr_ref[...], h_hi, nt, preferred_element_type=f32) + br_ref[...]
    rowi = lax.broadcasted_iota(jnp.int32, (LOGIT_ROWS, tm), 0).astype(f32)
    neg = jnp.float32(-jnp.inf)
    big = jnp.float32(LOGIT_ROWS)
    gl = jnp.where(rowi < N_EXPERT_GROUPS, lg, neg)
    gmax = jnp.max(gl, axis=0, keepdims=True)
    g_idx = jnp.min(jnp.where(gl == gmax, rowi, big), axis=0, keepdims=True)
    p_group = 1.0 / jnp.sum(jnp.exp(gl - gmax), axis=0, keepdims=True)
    e_lo = N_EXPERT_GROUPS + g_idx * EXPERTS_PER_GROUP
    el = jnp.where((rowi >= e_lo) & (rowi < e_lo + EXPERTS_PER_GROUP), lg, neg)
    m1 = jnp.max(el, axis=0, keepdims=True)
    i1 = jnp.min(jnp.where(el == m1, rowi, big), axis=0, keepdims=True)
    el2 = jnp.where(rowi == i1, neg, el)
    m2 = jnp.max(el2, axis=0, keepdims=True)
    i2 = jnp.min(jnp.where(el2 == m2, rowi, big), axis=0, keepdims=True)
    r = jnp.exp(m2 - m1)
    w1 = p_group / (1.0 + r)
    w2 = p_group * r / (1.0 + r)
    e1 = i1 - N_EXPERT_GROUPS
    e2 = i2 - N_EXPERT_GROUPS
    rowa = lax.broadcasted_iota(jnp.int32, (ASSIGN_ROWS, tm), 0).astype(f32)
    a12 = jnp.where(rowa == e1, 1.0, 0.0) + jnp.where(rowa == e2 + N_EXPERTS, 1.0, 0.0)
    rank = _dot(a12.astype(BF16), ut_ref[...])
    cnt = jnp.broadcast_to(jnp.sum(a12, axis=1, keepdims=True), (ASSIGN_ROWS, LANES))
    row = lax.broadcasted_iota(jnp.int32, (ASSIGN_ROWS, LANES), 0)
    tot = cnt + pltpu.roll(cnt, N_EXPERTS, 0)
    run = jnp.where(row < N_EXPERTS, jnp.floor((tot + (PIECE - 1)) * (1.0 / PIECE)), 0.0)
    off = PIECE * _dot(lt_ref[...], run.astype(BF16))
    base = jnp.where(row < N_EXPERTS, off, pltpu.roll(off + cnt, N_EXPERTS, 0))
    posmat = a12 * (rank + base[:, 0:1])
    pos1 = jnp.sum(posmat[0:N_EXPERTS, :], axis=0, keepdims=True)
    pos2 = jnp.sum(posmat[N_EXPERTS:2 * N_EXPERTS, :], axis=0, keepdims=True)
    ri = lax.broadcasted_iota(jnp.int32, (xs_ref.shape[0], tm), 0)
    perm = jnp.where((ri == pos1.astype(jnp.int32)) | (ri == pos2.astype(jnp.int32)), 1.0, 0.0).astype(BF16)
    xs_ref[...] = _dot(perm, h_hi).astype(xs_ref.dtype)
    rowr = lax.broadcasted_iota(jnp.int32, (SUBLANES, tm), 0)
    rec = jnp.zeros((SUBLANES, tm), f32)
    for col, val in ((REC_W1, w1), (REC_W2, w2), (REC_POS1, pos1), (REC_POS2, pos2)):
        rec = jnp.where(rowr == col, val, rec)
    rec = jnp.concatenate([rec, jnp.zeros((ROUTE_LANES - SUBLANES, tm), f32)], axis=0)
    route_ref[...] = rec.T
    len_ref[...] = run


def _local_rows(tm):
    return -(-(TOP_K * tm + N_EXPERTS * (PIECE - 1)) // PIECE) * PIECE


def _mix_out(y, siga, sb, x, r_emb, c_emb, mod, wv, wg, wo, ln1, wr, br, tm):
    b, l, d = x.shape
    n_tiles = l // tm
    xs_rows = _local_rows(tm)
    tok = lambda w: pl.BlockSpec((None, tm, w), lambda i, j: (j, i, 0))
    whole = lambda a: pl.BlockSpec(a.shape, lambda i, j: (0,) * a.ndim)
    ut = (jnp.arange(tm)[:, None] < jnp.arange(tm)[None, :]).astype(BF16)
    lt = (jnp.arange(ASSIGN_ROWS)[None, :] < jnp.arange(ASSIGN_ROWS)[:, None]).astype(BF16)
    return pl.pallas_call(
        _mix_out_kernel,
        grid=(n_tiles, b),
        in_specs=[pl.BlockSpec((tm // CHUNK, CHUNK * S5_WIDTH), lambda i, j: (j * n_tiles + i, 0)),
                  tok(d), tok(d), tok(d)] + _pos_specs(tm, d) + [
                  pl.BlockSpec((None, 4, d), lambda i, j: (j, 0, 0)),
                  whole(wv), whole(wg), whole(wo), whole(ln1), whole(wr), whole(br),
                  whole(ut), whole(lt)],
        out_specs=[tok(d),
                   pl.BlockSpec((xs_rows, d), lambda i, j: (j * n_tiles + i, 0)),
                   tok(ROUTE_LANES),
                   pl.BlockSpec((None, ASSIGN_ROWS, LANES), lambda i, j: (j * n_tiles + i, 0, 0))],
        out_shape=[jax.ShapeDtypeStruct((b, l, d), F32),
                   jax.ShapeDtypeStruct((b * n_tiles * xs_rows, d), BF16),
                   jax.ShapeDtypeStruct((b, l, ROUTE_LANES), F32),
                   jax.ShapeDtypeStruct((b * n_tiles, ASSIGN_ROWS, LANES), F32)],
        scratch_shapes=[pltpu.VMEM((S5_WIDTH // LANES, tm, LANES), F32)],
        compiler_params=pltpu.CompilerParams(
            dimension_semantics=("arbitrary", "arbitrary"), vmem_limit_bytes=VMEM_LIMIT),
        name="mix_out",
    )(y, siga, sb, x, r_emb, c_emb, mod, wv, wg, wo, ln1, wr, br, ut, lt)


def _piece_copy(src_hbm, src_row, dst, piece, sem):
    return pltpu.make_async_copy(src_hbm.at[pl.ds(pl.multiple_of(src_row, PIECE), PIECE), :],
                                 dst.at[pl.ds(pl.multiple_of(piece * PIECE, PIECE), PIECE), :], sem)


def _issue_pieces(src_hbm, table_ref, first, n_pieces, dst, sem):
    for p in range(n_pieces):
        _piece_copy(src_hbm, table_ref[first + p], dst, p, sem).start()


def _wait_pieces(src_hbm, dst, sem):
    pltpu.make_async_copy(src_hbm.at[pl.ds(0, dst.shape[0]), :], dst, sem).wait()


def _experts_kernel(be_ref, piece_ref, nused_ref, xs_hbm, wg_ref, wu_ref, wd_ref, ys_ref,
                    xs_buf0, xs_buf1, wg_bf, wu_bf, wd_bf, sem):
    i = pl.program_id(0)
    n_used = nused_ref[0]
    per_block = ROW_BLOCK // PIECE

    @pl.when(i == 0)
    def _():
        _issue_pieces(xs_hbm, piece_ref, 0, per_block, xs_buf0, sem.at[0])

    def block(cur, cur_sem, oth, oth_sem):
        @pl.when((i == 0) | (be_ref[i] != be_ref[jnp.maximum(i - 1, 0)]))
        def _():
            wg_bf[...] = wg_ref[...].astype(BF16)
            wu_bf[...] = wu_ref[...].astype(BF16)
            wd_bf[...] = wd_ref[...].astype(BF16)

        _wait_pieces(xs_hbm, cur, cur_sem)
        nxt = jnp.minimum(i + 1, n_used - 1)
        _issue_pieces(xs_hbm, piece_ref, nxt * per_block, per_block, oth, oth_sem)

        xb = cur[...]
        gate = _dot(xb, wg_bf[...])
        up = _dot(xb, wu_bf[...])
        hid = (gate * _sigmoid(gate) * up).astype(BF16)
        ys_ref[...] = _dot(hid, wd_bf[...]).astype(ys_ref.dtype)

        @pl.when(i == n_used - 1)
        def _():
            _wait_pieces(xs_hbm, oth, oth_sem)

    for s, (cur, oth) in enumerate(((xs_buf0, xs_buf1), (xs_buf1, xs_buf0))):
        @pl.when((i < n_used) & (i % 2 == s))
        def _(s=s, cur=cur, oth=oth):
            block(cur, sem.at[s], oth, sem.at[1 - s])

    @pl.when(i >= n_used)
    def _():
        ys_ref[...] = jnp.zeros(ys_ref.shape, ys_ref.dtype)


def _experts(block_e, piece_src, n_used, xs, wg, wu, wd, n_blocks):
    d = xs.shape[1]
    by_expert = lambda i, be, ps, nu: (0, be[i], 0, 0)
    grid_spec = pltpu.PrefetchScalarGridSpec(
        num_scalar_prefetch=3,
        grid=(n_blocks,),
        in_specs=[pl.BlockSpec(memory_space=pl.ANY),
                  pl.BlockSpec((None, None, d, EXPERT_FF), by_expert),
                  pl.BlockSpec((None, None, d, EXPERT_FF), by_expert),
                  pl.BlockSpec((None, None, EXPERT_FF, d), by_expert)],
        out_specs=pl.BlockSpec((ROW_BLOCK, d), lambda i, be, ps, nu: (i, 0)),
        scratch_shapes=[pltpu.VMEM((ROW_BLOCK, d), BF16), pltpu.VMEM((ROW_BLOCK, d), BF16),
                        pltpu.VMEM((d, EXPERT_FF), BF16), pltpu.VMEM((d, EXPERT_FF), BF16),
                        pltpu.VMEM((EXPERT_FF, d), BF16), pltpu.SemaphoreType.DMA((2,))],
    )
    return pl.pallas_call(
        _experts_kernel,
        grid_spec=grid_spec,
        out_shape=jax.ShapeDtypeStruct((n_blocks * ROW_BLOCK, d), BF16),
        compiler_params=pltpu.CompilerParams(
            dimension_semantics=("arbitrary",), vmem_limit_bytes=VMEM_LIMIT),
        name="experts",
    )(block_e, piece_src, n_used, xs, wg, wu, wd)


def _combine_kernel(piece_ref, ys_hbm, x1_ref, route_ref, mod_ref, ln2_ref, o_ref, buf, sem):
    i = pl.program_id(0)
    n = pl.num_programs(0)
    slot = i % 2
    rows = buf.shape[1]
    per_tile = rows // PIECE

    def issue(tile, s):
        def body(p, _):
            _piece_copy(ys_hbm, piece_ref[tile * per_tile + p], buf.at[s], p, sem.at[s]).start()
            return 0
        lax.fori_loop(0, per_tile, body, 0)

    @pl.when(i == 0)
    def _():
        issue(0, 0)

    _wait_pieces(ys_hbm, buf.at[slot], sem.at[slot])

    @pl.when(i + 1 < n)
    def _():
        issue(i + 1, 1 - slot)

    tm = x1_ref.shape[0]
    lane = lax.broadcasted_iota(jnp.int32, (tm, rows), 1)
    rec = lambda col: route_ref[:, col:col + 1]
    sel = jnp.where(lane == rec(REC_POS1).astype(jnp.int32), rec(REC_W1),
                    jnp.where(lane == rec(REC_POS2).astype(jnp.int32), rec(REC_W2), 0.0)).astype(BF16)
    moe = _dot(sel, buf[slot])
    z = ALPHA * x1_ref[...] + mod_ref[0:1, :] * moe
    o_ref[...] = _ln(z) * ln2_ref[0:1, :] + ln2_ref[1:2, :]


def _combine(piece_glob, ys, x1, route, mod, ln2, tm, tiles_per_batch):
    t, d = x1.shape
    grid_spec = pltpu.PrefetchScalarGridSpec(
        num_scalar_prefetch=1,
        grid=(t // tm,),
        in_specs=[pl.BlockSpec(memory_space=pl.ANY),
                  pl.BlockSpec((tm, d), lambda i, pg: (i, 0)),
                  pl.BlockSpec((tm, ROUTE_LANES), lambda i, pg: (i, 0)),
                  pl.BlockSpec((None, SUBLANES, d), lambda i, pg: (i // tiles_per_batch, 0, 0)),
                  pl.BlockSpec((2, d), lambda i, pg: (0, 0))],
        out_specs=pl.BlockSpec((tm, d), lambda i, pg: (i, 0)),
        scratch_shapes=[pltpu.VMEM((2, _local_rows(tm), d), BF16), pltpu.SemaphoreType.DMA((2,))],
    )
    return pl.pallas_call(
        _combine_kernel,
        grid_spec=grid_spec,
        out_shape=jax.ShapeDtypeStruct((t, d), F32),
        compiler_params=pltpu.CompilerParams(
            dimension_semantics=("arbitrary",), vmem_limit_bytes=VMEM_LIMIT),
        name="combine",
    )(piece_glob, ys, x1, route, mod, ln2)


def _sincos_2d(rows, cols, dim):
    q = dim // 4
    omega = 1.0 / (POS_BASE ** (jnp.arange(q, dtype=F32) / q))
    r = jnp.arange(rows, dtype=F32)[:, None] * omega
    cl = jnp.arange(cols, dtype=F32)[:, None] * omega
    r_emb = jnp.concatenate([jnp.sin(r), jnp.cos(r)], -1)
    c_emb = jnp.concatenate([jnp.sin(cl), jnp.cos(cl)], -1)
    return r_emb, c_emb


def _routing_tables(run_pieces, xs_rows, n_blocks):
    i32 = jnp.int32
    n_tiles = run_pieces.shape[0]
    ppb = ROW_BLOCK // PIECE
    loc_start = jnp.cumsum(run_pieces, axis=1) - run_pieces
    seg_tot = jnp.sum(run_pieces, axis=0)
    seg_pad = (seg_tot + ppb - 1) // ppb * ppb
    seg_end = jnp.cumsum(seg_pad)
    seg_start = seg_end - seg_pad
    run_t = run_pieces.T
    glob_start = seg_start[:, None] + jnp.cumsum(run_t, axis=1) - run_t
    n_used = (seg_end[-1] // ppb).astype(i32)
    blk = jnp.minimum(jnp.arange(n_blocks, dtype=i32), n_used - 1)
    block_e = jnp.minimum(jnp.sum((seg_end[None, :] <= (blk * ppb)[:, None]).astype(i32), axis=1),
                          N_EXPERTS - 1).astype(i32)
    lpt = xs_rows // PIECE
    starts = glob_start.reshape(-1)
    lens = run_t.reshape(-1)
    src0 = (jnp.arange(n_tiles, dtype=i32)[None, :] * lpt + loc_start.T).reshape(-1)
    p = jnp.arange(n_blocks * ppb, dtype=i32)
    within = p[:, None] - starts[None, :]
    hit = (within >= 0) & (within < lens[None, :])
    piece_src = jnp.sum(jnp.where(hit, (src0[None, :] + within) * PIECE, 0), axis=1).astype(i32)
    s = jnp.arange(lpt, dtype=i32)
    loc_within = s[None, :, None] - loc_start[:, None, :]
    hit = (loc_within >= 0) & (loc_within < run_pieces[:, None, :])
    piece_glob = jnp.sum(jnp.where(hit, (glob_start.T[:, None, :] + loc_within) * PIECE, 0), axis=2)
    return block_e, piece_src, piece_glob.astype(i32).reshape(-1), n_used.reshape(1)


def kernel(x, c, ctx, c_ctx, w_ada, b_ada, w_in, s5_log_dt_f, s5_a_re_f, s5_a_im_f, s5_b_re_f, s5_b_im_f, s5_c_re_f, s5_c_im_f, s5_log_dt_b, s5_a_re_b, s5_a_im_b, s5_b_re_b, s5_b_im_b, s5_c_re_b, s5_c_im_b, s5_d, s5_w_glu_val, s5_w_glu_gate, conv_w, conv_w_out, w_o, ln1_g, ln1_b, router_w_group, router_b_group, router_w_expert, router_b_expert, exp_w_gate, exp_w_up, exp_w_down, ln2_g, ln2_b):
    b, l, d = x.shape
    lc = ctx.shape[1]
    assert d == D_MODEL and b < SUBLANES and w_ada.shape[0] == DEPTH
    assert l % (SUBLANES * CHUNK) == 0 and lc % (SUBLANES * CHUNK) == 0 and l % GRID_W == 0
    t = b * l
    tm = min(TOKEN_TILE, l)
    tmc = min(TOKEN_TILE, lc)

    cc = jnp.concatenate([c, c_ctx[None, :], jnp.zeros((SUBLANES - b - 1, d), F32)], 0)
    mods = _mods(cc, w_ada[0], b_ada[0])
    sh1, sc1, g1, sh2, sc2, g2 = jnp.split(mods, 6, axis=-1)
    mod_a = jnp.stack([sh1[:b], 1.0 + sc1[:b]], 1)
    mod_ctx = jnp.broadcast_to(jnp.stack([sh1[b], 1.0 + sc1[b]], 0)[None], (b, 2, d))
    mod_c = jnp.stack([g1[:b], sh2[:b], 1.0 + sc2[:b], jnp.zeros((b, d), F32)], 1)
    mod_f = jnp.concatenate([g2[:b, None, :], jnp.zeros((b, SUBLANES - 1, d), F32)], 1)

    w_in_bf = w_in[0].astype(BF16)
    both = lambda fwd, bwd: jnp.concatenate([fwd, bwd], 0)
    s5_tab = _s5_tables(both(s5_log_dt_f, s5_log_dt_b), both(s5_a_re_f, s5_a_re_b), both(s5_a_im_f, s5_a_im_b),
                        both(s5_b_re_f, s5_b_re_b), both(s5_b_im_f, s5_b_im_b),
                        both(s5_c_re_f, s5_c_re_b), both(s5_c_im_f, s5_c_im_b))
    mi, ws, wo_s5, tab = _s5_operators(s5_tab, s5_d[0])

    (uc_ctx,) = _in_proj(ctx, jnp.zeros((lc // GRID_W, d // 2), F32), jnp.zeros((GRID_W, d // 2), F32),
                         mod_ctx, w_in_bf, None, None, tmc, False)
    zero_state = jnp.zeros((N_PAIRS, 4, SUBLANES, LANES), F32)
    s0 = _s5_scan(uc_ctx, None, ws, None, tab, zero_state, b, emit_y=False)

    r_emb, c_emb = _sincos_2d(l // GRID_W, GRID_W, d)
    uc, siga, sb = _in_proj(x, r_emb, c_emb, mod_a, w_in_bf, conv_w[0], conv_w_out[0].astype(BF16), tm, True)
    y, _ = _s5_scan(uc, mi, ws, wo_s5, tab, s0, b)

    wr = jnp.concatenate([router_w_group[0], router_w_expert[0],
                          jnp.zeros((d, LOGIT_ROWS - N_EXPERT_GROUPS - N_EXPERTS), F32)], 1).T.astype(BF16)
    br = jnp.concatenate([router_b_group[0], router_b_expert[0],
                          jnp.zeros((LOGIT_ROWS - N_EXPERT_GROUPS - N_EXPERTS,), F32)])[:, None]
    ln1 = jnp.stack([ln1_g[0], ln1_b[0]], 0)
    x1, xs, route, run_len = _mix_out(y, siga, sb, x, r_emb, c_emb, mod_c,
                                      s5_w_glu_val[0].astype(BF16), s5_w_glu_gate[0].astype(BF16),
                                      w_o[0].astype(BF16), ln1, wr, br, tm)

    x1 = x1.reshape(t, d)
    route = route.reshape(t, ROUTE_LANES)
    n_tiles = t // tm
    xs_rows = _local_rows(tm)
    run_pieces = run_len[:, :N_EXPERTS, 0].astype(jnp.int32)
    max_rows = t * TOP_K + n_tiles * N_EXPERTS * (PIECE - 1) + N_EXPERTS * (ROW_BLOCK - 1)
    n_blocks = -(-max_rows // ROW_BLOCK)
    block_e, piece_src, piece_glob, n_used = _routing_tables(run_pieces, xs_rows, n_blocks)
    ys = _experts(block_e, piece_src, n_used, xs, exp_w_gate, exp_w_up, exp_w_down, n_blocks)
    ln2 = jnp.stack([ln2_g[0], ln2_b[0]], 0)
    out = _combine(piece_glob, ys, x1, route, mod_f, ln2, tm, l // tm)
    return out.reshape(b, l, d)
```

```python
import functools
import math

import jax
import jax.numpy as jnp
from jax import lax
from jax.experimental import pallas as pl
from jax.experimental.pallas import tpu as pltpu

F32 = jnp.float32
BF16 = jnp.bfloat16
HI = lax.Precision.HIGHEST

D_MODEL = 1024
GRID_W = 64
S5_WIDTH = 512
S5_GROUP_CH = 16
S5_GROUPS = S5_WIDTH // S5_GROUP_CH
S5_STATE = 64
CONV_WIDTH = 512
N_EXPERT_GROUPS = 4
EXPERTS_PER_GROUP = 8
N_EXPERTS = N_EXPERT_GROUPS * EXPERTS_PER_GROUP
EXPERT_FF = 512
TOP_K = 2
DEPTH = 1
ALPHA = (2.0 * DEPTH) ** 0.25
LN_EPS = 1e-6
POS_BASE = 10000.0
GELU_K = math.sqrt(2.0 / math.pi)
GELU_A = 0.044715

LANES = 128
SUBLANES = 8
CHUNK = 16
GROUP_W = CHUNK * S5_GROUP_CH
PAIR_W = 2 * GROUP_W
N_PAIRS = S5_GROUPS // 2
TOK_PER_VREG = LANES // S5_GROUP_CH
TAB_ROWS = 24
TAB_POW = 2 * SUBLANES
MODS_COLS = 1536
ROUTE_LANES = 128
LOGIT_ROWS = 48
ASSIGN_ROWS = TOP_K * N_EXPERTS
REC_W1, REC_W2, REC_POS1, REC_POS2 = range(4)
TOKEN_TILE = 512
ROW_BLOCK = 512
PIECE = 16
VMEM_LIMIT = 56 * 1024 * 1024


def _ln(x):
    mu = jnp.mean(x, axis=-1, keepdims=True)
    xc = x - mu
    var = jnp.mean(xc * xc, axis=-1, keepdims=True)
    return xc * lax.rsqrt(var + LN_EPS)


def _sigmoid(x):
    return 0.5 * (jnp.tanh(0.5 * x) + 1.0)


def _dot(a, b):
    return jnp.dot(a, b, preferred_element_type=F32)


def _mods_kernel(c_ref, w_ref, b_ref, o_ref):
    c = c_ref[...]
    a = c * _sigmoid(c)
    o_ref[...] = jnp.dot(a, w_ref[...], precision=HI, preferred_element_type=F32) + b_ref[...]


def _mods(cc, w_ada, b_ada):
    n = w_ada.shape[1]
    nb = MODS_COLS
    return pl.pallas_call(
        _mods_kernel,
        grid=(n // nb,),
        in_specs=[pl.BlockSpec((SUBLANES, D_MODEL), lambda i: (0, 0)),
                  pl.BlockSpec((D_MODEL, nb), lambda i: (0, i)),
                  pl.BlockSpec((1, nb), lambda i: (0, i))],
        out_specs=pl.BlockSpec((SUBLANES, nb), lambda i: (0, i)),
        out_shape=jax.ShapeDtypeStruct((SUBLANES, n), F32),
        compiler_params=pltpu.CompilerParams(vmem_limit_bytes=VMEM_LIMIT),
        name="mods",
    )(cc, w_ada, b_ada.reshape(1, n))


def _slot_masks(rows):
    slot = lax.broadcasted_iota(jnp.int32, (rows, LANES), 1) // S5_GROUP_CH
    return [slot == s for s in range(TOK_PER_VREG)]


def _to_chunk_tile(u_scr, uc_ref):
    nch = uc_ref.shape[0]
    masks = _slot_masks(nch)
    for qh in range(CHUNK // TOK_PER_VREG):
        for v in range(S5_WIDTH // LANES):
            src = [u_scr[v, pl.ds(qh * TOK_PER_VREG + s, nch, stride=CHUNK), :] for s in range(TOK_PER_VREG)]
            for i in range(TOK_PER_VREG):
                acc = None
                for s in range(TOK_PER_VREG):
                    shift = ((s - i) * S5_GROUP_CH) % LANES
                    piece = pltpu.roll(src[s], shift, 1) if shift else src[s]
                    acc = piece if acc is None else jnp.where(masks[s], piece, acc)
                lo = (v * TOK_PER_VREG + i) * GROUP_W + qh * LANES
                uc_ref[:, lo:lo + LANES] = acc.astype(uc_ref.dtype)


def _from_chunk_tile(yc_ref, y_scr):
    nch = yc_ref.shape[0]
    masks = _slot_masks(nch)
    for qh in range(CHUNK // TOK_PER_VREG):
        for v in range(S5_WIDTH // LANES):
            src = []
            for i in range(TOK_PER_VREG):
                lo = (v * TOK_PER_VREG + i) * GROUP_W + qh * LANES
                src.append(yc_ref[:, lo:lo + LANES].astype(F32))
            for s in range(TOK_PER_VREG):
                acc = None
                for i in range(TOK_PER_VREG):
                    shift = ((i - s) * S5_GROUP_CH) % LANES
                    piece = pltpu.roll(src[i], shift, 1) if shift else src[i]
                    acc = piece if acc is None else jnp.where(masks[i], piece, acc)
                y_scr[v, pl.ds(qh * TOK_PER_VREG + s, nch, stride=CHUNK), :] = acc


def _with_positions(x_ref, remb_ref, cemb_ref):
    c = cemb_ref[...]
    slabs = []
    for j in range(x_ref.shape[0] // GRID_W):
        r = jnp.broadcast_to(remb_ref[j:j + 1, :], c.shape)
        slabs.append(x_ref[j * GRID_W:(j + 1) * GRID_W, :] + jnp.concatenate([r, c], axis=-1))
    return jnp.concatenate(slabs, axis=0)


def _in_proj_kernel(x_ref, remb_ref, cemb_ref, mod_ref, w_ref, *rest, full):
    if full:
        cw_ref, cwo_ref, uc_ref, siga_ref, sb_ref, u_scr = rest
    else:
        uc_ref, u_scr = rest
    xp = _with_positions(x_ref, remb_ref, cemb_ref)
    h = (_ln(xp) * mod_ref[1:2, :] + mod_ref[0:1, :]).astype(BF16)
    o1 = S5_WIDTH
    o2, o3, o4 = o1 + CONV_WIDTH, o1 + 2 * CONV_WIDTH, o1 + 3 * CONV_WIDTH
    o5 = o4 + D_MODEL
    u = _dot(h, w_ref[:, 0:o1])
    for v in range(S5_WIDTH // LANES):
        u_scr[v] = u[:, v * LANES:(v + 1) * LANES]
    _to_chunk_tile(u_scr, uc_ref)
    if not full:
        return
    z_b = _dot(h, w_ref[:, o1:o2])
    gate_c = _dot(h, w_ref[:, o3:o4])
    p = gate_c * z_b
    tm = p.shape[0]
    col = lax.broadcasted_iota(jnp.int32, (tm, 1), 0) % GRID_W
    prev = jnp.where(col == 0, 0.0, pltpu.roll(p, 1, 0))
    nxt = jnp.where(col == GRID_W - 1, 0.0, pltpu.roll(p, tm - 1, 0))
    v = cw_ref[0:1, :] * prev + cw_ref[1:2, :] * p + cw_ref[2:3, :] * nxt
    gate_b = _dot(h, w_ref[:, o2:o3])
    out_b = _dot((gate_b * v).astype(BF16), cwo_ref[...])
    merge_b = _dot(h, w_ref[:, o5:])
    sb_ref[...] = (_sigmoid(merge_b) * out_b).astype(sb_ref.dtype)
    merge_a = _dot(h, w_ref[:, o4:o5])
    siga_ref[...] = _sigmoid(merge_a).astype(siga_ref.dtype)


def _pos_specs(tm, d):
    return [pl.BlockSpec((tm // GRID_W, d // 2), lambda i, j: (i, 0)),
            pl.BlockSpec((GRID_W, d // 2), lambda i, j: (0, 0))]


def _in_proj(x, r_emb, c_emb, mod, w_in_bf, conv_w, conv_w_out_bf, tm, full):
    b, l, d = x.shape
    n_tiles = l // tm
    grid = (n_tiles, b)
    tok = lambda w: pl.BlockSpec((None, tm, w), lambda i, j: (j, i, 0))
    chunk_spec = pl.BlockSpec((tm // CHUNK, CHUNK * S5_WIDTH), lambda i, j: (j * n_tiles + i, 0))
    chunk_shape = jax.ShapeDtypeStruct((b * l // CHUNK, CHUNK * S5_WIDTH), BF16)
    in_specs = [tok(d)] + _pos_specs(tm, d) + [pl.BlockSpec((None, 2, d), lambda i, j: (j, 0, 0))]
    args = [x, r_emb, c_emb, mod]
    if full:
        in_specs += [pl.BlockSpec(w_in_bf.shape, lambda i, j: (0, 0)),
                     pl.BlockSpec(conv_w.shape, lambda i, j: (0, 0)),
                     pl.BlockSpec(conv_w_out_bf.shape, lambda i, j: (0, 0))]
        args += [w_in_bf, conv_w, conv_w_out_bf]
        out_specs = [chunk_spec, tok(d), tok(d)]
        out_shape = [chunk_shape,
                     jax.ShapeDtypeStruct((b, l, d), BF16),
                     jax.ShapeDtypeStruct((b, l, d), BF16)]
    else:
        in_specs += [pl.BlockSpec((d, S5_WIDTH), lambda i, j: (0, 0))]
        args += [w_in_bf]
        out_specs = [chunk_spec]
        out_shape = [chunk_shape]
    return pl.pallas_call(
        functools.partial(_in_proj_kernel, full=full),
        grid=grid, in_specs=in_specs, out_specs=out_specs, out_shape=out_shape,
        scratch_shapes=[pltpu.VMEM((S5_WIDTH // LANES, tm, LANES), F32)],
        compiler_params=pltpu.CompilerParams(
            dimension_semantics=("arbitrary", "arbitrary"), vmem_limit_bytes=VMEM_LIMIT),
        name="in_proj" if full else "in_proj_ctx",
    )(*args)


def _s5_tables(log_dt, a_re, a_im, b_re, b_im, c_re, c_im):
    f32 = F32
    dt = jnp.exp(log_dt.astype(f32))[..., None]
    a_re = a_re.astype(f32)
    a_im = a_im.astype(f32)
    mag = jnp.exp(dt * a_re)
    ab_re = mag * jnp.cos(dt * a_im)
    ab_im = mag * jnp.sin(dt * a_im)
    den = a_re * a_re + a_im * a_im
    x_re = ab_re - 1.0
    f_re = (x_re * a_re + ab_im * a_im) / den
    f_im = (ab_im * a_re - x_re * a_im) / den
    b_re = b_re.astype(f32)
    b_im = b_im.astype(f32)
    bb_re = f_re[..., None] * b_re - f_im[..., None] * b_im
    bb_im = f_re[..., None] * b_im + f_im[..., None] * b_re
    k = jnp.arange(CHUNK + 1, dtype=f32)[None, :, None, None]
    pmag = jnp.exp(k * (dt * a_re)[:, None])
    p_re = pmag * jnp.cos(k * (dt * a_im)[:, None])
    p_im = pmag * jnp.sin(k * (dt * a_im)[:, None])
    pb_re = p_re[..., None] * bb_re[:, None] - p_im[..., None] * bb_im[:, None]
    pb_im = p_re[..., None] * bb_im[:, None] + p_im[..., None] * bb_re[:, None]
    c_re = c_re.astype(f32)[:, None]
    c_im = c_im.astype(f32)[:, None]
    cp_re = c_re * p_re[:, :, :, None, :] - c_im * p_im[:, :, :, None, :]
    cp_im = -(c_re * p_im[:, :, :, None, :] + c_im * p_re[:, :, :, None, :])
    return dict(p_re=p_re, p_im=p_im, pb_re=pb_re, pb_im=pb_im, cp_re=cp_re, cp_im=cp_im,
                bb_re=bb_re, bb_im=bb_im)


def _lag_kernels(t):
    g, n, c = S5_GROUPS, S5_STATE, S5_GROUP_CH
    k = CHUNK + 1
    lhs = jnp.concatenate([t['cp_re'], t['cp_im']], -1)
    lhs = lhs.transpose(0, 2, 1, 3, 4).reshape(2 * g, k * c, 2 * n)
    rhs = jnp.concatenate([t['bb_re'], t['bb_im']], -2).reshape(2 * g, 2 * n, c)
    out = jnp.einsum('bmn,bnc->bmc', lhs, rhs, precision=HI)
    out = out.reshape(2, g, k, c, c).transpose(0, 2, 1, 4, 3)
    return out[0], out[1]


def _s5_operators(t, s5_d):
    q = CHUNK
    g, n, c = S5_GROUPS, S5_STATE, S5_GROUP_CH
    kern_f, kern_b = _lag_kernels(t)
    k0 = kern_f[0] + kern_b[0] + s5_d.astype(F32)[:, :, None] * jnp.eye(c, dtype=F32)[None]
    kc = jnp.concatenate([kern_b[1:q][::-1], k0[None], kern_f[1:q]], 0)
    kct = kc.transpose(1, 2, 0, 3)
    m_intra = jnp.stack([kct[:, :, q - 1 - i:2 * q - 1 - i, :] for i in range(q)], 1)
    m_intra = m_intra.reshape(g, q * c, q * c)
    w_st = jnp.stack([t['pb_re'][0, :q][::-1], t['pb_im'][0, :q][::-1],
                      t['pb_re'][1, :q], t['pb_im'][1, :q]], 0)
    w_st = w_st.transpose(2, 1, 4, 0, 3).reshape(g, q * c, 4, n)
    w_out = jnp.stack([t['cp_re'][0, 1:], t['cp_im'][0, 1:],
                       t['cp_re'][1, 1:][::-1], t['cp_im'][1, 1:][::-1]], 0)
    w_out = w_out.transpose(2, 0, 4, 1, 3).reshape(g, 4, n, q * c)
    np_ = N_PAIRS
    w_st = w_st.astype(BF16).reshape(np_, 2, q * c, 4, n)
    ws_pair = jnp.concatenate([jnp.pad(w_st[:, 0], ((0, 0), (0, 0), (0, 0), (0, n))),
                               jnp.pad(w_st[:, 1], ((0, 0), (0, 0), (0, 0), (n, 0)))], 1)
    ws_pair = ws_pair.reshape(np_, PAIR_W, 4 * 2 * n)
    w_out = w_out.astype(BF16).reshape(np_, 2, 4, n, q * c)
    wo_pair = jnp.stack([jnp.pad(w_out[:, 0], ((0, 0), (0, 0), (0, 0), (0, q * c))),
                         jnp.pad(w_out[:, 1], ((0, 0), (0, 0), (0, 0), (q * c, 0)))], 2)
    wo_pair = wo_pair.reshape(np_, 4 * 2 * n, PAIR_W)
    tab = _chunk_power_table(t).reshape(2 * TAB_ROWS, np_, 2 * n).transpose(1, 0, 2)
    return m_intra.astype(BF16), ws_pair, wo_pair, tab


def _chunk_power_table(t):
    def cmul(x, y):
        return x[0] * y[0] - x[1] * y[1], x[0] * y[1] + x[1] * y[0]
    p1 = (t['p_re'][:, CHUNK], t['p_im'][:, CHUNK])
    p2 = cmul(p1, p1)
    p4 = cmul(p2, p2)
    p8 = cmul(p4, p4)
    pr = [(jnp.ones_like(p1[0]), jnp.zeros_like(p1[0]))]
    for _ in range(SUBLANES - 1):
        pr.append(cmul(pr[-1], p1))
    pr_re = jnp.stack([p[0] for p in pr], 0)
    pr_im = jnp.stack([p[1] for p in pr], 0)
    pw = jnp.stack([p1[0], p1[1], p2[0], p2[1], p4[0], p4[1], p8[0], p8[1]], 0)
    return jnp.concatenate([pr_re[:, 0], pr_im[:, 0], pw[:, 0],
                            pr_re[::-1, 1], pr_im[::-1, 1], pw[:, 1]], 0)


def _s5_scan_kernel(uc_ref, *rest, batch, emit_y):
    if emit_y:
        mi_ref, ws_ref, wo_ref, tab_ref, s0_ref, y_ref, fin_ref, s_scr, in_scr = rest
    else:
        ws_ref, tab_ref, s0_ref, fin_ref, s_scr = rest
    rows = uc_ref.shape[0]
    chunks = rows // batch
    n_tiles = chunks // SUBLANES
    u = uc_ref[...]
    s_scr[...] = _dot(u, ws_ref[...])
    row = lax.broadcasted_iota(jnp.int32, (SUBLANES, LANES), 0)

    def tile_scan(r0, backward, c_re, c_im):
        base = TAB_ROWS if backward else 0
        col = 2 * LANES if backward else 0
        rs = pl.ds(r0, SUBLANES)

        def shift(z, k):
            if backward:
                return jnp.where(row < SUBLANES - k, pltpu.roll(z, SUBLANES - k, 0), 0.0)
            return jnp.where(row >= k, pltpu.roll(z, k, 0), 0.0)

        z_re = s_scr[rs, col:col + LANES]
        z_im = s_scr[rs, col + LANES:col + 2 * LANES]
        for k, t in ((1, TAB_POW), (2, TAB_POW + 2), (4, TAB_POW + 4)):
            a_re = tab_ref[base + t:base + t + 1, :]
            a_im = tab_ref[base + t + 1:base + t + 2, :]
            sh_re = shift(z_re, k)
            sh_im = shift(z_im, k)
            z_re, z_im = z_re + (a_re * sh_re - a_im * sh_im), z_im + (a_re * sh_im + a_im * sh_re)
        pr_re = tab_ref[base:base + SUBLANES, :]
        pr_im = tab_ref[base + SUBLANES:base + 2 * SUBLANES, :]
        if emit_y:
            in_scr[rs, col:col + LANES] = pr_re * c_re - pr_im * c_im + shift(z_re, 1)
            in_scr[rs, col + LANES:col + 2 * LANES] = pr_re * c_im + pr_im * c_re + shift(z_im, 1)
        last = 0 if backward else SUBLANES - 1
        l_re = jnp.broadcast_to(z_re[last:last + 1, :], (SUBLANES, LANES))
        l_im = jnp.broadcast_to(z_im[last:last + 1, :], (SUBLANES, LANES))
        p8_re = tab_ref[base + TAB_POW + 6:base + TAB_POW + 7, :]
        p8_im = tab_ref[base + TAB_POW + 7:base + TAB_POW + 8, :]
        return p8_re * c_re - p8_im * c_im + l_re, p8_re * c_im + p8_im * c_re + l_im

    def body(m, carry):
        out = []
        for b in range(batch):
            cf_re, cf_im, cb_re, cb_im = carry[4 * b:4 * b + 4]
            rf = pl.multiple_of(b * chunks + m * SUBLANES, SUBLANES)
            rb = pl.multiple_of(b * chunks + (n_tiles - 1 - m) * SUBLANES, SUBLANES)
            out += list(tile_scan(rf, False, cf_re, cf_im))
            out += list(tile_scan(rb, True, cb_re, cb_im))
        return tuple(out)

    init = tuple(jnp.broadcast_to(s0_ref[t, b:b + 1, :], (SUBLANES, LANES))
                 for b in range(batch) for t in range(4))
    fin = lax.fori_loop(0, n_tiles, body, init, unroll=min(4, n_tiles))
    fin_ref[...] = jnp.zeros(fin_ref.shape, F32)
    for b in range(batch):
        for t in range(4):
            fin_ref[t, b:b + 1, :] = fin[4 * b + t][0:1, :]
    if emit_y:
        y_intra = jnp.concatenate(
            [_dot(u[:, gl * GROUP_W:(gl + 1) * GROUP_W], mi_ref[gl]) for gl in range(2)], axis=-1)
        y = y_intra + _dot(in_scr[...].astype(BF16), wo_ref[...])
        y_ref[...] = y.astype(y_ref.dtype)


def _s5_scan(uc, mi, ws, wo, tab, s0, batch, emit_y=True):
    rows = uc.shape[0]
    pair = lambda *shape: pl.BlockSpec((None,) + shape, lambda p: (p,) + (0,) * len(shape))
    uc_spec = pl.BlockSpec((rows, PAIR_W), lambda p: (0, p))
    state_spec = pair(4, SUBLANES, LANES)
    state_shape = jax.ShapeDtypeStruct((N_PAIRS, 4, SUBLANES, LANES), F32)
    scratch = [pltpu.VMEM((rows, PAIR_W), F32)]
    if emit_y:
        in_specs = [uc_spec, pl.BlockSpec((2, GROUP_W, GROUP_W), lambda p: (p, 0, 0)),
                    pair(PAIR_W, PAIR_W), pair(PAIR_W, PAIR_W), pair(2 * TAB_ROWS, LANES), state_spec]
        args = (uc, mi, ws, wo, tab, s0)
        out_specs = [uc_spec, state_spec]
        out_shape = [jax.ShapeDtypeStruct((rows, N_PAIRS * PAIR_W), BF16), state_shape]
        scratch = scratch * 2
    else:
        in_specs = [uc_spec, pair(PAIR_W, PAIR_W), pair(2 * TAB_ROWS, LANES), state_spec]
        args = (uc, ws, tab, s0)
        out_specs = state_spec
        out_shape = state_shape
    return pl.pallas_call(
        functools.partial(_s5_scan_kernel, batch=batch, emit_y=emit_y),
        grid=(N_PAIRS,), in_specs=in_specs, out_specs=out_specs, out_shape=out_shape,
        scratch_shapes=scratch,
        compiler_params=pltpu.CompilerParams(
            dimension_semantics=("arbitrary",), vmem_limit_bytes=VMEM_LIMIT),
        name="s5_scan" if emit_y else "s5_scan_ctx",
    )(*args)


def _mix_out_kernel(y_ref, siga_ref, sb_ref, x_ref, remb_ref, cemb_ref, mod_ref, wv_ref, wg_ref, wo_ref,
                    ln1_ref, wr_ref, br_ref, ut_ref, lt_ref,
                    x1_ref, xs_ref, route_ref, len_ref, y_scr):
    _from_chunk_tile(y_ref, y_scr)
    y = jnp.concatenate([y_scr[v] for v in range(S5_WIDTH // LANES)], axis=-1)
    half_y = 0.5 * y
    ya = (half_y + half_y * jnp.tanh(y * (GELU_K + (GELU_K * GELU_A) * (y * y)))).astype(BF16)
    out_a = _dot(ya, wv_ref[...]) * _sigmoid(_dot(ya, wg_ref[...]))
    merged = siga_ref[...].astype(F32) * out_a + sb_ref[...].astype(F32)
    mix = _dot(merged.astype(BF16), wo_ref[...])
    xp = _with_positions(x_ref, remb_ref, cemb_ref)
    x1 = _ln(ALPHA * xp + mod_ref[0:1, :] * mix) * ln1_ref[0:1, :] + ln1_ref[1:2, :]
    x1_ref[...] = x1
    h_hi = (_ln(x1) * mod_ref[2:3, :] + mod_ref[1:2, :]).astype(BF16)
    _route_and_sort(h_hi, wr_ref, br_ref, ut_ref, lt_ref, xs_ref, route_ref, len_ref)


def _route_and_sort(h_hi, wr_ref, br_ref, ut_ref, lt_ref, xs_ref, route_ref, len_ref):
    tm = h_hi.shape[0]
    f32 = F32
    nt = (((1,), (1,)), ((), ()))
    lg = lax.dot_general(wr_ref[...], h_hi, nt, preferred_element_type=f32) + br_ref[...]
    rowi = lax.broadcasted_iota(jnp.int32, (LOGIT_ROWS, tm), 0).astype(f32)
    neg = jnp.float32(-jnp.inf)
    big = jnp.float32(LOGIT_ROWS)
    gl = jnp.where(rowi < N_EXPERT_GROUPS, lg, neg)
    gmax = jnp.max(gl, axis=0, keepdims=True)
    g_idx = jnp.min(jnp.where(gl == gmax, rowi, big), axis=0, keepdims=True)
    p_group = 1.0 / jnp.sum(jnp.exp(gl - gmax), axis=0, keepdims=True)
    e_lo = N_EXPERT_GROUPS + g_idx * EXPERTS_PER_GROUP
    el = jnp.where((rowi >= e_lo) & (rowi < e_lo + EXPERTS_PER_GROUP), lg, neg)
    m1 = jnp.max(el, axis=0, keepdims=True)
    i1 = jnp.min(jnp.where(el == m1, rowi, big), axis=0, keepdims=True)
    el2 = jnp.where(rowi == i1, neg, el)
    m2 = jnp.max(el2, axis=0, keepdims=True)
    i2 = jnp.min(jnp.where(el2 == m2, rowi, big), axis=0, keepdims=True)
    r = jnp.exp(m2 - m1)
    w1 = p_group / (1.0 + r)
    w2 = p_group * r / (1.0 + r)
    e1 = i1 - N_EXPERT_GROUPS
    e2 = i2 - N_EXPERT_GROUPS
    rowa = lax.broadcasted_iota(jnp.int32, (ASSIGN_ROWS, tm), 0).astype(f32)
    a12 = jnp.where(rowa == e1, 1.0, 0.0) + jnp.where(rowa == e2 + N_EXPERTS, 1.0, 0.0)
    rank = _dot(a12.astype(BF16), ut_ref[...])
    cnt = jnp.broadcast_to(jnp.sum(a12, axis=1, keepdims=True), (ASSIGN_ROWS, LANES))
    row = lax.broadcasted_iota(jnp.int32, (ASSIGN_ROWS, LANES), 0)
    tot = cnt + pltpu.roll(cnt, N_EXPERTS, 0)
    run = jnp.where(row < N_EXPERTS, jnp.floor((tot + (PIECE - 1)) * (1.0 / PIECE)), 0.0)
    off = PIECE * _dot(lt_ref[...], run.astype(BF16))
    base = jnp.where(row < N_EXPERTS, off, pltpu.roll(off + cnt, N_EXPERTS, 0))
    posmat = a12 * (rank + base[:, 0:1])
    pos1 = jnp.sum(posmat[0:N_EXPERTS, :], axis=0, keepdims=True)
    pos2 = jnp.sum(posmat[N_EXPERTS:2 * N_EXPERTS, :], axis=0, keepdims=True)
    ri = lax.broadcasted_iota(jnp.int32, (xs_ref.shape[0], tm), 0)
    perm = jnp.where((ri == pos1.astype(jnp.int32)) | (ri == pos2.astype(jnp.int32)), 1.0, 0.0).astype(BF16)
    xs_ref[...] = _dot(perm, h_hi).astype(xs_ref.dtype)
    rowr = lax.broadcasted_iota(jnp.int32, (SUBLANES, tm), 0)
    rec = jnp.zeros((SUBLANES, tm), f32)
    for col, val in ((REC_W1, w1), (REC_W2, w2), (REC_POS1, pos1), (REC_POS2, pos2)):
        rec = jnp.where(rowr == col, val, rec)
    rec = jnp.concatenate([rec, jnp.zeros((ROUTE_LANES - SUBLANES, tm), f32)], axis=0)
    route_ref[...] = rec.T
    len_ref[...] = run


def _local_rows(tm):
    return -(-(TOP_K * tm + N_EXPERTS * (PIECE - 1)) // PIECE) * PIECE


def _mix_out(y, siga, sb, x, r_emb, c_emb, mod, wv, wg, wo, ln1, wr, br, tm):
    b, l, d = x.shape
    n_tiles = l // tm
    xs_rows = _local_rows(tm)
    tok = lambda w: pl.BlockSpec((None, tm, w), lambda i, j: (j, i, 0))
    whole = lambda a: pl.BlockSpec(a.shape, lambda i, j: (0,) * a.ndim)
    ut = (jnp.arange(tm)[:, None] < jnp.arange(tm)[None, :]).astype(BF16)
    lt = (jnp.arange(ASSIGN_ROWS)[None, :] < jnp.arange(ASSIGN_ROWS)[:, None]).astype(BF16)
    return pl.pallas_call(
        _mix_out_kernel,
        grid=(n_tiles, b),
        in_specs=[pl.BlockSpec((tm // CHUNK, CHUNK * S5_WIDTH), lambda i, j: (j * n_tiles + i, 0)),
                  tok(d), tok(d), tok(d)] + _pos_specs(tm, d) + [
                  pl.BlockSpec((None, 4, d), lambda i, j: (j, 0, 0)),
                  whole(wv), whole(wg), whole(wo), whole(ln1), whole(wr), whole(br),
                  whole(ut), whole(lt)],
        out_specs=[tok(d),
                   pl.BlockSpec((xs_rows, d), lambda i, j: (j * n_tiles + i, 0)),
                   tok(ROUTE_LANES),
                   pl.BlockSpec((None, ASSIGN_ROWS, LANES), lambda i, j: (j * n_tiles + i, 0, 0))],
        out_shape=[jax.ShapeDtypeStruct((b, l, d), F32),
                   jax.ShapeDtypeStruct((b * n_tiles * xs_rows, d), BF16),
                   jax.ShapeDtypeStruct((b, l, ROUTE_LANES), F32),
                   jax.ShapeDtypeStruct((b * n_tiles, ASSIGN_ROWS, LANES), F32)],
        scratch_shapes=[pltpu.VMEM((S5_WIDTH // LANES, tm, LANES), F32)],
        compiler_params=pltpu.CompilerParams(
            dimension_semantics=("arbitrary", "arbitrary"), vmem_limit_bytes=VMEM_LIMIT),
        name="mix_out",
    )(y, siga, sb, x, r_emb, c_emb, mod, wv, wg, wo, ln1, wr, br, ut, lt)


def _piece_copy(src_hbm, src_row, dst, piece, sem):
    return pltpu.make_async_copy(src_hbm.at[pl.ds(pl.multiple_of(src_row, PIECE), PIECE), :],
                                 dst.at[pl.ds(pl.multiple_of(piece * PIECE, PIECE), PIECE), :], sem)


def _issue_pieces(src_hbm, table_ref, first, n_pieces, dst, sem):
    for p in range(n_pieces):
        _piece_copy(src_hbm, table_ref[first + p], dst, p, sem).start()


def _wait_pieces(src_hbm, dst, sem):
    pltpu.make_async_copy(src_hbm.at[pl.ds(0, dst.shape[0]), :], dst, sem).wait()


def _experts_kernel(be_ref, piece_ref, nused_ref, xs_hbm, wg_ref, wu_ref, wd_ref, ys_ref,
                    xs_buf0, xs_buf1, wg_bf, wu_bf, wd_bf, sem):
    i = pl.program_id(0)
    n_used = nused_ref[0]
    per_block = ROW_BLOCK // PIECE

    @pl.when(i == 0)
    def _():
        _issue_pieces(xs_hbm, piece_ref, 0, per_block, xs_buf0, sem.at[0])

    def block(cur, cur_sem, oth, oth_sem):
        @pl.when((i == 0) | (be_ref[i] != be_ref[jnp.maximum(i - 1, 0)]))
        def _():
            wg_bf[...] = wg_ref[...].astype(BF16)
            wu_bf[...] = wu_ref[...].astype(BF16)
            wd_bf[...] = wd_ref[...].astype(BF16)

        _wait_pieces(xs_hbm, cur, cur_sem)
        nxt = jnp.minimum(i + 1, n_used - 1)
        _issue_pieces(xs_hbm, piece_ref, nxt * per_block, per_block, oth, oth_sem)

        xb = cur[...]
        gate = _dot(xb, wg_bf[...])
        up = _dot(xb, wu_bf[...])
        hid = (gate * _sigmoid(gate) * up).astype(BF16)
        ys_ref[...] = _dot(hid, wd_bf[...]).astype(ys_ref.dtype)

        @pl.when(i == n_used - 1)
        def _():
            _wait_pieces(xs_hbm, oth, oth_sem)

    for s, (cur, oth) in enumerate(((xs_buf0, xs_buf1), (xs_buf1, xs_buf0))):
        @pl.when((i < n_used) & (i % 2 == s))
        def _(s=s, cur=cur, oth=oth):
            block(cur, sem.at[s], oth, sem.at[1 - s])

    @pl.when(i >= n_used)
    def _():
        ys_ref[...] = jnp.zeros(ys_ref.shape, ys_ref.dtype)


def _experts(block_e, piece_src, n_used, xs, wg, wu, wd, n_blocks):
    d = xs.shape[1]
    by_expert = lambda i, be, ps, nu: (0, be[i], 0, 0)
    grid_spec = pltpu.PrefetchScalarGridSpec(
        num_scalar_prefetch=3,
        grid=(n_blocks,),
        in_specs=[pl.BlockSpec(memory_space=pl.ANY),
                  pl.BlockSpec((None, None, d, EXPERT_FF), by_expert),
                  pl.BlockSpec((None, None, d, EXPERT_FF), by_expert),
                  pl.BlockSpec((None, None, EXPERT_FF, d), by_expert)],
        out_specs=pl.BlockSpec((ROW_BLOCK, d), lambda i, be, ps, nu: (i, 0)),
        scratch_shapes=[pltpu.VMEM((ROW_BLOCK, d), BF16), pltpu.VMEM((ROW_BLOCK, d), BF16),
                        pltpu.VMEM((d, EXPERT_FF), BF16), pltpu.VMEM((d, EXPERT_FF), BF16),
                        pltpu.VMEM((EXPERT_FF, d), BF16), pltpu.SemaphoreType.DMA((2,))],
    )
    return pl.pallas_call(
        _experts_kernel,
        grid_spec=grid_spec,
        out_shape=jax.ShapeDtypeStruct((n_blocks * ROW_BLOCK, d), BF16),
        compiler_params=pltpu.CompilerParams(
            dimension_semantics=("arbitrary",), vmem_limit_bytes=VMEM_LIMIT),
        name="experts",
    )(block_e, piece_src, n_used, xs, wg, wu, wd)


def _combine_kernel(piece_ref, ys_hbm, x1_ref, route_ref, mod_ref, ln2_ref, o_ref, buf, sem):
    i = pl.program_id(0)
    n = pl.num_programs(0)
    slot = i % 2
    rows = buf.shape[1]
    per_tile = rows // PIECE

    def issue(tile, s):
        def body(p, _):
            _piece_copy(ys_hbm, piece_ref[tile * per_tile + p], buf.at[s], p, sem.at[s]).start()
            return 0
        lax.fori_loop(0, per_tile, body, 0)

    @pl.when(i == 0)
    def _():
        issue(0, 0)

    _wait_pieces(ys_hbm, buf.at[slot], sem.at[slot])

    @pl.when(i + 1 < n)
    def _():
        issue(i + 1, 1 - slot)

    tm = x1_ref.shape[0]
    lane = lax.broadcasted_iota(jnp.int32, (tm, rows), 1)
    rec = lambda col: route_ref[:, col:col + 1]
    sel = jnp.where(lane == rec(REC_POS1).astype(jnp.int32), rec(REC_W1),
                    jnp.where(lane == rec(REC_POS2).astype(jnp.int32), rec(REC_W2), 0.0)).astype(BF16)
    moe = _dot(sel, buf[slot])
    z = ALPHA * x1_ref[...] + mod_ref[0:1, :] * moe
    o_ref[...] = _ln(z) * ln2_ref[0:1, :] + ln2_ref[1:2, :]


def _combine(piece_glob, ys, x1, route, mod, ln2, tm, tiles_per_batch):
    t, d = x1.shape
    grid_spec = pltpu.PrefetchScalarGridSpec(
        num_scalar_prefetch=1,
        grid=(t // tm,),
        in_specs=[pl.BlockSpec(memory_space=pl.ANY),
                  pl.BlockSpec((tm, d), lambda i, pg: (i, 0)),
                  pl.BlockSpec((tm, ROUTE_LANES), lambda i, pg: (i, 0)),
                  pl.BlockSpec((None, SUBLANES, d), lambda i, pg: (i // tiles_per_batch, 0, 0)),
                  pl.BlockSpec((2, d), lambda i, pg: (0, 0))],
        out_specs=pl.BlockSpec((tm, d), lambda i, pg: (i, 0)),
        scratch_shapes=[pltpu.VMEM((2, _local_rows(tm), d), BF16), pltpu.SemaphoreType.DMA((2,))],
    )
    return pl.pallas_call(
        _combine_kernel,
        grid_spec=grid_spec,
        out_shape=jax.ShapeDtypeStruct((t, d), F32),
        compiler_params=pltpu.CompilerParams(
            dimension_semantics=("arbitrary",), vmem_limit_bytes=VMEM_LIMIT),
        name="combine",
    )(piece_glob, ys, x1, route, mod, ln2)


def _sincos_2d(rows, cols, dim):
    q = dim // 4
    omega = 1.0 / (POS_BASE ** (jnp.arange(q, dtype=F32) / q))
    r = jnp.arange(rows, dtype=F32)[:, None] * omega
    cl = jnp.arange(cols, dtype=F32)[:, None] * omega
    r_emb = jnp.concatenate([jnp.sin(r), jnp.cos(r)], -1)
    c_emb = jnp.concatenate([jnp.sin(cl), jnp.cos(cl)], -1)
    return r_emb, c_emb


def _routing_tables(run_pieces, xs_rows, n_blocks):
    i32 = jnp.int32
    n_tiles = run_pieces.shape[0]
    ppb = ROW_BLOCK // PIECE
    loc_start = jnp.cumsum(run_pieces, axis=1) - run_pieces
    seg_tot = jnp.sum(run_pieces, axis=0)
    seg_pad = (seg_tot + ppb - 1) // ppb * ppb
    seg_end = jnp.cumsum(seg_pad)
    seg_start = seg_end - seg_pad
    run_t = run_pieces.T
    glob_start = seg_start[:, None] + jnp.cumsum(run_t, axis=1) - run_t
    n_used = (seg_end[-1] // ppb).astype(i32)
    blk = jnp.minimum(jnp.arange(n_blocks, dtype=i32), n_used - 1)
    block_e = jnp.minimum(jnp.sum((seg_end[None, :] <= (blk * ppb)[:, None]).astype(i32), axis=1),
                          N_EXPERTS - 1).astype(i32)
    lpt = xs_rows // PIECE
    src0 = jnp.arange(n_tiles, dtype=i32)[None, :] * lpt + loc_start.T
    is_e = block_e[:, None] == jnp.arange(N_EXPERTS, dtype=i32)[None, :]
    of_block = lambda tbl: jnp.sum(jnp.where(is_e[:, :, None], tbl[None], 0), axis=1)[:, None, :]
    p = jnp.arange(n_blocks * ppb, dtype=i32).reshape(n_blocks, ppb, 1)
    within = p - of_block(glob_start)
    hit = (within >= 0) & (within < of_block(run_t))
    piece_src = jnp.sum(jnp.where(hit, (of_block(src0) + within) * PIECE, 0), axis=2).astype(i32).reshape(-1)
    s = jnp.arange(lpt, dtype=i32)
    loc_within = s[None, :, None] - loc_start[:, None, :]
    hit = (loc_within >= 0) & (loc_within < run_pieces[:, None, :])
    piece_glob = jnp.sum(jnp.where(hit, (glob_start.T[:, None, :] + loc_within) * PIECE, 0), axis=2)
    return block_e, piece_src, piece_glob.astype(i32).reshape(-1), n_used.reshape(1)


def kernel(x, c, ctx, c_ctx, w_ada, b_ada, w_in, s5_log_dt_f, s5_a_re_f, s5_a_im_f, s5_b_re_f, s5_b_im_f, s5_c_re_f, s5_c_im_f, s5_log_dt_b, s5_a_re_b, s5_a_im_b, s5_b_re_b, s5_b_im_b, s5_c_re_b, s5_c_im_b, s5_d, s5_w_glu_val, s5_w_glu_gate, conv_w, conv_w_out, w_o, ln1_g, ln1_b, router_w_group, router_b_group, router_w_expert, router_b_expert, exp_w_gate, exp_w_up, exp_w_down, ln2_g, ln2_b):
    b, l, d = x.shape
    lc = ctx.shape[1]
    assert d == D_MODEL and b < SUBLANES and w_ada.shape[0] == DEPTH
    assert l % (SUBLANES * CHUNK) == 0 and lc % (SUBLANES * CHUNK) == 0 and l % GRID_W == 0
    t = b * l
    tm = min(TOKEN_TILE, l)
    tmc = min(TOKEN_TILE, lc)

    cc = jnp.concatenate([c, c_ctx[None, :], jnp.zeros((SUBLANES - b - 1, d), F32)], 0)
    mods = _mods(cc, w_ada[0], b_ada[0])
    sh1, sc1, g1, sh2, sc2, g2 = jnp.split(mods, 6, axis=-1)
    mod_a = jnp.stack([sh1[:b], 1.0 + sc1[:b]], 1)
    mod_ctx = jnp.broadcast_to(jnp.stack([sh1[b], 1.0 + sc1[b]], 0)[None], (b, 2, d))
    mod_c = jnp.stack([g1[:b], sh2[:b], 1.0 + sc2[:b], jnp.zeros((b, d), F32)], 1)
    mod_f = jnp.concatenate([g2[:b, None, :], jnp.zeros((b, SUBLANES - 1, d), F32)], 1)

    w_in_bf = w_in[0].astype(BF16)
    both = lambda fwd, bwd: jnp.concatenate([fwd, bwd], 0)
    s5_tab = _s5_tables(both(s5_log_dt_f, s5_log_dt_b), both(s5_a_re_f, s5_a_re_b), both(s5_a_im_f, s5_a_im_b),
                        both(s5_b_re_f, s5_b_re_b), both(s5_b_im_f, s5_b_im_b),
                        both(s5_c_re_f, s5_c_re_b), both(s5_c_im_f, s5_c_im_b))
    mi, ws, wo_s5, tab = _s5_operators(s5_tab, s5_d[0])

    (uc_ctx,) = _in_proj(ctx, jnp.zeros((lc // GRID_W, d // 2), F32), jnp.zeros((GRID_W, d // 2), F32),
                         mod_ctx, w_in_bf, None, None, tmc, False)
    zero_state = jnp.zeros((N_PAIRS, 4, SUBLANES, LANES), F32)
    s0 = _s5_scan(uc_ctx, None, ws, None, tab, zero_state, b, emit_y=False)

    r_emb, c_emb = _sincos_2d(l // GRID_W, GRID_W, d)
    uc, siga, sb = _in_proj(x, r_emb, c_emb, mod_a, w_in_bf, conv_w[0], conv_w_out[0].astype(BF16), tm, True)
    y, _ = _s5_scan(uc, mi, ws, wo_s5, tab, s0, b)

    wr = jnp.concatenate([router_w_group[0], router_w_expert[0],
                          jnp.zeros((d, LOGIT_ROWS - N_EXPERT_GROUPS - N_EXPERTS), F32)], 1).T.astype(BF16)
    br = jnp.concatenate([router_b_group[0], router_b_expert[0],
                          jnp.zeros((LOGIT_ROWS - N_EXPERT_GROUPS - N_EXPERTS,), F32)])[:, None]
    ln1 = jnp.stack([ln1_g[0], ln1_b[0]], 0)
    x1, xs, route, run_len = _mix_out(y, siga, sb, x, r_emb, c_emb, mod_c,
                                      s5_w_glu_val[0].astype(BF16), s5_w_glu_gate[0].astype(BF16),
                                      w_o[0].astype(BF16), ln1, wr, br, tm)

    x1 = x1.reshape(t, d)
    route = route.reshape(t, ROUTE_LANES)
    n_tiles = t // tm
    xs_rows = _local_rows(tm)
    run_pieces = run_len[:, :N_EXPERTS, 0].astype(jnp.int32)
    max_rows = t * TOP_K + n_tiles * N_EXPERTS * (PIECE - 1) + N_EXPERTS * (ROW_BLOCK - 1)
    n_blocks = -(-max_rows // ROW_BLOCK)
    block_e, piece_src, piece_glob, n_used = _routing_tables(run_pieces, xs_rows, n_blocks)
    ys = _experts(block_e, piece_src, n_used, xs, exp_w_gate, exp_w_up, exp_w_down, n_blocks)
    ln2 = jnp.stack([ln2_g[0], ln2_b[0]], 0)
    out = _combine(piece_glob, ys, x1, route, mod_f, ln2, tm, l // tm)
    return out.reshape(b, l, d)
```

```python
import functools
import math

import jax
import jax.numpy as jnp
from jax import lax
from jax.experimental import pallas as pl
from jax.experimental.pallas import tpu as pltpu

F32 = jnp.float32
BF16 = jnp.bfloat16
HI = lax.Precision.HIGHEST

D_MODEL = 1024
GRID_W = 64
S5_WIDTH = 512
S5_GROUP_CH = 16
S5_GROUPS = S5_WIDTH // S5_GROUP_CH
S5_STATE = 64
CONV_WIDTH = 512
N_EXPERT_GROUPS = 4
EXPERTS_PER_GROUP = 8
N_EXPERTS = N_EXPERT_GROUPS * EXPERTS_PER_GROUP
EXPERT_FF = 512
TOP_K = 2
DEPTH = 1
ALPHA = (2.0 * DEPTH) ** 0.25
LN_EPS = 1e-6
POS_BASE = 10000.0
GELU_K = math.sqrt(2.0 / math.pi)
GELU_A = 0.044715

LANES = 128
SUBLANES = 8
CHUNK = 16
GROUP_W = CHUNK * S5_GROUP_CH
PAIR_W = 2 * GROUP_W
N_PAIRS = S5_GROUPS // 2
TOK_PER_VREG = LANES // S5_GROUP_CH
TAB_ROWS = 24
TAB_POW = 2 * SUBLANES
MODS_COLS = 1536
ROUTE_LANES = 128
LOGIT_ROWS = 48
ASSIGN_ROWS = TOP_K * N_EXPERTS
REC_W1, REC_W2, REC_POS1, REC_POS2 = range(4)
TOKEN_TILE = 512
ROW_BLOCK = 512
PIECE = 16
VMEM_LIMIT = 56 * 1024 * 1024


def _ln(x):
    mu = jnp.mean(x, axis=-1, keepdims=True)
    xc = x - mu
    var = jnp.mean(xc * xc, axis=-1, keepdims=True)
    return xc * lax.rsqrt(var + LN_EPS)


def _sigmoid(x):
    return 0.5 * (jnp.tanh(0.5 * x) + 1.0)


def _dot(a, b):
    return jnp.dot(a, b, preferred_element_type=F32)


def _mods_kernel(c_ref, w_ref, b_ref, o_ref):
    c = c_ref[...]
    a = c * _sigmoid(c)
    o_ref[...] = jnp.dot(a, w_ref[...], precision=HI, preferred_element_type=F32) + b_ref[...]


def _mods(cc, w_ada, b_ada):
    n = w_ada.shape[1]
    nb = MODS_COLS
    return pl.pallas_call(
        _mods_kernel,
        grid=(n // nb,),
        in_specs=[pl.BlockSpec((SUBLANES, D_MODEL), lambda i: (0, 0)),
                  pl.BlockSpec((D_MODEL, nb), lambda i: (0, i)),
                  pl.BlockSpec((1, nb), lambda i: (0, i))],
        out_specs=pl.BlockSpec((SUBLANES, nb), lambda i: (0, i)),
        out_shape=jax.ShapeDtypeStruct((SUBLANES, n), F32),
        compiler_params=pltpu.CompilerParams(vmem_limit_bytes=VMEM_LIMIT),
        name="mods",
    )(cc, w_ada, b_ada.reshape(1, n))


def _slot_masks(rows):
    slot = lax.broadcasted_iota(jnp.int32, (rows, LANES), 1) // S5_GROUP_CH
    return [slot == s for s in range(TOK_PER_VREG)]


def _to_chunk_tile(u_scr, uc_ref):
    nch = uc_ref.shape[0]
    masks = _slot_masks(nch)
    for qh in range(CHUNK // TOK_PER_VREG):
        for v in range(S5_WIDTH // LANES):
            src = [u_scr[v, pl.ds(qh * TOK_PER_VREG + s, nch, stride=CHUNK), :] for s in range(TOK_PER_VREG)]
            for i in range(TOK_PER_VREG):
                acc = None
                for s in range(TOK_PER_VREG):
                    shift = ((s - i) * S5_GROUP_CH) % LANES
                    piece = pltpu.roll(src[s], shift, 1) if shift else src[s]
                    acc = piece if acc is None else jnp.where(masks[s], piece, acc)
                lo = (v * TOK_PER_VREG + i) * GROUP_W + qh * LANES
                uc_ref[:, lo:lo + LANES] = acc.astype(uc_ref.dtype)


def _from_chunk_tile(yc_ref, y_scr):
    nch = yc_ref.shape[0]
    masks = _slot_masks(nch)
    for qh in range(CHUNK // TOK_PER_VREG):
        for v in range(S5_WIDTH // LANES):
            src = []
            for i in range(TOK_PER_VREG):
                lo = (v * TOK_PER_VREG + i) * GROUP_W + qh * LANES
                src.append(yc_ref[:, lo:lo + LANES].astype(F32))
            for s in range(TOK_PER_VREG):
                acc = None
                for i in range(TOK_PER_VREG):
                    shift = ((i - s) * S5_GROUP_CH) % LANES
                    piece = pltpu.roll(src[i], shift, 1) if shift else src[i]
                    acc = piece if acc is None else jnp.where(masks[i], piece, acc)
                y_scr[v, pl.ds(qh * TOK_PER_VREG + s, nch, stride=CHUNK), :] = acc


def _with_positions(x_ref, remb_ref, cemb_ref):
    c = cemb_ref[...]
    slabs = []
    for j in range(x_ref.shape[0] // GRID_W):
        r = jnp.broadcast_to(remb_ref[j:j + 1, :], c.shape)
        slabs.append(x_ref[j * GRID_W:(j + 1) * GRID_W, :] + jnp.concatenate([r, c], axis=-1))
    return jnp.concatenate(slabs, axis=0)


def _in_proj_kernel(x_ref, remb_ref, cemb_ref, mod_ref, w_ref, *rest, full):
    if full:
        cw_ref, cwo_ref, uc_ref, siga_ref, sb_ref, u_scr = rest
    else:
        uc_ref, u_scr = rest
    xp = _with_positions(x_ref, remb_ref, cemb_ref)
    h = (_ln(xp) * mod_ref[1:2, :] + mod_ref[0:1, :]).astype(BF16)
    o1 = S5_WIDTH
    o2, o3, o4 = o1 + CONV_WIDTH, o1 + 2 * CONV_WIDTH, o1 + 3 * CONV_WIDTH
    o5 = o4 + D_MODEL
    u = _dot(h, w_ref[:, 0:o1])
    for v in range(S5_WIDTH // LANES):
        u_scr[v] = u[:, v * LANES:(v + 1) * LANES]
    _to_chunk_tile(u_scr, uc_ref)
    if not full:
        return
    z_b = _dot(h, w_ref[:, o1:o2])
    gate_c = _dot(h, w_ref[:, o3:o4])
    p = gate_c * z_b
    tm = p.shape[0]
    col = lax.broadcasted_iota(jnp.int32, (tm, 1), 0) % GRID_W
    prev = jnp.where(col == 0, 0.0, pltpu.roll(p, 1, 0))
    nxt = jnp.where(col == GRID_W - 1, 0.0, pltpu.roll(p, tm - 1, 0))
    v = cw_ref[0:1, :] * prev + cw_ref[1:2, :] * p + cw_ref[2:3, :] * nxt
    gate_b = _dot(h, w_ref[:, o2:o3])
    out_b = _dot((gate_b * v).astype(BF16), cwo_ref[...])
    merge_b = _dot(h, w_ref[:, o5:])
    sb_ref[...] = (_sigmoid(merge_b) * out_b).astype(sb_ref.dtype)
    merge_a = _dot(h, w_ref[:, o4:o5])
    siga_ref[...] = _sigmoid(merge_a).astype(siga_ref.dtype)


def _pos_specs(tm, d):
    return [pl.BlockSpec((tm // GRID_W, d // 2), lambda i, j: (i, 0)),
            pl.BlockSpec((GRID_W, d // 2), lambda i, j: (0, 0))]


def _in_proj(x, r_emb, c_emb, mod, w_in_bf, conv_w, conv_w_out_bf, tm, full):
    b, l, d = x.shape
    n_tiles = l // tm
    grid = (n_tiles, b)
    tok = lambda w: pl.BlockSpec((None, tm, w), lambda i, j: (j, i, 0))
    chunk_spec = pl.BlockSpec((tm // CHUNK, CHUNK * S5_WIDTH), lambda i, j: (j * n_tiles + i, 0))
    chunk_shape = jax.ShapeDtypeStruct((b * l // CHUNK, CHUNK * S5_WIDTH), BF16)
    in_specs = [tok(d)] + _pos_specs(tm, d) + [pl.BlockSpec((None, 2, d), lambda i, j: (j, 0, 0))]
    args = [x, r_emb, c_emb, mod]
    if full:
        in_specs += [pl.BlockSpec(w_in_bf.shape, lambda i, j: (0, 0)),
                     pl.BlockSpec(conv_w.shape, lambda i, j: (0, 0)),
                     pl.BlockSpec(conv_w_out_bf.shape, lambda i, j: (0, 0))]
        args += [w_in_bf, conv_w, conv_w_out_bf]
        out_specs = [chunk_spec, tok(d), tok(d)]
        out_shape = [chunk_shape,
                     jax.ShapeDtypeStruct((b, l, d), BF16),
                     jax.ShapeDtypeStruct((b, l, d), BF16)]
    else:
        in_specs += [pl.BlockSpec((d, S5_WIDTH), lambda i, j: (0, 0))]
        args += [w_in_bf]
        out_specs = [chunk_spec]
        out_shape = [chunk_shape]
    return pl.pallas_call(
        functools.partial(_in_proj_kernel, full=full),
        grid=grid, in_specs=in_specs, out_specs=out_specs, out_shape=out_shape,
        scratch_shapes=[pltpu.VMEM((S5_WIDTH // LANES, tm, LANES), F32)],
        compiler_params=pltpu.CompilerParams(
            dimension_semantics=("arbitrary", "arbitrary"), vmem_limit_bytes=VMEM_LIMIT),
        name="in_proj" if full else "in_proj_ctx",
    )(*args)


def _s5_tables(log_dt, a_re, a_im, b_re, b_im, c_re, c_im):
    f32 = F32
    dt = jnp.exp(log_dt.astype(f32))[..., None]
    a_re = a_re.astype(f32)
    a_im = a_im.astype(f32)
    mag = jnp.exp(dt * a_re)
    ab_re = mag * jnp.cos(dt * a_im)
    ab_im = mag * jnp.sin(dt * a_im)
    den = a_re * a_re + a_im * a_im
    x_re = ab_re - 1.0
    f_re = (x_re * a_re + ab_im * a_im) / den
    f_im = (ab_im * a_re - x_re * a_im) / den
    b_re = b_re.astype(f32)
    b_im = b_im.astype(f32)
    bb_re = f_re[..., None] * b_re - f_im[..., None] * b_im
    bb_im = f_re[..., None] * b_im + f_im[..., None] * b_re
    k = jnp.arange(CHUNK + 1, dtype=f32)[None, :, None, None]
    pmag = jnp.exp(k * (dt * a_re)[:, None])
    p_re = pmag * jnp.cos(k * (dt * a_im)[:, None])
    p_im = pmag * jnp.sin(k * (dt * a_im)[:, None])
    pb_re = p_re[..., None] * bb_re[:, None] - p_im[..., None] * bb_im[:, None]
    pb_im = p_re[..., None] * bb_im[:, None] + p_im[..., None] * bb_re[:, None]
    c_re = c_re.astype(f32)[:, None]
    c_im = c_im.astype(f32)[:, None]
    cp_re = c_re * p_re[:, :, :, None, :] - c_im * p_im[:, :, :, None, :]
    cp_im = -(c_re * p_im[:, :, :, None, :] + c_im * p_re[:, :, :, None, :])
    return dict(p_re=p_re, p_im=p_im, pb_re=pb_re, pb_im=pb_im, cp_re=cp_re, cp_im=cp_im,
                bb_re=bb_re, bb_im=bb_im)


def _lag_kernels(t):
    g, n, c = S5_GROUPS, S5_STATE, S5_GROUP_CH
    k = CHUNK + 1
    lhs = jnp.concatenate([t['cp_re'], t['cp_im']], -1)
    lhs = lhs.transpose(0, 2, 1, 3, 4).reshape(2 * g, k * c, 2 * n)
    rhs = jnp.concatenate([t['bb_re'], t['bb_im']], -2).reshape(2 * g, 2 * n, c)
    out = jnp.einsum('bmn,bnc->bmc', lhs, rhs, precision=HI)
    out = out.reshape(2, g, k, c, c).transpose(0, 2, 1, 4, 3)
    return out[0], out[1]


def _s5_operators(t, s5_d):
    q = CHUNK
    g, n, c = S5_GROUPS, S5_STATE, S5_GROUP_CH
    kern_f, kern_b = _lag_kernels(t)
    k0 = kern_f[0] + kern_b[0] + s5_d.astype(F32)[:, :, None] * jnp.eye(c, dtype=F32)[None]
    kc = jnp.concatenate([kern_b[1:q][::-1], k0[None], kern_f[1:q]], 0)
    kct = kc.transpose(1, 2, 0, 3)
    m_intra = jnp.stack([kct[:, :, q - 1 - i:2 * q - 1 - i, :] for i in range(q)], 1)
    m_intra = m_intra.reshape(g, q * c, q * c)
    w_st = jnp.stack([t['pb_re'][0, :q][::-1], t['pb_im'][0, :q][::-1],
                      t['pb_re'][1, :q], t['pb_im'][1, :q]], 0)
    w_st = w_st.transpose(2, 1, 4, 0, 3).reshape(g, q * c, 4, n)
    w_out = jnp.stack([t['cp_re'][0, 1:], t['cp_im'][0, 1:],
                       t['cp_re'][1, 1:][::-1], t['cp_im'][1, 1:][::-1]], 0)
    w_out = w_out.transpose(2, 0, 4, 1, 3).reshape(g, 4, n, q * c)
    np_ = N_PAIRS
    w_st = w_st.astype(BF16).reshape(np_, 2, q * c, 4, n)
    ws_pair = jnp.concatenate([jnp.pad(w_st[:, 0], ((0, 0), (0, 0), (0, 0), (0, n))),
                               jnp.pad(w_st[:, 1], ((0, 0), (0, 0), (0, 0), (n, 0)))], 1)
    ws_pair = ws_pair.reshape(np_, PAIR_W, 4 * 2 * n)
    w_out = w_out.astype(BF16).reshape(np_, 2, 4, n, q * c)
    wo_pair = jnp.stack([jnp.pad(w_out[:, 0], ((0, 0), (0, 0), (0, 0), (0, q * c))),
                         jnp.pad(w_out[:, 1], ((0, 0), (0, 0), (0, 0), (q * c, 0)))], 2)
    wo_pair = wo_pair.reshape(np_, 4 * 2 * n, PAIR_W)
    tab = _chunk_power_table(t).reshape(2 * TAB_ROWS, np_, 2 * n).transpose(1, 0, 2)
    return m_intra.astype(BF16), ws_pair, wo_pair, tab


def _chunk_power_table(t):
    def cmul(x, y):
        return x[0] * y[0] - x[1] * y[1], x[0] * y[1] + x[1] * y[0]
    p1 = (t['p_re'][:, CHUNK], t['p_im'][:, CHUNK])
    p2 = cmul(p1, p1)
    p4 = cmul(p2, p2)
    p8 = cmul(p4, p4)
    pr = [(jnp.ones_like(p1[0]), jnp.zeros_like(p1[0]))]
    for _ in range(SUBLANES - 1):
        pr.append(cmul(pr[-1], p1))
    pr_re = jnp.stack([p[0] for p in pr], 0)
    pr_im = jnp.stack([p[1] for p in pr], 0)
    pw = jnp.stack([p1[0], p1[1], p2[0], p2[1], p4[0], p4[1], p8[0], p8[1]], 0)
    return jnp.concatenate([pr_re[:, 0], pr_im[:, 0], pw[:, 0],
                            pr_re[::-1, 1], pr_im[::-1, 1], pw[:, 1]], 0)


def _s5_scan_kernel(uc_ref, *rest, batch, emit_y):
    if emit_y:
        mi_ref, ws_ref, wo_ref, tab_ref, s0_ref, y_ref, fin_ref, s_scr, in_scr = rest
    else:
        ws_ref, tab_ref, s0_ref, fin_ref, s_scr = rest
    rows = uc_ref.shape[0]
    chunks = rows // batch
    n_tiles = chunks // SUBLANES
    u = uc_ref[...]
    s_scr[...] = _dot(u, ws_ref[...])
    row = lax.broadcasted_iota(jnp.int32, (SUBLANES, LANES), 0)

    def tile_scan(r0, backward, c_re, c_im):
        base = TAB_ROWS if backward else 0
        col = 2 * LANES if backward else 0
        rs = pl.ds(r0, SUBLANES)

        def shift(z, k):
            if backward:
                return jnp.where(row < SUBLANES - k, pltpu.roll(z, SUBLANES - k, 0), 0.0)
            return jnp.where(row >= k, pltpu.roll(z, k, 0), 0.0)

        z_re = s_scr[rs, col:col + LANES]
        z_im = s_scr[rs, col + LANES:col + 2 * LANES]
        for k, t in ((1, TAB_POW), (2, TAB_POW + 2), (4, TAB_POW + 4)):
            a_re = tab_ref[base + t:base + t + 1, :]
            a_im = tab_ref[base + t + 1:base + t + 2, :]
            sh_re = shift(z_re, k)
            sh_im = shift(z_im, k)
            z_re, z_im = z_re + (a_re * sh_re - a_im * sh_im), z_im + (a_re * sh_im + a_im * sh_re)
        pr_re = tab_ref[base:base + SUBLANES, :]
        pr_im = tab_ref[base + SUBLANES:base + 2 * SUBLANES, :]
        if emit_y:
            in_scr[rs, col:col + LANES] = pr_re * c_re - pr_im * c_im + shift(z_re, 1)
            in_scr[rs, col + LANES:col + 2 * LANES] = pr_re * c_im + pr_im * c_re + shift(z_im, 1)
        last = 0 if backward else SUBLANES - 1
        l_re = jnp.broadcast_to(z_re[last:last + 1, :], (SUBLANES, LANES))
        l_im = jnp.broadcast_to(z_im[last:last + 1, :], (SUBLANES, LANES))
        p8_re = tab_ref[base + TAB_POW + 6:base + TAB_POW + 7, :]
        p8_im = tab_ref[base + TAB_POW + 7:base + TAB_POW + 8, :]
        return p8_re * c_re - p8_im * c_im + l_re, p8_re * c_im + p8_im * c_re + l_im

    def body(m, carry):
        out = []
        for b in range(batch):
            cf_re, cf_im, cb_re, cb_im = carry[4 * b:4 * b + 4]
            rf = pl.multiple_of(b * chunks + m * SUBLANES, SUBLANES)
            rb = pl.multiple_of(b * chunks + (n_tiles - 1 - m) * SUBLANES, SUBLANES)
            out += list(tile_scan(rf, False, cf_re, cf_im))
            out += list(tile_scan(rb, True, cb_re, cb_im))
        return tuple(out)

    init = tuple(jnp.broadcast_to(s0_ref[t, b:b + 1, :], (SUBLANES, LANES))
                 for b in range(batch) for t in range(4))
    fin = lax.fori_loop(0, n_tiles, body, init, unroll=min(4, n_tiles))
    fin_ref[...] = jnp.zeros(fin_ref.shape, F32)
    for b in range(batch):
        for t in range(4):
            fin_ref[t, b:b + 1, :] = fin[4 * b + t][0:1, :]
    if emit_y:
        y_intra = jnp.concatenate(
            [_dot(u[:, gl * GROUP_W:(gl + 1) * GROUP_W], mi_ref[gl]) for gl in range(2)], axis=-1)
        y = y_intra + _dot(in_scr[...].astype(BF16), wo_ref[...])
        y_ref[...] = y.astype(y_ref.dtype)


def _s5_scan(uc, mi, ws, wo, tab, s0, batch, emit_y=True):
    rows = uc.shape[0]
    pair = lambda *shape: pl.BlockSpec((None,) + shape, lambda p: (p,) + (0,) * len(shape))
    uc_spec = pl.BlockSpec((rows, PAIR_W), lambda p: (0, p))
    state_spec = pair(4, SUBLANES, LANES)
    state_shape = jax.ShapeDtypeStruct((N_PAIRS, 4, SUBLANES, LANES), F32)
    scratch = [pltpu.VMEM((rows, PAIR_W), F32)]
    if emit_y:
        in_specs = [uc_spec, pl.BlockSpec((2, GROUP_W, GROUP_W), lambda p: (p, 0, 0)),
                    pair(PAIR_W, PAIR_W), pair(PAIR_W, PAIR_W), pair(2 * TAB_ROWS, LANES), state_spec]
        args = (uc, mi, ws, wo, tab, s0)
        out_specs = [uc_spec, state_spec]
        out_shape = [jax.ShapeDtypeStruct((rows, N_PAIRS * PAIR_W), BF16), state_shape]
        scratch = scratch * 2
    else:
        in_specs = [uc_spec, pair(PAIR_W, PAIR_W), pair(2 * TAB_ROWS, LANES), state_spec]
        args = (uc, ws, tab, s0)
        out_specs = state_spec
        out_shape = state_shape
    return pl.pallas_call(
        functools.partial(_s5_scan_kernel, batch=batch, emit_y=emit_y),
        grid=(N_PAIRS,), in_specs=in_specs, out_specs=out_specs, out_shape=out_shape,
        scratch_shapes=scratch,
        compiler_params=pltpu.CompilerParams(
            dimension_semantics=("arbitrary",), vmem_limit_bytes=VMEM_LIMIT),
        name="s5_scan" if emit_y else "s5_scan_ctx",
    )(*args)


def _mix_out_kernel(y_ref, siga_ref, sb_ref, x_ref, remb_ref, cemb_ref, mod_ref, wv_ref, wg_ref, wo_ref,
                    ln1_ref, wr_ref, br_ref, ut_ref, lt_ref,
                    x1_ref, xs_ref, route_ref, len_ref, y_scr):
    _from_chunk_tile(y_ref, y_scr)
    y = jnp.concatenate([y_scr[v] for v in range(S5_WIDTH // LANES)], axis=-1)
    half_y = 0.5 * y
    ya = (half_y + half_y * jnp.tanh(y * (GELU_K + (GELU_K * GELU_A) * (y * y)))).astype(BF16)
    out_a = _dot(ya, wv_ref[...]) * _sigmoid(_dot(ya, wg_ref[...]))
    merged = siga_ref[...].astype(F32) * out_a + sb_ref[...].astype(F32)
    mix = _dot(merged.astype(BF16), wo_ref[...])
    xp = _with_positions(x_ref, remb_ref, cemb_ref)
    x1 = _ln(ALPHA * xp + mod_ref[0:1, :] * mix) * ln1_ref[0:1, :] + ln1_ref[1:2, :]
    x1_ref[...] = x1
    h_hi = (_ln(x1) * mod_ref[2:3, :] + mod_ref[1:2, :]).astype(BF16)
    _route_and_sort(h_hi, wr_ref, br_ref, ut_ref, lt_ref, xs_ref, route_ref, len_ref)


def _route_and_sort(h_hi, wr_ref, br_ref, ut_ref, lt_ref, xs_ref, route_ref, len_ref):
    tm = h_hi.shape[0]
    f32 = F32
    nt = (((1,), (1,)), ((), ()))
    lg = lax.dot_general(wr_ref[...], h_hi, nt, preferred_element_type=f32) + br_ref[...]
    rowi = lax.broadcasted_iota(jnp.int32, (LOGIT_ROWS, tm), 0).astype(f32)
    neg = jnp.float32(-jnp.inf)
    big = jnp.float32(LOGIT_ROWS)
    gl = jnp.where(rowi < N_EXPERT_GROUPS, lg, neg)
    gmax = jnp.max(gl, axis=0, keepdims=True)
    g_idx = jnp.min(jnp.where(gl == gmax, rowi, big), axis=0, keepdims=True)
    p_group = 1.0 / jnp.sum(jnp.exp(gl - gmax), axis=0, keepdims=True)
    e_lo = N_EXPERT_GROUPS + g_idx * EXPERTS_PER_GROUP
    el = jnp.where((rowi >= e_lo) & (rowi < e_lo + EXPERTS_PER_GROUP), lg, neg)
    m1 = jnp.max(el, axis=0, keepdims=True)
    i1 = jnp.min(jnp.where(el == m1, rowi, big), axis=0, keepdims=True)
    el2 = jnp.where(rowi == i1, neg, el)
    m2 = jnp.max(el2, axis=0, keepdims=True)
    i2 = jnp.min(jnp.where(el2 == m2, rowi, big), axis=0, keepdims=True)
    r = jnp.exp(m2 - m1)
    w1 = p_group / (1.0 + r)
    w2 = p_group * r / (1.0 + r)
    e1 = i1 - N_EXPERT_GROUPS
    e2 = i2 - N_EXPERT_GROUPS
    rowa = lax.broadcasted_iota(jnp.int32, (ASSIGN_ROWS, tm), 0).astype(f32)
    a12 = jnp.where(rowa == e1, 1.0, 0.0) + jnp.where(rowa == e2 + N_EXPERTS, 1.0, 0.0)
    rank = _dot(a12.astype(BF16), ut_ref[...])
    cnt = jnp.broadcast_to(jnp.sum(a12, axis=1, keepdims=True), (ASSIGN_ROWS, LANES))
    row = lax.broadcasted_iota(jnp.int32, (ASSIGN_ROWS, LANES), 0)
    tot = cnt + pltpu.roll(cnt, N_EXPERTS, 0)
    run = jnp.where(row < N_EXPERTS, jnp.floor((tot + (PIECE - 1)) * (1.0 / PIECE)), 0.0)
    off = PIECE * _dot(lt_ref[...], run.astype(BF16))
    base = jnp.where(row < N_EXPERTS, off, pltpu.roll(off + cnt, N_EXPERTS, 0))
    posmat = a12 * (rank + base[:, 0:1])
    pos1 = jnp.sum(posmat[0:N_EXPERTS, :], axis=0, keepdims=True)
    pos2 = jnp.sum(posmat[N_EXPERTS:2 * N_EXPERTS, :], axis=0, keepdims=True)
    ri = lax.broadcasted_iota(jnp.int32, (xs_ref.shape[0], tm), 0)
    perm = jnp.where((ri == pos1.astype(jnp.int32)) | (ri == pos2.astype(jnp.int32)), 1.0, 0.0).astype(BF16)
    xs_ref[...] = _dot(perm, h_hi).astype(xs_ref.dtype)
    rowr = lax.broadcasted_iota(jnp.int32, (SUBLANES, tm), 0)
    rec = jnp.zeros((SUBLANES, tm), f32)
    for col, val in ((REC_W1, w1), (REC_W2, w2), (REC_POS1, pos1), (REC_POS2, pos2)):
        rec = jnp.where(rowr == col, val, rec)
    rec = jnp.concatenate([rec, jnp.zeros((ROUTE_LANES - SUBLANES, tm), f32)], axis=0)
    route_ref[...] = rec.T
    len_ref[...] = run


def _local_rows(tm):
    return -(-(TOP_K * tm + N_EXPERTS * (PIECE - 1)) // PIECE) * PIECE


def _mix_out(y, siga, sb, x, r_emb, c_emb, mod, wv, wg, wo, ln1, wr, br, tm):
    b, l, d = x.shape
    n_tiles = l // tm
    xs_rows = _local_rows(tm)
    tok = lambda w: pl.BlockSpec((None, tm, w), lambda i, j: (j, i, 0))
    whole = lambda a: pl.BlockSpec(a.shape, lambda i, j: (0,) * a.ndim)
    ut = (jnp.arange(tm)[:, None] < jnp.arange(tm)[None, :]).astype(BF16)
    lt = (jnp.arange(ASSIGN_ROWS)[None, :] < jnp.arange(ASSIGN_ROWS)[:, None]).astype(BF16)
    return pl.pallas_call(
        _mix_out_kernel,
        grid=(n_tiles, b),
        in_specs=[pl.BlockSpec((tm // CHUNK, CHUNK * S5_WIDTH), lambda i, j: (j * n_tiles + i, 0)),
                  tok(d), tok(d), tok(d)] + _pos_specs(tm, d) + [
                  pl.BlockSpec((None, 4, d), lambda i, j: (j, 0, 0)),
                  whole(wv), whole(wg), whole(wo), whole(ln1), whole(wr), whole(br),
                  whole(ut), whole(lt)],
        out_specs=[tok(d),
                   pl.BlockSpec((xs_rows, d), lambda i, j: (j * n_tiles + i, 0)),
                   tok(ROUTE_LANES),
                   pl.BlockSpec((None, ASSIGN_ROWS, LANES), lambda i, j: (j * n_tiles + i, 0, 0))],
        out_shape=[jax.ShapeDtypeStruct((b, l, d), F32),
                   jax.ShapeDtypeStruct((b * n_tiles * xs_rows, d), BF16),
                   jax.ShapeDtypeStruct((b, l, ROUTE_LANES), F32),
                   jax.ShapeDtypeStruct((b * n_tiles, ASSIGN_ROWS, LANES), F32)],
        scratch_shapes=[pltpu.VMEM((S5_WIDTH // LANES, tm, LANES), F32)],
        compiler_params=pltpu.CompilerParams(
            dimension_semantics=("arbitrary", "arbitrary"), vmem_limit_bytes=VMEM_LIMIT),
        name="mix_out",
    )(y, siga, sb, x, r_emb, c_emb, mod, wv, wg, wo, ln1, wr, br, ut, lt)


def _piece_copy(src_hbm, src_row, dst, piece, sem):
    return pltpu.make_async_copy(src_hbm.at[pl.ds(pl.multiple_of(src_row, PIECE), PIECE), :],
                                 dst.at[pl.ds(pl.multiple_of(piece * PIECE, PIECE), PIECE), :], sem)


def _issue_pieces(src_hbm, table_ref, first, n_pieces, dst, sem):
    for p in range(n_pieces):
        _piece_copy(src_hbm, table_ref[first + p], dst, p, sem).start()


def _wait_pieces(src_hbm, dst, sem):
    pltpu.make_async_copy(src_hbm.at[pl.ds(0, dst.shape[0]), :], dst, sem).wait()


def _experts_kernel(be_ref, piece_ref, nused_ref, xs_hbm, wg_ref, wu_ref, wd_ref, ys_ref,
                    xs_buf0, xs_buf1, wg_bf, wu_bf, wd_bf, sem):
    i = pl.program_id(0)
    n_used = nused_ref[0]
    per_block = ROW_BLOCK // PIECE

    @pl.when(i == 0)
    def _():
        _issue_pieces(xs_hbm, piece_ref, 0, per_block, xs_buf0, sem.at[0])

    def block(cur, cur_sem, oth, oth_sem):
        @pl.when((i == 0) | (be_ref[i] != be_ref[jnp.maximum(i - 1, 0)]))
        def _():
            wg_bf[...] = wg_ref[...].astype(BF16)
            wu_bf[...] = wu_ref[...].astype(BF16)
            wd_bf[...] = wd_ref[...].astype(BF16)

        _wait_pieces(xs_hbm, cur, cur_sem)
        nxt = jnp.minimum(i + 1, n_used - 1)
        _issue_pieces(xs_hbm, piece_ref, nxt * per_block, per_block, oth, oth_sem)

        xb = cur[...]
        gate = _dot(xb, wg_bf[...])
        up = _dot(xb, wu_bf[...])
        hid = (gate * _sigmoid(gate) * up).astype(BF16)
        ys_ref[...] = _dot(hid, wd_bf[...]).astype(ys_ref.dtype)

        @pl.when(i == n_used - 1)
        def _():
            _wait_pieces(xs_hbm, oth, oth_sem)

    for s, (cur, oth) in enumerate(((xs_buf0, xs_buf1), (xs_buf1, xs_buf0))):
        @pl.when((i < n_used) & (i % 2 == s))
        def _(s=s, cur=cur, oth=oth):
            block(cur, sem.at[s], oth, sem.at[1 - s])

    @pl.when(i >= n_used)
    def _():
        ys_ref[...] = jnp.zeros(ys_ref.shape, ys_ref.dtype)


def _experts(block_e, piece_src, n_used, xs, wg, wu, wd, n_blocks):
    d = xs.shape[1]
    by_expert = lambda i, be, ps, nu: (0, be[i], 0, 0)
    grid_spec = pltpu.PrefetchScalarGridSpec(
        num_scalar_prefetch=3,
        grid=(n_blocks,),
        in_specs=[pl.BlockSpec(memory_space=pl.ANY),
                  pl.BlockSpec((None, None, d, EXPERT_FF), by_expert),
                  pl.BlockSpec((None, None, d, EXPERT_FF), by_expert),
                  pl.BlockSpec((None, None, EXPERT_FF, d), by_expert)],
        out_specs=pl.BlockSpec((ROW_BLOCK, d), lambda i, be, ps, nu: (i, 0)),
        scratch_shapes=[pltpu.VMEM((ROW_BLOCK, d), BF16), pltpu.VMEM((ROW_BLOCK, d), BF16),
                        pltpu.VMEM((d, EXPERT_FF), BF16), pltpu.VMEM((d, EXPERT_FF), BF16),
                        pltpu.VMEM((EXPERT_FF, d), BF16), pltpu.SemaphoreType.DMA((2,))],
    )
    return pl.pallas_call(
        _experts_kernel,
        grid_spec=grid_spec,
        out_shape=jax.ShapeDtypeStruct((n_blocks * ROW_BLOCK, d), BF16),
        compiler_params=pltpu.CompilerParams(
            dimension_semantics=("arbitrary",), vmem_limit_bytes=VMEM_LIMIT),
        name="experts",
    )(block_e, piece_src, n_used, xs, wg, wu, wd)


def _combine_kernel(piece_ref, ys_hbm, x1_ref, route_ref, mod_ref, ln2_ref, o_ref, buf, sem):
    i = pl.program_id(0)
    n = pl.num_programs(0)
    slot = i % 2
    rows = buf.shape[1]
    per_tile = rows // PIECE

    def issue(tile, s):
        def body(p, _):
            _piece_copy(ys_hbm, piece_ref[tile * per_tile + p], buf.at[s], p, sem.at[s]).start()
            return 0
        lax.fori_loop(0, per_tile, body, 0)

    @pl.when(i == 0)
    def _():
        issue(0, 0)

    _wait_pieces(ys_hbm, buf.at[slot], sem.at[slot])

    @pl.when(i + 1 < n)
    def _():
        issue(i + 1, 1 - slot)

    tm = x1_ref.shape[0]
    lane = lax.broadcasted_iota(jnp.int32, (tm, rows), 1)
    rec = lambda col: route_ref[:, col:col + 1]
    sel = jnp.where(lane == rec(REC_POS1).astype(jnp.int32), rec(REC_W1),
                    jnp.where(lane == rec(REC_POS2).astype(jnp.int32), rec(REC_W2), 0.0)).astype(BF16)
    moe = _dot(sel, buf[slot])
    z = ALPHA * x1_ref[...] + mod_ref[0:1, :] * moe
    o_ref[...] = _ln(z) * ln2_ref[0:1, :] + ln2_ref[1:2, :]


def _combine(piece_glob, ys, x1, route, mod, ln2, tm, tiles_per_batch):
    t, d = x1.shape
    grid_spec = pltpu.PrefetchScalarGridSpec(
        num_scalar_prefetch=1,
        grid=(t // tm,),
        in_specs=[pl.BlockSpec(memory_space=pl.ANY),
                  pl.BlockSpec((tm, d), lambda i, pg: (i, 0)),
                  pl.BlockSpec((tm, ROUTE_LANES), lambda i, pg: (i, 0)),
                  pl.BlockSpec((None, SUBLANES, d), lambda i, pg: (i // tiles_per_batch, 0, 0)),
                  pl.BlockSpec((2, d), lambda i, pg: (0, 0))],
        out_specs=pl.BlockSpec((tm, d), lambda i, pg: (i, 0)),
        scratch_shapes=[pltpu.VMEM((2, _local_rows(tm), d), BF16), pltpu.SemaphoreType.DMA((2,))],
    )
    return pl.pallas_call(
        _combine_kernel,
        grid_spec=grid_spec,
        out_shape=jax.ShapeDtypeStruct((t, d), F32),
        compiler_params=pltpu.CompilerParams(
            dimension_semantics=("arbitrary",), vmem_limit_bytes=VMEM_LIMIT),
        name="combine",
    )(piece_glob, ys, x1, route, mod, ln2)


def _sincos_2d(rows, cols, dim):
    q = dim // 4
    omega = 1.0 / (POS_BASE ** (jnp.arange(q, dtype=F32) / q))
    r = jnp.arange(rows, dtype=F32)[:, None] * omega
    cl = jnp.arange(cols, dtype=F32)[:, None] * omega
    r_emb = jnp.concatenate([jnp.sin(r), jnp.cos(r)], -1)
    c_emb = jnp.concatenate([jnp.sin(cl), jnp.cos(cl)], -1)
    return r_emb, c_emb


def _routing_tables(run_pieces, xs_rows, n_blocks):
    i32 = jnp.int32
    n_tiles = run_pieces.shape[0]
    ppb = ROW_BLOCK // PIECE
    loc_start = jnp.cumsum(run_pieces, axis=1) - run_pieces
    seg_tot = jnp.sum(run_pieces, axis=0)
    seg_pad = (seg_tot + ppb - 1) // ppb * ppb
    seg_end = jnp.cumsum(seg_pad)
    seg_start = seg_end - seg_pad
    run_t = run_pieces.T
    glob_start = seg_start[:, None] + jnp.cumsum(run_t, axis=1) - run_t
    n_used = (seg_end[-1] // ppb).astype(i32)
    blk = jnp.minimum(jnp.arange(n_blocks, dtype=i32), n_used - 1)
    block_e = jnp.minimum(jnp.sum((seg_end[None, :] <= (blk * ppb)[:, None]).astype(i32), axis=1),
                          N_EXPERTS - 1).astype(i32)
    lpt = xs_rows // PIECE
    src0 = jnp.arange(n_tiles, dtype=i32)[None, :] * lpt + loc_start.T
    is_e = block_e[:, None] == jnp.arange(N_EXPERTS, dtype=i32)[None, :]
    of_block = lambda tbl: jnp.sum(jnp.where(is_e[:, :, None], tbl[None], 0), axis=1)[:, None, :]
    p = jnp.arange(n_blocks * ppb, dtype=i32).reshape(n_blocks, ppb, 1)
    within = p - of_block(glob_start)
    hit = (within >= 0) & (within < of_block(run_t))
    piece_src = jnp.sum(jnp.where(hit, (of_block(src0) + within) * PIECE, 0), axis=2).astype(i32).reshape(-1)
    s = jnp.arange(lpt, dtype=i32)
    loc_within = s[None, :, None] - loc_start[:, None, :]
    hit = (loc_within >= 0) & (loc_within < run_pieces[:, None, :])
    piece_glob = jnp.sum(jnp.where(hit, (glob_start.T[:, None, :] + loc_within) * PIECE, 0), axis=2)
    return block_e, piece_src, piece_glob.astype(i32).reshape(-1), n_used.reshape(1)


def kernel(x, c, ctx, c_ctx, w_ada, b_ada, w_in, s5_log_dt_f, s5_a_re_f, s5_a_im_f, s5_b_re_f, s5_b_im_f, s5_c_re_f, s5_c_im_f, s5_log_dt_b, s5_a_re_b, s5_a_im_b, s5_b_re_b, s5_b_im_b, s5_c_re_b, s5_c_im_b, s5_d, s5_w_glu_val, s5_w_glu_gate, conv_w, conv_w_out, w_o, ln1_g, ln1_b, router_w_group, router_b_group, router_w_expert, router_b_expert, exp_w_gate, exp_w_up, exp_w_down, ln2_g, ln2_b):
    b, l, d = x.shape
    lc = ctx.shape[1]
    assert d == D_MODEL and b < SUBLANES and w_ada.shape[0] == DEPTH
    assert l % (SUBLANES * CHUNK) == 0 and lc % (SUBLANES * CHUNK) == 0 and l % GRID_W == 0
    t = b * l
    tm = min(TOKEN_TILE, l)
    tmc = min(TOKEN_TILE, lc)

    cc = jnp.concatenate([c, c_ctx[None, :], jnp.zeros((SUBLANES - b - 1, d), F32)], 0)
    mods = _mods(cc, w_ada[0], b_ada[0])
    sh1, sc1, g1, sh2, sc2, g2 = jnp.split(mods, 6, axis=-1)
    mod_a = jnp.stack([sh1[:b], 1.0 + sc1[:b]], 1)
    mod_ctx = jnp.broadcast_to(jnp.stack([sh1[b], 1.0 + sc1[b]], 0)[None], (b, 2, d))
    mod_c = jnp.stack([g1[:b], sh2[:b], 1.0 + sc2[:b], jnp.zeros((b, d), F32)], 1)
    mod_f = jnp.concatenate([g2[:b, None, :], jnp.zeros((b, SUBLANES - 1, d), F32)], 1)

    w_in_bf = w_in[0].astype(BF16)
    both = lambda fwd, bwd: jnp.concatenate([fwd, bwd], 0)
    s5_tab = _s5_tables(both(s5_log_dt_f, s5_log_dt_b), both(s5_a_re_f, s5_a_re_b), both(s5_a_im_f, s5_a_im_b),
                        both(s5_b_re_f, s5_b_re_b), both(s5_b_im_f, s5_b_im_b),
                        both(s5_c_re_f, s5_c_re_b), both(s5_c_im_f, s5_c_im_b))
    mi, ws, wo_s5, tab = _s5_operators(s5_tab, s5_d[0])

    (uc_ctx,) = _in_proj(ctx, jnp.zeros((lc // GRID_W, d // 2), F32), jnp.zeros((GRID_W, d // 2), F32),
                         mod_ctx, w_in_bf, None, None, tmc, False)
    zero_state = jnp.zeros((N_PAIRS, 4, SUBLANES, LANES), F32)
    s0 = _s5_scan(uc_ctx, None, ws, None, tab, zero_state, b, emit_y=False)

    r_emb, c_emb = _sincos_2d(l // GRID_W, GRID_W, d)
    uc, siga, sb = _in_proj(x, r_emb, c_emb, mod_a, w_in_bf, conv_w[0], conv_w_out[0].astype(BF16),
                            min(2 * TOKEN_TILE, l), True)
    y, _ = _s5_scan(uc, mi, ws, wo_s5, tab, s0, b)

    wr = jnp.concatenate([router_w_group[0], router_w_expert[0],
                          jnp.zeros((d, LOGIT_ROWS - N_EXPERT_GROUPS - N_EXPERTS), F32)], 1).T.astype(BF16)
    br = jnp.concatenate([router_b_group[0], router_b_expert[0],
                          jnp.zeros((LOGIT_ROWS - N_EXPERT_GROUPS - N_EXPERTS,), F32)])[:, None]
    ln1 = jnp.stack([ln1_g[0], ln1_b[0]], 0)
    x1, xs, route, run_len = _mix_out(y, siga, sb, x, r_emb, c_emb, mod_c,
                                      s5_w_glu_val[0].astype(BF16), s5_w_glu_gate[0].astype(BF16),
                                      w_o[0].astype(BF16), ln1, wr, br, tm)

    x1 = x1.reshape(t, d)
    route = route.reshape(t, ROUTE_LANES)
    n_tiles = t // tm
    xs_rows = _local_rows(tm)
    run_pieces = run_len[:, :N_EXPERTS, 0].astype(jnp.int32)
    max_rows = t * TOP_K + n_tiles * N_EXPERTS * (PIECE - 1) + N_EXPERTS * (ROW_BLOCK - 1)
    n_blocks = -(-max_rows // ROW_BLOCK)
    block_e, piece_src, piece_glob, n_used = _routing_tables(run_pieces, xs_rows, n_blocks)
    ys = _experts(block_e, piece_src, n_used, xs, exp_w_gate, exp_w_up, exp_w_down, n_blocks)
    ln2 = jnp.stack([ln2_g[0], ln2_b[0]], 0)
    out = _combine(piece_glob, ys, x1, route, mod_f, ln2, tm, l // tm)
    return out.reshape(b, l, d)
```

```python
import functools
import math

import jax
import jax.numpy as jnp
from jax import lax
from jax.experimental import pallas as pl
from jax.experimental.pallas import tpu as pltpu

F32 = jnp.float32
BF16 = jnp.bfloat16
HI = lax.Precision.HIGHEST

D_MODEL = 1024
GRID_W = 64
S5_WIDTH = 512
S5_GROUP_CH = 16
S5_GROUPS = S5_WIDTH // S5_GROUP_CH
S5_STATE = 64
CONV_WIDTH = 512
N_EXPERT_GROUPS = 4
EXPERTS_PER_GROUP = 8
N_EXPERTS = N_EXPERT_GROUPS * EXPERTS_PER_GROUP
EXPERT_FF = 512
TOP_K = 2
DEPTH = 1
ALPHA = (2.0 * DEPTH) ** 0.25
LN_EPS = 1e-6
POS_BASE = 10000.0
GELU_K = math.sqrt(2.0 / math.pi)
GELU_A = 0.044715

LANES = 128
SUBLANES = 8
CHUNK = 32
GROUP_W = CHUNK * S5_GROUP_CH
PAIR_W = 2 * GROUP_W
STATE_W = 4 * 2 * S5_STATE
N_PAIRS = S5_GROUPS // 2
TOK_PER_VREG = LANES // S5_GROUP_CH
TAB_ROWS = 24
TAB_POW = 2 * SUBLANES
MODS_COLS = 1536
ROUTE_LANES = 128
LOGIT_ROWS = 48
ASSIGN_ROWS = TOP_K * N_EXPERTS
REC_W1, REC_W2, REC_POS1, REC_POS2 = range(4)
TOKEN_TILE = 512
ROW_BLOCK = 512
PIECE = 16
VMEM_LIMIT = 56 * 1024 * 1024


def _ln(x):
    mu = jnp.mean(x, axis=-1, keepdims=True)
    xc = x - mu
    var = jnp.mean(xc * xc, axis=-1, keepdims=True)
    return xc * lax.rsqrt(var + LN_EPS)


def _sigmoid(x):
    return 0.5 * (jnp.tanh(0.5 * x) + 1.0)


def _dot(a, b):
    return jnp.dot(a, b, preferred_element_type=F32)


def _mods_kernel(c_ref, w_ref, b_ref, o_ref):
    c = c_ref[...]
    a = c * _sigmoid(c)
    o_ref[...] = jnp.dot(a, w_ref[...], precision=HI, preferred_element_type=F32) + b_ref[...]


def _mods(cc, w_ada, b_ada):
    n = w_ada.shape[1]
    nb = MODS_COLS
    return pl.pallas_call(
        _mods_kernel,
        grid=(n // nb,),
        in_specs=[pl.BlockSpec((SUBLANES, D_MODEL), lambda i: (0, 0)),
                  pl.BlockSpec((D_MODEL, nb), lambda i: (0, i)),
                  pl.BlockSpec((1, nb), lambda i: (0, i))],
        out_specs=pl.BlockSpec((SUBLANES, nb), lambda i: (0, i)),
        out_shape=jax.ShapeDtypeStruct((SUBLANES, n), F32),
        compiler_params=pltpu.CompilerParams(vmem_limit_bytes=VMEM_LIMIT),
        name="mods",
    )(cc, w_ada, b_ada.reshape(1, n))


def _slot_masks(rows):
    slot = lax.broadcasted_iota(jnp.int32, (rows, LANES), 1) // S5_GROUP_CH
    return [slot == s for s in range(TOK_PER_VREG)]


def _to_chunk_tile(u_scr, uc_ref):
    nch = uc_ref.shape[0]
    masks = _slot_masks(nch)
    for qh in range(CHUNK // TOK_PER_VREG):
        for v in range(S5_WIDTH // LANES):
            src = [u_scr[v, pl.ds(qh * TOK_PER_VREG + s, nch, stride=CHUNK), :] for s in range(TOK_PER_VREG)]
            for i in range(TOK_PER_VREG):
                acc = None
                for s in range(TOK_PER_VREG):
                    shift = ((s - i) * S5_GROUP_CH) % LANES
                    piece = pltpu.roll(src[s], shift, 1) if shift else src[s]
                    acc = piece if acc is None else jnp.where(masks[s], piece, acc)
                lo = (v * TOK_PER_VREG + i) * GROUP_W + qh * LANES
                uc_ref[:, lo:lo + LANES] = acc.astype(uc_ref.dtype)


def _from_chunk_tile(yc_ref, y_scr):
    nch = yc_ref.shape[0]
    masks = _slot_masks(nch)
    for qh in range(CHUNK // TOK_PER_VREG):
        for v in range(S5_WIDTH // LANES):
            src = []
            for i in range(TOK_PER_VREG):
                lo = (v * TOK_PER_VREG + i) * GROUP_W + qh * LANES
                src.append(yc_ref[:, lo:lo + LANES].astype(F32))
            for s in range(TOK_PER_VREG):
                acc = None
                for i in range(TOK_PER_VREG):
                    shift = ((i - s) * S5_GROUP_CH) % LANES
                    piece = pltpu.roll(src[i], shift, 1) if shift else src[i]
                    acc = piece if acc is None else jnp.where(masks[i], piece, acc)
                y_scr[v, pl.ds(qh * TOK_PER_VREG + s, nch, stride=CHUNK), :] = acc


def _with_positions(x_ref, remb_ref, cemb_ref):
    c = cemb_ref[...]
    slabs = []
    for j in range(x_ref.shape[0] // GRID_W):
        r = jnp.broadcast_to(remb_ref[j:j + 1, :], c.shape)
        slabs.append(x_ref[j * GRID_W:(j + 1) * GRID_W, :] + jnp.concatenate([r, c], axis=-1))
    return jnp.concatenate(slabs, axis=0)


def _in_proj_kernel(x_ref, remb_ref, cemb_ref, mod_ref, w_ref, *rest, full):
    if full:
        cw_ref, cwo_ref, uc_ref, siga_ref, sb_ref, u_scr = rest
    else:
        uc_ref, u_scr = rest
    xp = _with_positions(x_ref, remb_ref, cemb_ref)
    h = (_ln(xp) * mod_ref[1:2, :] + mod_ref[0:1, :]).astype(BF16)
    o1 = S5_WIDTH
    o2, o3, o4 = o1 + CONV_WIDTH, o1 + 2 * CONV_WIDTH, o1 + 3 * CONV_WIDTH
    o5 = o4 + D_MODEL
    u = _dot(h, w_ref[:, 0:o1])
    for v in range(S5_WIDTH // LANES):
        u_scr[v] = u[:, v * LANES:(v + 1) * LANES]
    _to_chunk_tile(u_scr, uc_ref)
    if not full:
        return
    z_b = _dot(h, w_ref[:, o1:o2])
    gate_c = _dot(h, w_ref[:, o3:o4])
    p = gate_c * z_b
    tm = p.shape[0]
    col = lax.broadcasted_iota(jnp.int32, (tm, 1), 0) % GRID_W
    prev = jnp.where(col == 0, 0.0, pltpu.roll(p, 1, 0))
    nxt = jnp.where(col == GRID_W - 1, 0.0, pltpu.roll(p, tm - 1, 0))
    v = cw_ref[0:1, :] * prev + cw_ref[1:2, :] * p + cw_ref[2:3, :] * nxt
    gate_b = _dot(h, w_ref[:, o2:o3])
    out_b = _dot((gate_b * v).astype(BF16), cwo_ref[...])
    merge_b = _dot(h, w_ref[:, o5:])
    sb_ref[...] = (_sigmoid(merge_b) * out_b).astype(sb_ref.dtype)
    merge_a = _dot(h, w_ref[:, o4:o5])
    siga_ref[...] = _sigmoid(merge_a).astype(siga_ref.dtype)


def _pos_specs(tm, d):
    return [pl.BlockSpec((tm // GRID_W, d // 2), lambda i, j: (i, 0)),
            pl.BlockSpec((GRID_W, d // 2), lambda i, j: (0, 0))]


def _in_proj(x, r_emb, c_emb, mod, w_in_bf, conv_w, conv_w_out_bf, tm, full):
    b, l, d = x.shape
    n_tiles = l // tm
    grid = (n_tiles, b)
    tok = lambda w: pl.BlockSpec((None, tm, w), lambda i, j: (j, i, 0))
    chunk_spec = pl.BlockSpec((tm // CHUNK, CHUNK * S5_WIDTH), lambda i, j: (j * n_tiles + i, 0))
    chunk_shape = jax.ShapeDtypeStruct((b * l // CHUNK, CHUNK * S5_WIDTH), BF16)
    in_specs = [tok(d)] + _pos_specs(tm, d) + [pl.BlockSpec((None, 2, d), lambda i, j: (j, 0, 0))]
    args = [x, r_emb, c_emb, mod]
    if full:
        in_specs += [pl.BlockSpec(w_in_bf.shape, lambda i, j: (0, 0)),
                     pl.BlockSpec(conv_w.shape, lambda i, j: (0, 0)),
                     pl.BlockSpec(conv_w_out_bf.shape, lambda i, j: (0, 0))]
        args += [w_in_bf, conv_w, conv_w_out_bf]
        out_specs = [chunk_spec, tok(d), tok(d)]
        out_shape = [chunk_shape,
                     jax.ShapeDtypeStruct((b, l, d), BF16),
                     jax.ShapeDtypeStruct((b, l, d), BF16)]
    else:
        in_specs += [pl.BlockSpec((d, S5_WIDTH), lambda i, j: (0, 0))]
        args += [w_in_bf]
        out_specs = [chunk_spec]
        out_shape = [chunk_shape]
    return pl.pallas_call(
        functools.partial(_in_proj_kernel, full=full),
        grid=grid, in_specs=in_specs, out_specs=out_specs, out_shape=out_shape,
        scratch_shapes=[pltpu.VMEM((S5_WIDTH // LANES, tm, LANES), F32)],
        compiler_params=pltpu.CompilerParams(
            dimension_semantics=("arbitrary", "arbitrary"), vmem_limit_bytes=VMEM_LIMIT),
        name="in_proj" if full else "in_proj_ctx",
    )(*args)


def _s5_tables(log_dt, a_re, a_im, b_re, b_im, c_re, c_im):
    f32 = F32
    dt = jnp.exp(log_dt.astype(f32))[..., None]
    a_re = a_re.astype(f32)
    a_im = a_im.astype(f32)
    mag = jnp.exp(dt * a_re)
    ab_re = mag * jnp.cos(dt * a_im)
    ab_im = mag * jnp.sin(dt * a_im)
    den = a_re * a_re + a_im * a_im
    x_re = ab_re - 1.0
    f_re = (x_re * a_re + ab_im * a_im) / den
    f_im = (ab_im * a_re - x_re * a_im) / den
    b_re = b_re.astype(f32)
    b_im = b_im.astype(f32)
    bb_re = f_re[..., None] * b_re - f_im[..., None] * b_im
    bb_im = f_re[..., None] * b_im + f_im[..., None] * b_re
    k = jnp.arange(CHUNK + 1, dtype=f32)[None, :, None, None]
    pmag = jnp.exp(k * (dt * a_re)[:, None])
    p_re = pmag * jnp.cos(k * (dt * a_im)[:, None])
    p_im = pmag * jnp.sin(k * (dt * a_im)[:, None])
    pb_re = p_re[..., None] * bb_re[:, None] - p_im[..., None] * bb_im[:, None]
    pb_im = p_re[..., None] * bb_im[:, None] + p_im[..., None] * bb_re[:, None]
    c_re = c_re.astype(f32)[:, None]
    c_im = c_im.astype(f32)[:, None]
    cp_re = c_re * p_re[:, :, :, None, :] - c_im * p_im[:, :, :, None, :]
    cp_im = -(c_re * p_im[:, :, :, None, :] + c_im * p_re[:, :, :, None, :])
    return dict(p_re=p_re, p_im=p_im, pb_re=pb_re, pb_im=pb_im, cp_re=cp_re, cp_im=cp_im,
                bb_re=bb_re, bb_im=bb_im)


def _lag_kernels(t):
    g, n, c = S5_GROUPS, S5_STATE, S5_GROUP_CH
    k = CHUNK + 1
    lhs = jnp.concatenate([t['cp_re'], t['cp_im']], -1)
    lhs = lhs.transpose(0, 2, 1, 3, 4).reshape(2 * g, k * c, 2 * n)
    rhs = jnp.concatenate([t['bb_re'], t['bb_im']], -2).reshape(2 * g, 2 * n, c)
    out = jnp.einsum('bmn,bnc->bmc', lhs, rhs, precision=HI)
    out = out.reshape(2, g, k, c, c).transpose(0, 2, 1, 4, 3)
    return out[0], out[1]


def _s5_operators(t, s5_d):
    q = CHUNK
    g, n, c = S5_GROUPS, S5_STATE, S5_GROUP_CH
    kern_f, kern_b = _lag_kernels(t)
    k0 = kern_f[0] + kern_b[0] + s5_d.astype(F32)[:, :, None] * jnp.eye(c, dtype=F32)[None]
    kc = jnp.concatenate([kern_b[1:q][::-1], k0[None], kern_f[1:q]], 0)
    kct = kc.transpose(1, 2, 0, 3)
    m_intra = jnp.stack([kct[:, :, q - 1 - i:2 * q - 1 - i, :] for i in range(q)], 1)
    m_intra = m_intra.reshape(g, q * c, q * c)
    w_st = jnp.stack([t['pb_re'][0, :q][::-1], t['pb_im'][0, :q][::-1],
                      t['pb_re'][1, :q], t['pb_im'][1, :q]], 0)
    w_st = w_st.transpose(2, 1, 4, 0, 3).reshape(g, q * c, 4, n)
    w_out = jnp.stack([t['cp_re'][0, 1:], t['cp_im'][0, 1:],
                       t['cp_re'][1, 1:][::-1], t['cp_im'][1, 1:][::-1]], 0)
    w_out = w_out.transpose(2, 0, 4, 1, 3).reshape(g, 4, n, q * c)
    np_ = N_PAIRS
    w_st = w_st.astype(BF16).reshape(np_, 2, q * c, 4, n)
    ws_pair = jnp.concatenate([jnp.pad(w_st[:, 0], ((0, 0), (0, 0), (0, 0), (0, n))),
                               jnp.pad(w_st[:, 1], ((0, 0), (0, 0), (0, 0), (n, 0)))], 1)
    ws_pair = ws_pair.reshape(np_, PAIR_W, 4 * 2 * n)
    w_out = w_out.astype(BF16).reshape(np_, 2, 4, n, q * c)
    wo_pair = jnp.stack([jnp.pad(w_out[:, 0], ((0, 0), (0, 0), (0, 0), (0, q * c))),
                         jnp.pad(w_out[:, 1], ((0, 0), (0, 0), (0, 0), (q * c, 0)))], 2)
    wo_pair = wo_pair.reshape(np_, 4 * 2 * n, PAIR_W)
    tab = _chunk_power_table(t).reshape(2 * TAB_ROWS, np_, 2 * n).transpose(1, 0, 2)
    return m_intra.astype(BF16), ws_pair, wo_pair, tab


def _chunk_power_table(t):
    def cmul(x, y):
        return x[0] * y[0] - x[1] * y[1], x[0] * y[1] + x[1] * y[0]
    p1 = (t['p_re'][:, CHUNK], t['p_im'][:, CHUNK])
    p2 = cmul(p1, p1)
    p4 = cmul(p2, p2)
    p8 = cmul(p4, p4)
    pr = [(jnp.ones_like(p1[0]), jnp.zeros_like(p1[0]))]
    for _ in range(SUBLANES - 1):
        pr.append(cmul(pr[-1], p1))
    pr_re = jnp.stack([p[0] for p in pr], 0)
    pr_im = jnp.stack([p[1] for p in pr], 0)
    pw = jnp.stack([p1[0], p1[1], p2[0], p2[1], p4[0], p4[1], p8[0], p8[1]], 0)
    return jnp.concatenate([pr_re[:, 0], pr_im[:, 0], pw[:, 0],
                            pr_re[::-1, 1], pr_im[::-1, 1], pw[:, 1]], 0)


def _s5_scan_kernel(uc_ref, *rest, batch, emit_y):
    if emit_y:
        mi_ref, ws_ref, wo_ref, tab_ref, s0_ref, y_ref, fin_ref, s_scr, in_scr = rest
    else:
        ws_ref, tab_ref, s0_ref, fin_ref, s_scr = rest
    rows = uc_ref.shape[0]
    chunks = rows // batch
    n_tiles = chunks // SUBLANES
    u = uc_ref[...]
    s_scr[...] = _dot(u, ws_ref[...])
    row = lax.broadcasted_iota(jnp.int32, (SUBLANES, LANES), 0)

    def tile_scan(r0, backward, c_re, c_im):
        base = TAB_ROWS if backward else 0
        col = 2 * LANES if backward else 0
        rs = pl.ds(r0, SUBLANES)

        def shift(z, k):
            if backward:
                return jnp.where(row < SUBLANES - k, pltpu.roll(z, SUBLANES - k, 0), 0.0)
            return jnp.where(row >= k, pltpu.roll(z, k, 0), 0.0)

        z_re = s_scr[rs, col:col + LANES]
        z_im = s_scr[rs, col + LANES:col + 2 * LANES]
        for k, t in ((1, TAB_POW), (2, TAB_POW + 2), (4, TAB_POW + 4)):
            a_re = tab_ref[base + t:base + t + 1, :]
            a_im = tab_ref[base + t + 1:base + t + 2, :]
            sh_re = shift(z_re, k)
            sh_im = shift(z_im, k)
            z_re, z_im = z_re + (a_re * sh_re - a_im * sh_im), z_im + (a_re * sh_im + a_im * sh_re)
        pr_re = tab_ref[base:base + SUBLANES, :]
        pr_im = tab_ref[base + SUBLANES:base + 2 * SUBLANES, :]
        if emit_y:
            in_scr[rs, col:col + LANES] = pr_re * c_re - pr_im * c_im + shift(z_re, 1)
            in_scr[rs, col + LANES:col + 2 * LANES] = pr_re * c_im + pr_im * c_re + shift(z_im, 1)
        last = 0 if backward else SUBLANES - 1
        l_re = jnp.broadcast_to(z_re[last:last + 1, :], (SUBLANES, LANES))
        l_im = jnp.broadcast_to(z_im[last:last + 1, :], (SUBLANES, LANES))
        p8_re = tab_ref[base + TAB_POW + 6:base + TAB_POW + 7, :]
        p8_im = tab_ref[base + TAB_POW + 7:base + TAB_POW + 8, :]
        return p8_re * c_re - p8_im * c_im + l_re, p8_re * c_im + p8_im * c_re + l_im

    def body(m, carry):
        out = []
        for b in range(batch):
            cf_re, cf_im, cb_re, cb_im = carry[4 * b:4 * b + 4]
            rf = pl.multiple_of(b * chunks + m * SUBLANES, SUBLANES)
            rb = pl.multiple_of(b * chunks + (n_tiles - 1 - m) * SUBLANES, SUBLANES)
            out += list(tile_scan(rf, False, cf_re, cf_im))
            out += list(tile_scan(rb, True, cb_re, cb_im))
        return tuple(out)

    init = tuple(jnp.broadcast_to(s0_ref[t, b:b + 1, :], (SUBLANES, LANES))
                 for b in range(batch) for t in range(4))
    fin = lax.fori_loop(0, n_tiles, body, init, unroll=min(4, n_tiles))
    fin_ref[...] = jnp.zeros(fin_ref.shape, F32)
    for b in range(batch):
        for t in range(4):
            fin_ref[t, b:b + 1, :] = fin[4 * b + t][0:1, :]
    if emit_y:
        y_intra = jnp.concatenate(
            [_dot(u[:, gl * GROUP_W:(gl + 1) * GROUP_W], mi_ref[gl]) for gl in range(2)], axis=-1)
        y = y_intra + _dot(in_scr[...].astype(BF16), wo_ref[...])
        y_ref[...] = y.astype(y_ref.dtype)


def _s5_scan(uc, mi, ws, wo, tab, s0, batch, emit_y=True):
    rows = uc.shape[0]
    pair = lambda *shape: pl.BlockSpec((None,) + shape, lambda p: (p,) + (0,) * len(shape))
    uc_spec = pl.BlockSpec((rows, PAIR_W), lambda p: (0, p))
    state_spec = pair(4, SUBLANES, LANES)
    state_shape = jax.ShapeDtypeStruct((N_PAIRS, 4, SUBLANES, LANES), F32)
    scratch = [pltpu.VMEM((rows, STATE_W), F32)]
    if emit_y:
        in_specs = [uc_spec, pl.BlockSpec((2, GROUP_W, GROUP_W), lambda p: (p, 0, 0)),
                    pair(PAIR_W, STATE_W), pair(STATE_W, PAIR_W), pair(2 * TAB_ROWS, LANES), state_spec]
        args = (uc, mi, ws, wo, tab, s0)
        out_specs = [uc_spec, state_spec]
        out_shape = [jax.ShapeDtypeStruct((rows, N_PAIRS * PAIR_W), BF16), state_shape]
        scratch = scratch * 2
    else:
        in_specs = [uc_spec, pair(PAIR_W, STATE_W), pair(2 * TAB_ROWS, LANES), state_spec]
        args = (uc, ws, tab, s0)
        out_specs = state_spec
        out_shape = state_shape
    return pl.pallas_call(
        functools.partial(_s5_scan_kernel, batch=batch, emit_y=emit_y),
        grid=(N_PAIRS,), in_specs=in_specs, out_specs=out_specs, out_shape=out_shape,
        scratch_shapes=scratch,
        compiler_params=pltpu.CompilerParams(
            dimension_semantics=("arbitrary",), vmem_limit_bytes=VMEM_LIMIT),
        name="s5_scan" if emit_y else "s5_scan_ctx",
    )(*args)


def _mix_out_kernel(y_ref, siga_ref, sb_ref, x_ref, remb_ref, cemb_ref, mod_ref, wv_ref, wg_ref, wo_ref,
                    ln1_ref, wr_ref, br_ref, ut_ref, lt_ref,
                    x1_ref, xs_ref, route_ref, len_ref, y_scr):
    _from_chunk_tile(y_ref, y_scr)
    y = jnp.concatenate([y_scr[v] for v in range(S5_WIDTH // LANES)], axis=-1)
    half_y = 0.5 * y
    ya = (half_y + half_y * jnp.tanh(y * (GELU_K + (GELU_K * GELU_A) * (y * y)))).astype(BF16)
    out_a = _dot(ya, wv_ref[...]) * _sigmoid(_dot(ya, wg_ref[...]))
    merged = siga_ref[...].astype(F32) * out_a + sb_ref[...].astype(F32)
    mix = _dot(merged.astype(BF16), wo_ref[...])
    xp = _with_positions(x_ref, remb_ref, cemb_ref)
    x1 = _ln(ALPHA * xp + mod_ref[0:1, :] * mix) * ln1_ref[0:1, :] + ln1_ref[1:2, :]
    x1_ref[...] = x1
    h_hi = (_ln(x1) * mod_ref[2:3, :] + mod_ref[1:2, :]).astype(BF16)
    _route_and_sort(h_hi, wr_ref, br_ref, ut_ref, lt_ref, xs_ref, route_ref, len_ref)


def _route_and_sort(h_hi, wr_ref, br_ref, ut_ref, lt_ref, xs_ref, route_ref, len_ref):
    tm = h_hi.shape[0]
    f32 = F32
    nt = (((1,), (1,)), ((), ()))
    lg = lax.dot_general(wr_ref[...], h_hi, nt, preferred_element_type=f32) + br_ref[...]
    rowi = lax.broadcasted_iota(jnp.int32, (LOGIT_ROWS, tm), 0).astype(f32)
    neg = jnp.float32(-jnp.inf)
    big = jnp.float32(LOGIT_ROWS)
    gl = jnp.where(rowi < N_EXPERT_GROUPS, lg, neg)
    gmax = jnp.max(gl, axis=0, keepdims=True)
    g_idx = jnp.min(jnp.where(gl == gmax, rowi, big), axis=0, keepdims=True)
    p_group = 1.0 / jnp.sum(jnp.exp(gl - gmax), axis=0, keepdims=True)
    e_lo = N_EXPERT_GROUPS + g_idx * EXPERTS_PER_GROUP
    el = jnp.where((rowi >= e_lo) & (rowi < e_lo + EXPERTS_PER_GROUP), lg, neg)
    m1 = jnp.max(el, axis=0, keepdims=True)
    i1 = jnp.min(jnp.where(el == m1, rowi, big), axis=0, keepdims=True)
    el2 = jnp.where(rowi == i1, neg, el)
    m2 = jnp.max(el2, axis=0, keepdims=True)
    i2 = jnp.min(jnp.where(el2 == m2, rowi, big), axis=0, keepdims=True)
    r = jnp.exp(m2 - m1)
    w1 = p_group / (1.0 + r)
    w2 = p_group * r / (1.0 + r)
    e1 = i1 - N_EXPERT_GROUPS
    e2 = i2 - N_EXPERT_GROUPS
    rowa = lax.broadcasted_iota(jnp.int32, (ASSIGN_ROWS, tm), 0).astype(f32)
    a12 = jnp.where(rowa == e1, 1.0, 0.0) + jnp.where(rowa == e2 + N_EXPERTS, 1.0, 0.0)
    rank = _dot(a12.astype(BF16), ut_ref[...])
    cnt = jnp.broadcast_to(jnp.sum(a12, axis=1, keepdims=True), (ASSIGN_ROWS, LANES))
    row = lax.broadcasted_iota(jnp.int32, (ASSIGN_ROWS, LANES), 0)
    tot = cnt + pltpu.roll(cnt, N_EXPERTS, 0)
    run = jnp.where(row < N_EXPERTS, jnp.floor((tot + (PIECE - 1)) * (1.0 / PIECE)), 0.0)
    off = PIECE * _dot(lt_ref[...], run.astype(BF16))
    base = jnp.where(row < N_EXPERTS, off, pltpu.roll(off + cnt, N_EXPERTS, 0))
    posmat = a12 * (rank + base[:, 0:1])
    pos1 = jnp.sum(posmat[0:N_EXPERTS, :], axis=0, keepdims=True)
    pos2 = jnp.sum(posmat[N_EXPERTS:2 * N_EXPERTS, :], axis=0, keepdims=True)
    ri = lax.broadcasted_iota(jnp.int32, (xs_ref.shape[0], tm), 0)
    perm = jnp.where((ri == pos1.astype(jnp.int32)) | (ri == pos2.astype(jnp.int32)), 1.0, 0.0).astype(BF16)
    xs_ref[...] = _dot(perm, h_hi).astype(xs_ref.dtype)
    rowr = lax.broadcasted_iota(jnp.int32, (SUBLANES, tm), 0)
    rec = jnp.zeros((SUBLANES, tm), f32)
    for col, val in ((REC_W1, w1), (REC_W2, w2), (REC_POS1, pos1), (REC_POS2, pos2)):
        rec = jnp.where(rowr == col, val, rec)
    rec = jnp.concatenate([rec, jnp.zeros((ROUTE_LANES - SUBLANES, tm), f32)], axis=0)
    route_ref[...] = rec.T
    len_ref[...] = run


def _local_rows(tm):
    return -(-(TOP_K * tm + N_EXPERTS * (PIECE - 1)) // PIECE) * PIECE


def _mix_out(y, siga, sb, x, r_emb, c_emb, mod, wv, wg, wo, ln1, wr, br, tm):
    b, l, d = x.shape
    n_tiles = l // tm
    xs_rows = _local_rows(tm)
    tok = lambda w: pl.BlockSpec((None, tm, w), lambda i, j: (j, i, 0))
    whole = lambda a: pl.BlockSpec(a.shape, lambda i, j: (0,) * a.ndim)
    ut = (jnp.arange(tm)[:, None] < jnp.arange(tm)[None, :]).astype(BF16)
    lt = (jnp.arange(ASSIGN_ROWS)[None, :] < jnp.arange(ASSIGN_ROWS)[:, None]).astype(BF16)
    return pl.pallas_call(
        _mix_out_kernel,
        grid=(n_tiles, b),
        in_specs=[pl.BlockSpec((tm // CHUNK, CHUNK * S5_WIDTH), lambda i, j: (j * n_tiles + i, 0)),
                  tok(d), tok(d), tok(d)] + _pos_specs(tm, d) + [
                  pl.BlockSpec((None, 4, d), lambda i, j: (j, 0, 0)),
                  whole(wv), whole(wg), whole(wo), whole(ln1), whole(wr), whole(br),
                  whole(ut), whole(lt)],
        out_specs=[tok(d),
                   pl.BlockSpec((xs_rows, d), lambda i, j: (j * n_tiles + i, 0)),
                   tok(ROUTE_LANES),
                   pl.BlockSpec((None, ASSIGN_ROWS, LANES), lambda i, j: (j * n_tiles + i, 0, 0))],
        out_shape=[jax.ShapeDtypeStruct((b, l, d), F32),
                   jax.ShapeDtypeStruct((b * n_tiles * xs_rows, d), BF16),
                   jax.ShapeDtypeStruct((b, l, ROUTE_LANES), F32),
                   jax.ShapeDtypeStruct((b * n_tiles, ASSIGN_ROWS, LANES), F32)],
        scratch_shapes=[pltpu.VMEM((S5_WIDTH // LANES, tm, LANES), F32)],
        compiler_params=pltpu.CompilerParams(
            dimension_semantics=("arbitrary", "arbitrary"), vmem_limit_bytes=VMEM_LIMIT),
        name="mix_out",
    )(y, siga, sb, x, r_emb, c_emb, mod, wv, wg, wo, ln1, wr, br, ut, lt)


def _piece_copy(src_hbm, src_row, dst, piece, sem):
    return pltpu.make_async_copy(src_hbm.at[pl.ds(pl.multiple_of(src_row, PIECE), PIECE), :],
                                 dst.at[pl.ds(pl.multiple_of(piece * PIECE, PIECE), PIECE), :], sem)


def _issue_pieces(src_hbm, table_ref, first, n_pieces, dst, sem):
    for p in range(n_pieces):
        _piece_copy(src_hbm, table_ref[first + p], dst, p, sem).start()


def _wait_pieces(src_hbm, dst, sem):
    pltpu.make_async_copy(src_hbm.at[pl.ds(0, dst.shape[0]), :], dst, sem).wait()


def _experts_kernel(be_ref, piece_ref, nused_ref, xs_hbm, wg_ref, wu_ref, wd_ref, ys_ref,
                    xs_buf0, xs_buf1, wg_bf, wu_bf, wd_bf, sem):
    i = pl.program_id(0)
    n_used = nused_ref[0]
    per_block = ROW_BLOCK // PIECE

    @pl.when(i == 0)
    def _():
        _issue_pieces(xs_hbm, piece_ref, 0, per_block, xs_buf0, sem.at[0])

    def block(cur, cur_sem, oth, oth_sem):
        @pl.when((i == 0) | (be_ref[i] != be_ref[jnp.maximum(i - 1, 0)]))
        def _():
            wg_bf[...] = wg_ref[...].astype(BF16)
            wu_bf[...] = wu_ref[...].astype(BF16)
            wd_bf[...] = wd_ref[...].astype(BF16)

        _wait_pieces(xs_hbm, cur, cur_sem)
        nxt = jnp.minimum(i + 1, n_used - 1)
        _issue_pieces(xs_hbm, piece_ref, nxt * per_block, per_block, oth, oth_sem)

        xb = cur[...]
        gate = _dot(xb, wg_bf[...])
        up = _dot(xb, wu_bf[...])
        hid = (gate * _sigmoid(gate) * up).astype(BF16)
        ys_ref[...] = _dot(hid, wd_bf[...]).astype(ys_ref.dtype)

        @pl.when(i == n_used - 1)
        def _():
            _wait_pieces(xs_hbm, oth, oth_sem)

    for s, (cur, oth) in enumerate(((xs_buf0, xs_buf1), (xs_buf1, xs_buf0))):
        @pl.when((i < n_used) & (i % 2 == s))
        def _(s=s, cur=cur, oth=oth):
            block(cur, sem.at[s], oth, sem.at[1 - s])

    @pl.when(i >= n_used)
    def _():
        ys_ref[...] = jnp.zeros(ys_ref.shape, ys_ref.dtype)


def _experts(block_e, piece_src, n_used, xs, wg, wu, wd, n_blocks):
    d = xs.shape[1]
    by_expert = lambda i, be, ps, nu: (0, be[i], 0, 0)
    grid_spec = pltpu.PrefetchScalarGridSpec(
        num_scalar_prefetch=3,
        grid=(n_blocks,),
        in_specs=[pl.BlockSpec(memory_space=pl.ANY),
                  pl.BlockSpec((None, None, d, EXPERT_FF), by_expert),
                  pl.BlockSpec((None, None, d, EXPERT_FF), by_expert),
                  pl.BlockSpec((None, None, EXPERT_FF, d), by_expert)],
        out_specs=pl.BlockSpec((ROW_BLOCK, d), lambda i, be, ps, nu: (i, 0)),
        scratch_shapes=[pltpu.VMEM((ROW_BLOCK, d), BF16), pltpu.VMEM((ROW_BLOCK, d), BF16),
                        pltpu.VMEM((d, EXPERT_FF), BF16), pltpu.VMEM((d, EXPERT_FF), BF16),
                        pltpu.VMEM((EXPERT_FF, d), BF16), pltpu.SemaphoreType.DMA((2,))],
    )
    return pl.pallas_call(
        _experts_kernel,
        grid_spec=grid_spec,
        out_shape=jax.ShapeDtypeStruct((n_blocks * ROW_BLOCK, d), BF16),
        compiler_params=pltpu.CompilerParams(
            dimension_semantics=("arbitrary",), vmem_limit_bytes=VMEM_LIMIT),
        name="experts",
    )(block_e, piece_src, n_used, xs, wg, wu, wd)


def _combine_kernel(piece_ref, ys_hbm, x1_ref, route_ref, mod_ref, ln2_ref, o_ref, buf, sem):
    i = pl.program_id(0)
    n = pl.num_programs(0)
    slot = i % 2
    rows = buf.shape[1]
    per_tile = rows // PIECE

    def issue(tile, s):
        def body(p, _):
            _piece_copy(ys_hbm, piece_ref[tile * per_tile + p], buf.at[s], p, sem.at[s]).start()
            return 0
        lax.fori_loop(0, per_tile, body, 0)

    @pl.when(i == 0)
    def _():
        issue(0, 0)

    _wait_pieces(ys_hbm, buf.at[slot], sem.at[slot])

    @pl.when(i + 1 < n)
    def _():
        issue(i + 1, 1 - slot)

    tm = x1_ref.shape[0]
    lane = lax.broadcasted_iota(jnp.int32, (tm, rows), 1)
    rec = lambda col: route_ref[:, col:col + 1]
    sel = jnp.where(lane == rec(REC_POS1).astype(jnp.int32), rec(REC_W1),
                    jnp.where(lane == rec(REC_POS2).astype(jnp.int32), rec(REC_W2), 0.0)).astype(BF16)
    moe = _dot(sel, buf[slot])
    z = ALPHA * x1_ref[...] + mod_ref[0:1, :] * moe
    o_ref[...] = _ln(z) * ln2_ref[0:1, :] + ln2_ref[1:2, :]


def _combine(piece_glob, ys, x1, route, mod, ln2, tm, tiles_per_batch):
    t, d = x1.shape
    grid_spec = pltpu.PrefetchScalarGridSpec(
        num_scalar_prefetch=1,
        grid=(t // tm,),
        in_specs=[pl.BlockSpec(memory_space=pl.ANY),
                  pl.BlockSpec((tm, d), lambda i, pg: (i, 0)),
                  pl.BlockSpec((tm, ROUTE_LANES), lambda i, pg: (i, 0)),
                  pl.BlockSpec((None, SUBLANES, d), lambda i, pg: (i // tiles_per_batch, 0, 0)),
                  pl.BlockSpec((2, d), lambda i, pg: (0, 0))],
        out_specs=pl.BlockSpec((tm, d), lambda i, pg: (i, 0)),
        scratch_shapes=[pltpu.VMEM((2, _local_rows(tm), d), BF16), pltpu.SemaphoreType.DMA((2,))],
    )
    return pl.pallas_call(
        _combine_kernel,
        grid_spec=grid_spec,
        out_shape=jax.ShapeDtypeStruct((t, d), F32),
        compiler_params=pltpu.CompilerParams(
            dimension_semantics=("arbitrary",), vmem_limit_bytes=VMEM_LIMIT),
        name="combine",
    )(piece_glob, ys, x1, route, mod, ln2)


def _sincos_2d(rows, cols, dim):
    q = dim // 4
    omega = 1.0 / (POS_BASE ** (jnp.arange(q, dtype=F32) / q))
    r = jnp.arange(rows, dtype=F32)[:, None] * omega
    cl = jnp.arange(cols, dtype=F32)[:, None] * omega
    r_emb = jnp.concatenate([jnp.sin(r), jnp.cos(r)], -1)
    c_emb = jnp.concatenate([jnp.sin(cl), jnp.cos(cl)], -1)
    return r_emb, c_emb


def _routing_tables(run_pieces, xs_rows, n_blocks):
    i32 = jnp.int32
    n_tiles = run_pieces.shape[0]
    ppb = ROW_BLOCK // PIECE
    loc_start = jnp.cumsum(run_pieces, axis=1) - run_pieces
    seg_tot = jnp.sum(run_pieces, axis=0)
    seg_pad = (seg_tot + ppb - 1) // ppb * ppb
    seg_end = jnp.cumsum(seg_pad)
    seg_start = seg_end - seg_pad
    run_t = run_pieces.T
    glob_start = seg_start[:, None] + jnp.cumsum(run_t, axis=1) - run_t
    n_used = (seg_end[-1] // ppb).astype(i32)
    blk = jnp.minimum(jnp.arange(n_blocks, dtype=i32), n_used - 1)
    block_e = jnp.minimum(jnp.sum((seg_end[None, :] <= (blk * ppb)[:, None]).astype(i32), axis=1),
                          N_EXPERTS - 1).astype(i32)
    lpt = xs_rows // PIECE
    src0 = jnp.arange(n_tiles, dtype=i32)[None, :] * lpt + loc_start.T
    is_e = block_e[:, None] == jnp.arange(N_EXPERTS, dtype=i32)[None, :]
    of_block = lambda tbl: jnp.sum(jnp.where(is_e[:, :, None], tbl[None], 0), axis=1)[:, None, :]
    p = jnp.arange(n_blocks * ppb, dtype=i32).reshape(n_blocks, ppb, 1)
    within = p - of_block(glob_start)
    hit = (within >= 0) & (within < of_block(run_t))
    piece_src = jnp.sum(jnp.where(hit, (of_block(src0) + within) * PIECE, 0), axis=2).astype(i32).reshape(-1)
    s = jnp.arange(lpt, dtype=i32)
    loc_within = s[None, :, None] - loc_start[:, None, :]
    hit = (loc_within >= 0) & (loc_within < run_pieces[:, None, :])
    piece_glob = jnp.sum(jnp.where(hit, (glob_start.T[:, None, :] + loc_within) * PIECE, 0), axis=2)
    return block_e, piece_src, piece_glob.astype(i32).reshape(-1), n_used.reshape(1)


def kernel(x, c, ctx, c_ctx, w_ada, b_ada, w_in, s5_log_dt_f, s5_a_re_f, s5_a_im_f, s5_b_re_f, s5_b_im_f, s5_c_re_f, s5_c_im_f, s5_log_dt_b, s5_a_re_b, s5_a_im_b, s5_b_re_b, s5_b_im_b, s5_c_re_b, s5_c_im_b, s5_d, s5_w_glu_val, s5_w_glu_gate, conv_w, conv_w_out, w_o, ln1_g, ln1_b, router_w_group, router_b_group, router_w_expert, router_b_expert, exp_w_gate, exp_w_up, exp_w_down, ln2_g, ln2_b):
    b, l, d = x.shape
    lc = ctx.shape[1]
    assert d == D_MODEL and b < SUBLANES and w_ada.shape[0] == DEPTH
    assert l % (SUBLANES * CHUNK) == 0 and lc % (SUBLANES * CHUNK) == 0 and l % GRID_W == 0
    t = b * l
    tm = min(TOKEN_TILE, l)
    tmc = min(TOKEN_TILE, lc)

    cc = jnp.concatenate([c, c_ctx[None, :], jnp.zeros((SUBLANES - b - 1, d), F32)], 0)
    mods = _mods(cc, w_ada[0], b_ada[0])
    sh1, sc1, g1, sh2, sc2, g2 = jnp.split(mods, 6, axis=-1)
    mod_a = jnp.stack([sh1[:b], 1.0 + sc1[:b]], 1)
    mod_ctx = jnp.broadcast_to(jnp.stack([sh1[b], 1.0 + sc1[b]], 0)[None], (b, 2, d))
    mod_c = jnp.stack([g1[:b], sh2[:b], 1.0 + sc2[:b], jnp.zeros((b, d), F32)], 1)
    mod_f = jnp.concatenate([g2[:b, None, :], jnp.zeros((b, SUBLANES - 1, d), F32)], 1)

    w_in_bf = w_in[0].astype(BF16)
    both = lambda fwd, bwd: jnp.concatenate([fwd, bwd], 0)
    s5_tab = _s5_tables(both(s5_log_dt_f, s5_log_dt_b), both(s5_a_re_f, s5_a_re_b), both(s5_a_im_f, s5_a_im_b),
                        both(s5_b_re_f, s5_b_re_b), both(s5_b_im_f, s5_b_im_b),
                        both(s5_c_re_f, s5_c_re_b), both(s5_c_im_f, s5_c_im_b))
    mi, ws, wo_s5, tab = _s5_operators(s5_tab, s5_d[0])

    (uc_ctx,) = _in_proj(ctx, jnp.zeros((lc // GRID_W, d // 2), F32), jnp.zeros((GRID_W, d // 2), F32),
                         mod_ctx, w_in_bf, None, None, tmc, False)
    zero_state = jnp.zeros((N_PAIRS, 4, SUBLANES, LANES), F32)
    s0 = _s5_scan(uc_ctx, None, ws, None, tab, zero_state, b, emit_y=False)

    r_emb, c_emb = _sincos_2d(l // GRID_W, GRID_W, d)
    uc, siga, sb = _in_proj(x, r_emb, c_emb, mod_a, w_in_bf, conv_w[0], conv_w_out[0].astype(BF16),
                            min(2 * TOKEN_TILE, l), True)
    y, _ = _s5_scan(uc, mi, ws, wo_s5, tab, s0, b)

    wr = jnp.concatenate([router_w_group[0], router_w_expert[0],
                          jnp.zeros((d, LOGIT_ROWS - N_EXPERT_GROUPS - N_EXPERTS), F32)], 1).T.astype(BF16)
    br = jnp.concatenate([router_b_group[0], router_b_expert[0],
                          jnp.zeros((LOGIT_ROWS - N_EXPERT_GROUPS - N_EXPERTS,), F32)])[:, None]
    ln1 = jnp.stack([ln1_g[0], ln1_b[0]], 0)
    x1, xs, route, run_len = _mix_out(y, siga, sb, x, r_emb, c_emb, mod_c,
                                      s5_w_glu_val[0].astype(BF16), s5_w_glu_gate[0].astype(BF16),
                                      w_o[0].astype(BF16), ln1, wr, br, tm)

    x1 = x1.reshape(t, d)
    route = route.reshape(t, ROUTE_LANES)
    n_tiles = t // tm
    xs_rows = _local_rows(tm)
    run_pieces = run_len[:, :N_EXPERTS, 0].astype(jnp.int32)
    max_rows = t * TOP_K + n_tiles * N_EXPERTS * (PIECE - 1) + N_EXPERTS * (ROW_BLOCK - 1)
    n_blocks = -(-max_rows // ROW_BLOCK)
    block_e, piece_src, piece_glob, n_used = _routing_tables(run_pieces, xs_rows, n_blocks)
    ys = _experts(block_e, piece_src, n_used, xs, exp_w_gate, exp_w_up, exp_w_down, n_blocks)
    ln2 = jnp.stack([ln2_g[0], ln2_b[0]], 0)
    out = _combine(piece_glob, ys, x1, route, mod_f, ln2, tm, l // tm)
    return out.reshape(b, l, d)
```

```python
import functools
import math

import jax
import jax.numpy as jnp
from jax import lax
from jax.experimental import pallas as pl
from jax.experimental.pallas import tpu as pltpu

F32 = jnp.float32
BF16 = jnp.bfloat16
HI = lax.Precision.HIGHEST

D_MODEL = 1024
GRID_W = 64
S5_WIDTH = 512
S5_GROUP_CH = 16
S5_GROUPS = S5_WIDTH // S5_GROUP_CH
S5_STATE = 64
CONV_WIDTH = 512
N_EXPERT_GROUPS = 4
EXPERTS_PER_GROUP = 8
N_EXPERTS = N_EXPERT_GROUPS * EXPERTS_PER_GROUP
EXPERT_FF = 512
TOP_K = 2
DEPTH = 1
ALPHA = (2.0 * DEPTH) ** 0.25
LN_EPS = 1e-6
POS_BASE = 10000.0
GELU_K = math.sqrt(2.0 / math.pi)
GELU_A = 0.044715

LANES = 128
SUBLANES = 8
CHUNK = 16
GROUP_W = CHUNK * S5_GROUP_CH
PAIR_W = 2 * GROUP_W
N_PAIRS = S5_GROUPS // 2
TOK_PER_VREG = LANES // S5_GROUP_CH
TAB_ROWS = 24
TAB_POW = 2 * SUBLANES
MODS_COLS = 1536
ROUTE_LANES = 128
LOGIT_ROWS = 48
ASSIGN_ROWS = TOP_K * N_EXPERTS
REC_W1, REC_W2, REC_POS1, REC_POS2 = range(4)
TOKEN_TILE = 512
ROW_BLOCK = 512
PIECE = 16
VMEM_LIMIT = 56 * 1024 * 1024


def _ln(x):
    mu = jnp.mean(x, axis=-1, keepdims=True)
    xc = x - mu
    var = jnp.mean(xc * xc, axis=-1, keepdims=True)
    return xc * lax.rsqrt(var + LN_EPS)


def _sigmoid(x):
    return 0.5 * (jnp.tanh(0.5 * x) + 1.0)


def _dot(a, b):
    return jnp.dot(a, b, preferred_element_type=F32)


def _mods_kernel(c_ref, w_ref, b_ref, o_ref):
    c = c_ref[...]
    a = c * _sigmoid(c)
    o_ref[...] = jnp.dot(a, w_ref[...], precision=HI, preferred_element_type=F32) + b_ref[...]


def _mods(cc, w_ada, b_ada):
    n = w_ada.shape[1]
    nb = MODS_COLS
    return pl.pallas_call(
        _mods_kernel,
        grid=(n // nb,),
        in_specs=[pl.BlockSpec((SUBLANES, D_MODEL), lambda i: (0, 0)),
                  pl.BlockSpec((D_MODEL, nb), lambda i: (0, i)),
                  pl.BlockSpec((1, nb), lambda i: (0, i))],
        out_specs=pl.BlockSpec((SUBLANES, nb), lambda i: (0, i)),
        out_shape=jax.ShapeDtypeStruct((SUBLANES, n), F32),
        compiler_params=pltpu.CompilerParams(vmem_limit_bytes=VMEM_LIMIT),
        name="mods",
    )(cc, w_ada, b_ada.reshape(1, n))


def _slot_masks(rows):
    slot = lax.broadcasted_iota(jnp.int32, (rows, LANES), 1) // S5_GROUP_CH
    return [slot == s for s in range(TOK_PER_VREG)]


def _to_chunk_tile(u_scr, uc_ref):
    nch = uc_ref.shape[0]
    masks = _slot_masks(nch)
    for qh in range(CHUNK // TOK_PER_VREG):
        for v in range(S5_WIDTH // LANES):
            src = [u_scr[v, pl.ds(qh * TOK_PER_VREG + s, nch, stride=CHUNK), :] for s in range(TOK_PER_VREG)]
            for i in range(TOK_PER_VREG):
                acc = None
                for s in range(TOK_PER_VREG):
                    shift = ((s - i) * S5_GROUP_CH) % LANES
                    piece = pltpu.roll(src[s], shift, 1) if shift else src[s]
                    acc = piece if acc is None else jnp.where(masks[s], piece, acc)
                lo = (v * TOK_PER_VREG + i) * GROUP_W + qh * LANES
                uc_ref[:, lo:lo + LANES] = acc.astype(uc_ref.dtype)


def _from_chunk_tile(yc_ref, y_scr):
    nch = yc_ref.shape[0]
    masks = _slot_masks(nch)
    for qh in range(CHUNK // TOK_PER_VREG):
        for v in range(S5_WIDTH // LANES):
            src = []
            for i in range(TOK_PER_VREG):
                lo = (v * TOK_PER_VREG + i) * GROUP_W + qh * LANES
                src.append(yc_ref[:, lo:lo + LANES].astype(F32))
            for s in range(TOK_PER_VREG):
                acc = None
                for i in range(TOK_PER_VREG):
                    shift = ((i - s) * S5_GROUP_CH) % LANES
                    piece = pltpu.roll(src[i], shift, 1) if shift else src[i]
                    acc = piece if acc is None else jnp.where(masks[i], piece, acc)
                y_scr[v, pl.ds(qh * TOK_PER_VREG + s, nch, stride=CHUNK), :] = acc


def _with_positions(x_ref, remb_ref, cemb_ref):
    c = cemb_ref[...]
    slabs = []
    for j in range(x_ref.shape[0] // GRID_W):
        r = jnp.broadcast_to(remb_ref[j:j + 1, :], c.shape)
        slabs.append(x_ref[j * GRID_W:(j + 1) * GRID_W, :] + jnp.concatenate([r, c], axis=-1))
    return jnp.concatenate(slabs, axis=0)


def _in_proj_kernel(x_ref, remb_ref, cemb_ref, mod_ref, w_ref, *rest, full):
    if full:
        cw_ref, cwo_ref, uc_ref, siga_ref, sb_ref, u_scr = rest
    else:
        uc_ref, u_scr = rest
    xp = _with_positions(x_ref, remb_ref, cemb_ref)
    h = (_ln(xp) * mod_ref[1:2, :] + mod_ref[0:1, :]).astype(BF16)
    o1 = S5_WIDTH
    o2, o3, o4 = o1 + CONV_WIDTH, o1 + 2 * CONV_WIDTH, o1 + 3 * CONV_WIDTH
    o5 = o4 + D_MODEL
    u = _dot(h, w_ref[:, 0:o1])
    for v in range(S5_WIDTH // LANES):
        u_scr[v] = u[:, v * LANES:(v + 1) * LANES]
    _to_chunk_tile(u_scr, uc_ref)
    if not full:
        return
    z_b = _dot(h, w_ref[:, o1:o2])
    gate_c = _dot(h, w_ref[:, o3:o4])
    p = gate_c * z_b
    tm = p.shape[0]
    col = lax.broadcasted_iota(jnp.int32, (tm, 1), 0) % GRID_W
    prev = jnp.where(col == 0, 0.0, pltpu.roll(p, 1, 0))
    nxt = jnp.where(col == GRID_W - 1, 0.0, pltpu.roll(p, tm - 1, 0))
    v = cw_ref[0:1, :] * prev + cw_ref[1:2, :] * p + cw_ref[2:3, :] * nxt
    gate_b = _dot(h, w_ref[:, o2:o3])
    out_b = _dot((gate_b * v).astype(BF16), cwo_ref[...])
    merge_b = _dot(h, w_ref[:, o5:])
    sb_ref[...] = (_sigmoid(merge_b) * out_b).astype(sb_ref.dtype)
    merge_a = _dot(h, w_ref[:, o4:o5])
    siga_ref[...] = _sigmoid(merge_a).astype(siga_ref.dtype)


def _pos_specs(tm, d):
    return [pl.BlockSpec((tm // GRID_W, d // 2), lambda i, j: (i, 0)),
            pl.BlockSpec((GRID_W, d // 2), lambda i, j: (0, 0))]


def _in_proj(x, r_emb, c_emb, mod, w_in_bf, conv_w, conv_w_out_bf, tm, full):
    b, l, d = x.shape
    n_tiles = l // tm
    grid = (n_tiles, b)
    tok = lambda w: pl.BlockSpec((None, tm, w), lambda i, j: (j, i, 0))
    chunk_spec = pl.BlockSpec((tm // CHUNK, CHUNK * S5_WIDTH), lambda i, j: (j * n_tiles + i, 0))
    chunk_shape = jax.ShapeDtypeStruct((b * l // CHUNK, CHUNK * S5_WIDTH), BF16)
    in_specs = [tok(d)] + _pos_specs(tm, d) + [pl.BlockSpec((None, 2, d), lambda i, j: (j, 0, 0))]
    args = [x, r_emb, c_emb, mod]
    if full:
        in_specs += [pl.BlockSpec(w_in_bf.shape, lambda i, j: (0, 0)),
                     pl.BlockSpec(conv_w.shape, lambda i, j: (0, 0)),
                     pl.BlockSpec(conv_w_out_bf.shape, lambda i, j: (0, 0))]
        args += [w_in_bf, conv_w, conv_w_out_bf]
        out_specs = [chunk_spec, tok(d), tok(d)]
        out_shape = [chunk_shape,
                     jax.ShapeDtypeStruct((b, l, d), BF16),
                     jax.ShapeDtypeStruct((b, l, d), BF16)]
    else:
        in_specs += [pl.BlockSpec((d, S5_WIDTH), lambda i, j: (0, 0))]
        args += [w_in_bf]
        out_specs = [chunk_spec]
        out_shape = [chunk_shape]
    return pl.pallas_call(
        functools.partial(_in_proj_kernel, full=full),
        grid=grid, in_specs=in_specs, out_specs=out_specs, out_shape=out_shape,
        scratch_shapes=[pltpu.VMEM((S5_WIDTH // LANES, tm, LANES), F32)],
        compiler_params=pltpu.CompilerParams(
            dimension_semantics=("arbitrary", "arbitrary"), vmem_limit_bytes=VMEM_LIMIT),
        name="in_proj" if full else "in_proj_ctx",
    )(*args)


def _s5_tables(log_dt, a_re, a_im, b_re, b_im, c_re, c_im):
    f32 = F32
    dt = jnp.exp(log_dt.astype(f32))[..., None]
    a_re = a_re.astype(f32)
    a_im = a_im.astype(f32)
    mag = jnp.exp(dt * a_re)
    ab_re = mag * jnp.cos(dt * a_im)
    ab_im = mag * jnp.sin(dt * a_im)
    den = a_re * a_re + a_im * a_im
    x_re = ab_re - 1.0
    f_re = (x_re * a_re + ab_im * a_im) / den
    f_im = (ab_im * a_re - x_re * a_im) / den
    b_re = b_re.astype(f32)
    b_im = b_im.astype(f32)
    bb_re = f_re[..., None] * b_re - f_im[..., None] * b_im
    bb_im = f_re[..., None] * b_im + f_im[..., None] * b_re
    k = jnp.arange(CHUNK + 1, dtype=f32)[None, :, None, None]
    pmag = jnp.exp(k * (dt * a_re)[:, None])
    p_re = pmag * jnp.cos(k * (dt * a_im)[:, None])
    p_im = pmag * jnp.sin(k * (dt * a_im)[:, None])
    pb_re = p_re[..., None] * bb_re[:, None] - p_im[..., None] * bb_im[:, None]
    pb_im = p_re[..., None] * bb_im[:, None] + p_im[..., None] * bb_re[:, None]
    c_re = c_re.astype(f32)[:, None]
    c_im = c_im.astype(f32)[:, None]
    cp_re = c_re * p_re[:, :, :, None, :] - c_im * p_im[:, :, :, None, :]
    cp_im = -(c_re * p_im[:, :, :, None, :] + c_im * p_re[:, :, :, None, :])
    return dict(p_re=p_re, p_im=p_im, pb_re=pb_re, pb_im=pb_im, cp_re=cp_re, cp_im=cp_im,
                bb_re=bb_re, bb_im=bb_im)


def _lag_kernels(t):
    g, n, c = S5_GROUPS, S5_STATE, S5_GROUP_CH
    k = CHUNK + 1
    lhs = jnp.concatenate([t['cp_re'], t['cp_im']], -1)
    lhs = lhs.transpose(0, 2, 1, 3, 4).reshape(2 * g, k * c, 2 * n)
    rhs = jnp.concatenate([t['bb_re'], t['bb_im']], -2).reshape(2 * g, 2 * n, c)
    out = jnp.einsum('bmn,bnc->bmc', lhs, rhs, precision=HI)
    out = out.reshape(2, g, k, c, c).transpose(0, 2, 1, 4, 3)
    return out[0], out[1]


def _s5_operators(t, s5_d):
    q = CHUNK
    g, n, c = S5_GROUPS, S5_STATE, S5_GROUP_CH
    kern_f, kern_b = _lag_kernels(t)
    k0 = kern_f[0] + kern_b[0] + s5_d.astype(F32)[:, :, None] * jnp.eye(c, dtype=F32)[None]
    kc = jnp.concatenate([kern_b[1:q][::-1], k0[None], kern_f[1:q]], 0)
    kct = kc.transpose(1, 2, 0, 3)
    m_intra = jnp.stack([kct[:, :, q - 1 - i:2 * q - 1 - i, :] for i in range(q)], 1)
    m_intra = m_intra.reshape(g, q * c, q * c)
    w_st = jnp.stack([t['pb_re'][0, :q][::-1], t['pb_im'][0, :q][::-1],
                      t['pb_re'][1, :q], t['pb_im'][1, :q]], 0)
    w_st = w_st.transpose(2, 1, 4, 0, 3).reshape(g, q * c, 4, n)
    w_out = jnp.stack([t['cp_re'][0, 1:], t['cp_im'][0, 1:],
                       t['cp_re'][1, 1:][::-1], t['cp_im'][1, 1:][::-1]], 0)
    w_out = w_out.transpose(2, 0, 4, 1, 3).reshape(g, 4, n, q * c)
    np_ = N_PAIRS
    w_st = w_st.astype(BF16).reshape(np_, 2, q * c, 4, n)
    ws_pair = jnp.concatenate([jnp.pad(w_st[:, 0], ((0, 0), (0, 0), (0, 0), (0, n))),
                               jnp.pad(w_st[:, 1], ((0, 0), (0, 0), (0, 0), (n, 0)))], 1)
    ws_pair = ws_pair.reshape(np_, PAIR_W, 4 * 2 * n)
    w_out = w_out.astype(BF16).reshape(np_, 2, 4, n, q * c)
    wo_pair = jnp.stack([jnp.pad(w_out[:, 0], ((0, 0), (0, 0), (0, 0), (0, q * c))),
                         jnp.pad(w_out[:, 1], ((0, 0), (0, 0), (0, 0), (q * c, 0)))], 2)
    wo_pair = wo_pair.reshape(np_, 4 * 2 * n, PAIR_W)
    tab = _chunk_power_table(t).reshape(2 * TAB_ROWS, np_, 2 * n).transpose(1, 0, 2)
    return m_intra.astype(BF16), ws_pair, wo_pair, tab


def _chunk_power_table(t):
    def cmul(x, y):
        return x[0] * y[0] - x[1] * y[1], x[0] * y[1] + x[1] * y[0]
    p1 = (t['p_re'][:, CHUNK], t['p_im'][:, CHUNK])
    p2 = cmul(p1, p1)
    p4 = cmul(p2, p2)
    p8 = cmul(p4, p4)
    pr = [(jnp.ones_like(p1[0]), jnp.zeros_like(p1[0]))]
    for _ in range(SUBLANES - 1):
        pr.append(cmul(pr[-1], p1))
    pr_re = jnp.stack([p[0] for p in pr], 0)
    pr_im = jnp.stack([p[1] for p in pr], 0)
    pw = jnp.stack([p1[0], p1[1], p2[0], p2[1], p4[0], p4[1], p8[0], p8[1]], 0)
    return jnp.concatenate([pr_re[:, 0], pr_im[:, 0], pw[:, 0],
                            pr_re[::-1, 1], pr_im[::-1, 1], pw[:, 1]], 0)


def _s5_scan_kernel(uc_ref, *rest, batch, emit_y):
    if emit_y:
        mi_ref, ws_ref, wo_ref, tab_ref, s0_ref, y_ref, fin_ref, s_scr, in_scr = rest
    else:
        ws_ref, tab_ref, s0_ref, fin_ref, s_scr = rest
    rows = uc_ref.shape[0]
    chunks = rows // batch
    n_tiles = chunks // SUBLANES
    u = uc_ref[...]
    s_scr[...] = _dot(u, ws_ref[...])
    row = lax.broadcasted_iota(jnp.int32, (SUBLANES, LANES), 0)

    def tile_scan(r0, backward, c_re, c_im):
        base = TAB_ROWS if backward else 0
        col = 2 * LANES if backward else 0
        rs = pl.ds(r0, SUBLANES)

        def shift(z, k):
            if backward:
                return jnp.where(row < SUBLANES - k, pltpu.roll(z, SUBLANES - k, 0), 0.0)
            return jnp.where(row >= k, pltpu.roll(z, k, 0), 0.0)

        z_re = s_scr[rs, col:col + LANES]
        z_im = s_scr[rs, col + LANES:col + 2 * LANES]
        for k, t in ((1, TAB_POW), (2, TAB_POW + 2), (4, TAB_POW + 4)):
            a_re = tab_ref[base + t:base + t + 1, :]
            a_im = tab_ref[base + t + 1:base + t + 2, :]
            sh_re = shift(z_re, k)
            sh_im = shift(z_im, k)
            z_re, z_im = z_re + (a_re * sh_re - a_im * sh_im), z_im + (a_re * sh_im + a_im * sh_re)
        pr_re = tab_ref[base:base + SUBLANES, :]
        pr_im = tab_ref[base + SUBLANES:base + 2 * SUBLANES, :]
        if emit_y:
            in_scr[rs, col:col + LANES] = pr_re * c_re - pr_im * c_im + shift(z_re, 1)
            in_scr[rs, col + LANES:col + 2 * LANES] = pr_re * c_im + pr_im * c_re + shift(z_im, 1)
        last = 0 if backward else SUBLANES - 1
        l_re = jnp.broadcast_to(z_re[last:last + 1, :], (SUBLANES, LANES))
        l_im = jnp.broadcast_to(z_im[last:last + 1, :], (SUBLANES, LANES))
        p8_re = tab_ref[base + TAB_POW + 6:base + TAB_POW + 7, :]
        p8_im = tab_ref[base + TAB_POW + 7:base + TAB_POW + 8, :]
        return p8_re * c_re - p8_im * c_im + l_re, p8_re * c_im + p8_im * c_re + l_im

    def body(m, carry):
        out = []
        for b in range(batch):
            cf_re, cf_im, cb_re, cb_im = carry[4 * b:4 * b + 4]
            rf = pl.multiple_of(b * chunks + m * SUBLANES, SUBLANES)
            rb = pl.multiple_of(b * chunks + (n_tiles - 1 - m) * SUBLANES, SUBLANES)
            out += list(tile_scan(rf, False, cf_re, cf_im))
            out += list(tile_scan(rb, True, cb_re, cb_im))
        return tuple(out)

    init = tuple(jnp.broadcast_to(s0_ref[t, b:b + 1, :], (SUBLANES, LANES))
                 for b in range(batch) for t in range(4))
    fin = lax.fori_loop(0, n_tiles, body, init, unroll=min(4, n_tiles))
    fin_ref[...] = jnp.zeros(fin_ref.shape, F32)
    for b in range(batch):
        for t in range(4):
            fin_ref[t, b:b + 1, :] = fin[4 * b + t][0:1, :]
    if emit_y:
        y_intra = jnp.concatenate(
            [_dot(u[:, gl * GROUP_W:(gl + 1) * GROUP_W], mi_ref[gl]) for gl in range(2)], axis=-1)
        y = y_intra + _dot(in_scr[...].astype(BF16), wo_ref[...])
        y_ref[...] = y.astype(y_ref.dtype)


def _s5_scan(uc, mi, ws, wo, tab, s0, batch, emit_y=True):
    rows = uc.shape[0]
    pair = lambda *shape: pl.BlockSpec((None,) + shape, lambda p: (p,) + (0,) * len(shape))
    uc_spec = pl.BlockSpec((rows, PAIR_W), lambda p: (0, p))
    state_spec = pair(4, SUBLANES, LANES)
    state_shape = jax.ShapeDtypeStruct((N_PAIRS, 4, SUBLANES, LANES), F32)
    scratch = [pltpu.VMEM((rows, PAIR_W), F32)]
    if emit_y:
        in_specs = [uc_spec, pl.BlockSpec((2, GROUP_W, GROUP_W), lambda p: (p, 0, 0)),
                    pair(PAIR_W, PAIR_W), pair(PAIR_W, PAIR_W), pair(2 * TAB_ROWS, LANES), state_spec]
        args = (uc, mi, ws, wo, tab, s0)
        out_specs = [uc_spec, state_spec]
        out_shape = [jax.ShapeDtypeStruct((rows, N_PAIRS * PAIR_W), BF16), state_shape]
        scratch = scratch * 2
    else:
        in_specs = [uc_spec, pair(PAIR_W, PAIR_W), pair(2 * TAB_ROWS, LANES), state_spec]
        args = (uc, ws, tab, s0)
        out_specs = state_spec
        out_shape = state_shape
    return pl.pallas_call(
        functools.partial(_s5_scan_kernel, batch=batch, emit_y=emit_y),
        grid=(N_PAIRS,), in_specs=in_specs, out_specs=out_specs, out_shape=out_shape,
        scratch_shapes=scratch,
        compiler_params=pltpu.CompilerParams(
            dimension_semantics=("arbitrary",), vmem_limit_bytes=VMEM_LIMIT),
        name="s5_scan" if emit_y else "s5_scan_ctx",
    )(*args)


def _mix_out_kernel(y_ref, siga_ref, sb_ref, x_ref, remb_ref, cemb_ref, mod_ref, wv_ref, wg_ref, wo_ref,
                    ln1_ref, wr_ref, br_ref, ut_ref, lt_ref,
                    x1_ref, xs_ref, route_ref, len_ref, y_scr):
    _from_chunk_tile(y_ref, y_scr)
    y = jnp.concatenate([y_scr[v] for v in range(S5_WIDTH // LANES)], axis=-1)
    half_y = 0.5 * y
    ya = (half_y + half_y * jnp.tanh(y * (GELU_K + (GELU_K * GELU_A) * (y * y)))).astype(BF16)
    out_a = _dot(ya, wv_ref[...]) * _sigmoid(_dot(ya, wg_ref[...]))
    merged = siga_ref[...].astype(F32) * out_a + sb_ref[...].astype(F32)
    mix = _dot(merged.astype(BF16), wo_ref[...])
    xp = _with_positions(x_ref, remb_ref, cemb_ref)
    x1 = _ln(ALPHA * xp + mod_ref[0:1, :] * mix) * ln1_ref[0:1, :] + ln1_ref[1:2, :]
    x1_ref[...] = x1
    h_hi = (_ln(x1) * mod_ref[2:3, :] + mod_ref[1:2, :]).astype(BF16)
    _route_and_sort(h_hi, wr_ref, br_ref, ut_ref, lt_ref, xs_ref, route_ref, len_ref)


def _route_and_sort(h_hi, wr_ref, br_ref, ut_ref, lt_ref, xs_ref, route_ref, len_ref):
    tm = h_hi.shape[0]
    f32 = F32
    nt = (((1,), (1,)), ((), ()))
    lg = lax.dot_general(wr_ref[...], h_hi, nt, preferred_element_type=f32) + br_ref[...]
    rowi = lax.broadcasted_iota(jnp.int32, (LOGIT_ROWS, tm), 0).astype(f32)
    neg = jnp.float32(-jnp.inf)
    big = jnp.float32(LOGIT_ROWS)
    gl = jnp.where(rowi < N_EXPERT_GROUPS, lg, neg)
    gmax = jnp.max(gl, axis=0, keepdims=True)
    g_idx = jnp.min(jnp.where(gl == gmax, rowi, big), axis=0, keepdims=True)
    p_group = 1.0 / jnp.sum(jnp.exp(gl - gmax), axis=0, keepdims=True)
    e_lo = N_EXPERT_GROUPS + g_idx * EXPERTS_PER_GROUP
    el = jnp.where((rowi >= e_lo) & (rowi < e_lo + EXPERTS_PER_GROUP), lg, neg)
    m1 = jnp.max(el, axis=0, keepdims=True)
    i1 = jnp.min(jnp.where(el == m1, rowi, big), axis=0, keepdims=True)
    el2 = jnp.where(rowi == i1, neg, el)
    m2 = jnp.max(el2, axis=0, keepdims=True)
    i2 = jnp.min(jnp.where(el2 == m2, rowi, big), axis=0, keepdims=True)
    r = jnp.exp(m2 - m1)
    w1 = p_group / (1.0 + r)
    w2 = p_group * r / (1.0 + r)
    e1 = i1 - N_EXPERT_GROUPS
    e2 = i2 - N_EXPERT_GROUPS
    rowa = lax.broadcasted_iota(jnp.int32, (ASSIGN_ROWS, tm), 0).astype(f32)
    a12 = jnp.where(rowa == e1, 1.0, 0.0) + jnp.where(rowa == e2 + N_EXPERTS, 1.0, 0.0)
    rank = _dot(a12.astype(BF16), ut_ref[...])
    cnt = jnp.broadcast_to(jnp.sum(a12, axis=1, keepdims=True), (ASSIGN_ROWS, LANES))
    row = lax.broadcasted_iota(jnp.int32, (ASSIGN_ROWS, LANES), 0)
    tot = cnt + pltpu.roll(cnt, N_EXPERTS, 0)
    run = jnp.where(row < N_EXPERTS, jnp.floor((tot + (PIECE - 1)) * (1.0 / PIECE)), 0.0)
    off = PIECE * _dot(lt_ref[...], run.astype(BF16))
    base = jnp.where(row < N_EXPERTS, off, pltpu.roll(off + cnt, N_EXPERTS, 0))
    posmat = a12 * (rank + base[:, 0:1])
    pos1 = jnp.sum(posmat[0:N_EXPERTS, :], axis=0, keepdims=True)
    pos2 = jnp.sum(posmat[N_EXPERTS:2 * N_EXPERTS, :], axis=0, keepdims=True)
    ri = lax.broadcasted_iota(jnp.int32, (xs_ref.shape[0], tm), 0)
    perm = jnp.where((ri == pos1.astype(jnp.int32)) | (ri == pos2.astype(jnp.int32)), 1.0, 0.0).astype(BF16)
    xs_ref[...] = _dot(perm, h_hi).astype(xs_ref.dtype)
    rowr = lax.broadcasted_iota(jnp.int32, (SUBLANES, tm), 0)
    rec = jnp.zeros((SUBLANES, tm), f32)
    for col, val in ((REC_W1, w1), (REC_W2, w2), (REC_POS1, pos1), (REC_POS2, pos2)):
        rec = jnp.where(rowr == col, val, rec)
    rec = jnp.concatenate([rec, jnp.zeros((ROUTE_LANES - SUBLANES, tm), f32)], axis=0)
    route_ref[...] = rec.T
    len_ref[...] = run


def _local_rows(tm):
    return -(-(TOP_K * tm + N_EXPERTS * (PIECE - 1)) // PIECE) * PIECE


def _mix_out(y, siga, sb, x, r_emb, c_emb, mod, wv, wg, wo, ln1, wr, br, tm):
    b, l, d = x.shape
    n_tiles = l // tm
    xs_rows = _local_rows(tm)
    tok = lambda w: pl.BlockSpec((None, tm, w), lambda i, j: (j, i, 0))
    whole = lambda a: pl.BlockSpec(a.shape, lambda i, j: (0,) * a.ndim)
    ut = (jnp.arange(tm)[:, None] < jnp.arange(tm)[None, :]).astype(BF16)
    lt = (jnp.arange(ASSIGN_ROWS)[None, :] < jnp.arange(ASSIGN_ROWS)[:, None]).astype(BF16)
    return pl.pallas_call(
        _mix_out_kernel,
        grid=(n_tiles, b),
        in_specs=[pl.BlockSpec((tm // CHUNK, CHUNK * S5_WIDTH), lambda i, j: (j * n_tiles + i, 0)),
                  tok(d), tok(d), tok(d)] + _pos_specs(tm, d) + [
                  pl.BlockSpec((None, 4, d), lambda i, j: (j, 0, 0)),
                  whole(wv), whole(wg), whole(wo), whole(ln1), whole(wr), whole(br),
                  whole(ut), whole(lt)],
        out_specs=[tok(d),
                   pl.BlockSpec((xs_rows, d), lambda i, j: (j * n_tiles + i, 0)),
                   tok(ROUTE_LANES),
                   pl.BlockSpec((None, ASSIGN_ROWS, LANES), lambda i, j: (j * n_tiles + i, 0, 0))],
        out_shape=[jax.ShapeDtypeStruct((b, l, d), F32),
                   jax.ShapeDtypeStruct((b * n_tiles * xs_rows, d), BF16),
                   jax.ShapeDtypeStruct((b, l, ROUTE_LANES), F32),
                   jax.ShapeDtypeStruct((b * n_tiles, ASSIGN_ROWS, LANES), F32)],
        scratch_shapes=[pltpu.VMEM((S5_WIDTH // LANES, tm, LANES), F32)],
        compiler_params=pltpu.CompilerParams(
            dimension_semantics=("arbitrary", "arbitrary"), vmem_limit_bytes=VMEM_LIMIT),
        name="mix_out",
    )(y, siga, sb, x, r_emb, c_emb, mod, wv, wg, wo, ln1, wr, br, ut, lt)


def _piece_copy(src_hbm, src_row, dst, piece, sem):
    return pltpu.make_async_copy(src_hbm.at[pl.ds(pl.multiple_of(src_row, PIECE), PIECE), :],
                                 dst.at[pl.ds(pl.multiple_of(piece * PIECE, PIECE), PIECE), :], sem)


def _issue_pieces(src_hbm, table_ref, first, n_pieces, dst, sem):
    for p in range(n_pieces):
        _piece_copy(src_hbm, table_ref[first + p], dst, p, sem).start(priority=p % 2)


def _wait_pieces(src_hbm, dst, sem):
    pltpu.make_async_copy(src_hbm.at[pl.ds(0, dst.shape[0]), :], dst, sem).wait()


def _experts_kernel(be_ref, piece_ref, nused_ref, xs_hbm, wg_ref, wu_ref, wd_ref, ys_ref,
                    xs_buf0, xs_buf1, wg_bf, wu_bf, wd_bf, sem):
    i = pl.program_id(0)
    n_used = nused_ref[0]
    per_block = ROW_BLOCK // PIECE

    @pl.when(i == 0)
    def _():
        _issue_pieces(xs_hbm, piece_ref, 0, per_block, xs_buf0, sem.at[0])

    def block(cur, cur_sem, oth, oth_sem):
        @pl.when((i == 0) | (be_ref[i] != be_ref[jnp.maximum(i - 1, 0)]))
        def _():
            wg_bf[...] = wg_ref[...].astype(BF16)
            wu_bf[...] = wu_ref[...].astype(BF16)
            wd_bf[...] = wd_ref[...].astype(BF16)

        _wait_pieces(xs_hbm, cur, cur_sem)
        nxt = jnp.minimum(i + 1, n_used - 1)
        _issue_pieces(xs_hbm, piece_ref, nxt * per_block, per_block, oth, oth_sem)

        xb = cur[...]
        gate = _dot(xb, wg_bf[...])
        up = _dot(xb, wu_bf[...])
        hid = (gate * _sigmoid(gate) * up).astype(BF16)
        ys_ref[...] = _dot(hid, wd_bf[...]).astype(ys_ref.dtype)

        @pl.when(i == n_used - 1)
        def _():
            _wait_pieces(xs_hbm, oth, oth_sem)

    for s, (cur, oth) in enumerate(((xs_buf0, xs_buf1), (xs_buf1, xs_buf0))):
        @pl.when((i < n_used) & (i % 2 == s))
        def _(s=s, cur=cur, oth=oth):
            block(cur, sem.at[s], oth, sem.at[1 - s])

    @pl.when(i >= n_used)
    def _():
        ys_ref[...] = jnp.zeros(ys_ref.shape, ys_ref.dtype)


def _experts(block_e, piece_src, n_used, xs, wg, wu, wd, n_blocks):
    d = xs.shape[1]
    by_expert = lambda i, be, ps, nu: (0, be[i], 0, 0)
    grid_spec = pltpu.PrefetchScalarGridSpec(
        num_scalar_prefetch=3,
        grid=(n_blocks,),
        in_specs=[pl.BlockSpec(memory_space=pl.ANY),
                  pl.BlockSpec((None, None, d, EXPERT_FF), by_expert),
                  pl.BlockSpec((None, None, d, EXPERT_FF), by_expert),
                  pl.BlockSpec((None, None, EXPERT_FF, d), by_expert)],
        out_specs=pl.BlockSpec((ROW_BLOCK, d), lambda i, be, ps, nu: (i, 0)),
        scratch_shapes=[pltpu.VMEM((ROW_BLOCK, d), BF16), pltpu.VMEM((ROW_BLOCK, d), BF16),
                        pltpu.VMEM((d, EXPERT_FF), BF16), pltpu.VMEM((d, EXPERT_FF), BF16),
                        pltpu.VMEM((EXPERT_FF, d), BF16), pltpu.SemaphoreType.DMA((2,))],
    )
    return pl.pallas_call(
        _experts_kernel,
        grid_spec=grid_spec,
        out_shape=jax.ShapeDtypeStruct((n_blocks * ROW_BLOCK, d), BF16),
        compiler_params=pltpu.CompilerParams(
            dimension_semantics=("arbitrary",), vmem_limit_bytes=VMEM_LIMIT),
        name="experts",
    )(block_e, piece_src, n_used, xs, wg, wu, wd)


def _combine_kernel(piece_ref, ys_hbm, x1_ref, route_ref, mod_ref, ln2_ref, o_ref, buf, sem):
    i = pl.program_id(0)
    n = pl.num_programs(0)
    slot = i % 2
    rows = buf.shape[1]
    per_tile = rows // PIECE

    def issue(tile, s):
        def body(h, _):
            for prio in range(2):
                p = 2 * h + prio
                _piece_copy(ys_hbm, piece_ref[tile * per_tile + p], buf.at[s], p, sem.at[s]).start(priority=prio)
            return 0
        lax.fori_loop(0, per_tile // 2, body, 0)

    @pl.when(i == 0)
    def _():
        issue(0, 0)

    _wait_pieces(ys_hbm, buf.at[slot], sem.at[slot])

    @pl.when(i + 1 < n)
    def _():
        issue(i + 1, 1 - slot)

    tm = x1_ref.shape[0]
    lane = lax.broadcasted_iota(jnp.int32, (tm, rows), 1)
    rec = lambda col: route_ref[:, col:col + 1]
    sel = jnp.where(lane == rec(REC_POS1).astype(jnp.int32), rec(REC_W1),
                    jnp.where(lane == rec(REC_POS2).astype(jnp.int32), rec(REC_W2), 0.0)).astype(BF16)
    moe = _dot(sel, buf[slot])
    z = ALPHA * x1_ref[...] + mod_ref[0:1, :] * moe
    o_ref[...] = _ln(z) * ln2_ref[0:1, :] + ln2_ref[1:2, :]


def _combine(piece_glob, ys, x1, route, mod, ln2, tm, tiles_per_batch):
    t, d = x1.shape
    assert (_local_rows(tm) // PIECE) % 2 == 0
    grid_spec = pltpu.PrefetchScalarGridSpec(
        num_scalar_prefetch=1,
        grid=(t // tm,),
        in_specs=[pl.BlockSpec(memory_space=pl.ANY),
                  pl.BlockSpec((tm, d), lambda i, pg: (i, 0)),
                  pl.BlockSpec((tm, ROUTE_LANES), lambda i, pg: (i, 0)),
                  pl.BlockSpec((None, SUBLANES, d), lambda i, pg: (i // tiles_per_batch, 0, 0)),
                  pl.BlockSpec((2, d), lambda i, pg: (0, 0))],
        out_specs=pl.BlockSpec((tm, d), lambda i, pg: (i, 0)),
        scratch_shapes=[pltpu.VMEM((2, _local_rows(tm), d), BF16), pltpu.SemaphoreType.DMA((2,))],
    )
    return pl.pallas_call(
        _combine_kernel,
        grid_spec=grid_spec,
        out_shape=jax.ShapeDtypeStruct((t, d), F32),
        compiler_params=pltpu.CompilerParams(
            dimension_semantics=("arbitrary",), vmem_limit_bytes=VMEM_LIMIT),
        name="combine",
    )(piece_glob, ys, x1, route, mod, ln2)


def _sincos_2d(rows, cols, dim):
    q = dim // 4
    omega = 1.0 / (POS_BASE ** (jnp.arange(q, dtype=F32) / q))
    r = jnp.arange(rows, dtype=F32)[:, None] * omega
    cl = jnp.arange(cols, dtype=F32)[:, None] * omega
    r_emb = jnp.concatenate([jnp.sin(r), jnp.cos(r)], -1)
    c_emb = jnp.concatenate([jnp.sin(cl), jnp.cos(cl)], -1)
    return r_emb, c_emb


def _routing_tables(run_pieces, xs_rows, n_blocks):
    i32 = jnp.int32
    n_tiles = run_pieces.shape[0]
    ppb = ROW_BLOCK // PIECE
    loc_start = jnp.cumsum(run_pieces, axis=1) - run_pieces
    seg_tot = jnp.sum(run_pieces, axis=0)
    seg_pad = (seg_tot + ppb - 1) // ppb * ppb
    seg_end = jnp.cumsum(seg_pad)
    seg_start = seg_end - seg_pad
    run_t = run_pieces.T
    glob_start = seg_start[:, None] + jnp.cumsum(run_t, axis=1) - run_t
    n_used = (seg_end[-1] // ppb).astype(i32)
    blk = jnp.minimum(jnp.arange(n_blocks, dtype=i32), n_used - 1)
    block_e = jnp.minimum(jnp.sum((seg_end[None, :] <= (blk * ppb)[:, None]).astype(i32), axis=1),
                          N_EXPERTS - 1).astype(i32)
    lpt = xs_rows // PIECE
    src0 = jnp.arange(n_tiles, dtype=i32)[None, :] * lpt + loc_start.T
    is_e = block_e[:, None] == jnp.arange(N_EXPERTS, dtype=i32)[None, :]
    of_block = lambda tbl: jnp.sum(jnp.where(is_e[:, :, None], tbl[None], 0), axis=1)[:, None, :]
    p = jnp.arange(n_blocks * ppb, dtype=i32).reshape(n_blocks, ppb, 1)
    within = p - of_block(glob_start)
    hit = (within >= 0) & (within < of_block(run_t))
    piece_src = jnp.sum(jnp.where(hit, (of_block(src0) + within) * PIECE, 0), axis=2).astype(i32).reshape(-1)
    s = jnp.arange(lpt, dtype=i32)
    loc_within = s[None, :, None] - loc_start[:, None, :]
    hit = (loc_within >= 0) & (loc_within < run_pieces[:, None, :])
    piece_glob = jnp.sum(jnp.where(hit, (glob_start.T[:, None, :] + loc_within) * PIECE, 0), axis=2)
    return block_e, piece_src, piece_glob.astype(i32).reshape(-1), n_used.reshape(1)


def kernel(x, c, ctx, c_ctx, w_ada, b_ada, w_in, s5_log_dt_f, s5_a_re_f, s5_a_im_f, s5_b_re_f, s5_b_im_f, s5_c_re_f, s5_c_im_f, s5_log_dt_b, s5_a_re_b, s5_a_im_b, s5_b_re_b, s5_b_im_b, s5_c_re_b, s5_c_im_b, s5_d, s5_w_glu_val, s5_w_glu_gate, conv_w, conv_w_out, w_o, ln1_g, ln1_b, router_w_group, router_b_group, router_w_expert, router_b_expert, exp_w_gate, exp_w_up, exp_w_down, ln2_g, ln2_b):
    b, l, d = x.shape
    lc = ctx.shape[1]
    assert d == D_MODEL and b < SUBLANES and w_ada.shape[0] == DEPTH
    assert l % (SUBLANES * CHUNK) == 0 and lc % (SUBLANES * CHUNK) == 0 and l % GRID_W == 0
    t = b * l
    tm = min(TOKEN_TILE, l)
    tmc = min(TOKEN_TILE, lc)

    cc = jnp.concatenate([c, c_ctx[None, :], jnp.zeros((SUBLANES - b - 1, d), F32)], 0)
    mods = _mods(cc, w_ada[0], b_ada[0])
    sh1, sc1, g1, sh2, sc2, g2 = jnp.split(mods, 6, axis=-1)
    mod_a = jnp.stack([sh1[:b], 1.0 + sc1[:b]], 1)
    mod_ctx = jnp.broadcast_to(jnp.stack([sh1[b], 1.0 + sc1[b]], 0)[None], (b, 2, d))
    mod_c = jnp.stack([g1[:b], sh2[:b], 1.0 + sc2[:b], jnp.zeros((b, d), F32)], 1)
    mod_f = jnp.concatenate([g2[:b, None, :], jnp.zeros((b, SUBLANES - 1, d), F32)], 1)

    w_in_bf = w_in[0].astype(BF16)
    both = lambda fwd, bwd: jnp.concatenate([fwd, bwd], 0)
    s5_tab = _s5_tables(both(s5_log_dt_f, s5_log_dt_b), both(s5_a_re_f, s5_a_re_b), both(s5_a_im_f, s5_a_im_b),
                        both(s5_b_re_f, s5_b_re_b), both(s5_b_im_f, s5_b_im_b),
                        both(s5_c_re_f, s5_c_re_b), both(s5_c_im_f, s5_c_im_b))
    mi, ws, wo_s5, tab = _s5_operators(s5_tab, s5_d[0])

    (uc_ctx,) = _in_proj(ctx, jnp.zeros((lc // GRID_W, d // 2), F32), jnp.zeros((GRID_W, d // 2), F32),
                         mod_ctx, w_in_bf, None, None, tmc, False)
    zero_state = jnp.zeros((N_PAIRS, 4, SUBLANES, LANES), F32)
    s0 = _s5_scan(uc_ctx, None, ws, None, tab, zero_state, b, emit_y=False)

    r_emb, c_emb = _sincos_2d(l // GRID_W, GRID_W, d)
    uc, siga, sb = _in_proj(x, r_emb, c_emb, mod_a, w_in_bf, conv_w[0], conv_w_out[0].astype(BF16),
                            min(2 * TOKEN_TILE, l), True)
    y, _ = _s5_scan(uc, mi, ws, wo_s5, tab, s0, b)

    wr = jnp.concatenate([router_w_group[0], router_w_expert[0],
                          jnp.zeros((d, LOGIT_ROWS - N_EXPERT_GROUPS - N_EXPERTS), F32)], 1).T.astype(BF16)
    br = jnp.concatenate([router_b_group[0], router_b_expert[0],
                          jnp.zeros((LOGIT_ROWS - N_EXPERT_GROUPS - N_EXPERTS,), F32)])[:, None]
    ln1 = jnp.stack([ln1_g[0], ln1_b[0]], 0)
    x1, xs, route, run_len = _mix_out(y, siga, sb, x, r_emb, c_emb, mod_c,
                                      s5_w_glu_val[0].astype(BF16), s5_w_glu_gate[0].astype(BF16),
                                      w_o[0].astype(BF16), ln1, wr, br, tm)

    x1 = x1.reshape(t, d)
    route = route.reshape(t, ROUTE_LANES)
    n_tiles = t // tm
    xs_rows = _local_rows(tm)
    run_pieces = run_len[:, :N_EXPERTS, 0].astype(jnp.int32)
    max_rows = t * TOP_K + n_tiles * N_EXPERTS * (PIECE - 1) + N_EXPERTS * (ROW_BLOCK - 1)
    n_blocks = -(-max_rows // ROW_BLOCK)
    block_e, piece_src, piece_glob, n_used = _routing_tables(run_pieces, xs_rows, n_blocks)
    ys = _experts(block_e, piece_src, n_used, xs, exp_w_gate, exp_w_up, exp_w_down, n_blocks)
    ln2 = jnp.stack([ln2_g[0], ln2_b[0]], 0)
    out = _combine(piece_glob, ys, x1, route, mod_f, ln2, tm, l // tm)
    return out.reshape(b, l, d)
```

```python
import functools
import math

import jax
import jax.numpy as jnp
from jax import lax
from jax.experimental import pallas as pl
from jax.experimental.pallas import tpu as pltpu

F32 = jnp.float32
BF16 = jnp.bfloat16
HI = lax.Precision.HIGHEST

D_MODEL = 1024
GRID_W = 64
S5_WIDTH = 512
S5_GROUP_CH = 16
S5_GROUPS = S5_WIDTH // S5_GROUP_CH
S5_STATE = 64
CONV_WIDTH = 512
N_EXPERT_GROUPS = 4
EXPERTS_PER_GROUP = 8
N_EXPERTS = N_EXPERT_GROUPS * EXPERTS_PER_GROUP
EXPERT_FF = 512
TOP_K = 2
DEPTH = 1
ALPHA = (2.0 * DEPTH) ** 0.25
LN_EPS = 1e-6
POS_BASE = 10000.0
GELU_K = math.sqrt(2.0 / math.pi)
GELU_A = 0.044715

LANES = 128
SUBLANES = 8
CHUNK = 16
GROUP_W = CHUNK * S5_GROUP_CH
PAIR_W = 2 * GROUP_W
N_PAIRS = S5_GROUPS // 2
TOK_PER_VREG = LANES // S5_GROUP_CH
TAB_ROWS = 24
TAB_POW = 2 * SUBLANES
MODS_COLS = 1536
ROUTE_LANES = 128
LOGIT_ROWS = 48
ASSIGN_ROWS = TOP_K * N_EXPERTS
REC_W1, REC_W2, REC_POS1, REC_POS2 = range(4)
TOKEN_TILE = 512
ROW_BLOCK = 512
PIECE = 16
VMEM_LIMIT = 56 * 1024 * 1024


def _ln(x):
    mu = jnp.mean(x, axis=-1, keepdims=True)
    xc = x - mu
    var = jnp.mean(xc * xc, axis=-1, keepdims=True)
    return xc * lax.rsqrt(var + LN_EPS)


def _sigmoid(x):
    return 0.5 * (jnp.tanh(0.5 * x) + 1.0)


def _dot(a, b):
    return jnp.dot(a, b, preferred_element_type=F32)


def _mods_kernel(c_ref, w_ref, b_ref, o_ref):
    c = c_ref[...]
    a = c * _sigmoid(c)
    o_ref[...] = jnp.dot(a, w_ref[...], precision=HI, preferred_element_type=F32) + b_ref[...]


def _mods(cc, w_ada, b_ada):
    n = w_ada.shape[1]
    nb = MODS_COLS
    return pl.pallas_call(
        _mods_kernel,
        grid=(n // nb,),
        in_specs=[pl.BlockSpec((SUBLANES, D_MODEL), lambda i: (0, 0)),
                  pl.BlockSpec((D_MODEL, nb), lambda i: (0, i)),
                  pl.BlockSpec((1, nb), lambda i: (0, i))],
        out_specs=pl.BlockSpec((SUBLANES, nb), lambda i: (0, i)),
        out_shape=jax.ShapeDtypeStruct((SUBLANES, n), F32),
        compiler_params=pltpu.CompilerParams(vmem_limit_bytes=VMEM_LIMIT),
        name="mods",
    )(cc, w_ada, b_ada.reshape(1, n))


def _slot_masks(rows):
    slot = lax.broadcasted_iota(jnp.int32, (rows, LANES), 1) // S5_GROUP_CH
    return [slot == s for s in range(TOK_PER_VREG)]


def _to_chunk_tile(u_scr, uc_ref):
    nch = uc_ref.shape[0]
    masks = _slot_masks(nch)
    for qh in range(CHUNK // TOK_PER_VREG):
        for v in range(S5_WIDTH // LANES):
            src = [u_scr[v, pl.ds(qh * TOK_PER_VREG + s, nch, stride=CHUNK), :] for s in range(TOK_PER_VREG)]
            for i in range(TOK_PER_VREG):
                acc = None
                for s in range(TOK_PER_VREG):
                    shift = ((s - i) * S5_GROUP_CH) % LANES
                    piece = pltpu.roll(src[s], shift, 1) if shift else src[s]
                    acc = piece if acc is None else jnp.where(masks[s], piece, acc)
                lo = (v * TOK_PER_VREG + i) * GROUP_W + qh * LANES
                uc_ref[:, lo:lo + LANES] = acc.astype(uc_ref.dtype)


def _from_chunk_tile(yc_ref, y_scr):
    nch = yc_ref.shape[0]
    masks = _slot_masks(nch)
    for qh in range(CHUNK // TOK_PER_VREG):
        for v in range(S5_WIDTH // LANES):
            src = []
            for i in range(TOK_PER_VREG):
                lo = (v * TOK_PER_VREG + i) * GROUP_W + qh * LANES
                src.append(yc_ref[:, lo:lo + LANES].astype(F32))
            for s in range(TOK_PER_VREG):
                acc = None
                for i in range(TOK_PER_VREG):
                    shift = ((i - s) * S5_GROUP_CH) % LANES
                    piece = pltpu.roll(src[i], shift, 1) if shift else src[i]
                    acc = piece if acc is None else jnp.where(masks[i], piece, acc)
                y_scr[v, pl.ds(qh * TOK_PER_VREG + s, nch, stride=CHUNK), :] = acc


def _with_positions(x_ref, remb_ref, cemb_ref):
    c = cemb_ref[...]
    slabs = []
    for j in range(x_ref.shape[0] // GRID_W):
        r = jnp.broadcast_to(remb_ref[j:j + 1, :], c.shape)
        slabs.append(x_ref[j * GRID_W:(j + 1) * GRID_W, :] + jnp.concatenate([r, c], axis=-1))
    return jnp.concatenate(slabs, axis=0)


def _in_proj_kernel(x_ref, remb_ref, cemb_ref, mod_ref, w_ref, *rest, full):
    if full:
        cw_ref, cwo_ref, uc_ref, siga_ref, sb_ref, u_scr = rest
    else:
        uc_ref, u_scr = rest
    xp = _with_positions(x_ref, remb_ref, cemb_ref)
    h = (_ln(xp) * mod_ref[1:2, :] + mod_ref[0:1, :]).astype(BF16)
    o1 = S5_WIDTH
    o2, o3, o4 = o1 + CONV_WIDTH, o1 + 2 * CONV_WIDTH, o1 + 3 * CONV_WIDTH
    o5 = o4 + D_MODEL
    u = _dot(h, w_ref[:, 0:o1])
    for v in range(S5_WIDTH // LANES):
        u_scr[v] = u[:, v * LANES:(v + 1) * LANES]
    _to_chunk_tile(u_scr, uc_ref)
    if not full:
        return
    z_b = _dot(h, w_ref[:, o1:o2])
    gate_c = _dot(h, w_ref[:, o3:o4])
    p = gate_c * z_b
    tm = p.shape[0]
    col = lax.broadcasted_iota(jnp.int32, (tm, 1), 0) % GRID_W
    prev = jnp.where(col == 0, 0.0, pltpu.roll(p, 1, 0))
    nxt = jnp.where(col == GRID_W - 1, 0.0, pltpu.roll(p, tm - 1, 0))
    v = cw_ref[0:1, :] * prev + cw_ref[1:2, :] * p + cw_ref[2:3, :] * nxt
    gate_b = _dot(h, w_ref[:, o2:o3])
    out_b = _dot((gate_b * v).astype(BF16), cwo_ref[...])
    merge_b = _dot(h, w_ref[:, o5:])
    sb_ref[...] = (_sigmoid(merge_b) * out_b).astype(sb_ref.dtype)
    merge_a = _dot(h, w_ref[:, o4:o5])
    siga_ref[...] = _sigmoid(merge_a).astype(siga_ref.dtype)


def _pos_specs(tm, d):
    return [pl.BlockSpec((tm // GRID_W, d // 2), lambda i, j: (i, 0)),
            pl.BlockSpec((GRID_W, d // 2), lambda i, j: (0, 0))]


def _in_proj(x, r_emb, c_emb, mod, w_in_bf, conv_w, conv_w_out_bf, tm, full):
    b, l, d = x.shape
    n_tiles = l // tm
    grid = (n_tiles, b)
    tok = lambda w: pl.BlockSpec((None, tm, w), lambda i, j: (j, i, 0))
    chunk_spec = pl.BlockSpec((tm // CHUNK, CHUNK * S5_WIDTH), lambda i, j: (j * n_tiles + i, 0))
    chunk_shape = jax.ShapeDtypeStruct((b * l // CHUNK, CHUNK * S5_WIDTH), BF16)
    in_specs = [tok(d)] + _pos_specs(tm, d) + [pl.BlockSpec((None, 2, d), lambda i, j: (j, 0, 0))]
    args = [x, r_emb, c_emb, mod]
    if full:
        in_specs += [pl.BlockSpec(w_in_bf.shape, lambda i, j: (0, 0)),
                     pl.BlockSpec(conv_w.shape, lambda i, j: (0, 0)),
                     pl.BlockSpec(conv_w_out_bf.shape, lambda i, j: (0, 0))]
        args += [w_in_bf, conv_w, conv_w_out_bf]
        out_specs = [chunk_spec, tok(d), tok(d)]
        out_shape = [chunk_shape,
                     jax.ShapeDtypeStruct((b, l, d), BF16),
                     jax.ShapeDtypeStruct((b, l, d), BF16)]
    else:
        in_specs += [pl.BlockSpec((d, S5_WIDTH), lambda i, j: (0, 0))]
        args += [w_in_bf]
        out_specs = [chunk_spec]
        out_shape = [chunk_shape]
    return pl.pallas_call(
        functools.partial(_in_proj_kernel, full=full),
        grid=grid, in_specs=in_specs, out_specs=out_specs, out_shape=out_shape,
        scratch_shapes=[pltpu.VMEM((S5_WIDTH // LANES, tm, LANES), F32)],
        compiler_params=pltpu.CompilerParams(
            dimension_semantics=("arbitrary", "arbitrary"), vmem_limit_bytes=VMEM_LIMIT),
        name="in_proj" if full else "in_proj_ctx",
    )(*args)


def _s5_tables(log_dt, a_re, a_im, b_re, b_im, c_re, c_im):
    f32 = F32
    dt = jnp.exp(log_dt.astype(f32))[..., None]
    a_re = a_re.astype(f32)
    a_im = a_im.astype(f32)
    mag = jnp.exp(dt * a_re)
    ab_re = mag * jnp.cos(dt * a_im)
    ab_im = mag * jnp.sin(dt * a_im)
    den = a_re * a_re + a_im * a_im
    x_re = ab_re - 1.0
    f_re = (x_re * a_re + ab_im * a_im) / den
    f_im = (ab_im * a_re - x_re * a_im) / den
    b_re = b_re.astype(f32)
    b_im = b_im.astype(f32)
    bb_re = f_re[..., None] * b_re - f_im[..., None] * b_im
    bb_im = f_re[..., None] * b_im + f_im[..., None] * b_re
    k = jnp.arange(CHUNK + 1, dtype=f32)[None, :, None, None]
    pmag = jnp.exp(k * (dt * a_re)[:, None])
    p_re = pmag * jnp.cos(k * (dt * a_im)[:, None])
    p_im = pmag * jnp.sin(k * (dt * a_im)[:, None])
    pb_re = p_re[..., None] * bb_re[:, None] - p_im[..., None] * bb_im[:, None]
    pb_im = p_re[..., None] * bb_im[:, None] + p_im[..., None] * bb_re[:, None]
    c_re = c_re.astype(f32)[:, None]
    c_im = c_im.astype(f32)[:, None]
    cp_re = c_re * p_re[:, :, :, None, :] - c_im * p_im[:, :, :, None, :]
    cp_im = -(c_re * p_im[:, :, :, None, :] + c_im * p_re[:, :, :, None, :])
    return dict(p_re=p_re, p_im=p_im, pb_re=pb_re, pb_im=pb_im, cp_re=cp_re, cp_im=cp_im,
                bb_re=bb_re, bb_im=bb_im)


def _lag_kernels(t):
    g, n, c = S5_GROUPS, S5_STATE, S5_GROUP_CH
    k = CHUNK + 1
    lhs = jnp.concatenate([t['cp_re'], t['cp_im']], -1)
    lhs = lhs.transpose(0, 2, 1, 3, 4).reshape(2 * g, k * c, 2 * n)
    rhs = jnp.concatenate([t['bb_re'], t['bb_im']], -2).reshape(2 * g, 2 * n, c)
    out = jnp.einsum('bmn,bnc->bmc', lhs, rhs, precision=HI)
    out = out.reshape(2, g, k, c, c).transpose(0, 2, 1, 4, 3)
    return out[0], out[1]


def _s5_operators(t, s5_d):
    q = CHUNK
    g, n, c = S5_GROUPS, S5_STATE, S5_GROUP_CH
    kern_f, kern_b = _lag_kernels(t)
    k0 = kern_f[0] + kern_b[0] + s5_d.astype(F32)[:, :, None] * jnp.eye(c, dtype=F32)[None]
    kc = jnp.concatenate([kern_b[1:q][::-1], k0[None], kern_f[1:q]], 0)
    kct = kc.transpose(1, 2, 0, 3)
    m_intra = jnp.stack([kct[:, :, q - 1 - i:2 * q - 1 - i, :] for i in range(q)], 1)
    m_intra = m_intra.reshape(g, q * c, q * c)
    w_st = jnp.stack([t['pb_re'][0, :q][::-1], t['pb_im'][0, :q][::-1],
                      t['pb_re'][1, :q], t['pb_im'][1, :q]], 0)
    w_st = w_st.transpose(2, 1, 4, 0, 3).reshape(g, q * c, 4, n)
    w_out = jnp.stack([t['cp_re'][0, 1:], t['cp_im'][0, 1:],
                       t['cp_re'][1, 1:][::-1], t['cp_im'][1, 1:][::-1]], 0)
    w_out = w_out.transpose(2, 0, 4, 1, 3).reshape(g, 4, n, q * c)
    np_ = N_PAIRS
    w_st = w_st.astype(BF16).reshape(np_, 2, q * c, 4, n)
    ws_pair = jnp.concatenate([jnp.pad(w_st[:, 0], ((0, 0), (0, 0), (0, 0), (0, n))),
                               jnp.pad(w_st[:, 1], ((0, 0), (0, 0), (0, 0), (n, 0)))], 1)
    ws_pair = ws_pair.reshape(np_, PAIR_W, 4 * 2 * n)
    w_out = w_out.astype(BF16).reshape(np_, 2, 4, n, q * c)
    wo_pair = jnp.stack([jnp.pad(w_out[:, 0], ((0, 0), (0, 0), (0, 0), (0, q * c))),
                         jnp.pad(w_out[:, 1], ((0, 0), (0, 0), (0, 0), (q * c, 0)))], 2)
    wo_pair = wo_pair.reshape(np_, 4 * 2 * n, PAIR_W)
    tab = _chunk_power_table(t).reshape(2 * TAB_ROWS, np_, 2 * n).transpose(1, 0, 2)
    return m_intra.astype(BF16), ws_pair, wo_pair, tab


def _chunk_power_table(t):
    def cmul(x, y):
        return x[0] * y[0] - x[1] * y[1], x[0] * y[1] + x[1] * y[0]
    p1 = (t['p_re'][:, CHUNK], t['p_im'][:, CHUNK])
    p2 = cmul(p1, p1)
    p4 = cmul(p2, p2)
    p8 = cmul(p4, p4)
    pr = [(jnp.ones_like(p1[0]), jnp.zeros_like(p1[0]))]
    for _ in range(SUBLANES - 1):
        pr.append(cmul(pr[-1], p1))
    pr_re = jnp.stack([p[0] for p in pr], 0)
    pr_im = jnp.stack([p[1] for p in pr], 0)
    pw = jnp.stack([p1[0], p1[1], p2[0], p2[1], p4[0], p4[1], p8[0], p8[1]], 0)
    return jnp.concatenate([pr_re[:, 0], pr_im[:, 0], pw[:, 0],
                            pr_re[::-1, 1], pr_im[::-1, 1], pw[:, 1]], 0)


def _s5_scan_kernel(uc_ref, *rest, batch, emit_y):
    if emit_y:
        mi_ref, ws_ref, wo_ref, tab_ref, s0_ref, y_ref, fin_ref, s_scr, in_scr = rest
    else:
        ws_ref, tab_ref, s0_ref, fin_ref, s_scr = rest
    rows = uc_ref.shape[0]
    chunks = rows // batch
    n_tiles = chunks // SUBLANES
    u = uc_ref[...]
    s_scr[...] = _dot(u, ws_ref[...])
    row = lax.broadcasted_iota(jnp.int32, (SUBLANES, LANES), 0)

    def tile_scan(r0, backward, c_re, c_im):
        base = TAB_ROWS if backward else 0
        col = 2 * LANES if backward else 0
        rs = pl.ds(r0, SUBLANES)

        def shift(z, k):
            if backward:
                return jnp.where(row < SUBLANES - k, pltpu.roll(z, SUBLANES - k, 0), 0.0)
            return jnp.where(row >= k, pltpu.roll(z, k, 0), 0.0)

        z_re = s_scr[rs, col:col + LANES]
        z_im = s_scr[rs, col + LANES:col + 2 * LANES]
        for k, t in ((1, TAB_POW), (2, TAB_POW + 2), (4, TAB_POW + 4)):
            a_re = tab_ref[base + t:base + t + 1, :]
            a_im = tab_ref[base + t + 1:base + t + 2, :]
            sh_re = shift(z_re, k)
            sh_im = shift(z_im, k)
            z_re, z_im = z_re + (a_re * sh_re - a_im * sh_im), z_im + (a_re * sh_im + a_im * sh_re)
        pr_re = tab_ref[base:base + SUBLANES, :]
        pr_im = tab_ref[base + SUBLANES:base + 2 * SUBLANES, :]
        if emit_y:
            in_scr[rs, col:col + LANES] = pr_re * c_re - pr_im * c_im + shift(z_re, 1)
            in_scr[rs, col + LANES:col + 2 * LANES] = pr_re * c_im + pr_im * c_re + shift(z_im, 1)
        last = 0 if backward else SUBLANES - 1
        l_re = jnp.broadcast_to(z_re[last:last + 1, :], (SUBLANES, LANES))
        l_im = jnp.broadcast_to(z_im[last:last + 1, :], (SUBLANES, LANES))
        p8_re = tab_ref[base + TAB_POW + 6:base + TAB_POW + 7, :]
        p8_im = tab_ref[base + TAB_POW + 7:base + TAB_POW + 8, :]
        return p8_re * c_re - p8_im * c_im + l_re, p8_re * c_im + p8_im * c_re + l_im

    def body(m, carry):
        out = []
        for b in range(batch):
            cf_re, cf_im, cb_re, cb_im = carry[4 * b:4 * b + 4]
            rf = pl.multiple_of(b * chunks + m * SUBLANES, SUBLANES)
            rb = pl.multiple_of(b * chunks + (n_tiles - 1 - m) * SUBLANES, SUBLANES)
            out += list(tile_scan(rf, False, cf_re, cf_im))
            out += list(tile_scan(rb, True, cb_re, cb_im))
        return tuple(out)

    init = tuple(jnp.broadcast_to(s0_ref[t, b:b + 1, :], (SUBLANES, LANES))
                 for b in range(batch) for t in range(4))
    fin = lax.fori_loop(0, n_tiles, body, init, unroll=min(4, n_tiles))
    fin_ref[...] = jnp.zeros(fin_ref.shape, F32)
    for b in range(batch):
        for t in range(4):
            fin_ref[t, b:b + 1, :] = fin[4 * b + t][0:1, :]
    if emit_y:
        y_intra = jnp.concatenate(
            [_dot(u[:, gl * GROUP_W:(gl + 1) * GROUP_W], mi_ref[gl]) for gl in range(2)], axis=-1)
        y = y_intra + _dot(in_scr[...].astype(BF16), wo_ref[...])
        y_ref[...] = y.astype(y_ref.dtype)


def _s5_scan(uc, mi, ws, wo, tab, s0, batch, emit_y=True):
    rows = uc.shape[0]
    pair = lambda *shape: pl.BlockSpec((None,) + shape, lambda p: (p,) + (0,) * len(shape))
    uc_spec = pl.BlockSpec((rows, PAIR_W), lambda p: (0, p))
    state_spec = pair(4, SUBLANES, LANES)
    state_shape = jax.ShapeDtypeStruct((N_PAIRS, 4, SUBLANES, LANES), F32)
    scratch = [pltpu.VMEM((rows, PAIR_W), F32)]
    if emit_y:
        in_specs = [uc_spec, pl.BlockSpec((2, GROUP_W, GROUP_W), lambda p: (p, 0, 0)),
                    pair(PAIR_W, PAIR_W), pair(PAIR_W, PAIR_W), pair(2 * TAB_ROWS, LANES), state_spec]
        args = (uc, mi, ws, wo, tab, s0)
        out_specs = [uc_spec, state_spec]
        out_shape = [jax.ShapeDtypeStruct((rows, N_PAIRS * PAIR_W), BF16), state_shape]
        scratch = scratch * 2
    else:
        in_specs = [uc_spec, pair(PAIR_W, PAIR_W), pair(2 * TAB_ROWS, LANES), state_spec]
        args = (uc, ws, tab, s0)
        out_specs = state_spec
        out_shape = state_shape
    return pl.pallas_call(
        functools.partial(_s5_scan_kernel, batch=batch, emit_y=emit_y),
        grid=(N_PAIRS,), in_specs=in_specs, out_specs=out_specs, out_shape=out_shape,
        scratch_shapes=scratch,
        compiler_params=pltpu.CompilerParams(
            dimension_semantics=("arbitrary",), vmem_limit_bytes=VMEM_LIMIT),
        name="s5_scan" if emit_y else "s5_scan_ctx",
    )(*args)


def _mix_out_kernel(y_ref, siga_ref, sb_ref, x_ref, remb_ref, cemb_ref, mod_ref, wv_ref, wg_ref, wo_ref,
                    ln1_ref, wr_ref, br_ref, ut_ref, lt_ref,
                    x1_ref, xs_ref, route_ref, len_ref, y_scr):
    _from_chunk_tile(y_ref, y_scr)
    y = jnp.concatenate([y_scr[v] for v in range(S5_WIDTH // LANES)], axis=-1)
    half_y = 0.5 * y
    ya = (half_y + half_y * jnp.tanh(y * (GELU_K + (GELU_K * GELU_A) * (y * y)))).astype(BF16)
    out_a = _dot(ya, wv_ref[...]) * _sigmoid(_dot(ya, wg_ref[...]))
    merged = siga_ref[...].astype(F32) * out_a + sb_ref[...].astype(F32)
    mix = _dot(merged.astype(BF16), wo_ref[...])
    xp = _with_positions(x_ref, remb_ref, cemb_ref)
    x1 = _ln(ALPHA * xp + mod_ref[0:1, :] * mix) * ln1_ref[0:1, :] + ln1_ref[1:2, :]
    x1_ref[...] = x1
    h_hi = (_ln(x1) * mod_ref[2:3, :] + mod_ref[1:2, :]).astype(BF16)
    _route_and_sort(h_hi, wr_ref, br_ref, ut_ref, lt_ref, xs_ref, route_ref, len_ref)


def _route_and_sort(h_hi, wr_ref, br_ref, ut_ref, lt_ref, xs_ref, route_ref, len_ref):
    tm = h_hi.shape[0]
    f32 = F32
    nt = (((1,), (1,)), ((), ()))
    lg = lax.dot_general(wr_ref[...], h_hi, nt, preferred_element_type=f32) + br_ref[...]
    rowi = lax.broadcasted_iota(jnp.int32, (LOGIT_ROWS, tm), 0).astype(f32)
    neg = jnp.float32(-jnp.inf)
    big = jnp.float32(LOGIT_ROWS)
    gl = jnp.where(rowi < N_EXPERT_GROUPS, lg, neg)
    gmax = jnp.max(gl, axis=0, keepdims=True)
    g_idx = jnp.min(jnp.where(gl == gmax, rowi, big), axis=0, keepdims=True)
    p_group = 1.0 / jnp.sum(jnp.exp(gl - gmax), axis=0, keepdims=True)
    e_lo = N_EXPERT_GROUPS + g_idx * EXPERTS_PER_GROUP
    el = jnp.where((rowi >= e_lo) & (rowi < e_lo + EXPERTS_PER_GROUP), lg, neg)
    m1 = jnp.max(el, axis=0, keepdims=True)
    i1 = jnp.min(jnp.where(el == m1, rowi, big), axis=0, keepdims=True)
    el2 = jnp.where(rowi == i1, neg, el)
    m2 = jnp.max(el2, axis=0, keepdims=True)
    i2 = jnp.min(jnp.where(el2 == m2, rowi, big), axis=0, keepdims=True)
    r = jnp.exp(m2 - m1)
    w1 = p_group / (1.0 + r)
    w2 = p_group * r / (1.0 + r)
    e1 = i1 - N_EXPERT_GROUPS
    e2 = i2 - N_EXPERT_GROUPS
    rowa = lax.broadcasted_iota(jnp.int32, (ASSIGN_ROWS, tm), 0).astype(f32)
    a12 = jnp.where(rowa == e1, 1.0, 0.0) + jnp.where(rowa == e2 + N_EXPERTS, 1.0, 0.0)
    rank = _dot(a12.astype(BF16), ut_ref[...])
    cnt = jnp.broadcast_to(jnp.sum(a12, axis=1, keepdims=True), (ASSIGN_ROWS, LANES))
    row = lax.broadcasted_iota(jnp.int32, (ASSIGN_ROWS, LANES), 0)
    tot = cnt + pltpu.roll(cnt, N_EXPERTS, 0)
    run = jnp.where(row < N_EXPERTS, jnp.floor((tot + (PIECE - 1)) * (1.0 / PIECE)), 0.0)
    off = PIECE * _dot(lt_ref[...], run.astype(BF16))
    base = jnp.where(row < N_EXPERTS, off, pltpu.roll(off + cnt, N_EXPERTS, 0))
    posmat = a12 * (rank + base[:, 0:1])
    pos1 = jnp.sum(posmat[0:N_EXPERTS, :], axis=0, keepdims=True)
    pos2 = jnp.sum(posmat[N_EXPERTS:2 * N_EXPERTS, :], axis=0, keepdims=True)
    ri = lax.broadcasted_iota(jnp.int32, (xs_ref.shape[0], tm), 0)
    perm = jnp.where((ri == pos1.astype(jnp.int32)) | (ri == pos2.astype(jnp.int32)), 1.0, 0.0).astype(BF16)
    xs_ref[...] = _dot(perm, h_hi).astype(xs_ref.dtype)
    rowr = lax.broadcasted_iota(jnp.int32, (SUBLANES, tm), 0)
    rec = jnp.zeros((SUBLANES, tm), f32)
    for col, val in ((REC_W1, w1), (REC_W2, w2), (REC_POS1, pos1), (REC_POS2, pos2)):
        rec = jnp.where(rowr == col, val, rec)
    rec = jnp.concatenate([rec, jnp.zeros((ROUTE_LANES - SUBLANES, tm), f32)], axis=0)
    route_ref[...] = rec.T
    len_ref[...] = run


def _local_rows(tm):
    return -(-(TOP_K * tm + N_EXPERTS * (PIECE - 1)) // PIECE) * PIECE


def _mix_out(y, siga, sb, x, r_emb, c_emb, mod, wv, wg, wo, ln1, wr, br, tm):
    b, l, d = x.shape
    n_tiles = l // tm
    xs_rows = _local_rows(tm)
    tok = lambda w: pl.BlockSpec((None, tm, w), lambda i, j: (j, i, 0))
    whole = lambda a: pl.BlockSpec(a.shape, lambda i, j: (0,) * a.ndim)
    ut = (jnp.arange(tm)[:, None] < jnp.arange(tm)[None, :]).astype(BF16)
    lt = (jnp.arange(ASSIGN_ROWS)[None, :] < jnp.arange(ASSIGN_ROWS)[:, None]).astype(BF16)
    return pl.pallas_call(
        _mix_out_kernel,
        grid=(n_tiles, b),
        in_specs=[pl.BlockSpec((tm // CHUNK, CHUNK * S5_WIDTH), lambda i, j: (j * n_tiles + i, 0)),
                  tok(d), tok(d), tok(d)] + _pos_specs(tm, d) + [
                  pl.BlockSpec((None, 4, d), lambda i, j: (j, 0, 0)),
                  whole(wv), whole(wg), whole(wo), whole(ln1), whole(wr), whole(br),
                  whole(ut), whole(lt)],
        out_specs=[tok(d),
                   pl.BlockSpec((xs_rows, d), lambda i, j: (j * n_tiles + i, 0)),
                   tok(ROUTE_LANES),
                   pl.BlockSpec((None, ASSIGN_ROWS, LANES), lambda i, j: (j * n_tiles + i, 0, 0))],
        out_shape=[jax.ShapeDtypeStruct((b, l, d), F32),
                   jax.ShapeDtypeStruct((b * n_tiles * xs_rows, d), BF16),
                   jax.ShapeDtypeStruct((b, l, ROUTE_LANES), F32),
                   jax.ShapeDtypeStruct((b * n_tiles, ASSIGN_ROWS, LANES), F32)],
        scratch_shapes=[pltpu.VMEM((S5_WIDTH // LANES, tm, LANES), F32)],
        compiler_params=pltpu.CompilerParams(
            dimension_semantics=("arbitrary", "arbitrary"), vmem_limit_bytes=VMEM_LIMIT),
        name="mix_out",
    )(y, siga, sb, x, r_emb, c_emb, mod, wv, wg, wo, ln1, wr, br, ut, lt)


def _piece_copy(src_hbm, src_row, dst, piece, sem):
    return pltpu.make_async_copy(src_hbm.at[pl.ds(pl.multiple_of(src_row, PIECE), PIECE), :],
                                 dst.at[pl.ds(pl.multiple_of(piece * PIECE, PIECE), PIECE), :], sem)


def _issue_pieces(src_hbm, table_ref, first, n_pieces, dst, sem):
    for p in range(n_pieces):
        _piece_copy(src_hbm, table_ref[first + p], dst, p, sem).start(priority=p % 2)


def _wait_pieces(src_hbm, dst, sem):
    pltpu.make_async_copy(src_hbm.at[pl.ds(0, dst.shape[0]), :], dst, sem).wait()


def _experts_kernel(be_ref, piece_ref, nused_ref, xs_hbm, wg_ref, wu_ref, wd_ref, ys_ref,
                    xs_buf0, xs_buf1, wg_bf, wu_bf, wd_bf, sem):
    i = pl.program_id(0)
    n_used = nused_ref[0]
    per_block = ROW_BLOCK // PIECE

    @pl.when(i == 0)
    def _():
        _issue_pieces(xs_hbm, piece_ref, 0, per_block, xs_buf0, sem.at[0])

    def block(cur, cur_sem, oth, oth_sem):
        @pl.when((i == 0) | (be_ref[i] != be_ref[jnp.maximum(i - 1, 0)]))
        def _():
            wg_bf[...] = wg_ref[...].astype(BF16)
            wu_bf[...] = wu_ref[...].astype(BF16)
            wd_bf[...] = wd_ref[...].astype(BF16)

        _wait_pieces(xs_hbm, cur, cur_sem)
        nxt = jnp.minimum(i + 1, n_used - 1)
        _issue_pieces(xs_hbm, piece_ref, nxt * per_block, per_block, oth, oth_sem)

        xb = cur[...]
        gate = _dot(xb, wg_bf[...])
        up = _dot(xb, wu_bf[...])
        hid = (gate * _sigmoid(gate) * up).astype(BF16)
        ys_ref[...] = _dot(hid, wd_bf[...]).astype(ys_ref.dtype)

        @pl.when(i == n_used - 1)
        def _():
            _wait_pieces(xs_hbm, oth, oth_sem)

    for s, (cur, oth) in enumerate(((xs_buf0, xs_buf1), (xs_buf1, xs_buf0))):
        @pl.when((i < n_used) & (i % 2 == s))
        def _(s=s, cur=cur, oth=oth):
            block(cur, sem.at[s], oth, sem.at[1 - s])

    @pl.when(i >= n_used)
    def _():
        ys_ref[...] = jnp.zeros(ys_ref.shape, ys_ref.dtype)


def _experts(block_e, piece_src, n_used, xs, wg, wu, wd, n_blocks):
    d = xs.shape[1]
    by_expert = lambda i, be, ps, nu: (0, be[i], 0, 0)
    grid_spec = pltpu.PrefetchScalarGridSpec(
        num_scalar_prefetch=3,
        grid=(n_blocks,),
        in_specs=[pl.BlockSpec(memory_space=pl.ANY),
                  pl.BlockSpec((None, None, d, EXPERT_FF), by_expert),
                  pl.BlockSpec((None, None, d, EXPERT_FF), by_expert),
                  pl.BlockSpec((None, None, EXPERT_FF, d), by_expert)],
        out_specs=pl.BlockSpec((ROW_BLOCK, d), lambda i, be, ps, nu: (i, 0)),
        scratch_shapes=[pltpu.VMEM((ROW_BLOCK, d), BF16), pltpu.VMEM((ROW_BLOCK, d), BF16),
                        pltpu.VMEM((d, EXPERT_FF), BF16), pltpu.VMEM((d, EXPERT_FF), BF16),
                        pltpu.VMEM((EXPERT_FF, d), BF16), pltpu.SemaphoreType.DMA((2,))],
    )
    return pl.pallas_call(
        _experts_kernel,
        grid_spec=grid_spec,
        out_shape=jax.ShapeDtypeStruct((n_blocks * ROW_BLOCK, d), BF16),
        compiler_params=pltpu.CompilerParams(
            dimension_semantics=("arbitrary",), vmem_limit_bytes=VMEM_LIMIT),
        name="experts",
    )(block_e, piece_src, n_used, xs, wg, wu, wd)


def _combine_kernel(piece_ref, ys_hbm, x1_ref, route_ref, mod_ref, ln2_ref, o_ref, buf, sem):
    i = pl.program_id(0)
    n = pl.num_programs(0)
    slot = i % 2
    rows = buf.shape[1]
    per_tile = rows // PIECE

    def issue(tile, s):
        def body(h, _):
            for prio in range(2):
                p = 2 * h + prio
                _piece_copy(ys_hbm, piece_ref[tile * per_tile + p], buf.at[s], p, sem.at[s]).start(priority=prio)
            return 0
        lax.fori_loop(0, per_tile // 2, body, 0)

    @pl.when(i == 0)
    def _():
        issue(0, 0)

    tm = x1_ref.shape[0]
    lane = lax.broadcasted_iota(jnp.int32, (tm, rows), 1)
    rec = lambda col: route_ref[:, col:col + 1]
    sel = jnp.where(lane == rec(REC_POS1).astype(jnp.int32), rec(REC_W1),
                    jnp.where(lane == rec(REC_POS2).astype(jnp.int32), rec(REC_W2), 0.0)).astype(BF16)

    _wait_pieces(ys_hbm, buf.at[slot], sem.at[slot])

    @pl.when(i + 1 < n)
    def _():
        issue(i + 1, 1 - slot)

    moe = _dot(sel, buf[slot])
    z = ALPHA * x1_ref[...] + mod_ref[0:1, :] * moe
    o_ref[...] = _ln(z) * ln2_ref[0:1, :] + ln2_ref[1:2, :]


def _combine(piece_glob, ys, x1, route, mod, ln2, tm, tiles_per_batch):
    t, d = x1.shape
    assert (_local_rows(tm) // PIECE) % 2 == 0
    grid_spec = pltpu.PrefetchScalarGridSpec(
        num_scalar_prefetch=1,
        grid=(t // tm,),
        in_specs=[pl.BlockSpec(memory_space=pl.ANY),
                  pl.BlockSpec((tm, d), lambda i, pg: (i, 0)),
                  pl.BlockSpec((tm, ROUTE_LANES), lambda i, pg: (i, 0)),
                  pl.BlockSpec((None, SUBLANES, d), lambda i, pg: (i // tiles_per_batch, 0, 0)),
                  pl.BlockSpec((2, d), lambda i, pg: (0, 0))],
        out_specs=pl.BlockSpec((tm, d), lambda i, pg: (i, 0)),
        scratch_shapes=[pltpu.VMEM((2, _local_rows(tm), d), BF16), pltpu.SemaphoreType.DMA((2,))],
    )
    return pl.pallas_call(
        _combine_kernel,
        grid_spec=grid_spec,
        out_shape=jax.ShapeDtypeStruct((t, d), F32),
        compiler_params=pltpu.CompilerParams(
            dimension_semantics=("arbitrary",), vmem_limit_bytes=VMEM_LIMIT),
        name="combine",
    )(piece_glob, ys, x1, route, mod, ln2)


def _sincos_2d(rows, cols, dim):
    q = dim // 4
    omega = 1.0 / (POS_BASE ** (jnp.arange(q, dtype=F32) / q))
    r = jnp.arange(rows, dtype=F32)[:, None] * omega
    cl = jnp.arange(cols, dtype=F32)[:, None] * omega
    r_emb = jnp.concatenate([jnp.sin(r), jnp.cos(r)], -1)
    c_emb = jnp.concatenate([jnp.sin(cl), jnp.cos(cl)], -1)
    return r_emb, c_emb


def _routing_tables(run_pieces, xs_rows, n_blocks):
    i32 = jnp.int32
    n_tiles = run_pieces.shape[0]
    ppb = ROW_BLOCK // PIECE
    loc_start = jnp.cumsum(run_pieces, axis=1) - run_pieces
    seg_tot = jnp.sum(run_pieces, axis=0)
    seg_pad = (seg_tot + ppb - 1) // ppb * ppb
    seg_end = jnp.cumsum(seg_pad)
    seg_start = seg_end - seg_pad
    run_t = run_pieces.T
    glob_start = seg_start[:, None] + jnp.cumsum(run_t, axis=1) - run_t
    n_used = (seg_end[-1] // ppb).astype(i32)
    blk = jnp.minimum(jnp.arange(n_blocks, dtype=i32), n_used - 1)
    block_e = jnp.minimum(jnp.sum((seg_end[None, :] <= (blk * ppb)[:, None]).astype(i32), axis=1),
                          N_EXPERTS - 1).astype(i32)
    lpt = xs_rows // PIECE
    src0 = jnp.arange(n_tiles, dtype=i32)[None, :] * lpt + loc_start.T
    is_e = block_e[:, None] == jnp.arange(N_EXPERTS, dtype=i32)[None, :]
    of_block = lambda tbl: jnp.sum(jnp.where(is_e[:, :, None], tbl[None], 0), axis=1)[:, None, :]
    p = jnp.arange(n_blocks * ppb, dtype=i32).reshape(n_blocks, ppb, 1)
    within = p - of_block(glob_start)
    hit = (within >= 0) & (within < of_block(run_t))
    piece_src = jnp.sum(jnp.where(hit, (of_block(src0) + within) * PIECE, 0), axis=2).astype(i32).reshape(-1)
    s = jnp.arange(lpt, dtype=i32)
    loc_within = s[None, :, None] - loc_start[:, None, :]
    hit = (loc_within >= 0) & (loc_within < run_pieces[:, None, :])
    piece_glob = jnp.sum(jnp.where(hit, (glob_start.T[:, None, :] + loc_within) * PIECE, 0), axis=2)
    return block_e, piece_src, piece_glob.astype(i32).reshape(-1), n_used.reshape(1)


def kernel(x, c, ctx, c_ctx, w_ada, b_ada, w_in, s5_log_dt_f, s5_a_re_f, s5_a_im_f, s5_b_re_f, s5_b_im_f, s5_c_re_f, s5_c_im_f, s5_log_dt_b, s5_a_re_b, s5_a_im_b, s5_b_re_b, s5_b_im_b, s5_c_re_b, s5_c_im_b, s5_d, s5_w_glu_val, s5_w_glu_gate, conv_w, conv_w_out, w_o, ln1_g, ln1_b, router_w_group, router_b_group, router_w_expert, router_b_expert, exp_w_gate, exp_w_up, exp_w_down, ln2_g, ln2_b):
    b, l, d = x.shape
    lc = ctx.shape[1]
    assert d == D_MODEL and b < SUBLANES and w_ada.shape[0] == DEPTH
    assert l % (SUBLANES * CHUNK) == 0 and lc % (SUBLANES * CHUNK) == 0 and l % GRID_W == 0
    t = b * l
    tm = min(TOKEN_TILE, l)
    tmc = min(TOKEN_TILE, lc)

    cc = jnp.concatenate([c, c_ctx[None, :], jnp.zeros((SUBLANES - b - 1, d), F32)], 0)
    mods = _mods(cc, w_ada[0], b_ada[0])
    sh1, sc1, g1, sh2, sc2, g2 = jnp.split(mods, 6, axis=-1)
    mod_a = jnp.stack([sh1[:b], 1.0 + sc1[:b]], 1)
    mod_ctx = jnp.broadcast_to(jnp.stack([sh1[b], 1.0 + sc1[b]], 0)[None], (b, 2, d))
    mod_c = jnp.stack([g1[:b], sh2[:b], 1.0 + sc2[:b], jnp.zeros((b, d), F32)], 1)
    mod_f = jnp.concatenate([g2[:b, None, :], jnp.zeros((b, SUBLANES - 1, d), F32)], 1)

    w_in_bf = w_in[0].astype(BF16)
    both = lambda fwd, bwd: jnp.concatenate([fwd, bwd], 0)
    s5_tab = _s5_tables(both(s5_log_dt_f, s5_log_dt_b), both(s5_a_re_f, s5_a_re_b), both(s5_a_im_f, s5_a_im_b),
                        both(s5_b_re_f, s5_b_re_b), both(s5_b_im_f, s5_b_im_b),
                        both(s5_c_re_f, s5_c_re_b), both(s5_c_im_f, s5_c_im_b))
    mi, ws, wo_s5, tab = _s5_operators(s5_tab, s5_d[0])

    (uc_ctx,) = _in_proj(ctx, jnp.zeros((lc // GRID_W, d // 2), F32), jnp.zeros((GRID_W, d // 2), F32),
                         mod_ctx, w_in_bf, None, None, tmc, False)
    zero_state = jnp.zeros((N_PAIRS, 4, SUBLANES, LANES), F32)
    s0 = _s5_scan(uc_ctx, None, ws, None, tab, zero_state, b, emit_y=False)

    r_emb, c_emb = _sincos_2d(l // GRID_W, GRID_W, d)
    uc, siga, sb = _in_proj(x, r_emb, c_emb, mod_a, w_in_bf, conv_w[0], conv_w_out[0].astype(BF16),
                            min(2 * TOKEN_TILE, l), True)
    y, _ = _s5_scan(uc, mi, ws, wo_s5, tab, s0, b)

    wr = jnp.concatenate([router_w_group[0], router_w_expert[0],
                          jnp.zeros((d, LOGIT_ROWS - N_EXPERT_GROUPS - N_EXPERTS), F32)], 1).T.astype(BF16)
    br = jnp.concatenate([router_b_group[0], router_b_expert[0],
                          jnp.zeros((LOGIT_ROWS - N_EXPERT_GROUPS - N_EXPERTS,), F32)])[:, None]
    ln1 = jnp.stack([ln1_g[0], ln1_b[0]], 0)
    x1, xs, route, run_len = _mix_out(y, siga, sb, x, r_emb, c_emb, mod_c,
                                      s5_w_glu_val[0].astype(BF16), s5_w_glu_gate[0].astype(BF16),
                                      w_o[0].astype(BF16), ln1, wr, br, tm)

    x1 = x1.reshape(t, d)
    route = route.reshape(t, ROUTE_LANES)
    n_tiles = t // tm
    xs_rows = _local_rows(tm)
    run_pieces = run_len[:, :N_EXPERTS, 0].astype(jnp.int32)
    max_rows = t * TOP_K + n_tiles * N_EXPERTS * (PIECE - 1) + N_EXPERTS * (ROW_BLOCK - 1)
    n_blocks = -(-max_rows // ROW_BLOCK)
    block_e, piece_src, piece_glob, n_used = _routing_tables(run_pieces, xs_rows, n_blocks)
    ys = _experts(block_e, piece_src, n_used, xs, exp_w_gate, exp_w_up, exp_w_down, n_blocks)
    ln2 = jnp.stack([ln2_g[0], ln2_b[0]], 0)
    out = _combine(piece_glob, ys, x1, route, mod_f, ln2, tm, l // tm)
    return out.reshape(b, l, d)
```

```python
import functools
import math

import jax
import jax.numpy as jnp
from jax import lax
from jax.experimental import pallas as pl
from jax.experimental.pallas import tpu as pltpu

F32 = jnp.float32
BF16 = jnp.bfloat16
HI = lax.Precision.HIGHEST

D_MODEL = 1024
GRID_W = 64
S5_WIDTH = 512
S5_GROUP_CH = 16
S5_GROUPS = S5_WIDTH // S5_GROUP_CH
S5_STATE = 64
CONV_WIDTH = 512
N_EXPERT_GROUPS = 4
EXPERTS_PER_GROUP = 8
N_EXPERTS = N_EXPERT_GROUPS * EXPERTS_PER_GROUP
EXPERT_FF = 512
TOP_K = 2
DEPTH = 1
ALPHA = (2.0 * DEPTH) ** 0.25
LN_EPS = 1e-6
POS_BASE = 10000.0
GELU_K = math.sqrt(2.0 / math.pi)
GELU_A = 0.044715

LANES = 128
SUBLANES = 8
CHUNK = 16
GROUP_W = CHUNK * S5_GROUP_CH
PAIR_W = 2 * GROUP_W
N_PAIRS = S5_GROUPS // 2
TOK_PER_VREG = LANES // S5_GROUP_CH
TAB_ROWS = 24
TAB_POW = 2 * SUBLANES
MODS_COLS = 3072
ROUTE_LANES = 128
LOGIT_ROWS = 48
ASSIGN_ROWS = TOP_K * N_EXPERTS
REC_W1, REC_W2, REC_POS1, REC_POS2 = range(4)
TOKEN_TILE = 512
ROW_BLOCK = 512
PIECE = 16
VMEM_LIMIT = 56 * 1024 * 1024


def _ln(x):
    mu = jnp.mean(x, axis=-1, keepdims=True)
    xc = x - mu
    var = jnp.mean(xc * xc, axis=-1, keepdims=True)
    return xc * lax.rsqrt(var + LN_EPS)


def _sigmoid(x):
    return 0.5 * (jnp.tanh(0.5 * x) + 1.0)


def _dot(a, b):
    return jnp.dot(a, b, preferred_element_type=F32)


def _mods_kernel(c_ref, w_ref, b_ref, o_ref):
    c = c_ref[...]
    a = c * _sigmoid(c)
    o_ref[...] = jnp.dot(a, w_ref[...], precision=HI, preferred_element_type=F32) + b_ref[...]


def _mods(cc, w_ada, b_ada):
    n = w_ada.shape[1]
    nb = MODS_COLS
    return pl.pallas_call(
        _mods_kernel,
        grid=(n // nb,),
        in_specs=[pl.BlockSpec((SUBLANES, D_MODEL), lambda i: (0, 0)),
                  pl.BlockSpec((D_MODEL, nb), lambda i: (0, i)),
                  pl.BlockSpec((1, nb), lambda i: (0, i))],
        out_specs=pl.BlockSpec((SUBLANES, nb), lambda i: (0, i)),
        out_shape=jax.ShapeDtypeStruct((SUBLANES, n), F32),
        compiler_params=pltpu.CompilerParams(vmem_limit_bytes=VMEM_LIMIT),
        name="mods",
    )(cc, w_ada, b_ada.reshape(1, n))


def _slot_masks(rows):
    slot = lax.broadcasted_iota(jnp.int32, (rows, LANES), 1) // S5_GROUP_CH
    return [slot == s for s in range(TOK_PER_VREG)]


def _to_chunk_tile(u_scr, uc_ref):
    nch = uc_ref.shape[0]
    masks = _slot_masks(nch)
    for qh in range(CHUNK // TOK_PER_VREG):
        for v in range(S5_WIDTH // LANES):
            src = [u_scr[v, pl.ds(qh * TOK_PER_VREG + s, nch, stride=CHUNK), :] for s in range(TOK_PER_VREG)]
            for i in range(TOK_PER_VREG):
                acc = None
                for s in range(TOK_PER_VREG):
                    shift = ((s - i) * S5_GROUP_CH) % LANES
                    piece = pltpu.roll(src[s], shift, 1) if shift else src[s]
                    acc = piece if acc is None else jnp.where(masks[s], piece, acc)
                lo = (v * TOK_PER_VREG + i) * GROUP_W + qh * LANES
                uc_ref[:, lo:lo + LANES] = acc.astype(uc_ref.dtype)


def _from_chunk_tile(yc_ref, y_scr):
    nch = yc_ref.shape[0]
    masks = _slot_masks(nch)
    for qh in range(CHUNK // TOK_PER_VREG):
        for v in range(S5_WIDTH // LANES):
            src = []
            for i in range(TOK_PER_VREG):
                lo = (v * TOK_PER_VREG + i) * GROUP_W + qh * LANES
                src.append(yc_ref[:, lo:lo + LANES].astype(F32))
            for s in range(TOK_PER_VREG):
                acc = None
                for i in range(TOK_PER_VREG):
                    shift = ((i - s) * S5_GROUP_CH) % LANES
                    piece = pltpu.roll(src[i], shift, 1) if shift else src[i]
                    acc = piece if acc is None else jnp.where(masks[i], piece, acc)
                y_scr[v, pl.ds(qh * TOK_PER_VREG + s, nch, stride=CHUNK), :] = acc


def _with_positions(x_ref, remb_ref, cemb_ref):
    c = cemb_ref[...]
    slabs = []
    for j in range(x_ref.shape[0] // GRID_W):
        r = jnp.broadcast_to(remb_ref[j:j + 1, :], c.shape)
        slabs.append(x_ref[j * GRID_W:(j + 1) * GRID_W, :] + jnp.concatenate([r, c], axis=-1))
    return jnp.concatenate(slabs, axis=0)


def _in_proj_kernel(x_ref, remb_ref, cemb_ref, mod_ref, w_ref, *rest, full):
    if full:
        cw_ref, cwo_ref, uc_ref, siga_ref, sb_ref, u_scr = rest
    else:
        uc_ref, u_scr = rest
    xp = _with_positions(x_ref, remb_ref, cemb_ref)
    h = (_ln(xp) * mod_ref[1:2, :] + mod_ref[0:1, :]).astype(BF16)
    o1 = S5_WIDTH
    o2, o3, o4 = o1 + CONV_WIDTH, o1 + 2 * CONV_WIDTH, o1 + 3 * CONV_WIDTH
    o5 = o4 + D_MODEL
    u = _dot(h, w_ref[:, 0:o1])
    for v in range(S5_WIDTH // LANES):
        u_scr[v] = u[:, v * LANES:(v + 1) * LANES]
    _to_chunk_tile(u_scr, uc_ref)
    if not full:
        return
    z_b = _dot(h, w_ref[:, o1:o2])
    gate_c = _dot(h, w_ref[:, o3:o4])
    p = gate_c * z_b
    tm = p.shape[0]
    col = lax.broadcasted_iota(jnp.int32, (tm, 1), 0) % GRID_W
    prev = jnp.where(col == 0, 0.0, pltpu.roll(p, 1, 0))
    nxt = jnp.where(col == GRID_W - 1, 0.0, pltpu.roll(p, tm - 1, 0))
    v = cw_ref[0:1, :] * prev + cw_ref[1:2, :] * p + cw_ref[2:3, :] * nxt
    gate_b = _dot(h, w_ref[:, o2:o3])
    out_b = _dot((gate_b * v).astype(BF16), cwo_ref[...])
    merge_b = _dot(h, w_ref[:, o5:])
    sb_ref[...] = (_sigmoid(merge_b) * out_b).astype(sb_ref.dtype)
    merge_a = _dot(h, w_ref[:, o4:o5])
    siga_ref[...] = _sigmoid(merge_a).astype(siga_ref.dtype)


def _pos_specs(tm, d):
    return [pl.BlockSpec((tm // GRID_W, d // 2), lambda i, j: (i, 0)),
            pl.BlockSpec((GRID_W, d // 2), lambda i, j: (0, 0))]


def _in_proj(x, r_emb, c_emb, mod, w_in_bf, conv_w, conv_w_out_bf, tm, full):
    b, l, d = x.shape
    n_tiles = l // tm
    grid = (n_tiles, b)
    tok = lambda w: pl.BlockSpec((None, tm, w), lambda i, j: (j, i, 0))
    chunk_spec = pl.BlockSpec((tm // CHUNK, CHUNK * S5_WIDTH), lambda i, j: (j * n_tiles + i, 0))
    chunk_shape = jax.ShapeDtypeStruct((b * l // CHUNK, CHUNK * S5_WIDTH), BF16)
    in_specs = [tok(d)] + _pos_specs(tm, d) + [pl.BlockSpec((None, 2, d), lambda i, j: (j, 0, 0))]
    args = [x, r_emb, c_emb, mod]
    if full:
        in_specs += [pl.BlockSpec(w_in_bf.shape, lambda i, j: (0, 0)),
                     pl.BlockSpec(conv_w.shape, lambda i, j: (0, 0)),
                     pl.BlockSpec(conv_w_out_bf.shape, lambda i, j: (0, 0))]
        args += [w_in_bf, conv_w, conv_w_out_bf]
        out_specs = [chunk_spec, tok(d), tok(d)]
        out_shape = [chunk_shape,
                     jax.ShapeDtypeStruct((b, l, d), BF16),
                     jax.ShapeDtypeStruct((b, l, d), BF16)]
    else:
        in_specs += [pl.BlockSpec((d, S5_WIDTH), lambda i, j: (0, 0))]
        args += [w_in_bf]
        out_specs = [chunk_spec]
        out_shape = [chunk_shape]
    return pl.pallas_call(
        functools.partial(_in_proj_kernel, full=full),
        grid=grid, in_specs=in_specs, out_specs=out_specs, out_shape=out_shape,
        scratch_shapes=[pltpu.VMEM((S5_WIDTH // LANES, tm, LANES), F32)],
        compiler_params=pltpu.CompilerParams(
            dimension_semantics=("arbitrary", "arbitrary"), vmem_limit_bytes=VMEM_LIMIT),
        name="in_proj" if full else "in_proj_ctx",
    )(*args)


def _s5_tables(log_dt, a_re, a_im, b_re, b_im, c_re, c_im):
    f32 = F32
    dt = jnp.exp(log_dt.astype(f32))[..., None]
    a_re = a_re.astype(f32)
    a_im = a_im.astype(f32)
    mag = jnp.exp(dt * a_re)
    ab_re = mag * jnp.cos(dt * a_im)
    ab_im = mag * jnp.sin(dt * a_im)
    den = a_re * a_re + a_im * a_im
    x_re = ab_re - 1.0
    f_re = (x_re * a_re + ab_im * a_im) / den
    f_im = (ab_im * a_re - x_re * a_im) / den
    b_re = b_re.astype(f32)
    b_im = b_im.astype(f32)
    bb_re = f_re[..., None] * b_re - f_im[..., None] * b_im
    bb_im = f_re[..., None] * b_im + f_im[..., None] * b_re
    k = jnp.arange(CHUNK + 1, dtype=f32)[None, :, None, None]
    pmag = jnp.exp(k * (dt * a_re)[:, None])
    p_re = pmag * jnp.cos(k * (dt * a_im)[:, None])
    p_im = pmag * jnp.sin(k * (dt * a_im)[:, None])
    pb_re = p_re[..., None] * bb_re[:, None] - p_im[..., None] * bb_im[:, None]
    pb_im = p_re[..., None] * bb_im[:, None] + p_im[..., None] * bb_re[:, None]
    c_re = c_re.astype(f32)[:, None]
    c_im = c_im.astype(f32)[:, None]
    cp_re = c_re * p_re[:, :, :, None, :] - c_im * p_im[:, :, :, None, :]
    cp_im = -(c_re * p_im[:, :, :, None, :] + c_im * p_re[:, :, :, None, :])
    return dict(p_re=p_re, p_im=p_im, pb_re=pb_re, pb_im=pb_im, cp_re=cp_re, cp_im=cp_im,
                bb_re=bb_re, bb_im=bb_im)


def _lag_kernels(t):
    g, n, c = S5_GROUPS, S5_STATE, S5_GROUP_CH
    k = CHUNK + 1
    lhs = jnp.concatenate([t['cp_re'], t['cp_im']], -1)
    lhs = lhs.transpose(0, 2, 1, 3, 4).reshape(2 * g, k * c, 2 * n)
    rhs = jnp.concatenate([t['bb_re'], t['bb_im']], -2).reshape(2 * g, 2 * n, c)
    out = jnp.einsum('bmn,bnc->bmc', lhs, rhs, precision=HI)
    out = out.reshape(2, g, k, c, c).transpose(0, 2, 1, 4, 3)
    return out[0], out[1]


def _s5_operators(t, s5_d):
    q = CHUNK
    g, n, c = S5_GROUPS, S5_STATE, S5_GROUP_CH
    kern_f, kern_b = _lag_kernels(t)
    k0 = kern_f[0] + kern_b[0] + s5_d.astype(F32)[:, :, None] * jnp.eye(c, dtype=F32)[None]
    kc = jnp.concatenate([kern_b[1:q][::-1], k0[None], kern_f[1:q]], 0)
    kct = kc.transpose(1, 2, 0, 3)
    m_intra = jnp.stack([kct[:, :, q - 1 - i:2 * q - 1 - i, :] for i in range(q)], 1)
    m_intra = m_intra.reshape(g, q * c, q * c)
    w_st = jnp.stack([t['pb_re'][0, :q][::-1], t['pb_im'][0, :q][::-1],
                      t['pb_re'][1, :q], t['pb_im'][1, :q]], 0)
    w_st = w_st.transpose(2, 1, 4, 0, 3).reshape(g, q * c, 4, n)
    w_out = jnp.stack([t['cp_re'][0, 1:], t['cp_im'][0, 1:],
                       t['cp_re'][1, 1:][::-1], t['cp_im'][1, 1:][::-1]], 0)
    w_out = w_out.transpose(2, 0, 4, 1, 3).reshape(g, 4, n, q * c)
    np_ = N_PAIRS
    w_st = w_st.astype(BF16).reshape(np_, 2, q * c, 4, n)
    ws_pair = jnp.concatenate([jnp.pad(w_st[:, 0], ((0, 0), (0, 0), (0, 0), (0, n))),
                               jnp.pad(w_st[:, 1], ((0, 0), (0, 0), (0, 0), (n, 0)))], 1)
    ws_pair = ws_pair.reshape(np_, PAIR_W, 4 * 2 * n)
    w_out = w_out.astype(BF16).reshape(np_, 2, 4, n, q * c)
    wo_pair = jnp.stack([jnp.pad(w_out[:, 0], ((0, 0), (0, 0), (0, 0), (0, q * c))),
                         jnp.pad(w_out[:, 1], ((0, 0), (0, 0), (0, 0), (q * c, 0)))], 2)
    wo_pair = wo_pair.reshape(np_, 4 * 2 * n, PAIR_W)
    tab = _chunk_power_table(t).reshape(2 * TAB_ROWS, np_, 2 * n).transpose(1, 0, 2)
    return m_intra.astype(BF16), ws_pair, wo_pair, tab


def _chunk_power_table(t):
    def cmul(x, y):
        return x[0] * y[0] - x[1] * y[1], x[0] * y[1] + x[1] * y[0]
    p1 = (t['p_re'][:, CHUNK], t['p_im'][:, CHUNK])
    p2 = cmul(p1, p1)
    p4 = cmul(p2, p2)
    p8 = cmul(p4, p4)
    pr = [(jnp.ones_like(p1[0]), jnp.zeros_like(p1[0]))]
    for _ in range(SUBLANES - 1):
        pr.append(cmul(pr[-1], p1))
    pr_re = jnp.stack([p[0] for p in pr], 0)
    pr_im = jnp.stack([p[1] for p in pr], 0)
    pw = jnp.stack([p1[0], p1[1], p2[0], p2[1], p4[0], p4[1], p8[0], p8[1]], 0)
    return jnp.concatenate([pr_re[:, 0], pr_im[:, 0], pw[:, 0],
                            pr_re[::-1, 1], pr_im[::-1, 1], pw[:, 1]], 0)


def _s5_scan_kernel(uc_ref, *rest, batch, emit_y):
    if emit_y:
        mi_ref, ws_ref, wo_ref, tab_ref, s0_ref, y_ref, fin_ref, s_scr, in_scr = rest
    else:
        ws_ref, tab_ref, s0_ref, fin_ref, s_scr = rest
    rows = uc_ref.shape[0]
    chunks = rows // batch
    n_tiles = chunks // SUBLANES
    u = uc_ref[...]
    s_scr[...] = _dot(u, ws_ref[...])
    row = lax.broadcasted_iota(jnp.int32, (SUBLANES, LANES), 0)

    def tile_scan(r0, backward, c_re, c_im):
        base = TAB_ROWS if backward else 0
        col = 2 * LANES if backward else 0
        rs = pl.ds(r0, SUBLANES)

        def shift(z, k):
            if backward:
                return jnp.where(row < SUBLANES - k, pltpu.roll(z, SUBLANES - k, 0), 0.0)
            return jnp.where(row >= k, pltpu.roll(z, k, 0), 0.0)

        z_re = s_scr[rs, col:col + LANES]
        z_im = s_scr[rs, col + LANES:col + 2 * LANES]
        for k, t in ((1, TAB_POW), (2, TAB_POW + 2), (4, TAB_POW + 4)):
            a_re = tab_ref[base + t:base + t + 1, :]
            a_im = tab_ref[base + t + 1:base + t + 2, :]
            sh_re = shift(z_re, k)
            sh_im = shift(z_im, k)
            z_re, z_im = z_re + (a_re * sh_re - a_im * sh_im), z_im + (a_re * sh_im + a_im * sh_re)
        pr_re = tab_ref[base:base + SUBLANES, :]
        pr_im = tab_ref[base + SUBLANES:base + 2 * SUBLANES, :]
        if emit_y:
            in_scr[rs, col:col + LANES] = pr_re * c_re - pr_im * c_im + shift(z_re, 1)
            in_scr[rs, col + LANES:col + 2 * LANES] = pr_re * c_im + pr_im * c_re + shift(z_im, 1)
        last = 0 if backward else SUBLANES - 1
        l_re = jnp.broadcast_to(z_re[last:last + 1, :], (SUBLANES, LANES))
        l_im = jnp.broadcast_to(z_im[last:last + 1, :], (SUBLANES, LANES))
        p8_re = tab_ref[base + TAB_POW + 6:base + TAB_POW + 7, :]
        p8_im = tab_ref[base + TAB_POW + 7:base + TAB_POW + 8, :]
        return p8_re * c_re - p8_im * c_im + l_re, p8_re * c_im + p8_im * c_re + l_im

    def body(m, carry):
        out = []
        for b in range(batch):
            cf_re, cf_im, cb_re, cb_im = carry[4 * b:4 * b + 4]
            rf = pl.multiple_of(b * chunks + m * SUBLANES, SUBLANES)
            rb = pl.multiple_of(b * chunks + (n_tiles - 1 - m) * SUBLANES, SUBLANES)
            out += list(tile_scan(rf, False, cf_re, cf_im))
            out += list(tile_scan(rb, True, cb_re, cb_im))
        return tuple(out)

    init = tuple(jnp.broadcast_to(s0_ref[t, b:b + 1, :], (SUBLANES, LANES))
                 for b in range(batch) for t in range(4))
    fin = lax.fori_loop(0, n_tiles, body, init, unroll=min(4, n_tiles))
    fin_ref[...] = jnp.zeros(fin_ref.shape, F32)
    for b in range(batch):
        for t in range(4):
            fin_ref[t, b:b + 1, :] = fin[4 * b + t][0:1, :]
    if emit_y:
        y_intra = jnp.concatenate(
            [_dot(u[:, gl * GROUP_W:(gl + 1) * GROUP_W], mi_ref[gl]) for gl in range(2)], axis=-1)
        y = y_intra + _dot(in_scr[...].astype(BF16), wo_ref[...])
        y_ref[...] = y.astype(y_ref.dtype)


def _s5_scan(uc, mi, ws, wo, tab, s0, batch, emit_y=True):
    rows = uc.shape[0]
    pair = lambda *shape: pl.BlockSpec((None,) + shape, lambda p: (p,) + (0,) * len(shape))
    uc_spec = pl.BlockSpec((rows, PAIR_W), lambda p: (0, p))
    state_spec = pair(4, SUBLANES, LANES)
    state_shape = jax.ShapeDtypeStruct((N_PAIRS, 4, SUBLANES, LANES), F32)
    scratch = [pltpu.VMEM((rows, PAIR_W), F32)]
    if emit_y:
        in_specs = [uc_spec, pl.BlockSpec((2, GROUP_W, GROUP_W), lambda p: (p, 0, 0)),
                    pair(PAIR_W, PAIR_W), pair(PAIR_W, PAIR_W), pair(2 * TAB_ROWS, LANES), state_spec]
        args = (uc, mi, ws, wo, tab, s0)
        out_specs = [uc_spec, state_spec]
        out_shape = [jax.ShapeDtypeStruct((rows, N_PAIRS * PAIR_W), BF16), state_shape]
        scratch = scratch * 2
    else:
        in_specs = [uc_spec, pair(PAIR_W, PAIR_W), pair(2 * TAB_ROWS, LANES), state_spec]
        args = (uc, ws, tab, s0)
        out_specs = state_spec
        out_shape = state_shape
    return pl.pallas_call(
        functools.partial(_s5_scan_kernel, batch=batch, emit_y=emit_y),
        grid=(N_PAIRS,), in_specs=in_specs, out_specs=out_specs, out_shape=out_shape,
        scratch_shapes=scratch,
        compiler_params=pltpu.CompilerParams(
            dimension_semantics=("arbitrary",), vmem_limit_bytes=VMEM_LIMIT),
        name="s5_scan" if emit_y else "s5_scan_ctx",
    )(*args)


def _mix_out_kernel(y_ref, siga_ref, sb_ref, x_ref, remb_ref, cemb_ref, mod_ref, wv_ref, wg_ref, wo_ref,
                    ln1_ref, wr_ref, br_ref, ut_ref, lt_ref,
                    x1_ref, xs_ref, route_ref, len_ref, y_scr):
    _from_chunk_tile(y_ref, y_scr)
    y = jnp.concatenate([y_scr[v] for v in range(S5_WIDTH // LANES)], axis=-1)
    half_y = 0.5 * y
    ya = (half_y + half_y * jnp.tanh(y * (GELU_K + (GELU_K * GELU_A) * (y * y)))).astype(BF16)
    out_a = _dot(ya, wv_ref[...]) * _sigmoid(_dot(ya, wg_ref[...]))
    merged = siga_ref[...].astype(F32) * out_a + sb_ref[...].astype(F32)
    mix = _dot(merged.astype(BF16), wo_ref[...])
    xp = _with_positions(x_ref, remb_ref, cemb_ref)
    x1 = _ln(ALPHA * xp + mod_ref[0:1, :] * mix) * ln1_ref[0:1, :] + ln1_ref[1:2, :]
    x1_ref[...] = x1
    h_hi = (_ln(x1) * mod_ref[2:3, :] + mod_ref[1:2, :]).astype(BF16)
    _route_and_sort(h_hi, wr_ref, br_ref, ut_ref, lt_ref, xs_ref, route_ref, len_ref)


def _route_and_sort(h_hi, wr_ref, br_ref, ut_ref, lt_ref, xs_ref, route_ref, len_ref):
    tm = h_hi.shape[0]
    f32 = F32
    nt = (((1,), (1,)), ((), ()))
    lg = lax.dot_general(wr_ref[...], h_hi, nt, preferred_element_type=f32) + br_ref[...]
    rowi = lax.broadcasted_iota(jnp.int32, (LOGIT_ROWS, tm), 0).astype(f32)
    neg = jnp.float32(-jnp.inf)
    big = jnp.float32(LOGIT_ROWS)
    gl = jnp.where(rowi < N_EXPERT_GROUPS, lg, neg)
    gmax = jnp.max(gl, axis=0, keepdims=True)
    g_idx = jnp.min(jnp.where(gl == gmax, rowi, big), axis=0, keepdims=True)
    p_group = 1.0 / jnp.sum(jnp.exp(gl - gmax), axis=0, keepdims=True)
    e_lo = N_EXPERT_GROUPS + g_idx * EXPERTS_PER_GROUP
    el = jnp.where((rowi >= e_lo) & (rowi < e_lo + EXPERTS_PER_GROUP), lg, neg)
    m1 = jnp.max(el, axis=0, keepdims=True)
    i1 = jnp.min(jnp.where(el == m1, rowi, big), axis=0, keepdims=True)
    el2 = jnp.where(rowi == i1, neg, el)
    m2 = jnp.max(el2, axis=0, keepdims=True)
    i2 = jnp.min(jnp.where(el2 == m2, rowi, big), axis=0, keepdims=True)
    r = jnp.exp(m2 - m1)
    w1 = p_group / (1.0 + r)
    w2 = p_group * r / (1.0 + r)
    e1 = i1 - N_EXPERT_GROUPS
    e2 = i2 - N_EXPERT_GROUPS
    rowa = lax.broadcasted_iota(jnp.int32, (ASSIGN_ROWS, tm), 0).astype(f32)
    a12 = jnp.where(rowa == e1, 1.0, 0.0) + jnp.where(rowa == e2 + N_EXPERTS, 1.0, 0.0)
    rank = _dot(a12.astype(BF16), ut_ref[...])
    cnt = jnp.broadcast_to(jnp.sum(a12, axis=1, keepdims=True), (ASSIGN_ROWS, LANES))
    row = lax.broadcasted_iota(jnp.int32, (ASSIGN_ROWS, LANES), 0)
    tot = cnt + pltpu.roll(cnt, N_EXPERTS, 0)
    run = jnp.where(row < N_EXPERTS, jnp.floor((tot + (PIECE - 1)) * (1.0 / PIECE)), 0.0)
    off = PIECE * _dot(lt_ref[...], run.astype(BF16))
    base = jnp.where(row < N_EXPERTS, off, pltpu.roll(off + cnt, N_EXPERTS, 0))
    posmat = a12 * (rank + base[:, 0:1])
    pos1 = jnp.sum(posmat[0:N_EXPERTS, :], axis=0, keepdims=True)
    pos2 = jnp.sum(posmat[N_EXPERTS:2 * N_EXPERTS, :], axis=0, keepdims=True)
    ri = lax.broadcasted_iota(jnp.int32, (xs_ref.shape[0], tm), 0)
    perm = jnp.where((ri == pos1.astype(jnp.int32)) | (ri == pos2.astype(jnp.int32)), 1.0, 0.0).astype(BF16)
    xs_ref[...] = _dot(perm, h_hi).astype(xs_ref.dtype)
    rowr = lax.broadcasted_iota(jnp.int32, (SUBLANES, tm), 0)
    rec = jnp.zeros((SUBLANES, tm), f32)
    for col, val in ((REC_W1, w1), (REC_W2, w2), (REC_POS1, pos1), (REC_POS2, pos2)):
        rec = jnp.where(rowr == col, val, rec)
    rec = jnp.concatenate([rec, jnp.zeros((ROUTE_LANES - SUBLANES, tm), f32)], axis=0)
    route_ref[...] = rec.T
    len_ref[...] = run


def _local_rows(tm):
    return -(-(TOP_K * tm + N_EXPERTS * (PIECE - 1)) // PIECE) * PIECE


def _mix_out(y, siga, sb, x, r_emb, c_emb, mod, wv, wg, wo, ln1, wr, br, tm):
    b, l, d = x.shape
    n_tiles = l // tm
    xs_rows = _local_rows(tm)
    tok = lambda w: pl.BlockSpec((None, tm, w), lambda i, j: (j, i, 0))
    whole = lambda a: pl.BlockSpec(a.shape, lambda i, j: (0,) * a.ndim)
    ut = (jnp.arange(tm)[:, None] < jnp.arange(tm)[None, :]).astype(BF16)
    lt = (jnp.arange(ASSIGN_ROWS)[None, :] < jnp.arange(ASSIGN_ROWS)[:, None]).astype(BF16)
    return pl.pallas_call(
        _mix_out_kernel,
        grid=(n_tiles, b),
        in_specs=[pl.BlockSpec((tm // CHUNK, CHUNK * S5_WIDTH), lambda i, j: (j * n_tiles + i, 0)),
                  tok(d), tok(d), tok(d)] + _pos_specs(tm, d) + [
                  pl.BlockSpec((None, 4, d), lambda i, j: (j, 0, 0)),
                  whole(wv), whole(wg), whole(wo), whole(ln1), whole(wr), whole(br),
                  whole(ut), whole(lt)],
        out_specs=[tok(d),
                   pl.BlockSpec((xs_rows, d), lambda i, j: (j * n_tiles + i, 0)),
                   tok(ROUTE_LANES),
                   pl.BlockSpec((None, ASSIGN_ROWS, LANES), lambda i, j: (j * n_tiles + i, 0, 0))],
        out_shape=[jax.ShapeDtypeStruct((b, l, d), F32),
                   jax.ShapeDtypeStruct((b * n_tiles * xs_rows, d), BF16),
                   jax.ShapeDtypeStruct((b, l, ROUTE_LANES), F32),
                   jax.ShapeDtypeStruct((b * n_tiles, ASSIGN_ROWS, LANES), F32)],
        scratch_shapes=[pltpu.VMEM((S5_WIDTH // LANES, tm, LANES), F32)],
        compiler_params=pltpu.CompilerParams(
            dimension_semantics=("arbitrary", "arbitrary"), vmem_limit_bytes=VMEM_LIMIT),
        name="mix_out",
    )(y, siga, sb, x, r_emb, c_emb, mod, wv, wg, wo, ln1, wr, br, ut, lt)


def _piece_copy(src_hbm, src_row, dst, piece, sem):
    return pltpu.make_async_copy(src_hbm.at[pl.ds(pl.multiple_of(src_row, PIECE), PIECE), :],
                                 dst.at[pl.ds(pl.multiple_of(piece * PIECE, PIECE), PIECE), :], sem)


def _issue_pieces(src_hbm, table_ref, first, n_pieces, dst, sem):
    for p in range(n_pieces):
        _piece_copy(src_hbm, table_ref[first + p], dst, p, sem).start(priority=p % 2)


def _wait_pieces(src_hbm, dst, sem):
    pltpu.make_async_copy(src_hbm.at[pl.ds(0, dst.shape[0]), :], dst, sem).wait()


def _experts_kernel(be_ref, piece_ref, nused_ref, xs_hbm, wg_ref, wu_ref, wd_ref, ys_ref,
                    xs_buf0, xs_buf1, wg_bf, wu_bf, wd_bf, sem):
    i = pl.program_id(0)
    n_used = nused_ref[0]
    per_block = ROW_BLOCK // PIECE

    @pl.when(i == 0)
    def _():
        _issue_pieces(xs_hbm, piece_ref, 0, per_block, xs_buf0, sem.at[0])

    def block(cur, cur_sem, oth, oth_sem):
        @pl.when((i == 0) | (be_ref[i] != be_ref[jnp.maximum(i - 1, 0)]))
        def _():
            wg_bf[...] = wg_ref[...].astype(BF16)
            wu_bf[...] = wu_ref[...].astype(BF16)
            wd_bf[...] = wd_ref[...].astype(BF16)

        _wait_pieces(xs_hbm, cur, cur_sem)
        nxt = jnp.minimum(i + 1, n_used - 1)
        _issue_pieces(xs_hbm, piece_ref, nxt * per_block, per_block, oth, oth_sem)

        xb = cur[...]
        gate = _dot(xb, wg_bf[...])
        up = _dot(xb, wu_bf[...])
        hid = (gate * _sigmoid(gate) * up).astype(BF16)
        ys_ref[...] = _dot(hid, wd_bf[...]).astype(ys_ref.dtype)

        @pl.when(i == n_used - 1)
        def _():
            _wait_pieces(xs_hbm, oth, oth_sem)

    for s, (cur, oth) in enumerate(((xs_buf0, xs_buf1), (xs_buf1, xs_buf0))):
        @pl.when((i < n_used) & (i % 2 == s))
        def _(s=s, cur=cur, oth=oth):
            block(cur, sem.at[s], oth, sem.at[1 - s])

    @pl.when(i >= n_used)
    def _():
        ys_ref[...] = jnp.zeros(ys_ref.shape, ys_ref.dtype)


def _experts(block_e, piece_src, n_used, xs, wg, wu, wd, n_blocks):
    d = xs.shape[1]
    by_expert = lambda i, be, ps, nu: (0, be[i], 0, 0)
    grid_spec = pltpu.PrefetchScalarGridSpec(
        num_scalar_prefetch=3,
        grid=(n_blocks,),
        in_specs=[pl.BlockSpec(memory_space=pl.ANY),
                  pl.BlockSpec((None, None, d, EXPERT_FF), by_expert),
                  pl.BlockSpec((None, None, d, EXPERT_FF), by_expert),
                  pl.BlockSpec((None, None, EXPERT_FF, d), by_expert)],
        out_specs=pl.BlockSpec((ROW_BLOCK, d), lambda i, be, ps, nu: (i, 0)),
        scratch_shapes=[pltpu.VMEM((ROW_BLOCK, d), BF16), pltpu.VMEM((ROW_BLOCK, d), BF16),
                        pltpu.VMEM((d, EXPERT_FF), BF16), pltpu.VMEM((d, EXPERT_FF), BF16),
                        pltpu.VMEM((EXPERT_FF, d), BF16), pltpu.SemaphoreType.DMA((2,))],
    )
    return pl.pallas_call(
        _experts_kernel,
        grid_spec=grid_spec,
        out_shape=jax.ShapeDtypeStruct((n_blocks * ROW_BLOCK, d), BF16),
        compiler_params=pltpu.CompilerParams(
            dimension_semantics=("arbitrary",), vmem_limit_bytes=VMEM_LIMIT),
        name="experts",
    )(block_e, piece_src, n_used, xs, wg, wu, wd)


def _combine_kernel(piece_ref, ys_hbm, x1_ref, route_ref, mod_ref, ln2_ref, o_ref, buf, sem):
    i = pl.program_id(0)
    n = pl.num_programs(0)
    slot = i % 2
    rows = buf.shape[1]
    per_tile = rows // PIECE

    def issue(tile, s):
        def body(h, _):
            for prio in range(2):
                p = 2 * h + prio
                _piece_copy(ys_hbm, piece_ref[tile * per_tile + p], buf.at[s], p, sem.at[s]).start(priority=prio)
            return 0
        lax.fori_loop(0, per_tile // 2, body, 0)

    @pl.when(i == 0)
    def _():
        issue(0, 0)

    _wait_pieces(ys_hbm, buf.at[slot], sem.at[slot])

    @pl.when(i + 1 < n)
    def _():
        issue(i + 1, 1 - slot)

    tm = x1_ref.shape[0]
    lane = lax.broadcasted_iota(jnp.int32, (tm, rows), 1)
    rec = lambda col: route_ref[:, col:col + 1]
    sel = jnp.where(lane == rec(REC_POS1).astype(jnp.int32), rec(REC_W1),
                    jnp.where(lane == rec(REC_POS2).astype(jnp.int32), rec(REC_W2), 0.0)).astype(BF16)
    moe = _dot(sel, buf[slot])
    z = ALPHA * x1_ref[...] + mod_ref[0:1, :] * moe
    o_ref[...] = _ln(z) * ln2_ref[0:1, :] + ln2_ref[1:2, :]


def _combine(piece_glob, ys, x1, route, mod, ln2, tm, tiles_per_batch):
    t, d = x1.shape
    assert (_local_rows(tm) // PIECE) % 2 == 0
    grid_spec = pltpu.PrefetchScalarGridSpec(
        num_scalar_prefetch=1,
        grid=(t // tm,),
        in_specs=[pl.BlockSpec(memory_space=pl.ANY),
                  pl.BlockSpec((tm, d), lambda i, pg: (i, 0)),
                  pl.BlockSpec((tm, ROUTE_LANES), lambda i, pg: (i, 0)),
                  pl.BlockSpec((None, SUBLANES, d), lambda i, pg: (i // tiles_per_batch, 0, 0)),
                  pl.BlockSpec((2, d), lambda i, pg: (0, 0))],
        out_specs=pl.BlockSpec((tm, d), lambda i, pg: (i, 0)),
        scratch_shapes=[pltpu.VMEM((2, _local_rows(tm), d), BF16), pltpu.SemaphoreType.DMA((2,))],
    )
    return pl.pallas_call(
        _combine_kernel,
        grid_spec=grid_spec,
        out_shape=jax.ShapeDtypeStruct((t, d), F32),
        compiler_params=pltpu.CompilerParams(
            dimension_semantics=("arbitrary",), vmem_limit_bytes=VMEM_LIMIT),
        name="combine",
    )(piece_glob, ys, x1, route, mod, ln2)


def _sincos_2d(rows, cols, dim):
    q = dim // 4
    omega = 1.0 / (POS_BASE ** (jnp.arange(q, dtype=F32) / q))
    r = jnp.arange(rows, dtype=F32)[:, None] * omega
    cl = jnp.arange(cols, dtype=F32)[:, None] * omega
    r_emb = jnp.concatenate([jnp.sin(r), jnp.cos(r)], -1)
    c_emb = jnp.concatenate([jnp.sin(cl), jnp.cos(cl)], -1)
    return r_emb, c_emb


def _routing_tables(run_pieces, xs_rows, n_blocks):
    i32 = jnp.int32
    n_tiles = run_pieces.shape[0]
    ppb = ROW_BLOCK // PIECE
    loc_start = jnp.cumsum(run_pieces, axis=1) - run_pieces
    seg_tot = jnp.sum(run_pieces, axis=0)
    seg_pad = (seg_tot + ppb - 1) // ppb * ppb
    seg_end = jnp.cumsum(seg_pad)
    seg_start = seg_end - seg_pad
    run_t = run_pieces.T
    glob_start = seg_start[:, None] + jnp.cumsum(run_t, axis=1) - run_t
    n_used = (seg_end[-1] // ppb).astype(i32)
    blk = jnp.minimum(jnp.arange(n_blocks, dtype=i32), n_used - 1)
    block_e = jnp.minimum(jnp.sum((seg_end[None, :] <= (blk * ppb)[:, None]).astype(i32), axis=1),
                          N_EXPERTS - 1).astype(i32)
    lpt = xs_rows // PIECE
    src0 = jnp.arange(n_tiles, dtype=i32)[None, :] * lpt + loc_start.T
    is_e = block_e[:, None] == jnp.arange(N_EXPERTS, dtype=i32)[None, :]
    of_block = lambda tbl: jnp.sum(jnp.where(is_e[:, :, None], tbl[None], 0), axis=1)[:, None, :]
    p = jnp.arange(n_blocks * ppb, dtype=i32).reshape(n_blocks, ppb, 1)
    within = p - of_block(glob_start)
    hit = (within >= 0) & (within < of_block(run_t))
    piece_src = jnp.sum(jnp.where(hit, (of_block(src0) + within) * PIECE, 0), axis=2).astype(i32).reshape(-1)
    s = jnp.arange(lpt, dtype=i32)
    loc_within = s[None, :, None] - loc_start[:, None, :]
    hit = (loc_within >= 0) & (loc_within < run_pieces[:, None, :])
    piece_glob = jnp.sum(jnp.where(hit, (glob_start.T[:, None, :] + loc_within) * PIECE, 0), axis=2)
    return block_e, piece_src, piece_glob.astype(i32).reshape(-1), n_used.reshape(1)


def kernel(x, c, ctx, c_ctx, w_ada, b_ada, w_in, s5_log_dt_f, s5_a_re_f, s5_a_im_f, s5_b_re_f, s5_b_im_f, s5_c_re_f, s5_c_im_f, s5_log_dt_b, s5_a_re_b, s5_a_im_b, s5_b_re_b, s5_b_im_b, s5_c_re_b, s5_c_im_b, s5_d, s5_w_glu_val, s5_w_glu_gate, conv_w, conv_w_out, w_o, ln1_g, ln1_b, router_w_group, router_b_group, router_w_expert, router_b_expert, exp_w_gate, exp_w_up, exp_w_down, ln2_g, ln2_b):
    b, l, d = x.shape
    lc = ctx.shape[1]
    assert d == D_MODEL and b < SUBLANES and w_ada.shape[0] == DEPTH
    assert l % (SUBLANES * CHUNK) == 0 and lc % (SUBLANES * CHUNK) == 0 and l % GRID_W == 0
    t = b * l
    tm = min(TOKEN_TILE, l)
    tmc = min(TOKEN_TILE, lc)

    cc = jnp.concatenate([c, c_ctx[None, :], jnp.zeros((SUBLANES - b - 1, d), F32)], 0)
    mods = _mods(cc, w_ada[0], b_ada[0])
    sh1, sc1, g1, sh2, sc2, g2 = jnp.split(mods, 6, axis=-1)
    mod_a = jnp.stack([sh1[:b], 1.0 + sc1[:b]], 1)
    mod_ctx = jnp.broadcast_to(jnp.stack([sh1[b], 1.0 + sc1[b]], 0)[None], (b, 2, d))
    mod_c = jnp.stack([g1[:b], sh2[:b], 1.0 + sc2[:b], jnp.zeros((b, d), F32)], 1)
    mod_f = jnp.concatenate([g2[:b, None, :], jnp.zeros((b, SUBLANES - 1, d), F32)], 1)

    w_in_bf = w_in[0].astype(BF16)
    both = lambda fwd, bwd: jnp.concatenate([fwd, bwd], 0)
    s5_tab = _s5_tables(both(s5_log_dt_f, s5_log_dt_b), both(s5_a_re_f, s5_a_re_b), both(s5_a_im_f, s5_a_im_b),
                        both(s5_b_re_f, s5_b_re_b), both(s5_b_im_f, s5_b_im_b),
                        both(s5_c_re_f, s5_c_re_b), both(s5_c_im_f, s5_c_im_b))
    mi, ws, wo_s5, tab = _s5_operators(s5_tab, s5_d[0])

    (uc_ctx,) = _in_proj(ctx, jnp.zeros((lc // GRID_W, d // 2), F32), jnp.zeros((GRID_W, d // 2), F32),
                         mod_ctx, w_in_bf, None, None, tmc, False)
    zero_state = jnp.zeros((N_PAIRS, 4, SUBLANES, LANES), F32)
    s0 = _s5_scan(uc_ctx, None, ws, None, tab, zero_state, b, emit_y=False)

    r_emb, c_emb = _sincos_2d(l // GRID_W, GRID_W, d)
    uc, siga, sb = _in_proj(x, r_emb, c_emb, mod_a, w_in_bf, conv_w[0], conv_w_out[0].astype(BF16),
                            min(2 * TOKEN_TILE, l), True)
    y, _ = _s5_scan(uc, mi, ws, wo_s5, tab, s0, b)

    wr = jnp.concatenate([router_w_group[0], router_w_expert[0],
                          jnp.zeros((d, LOGIT_ROWS - N_EXPERT_GROUPS - N_EXPERTS), F32)], 1).T.astype(BF16)
    br = jnp.concatenate([router_b_group[0], router_b_expert[0],
                          jnp.zeros((LOGIT_ROWS - N_EXPERT_GROUPS - N_EXPERTS,), F32)])[:, None]
    ln1 = jnp.stack([ln1_g[0], ln1_b[0]], 0)
    x1, xs, route, run_len = _mix_out(y, siga, sb, x, r_emb, c_emb, mod_c,
                                      s5_w_glu_val[0].astype(BF16), s5_w_glu_gate[0].astype(BF16),
                                      w_o[0].astype(BF16), ln1, wr, br, tm)

    x1 = x1.reshape(t, d)
    route = route.reshape(t, ROUTE_LANES)
    n_tiles = t // tm
    xs_rows = _local_rows(tm)
    run_pieces = run_len[:, :N_EXPERTS, 0].astype(jnp.int32)
    max_rows = t * TOP_K + n_tiles * N_EXPERTS * (PIECE - 1) + N_EXPERTS * (ROW_BLOCK - 1)
    n_blocks = -(-max_rows // ROW_BLOCK)
    block_e, piece_src, piece_glob, n_used = _routing_tables(run_pieces, xs_rows, n_blocks)
    ys = _experts(block_e, piece_src, n_used, xs, exp_w_gate, exp_w_up, exp_w_down, n_blocks)
    ln2 = jnp.stack([ln2_g[0], ln2_b[0]], 0)
    out = _combine(piece_glob, ys, x1, route, mod_f, ln2, tm, l // tm)
    return out.reshape(b, l, d)
```
